```python
import jax, jax.numpy as jnp
from jax import lax
import numpy as np

D_MODEL = 1024
BATCH = 4
SEQ = 4096
DEPTH = 1

HEAD_DIM = 64
ROPE_THETA = 10000.0
RMS_EPS = 1e-6
NEG_INF = -1e30
DIL_PAIRS = ((128, 1), (512, 4), (2048, 16))
DIL_HEADS = 4
DIL_TOTAL_HEADS = len(DIL_PAIRS) * DIL_HEADS
DIL_W = DIL_TOTAL_HEADS * HEAD_DIM
DIL_BLOCK = 64
SWA_Q_HEADS = 8
SWA_KV_HEADS = 2
SWA_WINDOW = 128
SWA_BLOCK = 128
SWA_QW = SWA_Q_HEADS * HEAD_DIM
SWA_KVW = SWA_KV_HEADS * HEAD_DIM
N_BRANCHES = 2
IN_SPLITS = tuple(int(v) for v in np.cumsum([DIL_W, DIL_W, DIL_W, SWA_QW, SWA_KVW, SWA_KVW, D_MODEL]))
IN_WIDTH = 3 * DIL_W + SWA_QW + 2 * SWA_KVW + N_BRANCHES * D_MODEL
MOE_GROUPS = 4
EXPERTS_PER_GROUP = 8
N_EXPERTS = MOE_GROUPS * EXPERTS_PER_GROUP
TOP_K = 2
D_EXPERT = 512
MOE_BLOCK = 128
N_MOD = 6

kernel_name = 'hybrid_dilated_swa_hmoe_adaln_encoder'


def rms_norm(x, g):
    x32 = x.astype(jnp.float32)
    y = x32 * lax.rsqrt(jnp.mean(x32 * x32, axis=-1, keepdims=True) + RMS_EPS)
    return y.astype(x.dtype) * g


def rope(t, positions):
    half = HEAD_DIM // 2
    inv_freq = ROPE_THETA ** (-jnp.arange(half, dtype=jnp.float32) * (2.0 / HEAD_DIM))
    ang = positions.astype(jnp.float32)[..., None] * inv_freq
    cos = jnp.cos(ang)[:, :, None, :]
    sin = jnp.sin(ang)[:, :, None, :]
    t32 = t.astype(jnp.float32)
    t1, t2 = t32[..., :half], t32[..., half:]
    return jnp.concatenate([t1 * cos - t2 * sin, t2 * cos + t1 * sin], axis=-1).astype(t.dtype)


def band_attention(q, k, v, window, block, sink=None):
    bt, length, hq, dh = q.shape
    hkv = k.shape[2]
    grp = hq // hkv
    nb = -(-length // block)
    lp = nb * block
    ks = block + 2 * window
    q = jnp.pad(q, ((0, 0), (0, lp - length), (0, 0), (0, 0)))
    kv_pad = ((0, 0), (window, lp - length + window), (0, 0), (0, 0))
    k = jnp.pad(k, kv_pad)
    v = jnp.pad(v, kv_pad)
    slab = jnp.arange(nb)[:, None] * block + jnp.arange(ks)[None, :]
    kb = k[:, slab]
    vb = v[:, slab]
    qb = q.reshape(bt, nb, block, hkv, grp, dh)
    s = jnp.einsum('bnqhgd,bnkhd->bnhgqk', qb, kb, preferred_element_type=jnp.float32) * (dh ** -0.5)
    rel = jnp.arange(ks)[None, :] - window - jnp.arange(block)[:, None]
    kpos = slab - window
    mask = (jnp.abs(rel) <= window)[None] & ((kpos >= 0) & (kpos < length))[:, None, :]
    s = jnp.where(mask[None, :, None, None], s, NEG_INF)
    m = jnp.max(s, axis=-1)
    if sink is not None:
        sk = sink.astype(jnp.float32).reshape(hkv, grp)[None, None, :, :, None]
        m = jnp.maximum(m, sk)
    e = jnp.exp(s - m[..., None])
    denom = jnp.sum(e, axis=-1)
    if sink is not None:
        denom = denom + jnp.exp(sk - m)
    o = jnp.einsum('bnhgqk,bnkhd->bnqhgd', e, vb.astype(jnp.float32))
    o = o / jnp.moveaxis(denom, -1, 2)[..., None]
    o = o.reshape(bt, lp, hq, dh)[:, :length]
    lse = jnp.moveaxis(m + jnp.log(denom), -1, 2).reshape(bt, lp, hq)[:, :length]
    return o.astype(q.dtype), lse


def to_strided(t, d):
    b, s = t.shape[0], t.shape[1]
    t = t.reshape((b, s // d, d) + t.shape[2:])
    t = jnp.swapaxes(t, 1, 2)
    return t.reshape((b * d, s // d) + t.shape[3:])


def from_strided(t, d, batch):
    length = t.shape[1]
    t = t.reshape((batch, d) + t.shape[1:])
    t = jnp.swapaxes(t, 1, 2)
    return t.reshape((batch, length * d) + t.shape[3:])


def dilated_attention(q, k, v):
    batch = q.shape[0]
    outs, lses = [], []
    for g, (w, d) in enumerate(DIL_PAIRS):
        sl = slice(g * DIL_HEADS, (g + 1) * DIL_HEADS)
        o, l = band_attention(to_strided(q[:, :, sl], d), to_strided(k[:, :, sl], d),
                              to_strided(v[:, :, sl], d), w // (2 * d), DIL_BLOCK)
        outs.append(from_strided(o, d, batch))
        lses.append(from_strided(l, d, batch))
    alpha = jax.nn.softmax(jnp.stack(lses), axis=0)
    o = jnp.stack(outs).astype(jnp.float32)
    return jnp.einsum('gbsh,gbshd->bshd', alpha, o).astype(q.dtype)


def token_mixer(h, positions, w_in, sink_logits, w_branch_a, w_branch_b, w_out):
    b, s, _ = h.shape
    p = h @ w_in
    qa, ka, va, qb, kb, vb, ga, gb = jnp.split(p, IN_SPLITS, axis=-1)
    heads = lambda t, n: t.reshape(b, s, n, HEAD_DIM)
    o_a = dilated_attention(rope(heads(qa, DIL_TOTAL_HEADS), positions),
                            rope(heads(ka, DIL_TOTAL_HEADS), positions),
                            heads(va, DIL_TOTAL_HEADS))
    o_b, _ = band_attention(rope(heads(qb, SWA_Q_HEADS), positions),
                            rope(heads(kb, SWA_KV_HEADS), positions),
                            heads(vb, SWA_KV_HEADS), SWA_WINDOW, SWA_BLOCK, sink_logits)
    y_a = o_a.reshape(b, s, -1) @ w_branch_a
    y_b = o_b.reshape(b, s, -1) @ w_branch_b
    merged = jax.nn.sigmoid(ga) * y_a + jax.nn.sigmoid(gb) * y_b
    return merged @ w_out


def hier_moe(h, w_group, b_group, w_route, b_route, w_gate, w_up, w_down):
    b, s, d = h.shape
    n = b * s
    t = h.reshape(n, d)
    glog = (t @ w_group).astype(jnp.float32) + b_group
    gprob = jax.nn.softmax(glog, axis=-1)
    gsel = jnp.argmax(glog, axis=-1)
    gw = jnp.take_along_axis(gprob, gsel[:, None], axis=1)[:, 0]
    elog = ((t @ w_route).astype(jnp.float32) + b_route).reshape(n, MOE_GROUPS, EXPERTS_PER_GROUP)
    elog = jnp.take_along_axis(elog, gsel[:, None, None], axis=1)[:, 0]
    topw, topi = lax.top_k(jax.nn.softmax(elog, axis=-1), TOP_K)
    topw = topw / jnp.sum(topw, axis=-1, keepdims=True) * gw[:, None]
    expert = gsel[:, None] * EXPERTS_PER_GROUP + topi
    a = n * TOP_K
    flat_e = expert.reshape(a).astype(jnp.int32)
    flat_tok = jnp.repeat(jnp.arange(n, dtype=jnp.int32), TOP_K)
    order = jnp.argsort(flat_e, stable=True)
    se = flat_e[order]
    counts = jnp.zeros((N_EXPERTS,), jnp.int32).at[flat_e].add(1)
    starts = jnp.cumsum(counts) - counts
    padded = (counts + MOE_BLOCK - 1) // MOE_BLOCK * MOE_BLOCK
    pend = jnp.cumsum(padded)
    pstarts = pend - padded
    dest_sorted = pstarts[se] + (jnp.arange(a, dtype=jnp.int32) - starts[se])
    dest = jnp.zeros((a,), jnp.int32).at[order].set(dest_sorted)
    rows = (-(-a // MOE_BLOCK) + N_EXPERTS) * MOE_BLOCK
    nblk = rows // MOE_BLOCK
    row_tok = jnp.full((rows,), n, jnp.int32).at[dest].set(flat_tok)
    t_pad = jnp.concatenate([t, jnp.zeros((1, d), t.dtype)], axis=0)
    x_rows = t_pad[row_tok].reshape(nblk, MOE_BLOCK, d)
    blk_e = jnp.clip(jnp.searchsorted(pend, jnp.arange(nblk) * MOE_BLOCK, side='right'), 0, N_EXPERTS - 1)

    def expert_block(args):
        xb, e = args
        return (jax.nn.silu(xb @ w_gate[e]) * (xb @ w_up[e])) @ w_down[e]

    y_rows = lax.map(expert_block, (x_rows, blk_e)).reshape(rows, d)
    picked = y_rows[dest].reshape(n, TOP_K, d)
    out = jnp.einsum('nk,nkd->nd', topw.astype(h.dtype), picked)
    return out.reshape(b, s, d)


def setup_inputs(seed: int = 0) -> dict:
    key = jax.random.key(seed)
    ks = jax.random.split(key, 20)
    f32 = jnp.float32
    nrm = lambda k, shape, fan: jax.random.normal(k, shape, f32) * (fan ** -0.5)
    L, D = DEPTH, D_MODEL
    return {
        'x': jax.random.normal(ks[0], (BATCH, SEQ, D), f32),
        'c': jax.random.normal(ks[1], (BATCH, D), f32),
        'positions': jnp.broadcast_to(jnp.arange(SEQ, dtype=jnp.int32), (BATCH, SEQ)),
        'w_ada': nrm(ks[2], (L, D, N_MOD * D), D),
        'b_ada': 0.02 * jax.random.normal(ks[3], (L, N_MOD * D), f32),
        'g_mix': 1.0 + 0.05 * jax.random.normal(ks[4], (L, D), f32),
        'w_in': nrm(ks[5], (L, D, IN_WIDTH), D),
        'sink_logits': jax.random.normal(ks[6], (L, SWA_Q_HEADS), f32),
        'w_branch_a': nrm(ks[7], (L, DIL_HEADS * HEAD_DIM, D), DIL_HEADS * HEAD_DIM),
        'w_branch_b': nrm(ks[8], (L, SWA_QW, D), SWA_QW),
        'w_out': nrm(ks[9], (L, D, D), D),
        'g_ffn': 1.0 + 0.05 * jax.random.normal(ks[10], (L, D), f32),
        'w_group': nrm(ks[11], (L, D, MOE_GROUPS), D),
        'b_group': 0.01 * jax.random.normal(ks[12], (L, MOE_GROUPS), f32),
        'w_route': nrm(ks[13], (L, D, N_EXPERTS), D),
        'b_route': 0.01 * jax.random.normal(ks[14], (L, N_EXPERTS), f32),
        'w_expert_gate': nrm(ks[15], (L, N_EXPERTS, D, D_EXPERT), D),
        'w_expert_up': nrm(ks[16], (L, N_EXPERTS, D, D_EXPERT), D),
        'w_expert_down': nrm(ks[17], (L, N_EXPERTS, D_EXPERT, D), D_EXPERT),
        'g_final': 1.0 + 0.05 * jax.random.normal(ks[18], (D,), f32),
    }


def reference(x, c, positions, w_ada, b_ada, g_mix, w_in, sink_logits, w_branch_a, w_branch_b,
              w_out, g_ffn, w_group, b_group, w_route, b_route, w_expert_gate, w_expert_up,
              w_expert_down, g_final):
    cs = jax.nn.silu(c)
    for l in range(DEPTH):
        mod = cs @ w_ada[l] + b_ada[l]
        sh1, sc1, gt1, sh2, sc2, gt2 = jnp.split(mod, N_MOD, axis=-1)
        h = rms_norm(x, g_mix[l]) * (1.0 + sc1[:, None]) + sh1[:, None]
        x = x + gt1[:, None] * token_mixer(h, positions, w_in[l], sink_logits[l],
                                           w_branch_a[l], w_branch_b[l], w_out[l])
        h = rms_norm(x, g_ffn[l]) * (1.0 + sc2[:, None]) + sh2[:, None]
        x = x + gt2[:, None] * hier_moe(h, w_group[l], b_group[l], w_route[l], b_route[l],
                                        w_expert_gate[l], w_expert_up[l], w_expert_down[l])
    return rms_norm(x, g_final)
```

```python
import functools

import jax
import jax.numpy as jnp
from jax import lax
from jax.experimental import pallas as pl
from jax.experimental.pallas import tpu as pltpu

F32 = jnp.float32
BF16 = jnp.bfloat16
I32 = jnp.int32

HEAD_DIM = 64
ROPE_THETA = 10000.0
RMS_EPS = 1e-6
NEG_INF = -1e30
DILATIONS = (1, 4, 16)
DIL_HALF_WINDOW = 64
DIL_GROUP_W = 256
SWA_WINDOW = 128
N_GROUPS = 4
EXPERTS_PER_GROUP = 8
N_EXPERTS = 32
LANES = 128
ROW_BLOCK = 256
VMEM_LIMIT = 56 * 1024 * 1024


def _cparams(sem):
    return pltpu.CompilerParams(dimension_semantics=sem, vmem_limit_bytes=VMEM_LIMIT)


def _ada_kernel(c_ref, w_ref, b_ref, o_ref):
    c = c_ref[...]
    cs = c * jax.nn.sigmoid(c)
    o_ref[...] = jnp.dot(cs.astype(BF16), w_ref[...].astype(BF16), preferred_element_type=F32) + b_ref[...]


def _ada(c8, w_ada, b_ada):
    d, n = w_ada.shape
    tn = 1536
    return pl.pallas_call(
        _ada_kernel,
        grid=(n // tn,),
        in_specs=[pl.BlockSpec((8, d), lambda j: (0, 0)),
                  pl.BlockSpec((d, tn), lambda j: (0, j)),
                  pl.BlockSpec((1, tn), lambda j: (0, j))],
        out_specs=pl.BlockSpec((8, tn), lambda j: (0, j)),
        out_shape=jax.ShapeDtypeStruct((8, n), F32),
        compiler_params=_cparams(("arbitrary",)),
        name="ada",
    )(c8, w_ada, b_ada)


def _rms_mod(x, g, shift, scale):
    ms = jnp.mean(x * x, axis=-1, keepdims=True)
    return (x * lax.rsqrt(ms + RMS_EPS)) * g * (1.0 + scale) + shift


def _inproj_kernel(x_ref, mod_ref, g_ref, cos_ref, sin_ref, w_ref,
                   q0_ref, k0_ref, v0_ref, q1_ref, k1_ref, v1_ref, q2_ref, k2_ref, v2_ref,
                   qb_ref, kb_ref, vb_ref, sga_ref, sgb_ref, stg_ref, *, tm):
    h = _rms_mod(x_ref[...], g_ref[...], mod_ref[0, 0:1, :], mod_ref[0, 1:2, :])
    hb = h.astype(BF16)
    cos = cos_ref[...]
    sin = sin_ref[...]
    lane = lax.broadcasted_iota(I32, (tm, LANES), 1)
    first_half = (lane & 32) == 0
    low = lane < 64

    def proj(c0, width):
        return jnp.dot(hb, w_ref[:, c0:c0 + width], preferred_element_type=F32)

    def rope(t):
        rot = jnp.where(first_half, pltpu.roll(t, 96, 1), pltpu.roll(t, 32, 1))
        return t * cos + rot * sin

    def rope256(p):
        return jnp.concatenate([rope(p[:, :LANES]), rope(p[:, LANES:])], axis=1)

    def store_group(ref, val, d):
        if d == 1:
            ref[0, 0] = val.astype(BF16)
        else:
            for c in range(2):
                stg_ref[c] = val[:, c * LANES:(c + 1) * LANES]
            for r in range(d):
                for c in range(2):
                    ref[0, r, :, c * LANES:(c + 1) * LANES] = (
                        stg_ref[c, pl.ds(r, tm // d, stride=d), :].astype(BF16))

    q_refs = (q0_ref, q1_ref, q2_ref)
    k_refs = (k0_ref, k1_ref, k2_ref)
    v_refs = (v0_ref, v1_ref, v2_ref)
    for g, d in enumerate(DILATIONS):
        store_group(q_refs[g], rope256(proj(g * 256, 256)) * 0.125, d)
        store_group(k_refs[g], rope256(proj(768 + g * 256, 256)), d)
        store_group(v_refs[g], proj(1536 + g * 256, 256), d)
    for j in range(2):
        qb_ref[:, j * 256:(j + 1) * 256] = (rope256(proj(2304 + j * 256, 256)) * 0.125).astype(BF16)
    kv = proj(2816, 256)
    kb = rope(kv[:, :LANES])
    vb = kv[:, LANES:]
    kb_sw = pltpu.roll(kb, 64, 1)
    vb_sw = pltpu.roll(vb, 64, 1)
    kb_ref[:, :LANES] = jnp.where(low, kb, kb_sw).astype(BF16)
    kb_ref[:, LANES:] = jnp.where(low, kb_sw, kb).astype(BF16)
    vb_ref[:, :LANES] = jnp.where(low, vb, vb_sw).astype(BF16)
    vb_ref[:, LANES:] = jnp.where(low, vb_sw, vb).astype(BF16)
    for j in range(4):
        sga_ref[:, j * 256:(j + 1) * 256] = jax.nn.sigmoid(proj(3072 + j * 256, 256)).astype(BF16)
        sgb_ref[:, j * 256:(j + 1) * 256] = jax.nn.sigmoid(proj(4096 + j * 256, 256)).astype(BF16)


def _inproj(x2, mod3, g_mix, cos_t, sin_t, w_in_bf, batch, seq):
    n, dm = x2.shape
    tm = 512
    tpb = seq // tm
    grid = (n // tm,)
    row = lambda i: (i, 0)
    strided_specs, strided_shapes = [], []
    for d in DILATIONS:
        for _ in range(3):
            strided_specs.append(pl.BlockSpec((1, d, tm // d, DIL_GROUP_W), lambda i: (i // tpb, 0, i % tpb, 0)))
            strided_shapes.append(jax.ShapeDtypeStruct((batch, d, seq // d, DIL_GROUP_W), BF16))
    out_specs = strided_specs + [
        pl.BlockSpec((tm, 512), row), pl.BlockSpec((tm, 256), row), pl.BlockSpec((tm, 256), row),
        pl.BlockSpec((tm, dm), row), pl.BlockSpec((tm, dm), row)]
    out_shapes = strided_shapes + [
        jax.ShapeDtypeStruct((n, 512), BF16), jax.ShapeDtypeStruct((n, 256), BF16),
        jax.ShapeDtypeStruct((n, 256), BF16), jax.ShapeDtypeStruct((n, dm), BF16),
        jax.ShapeDtypeStruct((n, dm), BF16)]
    return pl.pallas_call(
        functools.partial(_inproj_kernel, tm=tm),
        grid=grid,
        in_specs=[pl.BlockSpec((tm, dm), row),
                  pl.BlockSpec((1, 6, dm), lambda i: (i // tpb, 0, 0)),
                  pl.BlockSpec((1, dm), lambda i: (0, 0)),
                  pl.BlockSpec((tm, LANES), row),
                  pl.BlockSpec((tm, LANES), row),
                  pl.BlockSpec(w_in_bf.shape, lambda i: (0, 0))],
        out_specs=out_specs,
        out_shape=out_shapes,
        scratch_shapes=[pltpu.VMEM((2, tm, LANES), F32)],
        compiler_params=_cparams(("arbitrary",)),
        name="inproj",
    )(x2, mod3, g_mix, cos_t, sin_t, w_in_bf)


def _dil_kernel(q_ref, k_ref, v_ref, o_ref, l_ref, *, length, tq, tk):
    lane_q = lax.broadcasted_iota(I32, (tq, LANES), 1)
    low = lane_q < 64
    row = lax.broadcasted_iota(I32, (tq, tk), 0)
    col = lax.broadcasted_iota(I32, (tq, tk), 1)

    def body(j, carry):
        qs = pl.multiple_of(j * tq, tq)
        ks = pl.multiple_of(jnp.clip(qs - DIL_HALF_WINDOW, 0, length - tk), DIL_HALF_WINDOW)
        rel = (ks + col) - (qs + row)
        mask = jnp.abs(rel) <= DIL_HALF_WINDOW
        for c in range(DIL_GROUP_W // LANES):
            cs = slice(c * LANES, (c + 1) * LANES)
            q2 = q_ref[pl.ds(qs, tq), cs]
            k2 = k_ref[pl.ds(ks, tk), cs]
            v2 = v_ref[pl.ds(ks, tk), cs]
            outs, lses = [], []
            for sel in (low, ~low):
                qm = jnp.where(sel, q2, jnp.zeros_like(q2))
                s = lax.dot_general(qm, k2, (((1,), (1,)), ((), ())), preferred_element_type=F32)
                s = jnp.where(mask, s, NEG_INF)
                m = jnp.max(s, axis=-1, keepdims=True)
                e = jnp.exp(s - m)
                den = jnp.sum(e, axis=-1, keepdims=True)
                o = jnp.dot(e.astype(BF16), v2, preferred_element_type=F32)
                outs.append(o / den)
                lses.append(m + jnp.log(den))
            o_ref[pl.ds(qs, tq), cs] = jnp.where(low, outs[0], outs[1]).astype(BF16)
            l_ref[pl.ds(qs, tq), cs] = jnp.where(low, lses[0], lses[1])
        return carry

    lax.fori_loop(0, length // tq, body, 0)


def _dil_attention(q, k, v):
    batch, d, length, w = q.shape
    tq, tk = 128, 256
    spec = pl.BlockSpec((None, None, length, w), lambda b, r: (b, r, 0, 0))
    return pl.pallas_call(
        functools.partial(_dil_kernel, length=length, tq=tq, tk=tk),
        grid=(batch, d),
        in_specs=[spec, spec, spec],
        out_specs=[spec, spec],
        out_shape=[jax.ShapeDtypeStruct(q.shape, BF16), jax.ShapeDtypeStruct(q.shape, F32)],
        compiler_params=_cparams(("arbitrary", "arbitrary")),
        name=f"dil{d}",
    )(q, k, v)


def _swa_kernel(sink_ref, q_ref, k_ref, v_ref, o_ref, *, length, tq, tk):
    lane_q = lax.broadcasted_iota(I32, (tq, LANES), 1)
    low = lane_q < 64
    row = lax.broadcasted_iota(I32, (tq, tk), 0)
    col = lax.broadcasted_iota(I32, (tq, tk), 1)
    zero = jnp.zeros((tq, LANES), BF16)

    def body(j, carry):
        qs = pl.multiple_of(j * tq, tq)
        ks = pl.multiple_of(jnp.clip(qs - SWA_WINDOW, 0, length - tk), SWA_WINDOW)
        rel = (ks + col) - (qs + row)
        mask = jnp.abs(rel) <= SWA_WINDOW
        for c in range(2):
            k2 = k_ref[pl.ds(ks, tk), c * LANES:(c + 1) * LANES]
            v2 = v_ref[pl.ds(ks, tk), c * LANES:(c + 1) * LANES]
            for p in range(2):
                blk = 2 * c + p
                q2 = q_ref[pl.ds(qs, tq), blk * LANES:(blk + 1) * LANES]
                outs = []
                for hh, sel in enumerate((low, ~low)):
                    sink = sink_ref[2 * blk + hh]
                    qm = jnp.where(sel, q2, zero)
                    s = lax.dot_general(qm, k2, (((1,), (1,)), ((), ())), preferred_element_type=F32)
                    s = jnp.where(mask, s, NEG_INF)
                    m = jnp.maximum(jnp.max(s, axis=-1, keepdims=True), sink)
                    e = jnp.exp(s - m)
                    den = jnp.sum(e, axis=-1, keepdims=True) + jnp.exp(sink - m)
                    o = jnp.dot(e.astype(BF16), v2, preferred_element_type=F32)
                    outs.append(o / den)
                o_ref[pl.ds(qs, tq), blk * LANES:(blk + 1) * LANES] = jnp.where(low, outs[0], outs[1]).astype(BF16)
        return carry

    lax.fori_loop(0, length // tq, body, 0)


def _swa_attention(sink, q, k, v):
    batch, length, qw = q.shape
    tq, tk = 128, 384
    return pl.pallas_call(
        functools.partial(_swa_kernel, length=length, tq=tq, tk=tk),
        grid=(batch,),
        in_specs=[pl.BlockSpec(memory_space=pltpu.SMEM),
                  pl.BlockSpec((None, length, qw), lambda b: (b, 0, 0)),
                  pl.BlockSpec((None, length, k.shape[2]), lambda b: (b, 0, 0)),
                  pl.BlockSpec((None, length, v.shape[2]), lambda b: (b, 0, 0))],
        out_specs=pl.BlockSpec((None, length, qw), lambda b: (b, 0, 0)),
        out_shape=jax.ShapeDtypeStruct(q.shape, BF16),
        compiler_params=_cparams(("arbitrary",)),
        name="swa",
    )(sink, q, k, v)


def _outproj_kernel(o0_ref, l0_ref, o1_ref, l1_ref, o2_ref, l2_ref, ob_ref, sga_ref, sgb_ref, x_ref,
                    mod_ref, g_ref, wa_ref, wb_ref, wo_ref, wr_ref, br_ref,
                    x1_ref, h2_ref, route_ref, cnt_ref,
                    so1_ref, sl1_ref, so2_ref, sl2_ref, acc_ref, *, tm):
    i = pl.program_id(0)

    @pl.when(i == 0)
    def _():
        acc_ref[...] = jnp.zeros_like(acc_ref)

    for (o_ref, l_ref, so_ref, sl_ref, d) in ((o1_ref, l1_ref, so1_ref, sl1_ref, DILATIONS[1]),
                                              (o2_ref, l2_ref, so2_ref, sl2_ref, DILATIONS[2])):
        for r in range(d):
            for c in range(2):
                cs = slice(c * LANES, (c + 1) * LANES)
                so_ref[c, pl.ds(r, tm // d, stride=d), :] = o_ref[0, r, :, cs].astype(F32)
                sl_ref[c, pl.ds(r, tm // d, stride=d), :] = l_ref[0, r, :, cs]
    o0 = o0_ref[0, 0].astype(F32)
    l0 = l0_ref[0, 0]
    both = lambda ref: jnp.concatenate([ref[0], ref[1]], axis=1)
    o1, l1, o2, l2 = both(so1_ref), both(sl1_ref), both(so2_ref), both(sl2_ref)
    mx = jnp.maximum(jnp.maximum(l0, l1), l2)
    w0, w1, w2 = jnp.exp(l0 - mx), jnp.exp(l1 - mx), jnp.exp(l2 - mx)
    o_a = (w0 * o0 + w1 * o1 + w2 * o2) / (w0 + w1 + w2)
    y_a = jnp.dot(o_a.astype(BF16), wa_ref[...], preferred_element_type=F32)
    y_b = jnp.dot(ob_ref[...], wb_ref[...], preferred_element_type=F32)
    merged = sga_ref[...].astype(F32) * y_a + sgb_ref[...].astype(F32) * y_b
    mix = jnp.dot(merged.astype(BF16), wo_ref[...], preferred_element_type=F32)
    x1 = x_ref[...] + mod_ref[0, 2:3, :] * mix
    x1_ref[...] = x1
    h2 = _rms_mod(x1, g_ref[...], mod_ref[0, 3:4, :], mod_ref[0, 4:5, :])
    h2_ref[...] = h2

    logits = jnp.dot(h2, wr_ref[...], preferred_element_type=F32, precision=lax.Precision.HIGHEST) + br_ref[...]
    lane = lax.broadcasted_iota(I32, (tm, LANES), 1).astype(F32)
    big = 1e9
    is_g = lane < N_GROUPS
    gl = jnp.where(is_g, logits, NEG_INF)
    gmax = jnp.max(gl, axis=-1, keepdims=True)
    gsel = jnp.min(jnp.where(is_g & (gl == gmax), lane, big), axis=-1, keepdims=True)
    gw = 1.0 / jnp.sum(jnp.where(is_g, jnp.exp(gl - gmax), 0.0), axis=-1, keepdims=True)
    e_lo = N_GROUPS + gsel * EXPERTS_PER_GROUP
    in_grp = (lane >= e_lo) & (lane < e_lo + EXPERTS_PER_GROUP)
    el = jnp.where(in_grp, logits, NEG_INF)
    m1 = jnp.max(el, axis=-1, keepdims=True)
    i1 = jnp.min(jnp.where(in_grp & (el == m1), lane, big), axis=-1, keepdims=True)
    el2 = jnp.where(lane == i1, NEG_INF, el)
    m2 = jnp.max(el2, axis=-1, keepdims=True)
    i2 = jnp.min(jnp.where(in_grp & (lane != i1) & (el2 == m2), lane, big), axis=-1, keepdims=True)
    t = jnp.exp(m2 - m1)
    tw1 = gw / (1.0 + t)
    tw2 = gw * t / (1.0 + t)
    e1 = i1 - N_GROUPS
    e2 = i2 - N_GROUPS

    onehot = jnp.where((lane == e1) | (lane == e2), 1.0, 0.0)
    rr = lax.broadcasted_iota(I32, (tm, tm), 0)
    cc = lax.broadcasted_iota(I32, (tm, tm), 1)
    tri = jnp.where(rr > cc, 1.0, 0.0).astype(BF16)
    prefix = jnp.dot(tri, onehot.astype(BF16), preferred_element_type=F32) + acc_ref[0:1, :]
    r1 = jnp.sum(jnp.where(lane == e1, prefix, 0.0), axis=-1, keepdims=True)
    r2 = jnp.sum(jnp.where(lane == e2, prefix, 0.0), axis=-1, keepdims=True)
    acc_ref[...] = acc_ref[...] + jnp.sum(onehot, axis=0, keepdims=True)
    cnt_ref[...] = acc_ref[...]
    route = jnp.where(lane == 0, tw1, 0.0)
    route = jnp.where(lane == 1, tw2, route)
    route = jnp.where(lane == 2, e1, route)
    route = jnp.where(lane == 3, e2, route)
    route = jnp.where(lane == 4, r1, route)
    route = jnp.where(lane == 5, r2, route)
    route_ref[...] = route


def _outproj(dil_outs, ob, sga, sgb, x2, mod3, g_ffn, wa, wb, wo, wr, br, batch, seq):
    n, dm = x2.shape
    tm = 512
    tpb = seq // tm
    row = lambda i: (i, 0)
    const = lambda i: (0, 0)
    in_specs = []
    args = []
    for (o, l), d in zip(dil_outs, DILATIONS):
        spec = pl.BlockSpec((1, d, tm // d, DIL_GROUP_W), lambda i: (i // tpb, 0, i % tpb, 0))
        in_specs += [spec, spec]
        args += [o, l]
    in_specs += [pl.BlockSpec((tm, ob.shape[1]), row), pl.BlockSpec((tm, dm), row), pl.BlockSpec((tm, dm), row),
                 pl.BlockSpec((tm, dm), row),
                 pl.BlockSpec((1, 6, dm), lambda i: (i // tpb, 0, 0)),
                 pl.BlockSpec((1, dm), const),
                 pl.BlockSpec(wa.shape, const), pl.BlockSpec(wb.shape, const), pl.BlockSpec(wo.shape, const),
                 pl.BlockSpec(wr.shape, const), pl.BlockSpec(br.shape, const)]
    args += [ob, sga, sgb, x2, mod3, g_ffn, wa, wb, wo, wr, br]
    return pl.pallas_call(
        functools.partial(_outproj_kernel, tm=tm),
        grid=(n // tm,),
        in_specs=in_specs,
        out_specs=[pl.BlockSpec((tm, dm), row), pl.BlockSpec((tm, dm), row), pl.BlockSpec((tm, LANES), row),
                   pl.BlockSpec((8, LANES), const)],
        out_shape=[jax.ShapeDtypeStruct((n, dm), F32), jax.ShapeDtypeStruct((n, dm), F32),
                   jax.ShapeDtypeStruct((n, LANES), F32), jax.ShapeDtypeStruct((8, LANES), F32)],
        scratch_shapes=[pltpu.VMEM((2, tm, LANES), F32)] * 4 + [pltpu.VMEM((8, LANES), F32)],
        compiler_params=_cparams(("arbitrary",)),
        name="outproj",
    )(*args)


def _dispatch_kernel(pstart_ref, e_ref, r_ref, h_ref, zero_ref, xr_ref, sem, *, tb):
    del zero_ref

    def copy(t, k):
        idx = 2 * t + k
        dst = pstart_ref[e_ref[0, 0, idx]] + r_ref[0, 0, idx]
        return pltpu.make_async_copy(h_ref.at[pl.ds(t, 1), :], xr_ref.at[pl.ds(dst, 1), :], sem)

    def issue(t, carry):
        copy(t, 0).start()
        copy(t, 1).start()
        return carry

    lax.fori_loop(0, tb, issue, 0)

    def drain(t, carry):
        copy(t, 0).wait()
        copy(t, 1).wait()
        return carry

    lax.fori_loop(0, tb, drain, 0)


def _dispatch(pstart, e_flat, r_flat, h2, rows):
    n, dm = h2.shape
    tb = 512
    steps = n // tb
    e3 = e_flat.reshape(steps, 1, 2 * tb)
    r3 = r_flat.reshape(steps, 1, 2 * tb)
    zeros = jnp.zeros((rows, dm), F32)
    grid_spec = pltpu.PrefetchScalarGridSpec(
        num_scalar_prefetch=1,
        grid=(steps,),
        in_specs=[pl.BlockSpec((1, 1, 2 * tb), lambda i, ps: (i, 0, 0), memory_space=pltpu.SMEM),
                  pl.BlockSpec((1, 1, 2 * tb), lambda i, ps: (i, 0, 0), memory_space=pltpu.SMEM),
                  pl.BlockSpec((tb, dm), lambda i, ps: (i, 0)),
                  pl.BlockSpec(memory_space=pl.ANY)],
        out_specs=pl.BlockSpec(memory_space=pl.ANY),
        scratch_shapes=[pltpu.SemaphoreType.DMA(())],
    )
    return pl.pallas_call(
        functools.partial(_dispatch_kernel, tb=tb),
        grid_spec=grid_spec,
        out_shape=jax.ShapeDtypeStruct((rows, dm), F32),
        input_output_aliases={4: 0},
        compiler_params=_cparams(("arbitrary",)),
        name="dispatch",
    )(pstart, e3, r3, h2, zeros)


def _experts_kernel(blk_e_ref, nused_ref, x_ref, wg_ref, wu_ref, wd_ref, y_ref, wgb_ref, wub_ref, wdb_ref):
    i = pl.program_id(0)
    used = i < nused_ref[0]
    prev = blk_e_ref[jnp.maximum(i - 1, 0)]
    changed = (i == 0) | (used & (blk_e_ref[i] != prev))

    @pl.when(changed)
    def _():
        wgb_ref[...] = wg_ref[0].astype(BF16)
        wub_ref[...] = wu_ref[0].astype(BF16)
        wdb_ref[...] = wd_ref[0].astype(BF16)

    @pl.when(used)
    def _():
        xb = x_ref[...].astype(BF16)
        g = jnp.dot(xb, wgb_ref[...], preferred_element_type=F32)
        u = jnp.dot(xb, wub_ref[...], preferred_element_type=F32)
        a = (g * jax.nn.sigmoid(g)) * u
        y_ref[...] = jnp.dot(a.astype(BF16), wdb_ref[...], preferred_element_type=F32)

    @pl.when(jnp.logical_not(used))
    def _():
        y_ref[...] = jnp.zeros_like(y_ref)


def _experts(blk_e, nused, x_rows, w_gate, w_up, w_down):
    rows, dm = x_rows.shape
    de = w_gate.shape[2]
    nblk = rows // ROW_BLOCK
    xmap = lambda i, be, nu: (jnp.minimum(i, nu[0] - 1), 0)
    wmap = lambda i, be, nu: (be[jnp.minimum(i, nu[0] - 1)], 0, 0)
    grid_spec = pltpu.PrefetchScalarGridSpec(
        num_scalar_prefetch=2,
        grid=(nblk,),
        in_specs=[pl.BlockSpec((ROW_BLOCK, dm), xmap),
                  pl.BlockSpec((1, dm, de), wmap),
                  pl.BlockSpec((1, dm, de), wmap),
                  pl.BlockSpec((1, de, dm), wmap)],
        out_specs=pl.BlockSpec((ROW_BLOCK, dm), lambda i, be, nu: (i, 0)),
        scratch_shapes=[pltpu.VMEM((dm, de), BF16), pltpu.VMEM((dm, de), BF16), pltpu.VMEM((de, dm), BF16)],
    )
    return pl.pallas_call(
        _experts_kernel,
        grid_spec=grid_spec,
        out_shape=jax.ShapeDtypeStruct((rows, dm), F32),
        compiler_params=_cparams(("arbitrary",)),
        name="experts",
    )(blk_e, nused, x_rows, w_gate, w_up, w_down)


def _combine_kernel(pstart_ref, e_ref, r_ref, route_ref, x1_ref, mod_ref, g_ref, y_ref, o_ref, buf_ref, sem, *, tb):
    def copy(t, k):
        idx = 2 * t + k
        src = pstart_ref[e_ref[0, 0, idx]] + r_ref[0, 0, idx]
        return pltpu.make_async_copy(y_ref.at[pl.ds(src, 1), :], buf_ref.at[k, pl.ds(t, 1), :], sem)

    def issue(t, carry):
        copy(t, 0).start()
        copy(t, 1).start()
        return carry

    lax.fori_loop(0, tb, issue, 0)

    def drain(t, carry):
        copy(t, 0).wait()
        copy(t, 1).wait()
        return carry

    lax.fori_loop(0, tb, drain, 0)
    route = route_ref[...]
    moe = route[:, 0:1] * buf_ref[0] + route[:, 1:2] * buf_ref[1]
    x = x1_ref[...] + mod_ref[0, 5:6, :] * moe
    ms = jnp.mean(x * x, axis=-1, keepdims=True)
    o_ref[...] = (x * lax.rsqrt(ms + RMS_EPS)) * g_ref[...]


def _combine(pstart, e_flat, r_flat, route, x1, mod3, g_final, y_rows, seq):
    n, dm = x1.shape
    tb = 512
    steps = n // tb
    tpb = seq // tb
    e3 = e_flat.reshape(steps, 1, 2 * tb)
    r3 = r_flat.reshape(steps, 1, 2 * tb)
    grid_spec = pltpu.PrefetchScalarGridSpec(
        num_scalar_prefetch=1,
        grid=(steps,),
        in_specs=[pl.BlockSpec((1, 1, 2 * tb), lambda i, ps: (i, 0, 0), memory_space=pltpu.SMEM),
                  pl.BlockSpec((1, 1, 2 * tb), lambda i, ps: (i, 0, 0), memory_space=pltpu.SMEM),
                  pl.BlockSpec((tb, LANES), lambda i, ps: (i, 0)),
                  pl.BlockSpec((tb, dm), lambda i, ps: (i, 0)),
                  pl.BlockSpec((1, 6, dm), lambda i, ps: (i // tpb, 0, 0)),
                  pl.BlockSpec((1, dm), lambda i, ps: (0, 0)),
                  pl.BlockSpec(memory_space=pl.ANY)],
        out_specs=pl.BlockSpec((tb, dm), lambda i, ps: (i, 0)),
        scratch_shapes=[pltpu.VMEM((2, tb, dm), F32), pltpu.SemaphoreType.DMA(())],
    )
    return pl.pallas_call(
        functools.partial(_combine_kernel, tb=tb),
        grid_spec=grid_spec,
        out_shape=jax.ShapeDtypeStruct((n, dm), F32),
        compiler_params=_cparams(("arbitrary",)),
        name="combine",
    )(pstart, e3, r3, route, x1, mod3, g_final, y_rows)


def _rope_tables(positions):
    half = HEAD_DIM // 2
    inv_freq = ROPE_THETA ** (-jnp.arange(half, dtype=F32) * (2.0 / HEAD_DIM))
    ang = positions.astype(F32).reshape(-1, 1) * inv_freq
    cos, sin = jnp.cos(ang), jnp.sin(ang)
    cos_t = jnp.concatenate([cos, cos, cos, cos], axis=1)
    sin_t = jnp.concatenate([-sin, sin, -sin, sin], axis=1)
    return cos_t, sin_t


def kernel(x, c, positions, w_ada, b_ada, g_mix, w_in, sink_logits, w_branch_a, w_branch_b, w_out, g_ffn,
           w_group, b_group, w_route, b_route, w_expert_gate, w_expert_up, w_expert_down, g_final):
    batch, seq, dm = x.shape
    n = batch * seq
    assert w_ada.shape[0] == 1, "one layer"
    x2 = x.reshape(n, dm)

    c8 = jnp.pad(c, ((0, 8 - batch), (0, 0)))
    mod = _ada(c8, w_ada[0], b_ada[0].reshape(1, -1))
    mod3 = mod[:batch].reshape(batch, 6, dm)

    cos_t, sin_t = _rope_tables(positions)
    outs = _inproj(x2, mod3, g_mix[0].reshape(1, dm), cos_t, sin_t, w_in[0].astype(BF16), batch, seq)
    qkv = outs[:9]
    qb, kb, vb, sga, sgb = outs[9:]

    dil_outs = [_dil_attention(qkv[3 * g], qkv[3 * g + 1], qkv[3 * g + 2]) for g in range(len(DILATIONS))]
    ob = _swa_attention(sink_logits[0], qb.reshape(batch, seq, -1), kb.reshape(batch, seq, -1),
                        vb.reshape(batch, seq, -1)).reshape(n, -1)

    wr = jnp.zeros((dm, LANES), F32).at[:, :N_GROUPS].set(w_group[0]).at[:, N_GROUPS:N_GROUPS + N_EXPERTS].set(w_route[0])
    br = jnp.zeros((1, LANES), F32).at[0, :N_GROUPS].set(b_group[0]).at[0, N_GROUPS:N_GROUPS + N_EXPERTS].set(b_route[0])
    x1, h2, route, counts = _outproj(dil_outs, ob, sga, sgb, x2, mod3, g_ffn[0].reshape(1, dm),
                                     w_branch_a[0].astype(BF16), w_branch_b[0].astype(BF16), w_out[0].astype(BF16),
                                     wr, br, batch, seq)

    cnt = counts[0, :N_EXPERTS].astype(I32)
    padded = (cnt + ROW_BLOCK - 1) // ROW_BLOCK * ROW_BLOCK
    pend = jnp.cumsum(padded)
    pstart = pend - padded
    nblk = (2 * n) // ROW_BLOCK + N_EXPERTS
    rows = nblk * ROW_BLOCK
    blk_e = jnp.clip(jnp.searchsorted(pend, jnp.arange(nblk, dtype=I32) * ROW_BLOCK, side='right'),
                     0, N_EXPERTS - 1).astype(I32)
    nused = (pend[-1:] // ROW_BLOCK).astype(I32)
    e_flat = route[:, 2:4].astype(I32).reshape(-1)
    r_flat = route[:, 4:6].astype(I32).reshape(-1)

    x_rows = _dispatch(pstart, e_flat, r_flat, h2, rows)
    y_rows = _experts(blk_e, nused, x_rows, w_expert_gate[0], w_expert_up[0], w_expert_down[0])
    out = _combine(pstart, e_flat, r_flat, route, x1, mod3, g_final.reshape(1, dm), y_rows, seq)
    return out.reshape(batch, seq, dm)
```

```python
import functools

import jax
import jax.numpy as jnp
from jax import lax
from jax.experimental import pallas as pl
from jax.experimental.pallas import tpu as pltpu

F32 = jnp.float32
BF16 = jnp.bfloat16
I32 = jnp.int32

HEAD_DIM = 64
ROPE_THETA = 10000.0
RMS_EPS = 1e-6
NEG_INF = -1e30
DILATIONS = (1, 4, 16)
DIL_HALF_WINDOW = 64
DIL_GROUP_W = 256
SWA_WINDOW = 128
N_GROUPS = 4
EXPERTS_PER_GROUP = 8
N_EXPERTS = 32
LANES = 128
ROW_BLOCK = 256
SEG_ROWS = 16
MOE_TILE = 512
VMEM_LIMIT = 56 * 1024 * 1024


def _cparams(sem):
    return pltpu.CompilerParams(dimension_semantics=sem, vmem_limit_bytes=VMEM_LIMIT)


def _ada_kernel(c_ref, w_ref, b_ref, o_ref):
    c = c_ref[...]
    cs = c * jax.nn.sigmoid(c)
    o_ref[...] = jnp.dot(cs.astype(BF16), w_ref[...].astype(BF16), preferred_element_type=F32) + b_ref[...]


def _ada(c8, w_ada, b_ada):
    d, n = w_ada.shape
    tn = 1536
    return pl.pallas_call(
        _ada_kernel,
        grid=(n // tn,),
        in_specs=[pl.BlockSpec((8, d), lambda j: (0, 0)),
                  pl.BlockSpec((d, tn), lambda j: (0, j)),
                  pl.BlockSpec((1, tn), lambda j: (0, j))],
        out_specs=pl.BlockSpec((8, tn), lambda j: (0, j)),
        out_shape=jax.ShapeDtypeStruct((8, n), F32),
        compiler_params=_cparams(("arbitrary",)),
        name="ada",
    )(c8, w_ada, b_ada)


def _rms_mod(x, g, shift, scale):
    ms = jnp.mean(x * x, axis=-1, keepdims=True)
    return (x * lax.rsqrt(ms + RMS_EPS)) * g * (1.0 + scale) + shift


def _inproj_kernel(x_ref, mod_ref, g_ref, cos_ref, sin_ref, w_ref,
                   q0_ref, k0_ref, v0_ref, q1_ref, k1_ref, v1_ref, q2_ref, k2_ref, v2_ref,
                   qb_ref, kb_ref, vb_ref, sga_ref, sgb_ref, stg_ref, *, tm):
    h = _rms_mod(x_ref[...], g_ref[...], mod_ref[0, 0:1, :], mod_ref[0, 1:2, :])
    hb = h.astype(BF16)
    cos = cos_ref[...]
    sin = sin_ref[...]
    lane = lax.broadcasted_iota(I32, (tm, LANES), 1)
    first_half = (lane & 32) == 0
    low = lane < 64

    def proj(c0, width):
        return jnp.dot(hb, w_ref[:, c0:c0 + width], preferred_element_type=F32)

    def rope(t):
        rot = jnp.where(first_half, pltpu.roll(t, 96, 1), pltpu.roll(t, 32, 1))
        return t * cos + rot * sin

    def rope256(p):
        return jnp.concatenate([rope(p[:, :LANES]), rope(p[:, LANES:])], axis=1)

    def store_group(ref, val, d):
        if d == 1:
            ref[0, 0] = val.astype(BF16)
        else:
            for c in range(2):
                stg_ref[c] = val[:, c * LANES:(c + 1) * LANES]
            for r in range(d):
                for c in range(2):
                    ref[0, r, :, c * LANES:(c + 1) * LANES] = (
                        stg_ref[c, pl.ds(r, tm // d, stride=d), :].astype(BF16))

    q_refs = (q0_ref, q1_ref, q2_ref)
    k_refs = (k0_ref, k1_ref, k2_ref)
    v_refs = (v0_ref, v1_ref, v2_ref)
    for g, d in enumerate(DILATIONS):
        store_group(q_refs[g], rope256(proj(g * 256, 256)) * 0.125, d)
        store_group(k_refs[g], rope256(proj(768 + g * 256, 256)), d)
        store_group(v_refs[g], proj(1536 + g * 256, 256), d)
    for j in range(2):
        qb_ref[:, j * 256:(j + 1) * 256] = (rope256(proj(2304 + j * 256, 256)) * 0.125).astype(BF16)
    kv = proj(2816, 256)
    kb = rope(kv[:, :LANES])
    vb = kv[:, LANES:]
    kb_sw = pltpu.roll(kb, 64, 1)
    vb_sw = pltpu.roll(vb, 64, 1)
    kb_ref[:, :LANES] = jnp.where(low, kb, kb_sw).astype(BF16)
    kb_ref[:, LANES:] = jnp.where(low, kb_sw, kb).astype(BF16)
    vb_ref[:, :LANES] = jnp.where(low, vb, vb_sw).astype(BF16)
    vb_ref[:, LANES:] = jnp.where(low, vb_sw, vb).astype(BF16)
    for j in range(4):
        sga_ref[:, j * 256:(j + 1) * 256] = jax.nn.sigmoid(proj(3072 + j * 256, 256)).astype(BF16)
        sgb_ref[:, j * 256:(j + 1) * 256] = jax.nn.sigmoid(proj(4096 + j * 256, 256)).astype(BF16)


def _inproj(x2, mod3, g_mix, cos_t, sin_t, w_in_bf, batch, seq):
    n, dm = x2.shape
    tm = 512
    tpb = seq // tm
    grid = (n // tm,)
    row = lambda i: (i, 0)
    strided_specs, strided_shapes = [], []
    for d in DILATIONS:
        for _ in range(3):
            strided_specs.append(pl.BlockSpec((1, d, tm // d, DIL_GROUP_W), lambda i: (i // tpb, 0, i % tpb, 0)))
            strided_shapes.append(jax.ShapeDtypeStruct((batch, d, seq // d, DIL_GROUP_W), BF16))
    out_specs = strided_specs + [
        pl.BlockSpec((tm, 512), row), pl.BlockSpec((tm, 256), row), pl.BlockSpec((tm, 256), row),
        pl.BlockSpec((tm, dm), row), pl.BlockSpec((tm, dm), row)]
    out_shapes = strided_shapes + [
        jax.ShapeDtypeStruct((n, 512), BF16), jax.ShapeDtypeStruct((n, 256), BF16),
        jax.ShapeDtypeStruct((n, 256), BF16), jax.ShapeDtypeStruct((n, dm), BF16),
        jax.ShapeDtypeStruct((n, dm), BF16)]
    return pl.pallas_call(
        functools.partial(_inproj_kernel, tm=tm),
        grid=grid,
        in_specs=[pl.BlockSpec((tm, dm), row),
                  pl.BlockSpec((1, 6, dm), lambda i: (i // tpb, 0, 0)),
                  pl.BlockSpec((1, dm), lambda i: (0, 0)),
                  pl.BlockSpec((tm, LANES), row),
                  pl.BlockSpec((tm, LANES), row),
                  pl.BlockSpec(w_in_bf.shape, lambda i: (0, 0))],
        out_specs=out_specs,
        out_shape=out_shapes,
        scratch_shapes=[pltpu.VMEM((2, tm, LANES), F32)],
        compiler_params=_cparams(("arbitrary",)),
        name="inproj",
    )(x2, mod3, g_mix, cos_t, sin_t, w_in_bf)


def _dil_kernel(q_ref, k_ref, v_ref, o_ref, l_ref, *, length, tq, tk):
    lane_q = lax.broadcasted_iota(I32, (tq, LANES), 1)
    low = lane_q < 64
    row = lax.broadcasted_iota(I32, (tq, tk), 0)
    col = lax.broadcasted_iota(I32, (tq, tk), 1)

    def body(j, carry):
        qs = pl.multiple_of(j * tq, tq)
        ks = pl.multiple_of(jnp.clip(qs - DIL_HALF_WINDOW, 0, length - tk), DIL_HALF_WINDOW)
        rel = (ks + col) - (qs + row)
        mask = jnp.abs(rel) <= DIL_HALF_WINDOW
        for c in range(DIL_GROUP_W // LANES):
            cs = slice(c * LANES, (c + 1) * LANES)
            q2 = q_ref[pl.ds(qs, tq), cs]
            k2 = k_ref[pl.ds(ks, tk), cs]
            v2 = v_ref[pl.ds(ks, tk), cs]
            outs, lses = [], []
            for sel in (low, ~low):
                qm = jnp.where(sel, q2, jnp.zeros_like(q2))
                s = lax.dot_general(qm, k2, (((1,), (1,)), ((), ())), preferred_element_type=F32)
                s = jnp.where(mask, s, NEG_INF)
                m = jnp.max(s, axis=-1, keepdims=True)
                e = jnp.exp(s - m)
                den = jnp.sum(e, axis=-1, keepdims=True)
                o = jnp.dot(e.astype(BF16), v2, preferred_element_type=F32)
                outs.append(o / den)
                lses.append(m + jnp.log(den))
            o_ref[pl.ds(qs, tq), cs] = jnp.where(low, outs[0], outs[1]).astype(BF16)
            l_ref[pl.ds(qs, tq), cs] = jnp.where(low, lses[0], lses[1])
        return carry

    lax.fori_loop(0, length // tq, body, 0)


def _dil_attention(q, k, v):
    batch, d, length, w = q.shape
    tq, tk = 128, 256
    spec = pl.BlockSpec((None, None, length, w), lambda b, r: (b, r, 0, 0))
    return pl.pallas_call(
        functools.partial(_dil_kernel, length=length, tq=tq, tk=tk),
        grid=(batch, d),
        in_specs=[spec, spec, spec],
        out_specs=[spec, spec],
        out_shape=[jax.ShapeDtypeStruct(q.shape, BF16), jax.ShapeDtypeStruct(q.shape, F32)],
        compiler_params=_cparams(("arbitrary", "arbitrary")),
        name=f"dil{d}",
    )(q, k, v)


def _swa_kernel(sink_ref, q_ref, k_ref, v_ref, o_ref, *, length, tq, tk):
    lane_q = lax.broadcasted_iota(I32, (tq, LANES), 1)
    low = lane_q < 64
    row = lax.broadcasted_iota(I32, (tq, tk), 0)
    col = lax.broadcasted_iota(I32, (tq, tk), 1)
    zero = jnp.zeros((tq, LANES), BF16)

    def body(j, carry):
        qs = pl.multiple_of(j * tq, tq)
        ks = pl.multiple_of(jnp.clip(qs - SWA_WINDOW, 0, length - tk), SWA_WINDOW)
        rel = (ks + col) - (qs + row)
        mask = jnp.abs(rel) <= SWA_WINDOW
        for c in range(2):
            k2 = k_ref[pl.ds(ks, tk), c * LANES:(c + 1) * LANES]
            v2 = v_ref[pl.ds(ks, tk), c * LANES:(c + 1) * LANES]
            for p in range(2):
                blk = 2 * c + p
                q2 = q_ref[pl.ds(qs, tq), blk * LANES:(blk + 1) * LANES]
                outs = []
                for hh, sel in enumerate((low, ~low)):
                    sink = sink_ref[2 * blk + hh]
                    qm = jnp.where(sel, q2, zero)
                    s = lax.dot_general(qm, k2, (((1,), (1,)), ((), ())), preferred_element_type=F32)
                    s = jnp.where(mask, s, NEG_INF)
                    m = jnp.maximum(jnp.max(s, axis=-1, keepdims=True), sink)
                    e = jnp.exp(s - m)
                    den = jnp.sum(e, axis=-1, keepdims=True) + jnp.exp(sink - m)
                    o = jnp.dot(e.astype(BF16), v2, preferred_element_type=F32)
                    outs.append(o / den)
                o_ref[pl.ds(qs, tq), blk * LANES:(blk + 1) * LANES] = jnp.where(low, outs[0], outs[1]).astype(BF16)
        return carry

    lax.fori_loop(0, length // tq, body, 0)


def _swa_attention(sink, q, k, v):
    batch, length, qw = q.shape
    tq, tk = 128, 384
    return pl.pallas_call(
        functools.partial(_swa_kernel, length=length, tq=tq, tk=tk),
        grid=(batch,),
        in_specs=[pl.BlockSpec(memory_space=pltpu.SMEM),
                  pl.BlockSpec((None, length, qw), lambda b: (b, 0, 0)),
                  pl.BlockSpec((None, length, k.shape[2]), lambda b: (b, 0, 0)),
                  pl.BlockSpec((None, length, v.shape[2]), lambda b: (b, 0, 0))],
        out_specs=pl.BlockSpec((None, length, qw), lambda b: (b, 0, 0)),
        out_shape=jax.ShapeDtypeStruct(q.shape, BF16),
        compiler_params=_cparams(("arbitrary",)),
        name="swa",
    )(sink, q, k, v)


def _outproj_kernel(o0_ref, l0_ref, o1_ref, l1_ref, o2_ref, l2_ref, ob_ref, sga_ref, sgb_ref, x_ref,
                    mod_ref, g_ref, wa_ref, wb_ref, wo_ref, wr_ref, br_ref,
                    x1_ref, h2_ref, route_ref, len_ref,
                    so1_ref, sl1_ref, so2_ref, sl2_ref, *, tm):
    for (o_ref, l_ref, so_ref, sl_ref, d) in ((o1_ref, l1_ref, so1_ref, sl1_ref, DILATIONS[1]),
                                              (o2_ref, l2_ref, so2_ref, sl2_ref, DILATIONS[2])):
        for r in range(d):
            for c in range(2):
                cs = slice(c * LANES, (c + 1) * LANES)
                so_ref[c, pl.ds(r, tm // d, stride=d), :] = o_ref[0, r, :, cs].astype(F32)
                sl_ref[c, pl.ds(r, tm // d, stride=d), :] = l_ref[0, r, :, cs]
    o0 = o0_ref[0, 0].astype(F32)
    l0 = l0_ref[0, 0]
    both = lambda ref: jnp.concatenate([ref[0], ref[1]], axis=1)
    o1, l1, o2, l2 = both(so1_ref), both(sl1_ref), both(so2_ref), both(sl2_ref)
    mx = jnp.maximum(jnp.maximum(l0, l1), l2)
    w0, w1, w2 = jnp.exp(l0 - mx), jnp.exp(l1 - mx), jnp.exp(l2 - mx)
    o_a = (w0 * o0 + w1 * o1 + w2 * o2) / (w0 + w1 + w2)
    y_a = jnp.dot(o_a.astype(BF16), wa_ref[...], preferred_element_type=F32)
    y_b = jnp.dot(ob_ref[...], wb_ref[...], preferred_element_type=F32)
    merged = sga_ref[...].astype(F32) * y_a + sgb_ref[...].astype(F32) * y_b
    mix = jnp.dot(merged.astype(BF16), wo_ref[...], preferred_element_type=F32)
    x1 = x_ref[...] + mod_ref[0, 2:3, :] * mix
    x1_ref[...] = x1
    h2 = _rms_mod(x1, g_ref[...], mod_ref[0, 3:4, :], mod_ref[0, 4:5, :])
    h2_ref[...] = h2.astype(BF16)

    logits = jnp.dot(h2, wr_ref[...], preferred_element_type=F32, precision=lax.Precision.HIGHEST) + br_ref[...]
    lane = lax.broadcasted_iota(I32, (tm, LANES), 1).astype(F32)
    big = 1e9
    is_g = lane < N_GROUPS
    gl = jnp.where(is_g, logits, NEG_INF)
    gmax = jnp.max(gl, axis=-1, keepdims=True)
    gsel = jnp.min(jnp.where(is_g & (gl == gmax), lane, big), axis=-1, keepdims=True)
    gw = 1.0 / jnp.sum(jnp.where(is_g, jnp.exp(gl - gmax), 0.0), axis=-1, keepdims=True)
    e_lo = N_GROUPS + gsel * EXPERTS_PER_GROUP
    in_grp = (lane >= e_lo) & (lane < e_lo + EXPERTS_PER_GROUP)
    el = jnp.where(in_grp, logits, NEG_INF)
    m1 = jnp.max(el, axis=-1, keepdims=True)
    i1 = jnp.min(jnp.where(in_grp & (el == m1), lane, big), axis=-1, keepdims=True)
    el2 = jnp.where(lane == i1, NEG_INF, el)
    m2 = jnp.max(el2, axis=-1, keepdims=True)
    i2 = jnp.min(jnp.where(in_grp & (lane != i1) & (el2 == m2), lane, big), axis=-1, keepdims=True)
    t = jnp.exp(m2 - m1)
    tw1 = gw / (1.0 + t)
    tw2 = gw * t / (1.0 + t)
    e1 = i1 - N_GROUPS
    e2 = i2 - N_GROUPS

    onehot = jnp.where((lane == e1) | (lane == e2), 1.0, 0.0)
    rr = lax.broadcasted_iota(I32, (tm, tm), 0)
    cc = lax.broadcasted_iota(I32, (tm, tm), 1)
    tri = jnp.where(rr > cc, 1.0, 0.0).astype(BF16)
    prefix = jnp.dot(tri, onehot.astype(BF16), preferred_element_type=F32)
    cnt = jnp.sum(onehot, axis=0, keepdims=True)
    seg_len = jnp.ceil(cnt * (1.0 / SEG_ROWS)) * SEG_ROWS
    ur = lax.broadcasted_iota(I32, (LANES, LANES), 0)
    uc = lax.broadcasted_iota(I32, (LANES, LANES), 1)
    upper = jnp.where(ur < uc, 1.0, 0.0).astype(BF16)
    seg_off = jnp.dot(jnp.broadcast_to(seg_len, (8, LANES)).astype(BF16), upper, preferred_element_type=F32)[0:1, :]
    slot_map = seg_off + prefix
    s1 = jnp.sum(jnp.where(lane == e1, slot_map, 0.0), axis=-1, keepdims=True)
    s2 = jnp.sum(jnp.where(lane == e2, slot_map, 0.0), axis=-1, keepdims=True)
    len_ref[0] = jnp.broadcast_to(seg_len, (8, LANES))
    route = jnp.where(lane == 0, tw1, 0.0)
    route = jnp.where(lane == 1, tw2, route)
    route = jnp.where(lane == 2, s1, route)
    route = jnp.where(lane == 3, s2, route)
    route_ref[...] = route


def _outproj(dil_outs, ob, sga, sgb, x2, mod3, g_ffn, wa, wb, wo, wr, br, batch, seq):
    n, dm = x2.shape
    tm = MOE_TILE
    tpb = seq // tm
    row = lambda i: (i, 0)
    const = lambda i: (0, 0)
    in_specs = []
    args = []
    for (o, l), d in zip(dil_outs, DILATIONS):
        spec = pl.BlockSpec((1, d, tm // d, DIL_GROUP_W), lambda i: (i // tpb, 0, i % tpb, 0))
        in_specs += [spec, spec]
        args += [o, l]
    in_specs += [pl.BlockSpec((tm, ob.shape[1]), row), pl.BlockSpec((tm, dm), row), pl.BlockSpec((tm, dm), row),
                 pl.BlockSpec((tm, dm), row),
                 pl.BlockSpec((1, 6, dm), lambda i: (i // tpb, 0, 0)),
                 pl.BlockSpec((1, dm), const),
                 pl.BlockSpec(wa.shape, const), pl.BlockSpec(wb.shape, const), pl.BlockSpec(wo.shape, const),
                 pl.BlockSpec(wr.shape, const), pl.BlockSpec(br.shape, const)]
    args += [ob, sga, sgb, x2, mod3, g_ffn, wa, wb, wo, wr, br]
    return pl.pallas_call(
        functools.partial(_outproj_kernel, tm=tm),
        grid=(n // tm,),
        in_specs=in_specs,
        out_specs=[pl.BlockSpec((tm, dm), row), pl.BlockSpec((tm, dm), row), pl.BlockSpec((tm, LANES), row),
                   pl.BlockSpec((1, 8, LANES), lambda i: (i, 0, 0))],
        out_shape=[jax.ShapeDtypeStruct((n, dm), F32), jax.ShapeDtypeStruct((n, dm), BF16),
                   jax.ShapeDtypeStruct((n, LANES), F32), jax.ShapeDtypeStruct((n // tm, 8, LANES), F32)],
        scratch_shapes=[pltpu.VMEM((2, tm, LANES), F32)] * 4,
        compiler_params=_cparams(("arbitrary",)),
        name="outproj",
    )(*args)


def _segment_copies(i, base_ref, nch_ref, make_copy):
    def expert_body(e, off):
        nch = nch_ref[i * N_EXPERTS + e]
        base = base_ref[i * N_EXPERTS + e]

        def chunk(c, carry):
            make_copy(pl.multiple_of(off + c * SEG_ROWS, SEG_ROWS),
                      pl.multiple_of(base + c * SEG_ROWS, SEG_ROWS)).start()
            return carry

        lax.fori_loop(0, nch, chunk, 0)
        return off + nch * SEG_ROWS

    total_rows = lax.fori_loop(0, N_EXPERTS, expert_body, 0)
    return total_rows // SEG_ROWS


def _wait_copies(count, copy):
    def wait_one(c, carry):
        copy.wait()
        return carry

    lax.fori_loop(0, count, wait_one, 0)


def _dispatch_kernel(base_ref, nch_ref, tstart_ref, tch_ref, nused_ref, h_ref, route_ref, xr_ref,
                     sorted_ref, zero_ref, sem, *, tm, nslots, nblk):
    i = pl.program_id(0)
    dm = h_ref.shape[1]
    route = route_ref[...]
    lane = lax.broadcasted_iota(I32, (tm, LANES), 1)
    pr = lax.broadcasted_iota(I32, (8, LANES), 0)
    pc = lax.broadcasted_iota(I32, (8, LANES), 1)
    pick = jnp.where(pc == pr + 2, 1.0, 0.0)
    slots_t = lax.dot_general(pick, route, (((1,), (1,)), ((), ())), preferred_element_type=F32,
                              precision=lax.Precision.HIGHEST)
    srow = lax.broadcasted_iota(I32, (nslots, tm), 0).astype(F32)
    p0 = jnp.where(srow == slots_t[0:1, :], 1.0, 0.0)
    p1 = jnp.where(srow == slots_t[1:2, :], 1.0, 0.0)
    sorted_ref[:, :dm] = jnp.dot((p0 + p1).astype(BF16), h_ref[...], preferred_element_type=F32).astype(BF16)

    def weight_cols(k):
        w = route[:, k:k + 1]
        hi = w.astype(BF16).astype(F32)
        return jnp.where(lane == 0, hi, jnp.where(lane == 1, w - hi, 0.0)).astype(BF16)

    aux = (jnp.dot(p0.astype(BF16), weight_cols(0), preferred_element_type=F32)
           + jnp.dot(p1.astype(BF16), weight_cols(1), preferred_element_type=F32))
    sorted_ref[:, dm:] = aux.astype(BF16)

    def seg_copy(src, dst):
        return pltpu.make_async_copy(sorted_ref.at[pl.ds(src, SEG_ROWS), :], xr_ref.at[pl.ds(dst, SEG_ROWS), :], sem)

    nchunks = _segment_copies(i, base_ref, nch_ref, seg_copy)
    _wait_copies(nchunks, seg_copy(0, 0))

    @pl.when(i == pl.num_programs(0) - 1)
    def _():
        zero_ref[...] = jnp.zeros_like(zero_ref)

        def zero_seg(dst):
            return pltpu.make_async_copy(zero_ref.at[pl.ds(0, SEG_ROWS), :], xr_ref.at[pl.ds(dst, SEG_ROWS), :], sem)

        def tail(e, count):
            def chunk(c, carry):
                zero_seg(pl.multiple_of(tstart_ref[e] + c * SEG_ROWS, SEG_ROWS)).start()
                return carry

            lax.fori_loop(0, tch_ref[e], chunk, 0)
            return count + tch_ref[e]

        _wait_copies(lax.fori_loop(0, N_EXPERTS, tail, 0), zero_seg(0))

        def zero_blk(b):
            return pltpu.make_async_copy(zero_ref, xr_ref.at[pl.ds(pl.multiple_of(b * ROW_BLOCK, ROW_BLOCK), ROW_BLOCK), :], sem)

        def blk(b, carry):
            zero_blk(b).start()
            return carry

        lax.fori_loop(nused_ref[0], nblk, blk, 0)
        _wait_copies(nblk - nused_ref[0], zero_blk(0))


def _dispatch(tables, h2, route, nblk, tm, nslots):
    n, dm = h2.shape
    rows = nblk * ROW_BLOCK
    width = dm + LANES
    grid_spec = pltpu.PrefetchScalarGridSpec(
        num_scalar_prefetch=5,
        grid=(n // tm,),
        in_specs=[pl.BlockSpec((tm, dm), lambda i, *_: (i, 0)),
                  pl.BlockSpec((tm, LANES), lambda i, *_: (i, 0))],
        out_specs=pl.BlockSpec(memory_space=pl.ANY),
        scratch_shapes=[pltpu.VMEM((nslots, width), BF16), pltpu.VMEM((ROW_BLOCK, width), BF16),
                        pltpu.SemaphoreType.DMA(())],
    )
    return pl.pallas_call(
        functools.partial(_dispatch_kernel, tm=tm, nslots=nslots, nblk=nblk),
        grid_spec=grid_spec,
        out_shape=jax.ShapeDtypeStruct((rows, width), BF16),
        compiler_params=_cparams(("arbitrary",)),
        name="dispatch",
    )(*tables, h2, route)


def _experts_kernel(blk_e_ref, nused_ref, x_ref, wg_ref, wu_ref, wd_ref, y_ref, wgb_ref, wub_ref, wdb_ref):
    i = pl.program_id(0)
    used = i < nused_ref[0]
    prev = blk_e_ref[jnp.maximum(i - 1, 0)]
    changed = (i == 0) | (used & (blk_e_ref[i] != prev))

    @pl.when(changed)
    def _():
        wgb_ref[...] = wg_ref[0].astype(BF16)
        wub_ref[...] = wu_ref[0].astype(BF16)
        wdb_ref[...] = wd_ref[0].astype(BF16)

    @pl.when(used)
    def _():
        dm = y_ref.shape[1]
        xb = x_ref[:, :dm]
        aux = x_ref[:, dm:].astype(F32)
        w = aux[:, 0:1] + aux[:, 1:2]
        g = jnp.dot(xb, wgb_ref[...], preferred_element_type=F32)
        u = jnp.dot(xb, wub_ref[...], preferred_element_type=F32)
        a = (g * jax.nn.sigmoid(g)) * u
        y = jnp.dot(a.astype(BF16), wdb_ref[...], preferred_element_type=F32)
        y_ref[...] = (y * w).astype(BF16)

    @pl.when(jnp.logical_not(used))
    def _():
        y_ref[...] = jnp.zeros_like(y_ref)


def _experts(blk_e, nused, x_rows, w_gate, w_up, w_down):
    rows, width = x_rows.shape
    dm, de = w_gate.shape[1:]
    nblk = rows // ROW_BLOCK
    xmap = lambda i, be, nu: (jnp.minimum(i, nu[0] - 1), 0)
    wmap = lambda i, be, nu: (be[jnp.minimum(i, nu[0] - 1)], 0, 0)
    grid_spec = pltpu.PrefetchScalarGridSpec(
        num_scalar_prefetch=2,
        grid=(nblk,),
        in_specs=[pl.BlockSpec((ROW_BLOCK, width), xmap),
                  pl.BlockSpec((1, dm, de), wmap),
                  pl.BlockSpec((1, dm, de), wmap),
                  pl.BlockSpec((1, de, dm), wmap)],
        out_specs=pl.BlockSpec((ROW_BLOCK, dm), lambda i, be, nu: (i, 0)),
        scratch_shapes=[pltpu.VMEM((dm, de), BF16), pltpu.VMEM((dm, de), BF16), pltpu.VMEM((de, dm), BF16)],
    )
    return pl.pallas_call(
        _experts_kernel,
        grid_spec=grid_spec,
        out_shape=jax.ShapeDtypeStruct((rows, dm), BF16),
        compiler_params=_cparams(("arbitrary",)),
        name="experts",
    )(blk_e, nused, x_rows, w_gate, w_up, w_down)


def _combine_kernel(base_ref, nch_ref, route_ref, x1_ref, mod_ref, g_ref, y_ref, o_ref, ys_ref, sem, *, tm, nslots):
    i = pl.program_id(0)

    @pl.when(i == 0)
    def _():
        ys_ref[...] = jnp.zeros_like(ys_ref)

    def seg_copy(dst, src):
        return pltpu.make_async_copy(y_ref.at[pl.ds(src, SEG_ROWS), :], ys_ref.at[pl.ds(dst, SEG_ROWS), :], sem)

    nchunks = _segment_copies(i, base_ref, nch_ref, seg_copy)
    _wait_copies(nchunks, seg_copy(0, 0))
    route = route_ref[...]
    scol = lax.broadcasted_iota(I32, (tm, nslots), 1).astype(F32)
    pick = jnp.where((scol == route[:, 2:3]) | (scol == route[:, 3:4]), 1.0, 0.0).astype(BF16)
    moe = jnp.dot(pick, ys_ref[...], preferred_element_type=F32)
    x = x1_ref[...] + mod_ref[0, 5:6, :] * moe
    ms = jnp.mean(x * x, axis=-1, keepdims=True)
    o_ref[...] = (x * lax.rsqrt(ms + RMS_EPS)) * g_ref[...]


def _combine(base, nch, route, x1, mod3, g_final, y_rows, seq, tm, nslots):
    n, dm = x1.shape
    tpb = seq // tm
    grid_spec = pltpu.PrefetchScalarGridSpec(
        num_scalar_prefetch=2,
        grid=(n // tm,),
        in_specs=[pl.BlockSpec((tm, LANES), lambda i, *_: (i, 0)),
                  pl.BlockSpec((tm, dm), lambda i, *_: (i, 0)),
                  pl.BlockSpec((1, 6, dm), lambda i, *_: (i // tpb, 0, 0)),
                  pl.BlockSpec((1, dm), lambda i, *_: (0, 0)),
                  pl.BlockSpec(memory_space=pl.ANY)],
        out_specs=pl.BlockSpec((tm, dm), lambda i, *_: (i, 0)),
        scratch_shapes=[pltpu.VMEM((nslots, dm), BF16), pltpu.SemaphoreType.DMA(())],
    )
    return pl.pallas_call(
        functools.partial(_combine_kernel, tm=tm, nslots=nslots),
        grid_spec=grid_spec,
        out_shape=jax.ShapeDtypeStruct((n, dm), F32),
        compiler_params=_cparams(("arbitrary",)),
        name="combine",
    )(base, nch, route, x1, mod3, g_final, y_rows)


def _rope_tables(positions):
    half = HEAD_DIM // 2
    inv_freq = ROPE_THETA ** (-jnp.arange(half, dtype=F32) * (2.0 / HEAD_DIM))
    ang = positions.astype(F32).reshape(-1, 1) * inv_freq
    cos, sin = jnp.cos(ang), jnp.sin(ang)
    cos_t = jnp.concatenate([cos, cos, cos, cos], axis=1)
    sin_t = jnp.concatenate([-sin, sin, -sin, sin], axis=1)
    return cos_t, sin_t


def kernel(x, c, positions, w_ada, b_ada, g_mix, w_in, sink_logits, w_branch_a, w_branch_b, w_out, g_ffn,
           w_group, b_group, w_route, b_route, w_expert_gate, w_expert_up, w_expert_down, g_final):
    batch, seq, dm = x.shape
    n = batch * seq
    assert w_ada.shape[0] == 1, "one layer"
    x2 = x.reshape(n, dm)

    c8 = jnp.pad(c, ((0, 8 - batch), (0, 0)))
    mod = _ada(c8, w_ada[0], b_ada[0].reshape(1, -1))
    mod3 = mod[:batch].reshape(batch, 6, dm)

    cos_t, sin_t = _rope_tables(positions)
    outs = _inproj(x2, mod3, g_mix[0].reshape(1, dm), cos_t, sin_t, w_in[0].astype(BF16), batch, seq)
    qkv = outs[:9]
    qb, kb, vb, sga, sgb = outs[9:]

    dil_outs = [_dil_attention(qkv[3 * g], qkv[3 * g + 1], qkv[3 * g + 2]) for g in range(len(DILATIONS))]
    ob = _swa_attention(sink_logits[0], qb.reshape(batch, seq, -1), kb.reshape(batch, seq, -1),
                        vb.reshape(batch, seq, -1)).reshape(n, -1)

    wr = jnp.zeros((dm, LANES), F32).at[:, :N_GROUPS].set(w_group[0]).at[:, N_GROUPS:N_GROUPS + N_EXPERTS].set(w_route[0])
    br = jnp.zeros((1, LANES), F32).at[0, :N_GROUPS].set(b_group[0]).at[0, N_GROUPS:N_GROUPS + N_EXPERTS].set(b_route[0])
    x1, h2, route, seg_lens = _outproj(dil_outs, ob, sga, sgb, x2, mod3, g_ffn[0].reshape(1, dm),
                                       w_branch_a[0].astype(BF16), w_branch_b[0].astype(BF16), w_out[0].astype(BF16),
                                       wr, br, batch, seq)

    ntiles = n // MOE_TILE
    nslots = 2 * MOE_TILE + N_EXPERTS * SEG_ROWS
    nblk = -(-(2 * n + ntiles * N_EXPERTS * (SEG_ROWS - 1)) // ROW_BLOCK) + N_EXPERTS
    lens = seg_lens[:, 0, :N_EXPERTS].astype(I32)
    tot = jnp.sum(lens, axis=0)
    padded = (tot + ROW_BLOCK - 1) // ROW_BLOCK * ROW_BLOCK
    pend = jnp.cumsum(padded)
    pstart = pend - padded
    base = (pstart[None, :] + jnp.cumsum(lens, axis=0) - lens).reshape(-1)
    nch = (lens // SEG_ROWS).reshape(-1)
    blk_start = jnp.arange(nblk, dtype=I32) * ROW_BLOCK
    blk_e = jnp.minimum(jnp.sum((pend[None, :] <= blk_start[:, None]).astype(I32), axis=1), N_EXPERTS - 1)
    nused = pend[-1:] // ROW_BLOCK
    tables = (base, nch, pstart + tot, (padded - tot) // SEG_ROWS, nused)

    x_rows = _dispatch(tables, h2, route, nblk, MOE_TILE, nslots)
    y_rows = _experts(blk_e, nused, x_rows, w_expert_gate[0], w_expert_up[0], w_expert_down[0])
    out = _combine(base, nch, route, x1, mod3, g_final.reshape(1, dm), y_rows, seq, MOE_TILE, nslots)
    return out.reshape(batch, seq, dm)
```

```python
import functools

import jax
import jax.numpy as jnp
from jax import lax
from jax.experimental import pallas as pl
from jax.experimental.pallas import tpu as pltpu

F32 = jnp.float32
BF16 = jnp.bfloat16
I32 = jnp.int32

HEAD_DIM = 64
ROPE_THETA = 10000.0
RMS_EPS = 1e-6
NEG_INF = -1e30
DILATIONS = (1, 4, 16)
DIL_HALF_WINDOW = 64
DIL_GROUP_W = 256
SWA_WINDOW = 128
N_GROUPS = 4
EXPERTS_PER_GROUP = 8
N_EXPERTS = 32
LANES = 128
ROW_BLOCK = 256
SEG_ROWS = 16
MOE_TILE = 512
VMEM_LIMIT = 56 * 1024 * 1024


def _cparams(sem):
    return pltpu.CompilerParams(dimension_semantics=sem, vmem_limit_bytes=VMEM_LIMIT)


def _ada_kernel(c_ref, w_ref, b_ref, o_ref):
    c = c_ref[...]
    cs = c * jax.nn.sigmoid(c)
    o_ref[...] = jnp.dot(cs.astype(BF16), w_ref[...].astype(BF16), preferred_element_type=F32) + b_ref[...]


def _ada(c8, w_ada, b_ada):
    d, n = w_ada.shape
    tn = 1536
    return pl.pallas_call(
        _ada_kernel,
        grid=(n // tn,),
        in_specs=[pl.BlockSpec((8, d), lambda j: (0, 0)),
                  pl.BlockSpec((d, tn), lambda j: (0, j)),
                  pl.BlockSpec((1, tn), lambda j: (0, j))],
        out_specs=pl.BlockSpec((8, tn), lambda j: (0, j)),
        out_shape=jax.ShapeDtypeStruct((8, n), F32),
        compiler_params=_cparams(("arbitrary",)),
        name="ada",
    )(c8, w_ada, b_ada)


def _rms_mod(x, g, shift, scale):
    ms = jnp.mean(x * x, axis=-1, keepdims=True)
    return (x * lax.rsqrt(ms + RMS_EPS)) * g * (1.0 + scale) + shift


def _inproj_kernel(x_ref, mod_ref, g_ref, cos_ref, sin_ref, w_ref,
                   q0_ref, k0_ref, v0_ref, q1_ref, k1_ref, v1_ref, q2_ref, k2_ref, v2_ref,
                   qb_ref, kb_ref, vb_ref, sga_ref, sgb_ref, stg_ref, *, tm):
    h = _rms_mod(x_ref[...], g_ref[...], mod_ref[0, 0:1, :], mod_ref[0, 1:2, :])
    hb = h.astype(BF16)
    cos = cos_ref[...]
    sin = sin_ref[...]
    lane = lax.broadcasted_iota(I32, (tm, LANES), 1)
    first_half = (lane & 32) == 0
    low = lane < 64

    def proj(c0, width):
        return jnp.dot(hb, w_ref[:, c0:c0 + width], preferred_element_type=F32)

    def rope(t):
        rot = jnp.where(first_half, pltpu.roll(t, 96, 1), pltpu.roll(t, 32, 1))
        return t * cos + rot * sin

    def rope256(p):
        return jnp.concatenate([rope(p[:, :LANES]), rope(p[:, LANES:])], axis=1)

    def store_group(ref, val, d):
        if d == 1:
            ref[0, 0] = val.astype(BF16)
        else:
            for c in range(2):
                stg_ref[c] = val[:, c * LANES:(c + 1) * LANES]
            for r in range(d):
                for c in range(2):
                    ref[0, r, :, c * LANES:(c + 1) * LANES] = (
                        stg_ref[c, pl.ds(r, tm // d, stride=d), :].astype(BF16))

    q_refs = (q0_ref, q1_ref, q2_ref)
    k_refs = (k0_ref, k1_ref, k2_ref)
    v_refs = (v0_ref, v1_ref, v2_ref)
    for g, d in enumerate(DILATIONS):
        store_group(q_refs[g], rope256(proj(g * 256, 256)) * 0.125, d)
        store_group(k_refs[g], rope256(proj(768 + g * 256, 256)), d)
        store_group(v_refs[g], proj(1536 + g * 256, 256), d)
    for j in range(2):
        qb_ref[:, j * 256:(j + 1) * 256] = (rope256(proj(2304 + j * 256, 256)) * 0.125).astype(BF16)
    kv = proj(2816, 256)
    kb = rope(kv[:, :LANES])
    vb = kv[:, LANES:]
    kb_sw = pltpu.roll(kb, 64, 1)
    vb_sw = pltpu.roll(vb, 64, 1)
    kb_ref[:, :LANES] = jnp.where(low, kb, kb_sw).astype(BF16)
    kb_ref[:, LANES:] = jnp.where(low, kb_sw, kb).astype(BF16)
    vb_ref[:, :LANES] = jnp.where(low, vb, vb_sw).astype(BF16)
    vb_ref[:, LANES:] = jnp.where(low, vb_sw, vb).astype(BF16)
    for j in range(4):
        sga_ref[:, j * 256:(j + 1) * 256] = jax.nn.sigmoid(proj(3072 + j * 256, 256)).astype(BF16)
        sgb_ref[:, j * 256:(j + 1) * 256] = jax.nn.sigmoid(proj(4096 + j * 256, 256)).astype(BF16)


def _inproj(x2, mod3, g_mix, cos_t, sin_t, w_in_bf, batch, seq):
    n, dm = x2.shape
    tm = 512
    tpb = seq // tm
    grid = (n // tm,)
    row = lambda i: (i, 0)
    strided_specs, strided_shapes = [], []
    for d in DILATIONS:
        for _ in range(3):
            strided_specs.append(pl.BlockSpec((1, d, tm // d, DIL_GROUP_W), lambda i: (i // tpb, 0, i % tpb, 0)))
            strided_shapes.append(jax.ShapeDtypeStruct((batch, d, seq // d, DIL_GROUP_W), BF16))
    out_specs = strided_specs + [
        pl.BlockSpec((tm, 512), row), pl.BlockSpec((tm, 256), row), pl.BlockSpec((tm, 256), row),
        pl.BlockSpec((tm, dm), row), pl.BlockSpec((tm, dm), row)]
    out_shapes = strided_shapes + [
        jax.ShapeDtypeStruct((n, 512), BF16), jax.ShapeDtypeStruct((n, 256), BF16),
        jax.ShapeDtypeStruct((n, 256), BF16), jax.ShapeDtypeStruct((n, dm), BF16),
        jax.ShapeDtypeStruct((n, dm), BF16)]
    return pl.pallas_call(
        functools.partial(_inproj_kernel, tm=tm),
        grid=grid,
        in_specs=[pl.BlockSpec((tm, dm), row),
                  pl.BlockSpec((1, 6, dm), lambda i: (i // tpb, 0, 0)),
                  pl.BlockSpec((1, dm), lambda i: (0, 0)),
                  pl.BlockSpec((tm, LANES), row),
                  pl.BlockSpec((tm, LANES), row),
                  pl.BlockSpec(w_in_bf.shape, lambda i: (0, 0))],
        out_specs=out_specs,
        out_shape=out_shapes,
        scratch_shapes=[pltpu.VMEM((2, tm, LANES), F32)],
        compiler_params=_cparams(("arbitrary",)),
        name="inproj",
    )(x2, mod3, g_mix, cos_t, sin_t, w_in_bf)


def _split_heads(q2, low):
    zero = jnp.zeros_like(q2)
    return jnp.concatenate([jnp.where(low, q2, zero), jnp.where(low, zero, q2)], axis=0)


def _band_softmax(qst, k2, v2, mask, sinks):
    s = lax.dot_general(qst, k2, (((1,), (1,)), ((), ())), preferred_element_type=F32)
    s = jnp.where(mask, s, NEG_INF)
    rows, tk = s.shape
    m = jnp.max(s, axis=-1, keepdims=True)
    if sinks is not None:
        seg = rows // len(sinks)
        m = jnp.concatenate([jnp.maximum(m[h * seg:(h + 1) * seg], sk) for h, sk in enumerate(sinks)], axis=0)
    m = jnp.broadcast_to(m, (rows, LANES))
    e = jnp.concatenate([jnp.exp(s[:, c * LANES:(c + 1) * LANES] - m) for c in range(tk // LANES)], axis=1)
    v_ones = jnp.concatenate([v2, jnp.ones((tk, LANES), BF16)], axis=1)
    od = jnp.dot(e.astype(BF16), v_ones, preferred_element_type=F32)
    o, den = od[:, :LANES], od[:, LANES:]
    if sinks is not None:
        den = jnp.concatenate([den[h * seg:(h + 1) * seg] + jnp.exp(sk - m[h * seg:(h + 1) * seg])
                               for h, sk in enumerate(sinks)], axis=0)
    return o / den, m, den


def _band_mask(qs, ks, nstack, tq, tk, window):
    row = lax.broadcasted_iota(I32, (nstack * tq, tk), 0) & (tq - 1)
    col = lax.broadcasted_iota(I32, (nstack * tq, tk), 1)
    return jnp.abs((ks + col) - (qs + row)) <= window


def _dil_kernel(q_ref, k_ref, v_ref, o_ref, l_ref, *, length, tq, tk):
    low = lax.broadcasted_iota(I32, (tq, LANES), 1) < 64

    nq = length // tq

    def body(j, carry):
        r = j // nq
        qs = pl.multiple_of((j % nq) * tq, tq)
        ks = pl.multiple_of(jnp.clip(qs - DIL_HALF_WINDOW, 0, length - tk), DIL_HALF_WINDOW)
        mask = _band_mask(qs, ks, 2, tq, tk, DIL_HALF_WINDOW)
        for c in range(DIL_GROUP_W // LANES):
            cs = slice(c * LANES, (c + 1) * LANES)
            qst = _split_heads(q_ref[r, pl.ds(qs, tq), cs], low)
            o, m, den = _band_softmax(qst, k_ref[r, pl.ds(ks, tk), cs], v_ref[r, pl.ds(ks, tk), cs], mask, None)
            lse = m + jnp.log(den)
            o_ref[r, pl.ds(qs, tq), cs] = jnp.where(low, o[:tq], o[tq:]).astype(BF16)
            l_ref[r, pl.ds(qs, tq), cs] = jnp.where(low, lse[:tq], lse[tq:])
        return carry

    lax.fori_loop(0, q_ref.shape[0] * nq, body, 0, unroll=4)


def _dil_attention(q, k, v):
    batch, d, length, w = q.shape
    tq, tk = 128, 256
    spec = pl.BlockSpec((None, d, length, w), lambda b: (b, 0, 0, 0))
    return pl.pallas_call(
        functools.partial(_dil_kernel, length=length, tq=tq, tk=tk),
        grid=(batch,),
        in_specs=[spec, spec, spec],
        out_specs=[spec, spec],
        out_shape=[jax.ShapeDtypeStruct(q.shape, BF16), jax.ShapeDtypeStruct(q.shape, F32)],
        compiler_params=_cparams(("arbitrary",)),
        name=f"dil{d}",
    )(q, k, v)


def _swa_kernel(sink_ref, q_ref, k_ref, v_ref, o_ref, *, length, tq, tk):
    low = lax.broadcasted_iota(I32, (tq, LANES), 1) < 64
    nblk = q_ref.shape[1] // LANES

    def body(j, carry):
        qs = pl.multiple_of(j * tq, tq)
        ks = pl.multiple_of(jnp.clip(qs - SWA_WINDOW, 0, length - tk), SWA_WINDOW)
        mask = _band_mask(qs, ks, 2, tq, tk, SWA_WINDOW)
        for b in range(nblk):
            cs = slice((b // 2) * LANES, (b // 2 + 1) * LANES)
            bs = slice(b * LANES, (b + 1) * LANES)
            qst = _split_heads(q_ref[pl.ds(qs, tq), bs], low)
            sinks = (sink_ref[2 * b], sink_ref[2 * b + 1])
            o, _, _ = _band_softmax(qst, k_ref[pl.ds(ks, tk), cs], v_ref[pl.ds(ks, tk), cs], mask, sinks)
            o_ref[pl.ds(qs, tq), bs] = jnp.where(low, o[:tq], o[tq:]).astype(BF16)
        return carry

    lax.fori_loop(0, length // tq, body, 0, unroll=4)


def _swa_attention(sink, q, k, v):
    batch, length, qw = q.shape
    tq, tk = 128, 384
    return pl.pallas_call(
        functools.partial(_swa_kernel, length=length, tq=tq, tk=tk),
        grid=(batch,),
        in_specs=[pl.BlockSpec(memory_space=pltpu.SMEM),
                  pl.BlockSpec((None, length, qw), lambda b: (b, 0, 0)),
                  pl.BlockSpec((None, length, k.shape[2]), lambda b: (b, 0, 0)),
                  pl.BlockSpec((None, length, v.shape[2]), lambda b: (b, 0, 0))],
        out_specs=pl.BlockSpec((None, length, qw), lambda b: (b, 0, 0)),
        out_shape=jax.ShapeDtypeStruct(q.shape, BF16),
        compiler_params=_cparams(("arbitrary",)),
        name="swa",
    )(sink, q, k, v)


def _outproj_kernel(o0_ref, l0_ref, o1_ref, l1_ref, o2_ref, l2_ref, ob_ref, sga_ref, sgb_ref, x_ref,
                    mod_ref, g_ref, wa_ref, wb_ref, wo_ref, wr_ref, br_ref,
                    x1_ref, h2_ref, route_ref, len_ref,
                    so1_ref, sl1_ref, so2_ref, sl2_ref, *, tm):
    for (o_ref, l_ref, so_ref, sl_ref, d) in ((o1_ref, l1_ref, so1_ref, sl1_ref, DILATIONS[1]),
                                              (o2_ref, l2_ref, so2_ref, sl2_ref, DILATIONS[2])):
        for r in range(d):
            for c in range(2):
                cs = slice(c * LANES, (c + 1) * LANES)
                so_ref[c, pl.ds(r, tm // d, stride=d), :] = o_ref[0, r, :, cs].astype(F32)
                sl_ref[c, pl.ds(r, tm // d, stride=d), :] = l_ref[0, r, :, cs]
    o0 = o0_ref[0, 0].astype(F32)
    l0 = l0_ref[0, 0]
    both = lambda ref: jnp.concatenate([ref[0], ref[1]], axis=1)
    o1, l1, o2, l2 = both(so1_ref), both(sl1_ref), both(so2_ref), both(sl2_ref)
    mx = jnp.maximum(jnp.maximum(l0, l1), l2)
    w0, w1, w2 = jnp.exp(l0 - mx), jnp.exp(l1 - mx), jnp.exp(l2 - mx)
    o_a = (w0 * o0 + w1 * o1 + w2 * o2) / (w0 + w1 + w2)
    y_a = jnp.dot(o_a.astype(BF16), wa_ref[...], preferred_element_type=F32)
    y_b = jnp.dot(ob_ref[...], wb_ref[...], preferred_element_type=F32)
    merged = sga_ref[...].astype(F32) * y_a + sgb_ref[...].astype(F32) * y_b
    mix = jnp.dot(merged.astype(BF16), wo_ref[...], preferred_element_type=F32)
    x1 = x_ref[...] + mod_ref[0, 2:3, :] * mix
    x1_ref[...] = x1
    h2 = _rms_mod(x1, g_ref[...], mod_ref[0, 3:4, :], mod_ref[0, 4:5, :])
    h2_ref[...] = h2.astype(BF16)

    logits = jnp.dot(h2, wr_ref[...], preferred_element_type=F32, precision=lax.Precision.HIGHEST) + br_ref[...]
    lane = lax.broadcasted_iota(I32, (tm, LANES), 1).astype(F32)
    big = 1e9
    is_g = lane < N_GROUPS
    gl = jnp.where(is_g, logits, NEG_INF)
    gmax = jnp.max(gl, axis=-1, keepdims=True)
    gsel = jnp.min(jnp.where(is_g & (gl == gmax), lane, big), axis=-1, keepdims=True)
    gw = 1.0 / jnp.sum(jnp.where(is_g, jnp.exp(gl - gmax), 0.0), axis=-1, keepdims=True)
    e_lo = N_GROUPS + gsel * EXPERTS_PER_GROUP
    in_grp = (lane >= e_lo) & (lane < e_lo + EXPERTS_PER_GROUP)
    el = jnp.where(in_grp, logits, NEG_INF)
    m1 = jnp.max(el, axis=-1, keepdims=True)
    i1 = jnp.min(jnp.where(in_grp & (el == m1), lane, big), axis=-1, keepdims=True)
    el2 = jnp.where(lane == i1, NEG_INF, el)
    m2 = jnp.max(el2, axis=-1, keepdims=True)
    i2 = jnp.min(jnp.where(in_grp & (lane != i1) & (el2 == m2), lane, big), axis=-1, keepdims=True)
    t = jnp.exp(m2 - m1)
    tw1 = gw / (1.0 + t)
    tw2 = gw * t / (1.0 + t)
    e1 = i1 - N_GROUPS
    e2 = i2 - N_GROUPS

    onehot = jnp.where((lane == e1) | (lane == e2), 1.0, 0.0)
    rr = lax.broadcasted_iota(I32, (tm, tm), 0)
    cc = lax.broadcasted_iota(I32, (tm, tm), 1)
    tri = jnp.where(rr > cc, 1.0, 0.0).astype(BF16)
    prefix = jnp.dot(tri, onehot.astype(BF16), preferred_element_type=F32)
    cnt = jnp.sum(onehot, axis=0, keepdims=True)
    seg_len = jnp.ceil(cnt * (1.0 / SEG_ROWS)) * SEG_ROWS
    ur = lax.broadcasted_iota(I32, (LANES, LANES), 0)
    uc = lax.broadcasted_iota(I32, (LANES, LANES), 1)
    upper = jnp.where(ur < uc, 1.0, 0.0).astype(BF16)
    seg_off = jnp.dot(jnp.broadcast_to(seg_len, (8, LANES)).astype(BF16), upper, preferred_element_type=F32)[0:1, :]
    slot_map = seg_off + prefix
    s1 = jnp.sum(jnp.where(lane == e1, slot_map, 0.0), axis=-1, keepdims=True)
    s2 = jnp.sum(jnp.where(lane == e2, slot_map, 0.0), axis=-1, keepdims=True)
    len_ref[0] = jnp.broadcast_to(seg_len, (8, LANES))
    route = jnp.where(lane == 0, tw1, 0.0)
    route = jnp.where(lane == 1, tw2, route)
    route = jnp.where(lane == 2, s1, route)
    route = jnp.where(lane == 3, s2, route)
    route_ref[...] = route


def _outproj(dil_outs, ob, sga, sgb, x2, mod3, g_ffn, wa, wb, wo, wr, br, batch, seq):
    n, dm = x2.shape
    tm = MOE_TILE
    tpb = seq // tm
    row = lambda i: (i, 0)
    const = lambda i: (0, 0)
    in_specs = []
    args = []
    for (o, l), d in zip(dil_outs, DILATIONS):
        spec = pl.BlockSpec((1, d, tm // d, DIL_GROUP_W), lambda i: (i // tpb, 0, i % tpb, 0))
        in_specs += [spec, spec]
        args += [o, l]
    in_specs += [pl.BlockSpec((tm, ob.shape[1]), row), pl.BlockSpec((tm, dm), row), pl.BlockSpec((tm, dm), row),
                 pl.BlockSpec((tm, dm), row),
                 pl.BlockSpec((1, 6, dm), lambda i: (i // tpb, 0, 0)),
                 pl.BlockSpec((1, dm), const),
                 pl.BlockSpec(wa.shape, const), pl.BlockSpec(wb.shape, const), pl.BlockSpec(wo.shape, const),
                 pl.BlockSpec(wr.shape, const), pl.BlockSpec(br.shape, const)]
    args += [ob, sga, sgb, x2, mod3, g_ffn, wa, wb, wo, wr, br]
    return pl.pallas_call(
        functools.partial(_outproj_kernel, tm=tm),
        grid=(n // tm,),
        in_specs=in_specs,
        out_specs=[pl.BlockSpec((tm, dm), row), pl.BlockSpec((tm, dm), row), pl.BlockSpec((tm, LANES), row),
                   pl.BlockSpec((1, 8, LANES), lambda i: (i, 0, 0))],
        out_shape=[jax.ShapeDtypeStruct((n, dm), F32), jax.ShapeDtypeStruct((n, dm), BF16),
                   jax.ShapeDtypeStruct((n, LANES), F32), jax.ShapeDtypeStruct((n // tm, 8, LANES), F32)],
        scratch_shapes=[pltpu.VMEM((2, tm, LANES), F32)] * 4,
        compiler_params=_cparams(("arbitrary",)),
        name="outproj",
    )(*args)


def _segment_copies(i, base_ref, nch_ref, make_copy):
    def expert_body(e, off):
        nch = nch_ref[i * N_EXPERTS + e]
        base = base_ref[i * N_EXPERTS + e]

        def chunk(c, carry):
            make_copy(pl.multiple_of(off + c * SEG_ROWS, SEG_ROWS),
                      pl.multiple_of(base + c * SEG_ROWS, SEG_ROWS)).start()
            return carry

        lax.fori_loop(0, nch, chunk, 0)
        return off + nch * SEG_ROWS

    total_rows = lax.fori_loop(0, N_EXPERTS, expert_body, 0)
    return total_rows // SEG_ROWS


def _wait_copies(count, copy):
    def wait_one(c, carry):
        copy.wait()
        return carry

    lax.fori_loop(0, count, wait_one, 0)


def _dispatch_kernel(base_ref, nch_ref, tstart_ref, tch_ref, nused_ref, h_ref, route_ref, xr_ref,
                     sorted_ref, zero_ref, sem, *, tm, nslots, nblk):
    i = pl.program_id(0)
    dm = h_ref.shape[1]
    route = route_ref[...]
    lane = lax.broadcasted_iota(I32, (tm, LANES), 1)
    pr = lax.broadcasted_iota(I32, (8, LANES), 0)
    pc = lax.broadcasted_iota(I32, (8, LANES), 1)
    pick = jnp.where(pc == pr + 2, 1.0, 0.0)
    slots_t = lax.dot_general(pick, route, (((1,), (1,)), ((), ())), preferred_element_type=F32,
                              precision=lax.Precision.HIGHEST)
    srow = lax.broadcasted_iota(I32, (nslots, tm), 0).astype(F32)
    p0 = jnp.where(srow == slots_t[0:1, :], 1.0, 0.0)
    p1 = jnp.where(srow == slots_t[1:2, :], 1.0, 0.0)
    sorted_ref[:, :dm] = jnp.dot((p0 + p1).astype(BF16), h_ref[...], preferred_element_type=F32).astype(BF16)

    def weight_cols(k):
        w = route[:, k:k + 1]
        hi = w.astype(BF16).astype(F32)
        return jnp.where(lane == 0, hi, jnp.where(lane == 1, w - hi, 0.0)).astype(BF16)

    aux = (jnp.dot(p0.astype(BF16), weight_cols(0), preferred_element_type=F32)
           + jnp.dot(p1.astype(BF16), weight_cols(1), preferred_element_type=F32))
    sorted_ref[:, dm:] = aux.astype(BF16)

    def seg_copy(src, dst):
        return pltpu.make_async_copy(sorted_ref.at[pl.ds(src, SEG_ROWS), :], xr_ref.at[pl.ds(dst, SEG_ROWS), :], sem)

    nchunks = _segment_copies(i, base_ref, nch_ref, seg_copy)
    _wait_copies(nchunks, seg_copy(0, 0))

    @pl.when(i == pl.num_programs(0) - 1)
    def _():
        zero_ref[...] = jnp.zeros_like(zero_ref)

        def zero_seg(dst):
            return pltpu.make_async_copy(zero_ref.at[pl.ds(0, SEG_ROWS), :], xr_ref.at[pl.ds(dst, SEG_ROWS), :], sem)

        def tail(e, count):
            def chunk(c, carry):
                zero_seg(pl.multiple_of(tstart_ref[e] + c * SEG_ROWS, SEG_ROWS)).start()
                return carry

            lax.fori_loop(0, tch_ref[e], chunk, 0)
            return count + tch_ref[e]

        _wait_copies(lax.fori_loop(0, N_EXPERTS, tail, 0), zero_seg(0))

        def zero_blk(b):
            return pltpu.make_async_copy(zero_ref, xr_ref.at[pl.ds(pl.multiple_of(b * ROW_BLOCK, ROW_BLOCK), ROW_BLOCK), :], sem)

        def blk(b, carry):
            zero_blk(b).start()
            return carry

        lax.fori_loop(nused_ref[0], nblk, blk, 0)
        _wait_copies(nblk - nused_ref[0], zero_blk(0))


def _dispatch(tables, h2, route, nblk, tm, nslots):
    n, dm = h2.shape
    rows = nblk * ROW_BLOCK
    width = dm + LANES
    grid_spec = pltpu.PrefetchScalarGridSpec(
        num_scalar_prefetch=5,
        grid=(n // tm,),
        in_specs=[pl.BlockSpec((tm, dm), lambda i, *_: (i, 0)),
                  pl.BlockSpec((tm, LANES), lambda i, *_: (i, 0))],
        out_specs=pl.BlockSpec(memory_space=pl.ANY),
        scratch_shapes=[pltpu.VMEM((nslots, width), BF16), pltpu.VMEM((ROW_BLOCK, width), BF16),
                        pltpu.SemaphoreType.DMA(())],
    )
    return pl.pallas_call(
        functools.partial(_dispatch_kernel, tm=tm, nslots=nslots, nblk=nblk),
        grid_spec=grid_spec,
        out_shape=jax.ShapeDtypeStruct((rows, width), BF16),
        compiler_params=_cparams(("arbitrary",)),
        name="dispatch",
    )(*tables, h2, route)


def _experts_kernel(blk_e_ref, nused_ref, x_ref, wg_ref, wu_ref, wd_ref, y_ref, wgb_ref, wub_ref, wdb_ref):
    i = pl.program_id(0)
    used = i < nused_ref[0]
    prev = blk_e_ref[jnp.maximum(i - 1, 0)]
    changed = (i == 0) | (used & (blk_e_ref[i] != prev))

    @pl.when(changed)
    def _():
        wgb_ref[...] = wg_ref[0].astype(BF16)
        wub_ref[...] = wu_ref[0].astype(BF16)
        wdb_ref[...] = wd_ref[0].astype(BF16)

    @pl.when(used)
    def _():
        dm = y_ref.shape[1]
        xb = x_ref[:, :dm]
        aux = x_ref[:, dm:].astype(F32)
        w = aux[:, 0:1] + aux[:, 1:2]
        g = jnp.dot(xb, wgb_ref[...], preferred_element_type=F32)
        u = jnp.dot(xb, wub_ref[...], preferred_element_type=F32)
        a = (g * jax.nn.sigmoid(g)) * u
        y = jnp.dot(a.astype(BF16), wdb_ref[...], preferred_element_type=F32)
        y_ref[...] = (y * w).astype(BF16)

    @pl.when(jnp.logical_not(used))
    def _():
        y_ref[...] = jnp.zeros_like(y_ref)


def _experts(blk_e, nused, x_rows, w_gate, w_up, w_down):
    rows, width = x_rows.shape
    dm, de = w_gate.shape[1:]
    nblk = rows // ROW_BLOCK
    xmap = lambda i, be, nu: (jnp.minimum(i, nu[0] - 1), 0)
    wmap = lambda i, be, nu: (be[jnp.minimum(i, nu[0] - 1)], 0, 0)
    grid_spec = pltpu.PrefetchScalarGridSpec(
        num_scalar_prefetch=2,
        grid=(nblk,),
        in_specs=[pl.BlockSpec((ROW_BLOCK, width), xmap),
                  pl.BlockSpec((1, dm, de), wmap),
                  pl.BlockSpec((1, dm, de), wmap),
                  pl.BlockSpec((1, de, dm), wmap)],
        out_specs=pl.BlockSpec((ROW_BLOCK, dm), lambda i, be, nu: (i, 0)),
        scratch_shapes=[pltpu.VMEM((dm, de), BF16), pltpu.VMEM((dm, de), BF16), pltpu.VMEM((de, dm), BF16)],
    )
    return pl.pallas_call(
        _experts_kernel,
        grid_spec=grid_spec,
        out_shape=jax.ShapeDtypeStruct((rows, dm), BF16),
        compiler_params=_cparams(("arbitrary",)),
        name="experts",
    )(blk_e, nused, x_rows, w_gate, w_up, w_down)


def _combine_kernel(base_ref, nch_ref, route_ref, x1_ref, mod_ref, g_ref, y_ref, o_ref, ys_ref, sem, *, tm, nslots):
    i = pl.program_id(0)

    @pl.when(i == 0)
    def _():
        ys_ref[...] = jnp.zeros_like(ys_ref)

    def seg_copy(dst, src):
        return pltpu.make_async_copy(y_ref.at[pl.ds(src, SEG_ROWS), :], ys_ref.at[pl.ds(dst, SEG_ROWS), :], sem)

    nchunks = _segment_copies(i, base_ref, nch_ref, seg_copy)
    _wait_copies(nchunks, seg_copy(0, 0))
    route = route_ref[...]
    scol = lax.broadcasted_iota(I32, (tm, nslots), 1).astype(F32)
    pick = jnp.where((scol == route[:, 2:3]) | (scol == route[:, 3:4]), 1.0, 0.0).astype(BF16)
    moe = jnp.dot(pick, ys_ref[...], preferred_element_type=F32)
    x = x1_ref[...] + mod_ref[0, 5:6, :] * moe
    ms = jnp.mean(x * x, axis=-1, keepdims=True)
    o_ref[...] = (x * lax.rsqrt(ms + RMS_EPS)) * g_ref[...]


def _combine(base, nch, route, x1, mod3, g_final, y_rows, seq, tm, nslots):
    n, dm = x1.shape
    tpb = seq // tm
    grid_spec = pltpu.PrefetchScalarGridSpec(
        num_scalar_prefetch=2,
        grid=(n // tm,),
        in_specs=[pl.BlockSpec((tm, LANES), lambda i, *_: (i, 0)),
                  pl.BlockSpec((tm, dm), lambda i, *_: (i, 0)),
                  pl.BlockSpec((1, 6, dm), lambda i, *_: (i // tpb, 0, 0)),
                  pl.BlockSpec((1, dm), lambda i, *_: (0, 0)),
                  pl.BlockSpec(memory_space=pl.ANY)],
        out_specs=pl.BlockSpec((tm, dm), lambda i, *_: (i, 0)),
        scratch_shapes=[pltpu.VMEM((nslots, dm), BF16), pltpu.SemaphoreType.DMA(())],
    )
    return pl.pallas_call(
        functools.partial(_combine_kernel, tm=tm, nslots=nslots),
        grid_spec=grid_spec,
        out_shape=jax.ShapeDtypeStruct((n, dm), F32),
        compiler_params=_cparams(("arbitrary",)),
        name="combine",
    )(base, nch, route, x1, mod3, g_final, y_rows)


def _rope_tables(positions):
    half = HEAD_DIM // 2
    inv_freq = ROPE_THETA ** (-jnp.arange(half, dtype=F32) * (2.0 / HEAD_DIM))
    ang = positions.astype(F32).reshape(-1, 1) * inv_freq
    cos, sin = jnp.cos(ang), jnp.sin(ang)
    cos_t = jnp.concatenate([cos, cos, cos, cos], axis=1)
    sin_t = jnp.concatenate([-sin, sin, -sin, sin], axis=1)
    return cos_t, sin_t


def kernel(x, c, positions, w_ada, b_ada, g_mix, w_in, sink_logits, w_branch_a, w_branch_b, w_out, g_ffn,
           w_group, b_group, w_route, b_route, w_expert_gate, w_expert_up, w_expert_down, g_final):
    batch, seq, dm = x.shape
    n = batch * seq
    assert w_ada.shape[0] == 1, "one layer"
    x2 = x.reshape(n, dm)

    c8 = jnp.pad(c, ((0, 8 - batch), (0, 0)))
    mod = _ada(c8, w_ada[0], b_ada[0].reshape(1, -1))
    mod3 = mod[:batch].reshape(batch, 6, dm)

    cos_t, sin_t = _rope_tables(positions)
    outs = _inproj(x2, mod3, g_mix[0].reshape(1, dm), cos_t, sin_t, w_in[0].astype(BF16), batch, seq)
    qkv = outs[:9]
    qb, kb, vb, sga, sgb = outs[9:]

    dil_outs = [_dil_attention(qkv[3 * g], qkv[3 * g + 1], qkv[3 * g + 2]) for g in range(len(DILATIONS))]
    ob = _swa_attention(sink_logits[0], qb.reshape(batch, seq, -1), kb.reshape(batch, seq, -1),
                        vb.reshape(batch, seq, -1)).reshape(n, -1)

    wr = jnp.zeros((dm, LANES), F32).at[:, :N_GROUPS].set(w_group[0]).at[:, N_GROUPS:N_GROUPS + N_EXPERTS].set(w_route[0])
    br = jnp.zeros((1, LANES), F32).at[0, :N_GROUPS].set(b_group[0]).at[0, N_GROUPS:N_GROUPS + N_EXPERTS].set(b_route[0])
    x1, h2, route, seg_lens = _outproj(dil_outs, ob, sga, sgb, x2, mod3, g_ffn[0].reshape(1, dm),
                                       w_branch_a[0].astype(BF16), w_branch_b[0].astype(BF16), w_out[0].astype(BF16),
                                       wr, br, batch, seq)

    ntiles = n // MOE_TILE
    nslots = 2 * MOE_TILE + N_EXPERTS * SEG_ROWS
    nblk = -(-(2 * n + ntiles * N_EXPERTS * (SEG_ROWS - 1)) // ROW_BLOCK) + N_EXPERTS
    lens = seg_lens[:, 0, :N_EXPERTS].astype(I32)
    tot = jnp.sum(lens, axis=0)
    padded = (tot + ROW_BLOCK - 1) // ROW_BLOCK * ROW_BLOCK
    pend = jnp.cumsum(padded)
    pstart = pend - padded
    base = (pstart[None, :] + jnp.cumsum(lens, axis=0) - lens).reshape(-1)
    nch = (lens // SEG_ROWS).reshape(-1)
    blk_start = jnp.arange(nblk, dtype=I32) * ROW_BLOCK
    blk_e = jnp.minimum(jnp.sum((pend[None, :] <= blk_start[:, None]).astype(I32), axis=1), N_EXPERTS - 1)
    nused = pend[-1:] // ROW_BLOCK
    tables = (base, nch, pstart + tot, (padded - tot) // SEG_ROWS, nused)

    x_rows = _dispatch(tables, h2, route, nblk, MOE_TILE, nslots)
    y_rows = _experts(blk_e, nused, x_rows, w_expert_gate[0], w_expert_up[0], w_expert_down[0])
    out = _combine(base, nch, route, x1, mod3, g_final.reshape(1, dm), y_rows, seq, MOE_TILE, nslots)
    return out.reshape(batch, seq, dm)
```

```python
import functools

import jax
import jax.numpy as jnp
from jax import lax
from jax.experimental import pallas as pl
from jax.experimental.pallas import tpu as pltpu

F32 = jnp.float32
BF16 = jnp.bfloat16
I32 = jnp.int32

HEAD_DIM = 64
ROPE_THETA = 10000.0
RMS_EPS = 1e-6
NEG_INF = -1e30
DILATIONS = (1, 4, 16)
DIL_HALF_WINDOW = 64
DIL_GROUP_W = 256
SWA_WINDOW = 128
N_GROUPS = 4
EXPERTS_PER_GROUP = 8
N_EXPERTS = 32
LANES = 128
ROW_BLOCK = 256
SEG_ROWS = 16
MOE_TILE = 512
VMEM_LIMIT = 56 * 1024 * 1024


def _cparams(sem):
    return pltpu.CompilerParams(dimension_semantics=sem, vmem_limit_bytes=VMEM_LIMIT)


def _ada_kernel(c_ref, w_ref, b_ref, o_ref):
    c = c_ref[...]
    cs = c * jax.nn.sigmoid(c)
    o_ref[...] = jnp.dot(cs.astype(BF16), w_ref[...].astype(BF16), preferred_element_type=F32) + b_ref[...]


def _ada(c8, w_ada, b_ada):
    d, n = w_ada.shape
    tn = 1536
    return pl.pallas_call(
        _ada_kernel,
        grid=(n // tn,),
        in_specs=[pl.BlockSpec((8, d), lambda j: (0, 0)),
                  pl.BlockSpec((d, tn), lambda j: (0, j)),
                  pl.BlockSpec((1, tn), lambda j: (0, j))],
        out_specs=pl.BlockSpec((8, tn), lambda j: (0, j)),
        out_shape=jax.ShapeDtypeStruct((8, n), F32),
        compiler_params=_cparams(("arbitrary",)),
        name="ada",
    )(c8, w_ada, b_ada)


def _rms_mod(x, g, shift, scale):
    ms = jnp.mean(x * x, axis=-1, keepdims=True)
    return (x * lax.rsqrt(ms + RMS_EPS)) * g * (1.0 + scale) + shift


def _inproj_kernel(x_ref, mod_ref, g_ref, cos_ref, sin_ref, w_ref,
                   q0_ref, k0_ref, v0_ref, q1_ref, k1_ref, v1_ref, q2_ref, k2_ref, v2_ref,
                   qb_ref, kb_ref, vb_ref, sga_ref, sgb_ref, stg_ref, *, tm):
    h = _rms_mod(x_ref[...], g_ref[...], mod_ref[0, 0:1, :], mod_ref[0, 1:2, :])
    hb = h.astype(BF16)
    cos = cos_ref[...]
    sin = sin_ref[...]
    lane = lax.broadcasted_iota(I32, (tm, LANES), 1)
    first_half = (lane & 32) == 0
    low = lane < 64

    def proj(c0, width):
        return jnp.dot(hb, w_ref[:, c0:c0 + width], preferred_element_type=F32)

    def rope(t):
        rot = jnp.where(first_half, pltpu.roll(t, 96, 1), pltpu.roll(t, 32, 1))
        return t * cos + rot * sin

    def rope256(p):
        return jnp.concatenate([rope(p[:, :LANES]), rope(p[:, LANES:])], axis=1)

    def store_group(ref, val, d):
        if d == 1:
            ref[0, 0] = val.astype(BF16)
        else:
            for c in range(2):
                stg_ref[c] = val[:, c * LANES:(c + 1) * LANES]
            for r in range(d):
                for c in range(2):
                    ref[0, r, :, c * LANES:(c + 1) * LANES] = (
                        stg_ref[c, pl.ds(r, tm // d, stride=d), :].astype(BF16))

    q_refs = (q0_ref, q1_ref, q2_ref)
    k_refs = (k0_ref, k1_ref, k2_ref)
    v_refs = (v0_ref, v1_ref, v2_ref)
    for g, d in enumerate(DILATIONS):
        store_group(q_refs[g], rope256(proj(g * 256, 256)) * 0.125, d)
        store_group(k_refs[g], rope256(proj(768 + g * 256, 256)), d)
        store_group(v_refs[g], proj(1536 + g * 256, 256), d)
    for j in range(2):
        qb_ref[:, j * 256:(j + 1) * 256] = (rope256(proj(2304 + j * 256, 256)) * 0.125).astype(BF16)
    kv = proj(2816, 256)
    kb = rope(kv[:, :LANES])
    vb = kv[:, LANES:]
    kb_sw = pltpu.roll(kb, 64, 1)
    vb_sw = pltpu.roll(vb, 64, 1)
    kb_ref[:, :LANES] = jnp.where(low, kb, kb_sw).astype(BF16)
    kb_ref[:, LANES:] = jnp.where(low, kb_sw, kb).astype(BF16)
    vb_ref[:, :LANES] = jnp.where(low, vb, vb_sw).astype(BF16)
    vb_ref[:, LANES:] = jnp.where(low, vb_sw, vb).astype(BF16)
    for j in range(4):
        sga_ref[:, j * 256:(j + 1) * 256] = jax.nn.sigmoid(proj(3072 + j * 256, 256)).astype(BF16)
        sgb_ref[:, j * 256:(j + 1) * 256] = jax.nn.sigmoid(proj(4096 + j * 256, 256)).astype(BF16)


def _inproj(x2, mod3, g_mix, cos_t, sin_t, w_in_bf, batch, seq):
    n, dm = x2.shape
    tm = 512
    tpb = seq // tm
    grid = (n // tm,)
    row = lambda i: (i, 0)
    strided_specs, strided_shapes = [], []
    for d in DILATIONS:
        for _ in range(3):
            strided_specs.append(pl.BlockSpec((1, d, tm // d, DIL_GROUP_W), lambda i: (i // tpb, 0, i % tpb, 0)))
            strided_shapes.append(jax.ShapeDtypeStruct((batch, d, seq // d, DIL_GROUP_W), BF16))
    out_specs = strided_specs + [
        pl.BlockSpec((tm, 512), row), pl.BlockSpec((tm, 256), row), pl.BlockSpec((tm, 256), row),
        pl.BlockSpec((tm, dm), row), pl.BlockSpec((tm, dm), row)]
    out_shapes = strided_shapes + [
        jax.ShapeDtypeStruct((n, 512), BF16), jax.ShapeDtypeStruct((n, 256), BF16),
        jax.ShapeDtypeStruct((n, 256), BF16), jax.ShapeDtypeStruct((n, dm), BF16),
        jax.ShapeDtypeStruct((n, dm), BF16)]
    return pl.pallas_call(
        functools.partial(_inproj_kernel, tm=tm),
        grid=grid,
        in_specs=[pl.BlockSpec((tm, dm), row),
                  pl.BlockSpec((1, 6, dm), lambda i: (i // tpb, 0, 0)),
                  pl.BlockSpec((1, dm), lambda i: (0, 0)),
                  pl.BlockSpec((tm, LANES), row),
                  pl.BlockSpec((tm, LANES), row),
                  pl.BlockSpec(w_in_bf.shape, lambda i: (0, 0))],
        out_specs=out_specs,
        out_shape=out_shapes,
        scratch_shapes=[pltpu.VMEM((2, tm, LANES), F32)],
        compiler_params=_cparams(("arbitrary",)),
        name="inproj",
    )(x2, mod3, g_mix, cos_t, sin_t, w_in_bf)


def _split_heads(q2, low):
    zero = jnp.zeros_like(q2)
    return jnp.concatenate([jnp.where(low, q2, zero), jnp.where(low, zero, q2)], axis=0)


def _band_softmax(qst, k2, v2, mask, sinks):
    s = lax.dot_general(qst, k2, (((1,), (1,)), ((), ())), preferred_element_type=F32)
    s = jnp.where(mask, s, NEG_INF)
    rows, tk = s.shape
    m = jnp.max(s, axis=-1, keepdims=True)
    if sinks is not None:
        seg = rows // len(sinks)
        m = jnp.concatenate([jnp.maximum(m[h * seg:(h + 1) * seg], sk) for h, sk in enumerate(sinks)], axis=0)
    m = jnp.broadcast_to(m, (rows, LANES))
    e = jnp.concatenate([jnp.exp(s[:, c * LANES:(c + 1) * LANES] - m) for c in range(tk // LANES)], axis=1)
    v_ones = jnp.concatenate([v2, jnp.ones((tk, LANES), BF16)], axis=1)
    od = jnp.dot(e.astype(BF16), v_ones, preferred_element_type=F32)
    o, den = od[:, :LANES], od[:, LANES:]
    if sinks is not None:
        den = jnp.concatenate([den[h * seg:(h + 1) * seg] + jnp.exp(sk - m[h * seg:(h + 1) * seg])
                               for h, sk in enumerate(sinks)], axis=0)
    return o / den, m, den


def _band_mask(qs, ks, nstack, tq, tk, window):
    row = lax.broadcasted_iota(I32, (nstack * tq, tk), 0) & (tq - 1)
    col = lax.broadcasted_iota(I32, (nstack * tq, tk), 1)
    return jnp.abs((ks + col) - (qs + row)) <= window


def _dil_kernel(q_ref, k_ref, v_ref, o_ref, l_ref, *, length, tq, tk):
    low = lax.broadcasted_iota(I32, (tq, LANES), 1) < 64

    nq = length // tq

    def body(j, carry):
        r = j // nq
        qs = pl.multiple_of((j % nq) * tq, tq)
        ks = pl.multiple_of(jnp.clip(qs - DIL_HALF_WINDOW, 0, length - tk), DIL_HALF_WINDOW)
        mask = _band_mask(qs, ks, 2, tq, tk, DIL_HALF_WINDOW)
        for c in range(DIL_GROUP_W // LANES):
            cs = slice(c * LANES, (c + 1) * LANES)
            qst = _split_heads(q_ref[r, pl.ds(qs, tq), cs], low)
            o, m, den = _band_softmax(qst, k_ref[r, pl.ds(ks, tk), cs], v_ref[r, pl.ds(ks, tk), cs], mask, None)
            lse = m + jnp.log(den)
            o_ref[r, pl.ds(qs, tq), cs] = jnp.where(low, o[:tq], o[tq:]).astype(BF16)
            l_ref[r, pl.ds(qs, tq), cs] = jnp.where(low, lse[:tq], lse[tq:])
        return carry

    lax.fori_loop(0, q_ref.shape[0] * nq, body, 0, unroll=4)


def _dil_attention(q, k, v):
    batch, d, length, w = q.shape
    tq, tk = 128, 256
    spec = pl.BlockSpec((None, d, length, w), lambda b: (b, 0, 0, 0))
    return pl.pallas_call(
        functools.partial(_dil_kernel, length=length, tq=tq, tk=tk),
        grid=(batch,),
        in_specs=[spec, spec, spec],
        out_specs=[spec, spec],
        out_shape=[jax.ShapeDtypeStruct(q.shape, BF16), jax.ShapeDtypeStruct(q.shape, F32)],
        compiler_params=_cparams(("arbitrary",)),
        name=f"dil{d}",
    )(q, k, v)


def _swa_kernel(sink_ref, q_ref, k_ref, v_ref, o_ref, *, length, tq, tk):
    low = lax.broadcasted_iota(I32, (tq, LANES), 1) < 64
    nblk = q_ref.shape[1] // LANES

    def body(j, carry):
        qs = pl.multiple_of(j * tq, tq)
        ks = pl.multiple_of(jnp.clip(qs - SWA_WINDOW, 0, length - tk), SWA_WINDOW)
        mask = _band_mask(qs, ks, 2, tq, tk, SWA_WINDOW)
        for b in range(nblk):
            cs = slice((b // 2) * LANES, (b // 2 + 1) * LANES)
            bs = slice(b * LANES, (b + 1) * LANES)
            qst = _split_heads(q_ref[pl.ds(qs, tq), bs], low)
            sinks = (sink_ref[2 * b], sink_ref[2 * b + 1])
            o, _, _ = _band_softmax(qst, k_ref[pl.ds(ks, tk), cs], v_ref[pl.ds(ks, tk), cs], mask, sinks)
            o_ref[pl.ds(qs, tq), bs] = jnp.where(low, o[:tq], o[tq:]).astype(BF16)
        return carry

    lax.fori_loop(0, length // tq, body, 0, unroll=4)


def _swa_attention(sink, q, k, v):
    batch, length, qw = q.shape
    tq, tk = 128, 384
    return pl.pallas_call(
        functools.partial(_swa_kernel, length=length, tq=tq, tk=tk),
        grid=(batch,),
        in_specs=[pl.BlockSpec(memory_space=pltpu.SMEM),
                  pl.BlockSpec((None, length, qw), lambda b: (b, 0, 0)),
                  pl.BlockSpec((None, length, k.shape[2]), lambda b: (b, 0, 0)),
                  pl.BlockSpec((None, length, v.shape[2]), lambda b: (b, 0, 0))],
        out_specs=pl.BlockSpec((None, length, qw), lambda b: (b, 0, 0)),
        out_shape=jax.ShapeDtypeStruct(q.shape, BF16),
        compiler_params=_cparams(("arbitrary",)),
        name="swa",
    )(sink, q, k, v)


def _route_rows(logits):
    lane = lax.broadcasted_iota(I32, logits.shape, 1).astype(F32)
    big = 1e9
    is_g = lane < N_GROUPS
    gl = jnp.where(is_g, logits, NEG_INF)
    gmax = jnp.max(gl, axis=-1, keepdims=True)
    gsel = jnp.min(jnp.where(is_g & (gl == gmax), lane, big), axis=-1, keepdims=True)
    gw = 1.0 / jnp.sum(jnp.where(is_g, jnp.exp(gl - gmax), 0.0), axis=-1, keepdims=True)
    e_lo = N_GROUPS + gsel * EXPERTS_PER_GROUP
    in_grp = (lane >= e_lo) & (lane < e_lo + EXPERTS_PER_GROUP)
    el = jnp.where(in_grp, logits, NEG_INF)
    m1 = jnp.max(el, axis=-1, keepdims=True)
    i1 = jnp.min(jnp.where(in_grp & (el == m1), lane, big), axis=-1, keepdims=True)
    el2 = jnp.where(lane == i1, NEG_INF, el)
    m2 = jnp.max(el2, axis=-1, keepdims=True)
    i2 = jnp.min(jnp.where(in_grp & (lane != i1) & (el2 == m2), lane, big), axis=-1, keepdims=True)
    t = jnp.exp(m2 - m1)
    tw1 = gw / (1.0 + t)
    tw2 = gw * t / (1.0 + t)
    out = jnp.where(lane == 0, tw1, 0.0)
    out = jnp.where(lane == 1, tw2, out)
    out = jnp.where(lane == 2, i1 - N_GROUPS, out)
    return jnp.where(lane == 3, i2 - N_GROUPS, out)


def _outproj_kernel(o0_ref, l0_ref, o1_ref, l1_ref, o2_ref, l2_ref, ob_ref, sga_ref, sgb_ref, x_ref,
                    mod_ref, g_ref, wa_ref, wb_ref, wo_ref, wr_ref, br_ref,
                    x1_ref, h2_ref, route_ref, len_ref,
                    so1_ref, sl1_ref, so2_ref, sl2_ref, *, tm, sub):
    for (o_ref, l_ref, so_ref, sl_ref, d) in ((o1_ref, l1_ref, so1_ref, sl1_ref, DILATIONS[1]),
                                              (o2_ref, l2_ref, so2_ref, sl2_ref, DILATIONS[2])):
        for r in range(d):
            for c in range(2):
                cs = slice(c * LANES, (c + 1) * LANES)
                so_ref[c, pl.ds(r, tm // d, stride=d), :] = o_ref[0, r, :, cs].astype(F32)
                sl_ref[c, pl.ds(r, tm // d, stride=d), :] = l_ref[0, r, :, cs]
    for t in range(tm // sub):
        rs = slice(t * sub, (t + 1) * sub)
        both = lambda ref: jnp.concatenate([ref[0, rs, :], ref[1, rs, :]], axis=1)
        o0, l0 = o0_ref[0, 0, rs, :].astype(F32), l0_ref[0, 0, rs, :]
        o1, l1, o2, l2 = both(so1_ref), both(sl1_ref), both(so2_ref), both(sl2_ref)
        mx = jnp.maximum(jnp.maximum(l0, l1), l2)
        w0, w1, w2 = jnp.exp(l0 - mx), jnp.exp(l1 - mx), jnp.exp(l2 - mx)
        o_a = (w0 * o0 + w1 * o1 + w2 * o2) / (w0 + w1 + w2)
        y_a = jnp.dot(o_a.astype(BF16), wa_ref[...], preferred_element_type=F32)
        y_b = jnp.dot(ob_ref[rs, :], wb_ref[...], preferred_element_type=F32)
        merged = sga_ref[rs, :].astype(F32) * y_a + sgb_ref[rs, :].astype(F32) * y_b
        mix = jnp.dot(merged.astype(BF16), wo_ref[...], preferred_element_type=F32)
        x1 = x_ref[rs, :] + mod_ref[0, 2:3, :] * mix
        x1_ref[rs, :] = x1
        h2 = _rms_mod(x1, g_ref[...], mod_ref[0, 3:4, :], mod_ref[0, 4:5, :]).astype(BF16)
        h2_ref[rs, :] = h2
        logits = jnp.dot(h2, wr_ref[...], preferred_element_type=F32) + br_ref[...]
        route_ref[rs, :] = _route_rows(logits)

    part = route_ref[...]
    e1, e2 = part[:, 2:3], part[:, 3:4]
    lane = lax.broadcasted_iota(I32, (tm, LANES), 1).astype(F32)
    onehot = jnp.where((lane == e1) | (lane == e2), 1.0, 0.0)
    rr = lax.broadcasted_iota(I32, (tm, tm), 0)
    cc = lax.broadcasted_iota(I32, (tm, tm), 1)
    tri = jnp.where(rr > cc, 1.0, 0.0).astype(BF16)
    prefix = jnp.dot(tri, onehot.astype(BF16), preferred_element_type=F32)
    cnt = jnp.sum(onehot, axis=0, keepdims=True)
    seg_len = jnp.ceil(cnt * (1.0 / SEG_ROWS)) * SEG_ROWS
    ur = lax.broadcasted_iota(I32, (LANES, LANES), 0)
    uc = lax.broadcasted_iota(I32, (LANES, LANES), 1)
    upper = jnp.where(ur < uc, 1.0, 0.0).astype(BF16)
    seg_off = jnp.dot(jnp.broadcast_to(seg_len, (8, LANES)).astype(BF16), upper, preferred_element_type=F32)[0:1, :]
    slot_map = seg_off + prefix
    s1 = jnp.sum(jnp.where(lane == e1, slot_map, 0.0), axis=-1, keepdims=True)
    s2 = jnp.sum(jnp.where(lane == e2, slot_map, 0.0), axis=-1, keepdims=True)
    len_ref[0] = jnp.broadcast_to(seg_len, (8, LANES))
    route_ref[...] = jnp.where(lane == 2, s1, jnp.where(lane == 3, s2, part))


def _outproj(dil_outs, ob, sga, sgb, x2, mod3, g_ffn, wa, wb, wo, wr, br, batch, seq):
    n, dm = x2.shape
    tm = MOE_TILE
    tpb = seq // tm
    row = lambda i: (i, 0)
    const = lambda i: (0, 0)
    in_specs = []
    args = []
    for (o, l), d in zip(dil_outs, DILATIONS):
        spec = pl.BlockSpec((1, d, tm // d, DIL_GROUP_W), lambda i: (i // tpb, 0, i % tpb, 0))
        in_specs += [spec, spec]
        args += [o, l]
    in_specs += [pl.BlockSpec((tm, ob.shape[1]), row), pl.BlockSpec((tm, dm), row), pl.BlockSpec((tm, dm), row),
                 pl.BlockSpec((tm, dm), row),
                 pl.BlockSpec((1, 6, dm), lambda i: (i // tpb, 0, 0)),
                 pl.BlockSpec((1, dm), const),
                 pl.BlockSpec(wa.shape, const), pl.BlockSpec(wb.shape, const), pl.BlockSpec(wo.shape, const),
                 pl.BlockSpec(wr.shape, const), pl.BlockSpec(br.shape, const)]
    args += [ob, sga, sgb, x2, mod3, g_ffn, wa, wb, wo, wr, br]
    return pl.pallas_call(
        functools.partial(_outproj_kernel, tm=tm, sub=512),
        grid=(n // tm,),
        in_specs=in_specs,
        out_specs=[pl.BlockSpec((tm, dm), row), pl.BlockSpec((tm, dm), row), pl.BlockSpec((tm, LANES), row),
                   pl.BlockSpec((1, 8, LANES), lambda i: (i, 0, 0))],
        out_shape=[jax.ShapeDtypeStruct((n, dm), F32), jax.ShapeDtypeStruct((n, dm), BF16),
                   jax.ShapeDtypeStruct((n, LANES), F32), jax.ShapeDtypeStruct((n // tm, 8, LANES), F32)],
        scratch_shapes=[pltpu.VMEM((2, tm, LANES), F32)] * 4,
        compiler_params=_cparams(("arbitrary",)),
        name="outproj",
    )(*args)


def _segment_copies(i, base_ref, nch_ref, make_copy):
    def expert_body(e, off):
        nch = nch_ref[i * N_EXPERTS + e]
        base = base_ref[i * N_EXPERTS + e]

        def chunk(c, carry):
            make_copy(pl.multiple_of(off + c * SEG_ROWS, SEG_ROWS),
                      pl.multiple_of(base + c * SEG_ROWS, SEG_ROWS)).start()
            return carry

        lax.fori_loop(0, nch, chunk, 0)
        return off + nch * SEG_ROWS

    total_rows = lax.fori_loop(0, N_EXPERTS, expert_body, 0)
    return total_rows // SEG_ROWS


def _wait_copies(count, copy):
    def wait_one(c, carry):
        copy.wait()
        return carry

    lax.fori_loop(0, count, wait_one, 0)


def _dispatch_kernel(base_ref, nch_ref, tstart_ref, tch_ref, nused_ref, h_ref, route_ref, xr_ref,
                     sorted_ref, zero_ref, sem, *, tm, nslots, nblk):
    i = pl.program_id(0)
    dm = h_ref.shape[1]
    route = route_ref[...]
    lane = lax.broadcasted_iota(I32, (tm, LANES), 1)
    pr = lax.broadcasted_iota(I32, (8, LANES), 0)
    pc = lax.broadcasted_iota(I32, (8, LANES), 1)
    pick = jnp.where(pc == pr + 2, 1.0, 0.0)
    slots_t = lax.dot_general(pick, route, (((1,), (1,)), ((), ())), preferred_element_type=F32,
                              precision=lax.Precision.HIGHEST)
    srow = lax.broadcasted_iota(I32, (nslots, tm), 0).astype(F32)
    p0 = jnp.where(srow == slots_t[0:1, :], 1.0, 0.0)
    p1 = jnp.where(srow == slots_t[1:2, :], 1.0, 0.0)
    sorted_ref[:, :dm] = jnp.dot((p0 + p1).astype(BF16), h_ref[...], preferred_element_type=F32).astype(BF16)

    def weight_cols(k):
        w = route[:, k:k + 1]
        hi = w.astype(BF16).astype(F32)
        return jnp.where(lane == 0, hi, jnp.where(lane == 1, w - hi, 0.0)).astype(BF16)

    aux = (jnp.dot(p0.astype(BF16), weight_cols(0), preferred_element_type=F32)
           + jnp.dot(p1.astype(BF16), weight_cols(1), preferred_element_type=F32))
    sorted_ref[:, dm:] = aux.astype(BF16)

    def seg_copy(src, dst):
        return pltpu.make_async_copy(sorted_ref.at[pl.ds(src, SEG_ROWS), :], xr_ref.at[pl.ds(dst, SEG_ROWS), :], sem)

    nchunks = _segment_copies(i, base_ref, nch_ref, seg_copy)
    _wait_copies(nchunks, seg_copy(0, 0))

    @pl.when(i == pl.num_programs(0) - 1)
    def _():
        zero_ref[...] = jnp.zeros_like(zero_ref)

        def zero_seg(dst):
            return pltpu.make_async_copy(zero_ref.at[pl.ds(0, SEG_ROWS), :], xr_ref.at[pl.ds(dst, SEG_ROWS), :], sem)

        def tail(e, count):
            def chunk(c, carry):
                zero_seg(pl.multiple_of(tstart_ref[e] + c * SEG_ROWS, SEG_ROWS)).start()
                return carry

            lax.fori_loop(0, tch_ref[e], chunk, 0)
            return count + tch_ref[e]

        _wait_copies(lax.fori_loop(0, N_EXPERTS, tail, 0), zero_seg(0))

        def zero_blk(b):
            return pltpu.make_async_copy(zero_ref, xr_ref.at[pl.ds(pl.multiple_of(b * ROW_BLOCK, ROW_BLOCK), ROW_BLOCK), :], sem)

        def blk(b, carry):
            zero_blk(b).start()
            return carry

        lax.fori_loop(nused_ref[0], nblk, blk, 0)
        _wait_copies(nblk - nused_ref[0], zero_blk(0))


def _dispatch(tables, h2, route, nblk, tm, nslots):
    n, dm = h2.shape
    rows = nblk * ROW_BLOCK
    width = dm + LANES
    grid_spec = pltpu.PrefetchScalarGridSpec(
        num_scalar_prefetch=5,
        grid=(n // tm,),
        in_specs=[pl.BlockSpec((tm, dm), lambda i, *_: (i, 0)),
                  pl.BlockSpec((tm, LANES), lambda i, *_: (i, 0))],
        out_specs=pl.BlockSpec(memory_space=pl.ANY),
        scratch_shapes=[pltpu.VMEM((nslots, width), BF16), pltpu.VMEM((ROW_BLOCK, width), BF16),
                        pltpu.SemaphoreType.DMA(())],
    )
    return pl.pallas_call(
        functools.partial(_dispatch_kernel, tm=tm, nslots=nslots, nblk=nblk),
        grid_spec=grid_spec,
        out_shape=jax.ShapeDtypeStruct((rows, width), BF16),
        compiler_params=_cparams(("arbitrary",)),
        name="dispatch",
    )(*tables, h2, route)


def _experts_kernel(first_ref, count_ref, wg_ref, wu_ref, wd_ref, x_hbm, y_hbm,
                    xbuf, ybuf, wgb_ref, wub_ref, wdb_ref, xsem, ysem, *, nblk):
    e = pl.program_id(0)
    first = first_ref[e]
    count = count_ref[e]
    dm = ybuf.shape[2]

    def x_copy(b, slot):
        rows = pl.ds(pl.multiple_of((first + b) * ROW_BLOCK, ROW_BLOCK), ROW_BLOCK)
        return pltpu.make_async_copy(x_hbm.at[rows, :], xbuf.at[slot], xsem.at[slot])

    def y_copy(b, slot):
        rows = pl.ds(pl.multiple_of((first + b) * ROW_BLOCK, ROW_BLOCK), ROW_BLOCK)
        return pltpu.make_async_copy(ybuf.at[slot], y_hbm.at[rows, :], ysem.at[slot])

    @pl.when(count > 0)
    def _():
        x_copy(0, 0).start()

    wgb_ref[...] = wg_ref[0].astype(BF16)
    wub_ref[...] = wu_ref[0].astype(BF16)
    wdb_ref[...] = wd_ref[0].astype(BF16)

    def body(b, carry):
        slot = b & 1
        x_copy(b, slot).wait()

        @pl.when(b + 1 < count)
        def _():
            x_copy(b + 1, 1 - slot).start()

        @pl.when(b >= 2)
        def _():
            y_copy(b - 2, slot).wait()

        xb = xbuf[slot, :, :dm]
        aux = xbuf[slot, :, dm:].astype(F32)
        w = aux[:, 0:1] + aux[:, 1:2]
        g = jnp.dot(xb, wgb_ref[...], preferred_element_type=F32)
        u = jnp.dot(xb, wub_ref[...], preferred_element_type=F32)
        a = (g * jax.nn.sigmoid(g)) * u
        y = jnp.dot(a.astype(BF16), wdb_ref[...], preferred_element_type=F32)
        ybuf[slot] = (y * w).astype(BF16)
        y_copy(b, slot).start()
        return carry

    lax.fori_loop(0, count, body, 0)

    @pl.when(count >= 2)
    def _():
        y_copy(count - 2, count & 1).wait()

    @pl.when(count >= 1)
    def _():
        y_copy(count - 1, (count - 1) & 1).wait()

    @pl.when(e == pl.num_programs(0) - 1)
    def _():
        ybuf[0] = jnp.zeros((ROW_BLOCK, dm), BF16)
        used = first + count

        def zero_blk(b):
            rows = pl.ds(pl.multiple_of(b * ROW_BLOCK, ROW_BLOCK), ROW_BLOCK)
            return pltpu.make_async_copy(ybuf.at[0], y_hbm.at[rows, :], ysem.at[0])

        def start(b, carry):
            zero_blk(b).start()
            return carry

        lax.fori_loop(used, nblk, start, 0)
        _wait_copies(nblk - used, zero_blk(0))


def _experts(first_blk, count_blk, x_rows, w_gate, w_up, w_down):
    rows, width = x_rows.shape
    n_exp, dm, de = w_gate.shape
    nblk = rows // ROW_BLOCK
    wmap = lambda e, *_: (e, 0, 0)
    grid_spec = pltpu.PrefetchScalarGridSpec(
        num_scalar_prefetch=2,
        grid=(n_exp,),
        in_specs=[pl.BlockSpec((1, dm, de), wmap),
                  pl.BlockSpec((1, dm, de), wmap),
                  pl.BlockSpec((1, de, dm), wmap),
                  pl.BlockSpec(memory_space=pl.ANY)],
        out_specs=pl.BlockSpec(memory_space=pl.ANY),
        scratch_shapes=[pltpu.VMEM((2, ROW_BLOCK, width), BF16), pltpu.VMEM((2, ROW_BLOCK, dm), BF16),
                        pltpu.VMEM((dm, de), BF16), pltpu.VMEM((dm, de), BF16), pltpu.VMEM((de, dm), BF16),
                        pltpu.SemaphoreType.DMA((2,)), pltpu.SemaphoreType.DMA((2,))],
    )
    return pl.pallas_call(
        functools.partial(_experts_kernel, nblk=nblk),
        grid_spec=grid_spec,
        out_shape=jax.ShapeDtypeStruct((rows, dm), BF16),
        compiler_params=_cparams(("arbitrary",)),
        name="experts",
    )(first_blk, count_blk, w_gate, w_up, w_down, x_rows)


def _combine_kernel(base_ref, nch_ref, route_ref, x1_ref, mod_ref, g_ref, y_ref, o_ref, ys_ref, sem, *, tm, nslots):
    i = pl.program_id(0)

    @pl.when(i == 0)
    def _():
        ys_ref[...] = jnp.zeros_like(ys_ref)

    def seg_copy(dst, src):
        return pltpu.make_async_copy(y_ref.at[pl.ds(src, SEG_ROWS), :], ys_ref.at[pl.ds(dst, SEG_ROWS), :], sem)

    nchunks = _segment_copies(i, base_ref, nch_ref, seg_copy)
    _wait_copies(nchunks, seg_copy(0, 0))
    route = route_ref[...]
    scol = lax.broadcasted_iota(I32, (tm, nslots), 1).astype(F32)
    pick = jnp.where((scol == route[:, 2:3]) | (scol == route[:, 3:4]), 1.0, 0.0).astype(BF16)
    moe = jnp.dot(pick, ys_ref[...], preferred_element_type=F32)
    x = x1_ref[...] + mod_ref[0, 5:6, :] * moe
    ms = jnp.mean(x * x, axis=-1, keepdims=True)
    o_ref[...] = (x * lax.rsqrt(ms + RMS_EPS)) * g_ref[...]


def _combine(base, nch, route, x1, mod3, g_final, y_rows, seq, tm, nslots):
    n, dm = x1.shape
    tpb = seq // tm
    grid_spec = pltpu.PrefetchScalarGridSpec(
        num_scalar_prefetch=2,
        grid=(n // tm,),
        in_specs=[pl.BlockSpec((tm, LANES), lambda i, *_: (i, 0)),
                  pl.BlockSpec((tm, dm), lambda i, *_: (i, 0)),
                  pl.BlockSpec((1, 6, dm), lambda i, *_: (i // tpb, 0, 0)),
                  pl.BlockSpec((1, dm), lambda i, *_: (0, 0)),
                  pl.BlockSpec(memory_space=pl.ANY)],
        out_specs=pl.BlockSpec((tm, dm), lambda i, *_: (i, 0)),
        scratch_shapes=[pltpu.VMEM((nslots, dm), BF16), pltpu.SemaphoreType.DMA(())],
    )
    return pl.pallas_call(
        functools.partial(_combine_kernel, tm=tm, nslots=nslots),
        grid_spec=grid_spec,
        out_shape=jax.ShapeDtypeStruct((n, dm), F32),
        compiler_params=_cparams(("arbitrary",)),
        name="combine",
    )(base, nch, route, x1, mod3, g_final, y_rows)


def _rope_tables(positions):
    half = HEAD_DIM // 2
    inv_freq = ROPE_THETA ** (-jnp.arange(half, dtype=F32) * (2.0 / HEAD_DIM))
    ang = positions.astype(F32).reshape(-1, 1) * inv_freq
    cos, sin = jnp.cos(ang), jnp.sin(ang)
    cos_t = jnp.concatenate([cos, cos, cos, cos], axis=1)
    sin_t = jnp.concatenate([-sin, sin, -sin, sin], axis=1)
    return cos_t, sin_t


def kernel(x, c, positions, w_ada, b_ada, g_mix, w_in, sink_logits, w_branch_a, w_branch_b, w_out, g_ffn,
           w_group, b_group, w_route, b_route, w_expert_gate, w_expert_up, w_expert_down, g_final):
    batch, seq, dm = x.shape
    n = batch * seq
    assert w_ada.shape[0] == 1, "one layer"
    x2 = x.reshape(n, dm)

    c8 = jnp.pad(c, ((0, 8 - batch), (0, 0)))
    mod = _ada(c8, w_ada[0], b_ada[0].reshape(1, -1))
    mod3 = mod[:batch].reshape(batch, 6, dm)

    cos_t, sin_t = _rope_tables(positions)
    outs = _inproj(x2, mod3, g_mix[0].reshape(1, dm), cos_t, sin_t, w_in[0].astype(BF16), batch, seq)
    qkv = outs[:9]
    qb, kb, vb, sga, sgb = outs[9:]

    dil_outs = [_dil_attention(qkv[3 * g], qkv[3 * g + 1], qkv[3 * g + 2]) for g in range(len(DILATIONS))]
    ob = _swa_attention(sink_logits[0], qb.reshape(batch, seq, -1), kb.reshape(batch, seq, -1),
                        vb.reshape(batch, seq, -1)).reshape(n, -1)

    pad = LANES - N_GROUPS - N_EXPERTS
    wr = jnp.concatenate([w_group[0], w_route[0], jnp.zeros((dm, pad), F32)], axis=1).astype(BF16)
    br = jnp.concatenate([b_group[0], b_route[0], jnp.zeros((pad,), F32)]).reshape(1, LANES)
    x1, h2, route, seg_lens = _outproj(dil_outs, ob, sga, sgb, x2, mod3, g_ffn[0].reshape(1, dm),
                                       w_branch_a[0].astype(BF16), w_branch_b[0].astype(BF16), w_out[0].astype(BF16),
                                       wr, br, batch, seq)

    ntiles = n // MOE_TILE
    nslots = 2 * MOE_TILE + N_EXPERTS * SEG_ROWS
    nblk = -(-(2 * n + ntiles * N_EXPERTS * (SEG_ROWS - 1)) // ROW_BLOCK) + N_EXPERTS
    lens = seg_lens[:, 0, :N_EXPERTS].astype(I32)
    tot = jnp.sum(lens, axis=0)
    padded = (tot + ROW_BLOCK - 1) // ROW_BLOCK * ROW_BLOCK
    pend = jnp.cumsum(padded)
    pstart = pend - padded
    base = (pstart[None, :] + jnp.cumsum(lens, axis=0) - lens).reshape(-1)
    nch = (lens // SEG_ROWS).reshape(-1)
    nused = pend[-1:] // ROW_BLOCK
    tables = (base, nch, pstart + tot, (padded - tot) // SEG_ROWS, nused)

    x_rows = _dispatch(tables, h2, route, nblk, MOE_TILE, nslots)
    y_rows = _experts(pstart // ROW_BLOCK, padded // ROW_BLOCK, x_rows,
                      w_expert_gate[0], w_expert_up[0], w_expert_down[0])
    out = _combine(base, nch, route, x1, mod3, g_final.reshape(1, dm), y_rows, seq, MOE_TILE, nslots)
    return out.reshape(batch, seq, dm)
```

```python
import functools

import jax
import jax.numpy as jnp
from jax import lax
from jax.experimental import pallas as pl
from jax.experimental.pallas import tpu as pltpu

F32 = jnp.float32
BF16 = jnp.bfloat16
I32 = jnp.int32

HEAD_DIM = 64
ROPE_THETA = 10000.0
RMS_EPS = 1e-6
NEG_INF = -1e30
DILATIONS = (1, 4, 16)
DIL_HALF_WINDOW = 64
DIL_GROUP_W = 256
SWA_WINDOW = 128
N_GROUPS = 4
EXPERTS_PER_GROUP = 8
N_EXPERTS = 32
LANES = 128
ROW_BLOCK = 256
SEG_ROWS = 16
MOE_TILE = 512
VMEM_LIMIT = 56 * 1024 * 1024


def _cparams(sem):
    return pltpu.CompilerParams(dimension_semantics=sem, vmem_limit_bytes=VMEM_LIMIT)


def _ada_kernel(c_ref, w_ref, b_ref, o_ref):
    c = c_ref[...]
    cs = c * jax.nn.sigmoid(c)
    o_ref[...] = jnp.dot(cs.astype(BF16), w_ref[...].astype(BF16), preferred_element_type=F32) + b_ref[...]


def _ada(c8, w_ada, b_ada):
    d, n = w_ada.shape
    tn = 1536
    return pl.pallas_call(
        _ada_kernel,
        grid=(n // tn,),
        in_specs=[pl.BlockSpec((8, d), lambda j: (0, 0)),
                  pl.BlockSpec((d, tn), lambda j: (0, j)),
                  pl.BlockSpec((1, tn), lambda j: (0, j))],
        out_specs=pl.BlockSpec((8, tn), lambda j: (0, j)),
        out_shape=jax.ShapeDtypeStruct((8, n), F32),
        compiler_params=_cparams(("arbitrary",)),
        name="ada",
    )(c8, w_ada, b_ada)


def _rms_mod(x, g, shift, scale):
    ms = jnp.mean(x * x, axis=-1, keepdims=True)
    return (x * lax.rsqrt(ms + RMS_EPS)) * g * (1.0 + scale) + shift


def _inproj_kernel(x_ref, mod_ref, g_ref, cos_ref, sin_ref, w_ref,
                   q0_ref, k0_ref, v0_ref, q1_ref, k1_ref, v1_ref, q2_ref, k2_ref, v2_ref,
                   qb_ref, kb_ref, vb_ref, sga_ref, sgb_ref, stg_ref, *, tm):
    h = _rms_mod(x_ref[...], g_ref[...], mod_ref[0, 0:1, :], mod_ref[0, 1:2, :])
    hb = h.astype(BF16)
    cos = cos_ref[...]
    sin = sin_ref[...]
    lane = lax.broadcasted_iota(I32, (tm, LANES), 1)
    first_half = (lane & 32) == 0
    low = lane < 64

    def proj(c0, width):
        return jnp.dot(hb, w_ref[:, c0:c0 + width], preferred_element_type=F32)

    def rope(t):
        rot = jnp.where(first_half, pltpu.roll(t, 96, 1), pltpu.roll(t, 32, 1))
        return t * cos + rot * sin

    def rope256(p):
        return jnp.concatenate([rope(p[:, :LANES]), rope(p[:, LANES:])], axis=1)

    def store_group(ref, val, d):
        if d == 1:
            ref[0, 0] = val.astype(BF16)
        else:
            for c in range(2):
                stg_ref[c] = val[:, c * LANES:(c + 1) * LANES]
            for r in range(d):
                for c in range(2):
                    ref[0, r, :, c * LANES:(c + 1) * LANES] = (
                        stg_ref[c, pl.ds(r, tm // d, stride=d), :].astype(BF16))

    q_refs = (q0_ref, q1_ref, q2_ref)
    k_refs = (k0_ref, k1_ref, k2_ref)
    v_refs = (v0_ref, v1_ref, v2_ref)
    for g, d in enumerate(DILATIONS):
        store_group(q_refs[g], rope256(proj(g * 256, 256)) * 0.125, d)
        store_group(k_refs[g], rope256(proj(768 + g * 256, 256)), d)
        store_group(v_refs[g], proj(1536 + g * 256, 256), d)
    for j in range(2):
        qb_ref[:, j * 256:(j + 1) * 256] = (rope256(proj(2304 + j * 256, 256)) * 0.125).astype(BF16)
    kv = proj(2816, 256)
    kb = rope(kv[:, :LANES])
    vb = kv[:, LANES:]
    kb_sw = pltpu.roll(kb, 64, 1)
    vb_sw = pltpu.roll(vb, 64, 1)
    kb_ref[:, :LANES] = jnp.where(low, kb, kb_sw).astype(BF16)
    kb_ref[:, LANES:] = jnp.where(low, kb_sw, kb).astype(BF16)
    vb_ref[:, :LANES] = jnp.where(low, vb, vb_sw).astype(BF16)
    vb_ref[:, LANES:] = jnp.where(low, vb_sw, vb).astype(BF16)
    for j in range(4):
        sga_ref[:, j * 256:(j + 1) * 256] = jax.nn.sigmoid(proj(3072 + j * 256, 256)).astype(BF16)
        sgb_ref[:, j * 256:(j + 1) * 256] = jax.nn.sigmoid(proj(4096 + j * 256, 256)).astype(BF16)


def _inproj(x2, mod3, g_mix, cos_t, sin_t, w_in_bf, batch, seq):
    n, dm = x2.shape
    tm = 512
    tpb = seq // tm
    grid = (n // tm,)
    row = lambda i: (i, 0)
    strided_specs, strided_shapes = [], []
    for d in DILATIONS:
        for _ in range(3):
            strided_specs.append(pl.BlockSpec((1, d, tm // d, DIL_GROUP_W), lambda i: (i // tpb, 0, i % tpb, 0)))
            strided_shapes.append(jax.ShapeDtypeStruct((batch, d, seq // d, DIL_GROUP_W), BF16))
    out_specs = strided_specs + [
        pl.BlockSpec((tm, 512), row), pl.BlockSpec((tm, 256), row), pl.BlockSpec((tm, 256), row),
        pl.BlockSpec((tm, dm), row), pl.BlockSpec((tm, dm), row)]
    out_shapes = strided_shapes + [
        jax.ShapeDtypeStruct((n, 512), BF16), jax.ShapeDtypeStruct((n, 256), BF16),
        jax.ShapeDtypeStruct((n, 256), BF16), jax.ShapeDtypeStruct((n, dm), BF16),
        jax.ShapeDtypeStruct((n, dm), BF16)]
    return pl.pallas_call(
        functools.partial(_inproj_kernel, tm=tm),
        grid=grid,
        in_specs=[pl.BlockSpec((tm, dm), row),
                  pl.BlockSpec((1, 6, dm), lambda i: (i // tpb, 0, 0)),
                  pl.BlockSpec((1, dm), lambda i: (0, 0)),
                  pl.BlockSpec((tm, LANES), row),
                  pl.BlockSpec((tm, LANES), row),
                  pl.BlockSpec(w_in_bf.shape, lambda i: (0, 0))],
        out_specs=out_specs,
        out_shape=out_shapes,
        scratch_shapes=[pltpu.VMEM((2, tm, LANES), F32)],
        compiler_params=_cparams(("arbitrary",)),
        name="inproj",
    )(x2, mod3, g_mix, cos_t, sin_t, w_in_bf)


def _split_heads(q2, low):
    zero = jnp.zeros_like(q2)
    return jnp.concatenate([jnp.where(low, q2, zero), jnp.where(low, zero, q2)], axis=0)


def _band_softmax(qst, k2, v2, mask, sinks):
    s = lax.dot_general(qst, k2, (((1,), (1,)), ((), ())), preferred_element_type=F32)
    s = jnp.where(mask, s, NEG_INF)
    rows, tk = s.shape
    m = jnp.max(s, axis=-1, keepdims=True)
    if sinks is not None:
        seg = rows // len(sinks)
        m = jnp.concatenate([jnp.maximum(m[h * seg:(h + 1) * seg], sk) for h, sk in enumerate(sinks)], axis=0)
    m = jnp.broadcast_to(m, (rows, LANES))
    e = jnp.concatenate([jnp.exp(s[:, c * LANES:(c + 1) * LANES] - m) for c in range(tk // LANES)], axis=1)
    v_ones = jnp.concatenate([v2, jnp.ones((tk, LANES), BF16)], axis=1)
    od = jnp.dot(e.astype(BF16), v_ones, preferred_element_type=F32)
    o, den = od[:, :LANES], od[:, LANES:]
    if sinks is not None:
        den = jnp.concatenate([den[h * seg:(h + 1) * seg] + jnp.exp(sk - m[h * seg:(h + 1) * seg])
                               for h, sk in enumerate(sinks)], axis=0)
    return o / den, m, den


def _band_mask(qs, ks, nstack, tq, tk, window):
    row = lax.broadcasted_iota(I32, (nstack * tq, tk), 0) & (tq - 1)
    col = lax.broadcasted_iota(I32, (nstack * tq, tk), 1)
    return jnp.abs((ks + col) - (qs + row)) <= window


def _dil_kernel(q_ref, k_ref, v_ref, o_ref, l_ref, *, length, tq, tk):
    low = lax.broadcasted_iota(I32, (tq, LANES), 1) < 64

    nq = length // tq

    def body(j, carry):
        r = j // nq
        qs = pl.multiple_of((j % nq) * tq, tq)
        ks = pl.multiple_of(jnp.clip(qs - DIL_HALF_WINDOW, 0, length - tk), DIL_HALF_WINDOW)
        mask = _band_mask(qs, ks, 2, tq, tk, DIL_HALF_WINDOW)
        for c in range(DIL_GROUP_W // LANES):
            cs = slice(c * LANES, (c + 1) * LANES)
            qst = _split_heads(q_ref[r, pl.ds(qs, tq), cs], low)
            o, m, den = _band_softmax(qst, k_ref[r, pl.ds(ks, tk), cs], v_ref[r, pl.ds(ks, tk), cs], mask, None)
            lse = m + jnp.log(den)
            o_ref[r, pl.ds(qs, tq), cs] = jnp.where(low, o[:tq], o[tq:]).astype(BF16)
            l_ref[r, pl.ds(qs, tq), cs] = jnp.where(low, lse[:tq], lse[tq:])
        return carry

    lax.fori_loop(0, q_ref.shape[0] * nq, body, 0, unroll=4)


def _dil_attention(q, k, v):
    batch, d, length, w = q.shape
    tq, tk = 128, 256
    spec = pl.BlockSpec((None, d, length, w), lambda b: (b, 0, 0, 0))
    return pl.pallas_call(
        functools.partial(_dil_kernel, length=length, tq=tq, tk=tk),
        grid=(batch,),
        in_specs=[spec, spec, spec],
        out_specs=[spec, spec],
        out_shape=[jax.ShapeDtypeStruct(q.shape, BF16), jax.ShapeDtypeStruct(q.shape, F32)],
        compiler_params=_cparams(("arbitrary",)),
        name=f"dil{d}",
    )(q, k, v)


def _swa_kernel(sink_ref, q_ref, k_ref, v_ref, o_ref, *, length, tq, tk):
    low = lax.broadcasted_iota(I32, (tq, LANES), 1) < 64
    nblk = q_ref.shape[1] // LANES

    def body(j, carry):
        qs = pl.multiple_of(j * tq, tq)
        ks = pl.multiple_of(jnp.clip(qs - SWA_WINDOW, 0, length - tk), SWA_WINDOW)
        mask = _band_mask(qs, ks, 2, tq, tk, SWA_WINDOW)
        for b in range(nblk):
            cs = slice((b // 2) * LANES, (b // 2 + 1) * LANES)
            bs = slice(b * LANES, (b + 1) * LANES)
            qst = _split_heads(q_ref[pl.ds(qs, tq), bs], low)
            sinks = (sink_ref[2 * b], sink_ref[2 * b + 1])
            o, _, _ = _band_softmax(qst, k_ref[pl.ds(ks, tk), cs], v_ref[pl.ds(ks, tk), cs], mask, sinks)
            o_ref[pl.ds(qs, tq), bs] = jnp.where(low, o[:tq], o[tq:]).astype(BF16)
        return carry

    lax.fori_loop(0, length // tq, body, 0, unroll=4)


def _swa_attention(sink, q, k, v):
    batch, length, qw = q.shape
    tq, tk = 128, 384
    return pl.pallas_call(
        functools.partial(_swa_kernel, length=length, tq=tq, tk=tk),
        grid=(batch,),
        in_specs=[pl.BlockSpec(memory_space=pltpu.SMEM),
                  pl.BlockSpec((None, length, qw), lambda b: (b, 0, 0)),
                  pl.BlockSpec((None, length, k.shape[2]), lambda b: (b, 0, 0)),
                  pl.BlockSpec((None, length, v.shape[2]), lambda b: (b, 0, 0))],
        out_specs=pl.BlockSpec((None, length, qw), lambda b: (b, 0, 0)),
        out_shape=jax.ShapeDtypeStruct(q.shape, BF16),
        compiler_params=_cparams(("arbitrary",)),
        name="swa",
    )(sink, q, k, v)


def _route_rows(logits):
    lane = lax.broadcasted_iota(I32, logits.shape, 1).astype(F32)
    big = 1e9
    is_g = lane < N_GROUPS
    gl = jnp.where(is_g, logits, NEG_INF)
    gmax = jnp.max(gl, axis=-1, keepdims=True)
    gsel = jnp.min(jnp.where(is_g & (gl == gmax), lane, big), axis=-1, keepdims=True)
    gw = 1.0 / jnp.sum(jnp.where(is_g, jnp.exp(gl - gmax), 0.0), axis=-1, keepdims=True)
    e_lo = N_GROUPS + gsel * EXPERTS_PER_GROUP
    in_grp = (lane >= e_lo) & (lane < e_lo + EXPERTS_PER_GROUP)
    el = jnp.where(in_grp, logits, NEG_INF)
    m1 = jnp.max(el, axis=-1, keepdims=True)
    i1 = jnp.min(jnp.where(in_grp & (el == m1), lane, big), axis=-1, keepdims=True)
    el2 = jnp.where(lane == i1, NEG_INF, el)
    m2 = jnp.max(el2, axis=-1, keepdims=True)
    i2 = jnp.min(jnp.where(in_grp & (lane != i1) & (el2 == m2), lane, big), axis=-1, keepdims=True)
    t = jnp.exp(m2 - m1)
    tw1 = gw / (1.0 + t)
    tw2 = gw * t / (1.0 + t)
    out = jnp.where(lane == 0, tw1, 0.0)
    out = jnp.where(lane == 1, tw2, out)
    out = jnp.where(lane == 2, i1 - N_GROUPS, out)
    return jnp.where(lane == 3, i2 - N_GROUPS, out)


def _outproj_kernel(o0_ref, l0_ref, o1_ref, l1_ref, o2_ref, l2_ref, ob_ref, sga_ref, sgb_ref, x_ref,
                    mod_ref, g_ref, wa_ref, wb_ref, wo_ref, wr_ref, br_ref,
                    x1_ref, h2_ref, route_ref, len_ref,
                    so1_ref, sl1_ref, so2_ref, sl2_ref, *, tm, sub):
    for (o_ref, l_ref, so_ref, sl_ref, d) in ((o1_ref, l1_ref, so1_ref, sl1_ref, DILATIONS[1]),
                                              (o2_ref, l2_ref, so2_ref, sl2_ref, DILATIONS[2])):
        for r in range(d):
            for c in range(2):
                cs = slice(c * LANES, (c + 1) * LANES)
                so_ref[c, pl.ds(r, tm // d, stride=d), :] = o_ref[0, r, :, cs].astype(F32)
                sl_ref[c, pl.ds(r, tm // d, stride=d), :] = l_ref[0, r, :, cs]
    for t in range(tm // sub):
        rs = slice(t * sub, (t + 1) * sub)
        both = lambda ref: jnp.concatenate([ref[0, rs, :], ref[1, rs, :]], axis=1)
        o0, l0 = o0_ref[0, 0, rs, :].astype(F32), l0_ref[0, 0, rs, :]
        o1, l1, o2, l2 = both(so1_ref), both(sl1_ref), both(so2_ref), both(sl2_ref)
        mx = jnp.maximum(jnp.maximum(l0, l1), l2)
        w0, w1, w2 = jnp.exp(l0 - mx), jnp.exp(l1 - mx), jnp.exp(l2 - mx)
        o_a = (w0 * o0 + w1 * o1 + w2 * o2) / (w0 + w1 + w2)
        y_a = jnp.dot(o_a.astype(BF16), wa_ref[...], preferred_element_type=F32)
        y_b = jnp.dot(ob_ref[rs, :], wb_ref[...], preferred_element_type=F32)
        merged = sga_ref[rs, :].astype(F32) * y_a + sgb_ref[rs, :].astype(F32) * y_b
        mix = jnp.dot(merged.astype(BF16), wo_ref[...], preferred_element_type=F32)
        x1 = x_ref[rs, :] + mod_ref[0, 2:3, :] * mix
        x1_ref[rs, :] = x1
        h2 = _rms_mod(x1, g_ref[...], mod_ref[0, 3:4, :], mod_ref[0, 4:5, :]).astype(BF16)
        h2_ref[rs, :] = h2
        logits = jnp.dot(h2, wr_ref[...], preferred_element_type=F32) + br_ref[...]
        route_ref[rs, :] = _route_rows(logits)

    part = route_ref[...]
    e1, e2 = part[:, 2:3], part[:, 3:4]
    lane = lax.broadcasted_iota(I32, (tm, LANES), 1).astype(F32)
    onehot = jnp.where((lane == e1) | (lane == e2), 1.0, 0.0)
    rr = lax.broadcasted_iota(I32, (tm, tm), 0)
    cc = lax.broadcasted_iota(I32, (tm, tm), 1)
    tri = jnp.where(rr > cc, 1.0, 0.0).astype(BF16)
    prefix = jnp.dot(tri, onehot.astype(BF16), preferred_element_type=F32)
    cnt = jnp.sum(onehot, axis=0, keepdims=True)
    seg_len = jnp.ceil(cnt * (1.0 / SEG_ROWS)) * SEG_ROWS
    ur = lax.broadcasted_iota(I32, (LANES, LANES), 0)
    uc = lax.broadcasted_iota(I32, (LANES, LANES), 1)
    upper = jnp.where(ur < uc, 1.0, 0.0).astype(BF16)
    seg_off = jnp.dot(jnp.broadcast_to(seg_len, (8, LANES)).astype(BF16), upper, preferred_element_type=F32)[0:1, :]
    slot_map = seg_off + prefix
    s1 = jnp.sum(jnp.where(lane == e1, slot_map, 0.0), axis=-1, keepdims=True)
    s2 = jnp.sum(jnp.where(lane == e2, slot_map, 0.0), axis=-1, keepdims=True)
    len_ref[0] = jnp.broadcast_to(seg_len, (8, LANES))
    route_ref[...] = jnp.where(lane == 2, s1, jnp.where(lane == 3, s2, part))


def _outproj(dil_outs, ob, sga, sgb, x2, mod3, g_ffn, wa, wb, wo, wr, br, batch, seq):
    n, dm = x2.shape
    tm = MOE_TILE
    tpb = seq // tm
    row = lambda i: (i, 0)
    const = lambda i: (0, 0)
    in_specs = []
    args = []
    for (o, l), d in zip(dil_outs, DILATIONS):
        spec = pl.BlockSpec((1, d, tm // d, DIL_GROUP_W), lambda i: (i // tpb, 0, i % tpb, 0))
        in_specs += [spec, spec]
        args += [o, l]
    in_specs += [pl.BlockSpec((tm, ob.shape[1]), row), pl.BlockSpec((tm, dm), row), pl.BlockSpec((tm, dm), row),
                 pl.BlockSpec((tm, dm), row),
                 pl.BlockSpec((1, 6, dm), lambda i: (i // tpb, 0, 0)),
                 pl.BlockSpec((1, dm), const),
                 pl.BlockSpec(wa.shape, const), pl.BlockSpec(wb.shape, const), pl.BlockSpec(wo.shape, const),
                 pl.BlockSpec(wr.shape, const), pl.BlockSpec(br.shape, const)]
    args += [ob, sga, sgb, x2, mod3, g_ffn, wa, wb, wo, wr, br]
    return pl.pallas_call(
        functools.partial(_outproj_kernel, tm=tm, sub=512),
        grid=(n // tm,),
        in_specs=in_specs,
        out_specs=[pl.BlockSpec((tm, dm), row), pl.BlockSpec((tm, dm), row), pl.BlockSpec((tm, LANES), row),
                   pl.BlockSpec((1, 8, LANES), lambda i: (i, 0, 0))],
        out_shape=[jax.ShapeDtypeStruct((n, dm), F32), jax.ShapeDtypeStruct((n, dm), BF16),
                   jax.ShapeDtypeStruct((n, LANES), F32), jax.ShapeDtypeStruct((n // tm, 8, LANES), F32)],
        scratch_shapes=[pltpu.VMEM((2, tm, LANES), F32)] * 4,
        compiler_params=_cparams(("arbitrary",)),
        name="outproj",
    )(*args)


def _segment_copies(i, base_ref, nch_ref, make_copy):
    def expert_body(e, off):
        nch = nch_ref[i * N_EXPERTS + e]
        base = base_ref[i * N_EXPERTS + e]

        def chunk(c, carry):
            make_copy(pl.multiple_of(off + c * SEG_ROWS, SEG_ROWS),
                      pl.multiple_of(base + c * SEG_ROWS, SEG_ROWS)).start()
            return carry

        lax.fori_loop(0, nch, chunk, 0)
        return off + nch * SEG_ROWS

    total_rows = lax.fori_loop(0, N_EXPERTS, expert_body, 0)
    return total_rows // SEG_ROWS


def _tile_chunks(i, nch_ref):
    return lax.fori_loop(0, N_EXPERTS, lambda e, acc: acc + nch_ref[i * N_EXPERTS + e], 0)


def _wait_copies(count, copy):
    def wait_one(c, carry):
        copy.wait()
        return carry

    lax.fori_loop(0, count, wait_one, 0)


def _dispatch_kernel(base_ref, nch_ref, tstart_ref, tch_ref, nused_ref, h_ref, route_ref, xr_ref,
                     sorted_ref, zero_ref, sem, *, tm, nslots, nblk):
    i = pl.program_id(0)
    last = pl.num_programs(0) - 1
    slot = i & 1
    dm = h_ref.shape[1]

    def seg_copy(s):
        return lambda src, dst: pltpu.make_async_copy(
            sorted_ref.at[s, pl.ds(src, SEG_ROWS), :], xr_ref.at[pl.ds(dst, SEG_ROWS), :], sem.at[s])

    @pl.when(i >= 2)
    def _():
        _wait_copies(_tile_chunks(i - 2, nch_ref), seg_copy(slot)(0, 0))

    route = route_ref[...]
    lane = lax.broadcasted_iota(I32, (tm, LANES), 1)
    pr = lax.broadcasted_iota(I32, (8, LANES), 0)
    pc = lax.broadcasted_iota(I32, (8, LANES), 1)
    pick = jnp.where(pc == pr + 2, 1.0, 0.0)
    slots_t = lax.dot_general(pick, route, (((1,), (1,)), ((), ())), preferred_element_type=F32,
                              precision=lax.Precision.HIGHEST)
    srow = lax.broadcasted_iota(I32, (nslots, tm), 0).astype(F32)
    p0 = jnp.where(srow == slots_t[0:1, :], 1.0, 0.0)
    p1 = jnp.where(srow == slots_t[1:2, :], 1.0, 0.0)
    sorted_ref[slot, :, :dm] = jnp.dot((p0 + p1).astype(BF16), h_ref[...], preferred_element_type=F32).astype(BF16)

    def weight_cols(k):
        w = route[:, k:k + 1]
        hi = w.astype(BF16).astype(F32)
        return jnp.where(lane == 0, hi, jnp.where(lane == 1, w - hi, 0.0)).astype(BF16)

    aux = (jnp.dot(p0.astype(BF16), weight_cols(0), preferred_element_type=F32)
           + jnp.dot(p1.astype(BF16), weight_cols(1), preferred_element_type=F32))
    sorted_ref[slot, :, dm:] = aux.astype(BF16)
    nchunks = _segment_copies(i, base_ref, nch_ref, seg_copy(slot))

    @pl.when(i == last)
    def _():
        _wait_copies(nchunks, seg_copy(slot)(0, 0))

        @pl.when(i >= 1)
        def _():
            _wait_copies(_tile_chunks(i - 1, nch_ref), seg_copy(1 - slot)(0, 0))

        zero_ref[...] = jnp.zeros_like(zero_ref)
        zsem = sem.at[0]

        def zero_seg(dst):
            return pltpu.make_async_copy(zero_ref.at[pl.ds(0, SEG_ROWS), :], xr_ref.at[pl.ds(dst, SEG_ROWS), :], zsem)

        def tail(e, count):
            def chunk(c, carry):
                zero_seg(pl.multiple_of(tstart_ref[e] + c * SEG_ROWS, SEG_ROWS)).start()
                return carry

            lax.fori_loop(0, tch_ref[e], chunk, 0)
            return count + tch_ref[e]

        _wait_copies(lax.fori_loop(0, N_EXPERTS, tail, 0), zero_seg(0))

        def zero_blk(b):
            return pltpu.make_async_copy(zero_ref, xr_ref.at[pl.ds(pl.multiple_of(b * ROW_BLOCK, ROW_BLOCK), ROW_BLOCK), :], zsem)

        def blk(b, carry):
            zero_blk(b).start()
            return carry

        lax.fori_loop(nused_ref[0], nblk, blk, 0)
        _wait_copies(nblk - nused_ref[0], zero_blk(0))


def _dispatch(tables, h2, route, nblk, tm, nslots):
    n, dm = h2.shape
    rows = nblk * ROW_BLOCK
    width = dm + LANES
    grid_spec = pltpu.PrefetchScalarGridSpec(
        num_scalar_prefetch=5,
        grid=(n // tm,),
        in_specs=[pl.BlockSpec((tm, dm), lambda i, *_: (i, 0)),
                  pl.BlockSpec((tm, LANES), lambda i, *_: (i, 0))],
        out_specs=pl.BlockSpec(memory_space=pl.ANY),
        scratch_shapes=[pltpu.VMEM((2, nslots, width), BF16), pltpu.VMEM((ROW_BLOCK, width), BF16),
                        pltpu.SemaphoreType.DMA((2,))],
    )
    return pl.pallas_call(
        functools.partial(_dispatch_kernel, tm=tm, nslots=nslots, nblk=nblk),
        grid_spec=grid_spec,
        out_shape=jax.ShapeDtypeStruct((rows, width), BF16),
        compiler_params=_cparams(("arbitrary",)),
        name="dispatch",
    )(*tables, h2, route)


def _experts_kernel(first_ref, count_ref, wg_ref, wu_ref, wd_ref, x_hbm, y_hbm,
                    xbuf, ybuf, wgb_ref, wub_ref, wdb_ref, xsem, ysem, *, nblk):
    e = pl.program_id(0)
    first = first_ref[e]
    count = count_ref[e]
    dm = ybuf.shape[2]

    def x_copy(b, slot):
        rows = pl.ds(pl.multiple_of((first + b) * ROW_BLOCK, ROW_BLOCK), ROW_BLOCK)
        return pltpu.make_async_copy(x_hbm.at[rows, :], xbuf.at[slot], xsem.at[slot])

    def y_copy(b, slot):
        rows = pl.ds(pl.multiple_of((first + b) * ROW_BLOCK, ROW_BLOCK), ROW_BLOCK)
        return pltpu.make_async_copy(ybuf.at[slot], y_hbm.at[rows, :], ysem.at[slot])

    @pl.when((e == 0) & (count > 0))
    def _():
        x_copy(0, 0).start(priority=1)

    wgb_ref[...] = wg_ref[0].astype(BF16)
    wub_ref[...] = wu_ref[0].astype(BF16)
    wdb_ref[...] = wd_ref[0].astype(BF16)

    def body(b, carry):
        slot = b & 1
        x_copy(b, slot).wait()

        @pl.when(b + 1 < count)
        def _():
            x_copy(b + 1, 1 - slot).start(priority=1)

        @pl.when(b >= 2)
        def _():
            y_copy(b - 2, slot).wait()

        xb = xbuf[slot, :, :dm]
        aux = xbuf[slot, :, dm:].astype(F32)
        w = aux[:, 0:1] + aux[:, 1:2]
        g = jnp.dot(xb, wgb_ref[...], preferred_element_type=F32)
        u = jnp.dot(xb, wub_ref[...], preferred_element_type=F32)
        a = (g * jax.nn.sigmoid(g)) * u
        y = jnp.dot(a.astype(BF16), wdb_ref[...], preferred_element_type=F32)
        ybuf[slot] = (y * w).astype(BF16)
        y_copy(b, slot).start()
        return carry

    lax.fori_loop(0, count, body, 0)

    nxt = jnp.minimum(e + 1, pl.num_programs(0) - 1)

    @pl.when((e + 1 < pl.num_programs(0)) & (count_ref[nxt] > 0))
    def _():
        rows = pl.ds(pl.multiple_of(first_ref[nxt] * ROW_BLOCK, ROW_BLOCK), ROW_BLOCK)
        pltpu.make_async_copy(x_hbm.at[rows, :], xbuf.at[0], xsem.at[0]).start(priority=1)

    @pl.when(count >= 2)
    def _():
        y_copy(count - 2, count & 1).wait()

    @pl.when(count >= 1)
    def _():
        y_copy(count - 1, (count - 1) & 1).wait()

    @pl.when(e == pl.num_programs(0) - 1)
    def _():
        ybuf[0] = jnp.zeros((ROW_BLOCK, dm), BF16)
        used = first + count

        def zero_blk(b):
            rows = pl.ds(pl.multiple_of(b * ROW_BLOCK, ROW_BLOCK), ROW_BLOCK)
            return pltpu.make_async_copy(ybuf.at[0], y_hbm.at[rows, :], ysem.at[0])

        def start(b, carry):
            zero_blk(b).start()
            return carry

        lax.fori_loop(used, nblk, start, 0)
        _wait_copies(nblk - used, zero_blk(0))


def _experts(first_blk, count_blk, x_rows, w_gate, w_up, w_down):
    rows, width = x_rows.shape
    n_exp, dm, de = w_gate.shape
    nblk = rows // ROW_BLOCK
    wmap = lambda e, *_: (e, 0, 0)
    grid_spec = pltpu.PrefetchScalarGridSpec(
        num_scalar_prefetch=2,
        grid=(n_exp,),
        in_specs=[pl.BlockSpec((1, dm, de), wmap),
                  pl.BlockSpec((1, dm, de), wmap),
                  pl.BlockSpec((1, de, dm), wmap),
                  pl.BlockSpec(memory_space=pl.ANY)],
        out_specs=pl.BlockSpec(memory_space=pl.ANY),
        scratch_shapes=[pltpu.VMEM((2, ROW_BLOCK, width), BF16), pltpu.VMEM((2, ROW_BLOCK, dm), BF16),
                        pltpu.VMEM((dm, de), BF16), pltpu.VMEM((dm, de), BF16), pltpu.VMEM((de, dm), BF16),
                        pltpu.SemaphoreType.DMA((2,)), pltpu.SemaphoreType.DMA((2,))],
    )
    return pl.pallas_call(
        functools.partial(_experts_kernel, nblk=nblk),
        grid_spec=grid_spec,
        out_shape=jax.ShapeDtypeStruct((rows, dm), BF16),
        compiler_params=_cparams(("arbitrary",)),
        name="experts",
    )(first_blk, count_blk, w_gate, w_up, w_down, x_rows)


def _combine_kernel(base_ref, nch_ref, route_ref, x1_ref, mod_ref, g_ref, y_ref, o_ref, ys_ref, sem, *, tm, nslots):
    i = pl.program_id(0)
    slot = i & 1

    def seg_copy(s):
        return lambda dst, src: pltpu.make_async_copy(
            y_ref.at[pl.ds(src, SEG_ROWS), :], ys_ref.at[s, pl.ds(dst, SEG_ROWS), :], sem.at[s])

    @pl.when(i == 0)
    def _():
        ys_ref[...] = jnp.zeros_like(ys_ref)
        _segment_copies(0, base_ref, nch_ref, seg_copy(0))

    @pl.when(i + 1 < pl.num_programs(0))
    def _():
        _segment_copies(i + 1, base_ref, nch_ref, seg_copy(1 - slot))

    _wait_copies(_tile_chunks(i, nch_ref), seg_copy(slot)(0, 0))
    route = route_ref[...]
    scol = lax.broadcasted_iota(I32, (tm, nslots), 1).astype(F32)
    pick = jnp.where((scol == route[:, 2:3]) | (scol == route[:, 3:4]), 1.0, 0.0).astype(BF16)
    moe = jnp.dot(pick, ys_ref[slot], preferred_element_type=F32)
    x = x1_ref[...] + mod_ref[0, 5:6, :] * moe
    ms = jnp.mean(x * x, axis=-1, keepdims=True)
    o_ref[...] = (x * lax.rsqrt(ms + RMS_EPS)) * g_ref[...]


def _combine(base, nch, route, x1, mod3, g_final, y_rows, seq, tm, nslots):
    n, dm = x1.shape
    tpb = seq // tm
    grid_spec = pltpu.PrefetchScalarGridSpec(
        num_scalar_prefetch=2,
        grid=(n // tm,),
        in_specs=[pl.BlockSpec((tm, LANES), lambda i, *_: (i, 0)),
                  pl.BlockSpec((tm, dm), lambda i, *_: (i, 0)),
                  pl.BlockSpec((1, 6, dm), lambda i, *_: (i // tpb, 0, 0)),
                  pl.BlockSpec((1, dm), lambda i, *_: (0, 0)),
                  pl.BlockSpec(memory_space=pl.ANY)],
        out_specs=pl.BlockSpec((tm, dm), lambda i, *_: (i, 0)),
        scratch_shapes=[pltpu.VMEM((2, nslots, dm), BF16), pltpu.SemaphoreType.DMA((2,))],
    )
    return pl.pallas_call(
        functools.partial(_combine_kernel, tm=tm, nslots=nslots),
        grid_spec=grid_spec,
        out_shape=jax.ShapeDtypeStruct((n, dm), F32),
        compiler_params=_cparams(("arbitrary",)),
        name="combine",
    )(base, nch, route, x1, mod3, g_final, y_rows)


def _rope_tables(positions):
    half = HEAD_DIM // 2
    inv_freq = ROPE_THETA ** (-jnp.arange(half, dtype=F32) * (2.0 / HEAD_DIM))
    ang = positions.astype(F32).reshape(-1, 1) * inv_freq
    cos, sin = jnp.cos(ang), jnp.sin(ang)
    cos_t = jnp.concatenate([cos, cos, cos, cos], axis=1)
    sin_t = jnp.concatenate([-sin, sin, -sin, sin], axis=1)
    return cos_t, sin_t


def kernel(x, c, positions, w_ada, b_ada, g_mix, w_in, sink_logits, w_branch_a, w_branch_b, w_out, g_ffn,
           w_group, b_group, w_route, b_route, w_expert_gate, w_expert_up, w_expert_down, g_final):
    batch, seq, dm = x.shape
    n = batch * seq
    assert w_ada.shape[0] == 1, "one layer"
    x2 = x.reshape(n, dm)

    c8 = jnp.pad(c, ((0, 8 - batch), (0, 0)))
    mod = _ada(c8, w_ada[0], b_ada[0].reshape(1, -1))
    mod3 = mod[:batch].reshape(batch, 6, dm)

    cos_t, sin_t = _rope_tables(positions)
    outs = _inproj(x2, mod3, g_mix[0].reshape(1, dm), cos_t, sin_t, w_in[0].astype(BF16), batch, seq)
    qkv = outs[:9]
    qb, kb, vb, sga, sgb = outs[9:]

    dil_outs = [_dil_attention(qkv[3 * g], qkv[3 * g + 1], qkv[3 * g + 2]) for g in range(len(DILATIONS))]
    ob = _swa_attention(sink_logits[0], qb.reshape(batch, seq, -1), kb.reshape(batch, seq, -1),
                        vb.reshape(batch, seq, -1)).reshape(n, -1)

    pad = LANES - N_GROUPS - N_EXPERTS
    wr = jnp.concatenate([w_group[0], w_route[0], jnp.zeros((dm, pad), F32)], axis=1).astype(BF16)
    br = jnp.concatenate([b_group[0], b_route[0], jnp.zeros((pad,), F32)]).reshape(1, LANES)
    x1, h2, route, seg_lens = _outproj(dil_outs, ob, sga, sgb, x2, mod3, g_ffn[0].reshape(1, dm),
                                       w_branch_a[0].astype(BF16), w_branch_b[0].astype(BF16), w_out[0].astype(BF16),
                                       wr, br, batch, seq)

    ntiles = n // MOE_TILE
    nslots = 2 * MOE_TILE + N_EXPERTS * SEG_ROWS
    nblk = -(-(2 * n + ntiles * N_EXPERTS * (SEG_ROWS - 1)) // ROW_BLOCK) + N_EXPERTS
    lens = seg_lens[:, 0, :N_EXPERTS].astype(I32)
    tot = jnp.sum(lens, axis=0)
    padded = (tot + ROW_BLOCK - 1) // ROW_BLOCK * ROW_BLOCK
    pend = jnp.cumsum(padded)
    pstart = pend - padded
    base = (pstart[None, :] + jnp.cumsum(lens, axis=0) - lens).reshape(-1)
    nch = (lens // SEG_ROWS).reshape(-1)
    nused = pend[-1:] // ROW_BLOCK
    tables = (base, nch, pstart + tot, (padded - tot) // SEG_ROWS, nused)

    x_rows = _dispatch(tables, h2, route, nblk, MOE_TILE, nslots)
    y_rows = _experts(pstart // ROW_BLOCK, padded // ROW_BLOCK, x_rows,
                      w_expert_gate[0], w_expert_up[0], w_expert_down[0])
    out = _combine(base, nch, route, x1, mod3, g_final.reshape(1, dm), y_rows, seq, MOE_TILE, nslots)
    return out.reshape(batch, seq, dm)
```

```python
import functools

import jax
import jax.numpy as jnp
from jax import lax
from jax.experimental import pallas as pl
from jax.experimental.pallas import tpu as pltpu

F32 = jnp.float32
BF16 = jnp.bfloat16
I32 = jnp.int32

HEAD_DIM = 64
ROPE_THETA = 10000.0
RMS_EPS = 1e-6
NEG_INF = -1e30
DILATIONS = (1, 4, 16)
DIL_HALF_WINDOW = 64
DIL_GROUP_W = 256
SWA_WINDOW = 128
N_GROUPS = 4
EXPERTS_PER_GROUP = 8
N_EXPERTS = 32
LANES = 128
ROW_BLOCK = 512
SEG_ROWS = 16
MOE_TILE = 512
VMEM_LIMIT = 56 * 1024 * 1024


def _cparams(sem):
    return pltpu.CompilerParams(dimension_semantics=sem, vmem_limit_bytes=VMEM_LIMIT)


def _ada_kernel(c_ref, w_ref, b_ref, o_ref):
    c = c_ref[...]
    cs = c * jax.nn.sigmoid(c)
    o_ref[...] = jnp.dot(cs.astype(BF16), w_ref[...].astype(BF16), preferred_element_type=F32) + b_ref[...]


def _ada(c8, w_ada, b_ada):
    d, n = w_ada.shape
    tn = 1536
    return pl.pallas_call(
        _ada_kernel,
        grid=(n // tn,),
        in_specs=[pl.BlockSpec((8, d), lambda j: (0, 0)),
                  pl.BlockSpec((d, tn), lambda j: (0, j)),
                  pl.BlockSpec((1, tn), lambda j: (0, j))],
        out_specs=pl.BlockSpec((8, tn), lambda j: (0, j)),
        out_shape=jax.ShapeDtypeStruct((8, n), F32),
        compiler_params=_cparams(("arbitrary",)),
        name="ada",
    )(c8, w_ada, b_ada)


def _rms_mod(x, g, shift, scale):
    ms = jnp.mean(x * x, axis=-1, keepdims=True)
    return (x * lax.rsqrt(ms + RMS_EPS)) * g * (1.0 + scale) + shift


def _inproj_kernel(x_ref, mod_ref, g_ref, cos_ref, sin_ref, w_ref,
                   q0_ref, k0_ref, v0_ref, q1_ref, k1_ref, v1_ref, q2_ref, k2_ref, v2_ref,
                   qb_ref, kb_ref, vb_ref, sga_ref, sgb_ref, stg_ref, *, tm):
    h = _rms_mod(x_ref[...], g_ref[...], mod_ref[0, 0:1, :], mod_ref[0, 1:2, :])
    hb = h.astype(BF16)
    cos = cos_ref[...]
    sin = sin_ref[...]
    lane = lax.broadcasted_iota(I32, (tm, LANES), 1)
    first_half = (lane & 32) == 0
    low = lane < 64

    def proj(c0, width):
        return jnp.dot(hb, w_ref[:, c0:c0 + width], preferred_element_type=F32)

    def rope(t):
        rot = jnp.where(first_half, pltpu.roll(t, 96, 1), pltpu.roll(t, 32, 1))
        return t * cos + rot * sin

    def rope256(p):
        return jnp.concatenate([rope(p[:, :LANES]), rope(p[:, LANES:])], axis=1)

    def store_group(ref, val, d):
        if d == 1:
            ref[0, 0] = val.astype(BF16)
        else:
            for c in range(2):
                stg_ref[c] = val[:, c * LANES:(c + 1) * LANES]
            for r in range(d):
                for c in range(2):
                    ref[0, r, :, c * LANES:(c + 1) * LANES] = (
                        stg_ref[c, pl.ds(r, tm // d, stride=d), :].astype(BF16))

    q_refs = (q0_ref, q1_ref, q2_ref)
    k_refs = (k0_ref, k1_ref, k2_ref)
    v_refs = (v0_ref, v1_ref, v2_ref)
    for g, d in enumerate(DILATIONS):
        store_group(q_refs[g], rope256(proj(g * 256, 256)) * 0.125, d)
        store_group(k_refs[g], rope256(proj(768 + g * 256, 256)), d)
        store_group(v_refs[g], proj(1536 + g * 256, 256), d)
    for j in range(2):
        qb_ref[:, j * 256:(j + 1) * 256] = (rope256(proj(2304 + j * 256, 256)) * 0.125).astype(BF16)
    kv = proj(2816, 256)
    kb = rope(kv[:, :LANES])
    vb = kv[:, LANES:]
    kb_sw = pltpu.roll(kb, 64, 1)
    vb_sw = pltpu.roll(vb, 64, 1)
    kb_ref[:, :LANES] = jnp.where(low, kb, kb_sw).astype(BF16)
    kb_ref[:, LANES:] = jnp.where(low, kb_sw, kb).astype(BF16)
    vb_ref[:, :LANES] = jnp.where(low, vb, vb_sw).astype(BF16)
    vb_ref[:, LANES:] = jnp.where(low, vb_sw, vb).astype(BF16)
    for j in range(4):
        sga_ref[:, j * 256:(j + 1) * 256] = jax.nn.sigmoid(proj(3072 + j * 256, 256)).astype(BF16)
        sgb_ref[:, j * 256:(j + 1) * 256] = jax.nn.sigmoid(proj(4096 + j * 256, 256)).astype(BF16)


def _inproj(x2, mod3, g_mix, cos_t, sin_t, w_in_bf, batch, seq):
    n, dm = x2.shape
    tm = 512
    tpb = seq // tm
    grid = (n // tm,)
    row = lambda i: (i, 0)
    strided_specs, strided_shapes = [], []
    for d in DILATIONS:
        for _ in range(3):
            strided_specs.append(pl.BlockSpec((1, d, tm // d, DIL_GROUP_W), lambda i: (i // tpb, 0, i % tpb, 0)))
            strided_shapes.append(jax.ShapeDtypeStruct((batch, d, seq // d, DIL_GROUP_W), BF16))
    out_specs = strided_specs + [
        pl.BlockSpec((tm, 512), row), pl.BlockSpec((tm, 256), row), pl.BlockSpec((tm, 256), row),
        pl.BlockSpec((tm, dm), row), pl.BlockSpec((tm, dm), row)]
    out_shapes = strided_shapes + [
        jax.ShapeDtypeStruct((n, 512), BF16), jax.ShapeDtypeStruct((n, 256), BF16),
        jax.ShapeDtypeStruct((n, 256), BF16), jax.ShapeDtypeStruct((n, dm), BF16),
        jax.ShapeDtypeStruct((n, dm), BF16)]
    return pl.pallas_call(
        functools.partial(_inproj_kernel, tm=tm),
        grid=grid,
        in_specs=[pl.BlockSpec((tm, dm), row),
                  pl.BlockSpec((1, 6, dm), lambda i: (i // tpb, 0, 0)),
                  pl.BlockSpec((1, dm), lambda i: (0, 0)),
                  pl.BlockSpec((tm, LANES), row),
                  pl.BlockSpec((tm, LANES), row),
                  pl.BlockSpec(w_in_bf.shape, lambda i: (0, 0))],
        out_specs=out_specs,
        out_shape=out_shapes,
        scratch_shapes=[pltpu.VMEM((2, tm, LANES), F32)],
        compiler_params=_cparams(("arbitrary",)),
        name="inproj",
    )(x2, mod3, g_mix, cos_t, sin_t, w_in_bf)


def _split_heads(q2, low):
    zero = jnp.zeros_like(q2)
    return jnp.concatenate([jnp.where(low, q2, zero), jnp.where(low, zero, q2)], axis=0)


def _band_softmax(qst, k2, v2, mask, sinks):
    s = lax.dot_general(qst, k2, (((1,), (1,)), ((), ())), preferred_element_type=F32)
    s = jnp.where(mask, s, NEG_INF)
    rows, tk = s.shape
    m = jnp.max(s, axis=-1, keepdims=True)
    if sinks is not None:
        seg = rows // len(sinks)
        m = jnp.concatenate([jnp.maximum(m[h * seg:(h + 1) * seg], sk) for h, sk in enumerate(sinks)], axis=0)
    m = jnp.broadcast_to(m, (rows, LANES))
    e = jnp.concatenate([jnp.exp(s[:, c * LANES:(c + 1) * LANES] - m) for c in range(tk // LANES)], axis=1)
    v_ones = jnp.concatenate([v2, jnp.ones((tk, LANES), BF16)], axis=1)
    od = jnp.dot(e.astype(BF16), v_ones, preferred_element_type=F32)
    o, den = od[:, :LANES], od[:, LANES:]
    if sinks is not None:
        den = jnp.concatenate([den[h * seg:(h + 1) * seg] + jnp.exp(sk - m[h * seg:(h + 1) * seg])
                               for h, sk in enumerate(sinks)], axis=0)
    return o / den, m, den


def _band_mask(qs, ks, nstack, tq, tk, window):
    row = lax.broadcasted_iota(I32, (nstack * tq, tk), 0) & (tq - 1)
    col = lax.broadcasted_iota(I32, (nstack * tq, tk), 1)
    return jnp.abs((ks + col) - (qs + row)) <= window


def _dil_kernel(q_ref, k_ref, v_ref, o_ref, l_ref, *, length, tq, tk):
    low = lax.broadcasted_iota(I32, (tq, LANES), 1) < 64

    nq = length // tq

    def body(j, carry):
        r = j // nq
        qs = pl.multiple_of((j % nq) * tq, tq)
        ks = pl.multiple_of(jnp.clip(qs - DIL_HALF_WINDOW, 0, length - tk), DIL_HALF_WINDOW)
        mask = _band_mask(qs, ks, 2, tq, tk, DIL_HALF_WINDOW)
        for c in range(DIL_GROUP_W // LANES):
            cs = slice(c * LANES, (c + 1) * LANES)
            qst = _split_heads(q_ref[r, pl.ds(qs, tq), cs], low)
            o, m, den = _band_softmax(qst, k_ref[r, pl.ds(ks, tk), cs], v_ref[r, pl.ds(ks, tk), cs], mask, None)
            lse = m + jnp.log(den)
            o_ref[r, pl.ds(qs, tq), cs] = jnp.where(low, o[:tq], o[tq:]).astype(BF16)
            l_ref[r, pl.ds(qs, tq), cs] = jnp.where(low, lse[:tq], lse[tq:])
        return carry

    lax.fori_loop(0, q_ref.shape[0] * nq, body, 0, unroll=4)


def _dil_attention(q, k, v):
    batch, d, length, w = q.shape
    tq, tk = 128, 256
    spec = pl.BlockSpec((None, d, length, w), lambda b: (b, 0, 0, 0))
    return pl.pallas_call(
        functools.partial(_dil_kernel, length=length, tq=tq, tk=tk),
        grid=(batch,),
        in_specs=[spec, spec, spec],
        out_specs=[spec, spec],
        out_shape=[jax.ShapeDtypeStruct(q.shape, BF16), jax.ShapeDtypeStruct(q.shape, F32)],
        compiler_params=_cparams(("arbitrary",)),
        name=f"dil{d}",
    )(q, k, v)


def _swa_kernel(sink_ref, q_ref, k_ref, v_ref, o_ref, *, length, tq, tk):
    low = lax.broadcasted_iota(I32, (tq, LANES), 1) < 64
    nblk = q_ref.shape[1] // LANES

    def body(j, carry):
        qs = pl.multiple_of(j * tq, tq)
        ks = pl.multiple_of(jnp.clip(qs - SWA_WINDOW, 0, length - tk), SWA_WINDOW)
        mask = _band_mask(qs, ks, 2, tq, tk, SWA_WINDOW)
        for b in range(nblk):
            cs = slice((b // 2) * LANES, (b // 2 + 1) * LANES)
            bs = slice(b * LANES, (b + 1) * LANES)
            qst = _split_heads(q_ref[pl.ds(qs, tq), bs], low)
            sinks = (sink_ref[2 * b], sink_ref[2 * b + 1])
            o, _, _ = _band_softmax(qst, k_ref[pl.ds(ks, tk), cs], v_ref[pl.ds(ks, tk), cs], mask, sinks)
            o_ref[pl.ds(qs, tq), bs] = jnp.where(low, o[:tq], o[tq:]).astype(BF16)
        return carry

    lax.fori_loop(0, length // tq, body, 0, unroll=4)


def _swa_attention(sink, q, k, v):
    batch, length, qw = q.shape
    tq, tk = 128, 384
    return pl.pallas_call(
        functools.partial(_swa_kernel, length=length, tq=tq, tk=tk),
        grid=(batch,),
        in_specs=[pl.BlockSpec(memory_space=pltpu.SMEM),
                  pl.BlockSpec((None, length, qw), lambda b: (b, 0, 0)),
                  pl.BlockSpec((None, length, k.shape[2]), lambda b: (b, 0, 0)),
                  pl.BlockSpec((None, length, v.shape[2]), lambda b: (b, 0, 0))],
        out_specs=pl.BlockSpec((None, length, qw), lambda b: (b, 0, 0)),
        out_shape=jax.ShapeDtypeStruct(q.shape, BF16),
        compiler_params=_cparams(("arbitrary",)),
        name="swa",
    )(sink, q, k, v)


def _route_rows(logits):
    lane = lax.broadcasted_iota(I32, logits.shape, 1).astype(F32)
    big = 1e9
    is_g = lane < N_GROUPS
    gl = jnp.where(is_g, logits, NEG_INF)
    gmax = jnp.max(gl, axis=-1, keepdims=True)
    gsel = jnp.min(jnp.where(is_g & (gl == gmax), lane, big), axis=-1, keepdims=True)
    gw = 1.0 / jnp.sum(jnp.where(is_g, jnp.exp(gl - gmax), 0.0), axis=-1, keepdims=True)
    e_lo = N_GROUPS + gsel * EXPERTS_PER_GROUP
    in_grp = (lane >= e_lo) & (lane < e_lo + EXPERTS_PER_GROUP)
    el = jnp.where(in_grp, logits, NEG_INF)
    m1 = jnp.max(el, axis=-1, keepdims=True)
    i1 = jnp.min(jnp.where(in_grp & (el == m1), lane, big), axis=-1, keepdims=True)
    el2 = jnp.where(lane == i1, NEG_INF, el)
    m2 = jnp.max(el2, axis=-1, keepdims=True)
    i2 = jnp.min(jnp.where(in_grp & (lane != i1) & (el2 == m2), lane, big), axis=-1, keepdims=True)
    t = jnp.exp(m2 - m1)
    tw1 = gw / (1.0 + t)
    tw2 = gw * t / (1.0 + t)
    out = jnp.where(lane == 0, tw1, 0.0)
    out = jnp.where(lane == 1, tw2, out)
    out = jnp.where(lane == 2, i1 - N_GROUPS, out)
    return jnp.where(lane == 3, i2 - N_GROUPS, out)


def _outproj_kernel(o0_ref, l0_ref, o1_ref, l1_ref, o2_ref, l2_ref, ob_ref, sga_ref, sgb_ref, x_ref,
                    mod_ref, g_ref, wa_ref, wb_ref, wo_ref, wr_ref, br_ref,
                    x1_ref, h2_ref, route_ref, len_ref,
                    so1_ref, sl1_ref, so2_ref, sl2_ref, *, tm, sub):
    dm = x_ref.shape[1]
    for (o_ref, l_ref, so_ref, sl_ref, d) in ((o1_ref, l1_ref, so1_ref, sl1_ref, DILATIONS[1]),
                                              (o2_ref, l2_ref, so2_ref, sl2_ref, DILATIONS[2])):
        for r in range(d):
            for c in range(2):
                cs = slice(c * LANES, (c + 1) * LANES)
                so_ref[c, pl.ds(r, tm // d, stride=d), :] = o_ref[0, r, :, cs].astype(F32)
                sl_ref[c, pl.ds(r, tm // d, stride=d), :] = l_ref[0, r, :, cs]
    mr = lax.broadcasted_iota(I32, (LANES, LANES), 0)
    mc = lax.broadcasted_iota(I32, (LANES, LANES), 1)
    move_hi = jnp.where(((mr < 2) & (mc == 2 * mr)) | ((mr >= 2) & (mr < 4) & (mc == mr + 2)), 1.0, 0.0).astype(BF16)
    move_lo = jnp.where((mr < 2) & (mc == 2 * mr + 1), 1.0, 0.0).astype(BF16)
    for t in range(tm // sub):
        rs = slice(t * sub, (t + 1) * sub)
        both = lambda ref: jnp.concatenate([ref[0, rs, :], ref[1, rs, :]], axis=1)
        o0, l0 = o0_ref[0, 0, rs, :].astype(F32), l0_ref[0, 0, rs, :]
        o1, l1, o2, l2 = both(so1_ref), both(sl1_ref), both(so2_ref), both(sl2_ref)
        mx = jnp.maximum(jnp.maximum(l0, l1), l2)
        w0, w1, w2 = jnp.exp(l0 - mx), jnp.exp(l1 - mx), jnp.exp(l2 - mx)
        o_a = (w0 * o0 + w1 * o1 + w2 * o2) / (w0 + w1 + w2)
        y_a = jnp.dot(o_a.astype(BF16), wa_ref[...], preferred_element_type=F32)
        y_b = jnp.dot(ob_ref[rs, :], wb_ref[...], preferred_element_type=F32)
        merged = sga_ref[rs, :].astype(F32) * y_a + sgb_ref[rs, :].astype(F32) * y_b
        mix = jnp.dot(merged.astype(BF16), wo_ref[...], preferred_element_type=F32)
        x1 = x_ref[rs, :] + mod_ref[0, 2:3, :] * mix
        x1_ref[rs, :] = x1
        h2 = _rms_mod(x1, g_ref[...], mod_ref[0, 3:4, :], mod_ref[0, 4:5, :]).astype(BF16)
        h2_ref[rs, :dm] = h2
        logits = jnp.dot(h2, wr_ref[...], preferred_element_type=F32) + br_ref[...]
        rt = _route_rows(logits)
        route_ref[rs, :] = rt
        hi = rt.astype(BF16)
        lo = (rt - hi.astype(F32)).astype(BF16)
        aux = (jnp.dot(hi, move_hi, preferred_element_type=F32) + jnp.dot(lo, move_lo, preferred_element_type=F32))
        h2_ref[rs, dm:] = aux.astype(BF16)

    part = route_ref[...]
    e1, e2 = part[:, 2:3], part[:, 3:4]
    lane = lax.broadcasted_iota(I32, (tm, LANES), 1).astype(F32)
    onehot = jnp.where((lane == e1) | (lane == e2), 1.0, 0.0)
    rr = lax.broadcasted_iota(I32, (tm, tm), 0)
    cc = lax.broadcasted_iota(I32, (tm, tm), 1)
    tri = jnp.where(rr > cc, 1.0, 0.0).astype(BF16)
    prefix = jnp.dot(tri, onehot.astype(BF16), preferred_element_type=F32)
    cnt = jnp.sum(onehot, axis=0, keepdims=True)
    seg_len = jnp.ceil(cnt * (1.0 / SEG_ROWS)) * SEG_ROWS
    ur = lax.broadcasted_iota(I32, (LANES, LANES), 0)
    uc = lax.broadcasted_iota(I32, (LANES, LANES), 1)
    upper = jnp.where(ur < uc, 1.0, 0.0).astype(BF16)
    seg_off = jnp.dot(jnp.broadcast_to(seg_len, (8, LANES)).astype(BF16), upper, preferred_element_type=F32)[0:1, :]
    slot_map = seg_off + prefix
    s1 = jnp.sum(jnp.where(lane == e1, slot_map, 0.0), axis=-1, keepdims=True)
    s2 = jnp.sum(jnp.where(lane == e2, slot_map, 0.0), axis=-1, keepdims=True)
    len_ref[0] = jnp.broadcast_to(seg_len, (8, LANES))
    route_ref[...] = jnp.where(lane == 2, s1, jnp.where(lane == 3, s2, part))


def _outproj(dil_outs, ob, sga, sgb, x2, mod3, g_ffn, wa, wb, wo, wr, br, batch, seq):
    n, dm = x2.shape
    tm = MOE_TILE
    tpb = seq // tm
    row = lambda i: (i, 0)
    const = lambda i: (0, 0)
    in_specs = []
    args = []
    for (o, l), d in zip(dil_outs, DILATIONS):
        spec = pl.BlockSpec((1, d, tm // d, DIL_GROUP_W), lambda i: (i // tpb, 0, i % tpb, 0))
        in_specs += [spec, spec]
        args += [o, l]
    in_specs += [pl.BlockSpec((tm, ob.shape[1]), row), pl.BlockSpec((tm, dm), row), pl.BlockSpec((tm, dm), row),
                 pl.BlockSpec((tm, dm), row),
                 pl.BlockSpec((1, 6, dm), lambda i: (i // tpb, 0, 0)),
                 pl.BlockSpec((1, dm), const),
                 pl.BlockSpec(wa.shape, const), pl.BlockSpec(wb.shape, const), pl.BlockSpec(wo.shape, const),
                 pl.BlockSpec(wr.shape, const), pl.BlockSpec(br.shape, const)]
    args += [ob, sga, sgb, x2, mod3, g_ffn, wa, wb, wo, wr, br]
    return pl.pallas_call(
        functools.partial(_outproj_kernel, tm=tm, sub=512),
        grid=(n // tm,),
        in_specs=in_specs,
        out_specs=[pl.BlockSpec((tm, dm), row), pl.BlockSpec((tm, dm + LANES), row), pl.BlockSpec((tm, LANES), row),
                   pl.BlockSpec((1, 8, LANES), lambda i: (i, 0, 0))],
        out_shape=[jax.ShapeDtypeStruct((n, dm), F32), jax.ShapeDtypeStruct((n, dm + LANES), BF16),
                   jax.ShapeDtypeStruct((n, LANES), F32), jax.ShapeDtypeStruct((n // tm, 8, LANES), F32)],
        scratch_shapes=[pltpu.VMEM((2, tm, LANES), F32)] * 4,
        compiler_params=_cparams(("arbitrary",)),
        name="outproj",
    )(*args)


def _segment_copies(i, base_ref, nch_ref, make_copy):
    def expert_body(e, off):
        nch = nch_ref[i * N_EXPERTS + e]
        base = base_ref[i * N_EXPERTS + e]

        def chunk(c, carry):
            make_copy(pl.multiple_of(off + c * SEG_ROWS, SEG_ROWS),
                      pl.multiple_of(base + c * SEG_ROWS, SEG_ROWS)).start()
            return carry

        lax.fori_loop(0, nch, chunk, 0)
        return off + nch * SEG_ROWS

    total_rows = lax.fori_loop(0, N_EXPERTS, expert_body, 0)
    return total_rows // SEG_ROWS


def _tile_chunks(i, nch_ref):
    return lax.fori_loop(0, N_EXPERTS, lambda e, acc: acc + nch_ref[i * N_EXPERTS + e], 0)


def _wait_copies(count, copy):
    def wait_one(c, carry):
        copy.wait()
        return carry

    lax.fori_loop(0, count, wait_one, 0)


def _dispatch_kernel(base_ref, nch_ref, tstart_ref, tch_ref, nused_ref, h_ref, route_ref, xr_ref,
                     sorted_ref, zero_ref, sem, *, tm, nslots, nblk):
    i = pl.program_id(0)
    last = pl.num_programs(0) - 1
    slot = i & 1

    def seg_copy(s):
        return lambda src, dst: pltpu.make_async_copy(
            sorted_ref.at[s, pl.ds(src, SEG_ROWS), :], xr_ref.at[pl.ds(dst, SEG_ROWS), :], sem.at[s])

    @pl.when(i >= 2)
    def _():
        _wait_copies(_tile_chunks(i - 2, nch_ref), seg_copy(slot)(0, 0))

    pr = lax.broadcasted_iota(I32, (8, LANES), 0)
    pc = lax.broadcasted_iota(I32, (8, LANES), 1)
    lane_pick = jnp.where(pc == pr + 2, 1.0, 0.0)
    slots_t = lax.dot_general(lane_pick, route_ref[...], (((1,), (1,)), ((), ())), preferred_element_type=F32,
                              precision=lax.Precision.HIGHEST)
    srow = lax.broadcasted_iota(I32, (nslots, tm), 0).astype(F32)
    pick = jnp.where((srow == slots_t[0:1, :]) | (srow == slots_t[1:2, :]), 1.0, 0.0).astype(BF16)
    sorted_ref[slot] = jnp.dot(pick, h_ref[...], preferred_element_type=F32).astype(BF16)
    nchunks = _segment_copies(i, base_ref, nch_ref, seg_copy(slot))

    @pl.when(i == last)
    def _():
        _wait_copies(nchunks, seg_copy(slot)(0, 0))

        @pl.when(i >= 1)
        def _():
            _wait_copies(_tile_chunks(i - 1, nch_ref), seg_copy(1 - slot)(0, 0))

        zero_ref[...] = jnp.zeros_like(zero_ref)
        zsem = sem.at[0]

        def zero_seg(dst):
            return pltpu.make_async_copy(zero_ref.at[pl.ds(0, SEG_ROWS), :], xr_ref.at[pl.ds(dst, SEG_ROWS), :], zsem)

        def tail(e, count):
            def chunk(c, carry):
                zero_seg(pl.multiple_of(tstart_ref[e] + c * SEG_ROWS, SEG_ROWS)).start()
                return carry

            lax.fori_loop(0, tch_ref[e], chunk, 0)
            return count + tch_ref[e]

        _wait_copies(lax.fori_loop(0, N_EXPERTS, tail, 0), zero_seg(0))

        def zero_blk(b):
            return pltpu.make_async_copy(zero_ref, xr_ref.at[pl.ds(pl.multiple_of(b * ROW_BLOCK, ROW_BLOCK), ROW_BLOCK), :], zsem)

        def blk(b, carry):
            zero_blk(b).start()
            return carry

        lax.fori_loop(nused_ref[0], nblk, blk, 0)
        _wait_copies(nblk - nused_ref[0], zero_blk(0))


def _dispatch(tables, h2, route, nblk, tm, nslots):
    n, width = h2.shape
    rows = nblk * ROW_BLOCK
    grid_spec = pltpu.PrefetchScalarGridSpec(
        num_scalar_prefetch=5,
        grid=(n // tm,),
        in_specs=[pl.BlockSpec((tm, width), lambda i, *_: (i, 0)),
                  pl.BlockSpec((tm, LANES), lambda i, *_: (i, 0))],
        out_specs=pl.BlockSpec(memory_space=pl.ANY),
        scratch_shapes=[pltpu.VMEM((2, nslots, width), BF16), pltpu.VMEM((ROW_BLOCK, width), BF16),
                        pltpu.SemaphoreType.DMA((2,))],
    )
    return pl.pallas_call(
        functools.partial(_dispatch_kernel, tm=tm, nslots=nslots, nblk=nblk),
        grid_spec=grid_spec,
        out_shape=jax.ShapeDtypeStruct((rows, width), BF16),
        compiler_params=_cparams(("arbitrary",)),
        name="dispatch",
    )(*tables, h2, route)


def _experts_kernel(first_ref, count_ref, wg_ref, wu_ref, wd_ref, x_hbm, y_hbm,
                    xbuf, ybuf, wgb_ref, wub_ref, wdb_ref, xsem, ysem, *, nblk):
    e = pl.program_id(0)
    first = first_ref[e]
    count = count_ref[e]
    dm = ybuf.shape[2]

    def x_copy(b, slot):
        rows = pl.ds(pl.multiple_of((first + b) * ROW_BLOCK, ROW_BLOCK), ROW_BLOCK)
        return pltpu.make_async_copy(x_hbm.at[rows, :], xbuf.at[slot], xsem.at[slot])

    def y_copy(b, slot):
        rows = pl.ds(pl.multiple_of((first + b) * ROW_BLOCK, ROW_BLOCK), ROW_BLOCK)
        return pltpu.make_async_copy(ybuf.at[slot], y_hbm.at[rows, :], ysem.at[slot])

    @pl.when((e == 0) & (count > 0))
    def _():
        x_copy(0, 0).start(priority=1)

    wgb_ref[...] = wg_ref[0].astype(BF16)
    wub_ref[...] = wu_ref[0].astype(BF16)
    wdb_ref[...] = wd_ref[0].astype(BF16)

    def body(b, carry):
        slot = b & 1
        x_copy(b, slot).wait()

        @pl.when(b + 1 < count)
        def _():
            x_copy(b + 1, 1 - slot).start(priority=1)

        @pl.when(b >= 2)
        def _():
            y_copy(b - 2, slot).wait()

        xb = xbuf[slot, :, :dm]
        aux = xbuf[slot, :, dm:].astype(F32)
        w = jnp.where(aux[:, 4:5] == e.astype(F32), aux[:, 0:1] + aux[:, 1:2], aux[:, 2:3] + aux[:, 3:4])
        g = jnp.dot(xb, wgb_ref[...], preferred_element_type=F32)
        u = jnp.dot(xb, wub_ref[...], preferred_element_type=F32)
        a = (g * jax.nn.sigmoid(g)) * u
        y = jnp.dot(a.astype(BF16), wdb_ref[...], preferred_element_type=F32)
        ybuf[slot] = (y * w).astype(BF16)
        y_copy(b, slot).start()
        return carry

    lax.fori_loop(0, count, body, 0)

    nxt = jnp.minimum(e + 1, pl.num_programs(0) - 1)

    @pl.when((e + 1 < pl.num_programs(0)) & (count_ref[nxt] > 0))
    def _():
        rows = pl.ds(pl.multiple_of(first_ref[nxt] * ROW_BLOCK, ROW_BLOCK), ROW_BLOCK)
        pltpu.make_async_copy(x_hbm.at[rows, :], xbuf.at[0], xsem.at[0]).start(priority=1)

    @pl.when(count >= 2)
    def _():
        y_copy(count - 2, count & 1).wait()

    @pl.when(count >= 1)
    def _():
        y_copy(count - 1, (count - 1) & 1).wait()

    @pl.when(e == pl.num_programs(0) - 1)
    def _():
        ybuf[0] = jnp.zeros((ROW_BLOCK, dm), BF16)
        used = first + count

        def zero_blk(b):
            rows = pl.ds(pl.multiple_of(b * ROW_BLOCK, ROW_BLOCK), ROW_BLOCK)
            return pltpu.make_async_copy(ybuf.at[0], y_hbm.at[rows, :], ysem.at[0])

        def start(b, carry):
            zero_blk(b).start()
            return carry

        lax.fori_loop(used, nblk, start, 0)
        _wait_copies(nblk - used, zero_blk(0))


def _experts(first_blk, count_blk, x_rows, w_gate, w_up, w_down):
    rows, width = x_rows.shape
    n_exp, dm, de = w_gate.shape
    nblk = rows // ROW_BLOCK
    wmap = lambda e, *_: (e, 0, 0)
    grid_spec = pltpu.PrefetchScalarGridSpec(
        num_scalar_prefetch=2,
        grid=(n_exp,),
        in_specs=[pl.BlockSpec((1, dm, de), wmap),
                  pl.BlockSpec((1, dm, de), wmap),
                  pl.BlockSpec((1, de, dm), wmap),
                  pl.BlockSpec(memory_space=pl.ANY)],
        out_specs=pl.BlockSpec(memory_space=pl.ANY),
        scratch_shapes=[pltpu.VMEM((2, ROW_BLOCK, width), BF16), pltpu.VMEM((2, ROW_BLOCK, dm), BF16),
                        pltpu.VMEM((dm, de), BF16), pltpu.VMEM((dm, de), BF16), pltpu.VMEM((de, dm), BF16),
                        pltpu.SemaphoreType.DMA((2,)), pltpu.SemaphoreType.DMA((2,))],
    )
    return pl.pallas_call(
        functools.partial(_experts_kernel, nblk=nblk),
        grid_spec=grid_spec,
        out_shape=jax.ShapeDtypeStruct((rows, dm), BF16),
        compiler_params=_cparams(("arbitrary",)),
        name="experts",
    )(first_blk, count_blk, w_gate, w_up, w_down, x_rows)


def _combine_kernel(base_ref, nch_ref, route_ref, x1_ref, mod_ref, g_ref, y_ref, o_ref, ys_ref, sem, *, tm, nslots):
    i = pl.program_id(0)
    slot = i & 1

    def seg_copy(s):
        return lambda dst, src: pltpu.make_async_copy(
            y_ref.at[pl.ds(src, SEG_ROWS), :], ys_ref.at[s, pl.ds(dst, SEG_ROWS), :], sem.at[s])

    @pl.when(i == 0)
    def _():
        ys_ref[...] = jnp.zeros_like(ys_ref)
        _segment_copies(0, base_ref, nch_ref, seg_copy(0))

    @pl.when(i + 1 < pl.num_programs(0))
    def _():
        _segment_copies(i + 1, base_ref, nch_ref, seg_copy(1 - slot))

    _wait_copies(_tile_chunks(i, nch_ref), seg_copy(slot)(0, 0))
    route = route_ref[...]
    scol = lax.broadcasted_iota(I32, (tm, nslots), 1).astype(F32)
    pick = jnp.where((scol == route[:, 2:3]) | (scol == route[:, 3:4]), 1.0, 0.0).astype(BF16)
    moe = jnp.dot(pick, ys_ref[slot], preferred_element_type=F32)
    x = x1_ref[...] + mod_ref[0, 5:6, :] * moe
    ms = jnp.mean(x * x, axis=-1, keepdims=True)
    o_ref[...] = (x * lax.rsqrt(ms + RMS_EPS)) * g_ref[...]


def _combine(base, nch, route, x1, mod3, g_final, y_rows, seq, tm, nslots):
    n, dm = x1.shape
    tpb = seq // tm
    grid_spec = pltpu.PrefetchScalarGridSpec(
        num_scalar_prefetch=2,
        grid=(n // tm,),
        in_specs=[pl.BlockSpec((tm, LANES), lambda i, *_: (i, 0)),
                  pl.BlockSpec((tm, dm), lambda i, *_: (i, 0)),
                  pl.BlockSpec((1, 6, dm), lambda i, *_: (i // tpb, 0, 0)),
                  pl.BlockSpec((1, dm), lambda i, *_: (0, 0)),
                  pl.BlockSpec(memory_space=pl.ANY)],
        out_specs=pl.BlockSpec((tm, dm), lambda i, *_: (i, 0)),
        scratch_shapes=[pltpu.VMEM((2, nslots, dm), BF16), pltpu.SemaphoreType.DMA((2,))],
    )
    return pl.pallas_call(
        functools.partial(_combine_kernel, tm=tm, nslots=nslots),
        grid_spec=grid_spec,
        out_shape=jax.ShapeDtypeStruct((n, dm), F32),
        compiler_params=_cparams(("arbitrary",)),
        name="combine",
    )(base, nch, route, x1, mod3, g_final, y_rows)


def _rope_tables(positions):
    half = HEAD_DIM // 2
    inv_freq = ROPE_THETA ** (-jnp.arange(half, dtype=F32) * (2.0 / HEAD_DIM))
    ang = positions.astype(F32).reshape(-1, 1) * inv_freq
    cos, sin = jnp.cos(ang), jnp.sin(ang)
    cos_t = jnp.concatenate([cos, cos, cos, cos], axis=1)
    sin_t = jnp.concatenate([-sin, sin, -sin, sin], axis=1)
    return cos_t, sin_t


def kernel(x, c, positions, w_ada, b_ada, g_mix, w_in, sink_logits, w_branch_a, w_branch_b, w_out, g_ffn,
           w_group, b_group, w_route, b_route, w_expert_gate, w_expert_up, w_expert_down, g_final):
    batch, seq, dm = x.shape
    n = batch * seq
    assert w_ada.shape[0] == 1, "one layer"
    x2 = x.reshape(n, dm)

    c8 = jnp.pad(c, ((0, 8 - batch), (0, 0)))
    mod = _ada(c8, w_ada[0], b_ada[0].reshape(1, -1))
    mod3 = mod[:batch].reshape(batch, 6, dm)

    cos_t, sin_t = _rope_tables(positions)
    outs = _inproj(x2, mod3, g_mix[0].reshape(1, dm), cos_t, sin_t, w_in[0].astype(BF16), batch, seq)
    qkv = outs[:9]
    qb, kb, vb, sga, sgb = outs[9:]

    dil_outs = [_dil_attention(qkv[3 * g], qkv[3 * g + 1], qkv[3 * g + 2]) for g in range(len(DILATIONS))]
    ob = _swa_attention(sink_logits[0], qb.reshape(batch, seq, -1), kb.reshape(batch, seq, -1),
                        vb.reshape(batch, seq, -1)).reshape(n, -1)

    pad = LANES - N_GROUPS - N_EXPERTS
    wr = jnp.concatenate([w_group[0], w_route[0], jnp.zeros((dm, pad), F32)], axis=1).astype(BF16)
    br = jnp.concatenate([b_group[0], b_route[0], jnp.zeros((pad,), F32)]).reshape(1, LANES)
    x1, h2, route, seg_lens = _outproj(dil_outs, ob, sga, sgb, x2, mod3, g_ffn[0].reshape(1, dm),
                                       w_branch_a[0].astype(BF16), w_branch_b[0].astype(BF16), w_out[0].astype(BF16),
                                       wr, br, batch, seq)

    ntiles = n // MOE_TILE
    nslots = 2 * MOE_TILE + N_EXPERTS * SEG_ROWS
    nblk = -(-(2 * n + ntiles * N_EXPERTS * (SEG_ROWS - 1)) // ROW_BLOCK) + N_EXPERTS
    lens = seg_lens[:, 0, :N_EXPERTS].astype(I32)
    tot = jnp.sum(lens, axis=0)
    padded = (tot + ROW_BLOCK - 1) // ROW_BLOCK * ROW_BLOCK
    pend = jnp.cumsum(padded)
    pstart = pend - padded
    base = (pstart[None, :] + jnp.cumsum(lens, axis=0) - lens).reshape(-1)
    nch = (lens // SEG_ROWS).reshape(-1)
    nused = pend[-1:] // ROW_BLOCK
    tables = (base, nch, pstart + tot, (padded - tot) // SEG_ROWS, nused)

    x_rows = _dispatch(tables, h2, route, nblk, MOE_TILE, nslots)
    y_rows = _experts(pstart // ROW_BLOCK, padded // ROW_BLOCK, x_rows,
                      w_expert_gate[0], w_expert_up[0], w_expert_down[0])
    out = _combine(base, nch, route, x1, mod3, g_final.reshape(1, dm), y_rows, seq, MOE_TILE, nslots)
    return out.reshape(batch, seq, dm)
```

```python
import functools

import jax
import jax.numpy as jnp
from jax import lax
from jax.experimental import pallas as pl
from jax.experimental.pallas import tpu as pltpu

F32 = jnp.float32
BF16 = jnp.bfloat16
I32 = jnp.int32

HEAD_DIM = 64
ROPE_THETA = 10000.0
RMS_EPS = 1e-6
NEG_INF = -1e30
DILATIONS = (1, 4, 16)
DIL_HALF_WINDOW = 64
DIL_GROUP_W = 256
SWA_WINDOW = 128
N_GROUPS = 4
EXPERTS_PER_GROUP = 8
N_EXPERTS = 32
LANES = 128
ROW_BLOCK = 512
SEG_ROWS = 16
MOE_TILE = 512
VMEM_LIMIT = 56 * 1024 * 1024


def _cparams(sem):
    return pltpu.CompilerParams(dimension_semantics=sem, vmem_limit_bytes=VMEM_LIMIT)


def _ada_kernel(c_ref, w_ref, b_ref, o_ref):
    c = c_ref[...]
    cs = c * jax.nn.sigmoid(c)
    o_ref[...] = jnp.dot(cs.astype(BF16), w_ref[...].astype(BF16), preferred_element_type=F32) + b_ref[...]


def _ada(c8, w_ada, b_ada):
    d, n = w_ada.shape
    tn = 1536
    return pl.pallas_call(
        _ada_kernel,
        grid=(n // tn,),
        in_specs=[pl.BlockSpec((8, d), lambda j: (0, 0)),
                  pl.BlockSpec((d, tn), lambda j: (0, j)),
                  pl.BlockSpec((1, tn), lambda j: (0, j))],
        out_specs=pl.BlockSpec((8, tn), lambda j: (0, j)),
        out_shape=jax.ShapeDtypeStruct((8, n), F32),
        compiler_params=_cparams(("arbitrary",)),
        name="ada",
    )(c8, w_ada, b_ada)


def _rms_mod(x, g, shift, scale):
    ms = jnp.mean(x * x, axis=-1, keepdims=True)
    return (x * lax.rsqrt(ms + RMS_EPS)) * g * (1.0 + scale) + shift


def _inproj_kernel(x_ref, mod_ref, g_ref, cos_ref, sin_ref, w_ref,
                   q0_ref, k0_ref, v0_ref, q1_ref, k1_ref, v1_ref, q2_ref, k2_ref, v2_ref,
                   qb_ref, kb_ref, vb_ref, sga_ref, sgb_ref, stg_ref, *, tm):
    h = _rms_mod(x_ref[...], g_ref[...], mod_ref[0, 0:1, :], mod_ref[0, 1:2, :])
    hb = h.astype(BF16)
    cos = cos_ref[...]
    sin = sin_ref[...]
    lane = lax.broadcasted_iota(I32, (tm, LANES), 1)
    first_half = (lane & 32) == 0
    low = lane < 64

    def proj(c0, width):
        return jnp.dot(hb, w_ref[:, c0:c0 + width], preferred_element_type=F32)

    def rope(t):
        rot = jnp.where(first_half, pltpu.roll(t, 96, 1), pltpu.roll(t, 32, 1))
        return t * cos + rot * sin

    def rope256(p):
        return jnp.concatenate([rope(p[:, :LANES]), rope(p[:, LANES:])], axis=1)

    def store_group(ref, val, d):
        if d == 1:
            ref[0, 0] = val.astype(BF16)
        else:
            for c in range(2):
                stg_ref[c] = val[:, c * LANES:(c + 1) * LANES]
            for r in range(d):
                for c in range(2):
                    ref[0, r, :, c * LANES:(c + 1) * LANES] = (
                        stg_ref[c, pl.ds(r, tm // d, stride=d), :].astype(BF16))

    q_refs = (q0_ref, q1_ref, q2_ref)
    k_refs = (k0_ref, k1_ref, k2_ref)
    v_refs = (v0_ref, v1_ref, v2_ref)
    for g, d in enumerate(DILATIONS):
        store_group(q_refs[g], rope256(proj(g * 256, 256)) * 0.125, d)
        store_group(k_refs[g], rope256(proj(768 + g * 256, 256)), d)
        store_group(v_refs[g], proj(1536 + g * 256, 256), d)
    for j in range(2):
        qb_ref[:, j * 256:(j + 1) * 256] = (rope256(proj(2304 + j * 256, 256)) * 0.125).astype(BF16)
    kv = proj(2816, 256)
    kb = rope(kv[:, :LANES])
    vb = kv[:, LANES:]
    kb_sw = pltpu.roll(kb, 64, 1)
    vb_sw = pltpu.roll(vb, 64, 1)
    kb_ref[:, :LANES] = jnp.where(low, kb, kb_sw).astype(BF16)
    kb_ref[:, LANES:] = jnp.where(low, kb_sw, kb).astype(BF16)
    vb_ref[:, :LANES] = jnp.where(low, vb, vb_sw).astype(BF16)
    vb_ref[:, LANES:] = jnp.where(low, vb_sw, vb).astype(BF16)
    for j in range(4):
        sga_ref[:, j * 256:(j + 1) * 256] = jax.nn.sigmoid(proj(3072 + j * 256, 256)).astype(BF16)
        sgb_ref[:, j * 256:(j + 1) * 256] = jax.nn.sigmoid(proj(4096 + j * 256, 256)).astype(BF16)


def _inproj(x2, mod3, g_mix, cos_t, sin_t, w_in_bf, batch, seq):
    n, dm = x2.shape
    tm = 512
    tpb = seq // tm
    grid = (n // tm,)
    row = lambda i: (i, 0)
    strided_specs, strided_shapes = [], []
    for d in DILATIONS:
        for _ in range(3):
            strided_specs.append(pl.BlockSpec((1, d, tm // d, DIL_GROUP_W), lambda i: (i // tpb, 0, i % tpb, 0)))
            strided_shapes.append(jax.ShapeDtypeStruct((batch, d, seq // d, DIL_GROUP_W), BF16))
    out_specs = strided_specs + [
        pl.BlockSpec((tm, 512), row), pl.BlockSpec((tm, 256), row), pl.BlockSpec((tm, 256), row),
        pl.BlockSpec((tm, dm), row), pl.BlockSpec((tm, dm), row)]
    out_shapes = strided_shapes + [
        jax.ShapeDtypeStruct((n, 512), BF16), jax.ShapeDtypeStruct((n, 256), BF16),
        jax.ShapeDtypeStruct((n, 256), BF16), jax.ShapeDtypeStruct((n, dm), BF16),
        jax.ShapeDtypeStruct((n, dm), BF16)]
    return pl.pallas_call(
        functools.partial(_inproj_kernel, tm=tm),
        grid=grid,
        in_specs=[pl.BlockSpec((tm, dm), row),
                  pl.BlockSpec((1, 6, dm), lambda i: (i // tpb, 0, 0)),
                  pl.BlockSpec((1, dm), lambda i: (0, 0)),
                  pl.BlockSpec((tm, LANES), row),
                  pl.BlockSpec((tm, LANES), row),
                  pl.BlockSpec(w_in_bf.shape, lambda i: (0, 0))],
        out_specs=out_specs,
        out_shape=out_shapes,
        scratch_shapes=[pltpu.VMEM((2, tm, LANES), F32)],
        compiler_params=_cparams(("arbitrary",)),
        name="inproj",
    )(x2, mod3, g_mix, cos_t, sin_t, w_in_bf)


def _split_heads(q2, low):
    zero = jnp.zeros_like(q2)
    return jnp.concatenate([jnp.where(low, q2, zero), jnp.where(low, zero, q2)], axis=0)


def _band_softmax(qst, k2, v2, mask, sinks):
    s = lax.dot_general(qst, k2, (((1,), (1,)), ((), ())), preferred_element_type=F32)
    s = jnp.where(mask, s, NEG_INF)
    rows, tk = s.shape
    m = jnp.max(s, axis=-1, keepdims=True)
    if sinks is not None:
        seg = rows // len(sinks)
        m = jnp.concatenate([jnp.maximum(m[h * seg:(h + 1) * seg], sk) for h, sk in enumerate(sinks)], axis=0)
    m = jnp.broadcast_to(m, (rows, LANES))
    e = jnp.concatenate([jnp.exp(s[:, c * LANES:(c + 1) * LANES] - m) for c in range(tk // LANES)], axis=1)
    v_ones = jnp.concatenate([v2, jnp.ones((tk, LANES), BF16)], axis=1)
    od = jnp.dot(e.astype(BF16), v_ones, preferred_element_type=F32)
    o, den = od[:, :LANES], od[:, LANES:]
    if sinks is not None:
        den = jnp.concatenate([den[h * seg:(h + 1) * seg] + jnp.exp(sk - m[h * seg:(h + 1) * seg])
                               for h, sk in enumerate(sinks)], axis=0)
    return o / den, m, den


def _band_mask(qs, ks, nstack, tq, tk, window):
    row = lax.broadcasted_iota(I32, (nstack * tq, tk), 0) & (tq - 1)
    col = lax.broadcasted_iota(I32, (nstack * tq, tk), 1)
    return jnp.abs((ks + col) - (qs + row)) <= window


def _dil_kernel(q_ref, k_ref, v_ref, o_ref, l_ref, *, length, tq, tk):
    low = lax.broadcasted_iota(I32, (tq, LANES), 1) < 64

    nq = length // tq

    def body(j, carry):
        r = j // nq
        qs = pl.multiple_of((j % nq) * tq, tq)
        ks = pl.multiple_of(jnp.clip(qs - DIL_HALF_WINDOW, 0, length - tk), DIL_HALF_WINDOW)
        mask = _band_mask(qs, ks, 2, tq, tk, DIL_HALF_WINDOW)
        for c in range(DIL_GROUP_W // LANES):
            cs = slice(c * LANES, (c + 1) * LANES)
            qst = _split_heads(q_ref[r, pl.ds(qs, tq), cs], low)
            o, m, den = _band_softmax(qst, k_ref[r, pl.ds(ks, tk), cs], v_ref[r, pl.ds(ks, tk), cs], mask, None)
            lse = m + jnp.log(den)
            o_ref[r, pl.ds(qs, tq), cs] = jnp.where(low, o[:tq], o[tq:]).astype(BF16)
            l_ref[r, pl.ds(qs, tq), cs] = jnp.where(low, lse[:tq], lse[tq:])
        return carry

    lax.fori_loop(0, q_ref.shape[0] * nq, body, 0, unroll=4)


def _dil_attention(q, k, v):
    batch, d, length, w = q.shape
    tq, tk = 128, 256
    spec = pl.BlockSpec((None, d, length, w), lambda b: (b, 0, 0, 0))
    return pl.pallas_call(
        functools.partial(_dil_kernel, length=length, tq=tq, tk=tk),
        grid=(batch,),
        in_specs=[spec, spec, spec],
        out_specs=[spec, spec],
        out_shape=[jax.ShapeDtypeStruct(q.shape, BF16), jax.ShapeDtypeStruct(q.shape, F32)],
        compiler_params=_cparams(("arbitrary",)),
        name=f"dil{d}",
    )(q, k, v)


def _swa_kernel(sink_ref, q_ref, k_ref, v_ref, o_ref, *, length, tq, tk):
    low = lax.broadcasted_iota(I32, (tq, LANES), 1) < 64
    nblk = q_ref.shape[1] // LANES

    def body(j, carry):
        qs = pl.multiple_of(j * tq, tq)
        ks = pl.multiple_of(jnp.clip(qs - SWA_WINDOW, 0, length - tk), SWA_WINDOW)
        mask = _band_mask(qs, ks, 2, tq, tk, SWA_WINDOW)
        for b in range(nblk):
            cs = slice((b // 2) * LANES, (b // 2 + 1) * LANES)
            bs = slice(b * LANES, (b + 1) * LANES)
            qst = _split_heads(q_ref[pl.ds(qs, tq), bs], low)
            sinks = (sink_ref[2 * b], sink_ref[2 * b + 1])
            o, _, _ = _band_softmax(qst, k_ref[pl.ds(ks, tk), cs], v_ref[pl.ds(ks, tk), cs], mask, sinks)
            o_ref[pl.ds(qs, tq), bs] = jnp.where(low, o[:tq], o[tq:]).astype(BF16)
        return carry

    lax.fori_loop(0, length // tq, body, 0, unroll=4)


def _swa_attention(sink, q, k, v):
    batch, length, qw = q.shape
    tq, tk = 128, 384
    return pl.pallas_call(
        functools.partial(_swa_kernel, length=length, tq=tq, tk=tk),
        grid=(batch,),
        in_specs=[pl.BlockSpec(memory_space=pltpu.SMEM),
                  pl.BlockSpec((None, length, qw), lambda b: (b, 0, 0)),
                  pl.BlockSpec((None, length, k.shape[2]), lambda b: (b, 0, 0)),
                  pl.BlockSpec((None, length, v.shape[2]), lambda b: (b, 0, 0))],
        out_specs=pl.BlockSpec((None, length, qw), lambda b: (b, 0, 0)),
        out_shape=jax.ShapeDtypeStruct(q.shape, BF16),
        compiler_params=_cparams(("arbitrary",)),
        name="swa",
    )(sink, q, k, v)


def _route_rows(logits):
    lane = lax.broadcasted_iota(I32, logits.shape, 1).astype(F32)
    big = 1e9
    is_g = lane < N_GROUPS
    gl = jnp.where(is_g, logits, NEG_INF)
    gmax = jnp.max(gl, axis=-1, keepdims=True)
    gsel = jnp.min(jnp.where(is_g & (gl == gmax), lane, big), axis=-1, keepdims=True)
    gw = 1.0 / jnp.sum(jnp.where(is_g, jnp.exp(gl - gmax), 0.0), axis=-1, keepdims=True)
    e_lo = N_GROUPS + gsel * EXPERTS_PER_GROUP
    in_grp = (lane >= e_lo) & (lane < e_lo + EXPERTS_PER_GROUP)
    el = jnp.where(in_grp, logits, NEG_INF)
    m1 = jnp.max(el, axis=-1, keepdims=True)
    i1 = jnp.min(jnp.where(in_grp & (el == m1), lane, big), axis=-1, keepdims=True)
    el2 = jnp.where(lane == i1, NEG_INF, el)
    m2 = jnp.max(el2, axis=-1, keepdims=True)
    i2 = jnp.min(jnp.where(in_grp & (lane != i1) & (el2 == m2), lane, big), axis=-1, keepdims=True)
    t = jnp.exp(m2 - m1)
    tw1 = gw / (1.0 + t)
    tw2 = gw * t / (1.0 + t)
    out = jnp.where(lane == 0, tw1, 0.0)
    out = jnp.where(lane == 1, tw2, out)
    out = jnp.where(lane == 2, i1 - N_GROUPS, out)
    return jnp.where(lane == 3, i2 - N_GROUPS, out)


def _outproj_kernel(o0_ref, l0_ref, o1_ref, l1_ref, o2_ref, l2_ref, ob_ref, sga_ref, sgb_ref, x_ref,
                    mod_ref, g_ref, wa_ref, wb_ref, wo_ref, wr_ref, br_ref,
                    x1_ref, h2_ref, route_ref, len_ref,
                    so1_ref, sl1_ref, so2_ref, sl2_ref, *, tm, sub):
    dm = x_ref.shape[1]
    for (o_ref, l_ref, so_ref, sl_ref, d) in ((o1_ref, l1_ref, so1_ref, sl1_ref, DILATIONS[1]),
                                              (o2_ref, l2_ref, so2_ref, sl2_ref, DILATIONS[2])):
        for r in range(d):
            for c in range(2):
                cs = slice(c * LANES, (c + 1) * LANES)
                so_ref[c, pl.ds(r, tm // d, stride=d), :] = o_ref[0, r, :, cs].astype(F32)
                sl_ref[c, pl.ds(r, tm // d, stride=d), :] = l_ref[0, r, :, cs]
    mr = lax.broadcasted_iota(I32, (LANES, LANES), 0)
    mc = lax.broadcasted_iota(I32, (LANES, LANES), 1)
    move_hi = jnp.where(((mr < 2) & (mc == 2 * mr)) | ((mr >= 2) & (mr < 4) & (mc == mr + 2)), 1.0, 0.0).astype(BF16)
    move_lo = jnp.where((mr < 2) & (mc == 2 * mr + 1), 1.0, 0.0).astype(BF16)
    for t in range(tm // sub):
        rs = slice(t * sub, (t + 1) * sub)
        both = lambda ref: jnp.concatenate([ref[0, rs, :], ref[1, rs, :]], axis=1)
        o0, l0 = o0_ref[0, 0, rs, :].astype(F32), l0_ref[0, 0, rs, :]
        o1, l1, o2, l2 = both(so1_ref), both(sl1_ref), both(so2_ref), both(sl2_ref)
        mx = jnp.maximum(jnp.maximum(l0, l1), l2)
        w0, w1, w2 = jnp.exp(l0 - mx), jnp.exp(l1 - mx), jnp.exp(l2 - mx)
        o_a = (w0 * o0 + w1 * o1 + w2 * o2) / (w0 + w1 + w2)
        y_a = jnp.dot(o_a.astype(BF16), wa_ref[...], preferred_element_type=F32)
        y_b = jnp.dot(ob_ref[rs, :], wb_ref[...], preferred_element_type=F32)
        merged = sga_ref[rs, :].astype(F32) * y_a + sgb_ref[rs, :].astype(F32) * y_b
        mix = jnp.dot(merged.astype(BF16), wo_ref[...], preferred_element_type=F32)
        x1 = x_ref[rs, :] + mod_ref[0, 2:3, :] * mix
        x1_ref[rs, :] = x1
        h2 = _rms_mod(x1, g_ref[...], mod_ref[0, 3:4, :], mod_ref[0, 4:5, :]).astype(BF16)
        h2_ref[rs, :dm] = h2
        logits = jnp.dot(h2, wr_ref[...], preferred_element_type=F32) + br_ref[...]
        rt = _route_rows(logits)
        route_ref[rs, :] = rt
        hi = rt.astype(BF16)
        lo = (rt - hi.astype(F32)).astype(BF16)
        aux = (jnp.dot(hi, move_hi, preferred_element_type=F32) + jnp.dot(lo, move_lo, preferred_element_type=F32))
        h2_ref[rs, dm:] = aux.astype(BF16)

    part = route_ref[...]
    e1, e2 = part[:, 2:3], part[:, 3:4]
    lane = lax.broadcasted_iota(I32, (tm, LANES), 1).astype(F32)
    onehot = jnp.where((lane == e1) | (lane == e2), 1.0, 0.0)
    rr = lax.broadcasted_iota(I32, (tm, tm), 0)
    cc = lax.broadcasted_iota(I32, (tm, tm), 1)
    tri = jnp.where(rr > cc, 1.0, 0.0).astype(BF16)
    prefix = jnp.dot(tri, onehot.astype(BF16), preferred_element_type=F32)
    cnt = jnp.sum(onehot, axis=0, keepdims=True)
    seg_len = jnp.ceil(cnt * (1.0 / SEG_ROWS)) * SEG_ROWS
    ur = lax.broadcasted_iota(I32, (LANES, LANES), 0)
    uc = lax.broadcasted_iota(I32, (LANES, LANES), 1)
    upper = jnp.where(ur < uc, 1.0, 0.0).astype(BF16)
    seg_off = jnp.dot(jnp.broadcast_to(seg_len, (8, LANES)).astype(BF16), upper, preferred_element_type=F32)[0:1, :]
    slot_map = seg_off + prefix
    s1 = jnp.sum(jnp.where(lane == e1, slot_map, 0.0), axis=-1, keepdims=True)
    s2 = jnp.sum(jnp.where(lane == e2, slot_map, 0.0), axis=-1, keepdims=True)
    len_ref[0] = jnp.broadcast_to(seg_len, (8, LANES))
    route_ref[...] = jnp.where(lane == 2, s1, jnp.where(lane == 3, s2, part))


def _outproj(dil_outs, ob, sga, sgb, x2, mod3, g_ffn, wa, wb, wo, wr, br, batch, seq):
    n, dm = x2.shape
    tm = MOE_TILE
    tpb = seq // tm
    row = lambda i: (i, 0)
    const = lambda i: (0, 0)
    in_specs = []
    args = []
    for (o, l), d in zip(dil_outs, DILATIONS):
        spec = pl.BlockSpec((1, d, tm // d, DIL_GROUP_W), lambda i: (i // tpb, 0, i % tpb, 0))
        in_specs += [spec, spec]
        args += [o, l]
    in_specs += [pl.BlockSpec((tm, ob.shape[1]), row), pl.BlockSpec((tm, dm), row), pl.BlockSpec((tm, dm), row),
                 pl.BlockSpec((tm, dm), row),
                 pl.BlockSpec((1, 6, dm), lambda i: (i // tpb, 0, 0)),
                 pl.BlockSpec((1, dm), const),
                 pl.BlockSpec(wa.shape, const), pl.BlockSpec(wb.shape, const), pl.BlockSpec(wo.shape, const),
                 pl.BlockSpec(wr.shape, const), pl.BlockSpec(br.shape, const)]
    args += [ob, sga, sgb, x2, mod3, g_ffn, wa, wb, wo, wr, br]
    return pl.pallas_call(
        functools.partial(_outproj_kernel, tm=tm, sub=512),
        grid=(n // tm,),
        in_specs=in_specs,
        out_specs=[pl.BlockSpec((tm, dm), row), pl.BlockSpec((tm, dm + LANES), row), pl.BlockSpec((tm, LANES), row),
                   pl.BlockSpec((1, 8, LANES), lambda i: (i, 0, 0))],
        out_shape=[jax.ShapeDtypeStruct((n, dm), F32), jax.ShapeDtypeStruct((n, dm + LANES), BF16),
                   jax.ShapeDtypeStruct((n, LANES), F32), jax.ShapeDtypeStruct((n // tm, 8, LANES), F32)],
        scratch_shapes=[pltpu.VMEM((2, tm, LANES), F32)] * 4,
        compiler_params=_cparams(("arbitrary",)),
        name="outproj",
    )(*args)


def _wait_copies(count, copy):
    def wait_one(c, carry):
        copy.wait()
        return carry

    lax.fori_loop(0, count, wait_one, 0)


def _dispatch_kernel(dst_ref, tstart_ref, tch_ref, nused_ref, h_ref, route_ref, xr_ref,
                     sorted_ref, zero_ref, sem, *, tm, nslots, nblk, group):
    i = pl.program_id(0)
    last = pl.num_programs(0) - 1
    slot = i & 1
    nchk = nslots // SEG_ROWS
    zsem = sem.at[0]

    def all_chunks(s):
        return pltpu.make_async_copy(sorted_ref.at[s], xr_ref.at[pl.ds(0, nslots), :], sem.at[s])

    def zero_blk(b):
        return pltpu.make_async_copy(zero_ref, xr_ref.at[pl.ds(pl.multiple_of(b * ROW_BLOCK, ROW_BLOCK), ROW_BLOCK), :], zsem)

    def zero_blocks(lo, hi):
        def blk(b, carry):
            zero_blk(b).start()
            return carry

        lax.fori_loop(lo, hi, blk, 0)
        _wait_copies(hi - lo, zero_blk(0))

    @pl.when(i == 0)
    def _():
        zero_ref[...] = jnp.zeros_like(zero_ref)
        zero_blocks(nblk, nblk + (2 * nslots) // ROW_BLOCK)

    @pl.when(i >= 2)
    def _():
        all_chunks(slot).wait()

    pr = lax.broadcasted_iota(I32, (8, LANES), 0)
    pc = lax.broadcasted_iota(I32, (8, LANES), 1)
    lane_pick = jnp.where(pc == pr + 2, 1.0, 0.0)
    slots_t = lax.dot_general(lane_pick, route_ref[...], (((1,), (1,)), ((), ())), preferred_element_type=F32,
                              precision=lax.Precision.HIGHEST)
    for g in range(nslots // group):
        srow = (lax.broadcasted_iota(I32, (group, tm), 0) + g * group).astype(F32)
        pick = jnp.where((srow == slots_t[0:1, :]) | (srow == slots_t[1:2, :]), 1.0, 0.0).astype(BF16)
        sorted_ref[slot, g * group:(g + 1) * group, :] = jnp.dot(
            pick, h_ref[...], preferred_element_type=F32).astype(BF16)
        for c in range(g * group // SEG_ROWS, (g + 1) * group // SEG_ROWS):
            dst = pl.multiple_of(dst_ref[i * nchk + c], SEG_ROWS)
            pltpu.make_async_copy(sorted_ref.at[slot, pl.ds(c * SEG_ROWS, SEG_ROWS), :],
                                  xr_ref.at[pl.ds(dst, SEG_ROWS), :], sem.at[slot]).start()

    @pl.when(i == last)
    def _():
        all_chunks(slot).wait()

        @pl.when(i >= 1)
        def _():
            all_chunks(1 - slot).wait()

        def zero_seg(dst):
            return pltpu.make_async_copy(zero_ref.at[pl.ds(0, SEG_ROWS), :], xr_ref.at[pl.ds(dst, SEG_ROWS), :], zsem)

        def tail(e, count):
            def chunk(c, carry):
                zero_seg(pl.multiple_of(tstart_ref[e] + c * SEG_ROWS, SEG_ROWS)).start()
                return carry

            lax.fori_loop(0, tch_ref[e], chunk, 0)
            return count + tch_ref[e]

        _wait_copies(lax.fori_loop(0, N_EXPERTS, tail, 0), zero_seg(0))
        zero_blocks(nused_ref[0], nblk)


def _dispatch(tables, h2, route, nblk, tm, nslots):
    n, width = h2.shape
    rows = nblk * ROW_BLOCK + 2 * nslots
    grid_spec = pltpu.PrefetchScalarGridSpec(
        num_scalar_prefetch=4,
        grid=(n // tm,),
        in_specs=[pl.BlockSpec((tm, width), lambda i, *_: (i, 0)),
                  pl.BlockSpec((tm, LANES), lambda i, *_: (i, 0))],
        out_specs=pl.BlockSpec(memory_space=pl.ANY),
        scratch_shapes=[pltpu.VMEM((2, nslots, width), BF16), pltpu.VMEM((ROW_BLOCK, width), BF16),
                        pltpu.SemaphoreType.DMA((2,))],
    )
    return pl.pallas_call(
        functools.partial(_dispatch_kernel, tm=tm, nslots=nslots, nblk=nblk, group=256),
        grid_spec=grid_spec,
        out_shape=jax.ShapeDtypeStruct((rows, width), BF16),
        compiler_params=_cparams(("arbitrary",)),
        name="dispatch",
    )(*tables, h2, route)


def _experts_kernel(first_ref, count_ref, wg_ref, wu_ref, wd_ref, x_hbm, y_hbm,
                    xbuf, ybuf, wgb_ref, wub_ref, wdb_ref, xsem, ysem, *, nblk):
    e = pl.program_id(0)
    first = first_ref[e]
    count = count_ref[e]
    dm = ybuf.shape[2]

    def x_copy(b, slot):
        rows = pl.ds(pl.multiple_of((first + b) * ROW_BLOCK, ROW_BLOCK), ROW_BLOCK)
        return pltpu.make_async_copy(x_hbm.at[rows, :], xbuf.at[slot], xsem.at[slot])

    def y_copy(b, slot):
        rows = pl.ds(pl.multiple_of((first + b) * ROW_BLOCK, ROW_BLOCK), ROW_BLOCK)
        return pltpu.make_async_copy(ybuf.at[slot], y_hbm.at[rows, :], ysem.at[slot])

    @pl.when((e == 0) & (count > 0))
    def _():
        x_copy(0, 0).start(priority=1)

    wgb_ref[...] = wg_ref[0].astype(BF16)
    wub_ref[...] = wu_ref[0].astype(BF16)
    wdb_ref[...] = wd_ref[0].astype(BF16)

    def body(b, carry):
        slot = b & 1
        x_copy(b, slot).wait()

        @pl.when(b + 1 < count)
        def _():
            x_copy(b + 1, 1 - slot).start(priority=1)

        @pl.when(b >= 2)
        def _():
            y_copy(b - 2, slot).wait()

        xb = xbuf[slot, :, :dm]
        aux = xbuf[slot, :, dm:].astype(F32)
        w = jnp.where(aux[:, 4:5] == e.astype(F32), aux[:, 0:1] + aux[:, 1:2], aux[:, 2:3] + aux[:, 3:4])
        g = jnp.dot(xb, wgb_ref[...], preferred_element_type=F32)
        u = jnp.dot(xb, wub_ref[...], preferred_element_type=F32)
        a = (g * jax.nn.sigmoid(g)) * u
        y = jnp.dot(a.astype(BF16), wdb_ref[...], preferred_element_type=F32)
        ybuf[slot] = (y * w).astype(BF16)
        y_copy(b, slot).start()
        return carry

    lax.fori_loop(0, count, body, 0)

    nxt = jnp.minimum(e + 1, pl.num_programs(0) - 1)

    @pl.when((e + 1 < pl.num_programs(0)) & (count_ref[nxt] > 0))
    def _():
        rows = pl.ds(pl.multiple_of(first_ref[nxt] * ROW_BLOCK, ROW_BLOCK), ROW_BLOCK)
        pltpu.make_async_copy(x_hbm.at[rows, :], xbuf.at[0], xsem.at[0]).start(priority=1)

    @pl.when(count >= 2)
    def _():
        y_copy(count - 2, count & 1).wait()

    @pl.when(count >= 1)
    def _():
        y_copy(count - 1, (count - 1) & 1).wait()

    @pl.when(e == pl.num_programs(0) - 1)
    def _():
        ybuf[0] = jnp.zeros((ROW_BLOCK, dm), BF16)
        used = first + count

        def zero_blk(b):
            rows = pl.ds(pl.multiple_of(b * ROW_BLOCK, ROW_BLOCK), ROW_BLOCK)
            return pltpu.make_async_copy(ybuf.at[0], y_hbm.at[rows, :], ysem.at[0])

        def start(b, carry):
            zero_blk(b).start()
            return carry

        lax.fori_loop(used, nblk, start, 0)
        _wait_copies(nblk - used, zero_blk(0))


def _experts(first_blk, count_blk, x_rows, w_gate, w_up, w_down):
    rows, width = x_rows.shape
    n_exp, dm, de = w_gate.shape
    nblk = rows // ROW_BLOCK
    wmap = lambda e, *_: (e, 0, 0)
    grid_spec = pltpu.PrefetchScalarGridSpec(
        num_scalar_prefetch=2,
        grid=(n_exp,),
        in_specs=[pl.BlockSpec((1, dm, de), wmap),
                  pl.BlockSpec((1, dm, de), wmap),
                  pl.BlockSpec((1, de, dm), wmap),
                  pl.BlockSpec(memory_space=pl.ANY)],
        out_specs=pl.BlockSpec(memory_space=pl.ANY),
        scratch_shapes=[pltpu.VMEM((2, ROW_BLOCK, width), BF16), pltpu.VMEM((2, ROW_BLOCK, dm), BF16),
                        pltpu.VMEM((dm, de), BF16), pltpu.VMEM((dm, de), BF16), pltpu.VMEM((de, dm), BF16),
                        pltpu.SemaphoreType.DMA((2,)), pltpu.SemaphoreType.DMA((2,))],
    )
    return pl.pallas_call(
        functools.partial(_experts_kernel, nblk=nblk),
        grid_spec=grid_spec,
        out_shape=jax.ShapeDtypeStruct((rows, dm), BF16),
        compiler_params=_cparams(("arbitrary",)),
        name="experts",
    )(first_blk, count_blk, w_gate, w_up, w_down, x_rows)


def _combine_kernel(src_ref, route_ref, x1_ref, mod_ref, g_ref, y_ref, o_ref, ys_ref, sem, *, tm, nslots):
    i = pl.program_id(0)
    last = pl.num_programs(0) - 1
    slot = i & 1
    nchk = nslots // SEG_ROWS

    def all_chunks(s):
        return pltpu.make_async_copy(y_ref.at[pl.ds(0, nslots), :], ys_ref.at[s], sem.at[s])

    def gather(tile, s):
        for c in range(nchk):
            src = pl.multiple_of(src_ref[tile * nchk + c], SEG_ROWS)
            pltpu.make_async_copy(y_ref.at[pl.ds(src, SEG_ROWS), :],
                                  ys_ref.at[s, pl.ds(c * SEG_ROWS, SEG_ROWS), :], sem.at[s]).start()

    @pl.when(i == 0)
    def _():
        gather(0, 0)

    gather(jnp.minimum(i + 1, last), 1 - slot)
    all_chunks(slot).wait()
    route = route_ref[...]
    scol = lax.broadcasted_iota(I32, (tm, nslots), 1).astype(F32)
    pick = jnp.where((scol == route[:, 2:3]) | (scol == route[:, 3:4]), 1.0, 0.0).astype(BF16)
    moe = jnp.dot(pick, ys_ref[slot], preferred_element_type=F32)
    x = x1_ref[...] + mod_ref[0, 5:6, :] * moe
    ms = jnp.mean(x * x, axis=-1, keepdims=True)
    o_ref[...] = (x * lax.rsqrt(ms + RMS_EPS)) * g_ref[...]

    @pl.when(i == last)
    def _():
        all_chunks(1 - slot).wait()


def _combine(src_tbl, route, x1, mod3, g_final, y_rows, seq, tm, nslots):
    n, dm = x1.shape
    tpb = seq // tm
    grid_spec = pltpu.PrefetchScalarGridSpec(
        num_scalar_prefetch=1,
        grid=(n // tm,),
        in_specs=[pl.BlockSpec((tm, LANES), lambda i, *_: (i, 0)),
                  pl.BlockSpec((tm, dm), lambda i, *_: (i, 0)),
                  pl.BlockSpec((1, 6, dm), lambda i, *_: (i // tpb, 0, 0)),
                  pl.BlockSpec((1, dm), lambda i, *_: (0, 0)),
                  pl.BlockSpec(memory_space=pl.ANY)],
        out_specs=pl.BlockSpec((tm, dm), lambda i, *_: (i, 0)),
        scratch_shapes=[pltpu.VMEM((2, nslots, dm), BF16), pltpu.SemaphoreType.DMA((2,))],
    )
    return pl.pallas_call(
        functools.partial(_combine_kernel, tm=tm, nslots=nslots),
        grid_spec=grid_spec,
        out_shape=jax.ShapeDtypeStruct((n, dm), F32),
        compiler_params=_cparams(("arbitrary",)),
        name="combine",
    )(src_tbl, route, x1, mod3, g_final, y_rows)


def _rope_tables(positions):
    half = HEAD_DIM // 2
    inv_freq = ROPE_THETA ** (-jnp.arange(half, dtype=F32) * (2.0 / HEAD_DIM))
    freq = jnp.tile(inv_freq, LANES // half)
    sign = jnp.tile(jnp.concatenate([-jnp.ones((half,), F32), jnp.ones((half,), F32)]), LANES // HEAD_DIM)
    ang = positions.astype(F32).reshape(-1, 1) * freq
    return jnp.cos(ang), jnp.sin(ang) * sign


def kernel(x, c, positions, w_ada, b_ada, g_mix, w_in, sink_logits, w_branch_a, w_branch_b, w_out, g_ffn,
           w_group, b_group, w_route, b_route, w_expert_gate, w_expert_up, w_expert_down, g_final):
    batch, seq, dm = x.shape
    n = batch * seq
    assert w_ada.shape[0] == 1, "one layer"
    x2 = x.reshape(n, dm)

    c8 = jnp.pad(c, ((0, 8 - batch), (0, 0)))
    mod = _ada(c8, w_ada[0], b_ada[0].reshape(1, -1))
    mod3 = mod[:batch].reshape(batch, 6, dm)

    cos_t, sin_t = _rope_tables(positions)
    outs = _inproj(x2, mod3, g_mix[0].reshape(1, dm), cos_t, sin_t, w_in[0].astype(BF16), batch, seq)
    qkv = outs[:9]
    qb, kb, vb, sga, sgb = outs[9:]

    dil_outs = [_dil_attention(qkv[3 * g], qkv[3 * g + 1], qkv[3 * g + 2]) for g in range(len(DILATIONS))]
    ob = _swa_attention(sink_logits[0], qb.reshape(batch, seq, -1), kb.reshape(batch, seq, -1),
                        vb.reshape(batch, seq, -1)).reshape(n, -1)

    pad = LANES - N_GROUPS - N_EXPERTS
    wr = jnp.concatenate([w_group[0], w_route[0], jnp.zeros((dm, pad), F32)], axis=1).astype(BF16)
    br = jnp.concatenate([b_group[0], b_route[0], jnp.zeros((pad,), F32)]).reshape(1, LANES)
    x1, h2, route, seg_lens = _outproj(dil_outs, ob, sga, sgb, x2, mod3, g_ffn[0].reshape(1, dm),
                                       w_branch_a[0].astype(BF16), w_branch_b[0].astype(BF16), w_out[0].astype(BF16),
                                       wr, br, batch, seq)

    ntiles = n // MOE_TILE
    nslots = 2 * MOE_TILE + N_EXPERTS * SEG_ROWS
    nblk = -(-(2 * n + ntiles * N_EXPERTS * (SEG_ROWS - 1)) // ROW_BLOCK) + N_EXPERTS
    lens = seg_lens[:, 0, :N_EXPERTS].astype(I32)
    tot = jnp.sum(lens, axis=0)
    padded = (tot + ROW_BLOCK - 1) // ROW_BLOCK * ROW_BLOCK
    pend = jnp.cumsum(padded)
    pstart = pend - padded
    base = pstart[None, :] + jnp.cumsum(lens, axis=0) - lens
    nused = pend[-1:] // ROW_BLOCK
    nchk = nslots // SEG_ROWS
    run_end = jnp.cumsum(lens // SEG_ROWS, axis=1)
    chunk = jnp.arange(nchk, dtype=I32)
    owner = jnp.sum((run_end[:, None, :] <= chunk[None, :, None]).astype(I32), axis=2)
    live = owner < N_EXPERTS
    owner = jnp.minimum(owner, N_EXPERTS - 1)
    run_first = jnp.take_along_axis(run_end - lens // SEG_ROWS, owner, axis=1)
    row = jnp.take_along_axis(base, owner, axis=1) + (chunk[None, :] - run_first) * SEG_ROWS
    spare = nblk * ROW_BLOCK + (jnp.arange(ntiles, dtype=I32)[:, None] % 2) * nslots + chunk[None, :] * SEG_ROWS
    dst_tbl = jnp.where(live, row, spare).reshape(-1)
    src_tbl = jnp.where(live, row, chunk[None, :] * SEG_ROWS).reshape(-1)
    tables = (dst_tbl, pstart + tot, (padded - tot) // SEG_ROWS, nused)

    x_rows = _dispatch(tables, h2, route, nblk, MOE_TILE, nslots)
    y_rows = _experts(pstart // ROW_BLOCK, padded // ROW_BLOCK, x_rows,
                      w_expert_gate[0], w_expert_up[0], w_expert_down[0])
    out = _combine(src_tbl, route, x1, mod3, g_final.reshape(1, dm), y_rows, seq, MOE_TILE, nslots)
    return out.reshape(batch, seq, dm)
```

```python
import functools

import jax
import jax.numpy as jnp
from jax import lax
from jax.experimental import pallas as pl
from jax.experimental.pallas import tpu as pltpu

F32 = jnp.float32
BF16 = jnp.bfloat16
I32 = jnp.int32

HEAD_DIM = 64
ROPE_THETA = 10000.0
RMS_EPS = 1e-6
NEG_INF = -1e30
DILATIONS = (1, 4, 16)
DIL_HALF_WINDOW = 64
DIL_GROUP_W = 256
SWA_WINDOW = 128
N_GROUPS = 4
EXPERTS_PER_GROUP = 8
N_EXPERTS = 32
LANES = 128
ROW_BLOCK = 512
SEG_ROWS = 16
MOE_TILE = 512
VMEM_LIMIT = 56 * 1024 * 1024


def _cparams(sem):
    return pltpu.CompilerParams(dimension_semantics=sem, vmem_limit_bytes=VMEM_LIMIT)


def _ada_kernel(c_ref, w_ref, b_ref, o_ref):
    c = c_ref[...]
    cs = c * jax.nn.sigmoid(c)
    o_ref[...] = jnp.dot(cs.astype(BF16), w_ref[...].astype(BF16), preferred_element_type=F32) + b_ref[...]


def _ada(c8, w_ada, b_ada):
    d, n = w_ada.shape
    tn = 1536
    return pl.pallas_call(
        _ada_kernel,
        grid=(n // tn,),
        in_specs=[pl.BlockSpec((8, d), lambda j: (0, 0)),
                  pl.BlockSpec((d, tn), lambda j: (0, j)),
                  pl.BlockSpec((1, tn), lambda j: (0, j))],
        out_specs=pl.BlockSpec((8, tn), lambda j: (0, j)),
        out_shape=jax.ShapeDtypeStruct((8, n), F32),
        compiler_params=_cparams(("arbitrary",)),
        name="ada",
    )(c8, w_ada, b_ada)


def _rms_mod(x, g, shift, scale):
    ms = jnp.mean(x * x, axis=-1, keepdims=True)
    return (x * lax.rsqrt(ms + RMS_EPS)) * g * (1.0 + scale) + shift


def _inproj_kernel(x_ref, mod_ref, g_ref, cos_ref, sin_ref, w_ref,
                   q0_ref, k0_ref, v0_ref, q1_ref, k1_ref, v1_ref, q2_ref, k2_ref, v2_ref,
                   qb_ref, kb_ref, vb_ref, sga_ref, sgb_ref, stg_ref, *, tm):
    h = _rms_mod(x_ref[...], g_ref[...], mod_ref[0, 0:1, :], mod_ref[0, 1:2, :])
    hb = h.astype(BF16)
    cos = cos_ref[...]
    sin = sin_ref[...]
    lane = lax.broadcasted_iota(I32, (tm, LANES), 1)
    first_half = (lane & 32) == 0
    low = lane < 64

    def proj(c0, width):
        return jnp.dot(hb, w_ref[:, c0:c0 + width], preferred_element_type=F32)

    def rope(t):
        rot = jnp.where(first_half, pltpu.roll(t, 96, 1), pltpu.roll(t, 32, 1))
        return t * cos + rot * sin

    def rope256(p):
        return jnp.concatenate([rope(p[:, :LANES]), rope(p[:, LANES:])], axis=1)

    def store_group(ref, val, d):
        if d == 1:
            ref[0, 0] = val.astype(BF16)
        else:
            for c in range(2):
                stg_ref[c] = val[:, c * LANES:(c + 1) * LANES]
            for r in range(d):
                for c in range(2):
                    ref[0, r, :, c * LANES:(c + 1) * LANES] = (
                        stg_ref[c, pl.ds(r, tm // d, stride=d), :].astype(BF16))

    q_refs = (q0_ref, q1_ref, q2_ref)
    k_refs = (k0_ref, k1_ref, k2_ref)
    v_refs = (v0_ref, v1_ref, v2_ref)
    for g, d in enumerate(DILATIONS):
        store_group(q_refs[g], rope256(proj(g * 256, 256)) * 0.125, d)
        store_group(k_refs[g], rope256(proj(768 + g * 256, 256)), d)
        store_group(v_refs[g], proj(1536 + g * 256, 256), d)
    for j in range(2):
        qb_ref[:, j * 256:(j + 1) * 256] = (rope256(proj(2304 + j * 256, 256)) * 0.125).astype(BF16)
    kv = proj(2816, 256)
    kb = rope(kv[:, :LANES])
    vb = kv[:, LANES:]
    kb_sw = pltpu.roll(kb, 64, 1)
    vb_sw = pltpu.roll(vb, 64, 1)
    kb_ref[:, :LANES] = jnp.where(low, kb, kb_sw).astype(BF16)
    kb_ref[:, LANES:] = jnp.where(low, kb_sw, kb).astype(BF16)
    vb_ref[:, :LANES] = jnp.where(low, vb, vb_sw).astype(BF16)
    vb_ref[:, LANES:] = jnp.where(low, vb_sw, vb).astype(BF16)
    for j in range(4):
        sga_ref[:, j * 256:(j + 1) * 256] = jax.nn.sigmoid(proj(3072 + j * 256, 256)).astype(BF16)
        sgb_ref[:, j * 256:(j + 1) * 256] = jax.nn.sigmoid(proj(4096 + j * 256, 256)).astype(BF16)


def _inproj(x2, mod3, g_mix, cos_t, sin_t, w_in_bf, batch, seq):
    n, dm = x2.shape
    tm = 512
    tpb = seq // tm
    grid = (n // tm,)
    row = lambda i: (i, 0)
    strided_specs, strided_shapes = [], []
    for d in DILATIONS:
        for _ in range(3):
            strided_specs.append(pl.BlockSpec((1, d, tm // d, DIL_GROUP_W), lambda i: (i // tpb, 0, i % tpb, 0)))
            strided_shapes.append(jax.ShapeDtypeStruct((batch, d, seq // d, DIL_GROUP_W), BF16))
    out_specs = strided_specs + [
        pl.BlockSpec((tm, 512), row), pl.BlockSpec((tm, 256), row), pl.BlockSpec((tm, 256), row),
        pl.BlockSpec((tm, dm), row), pl.BlockSpec((tm, dm), row)]
    out_shapes = strided_shapes + [
        jax.ShapeDtypeStruct((n, 512), BF16), jax.ShapeDtypeStruct((n, 256), BF16),
        jax.ShapeDtypeStruct((n, 256), BF16), jax.ShapeDtypeStruct((n, dm), BF16),
        jax.ShapeDtypeStruct((n, dm), BF16)]
    return pl.pallas_call(
        functools.partial(_inproj_kernel, tm=tm),
        grid=grid,
        in_specs=[pl.BlockSpec((tm, dm), row),
                  pl.BlockSpec((1, 6, dm), lambda i: (i // tpb, 0, 0)),
                  pl.BlockSpec((1, dm), lambda i: (0, 0)),
                  pl.BlockSpec((tm, LANES), row),
                  pl.BlockSpec((tm, LANES), row),
                  pl.BlockSpec(w_in_bf.shape, lambda i: (0, 0))],
        out_specs=out_specs,
        out_shape=out_shapes,
        scratch_shapes=[pltpu.VMEM((2, tm, LANES), F32)],
        compiler_params=_cparams(("arbitrary",)),
        name="inproj",
    )(x2, mod3, g_mix, cos_t, sin_t, w_in_bf)


def _split_heads(q2, low):
    zero = jnp.zeros_like(q2)
    return jnp.concatenate([jnp.where(low, q2, zero), jnp.where(low, zero, q2)], axis=0)


def _band_softmax(qst, k2, v2, mask, sinks):
    s = lax.dot_general(qst, k2, (((1,), (1,)), ((), ())), preferred_element_type=F32)
    s = jnp.where(mask, s, NEG_INF)
    rows, tk = s.shape
    m = jnp.max(s, axis=-1, keepdims=True)
    if sinks is not None:
        seg = rows // len(sinks)
        m = jnp.concatenate([jnp.maximum(m[h * seg:(h + 1) * seg], sk) for h, sk in enumerate(sinks)], axis=0)
    m = jnp.broadcast_to(m, (rows, LANES))
    e = jnp.concatenate([jnp.exp(s[:, c * LANES:(c + 1) * LANES] - m) for c in range(tk // LANES)], axis=1)
    v_ones = jnp.concatenate([v2, jnp.ones((tk, LANES), BF16)], axis=1)
    od = jnp.dot(e.astype(BF16), v_ones, preferred_element_type=F32)
    o, den = od[:, :LANES], od[:, LANES:]
    if sinks is not None:
        den = jnp.concatenate([den[h * seg:(h + 1) * seg] + jnp.exp(sk - m[h * seg:(h + 1) * seg])
                               for h, sk in enumerate(sinks)], axis=0)
    return o / den, m, den


def _band_mask(qs, ks, nstack, tq, tk, window):
    row = lax.broadcasted_iota(I32, (nstack * tq, tk), 0) & (tq - 1)
    col = lax.broadcasted_iota(I32, (nstack * tq, tk), 1)
    return jnp.abs((ks + col) - (qs + row)) <= window


def _dil_kernel(q_ref, k_ref, v_ref, o_ref, l_ref, *, length, tq, tk):
    low = lax.broadcasted_iota(I32, (tq, LANES), 1) < 64

    nq = length // tq

    def body(j, carry):
        r = j // nq
        qs = pl.multiple_of((j % nq) * tq, tq)
        ks = pl.multiple_of(jnp.clip(qs - DIL_HALF_WINDOW, 0, length - tk), DIL_HALF_WINDOW)
        mask = _band_mask(qs, ks, 2, tq, tk, DIL_HALF_WINDOW)
        for c in range(DIL_GROUP_W // LANES):
            cs = slice(c * LANES, (c + 1) * LANES)
            qst = _split_heads(q_ref[r, pl.ds(qs, tq), cs], low)
            o, m, den = _band_softmax(qst, k_ref[r, pl.ds(ks, tk), cs], v_ref[r, pl.ds(ks, tk), cs], mask, None)
            lse = m + jnp.log(den)
            o_ref[r, pl.ds(qs, tq), cs] = jnp.where(low, o[:tq], o[tq:]).astype(BF16)
            l_ref[r, pl.ds(qs, tq), cs] = jnp.where(low, lse[:tq], lse[tq:])
        return carry

    lax.fori_loop(0, q_ref.shape[0] * nq, body, 0, unroll=4)


def _dil_attention(q, k, v):
    batch, d, length, w = q.shape
    tq, tk = 128, 256
    spec = pl.BlockSpec((None, d, length, w), lambda b: (b, 0, 0, 0))
    return pl.pallas_call(
        functools.partial(_dil_kernel, length=length, tq=tq, tk=tk),
        grid=(batch,),
        in_specs=[spec, spec, spec],
        out_specs=[spec, spec],
        out_shape=[jax.ShapeDtypeStruct(q.shape, BF16), jax.ShapeDtypeStruct(q.shape, F32)],
        compiler_params=_cparams(("arbitrary",)),
        name=f"dil{d}",
    )(q, k, v)


def _swa_kernel(sink_ref, q_ref, k_ref, v_ref, o_ref, *, length, tq, tk):
    low = lax.broadcasted_iota(I32, (tq, LANES), 1) < 64
    nblk = q_ref.shape[1] // LANES

    def body(j, carry):
        qs = pl.multiple_of(j * tq, tq)
        ks = pl.multiple_of(jnp.clip(qs - SWA_WINDOW, 0, length - tk), SWA_WINDOW)
        mask = _band_mask(qs, ks, 2, tq, tk, SWA_WINDOW)
        for b in range(nblk):
            cs = slice((b // 2) * LANES, (b // 2 + 1) * LANES)
            bs = slice(b * LANES, (b + 1) * LANES)
            qst = _split_heads(q_ref[pl.ds(qs, tq), bs], low)
            sinks = (sink_ref[2 * b], sink_ref[2 * b + 1])
            o, _, _ = _band_softmax(qst, k_ref[pl.ds(ks, tk), cs], v_ref[pl.ds(ks, tk), cs], mask, sinks)
            o_ref[pl.ds(qs, tq), bs] = jnp.where(low, o[:tq], o[tq:]).astype(BF16)
        return carry

    lax.fori_loop(0, length // tq, body, 0, unroll=4)


def _swa_attention(sink, q, k, v):
    batch, length, qw = q.shape
    tq, tk = 128, 384
    return pl.pallas_call(
        functools.partial(_swa_kernel, length=length, tq=tq, tk=tk),
        grid=(batch,),
        in_specs=[pl.BlockSpec(memory_space=pltpu.SMEM),
                  pl.BlockSpec((None, length, qw), lambda b: (b, 0, 0)),
                  pl.BlockSpec((None, length, k.shape[2]), lambda b: (b, 0, 0)),
                  pl.BlockSpec((None, length, v.shape[2]), lambda b: (b, 0, 0))],
        out_specs=pl.BlockSpec((None, length, qw), lambda b: (b, 0, 0)),
        out_shape=jax.ShapeDtypeStruct(q.shape, BF16),
        compiler_params=_cparams(("arbitrary",)),
        name="swa",
    )(sink, q, k, v)


def _route_rows(logits):
    lane = lax.broadcasted_iota(I32, logits.shape, 1).astype(F32)
    big = 1e9
    is_g = lane < N_GROUPS
    gl = jnp.where(is_g, logits, NEG_INF)
    gmax = jnp.max(gl, axis=-1, keepdims=True)
    gsel = jnp.min(jnp.where(is_g & (gl == gmax), lane, big), axis=-1, keepdims=True)
    gw = 1.0 / jnp.sum(jnp.where(is_g, jnp.exp(gl - gmax), 0.0), axis=-1, keepdims=True)
    e_lo = N_GROUPS + gsel * EXPERTS_PER_GROUP
    in_grp = (lane >= e_lo) & (lane < e_lo + EXPERTS_PER_GROUP)
    el = jnp.where(in_grp, logits, NEG_INF)
    m1 = jnp.max(el, axis=-1, keepdims=True)
    i1 = jnp.min(jnp.where(in_grp & (el == m1), lane, big), axis=-1, keepdims=True)
    el2 = jnp.where(lane == i1, NEG_INF, el)
    m2 = jnp.max(el2, axis=-1, keepdims=True)
    i2 = jnp.min(jnp.where(in_grp & (lane != i1) & (el2 == m2), lane, big), axis=-1, keepdims=True)
    t = jnp.exp(m2 - m1)
    tw1 = gw / (1.0 + t)
    tw2 = gw * t / (1.0 + t)
    out = jnp.where(lane == 0, tw1, 0.0)
    out = jnp.where(lane == 1, tw2, out)
    out = jnp.where(lane == 2, i1 - N_GROUPS, out)
    return jnp.where(lane == 3, i2 - N_GROUPS, out)


def _outproj_kernel(o0_ref, l0_ref, o1_ref, l1_ref, o2_ref, l2_ref, ob_ref, sga_ref, sgb_ref, x_ref,
                    mod_ref, g_ref, wa_ref, wb_ref, wo_ref, wr_ref, br_ref,
                    x1_ref, h2_ref, route_ref, len_ref,
                    so1_ref, sl1_ref, so2_ref, sl2_ref, *, tm, sub):
    dm = x_ref.shape[1]
    for (o_ref, l_ref, so_ref, sl_ref, d) in ((o1_ref, l1_ref, so1_ref, sl1_ref, DILATIONS[1]),
                                              (o2_ref, l2_ref, so2_ref, sl2_ref, DILATIONS[2])):
        for r in range(d):
            for c in range(2):
                cs = slice(c * LANES, (c + 1) * LANES)
                so_ref[c, pl.ds(r, tm // d, stride=d), :] = o_ref[0, r, :, cs].astype(F32)
                sl_ref[c, pl.ds(r, tm // d, stride=d), :] = l_ref[0, r, :, cs]
    mr = lax.broadcasted_iota(I32, (LANES, LANES), 0)
    mc = lax.broadcasted_iota(I32, (LANES, LANES), 1)
    move_hi = jnp.where(((mr < 2) & (mc == 2 * mr)) | ((mr >= 2) & (mr < 4) & (mc == mr + 2)), 1.0, 0.0).astype(BF16)
    move_lo = jnp.where((mr < 2) & (mc == 2 * mr + 1), 1.0, 0.0).astype(BF16)
    for t in range(tm // sub):
        rs = slice(t * sub, (t + 1) * sub)
        both = lambda ref: jnp.concatenate([ref[0, rs, :], ref[1, rs, :]], axis=1)
        o0, l0 = o0_ref[0, 0, rs, :].astype(F32), l0_ref[0, 0, rs, :]
        o1, l1, o2, l2 = both(so1_ref), both(sl1_ref), both(so2_ref), both(sl2_ref)
        mx = jnp.maximum(jnp.maximum(l0, l1), l2)
        w0, w1, w2 = jnp.exp(l0 - mx), jnp.exp(l1 - mx), jnp.exp(l2 - mx)
        o_a = (w0 * o0 + w1 * o1 + w2 * o2) / (w0 + w1 + w2)
        y_a = jnp.dot(o_a.astype(BF16), wa_ref[...], preferred_element_type=F32)
        y_b = jnp.dot(ob_ref[rs, :], wb_ref[...], preferred_element_type=F32)
        merged = sga_ref[rs, :].astype(F32) * y_a + sgb_ref[rs, :].astype(F32) * y_b
        mix = jnp.dot(merged.astype(BF16), wo_ref[...], preferred_element_type=F32)
        x1 = x_ref[rs, :] + mod_ref[0, 2:3, :] * mix
        x1_ref[rs, :] = x1
        h2 = _rms_mod(x1, g_ref[...], mod_ref[0, 3:4, :], mod_ref[0, 4:5, :]).astype(BF16)
        h2_ref[rs, :dm] = h2
        logits = jnp.dot(h2, wr_ref[...], preferred_element_type=F32) + br_ref[...]
        rt = _route_rows(logits)
        route_ref[rs, :] = rt
        hi = rt.astype(BF16)
        lo = (rt - hi.astype(F32)).astype(BF16)
        aux = (jnp.dot(hi, move_hi, preferred_element_type=F32) + jnp.dot(lo, move_lo, preferred_element_type=F32))
        h2_ref[rs, dm:] = aux.astype(BF16)

    part = route_ref[...]
    e1, e2 = part[:, 2:3], part[:, 3:4]
    lane = lax.broadcasted_iota(I32, (tm, LANES), 1).astype(F32)
    onehot = jnp.where((lane == e1) | (lane == e2), 1.0, 0.0)
    rr = lax.broadcasted_iota(I32, (tm, tm), 0)
    cc = lax.broadcasted_iota(I32, (tm, tm), 1)
    tri = jnp.where(rr > cc, 1.0, 0.0).astype(BF16)
    prefix = jnp.dot(tri, onehot.astype(BF16), preferred_element_type=F32)
    cnt = jnp.sum(onehot, axis=0, keepdims=True)
    seg_len = jnp.ceil(cnt * (1.0 / SEG_ROWS)) * SEG_ROWS
    ur = lax.broadcasted_iota(I32, (LANES, LANES), 0)
    uc = lax.broadcasted_iota(I32, (LANES, LANES), 1)
    upper = jnp.where(ur < uc, 1.0, 0.0).astype(BF16)
    seg_off = jnp.dot(jnp.broadcast_to(seg_len, (8, LANES)).astype(BF16), upper, preferred_element_type=F32)[0:1, :]
    slot_map = seg_off + prefix
    s1 = jnp.sum(jnp.where(lane == e1, slot_map, 0.0), axis=-1, keepdims=True)
    s2 = jnp.sum(jnp.where(lane == e2, slot_map, 0.0), axis=-1, keepdims=True)
    len_ref[0] = jnp.broadcast_to(seg_len, (8, LANES))
    route_ref[...] = jnp.where(lane == 2, s1, jnp.where(lane == 3, s2, part))


def _outproj(dil_outs, ob, sga, sgb, x2, mod3, g_ffn, wa, wb, wo, wr, br, batch, seq):
    n, dm = x2.shape
    tm = MOE_TILE
    tpb = seq // tm
    row = lambda i: (i, 0)
    const = lambda i: (0, 0)
    in_specs = []
    args = []
    for (o, l), d in zip(dil_outs, DILATIONS):
        spec = pl.BlockSpec((1, d, tm // d, DIL_GROUP_W), lambda i: (i // tpb, 0, i % tpb, 0))
        in_specs += [spec, spec]
        args += [o, l]
    in_specs += [pl.BlockSpec((tm, ob.shape[1]), row), pl.BlockSpec((tm, dm), row), pl.BlockSpec((tm, dm), row),
                 pl.BlockSpec((tm, dm), row),
                 pl.BlockSpec((1, 6, dm), lambda i: (i // tpb, 0, 0)),
                 pl.BlockSpec((1, dm), const),
                 pl.BlockSpec(wa.shape, const), pl.BlockSpec(wb.shape, const), pl.BlockSpec(wo.shape, const),
                 pl.BlockSpec(wr.shape, const), pl.BlockSpec(br.shape, const)]
    args += [ob, sga, sgb, x2, mod3, g_ffn, wa, wb, wo, wr, br]
    return pl.pallas_call(
        functools.partial(_outproj_kernel, tm=tm, sub=512),
        grid=(n // tm,),
        in_specs=in_specs,
        out_specs=[pl.BlockSpec((tm, dm), row), pl.BlockSpec((tm, dm + LANES), row), pl.BlockSpec((tm, LANES), row),
                   pl.BlockSpec((1, 8, LANES), lambda i: (i, 0, 0))],
        out_shape=[jax.ShapeDtypeStruct((n, dm), F32), jax.ShapeDtypeStruct((n, dm + LANES), BF16),
                   jax.ShapeDtypeStruct((n, LANES), F32), jax.ShapeDtypeStruct((n // tm, 8, LANES), F32)],
        scratch_shapes=[pltpu.VMEM((2, tm, LANES), F32)] * 4,
        compiler_params=_cparams(("arbitrary",)),
        name="outproj",
    )(*args)


def _wait_copies(count, copy):
    def wait_one(c, carry):
        copy.wait()
        return carry

    lax.fori_loop(0, count, wait_one, 0)


def _dispatch_kernel(dst_ref, tstart_ref, tch_ref, nused_ref, h_ref, route_ref, xr_ref,
                     sorted_ref, zero_ref, sem, *, tm, nslots, nblk, group):
    i = pl.program_id(0)
    last = pl.num_programs(0) - 1
    slot = i & 1
    nchk = nslots // SEG_ROWS
    nspare = (2 * nslots) // ROW_BLOCK
    zsem = sem.at[2]

    def all_chunks(s):
        return pltpu.make_async_copy(sorted_ref.at[s], xr_ref.at[pl.ds(0, nslots), :], sem.at[s])

    def zero_blk(b):
        return pltpu.make_async_copy(zero_ref, xr_ref.at[pl.ds(pl.multiple_of(b * ROW_BLOCK, ROW_BLOCK), ROW_BLOCK), :], zsem)

    def zero_seg(dst):
        return pltpu.make_async_copy(zero_ref.at[pl.ds(0, SEG_ROWS), :], xr_ref.at[pl.ds(dst, SEG_ROWS), :], zsem)

    def start_blocks(lo, hi):
        def blk(b, carry):
            zero_blk(b).start()
            return carry

        lax.fori_loop(lo, hi, blk, 0)

    ntail = lax.fori_loop(0, N_EXPERTS, lambda e, acc: acc + tch_ref[e], 0)

    @pl.when(i == 0)
    def _():
        zero_ref[...] = jnp.zeros_like(zero_ref)
        start_blocks(nblk, nblk + nspare)
        _wait_copies(nspare, zero_blk(0))
        def tail(e, carry):
            def chunk(c, carry2):
                zero_seg(pl.multiple_of(tstart_ref[e] + c * SEG_ROWS, SEG_ROWS)).start()
                return carry2

            lax.fori_loop(0, tch_ref[e], chunk, 0)
            return carry

        lax.fori_loop(0, N_EXPERTS, tail, 0)
        start_blocks(nused_ref[0], nblk)

    @pl.when(i >= 2)
    def _():
        all_chunks(slot).wait()

    pr = lax.broadcasted_iota(I32, (8, LANES), 0)
    pc = lax.broadcasted_iota(I32, (8, LANES), 1)
    lane_pick = jnp.where(pc == pr + 2, 1.0, 0.0)
    slots_t = lax.dot_general(lane_pick, route_ref[...], (((1,), (1,)), ((), ())), preferred_element_type=F32,
                              precision=lax.Precision.HIGHEST)
    for g in range(nslots // group):
        srow = (lax.broadcasted_iota(I32, (group, tm), 0) + g * group).astype(F32)
        pick = jnp.where((srow == slots_t[0:1, :]) | (srow == slots_t[1:2, :]), 1.0, 0.0).astype(BF16)
        sorted_ref[slot, g * group:(g + 1) * group, :] = jnp.dot(
            pick, h_ref[...], preferred_element_type=F32).astype(BF16)
        for c in range(g * group // SEG_ROWS, (g + 1) * group // SEG_ROWS):
            dst = pl.multiple_of(dst_ref[i * nchk + c], SEG_ROWS)
            pltpu.make_async_copy(sorted_ref.at[slot, pl.ds(c * SEG_ROWS, SEG_ROWS), :],
                                  xr_ref.at[pl.ds(dst, SEG_ROWS), :], sem.at[slot]).start()

    @pl.when(i == last)
    def _():
        all_chunks(slot).wait()

        @pl.when(i >= 1)
        def _():
            all_chunks(1 - slot).wait()

        _wait_copies(ntail, zero_seg(0))
        _wait_copies(nblk - nused_ref[0], zero_blk(0))


def _dispatch(tables, h2, route, nblk, tm, nslots):
    n, width = h2.shape
    rows = nblk * ROW_BLOCK + 2 * nslots
    grid_spec = pltpu.PrefetchScalarGridSpec(
        num_scalar_prefetch=4,
        grid=(n // tm,),
        in_specs=[pl.BlockSpec((tm, width), lambda i, *_: (i, 0)),
                  pl.BlockSpec((tm, LANES), lambda i, *_: (i, 0))],
        out_specs=pl.BlockSpec(memory_space=pl.ANY),
        scratch_shapes=[pltpu.VMEM((2, nslots, width), BF16), pltpu.VMEM((ROW_BLOCK, width), BF16),
                        pltpu.SemaphoreType.DMA((3,))],
    )
    return pl.pallas_call(
        functools.partial(_dispatch_kernel, tm=tm, nslots=nslots, nblk=nblk, group=256),
        grid_spec=grid_spec,
        out_shape=jax.ShapeDtypeStruct((rows, width), BF16),
        compiler_params=_cparams(("arbitrary",)),
        name="dispatch",
    )(*tables, h2, route)


def _experts_kernel(first_ref, count_ref, wg_ref, wu_ref, wd_ref, x_hbm, y_hbm,
                    xbuf, ybuf, zbuf, wgb_ref, wub_ref, wdb_ref, xsem, ysem, zsem, *, nblk):
    e = pl.program_id(0)
    first = first_ref[e]
    count = count_ref[e]
    dm = ybuf.shape[2]

    def x_copy(g):
        rows = pl.ds(pl.multiple_of(g * ROW_BLOCK, ROW_BLOCK), ROW_BLOCK)
        return pltpu.make_async_copy(x_hbm.at[rows, :], xbuf.at[g & 1], xsem.at[g & 1])

    def y_copy(g):
        rows = pl.ds(pl.multiple_of(g * ROW_BLOCK, ROW_BLOCK), ROW_BLOCK)
        return pltpu.make_async_copy(ybuf.at[g & 1], y_hbm.at[rows, :], ysem.at[g & 1])

    used = first_ref[pl.num_programs(0) - 1] + count_ref[pl.num_programs(0) - 1]

    def zero_blk(b):
        rows = pl.ds(pl.multiple_of(b * ROW_BLOCK, ROW_BLOCK), ROW_BLOCK)
        return pltpu.make_async_copy(zbuf, y_hbm.at[rows, :], zsem)

    @pl.when(e == 0)
    def _():
        zbuf[...] = jnp.zeros_like(zbuf)

        def start(b, carry):
            zero_blk(b).start()
            return carry

        lax.fori_loop(used, nblk, start, 0)

    @pl.when((e == 0) & (used > 0))
    def _():
        x_copy(0).start(priority=1)

    wgb_ref[...] = wg_ref[0].astype(BF16)
    wub_ref[...] = wu_ref[0].astype(BF16)
    wdb_ref[...] = wd_ref[0].astype(BF16)

    def body(b, carry):
        blk = first + b
        slot = blk & 1
        x_copy(blk).wait()

        @pl.when(blk + 1 < used)
        def _():
            x_copy(blk + 1).start(priority=1)

        @pl.when(blk >= 2)
        def _():
            y_copy(blk - 2).wait()

        xb = xbuf[slot, :, :dm]
        aux = xbuf[slot, :, dm:].astype(F32)
        w = jnp.where(aux[:, 4:5] == e.astype(F32), aux[:, 0:1] + aux[:, 1:2], aux[:, 2:3] + aux[:, 3:4])
        g = jnp.dot(xb, wgb_ref[...], preferred_element_type=F32)
        u = jnp.dot(xb, wub_ref[...], preferred_element_type=F32)
        a = (g * jax.nn.sigmoid(g)) * u
        y = jnp.dot(a.astype(BF16), wdb_ref[...], preferred_element_type=F32)
        ybuf[slot] = (y * w).astype(BF16)
        y_copy(blk).start()
        return carry

    lax.fori_loop(0, count, body, 0)

    @pl.when(e == pl.num_programs(0) - 1)
    def _():
        @pl.when(used >= 2)
        def _():
            y_copy(used - 2).wait()

        @pl.when(used >= 1)
        def _():
            y_copy(used - 1).wait()

        _wait_copies(nblk - used, zero_blk(0))


def _experts(first_blk, count_blk, x_rows, w_gate, w_up, w_down):
    rows, width = x_rows.shape
    n_exp, dm, de = w_gate.shape
    nblk = rows // ROW_BLOCK
    wmap = lambda e, *_: (e, 0, 0)
    grid_spec = pltpu.PrefetchScalarGridSpec(
        num_scalar_prefetch=2,
        grid=(n_exp,),
        in_specs=[pl.BlockSpec((1, dm, de), wmap),
                  pl.BlockSpec((1, dm, de), wmap),
                  pl.BlockSpec((1, de, dm), wmap),
                  pl.BlockSpec(memory_space=pl.ANY)],
        out_specs=pl.BlockSpec(memory_space=pl.ANY),
        scratch_shapes=[pltpu.VMEM((2, ROW_BLOCK, width), BF16), pltpu.VMEM((2, ROW_BLOCK, dm), BF16),
                        pltpu.VMEM((ROW_BLOCK, dm), BF16),
                        pltpu.VMEM((dm, de), BF16), pltpu.VMEM((dm, de), BF16), pltpu.VMEM((de, dm), BF16),
                        pltpu.SemaphoreType.DMA((2,)), pltpu.SemaphoreType.DMA((2,)), pltpu.SemaphoreType.DMA(())],
    )
    return pl.pallas_call(
        functools.partial(_experts_kernel, nblk=nblk),
        grid_spec=grid_spec,
        out_shape=jax.ShapeDtypeStruct((rows, dm), BF16),
        compiler_params=_cparams(("arbitrary",)),
        name="experts",
    )(first_blk, count_blk, w_gate, w_up, w_down, x_rows)


def _combine_kernel(src_ref, route_ref, x1_ref, mod_ref, g_ref, y_ref, o_ref, ys_ref, sem, *, tm, nslots):
    i = pl.program_id(0)
    last = pl.num_programs(0) - 1
    slot = i & 1
    nchk = nslots // SEG_ROWS

    def all_chunks(s):
        return pltpu.make_async_copy(y_ref.at[pl.ds(0, nslots), :], ys_ref.at[s], sem.at[s])

    def gather(tile, s):
        for c in range(nchk):
            src = pl.multiple_of(src_ref[tile * nchk + c], SEG_ROWS)
            pltpu.make_async_copy(y_ref.at[pl.ds(src, SEG_ROWS), :],
                                  ys_ref.at[s, pl.ds(c * SEG_ROWS, SEG_ROWS), :], sem.at[s]).start()

    @pl.when(i == 0)
    def _():
        gather(0, 0)

    gather(jnp.minimum(i + 1, last), 1 - slot)
    all_chunks(slot).wait()
    route = route_ref[...]
    scol = lax.broadcasted_iota(I32, (tm, nslots), 1).astype(F32)
    pick = jnp.where((scol == route[:, 2:3]) | (scol == route[:, 3:4]), 1.0, 0.0).astype(BF16)
    moe = jnp.dot(pick, ys_ref[slot], preferred_element_type=F32)
    x = x1_ref[...] + mod_ref[0, 5:6, :] * moe
    ms = jnp.mean(x * x, axis=-1, keepdims=True)
    o_ref[...] = (x * lax.rsqrt(ms + RMS_EPS)) * g_ref[...]

    @pl.when(i == last)
    def _():
        all_chunks(1 - slot).wait()


def _combine(src_tbl, route, x1, mod3, g_final, y_rows, seq, tm, nslots):
    n, dm = x1.shape
    tpb = seq // tm
    grid_spec = pltpu.PrefetchScalarGridSpec(
        num_scalar_prefetch=1,
        grid=(n // tm,),
        in_specs=[pl.BlockSpec((tm, LANES), lambda i, *_: (i, 0)),
                  pl.BlockSpec((tm, dm), lambda i, *_: (i, 0)),
                  pl.BlockSpec((1, 6, dm), lambda i, *_: (i // tpb, 0, 0)),
                  pl.BlockSpec((1, dm), lambda i, *_: (0, 0)),
                  pl.BlockSpec(memory_space=pl.ANY)],
        out_specs=pl.BlockSpec((tm, dm), lambda i, *_: (i, 0)),
        scratch_shapes=[pltpu.VMEM((2, nslots, dm), BF16), pltpu.SemaphoreType.DMA((2,))],
    )
    return pl.pallas_call(
        functools.partial(_combine_kernel, tm=tm, nslots=nslots),
        grid_spec=grid_spec,
        out_shape=jax.ShapeDtypeStruct((n, dm), F32),
        compiler_params=_cparams(("arbitrary",)),
        name="combine",
    )(src_tbl, route, x1, mod3, g_final, y_rows)


def _rope_tables(positions):
    half = HEAD_DIM // 2
    inv_freq = ROPE_THETA ** (-jnp.arange(half, dtype=F32) * (2.0 / HEAD_DIM))
    freq = jnp.tile(inv_freq, LANES // half)
    sign = jnp.tile(jnp.concatenate([-jnp.ones((half,), F32), jnp.ones((half,), F32)]), LANES // HEAD_DIM)
    ang = positions.astype(F32).reshape(-1, 1) * freq
    return jnp.cos(ang), jnp.sin(ang) * sign


def kernel(x, c, positions, w_ada, b_ada, g_mix, w_in, sink_logits, w_branch_a, w_branch_b, w_out, g_ffn,
           w_group, b_group, w_route, b_route, w_expert_gate, w_expert_up, w_expert_down, g_final):
    batch, seq, dm = x.shape
    n = batch * seq
    assert w_ada.shape[0] == 1, "one layer"
    x2 = x.reshape(n, dm)

    c8 = jnp.pad(c, ((0, 8 - batch), (0, 0)))
    mod = _ada(c8, w_ada[0], b_ada[0].reshape(1, -1))
    mod3 = mod[:batch].reshape(batch, 6, dm)

    cos_t, sin_t = _rope_tables(positions)
    outs = _inproj(x2, mod3, g_mix[0].reshape(1, dm), cos_t, sin_t, w_in[0].astype(BF16), batch, seq)
    qkv = outs[:9]
    qb, kb, vb, sga, sgb = outs[9:]

    dil_outs = [_dil_attention(qkv[3 * g], qkv[3 * g + 1], qkv[3 * g + 2]) for g in range(len(DILATIONS))]
    ob = _swa_attention(sink_logits[0], qb.reshape(batch, seq, -1), kb.reshape(batch, seq, -1),
                        vb.reshape(batch, seq, -1)).reshape(n, -1)

    pad = LANES - N_GROUPS - N_EXPERTS
    wr = jnp.concatenate([w_group[0], w_route[0], jnp.zeros((dm, pad), F32)], axis=1).astype(BF16)
    br = jnp.concatenate([b_group[0], b_route[0], jnp.zeros((pad,), F32)]).reshape(1, LANES)
    x1, h2, route, seg_lens = _outproj(dil_outs, ob, sga, sgb, x2, mod3, g_ffn[0].reshape(1, dm),
                                       w_branch_a[0].astype(BF16), w_branch_b[0].astype(BF16), w_out[0].astype(BF16),
                                       wr, br, batch, seq)

    ntiles = n // MOE_TILE
    nslots = 2 * MOE_TILE + N_EXPERTS * SEG_ROWS
    nblk = -(-(2 * n + ntiles * N_EXPERTS * (SEG_ROWS - 1)) // ROW_BLOCK) + N_EXPERTS
    lens = seg_lens[:, 0, :N_EXPERTS].astype(I32)
    tot = jnp.sum(lens, axis=0)
    padded = (tot + ROW_BLOCK - 1) // ROW_BLOCK * ROW_BLOCK
    pend = jnp.cumsum(padded)
    pstart = pend - padded
    base = pstart[None, :] + jnp.cumsum(lens, axis=0) - lens
    nused = pend[-1:] // ROW_BLOCK
    nchk = nslots // SEG_ROWS
    run_end = jnp.cumsum(lens // SEG_ROWS, axis=1)
    chunk = jnp.arange(nchk, dtype=I32)
    owner = jnp.sum((run_end[:, None, :] <= chunk[None, :, None]).astype(I32), axis=2)
    live = owner < N_EXPERTS
    is_owner = owner[:, :, None] == jnp.arange(N_EXPERTS, dtype=I32)[None, None, :]
    run_row0 = base - (run_end - lens // SEG_ROWS) * SEG_ROWS
    row = jnp.sum(jnp.where(is_owner, run_row0[:, None, :], 0), axis=2) + chunk[None, :] * SEG_ROWS
    spare = nblk * ROW_BLOCK + (jnp.arange(ntiles, dtype=I32)[:, None] % 2) * nslots + chunk[None, :] * SEG_ROWS
    dst_tbl = jnp.where(live, row, spare).reshape(-1)
    src_tbl = jnp.where(live, row, chunk[None, :] * SEG_ROWS).reshape(-1)
    tables = (dst_tbl, pstart + tot, (padded - tot) // SEG_ROWS, nused)

    x_rows = _dispatch(tables, h2, route, nblk, MOE_TILE, nslots)
    y_rows = _experts(pstart // ROW_BLOCK, padded // ROW_BLOCK, x_rows,
                      w_expert_gate[0], w_expert_up[0], w_expert_down[0])
    out = _combine(src_tbl, route, x1, mod3, g_final.reshape(1, dm), y_rows, seq, MOE_TILE, nslots)
    return out.reshape(batch, seq, dm)
```

```python
import functools

import jax
import jax.numpy as jnp
from jax import lax
from jax.experimental import pallas as pl
from jax.experimental.pallas import tpu as pltpu

F32 = jnp.float32
BF16 = jnp.bfloat16
I32 = jnp.int32

HEAD_DIM = 64
ROPE_THETA = 10000.0
RMS_EPS = 1e-6
NEG_INF = -1e30
Q_SCALE = HEAD_DIM ** -0.5
DILATIONS = (1, 4, 16)
DIL_HALF_WINDOW = 64
DIL_GROUP_W = 256
SWA_WINDOW = 128
N_GROUPS = 4
EXPERTS_PER_GROUP = 8
N_EXPERTS = 32
LANES = 128
ROW_BLOCK = 512
SEG_ROWS = 16
MOE_TILE = 512
SLOT_GROUP = 256
VMEM_LIMIT = 56 * 1024 * 1024


def _cparams(sem):
    return pltpu.CompilerParams(dimension_semantics=sem, vmem_limit_bytes=VMEM_LIMIT)


def _ada_kernel(c_ref, w_ref, b_ref, o_ref):
    c = c_ref[...]
    cs = c * jax.nn.sigmoid(c)
    o_ref[...] = jnp.dot(cs.astype(BF16), w_ref[...].astype(BF16), preferred_element_type=F32) + b_ref[...]


def _ada(c8, w_ada, b_ada):
    d, n = w_ada.shape
    tn = 1536
    return pl.pallas_call(
        _ada_kernel,
        grid=(n // tn,),
        in_specs=[pl.BlockSpec((8, d), lambda j: (0, 0)),
                  pl.BlockSpec((d, tn), lambda j: (0, j)),
                  pl.BlockSpec((1, tn), lambda j: (0, j))],
        out_specs=pl.BlockSpec((8, tn), lambda j: (0, j)),
        out_shape=jax.ShapeDtypeStruct((8, n), F32),
        compiler_params=_cparams(("arbitrary",)),
        name="ada",
    )(c8, w_ada, b_ada)


def _rms_mod(x, g, shift, scale):
    ms = jnp.mean(x * x, axis=-1, keepdims=True)
    return (x * lax.rsqrt(ms + RMS_EPS)) * (g * (1.0 + scale)) + shift


def _inproj_kernel(x_ref, mod_ref, g_ref, cos_ref, sin_ref, w_ref,
                   q0_ref, k0_ref, v0_ref, q1_ref, k1_ref, v1_ref, q2_ref, k2_ref, v2_ref,
                   qb_ref, kb_ref, vb_ref, sga_ref, sgb_ref, stg_ref, *, tm):
    h = _rms_mod(x_ref[...], g_ref[...], mod_ref[0, 0:1, :], mod_ref[0, 1:2, :])
    hb = h.astype(BF16)
    cos = cos_ref[...]
    sin = sin_ref[...]
    lane = lax.broadcasted_iota(I32, (tm, LANES), 1)
    first_half = (lane & 32) == 0
    low = lane < 64

    def proj(c0, width):
        return jnp.dot(hb, w_ref[:, c0:c0 + width], preferred_element_type=F32)

    def rope(t):
        rot = jnp.where(first_half, pltpu.roll(t, 96, 1), pltpu.roll(t, 32, 1))
        return t * cos + rot * sin

    def rope256(p):
        return jnp.concatenate([rope(p[:, :LANES]), rope(p[:, LANES:])], axis=1)

    def store_group(ref, val, d):
        if d == 1:
            ref[0, 0] = val.astype(BF16)
        else:
            for c in range(2):
                stg_ref[c] = val[:, c * LANES:(c + 1) * LANES]
            for r in range(d):
                for c in range(2):
                    ref[0, r, :, c * LANES:(c + 1) * LANES] = (
                        stg_ref[c, pl.ds(r, tm // d, stride=d), :].astype(BF16))

    q_refs = (q0_ref, q1_ref, q2_ref)
    k_refs = (k0_ref, k1_ref, k2_ref)
    v_refs = (v0_ref, v1_ref, v2_ref)
    for g, d in enumerate(DILATIONS):
        store_group(q_refs[g], rope256(proj(g * 256, 256)) * Q_SCALE, d)
        store_group(k_refs[g], rope256(proj(768 + g * 256, 256)), d)
        store_group(v_refs[g], proj(1536 + g * 256, 256), d)
    for j in range(2):
        qb_ref[:, j * 256:(j + 1) * 256] = (rope256(proj(2304 + j * 256, 256)) * Q_SCALE).astype(BF16)
    kv = proj(2816, 256)
    kb = rope(kv[:, :LANES])
    vb = kv[:, LANES:]
    kb_sw = pltpu.roll(kb, 64, 1)
    vb_sw = pltpu.roll(vb, 64, 1)
    kb_ref[:, :LANES] = jnp.where(low, kb, kb_sw).astype(BF16)
    kb_ref[:, LANES:] = jnp.where(low, kb_sw, kb).astype(BF16)
    vb_ref[:, :LANES] = jnp.where(low, vb, vb_sw).astype(BF16)
    vb_ref[:, LANES:] = jnp.where(low, vb_sw, vb).astype(BF16)
    for j in range(4):
        sga_ref[:, j * 256:(j + 1) * 256] = jax.nn.sigmoid(proj(3072 + j * 256, 256)).astype(BF16)
        sgb_ref[:, j * 256:(j + 1) * 256] = jax.nn.sigmoid(proj(4096 + j * 256, 256)).astype(BF16)


def _inproj(x2, mod3, g_mix, cos_t, sin_t, w_in_bf, batch, seq):
    n, dm = x2.shape
    tm = 512
    tpb = seq // tm
    grid = (n // tm,)
    row = lambda i: (i, 0)
    strided_specs, strided_shapes = [], []
    for d in DILATIONS:
        for _ in range(3):
            strided_specs.append(pl.BlockSpec((1, d, tm // d, DIL_GROUP_W), lambda i: (i // tpb, 0, i % tpb, 0)))
            strided_shapes.append(jax.ShapeDtypeStruct((batch, d, seq // d, DIL_GROUP_W), BF16))
    out_specs = strided_specs + [
        pl.BlockSpec((tm, 512), row), pl.BlockSpec((tm, 256), row), pl.BlockSpec((tm, 256), row),
        pl.BlockSpec((tm, dm), row), pl.BlockSpec((tm, dm), row)]
    out_shapes = strided_shapes + [
        jax.ShapeDtypeStruct((n, 512), BF16), jax.ShapeDtypeStruct((n, 256), BF16),
        jax.ShapeDtypeStruct((n, 256), BF16), jax.ShapeDtypeStruct((n, dm), BF16),
        jax.ShapeDtypeStruct((n, dm), BF16)]
    return pl.pallas_call(
        functools.partial(_inproj_kernel, tm=tm),
        grid=grid,
        in_specs=[pl.BlockSpec((tm, dm), row),
                  pl.BlockSpec((1, 6, dm), lambda i: (i // tpb, 0, 0)),
                  pl.BlockSpec((1, dm), lambda i: (0, 0)),
                  pl.BlockSpec((tm, LANES), row),
                  pl.BlockSpec((tm, LANES), row),
                  pl.BlockSpec(w_in_bf.shape, lambda i: (0, 0))],
        out_specs=out_specs,
        out_shape=out_shapes,
        scratch_shapes=[pltpu.VMEM((2, tm, LANES), F32)],
        compiler_params=_cparams(("arbitrary",)),
        name="inproj",
    )(x2, mod3, g_mix, cos_t, sin_t, w_in_bf)


def _split_heads(q2, low):
    zero = jnp.zeros_like(q2)
    return jnp.concatenate([jnp.where(low, q2, zero), jnp.where(low, zero, q2)], axis=0)


def _band_softmax(qst, k2, v2, mask, sinks):
    s = lax.dot_general(qst, k2, (((1,), (1,)), ((), ())), preferred_element_type=F32)
    s = jnp.where(mask, s, NEG_INF)
    rows, tk = s.shape
    m = jnp.max(s, axis=-1, keepdims=True)
    if sinks is not None:
        seg = rows // len(sinks)
        m = jnp.concatenate([jnp.maximum(m[h * seg:(h + 1) * seg], sk) for h, sk in enumerate(sinks)], axis=0)
    m = jnp.broadcast_to(m, (rows, LANES))
    e = jnp.concatenate([jnp.exp(s[:, c * LANES:(c + 1) * LANES] - m) for c in range(tk // LANES)], axis=1)
    v_ones = jnp.concatenate([v2, jnp.ones((tk, LANES), BF16)], axis=1)
    od = jnp.dot(e.astype(BF16), v_ones, preferred_element_type=F32)
    o, den = od[:, :LANES], od[:, LANES:]
    if sinks is not None:
        den = jnp.concatenate([den[h * seg:(h + 1) * seg] + jnp.exp(sk - m[h * seg:(h + 1) * seg])
                               for h, sk in enumerate(sinks)], axis=0)
    return o / den, m, den


def _band_mask(qs, ks, nstack, tq, tk, window):
    row = lax.broadcasted_iota(I32, (nstack * tq, tk), 0) & (tq - 1)
    col = lax.broadcasted_iota(I32, (nstack * tq, tk), 1)
    return jnp.abs((ks + col) - (qs + row)) <= window


def _dil_kernel(q_ref, k_ref, v_ref, o_ref, l_ref, *, length, tq, tk):
    low = lax.broadcasted_iota(I32, (tq, LANES), 1) < 64

    nq = length // tq

    def body(j, carry):
        r = j // nq
        qs = pl.multiple_of((j % nq) * tq, tq)
        ks = pl.multiple_of(jnp.clip(qs - DIL_HALF_WINDOW, 0, length - tk), DIL_HALF_WINDOW)
        mask = _band_mask(qs, ks, 2, tq, tk, DIL_HALF_WINDOW)
        for c in range(DIL_GROUP_W // LANES):
            cs = slice(c * LANES, (c + 1) * LANES)
            qst = _split_heads(q_ref[r, pl.ds(qs, tq), cs], low)
            o, m, den = _band_softmax(qst, k_ref[r, pl.ds(ks, tk), cs], v_ref[r, pl.ds(ks, tk), cs], mask, None)
            lse = m + jnp.log(den)
            o_ref[r, pl.ds(qs, tq), cs] = jnp.where(low, o[:tq], o[tq:]).astype(BF16)
            l_ref[r, pl.ds(qs, tq), cs] = jnp.where(low, lse[:tq], lse[tq:])
        return carry

    lax.fori_loop(0, q_ref.shape[0] * nq, body, 0, unroll=4)


def _dil_attention(q, k, v):
    batch, d, length, w = q.shape
    tq, tk = 128, 256
    spec = pl.BlockSpec((None, d, length, w), lambda b: (b, 0, 0, 0))
    return pl.pallas_call(
        functools.partial(_dil_kernel, length=length, tq=tq, tk=tk),
        grid=(batch,),
        in_specs=[spec, spec, spec],
        out_specs=[spec, spec],
        out_shape=[jax.ShapeDtypeStruct(q.shape, BF16), jax.ShapeDtypeStruct(q.shape, F32)],
        compiler_params=_cparams(("arbitrary",)),
        name=f"dil{d}",
    )(q, k, v)


def _swa_kernel(sink_ref, q_ref, k_ref, v_ref, o_ref, *, length, tq, tk):
    low = lax.broadcasted_iota(I32, (tq, LANES), 1) < 64
    nblk = q_ref.shape[1] // LANES

    def body(j, carry):
        qs = pl.multiple_of(j * tq, tq)
        ks = pl.multiple_of(jnp.clip(qs - SWA_WINDOW, 0, length - tk), SWA_WINDOW)
        mask = _band_mask(qs, ks, 2, tq, tk, SWA_WINDOW)
        for b in range(nblk):
            cs = slice((b // 2) * LANES, (b // 2 + 1) * LANES)
            bs = slice(b * LANES, (b + 1) * LANES)
            qst = _split_heads(q_ref[pl.ds(qs, tq), bs], low)
            sinks = (sink_ref[2 * b], sink_ref[2 * b + 1])
            o, _, _ = _band_softmax(qst, k_ref[pl.ds(ks, tk), cs], v_ref[pl.ds(ks, tk), cs], mask, sinks)
            o_ref[pl.ds(qs, tq), bs] = jnp.where(low, o[:tq], o[tq:]).astype(BF16)
        return carry

    lax.fori_loop(0, length // tq, body, 0, unroll=4)


def _swa_attention(sink, q, k, v):
    batch, length, qw = q.shape
    tq, tk = 128, 384
    return pl.pallas_call(
        functools.partial(_swa_kernel, length=length, tq=tq, tk=tk),
        grid=(batch,),
        in_specs=[pl.BlockSpec(memory_space=pltpu.SMEM),
                  pl.BlockSpec((None, length, qw), lambda b: (b, 0, 0)),
                  pl.BlockSpec((None, length, k.shape[2]), lambda b: (b, 0, 0)),
                  pl.BlockSpec((None, length, v.shape[2]), lambda b: (b, 0, 0))],
        out_specs=pl.BlockSpec((None, length, qw), lambda b: (b, 0, 0)),
        out_shape=jax.ShapeDtypeStruct(q.shape, BF16),
        compiler_params=_cparams(("arbitrary",)),
        name="swa",
    )(sink, q, k, v)


def _route_rows(logits):
    lane = lax.broadcasted_iota(I32, logits.shape, 1).astype(F32)
    big = 1e9
    is_g = lane < N_GROUPS
    gl = jnp.where(is_g, logits, NEG_INF)
    gmax = jnp.max(gl, axis=-1, keepdims=True)
    gsel = jnp.min(jnp.where(is_g & (gl == gmax), lane, big), axis=-1, keepdims=True)
    gw = 1.0 / jnp.sum(jnp.where(is_g, jnp.exp(gl - gmax), 0.0), axis=-1, keepdims=True)
    e_lo = N_GROUPS + gsel * EXPERTS_PER_GROUP
    in_grp = (lane >= e_lo) & (lane < e_lo + EXPERTS_PER_GROUP)
    el = jnp.where(in_grp, logits, NEG_INF)
    m1 = jnp.max(el, axis=-1, keepdims=True)
    i1 = jnp.min(jnp.where(in_grp & (el == m1), lane, big), axis=-1, keepdims=True)
    el2 = jnp.where(lane == i1, NEG_INF, el)
    m2 = jnp.max(el2, axis=-1, keepdims=True)
    i2 = jnp.min(jnp.where(in_grp & (lane != i1) & (el2 == m2), lane, big), axis=-1, keepdims=True)
    t = jnp.exp(m2 - m1)
    tw1 = gw / (1.0 + t)
    tw2 = gw * t / (1.0 + t)
    out = jnp.where(lane == 0, tw1, 0.0)
    out = jnp.where(lane == 1, tw2, out)
    out = jnp.where(lane == 2, i1 - N_GROUPS, out)
    return jnp.where(lane == 3, i2 - N_GROUPS, out)


def _outproj_kernel(o0_ref, l0_ref, o1_ref, l1_ref, o2_ref, l2_ref, ob_ref, sga_ref, sgb_ref, x_ref,
                    mod_ref, g_ref, wa_ref, wb_ref, wo_ref, wr_ref, br_ref,
                    x1_ref, h2_ref, route_ref, len_ref,
                    so1_ref, sl1_ref, so2_ref, sl2_ref, *, tm, sub):
    dm = x_ref.shape[1]
    for (o_ref, l_ref, so_ref, sl_ref, d) in ((o1_ref, l1_ref, so1_ref, sl1_ref, DILATIONS[1]),
                                              (o2_ref, l2_ref, so2_ref, sl2_ref, DILATIONS[2])):
        for r in range(d):
            for c in range(2):
                cs = slice(c * LANES, (c + 1) * LANES)
                so_ref[c, pl.ds(r, tm // d, stride=d), :] = o_ref[0, r, :, cs].astype(F32)
                sl_ref[c, pl.ds(r, tm // d, stride=d), :] = l_ref[0, r, :, cs]
    mr = lax.broadcasted_iota(I32, (LANES, LANES), 0)
    mc = lax.broadcasted_iota(I32, (LANES, LANES), 1)
    move_hi = jnp.where(((mr < 2) & (mc == 2 * mr)) | ((mr >= 2) & (mr < 4) & (mc == mr + 2)), 1.0, 0.0).astype(BF16)
    move_lo = jnp.where((mr < 2) & (mc == 2 * mr + 1), 1.0, 0.0).astype(BF16)
    for t in range(tm // sub):
        rs = slice(t * sub, (t + 1) * sub)
        both = lambda ref: jnp.concatenate([ref[0, rs, :], ref[1, rs, :]], axis=1)
        o0, l0 = o0_ref[0, 0, rs, :].astype(F32), l0_ref[0, 0, rs, :]
        o1, l1, o2, l2 = both(so1_ref), both(sl1_ref), both(so2_ref), both(sl2_ref)
        mx = jnp.maximum(jnp.maximum(l0, l1), l2)
        w0, w1, w2 = jnp.exp(l0 - mx), jnp.exp(l1 - mx), jnp.exp(l2 - mx)
        o_a = (w0 * o0 + w1 * o1 + w2 * o2) / (w0 + w1 + w2)
        y_a = jnp.dot(o_a.astype(BF16), wa_ref[...], preferred_element_type=F32)
        y_b = jnp.dot(ob_ref[rs, :], wb_ref[...], preferred_element_type=F32)
        merged = sga_ref[rs, :].astype(F32) * y_a + sgb_ref[rs, :].astype(F32) * y_b
        mix = jnp.dot(merged.astype(BF16), wo_ref[...], preferred_element_type=F32)
        x1 = x_ref[rs, :] + mod_ref[0, 2:3, :] * mix
        x1_ref[rs, :] = x1
        h2 = _rms_mod(x1, g_ref[...], mod_ref[0, 3:4, :], mod_ref[0, 4:5, :]).astype(BF16)
        h2_ref[rs, :dm] = h2
        logits = jnp.dot(h2, wr_ref[...], preferred_element_type=F32) + br_ref[...]
        rt = _route_rows(logits)
        route_ref[rs, :] = rt
        hi = rt.astype(BF16)
        lo = (rt - hi.astype(F32)).astype(BF16)
        aux = (jnp.dot(hi, move_hi, preferred_element_type=F32) + jnp.dot(lo, move_lo, preferred_element_type=F32))
        h2_ref[rs, dm:] = aux.astype(BF16)

    part = route_ref[...]
    e1, e2 = part[:, 2:3], part[:, 3:4]
    lane = lax.broadcasted_iota(I32, (tm, LANES), 1).astype(F32)
    onehot = jnp.where((lane == e1) | (lane == e2), 1.0, 0.0)
    rr = lax.broadcasted_iota(I32, (tm, tm), 0)
    cc = lax.broadcasted_iota(I32, (tm, tm), 1)
    tri = jnp.where(rr > cc, 1.0, 0.0).astype(BF16)
    prefix = jnp.dot(tri, onehot.astype(BF16), preferred_element_type=F32)
    cnt = jnp.sum(onehot, axis=0, keepdims=True)
    seg_len = jnp.ceil(cnt * (1.0 / SEG_ROWS)) * SEG_ROWS
    ur = lax.broadcasted_iota(I32, (LANES, LANES), 0)
    uc = lax.broadcasted_iota(I32, (LANES, LANES), 1)
    upper = jnp.where(ur < uc, 1.0, 0.0).astype(BF16)
    seg_off = jnp.dot(jnp.broadcast_to(seg_len, (8, LANES)).astype(BF16), upper, preferred_element_type=F32)[0:1, :]
    slot_map = seg_off + prefix
    s1 = jnp.sum(jnp.where(lane == e1, slot_map, 0.0), axis=-1, keepdims=True)
    s2 = jnp.sum(jnp.where(lane == e2, slot_map, 0.0), axis=-1, keepdims=True)
    len_ref[0] = jnp.broadcast_to(seg_len, (8, LANES))
    route_ref[...] = jnp.where(lane == 2, s1, jnp.where(lane == 3, s2, part))


def _outproj(dil_outs, ob, sga, sgb, x2, mod3, g_ffn, wa, wb, wo, wr, br, batch, seq):
    n, dm = x2.shape
    tm = MOE_TILE
    tpb = seq // tm
    row = lambda i: (i, 0)
    const = lambda i: (0, 0)
    in_specs = []
    args = []
    for (o, l), d in zip(dil_outs, DILATIONS):
        spec = pl.BlockSpec((1, d, tm // d, DIL_GROUP_W), lambda i: (i // tpb, 0, i % tpb, 0))
        in_specs += [spec, spec]
        args += [o, l]
    in_specs += [pl.BlockSpec((tm, ob.shape[1]), row), pl.BlockSpec((tm, dm), row), pl.BlockSpec((tm, dm), row),
                 pl.BlockSpec((tm, dm), row),
                 pl.BlockSpec((1, 6, dm), lambda i: (i // tpb, 0, 0)),
                 pl.BlockSpec((1, dm), const),
                 pl.BlockSpec(wa.shape, const), pl.BlockSpec(wb.shape, const), pl.BlockSpec(wo.shape, const),
                 pl.BlockSpec(wr.shape, const), pl.BlockSpec(br.shape, const)]
    args += [ob, sga, sgb, x2, mod3, g_ffn, wa, wb, wo, wr, br]
    return pl.pallas_call(
        functools.partial(_outproj_kernel, tm=tm, sub=512),
        grid=(n // tm,),
        in_specs=in_specs,
        out_specs=[pl.BlockSpec((tm, dm), row), pl.BlockSpec((tm, dm + LANES), row), pl.BlockSpec((tm, LANES), row),
                   pl.BlockSpec((1, 8, LANES), lambda i: (i, 0, 0))],
        out_shape=[jax.ShapeDtypeStruct((n, dm), F32), jax.ShapeDtypeStruct((n, dm + LANES), BF16),
                   jax.ShapeDtypeStruct((n, LANES), F32), jax.ShapeDtypeStruct((n // tm, 8, LANES), F32)],
        scratch_shapes=[pltpu.VMEM((2, tm, LANES), F32)] * 4,
        compiler_params=_cparams(("arbitrary",)),
        name="outproj",
    )(*args)


def _wait_copies(count, copy):
    def wait_one(c, carry):
        copy.wait()
        return carry

    lax.fori_loop(0, count, wait_one, 0)


def _dispatch_kernel(dst_ref, live_ref, tstart_ref, tch_ref, nused_ref, h_ref, route_ref, xr_ref,
                     sorted_ref, zero_ref, sem, *, tm, nslots, nblk, group):
    i = pl.program_id(0)
    last = pl.num_programs(0) - 1
    slot = i & 1
    nchk = nslots // SEG_ROWS
    nspare = (2 * nslots) // ROW_BLOCK
    zsem = sem.at[2]

    def wait_tile(tile, s):
        for lo in range(0, nslots, group):
            copy = pltpu.make_async_copy(sorted_ref.at[s, pl.ds(lo, group), :], xr_ref.at[pl.ds(0, group), :], sem.at[s])
            if lo < 2 * tm:
                copy.wait()
            else:
                pl.when(live_ref[tile] > lo)(copy.wait)

    def zero_blk(b):
        return pltpu.make_async_copy(zero_ref, xr_ref.at[pl.ds(pl.multiple_of(b * ROW_BLOCK, ROW_BLOCK), ROW_BLOCK), :], zsem)

    def zero_seg(dst):
        return pltpu.make_async_copy(zero_ref.at[pl.ds(0, SEG_ROWS), :], xr_ref.at[pl.ds(dst, SEG_ROWS), :], zsem)

    def start_blocks(lo, hi):
        def blk(b, carry):
            zero_blk(b).start()
            return carry

        lax.fori_loop(lo, hi, blk, 0)

    ntail = lax.fori_loop(0, N_EXPERTS, lambda e, acc: acc + tch_ref[e], 0)

    @pl.when(i == 0)
    def _():
        zero_ref[...] = jnp.zeros_like(zero_ref)
        start_blocks(nblk, nblk + nspare)
        _wait_copies(nspare, zero_blk(0))
        def tail(e, carry):
            def chunk(c, carry2):
                zero_seg(pl.multiple_of(tstart_ref[e] + c * SEG_ROWS, SEG_ROWS)).start()
                return carry2

            lax.fori_loop(0, tch_ref[e], chunk, 0)
            return carry

        lax.fori_loop(0, N_EXPERTS, tail, 0)
        start_blocks(nused_ref[0], nblk)

    @pl.when(i >= 2)
    def _():
        wait_tile(i - 2, slot)

    pr = lax.broadcasted_iota(I32, (8, LANES), 0)
    pc = lax.broadcasted_iota(I32, (8, LANES), 1)
    lane_pick = jnp.where(pc == pr + 2, 1.0, 0.0)
    slots_t = lax.dot_general(lane_pick, route_ref[...], (((1,), (1,)), ((), ())), preferred_element_type=F32,
                              precision=lax.Precision.HIGHEST)

    def sort_group(lo):
        srow = (lax.broadcasted_iota(I32, (group, tm), 0) + lo).astype(F32)
        pick = jnp.where((srow == slots_t[0:1, :]) | (srow == slots_t[1:2, :]), 1.0, 0.0).astype(BF16)
        sorted_ref[slot, lo:lo + group, :] = jnp.dot(pick, h_ref[...], preferred_element_type=F32).astype(BF16)
        for c in range(lo // SEG_ROWS, (lo + group) // SEG_ROWS):
            dst = pl.multiple_of(dst_ref[i * nchk + c], SEG_ROWS)
            pltpu.make_async_copy(sorted_ref.at[slot, pl.ds(c * SEG_ROWS, SEG_ROWS), :],
                                  xr_ref.at[pl.ds(dst, SEG_ROWS), :], sem.at[slot]).start()

    for lo in range(0, nslots, group):
        if lo < 2 * tm:
            sort_group(lo)
        else:
            pl.when(live_ref[i] > lo)(functools.partial(sort_group, lo))

    @pl.when(i == last)
    def _():
        wait_tile(i, slot)

        @pl.when(i >= 1)
        def _():
            wait_tile(i - 1, 1 - slot)

        _wait_copies(ntail, zero_seg(0))
        _wait_copies(nblk - nused_ref[0], zero_blk(0))


def _dispatch(tables, h2, route, nblk, tm, nslots):
    n, width = h2.shape
    rows = nblk * ROW_BLOCK + 2 * nslots
    grid_spec = pltpu.PrefetchScalarGridSpec(
        num_scalar_prefetch=5,
        grid=(n // tm,),
        in_specs=[pl.BlockSpec((tm, width), lambda i, *_: (i, 0)),
                  pl.BlockSpec((tm, LANES), lambda i, *_: (i, 0))],
        out_specs=pl.BlockSpec(memory_space=pl.ANY),
        scratch_shapes=[pltpu.VMEM((2, nslots, width), BF16), pltpu.VMEM((ROW_BLOCK, width), BF16),
                        pltpu.SemaphoreType.DMA((3,))],
    )
    return pl.pallas_call(
        functools.partial(_dispatch_kernel, tm=tm, nslots=nslots, nblk=nblk, group=SLOT_GROUP),
        grid_spec=grid_spec,
        out_shape=jax.ShapeDtypeStruct((rows, width), BF16),
        compiler_params=_cparams(("arbitrary",)),
        name="dispatch",
    )(*tables, h2, route)


def _experts_kernel(first_ref, count_ref, wg_ref, wu_ref, wd_ref, x_hbm, y_hbm,
                    xbuf, ybuf, zbuf, wgb_ref, wub_ref, wdb_ref, xsem, ysem, zsem, *, nblk):
    e = pl.program_id(0)
    first = first_ref[e]
    count = count_ref[e]
    dm = ybuf.shape[2]

    def x_copy(g):
        rows = pl.ds(pl.multiple_of(g * ROW_BLOCK, ROW_BLOCK), ROW_BLOCK)
        return pltpu.make_async_copy(x_hbm.at[rows, :], xbuf.at[g & 1], xsem.at[g & 1])

    def y_copy(g):
        rows = pl.ds(pl.multiple_of(g * ROW_BLOCK, ROW_BLOCK), ROW_BLOCK)
        return pltpu.make_async_copy(ybuf.at[g & 1], y_hbm.at[rows, :], ysem.at[g & 1])

    used = first_ref[pl.num_programs(0) - 1] + count_ref[pl.num_programs(0) - 1]

    def zero_blk(b):
        rows = pl.ds(pl.multiple_of(b * ROW_BLOCK, ROW_BLOCK), ROW_BLOCK)
        return pltpu.make_async_copy(zbuf, y_hbm.at[rows, :], zsem)

    @pl.when(e == 0)
    def _():
        zbuf[...] = jnp.zeros_like(zbuf)

        def start(b, carry):
            zero_blk(b).start()
            return carry

        lax.fori_loop(used, nblk, start, 0)

    @pl.when((e == 0) & (used > 0))
    def _():
        x_copy(0).start(priority=1)

    wgb_ref[...] = wg_ref[0].astype(BF16)
    wub_ref[...] = wu_ref[0].astype(BF16)
    wdb_ref[...] = wd_ref[0].astype(BF16)

    def body(b, carry):
        blk = first + b
        slot = blk & 1
        x_copy(blk).wait()

        @pl.when(blk + 1 < used)
        def _():
            x_copy(blk + 1).start(priority=1)

        @pl.when(blk >= 2)
        def _():
            y_copy(blk - 2).wait()

        xb = xbuf[slot, :, :dm]
        aux = xbuf[slot, :, dm:].astype(F32)
        w = jnp.where(aux[:, 4:5] == e.astype(F32), aux[:, 0:1] + aux[:, 1:2], aux[:, 2:3] + aux[:, 3:4])
        g = jnp.dot(xb, wgb_ref[...], preferred_element_type=F32)
        u = jnp.dot(xb, wub_ref[...], preferred_element_type=F32)
        a = (g * jax.nn.sigmoid(g)) * u
        y = jnp.dot(a.astype(BF16), wdb_ref[...], preferred_element_type=F32)
        ybuf[slot] = (y * w).astype(BF16)
        y_copy(blk).start()
        return carry

    lax.fori_loop(0, count, body, 0)

    @pl.when(e == pl.num_programs(0) - 1)
    def _():
        @pl.when(used >= 2)
        def _():
            y_copy(used - 2).wait()

        @pl.when(used >= 1)
        def _():
            y_copy(used - 1).wait()

        _wait_copies(nblk - used, zero_blk(0))


def _experts(first_blk, count_blk, x_rows, w_gate, w_up, w_down):
    rows, width = x_rows.shape
    n_exp, dm, de = w_gate.shape
    nblk = rows // ROW_BLOCK
    wmap = lambda e, *_: (e, 0, 0)
    grid_spec = pltpu.PrefetchScalarGridSpec(
        num_scalar_prefetch=2,
        grid=(n_exp,),
        in_specs=[pl.BlockSpec((1, dm, de), wmap),
                  pl.BlockSpec((1, dm, de), wmap),
                  pl.BlockSpec((1, de, dm), wmap),
                  pl.BlockSpec(memory_space=pl.ANY)],
        out_specs=pl.BlockSpec(memory_space=pl.ANY),
        scratch_shapes=[pltpu.VMEM((2, ROW_BLOCK, width), BF16), pltpu.VMEM((2, ROW_BLOCK, dm), BF16),
                        pltpu.VMEM((ROW_BLOCK, dm), BF16),
                        pltpu.VMEM((dm, de), BF16), pltpu.VMEM((dm, de), BF16), pltpu.VMEM((de, dm), BF16),
                        pltpu.SemaphoreType.DMA((2,)), pltpu.SemaphoreType.DMA((2,)), pltpu.SemaphoreType.DMA(())],
    )
    return pl.pallas_call(
        functools.partial(_experts_kernel, nblk=nblk),
        grid_spec=grid_spec,
        out_shape=jax.ShapeDtypeStruct((rows, dm), BF16),
        compiler_params=_cparams(("arbitrary",)),
        name="experts",
    )(first_blk, count_blk, w_gate, w_up, w_down, x_rows)


def _slot_parts(tm, nslots, group):
    return [(0, 2 * tm)] + [(lo, group) for lo in range(2 * tm, nslots, group)]


def _combine_kernel(src_ref, live_ref, route_ref, x1_ref, mod_ref, g_ref, y_ref, o_ref, ys_ref, sem,
                    *, tm, nslots, group):
    i = pl.program_id(0)
    last = pl.num_programs(0) - 1
    slot = i & 1
    nchk = nslots // SEG_ROWS
    parts = _slot_parts(tm, nslots, group)

    def for_live_parts(tile, fn):
        for p, (lo, rows) in enumerate(parts):
            if p == 0:
                fn(lo, rows)
            else:
                pl.when(live_ref[tile] > lo)(functools.partial(fn, lo, rows))

    def gather(tile, s):
        def start(lo, rows):
            for c in range(lo // SEG_ROWS, (lo + rows) // SEG_ROWS):
                src = pl.multiple_of(src_ref[tile * nchk + c], SEG_ROWS)
                pltpu.make_async_copy(y_ref.at[pl.ds(src, SEG_ROWS), :],
                                      ys_ref.at[s, pl.ds(c * SEG_ROWS, SEG_ROWS), :], sem.at[s]).start()

        for_live_parts(tile, start)

    def wait(tile, s):
        for_live_parts(tile, lambda lo, rows: pltpu.make_async_copy(
            y_ref.at[pl.ds(0, rows), :], ys_ref.at[s, pl.ds(lo, rows), :], sem.at[s]).wait())

    @pl.when(i == 0)
    def _():
        gather(0, 0)

    nxt = jnp.minimum(i + 1, last)
    gather(nxt, 1 - slot)
    wait(i, slot)
    route = route_ref[...]

    def finish(rows):
        scol = lax.broadcasted_iota(I32, (tm, rows), 1).astype(F32)
        pick = jnp.where((scol == route[:, 2:3]) | (scol == route[:, 3:4]), 1.0, 0.0).astype(BF16)
        moe = jnp.dot(pick, ys_ref[slot, :rows, :], preferred_element_type=F32)
        x = x1_ref[...] + mod_ref[0, 5:6, :] * moe
        ms = jnp.mean(x * x, axis=-1, keepdims=True)
        o_ref[...] = (x * lax.rsqrt(ms + RMS_EPS)) * g_ref[...]

    ends = [lo + rows for lo, rows in parts]
    for p, end in enumerate(ends):
        above = live_ref[i] > (ends[p - 1] if p else -1)
        cond = above if p == len(ends) - 1 else above & (live_ref[i] <= end)
        pl.when(cond)(functools.partial(finish, end))

    @pl.when(i == last)
    def _():
        wait(nxt, 1 - slot)


def _combine(src_tbl, live_tbl, route, x1, mod3, g_final, y_rows, seq, tm, nslots):
    n, dm = x1.shape
    tpb = seq // tm
    grid_spec = pltpu.PrefetchScalarGridSpec(
        num_scalar_prefetch=2,
        grid=(n // tm,),
        in_specs=[pl.BlockSpec((tm, LANES), lambda i, *_: (i, 0)),
                  pl.BlockSpec((tm, dm), lambda i, *_: (i, 0)),
                  pl.BlockSpec((1, 6, dm), lambda i, *_: (i // tpb, 0, 0)),
                  pl.BlockSpec((1, dm), lambda i, *_: (0, 0)),
                  pl.BlockSpec(memory_space=pl.ANY)],
        out_specs=pl.BlockSpec((tm, dm), lambda i, *_: (i, 0)),
        scratch_shapes=[pltpu.VMEM((2, nslots, dm), BF16), pltpu.SemaphoreType.DMA((2,))],
    )
    return pl.pallas_call(
        functools.partial(_combine_kernel, tm=tm, nslots=nslots, group=SLOT_GROUP),
        grid_spec=grid_spec,
        out_shape=jax.ShapeDtypeStruct((n, dm), F32),
        compiler_params=_cparams(("arbitrary",)),
        name="combine",
    )(src_tbl, live_tbl, route, x1, mod3, g_final, y_rows)


def _rope_tables(positions):
    half = HEAD_DIM // 2
    inv_freq = ROPE_THETA ** (-jnp.arange(half, dtype=F32) * (2.0 / HEAD_DIM))
    freq = jnp.tile(inv_freq, LANES // half)
    sign = jnp.tile(jnp.concatenate([-jnp.ones((half,), F32), jnp.ones((half,), F32)]), LANES // HEAD_DIM)
    ang = positions.astype(F32).reshape(-1, 1) * freq
    return jnp.cos(ang), jnp.sin(ang) * sign


def kernel(x, c, positions, w_ada, b_ada, g_mix, w_in, sink_logits, w_branch_a, w_branch_b, w_out, g_ffn,
           w_group, b_group, w_route, b_route, w_expert_gate, w_expert_up, w_expert_down, g_final):
    batch, seq, dm = x.shape
    n = batch * seq
    assert w_ada.shape[0] == 1, "one layer"
    x2 = x.reshape(n, dm)

    c8 = jnp.pad(c, ((0, 8 - batch), (0, 0)))
    mod = _ada(c8, w_ada[0], b_ada[0].reshape(1, -1))
    mod3 = mod[:batch].reshape(batch, 6, dm)

    cos_t, sin_t = _rope_tables(positions)
    outs = _inproj(x2, mod3, g_mix[0].reshape(1, dm), cos_t, sin_t, w_in[0].astype(BF16), batch, seq)
    qkv = outs[:9]
    qb, kb, vb, sga, sgb = outs[9:]

    dil_outs = [_dil_attention(qkv[3 * g], qkv[3 * g + 1], qkv[3 * g + 2]) for g in range(len(DILATIONS))]
    ob = _swa_attention(sink_logits[0], qb.reshape(batch, seq, -1), kb.reshape(batch, seq, -1),
                        vb.reshape(batch, seq, -1)).reshape(n, -1)

    pad = LANES - N_GROUPS - N_EXPERTS
    wr = jnp.concatenate([w_group[0], w_route[0], jnp.zeros((dm, pad), F32)], axis=1).astype(BF16)
    br = jnp.concatenate([b_group[0], b_route[0], jnp.zeros((pad,), F32)]).reshape(1, LANES)
    x1, h2, route, seg_lens = _outproj(dil_outs, ob, sga, sgb, x2, mod3, g_ffn[0].reshape(1, dm),
                                       w_branch_a[0].astype(BF16), w_branch_b[0].astype(BF16), w_out[0].astype(BF16),
                                       wr, br, batch, seq)

    ntiles = n // MOE_TILE
    nslots = 2 * MOE_TILE + N_EXPERTS * SEG_ROWS
    nblk = -(-(2 * n + ntiles * N_EXPERTS * (SEG_ROWS - 1)) // ROW_BLOCK) + N_EXPERTS
    lens = seg_lens[:, 0, :N_EXPERTS].astype(I32)
    tot = jnp.sum(lens, axis=0)
    padded = (tot + ROW_BLOCK - 1) // ROW_BLOCK * ROW_BLOCK
    pend = jnp.cumsum(padded)
    pstart = pend - padded
    base = pstart[None, :] + jnp.cumsum(lens, axis=0) - lens
    nused = pend[-1:] // ROW_BLOCK
    nchk = nslots // SEG_ROWS
    run_end = jnp.cumsum(lens // SEG_ROWS, axis=1)
    chunk = jnp.arange(nchk, dtype=I32)
    owner = jnp.sum((run_end[:, None, :] <= chunk[None, :, None]).astype(I32), axis=2)
    live = owner < N_EXPERTS
    is_owner = owner[:, :, None] == jnp.arange(N_EXPERTS, dtype=I32)[None, None, :]
    run_row0 = base - (run_end - lens // SEG_ROWS) * SEG_ROWS
    row = jnp.sum(jnp.where(is_owner, run_row0[:, None, :], 0), axis=2) + chunk[None, :] * SEG_ROWS
    spare = nblk * ROW_BLOCK + (jnp.arange(ntiles, dtype=I32)[:, None] % 2) * nslots + chunk[None, :] * SEG_ROWS
    dst_tbl = jnp.where(live, row, spare).reshape(-1)
    src_tbl = jnp.where(live, row, chunk[None, :] * SEG_ROWS).reshape(-1)
    live_tbl = jnp.sum(lens, axis=1)
    tables = (dst_tbl, live_tbl, pstart + tot, (padded - tot) // SEG_ROWS, nused)

    x_rows = _dispatch(tables, h2, route, nblk, MOE_TILE, nslots)
    y_rows = _experts(pstart // ROW_BLOCK, padded // ROW_BLOCK, x_rows,
                      w_expert_gate[0], w_expert_up[0], w_expert_down[0])
    out = _combine(src_tbl, live_tbl, route, x1, mod3, g_final.reshape(1, dm), y_rows, seq, MOE_TILE, nslots)
    return out.reshape(batch, seq, dm)
```

```python
import functools

import jax
import jax.numpy as jnp
from jax import lax
from jax.experimental import pallas as pl
from jax.experimental.pallas import tpu as pltpu

F32 = jnp.float32
BF16 = jnp.bfloat16
I32 = jnp.int32

HEAD_DIM = 64
ROPE_THETA = 10000.0
RMS_EPS = 1e-6
NEG_INF = -1e30
Q_SCALE = HEAD_DIM ** -0.5
DILATIONS = (1, 4, 16)
DIL_HALF_WINDOW = 64
DIL_GROUP_W = 256
SWA_WINDOW = 128
N_GROUPS = 4
EXPERTS_PER_GROUP = 8
N_EXPERTS = 32
LANES = 128
ROW_BLOCK = 512
SEG_ROWS = 16
MOE_TILE = 512
SLOT_GROUP = 256
VMEM_LIMIT = 56 * 1024 * 1024


def _cparams(sem):
    return pltpu.CompilerParams(dimension_semantics=sem, vmem_limit_bytes=VMEM_LIMIT)


def _ada_kernel(c_ref, w_ref, b_ref, o_ref):
    c = c_ref[...]
    cs = c * jax.nn.sigmoid(c)
    o_ref[...] = jnp.dot(cs.astype(BF16), w_ref[...].astype(BF16), preferred_element_type=F32) + b_ref[...]


def _ada(c8, w_ada, b_ada):
    d, n = w_ada.shape
    tn = 1536
    return pl.pallas_call(
        _ada_kernel,
        grid=(n // tn,),
        in_specs=[pl.BlockSpec((8, d), lambda j: (0, 0)),
                  pl.BlockSpec((d, tn), lambda j: (0, j)),
                  pl.BlockSpec((1, tn), lambda j: (0, j))],
        out_specs=pl.BlockSpec((8, tn), lambda j: (0, j)),
        out_shape=jax.ShapeDtypeStruct((8, n), F32),
        compiler_params=_cparams(("arbitrary",)),
        name="ada",
    )(c8, w_ada, b_ada)


def _rms_mod(x, g, shift, scale):
    ms = jnp.mean(x * x, axis=-1, keepdims=True)
    return (x * lax.rsqrt(ms + RMS_EPS)) * (g * (1.0 + scale)) + shift


def _inproj_kernel(x_ref, mod_ref, g_ref, cos_ref, sin_ref, w_ref,
                   q0_ref, k0_ref, v0_ref, q1_ref, k1_ref, v1_ref, q2_ref, k2_ref, v2_ref,
                   qb_ref, kb_ref, vb_ref, sga_ref, sgb_ref, stg_ref, *, tm):
    h = _rms_mod(x_ref[...], g_ref[...], mod_ref[0, 0:1, :], mod_ref[0, 1:2, :])
    hb = h.astype(BF16)
    cos = cos_ref[...]
    sin = sin_ref[...]
    lane = lax.broadcasted_iota(I32, (tm, LANES), 1)
    first_half = (lane & 32) == 0
    low = lane < 64

    def proj(c0, width):
        return jnp.dot(hb, w_ref[:, c0:c0 + width], preferred_element_type=F32)

    def rope(t):
        rot = jnp.where(first_half, pltpu.roll(t, 96, 1), pltpu.roll(t, 32, 1))
        return t * cos + rot * sin

    def rope256(p):
        return jnp.concatenate([rope(p[:, :LANES]), rope(p[:, LANES:])], axis=1)

    def store_group(ref, val, d):
        if d == 1:
            ref[0, 0] = val.astype(BF16)
        else:
            for c in range(2):
                stg_ref[c] = val[:, c * LANES:(c + 1) * LANES]
            for r in range(d):
                for c in range(2):
                    ref[0, r, :, c * LANES:(c + 1) * LANES] = (
                        stg_ref[c, pl.ds(r, tm // d, stride=d), :].astype(BF16))

    q_refs = (q0_ref, q1_ref, q2_ref)
    k_refs = (k0_ref, k1_ref, k2_ref)
    v_refs = (v0_ref, v1_ref, v2_ref)
    for g, d in enumerate(DILATIONS):
        store_group(q_refs[g], rope256(proj(g * 256, 256)) * Q_SCALE, d)
        store_group(k_refs[g], rope256(proj(768 + g * 256, 256)), d)
        store_group(v_refs[g], proj(1536 + g * 256, 256), d)
    for j in range(2):
        qb_ref[:, j * 256:(j + 1) * 256] = (rope256(proj(2304 + j * 256, 256)) * Q_SCALE).astype(BF16)
    kv = proj(2816, 256)
    kb = rope(kv[:, :LANES])
    vb = kv[:, LANES:]
    kb_sw = pltpu.roll(kb, 64, 1)
    vb_sw = pltpu.roll(vb, 64, 1)
    kb_ref[:, :LANES] = jnp.where(low, kb, kb_sw).astype(BF16)
    kb_ref[:, LANES:] = jnp.where(low, kb_sw, kb).astype(BF16)
    vb_ref[:, :LANES] = jnp.where(low, vb, vb_sw).astype(BF16)
    vb_ref[:, LANES:] = jnp.where(low, vb_sw, vb).astype(BF16)
    for j in range(4):
        sga_ref[:, j * 256:(j + 1) * 256] = jax.nn.sigmoid(proj(3072 + j * 256, 256)).astype(BF16)
        sgb_ref[:, j * 256:(j + 1) * 256] = jax.nn.sigmoid(proj(4096 + j * 256, 256)).astype(BF16)


def _inproj(x2, mod3, g_mix, cos_t, sin_t, w_in_bf, batch, seq):
    n, dm = x2.shape
    tm = 512
    tpb = seq // tm
    grid = (n // tm,)
    row = lambda i: (i, 0)
    strided_specs, strided_shapes = [], []
    for d in DILATIONS:
        for _ in range(3):
            strided_specs.append(pl.BlockSpec((1, d, tm // d, DIL_GROUP_W), lambda i: (i // tpb, 0, i % tpb, 0)))
            strided_shapes.append(jax.ShapeDtypeStruct((batch, d, seq // d, DIL_GROUP_W), BF16))
    out_specs = strided_specs + [
        pl.BlockSpec((tm, 512), row), pl.BlockSpec((tm, 256), row), pl.BlockSpec((tm, 256), row),
        pl.BlockSpec((tm, dm), row), pl.BlockSpec((tm, dm), row)]
    out_shapes = strided_shapes + [
        jax.ShapeDtypeStruct((n, 512), BF16), jax.ShapeDtypeStruct((n, 256), BF16),
        jax.ShapeDtypeStruct((n, 256), BF16), jax.ShapeDtypeStruct((n, dm), BF16),
        jax.ShapeDtypeStruct((n, dm), BF16)]
    return pl.pallas_call(
        functools.partial(_inproj_kernel, tm=tm),
        grid=grid,
        in_specs=[pl.BlockSpec((tm, dm), row),
                  pl.BlockSpec((1, 6, dm), lambda i: (i // tpb, 0, 0)),
                  pl.BlockSpec((1, dm), lambda i: (0, 0)),
                  pl.BlockSpec((tm, LANES), row),
                  pl.BlockSpec((tm, LANES), row),
                  pl.BlockSpec(w_in_bf.shape, lambda i: (0, 0))],
        out_specs=out_specs,
        out_shape=out_shapes,
        scratch_shapes=[pltpu.VMEM((2, tm, LANES), F32)],
        compiler_params=_cparams(("arbitrary",)),
        name="inproj",
    )(x2, mod3, g_mix, cos_t, sin_t, w_in_bf)


def _split_heads(q2, low):
    zero = jnp.zeros_like(q2)
    return jnp.concatenate([jnp.where(low, q2, zero), jnp.where(low, zero, q2)], axis=0)


def _band_softmax(qst, k2, v2, mask, sinks):
    s = lax.dot_general(qst, k2, (((1,), (1,)), ((), ())), preferred_element_type=F32)
    s = jnp.where(mask, s, NEG_INF)
    rows, tk = s.shape
    m = jnp.max(s, axis=-1, keepdims=True)
    if sinks is not None:
        seg = rows // len(sinks)
        m = jnp.concatenate([jnp.maximum(m[h * seg:(h + 1) * seg], sk) for h, sk in enumerate(sinks)], axis=0)
    m = jnp.broadcast_to(m, (rows, LANES))
    e = jnp.concatenate([jnp.exp(s[:, c * LANES:(c + 1) * LANES] - m) for c in range(tk // LANES)], axis=1)
    v_ones = jnp.concatenate([v2, jnp.ones((tk, LANES), BF16)], axis=1)
    od = jnp.dot(e.astype(BF16), v_ones, preferred_element_type=F32)
    o, den = od[:, :LANES], od[:, LANES:]
    if sinks is not None:
        den = jnp.concatenate([den[h * seg:(h + 1) * seg] + jnp.exp(sk - m[h * seg:(h + 1) * seg])
                               for h, sk in enumerate(sinks)], axis=0)
    return o / den, m, den


def _band_mask(qs, ks, nstack, tq, tk, window):
    row = lax.broadcasted_iota(I32, (nstack * tq, tk), 0) & (tq - 1)
    col = lax.broadcasted_iota(I32, (nstack * tq, tk), 1)
    return jnp.abs((ks + col) - (qs + row)) <= window


def _dil_kernel(q_ref, k_ref, v_ref, o_ref, l_ref, *, length, tq, tk):
    low = lax.broadcasted_iota(I32, (tq, LANES), 1) < 64

    nq = length // tq

    def body(j, carry):
        r = j // nq
        qs = pl.multiple_of((j % nq) * tq, tq)
        ks = pl.multiple_of(jnp.clip(qs - DIL_HALF_WINDOW, 0, length - tk), DIL_HALF_WINDOW)
        mask = _band_mask(qs, ks, 2, tq, tk, DIL_HALF_WINDOW)
        for c in range(DIL_GROUP_W // LANES):
            cs = slice(c * LANES, (c + 1) * LANES)
            qst = _split_heads(q_ref[r, pl.ds(qs, tq), cs], low)
            o, m, den = _band_softmax(qst, k_ref[r, pl.ds(ks, tk), cs], v_ref[r, pl.ds(ks, tk), cs], mask, None)
            lse = m + jnp.log(den)
            o_ref[r, pl.ds(qs, tq), cs] = jnp.where(low, o[:tq], o[tq:]).astype(BF16)
            l_ref[r, pl.ds(qs, tq), cs] = jnp.where(low, lse[:tq], lse[tq:])
        return carry

    lax.fori_loop(0, q_ref.shape[0] * nq, body, 0, unroll=4)


def _dil_attention(q, k, v):
    batch, d, length, w = q.shape
    tq, tk = 128, 256
    spec = pl.BlockSpec((None, d, length, w), lambda b: (b, 0, 0, 0))
    return pl.pallas_call(
        functools.partial(_dil_kernel, length=length, tq=tq, tk=tk),
        grid=(batch,),
        in_specs=[spec, spec, spec],
        out_specs=[spec, spec],
        out_shape=[jax.ShapeDtypeStruct(q.shape, BF16), jax.ShapeDtypeStruct(q.shape, F32)],
        compiler_params=_cparams(("arbitrary",)),
        name=f"dil{d}",
    )(q, k, v)


def _swa_kernel(sink_ref, q_ref, k_ref, v_ref, o_ref, *, length, tq, tk):
    low = lax.broadcasted_iota(I32, (tq, LANES), 1) < 64
    nblk = q_ref.shape[1] // LANES

    def body(j, carry):
        qs = pl.multiple_of(j * tq, tq)
        ks = pl.multiple_of(jnp.clip(qs - SWA_WINDOW, 0, length - tk), SWA_WINDOW)
        mask = _band_mask(qs, ks, 2, tq, tk, SWA_WINDOW)
        for b in range(nblk):
            cs = slice((b // 2) * LANES, (b // 2 + 1) * LANES)
            bs = slice(b * LANES, (b + 1) * LANES)
            qst = _split_heads(q_ref[pl.ds(qs, tq), bs], low)
            sinks = (sink_ref[2 * b], sink_ref[2 * b + 1])
            o, _, _ = _band_softmax(qst, k_ref[pl.ds(ks, tk), cs], v_ref[pl.ds(ks, tk), cs], mask, sinks)
            o_ref[pl.ds(qs, tq), bs] = jnp.where(low, o[:tq], o[tq:]).astype(BF16)
        return carry

    lax.fori_loop(0, length // tq, body, 0, unroll=4)


def _swa_attention(sink, q, k, v):
    batch, length, qw = q.shape
    tq, tk = 128, 384
    return pl.pallas_call(
        functools.partial(_swa_kernel, length=length, tq=tq, tk=tk),
        grid=(batch,),
        in_specs=[pl.BlockSpec(memory_space=pltpu.SMEM),
                  pl.BlockSpec((None, length, qw), lambda b: (b, 0, 0)),
                  pl.BlockSpec((None, length, k.shape[2]), lambda b: (b, 0, 0)),
                  pl.BlockSpec((None, length, v.shape[2]), lambda b: (b, 0, 0))],
        out_specs=pl.BlockSpec((None, length, qw), lambda b: (b, 0, 0)),
        out_shape=jax.ShapeDtypeStruct(q.shape, BF16),
        compiler_params=_cparams(("arbitrary",)),
        name="swa",
    )(sink, q, k, v)


def _route_rows(logits):
    lane = lax.broadcasted_iota(I32, logits.shape, 1).astype(F32)
    big = 1e9
    is_g = lane < N_GROUPS
    gl = jnp.where(is_g, logits, NEG_INF)
    gmax = jnp.max(gl, axis=-1, keepdims=True)
    gsel = jnp.min(jnp.where(is_g & (gl == gmax), lane, big), axis=-1, keepdims=True)
    gw = 1.0 / jnp.sum(jnp.where(is_g, jnp.exp(gl - gmax), 0.0), axis=-1, keepdims=True)
    e_lo = N_GROUPS + gsel * EXPERTS_PER_GROUP
    in_grp = (lane >= e_lo) & (lane < e_lo + EXPERTS_PER_GROUP)
    el = jnp.where(in_grp, logits, NEG_INF)
    m1 = jnp.max(el, axis=-1, keepdims=True)
    i1 = jnp.min(jnp.where(in_grp & (el == m1), lane, big), axis=-1, keepdims=True)
    el2 = jnp.where(lane == i1, NEG_INF, el)
    m2 = jnp.max(el2, axis=-1, keepdims=True)
    i2 = jnp.min(jnp.where(in_grp & (lane != i1) & (el2 == m2), lane, big), axis=-1, keepdims=True)
    t = jnp.exp(m2 - m1)
    tw1 = gw / (1.0 + t)
    tw2 = gw * t / (1.0 + t)
    out = jnp.where(lane == 0, tw1, 0.0)
    out = jnp.where(lane == 1, tw2, out)
    out = jnp.where(lane == 2, i1 - N_GROUPS, out)
    return jnp.where(lane == 3, i2 - N_GROUPS, out)


def _outproj_kernel(o0_ref, l0_ref, o1_ref, l1_ref, o2_ref, l2_ref, ob_ref, sga_ref, sgb_ref, x_ref,
                    mod_ref, g_ref, wa_ref, wb_ref, wo_ref, wr_ref, br_ref,
                    x1_ref, xs_ref, route_ref, len_ref,
                    so1_ref, sl1_ref, so2_ref, sl2_ref, h2_ref, *, tm, sub, group):
    ntiles = pl.num_programs(0) - 1

    @pl.when(pl.program_id(0) == ntiles)
    def _():
        xs_ref[...] = jnp.zeros_like(xs_ref)

    pl.when(pl.program_id(0) < ntiles)(functools.partial(
        _outproj_tile, o0_ref, l0_ref, o1_ref, l1_ref, o2_ref, l2_ref, ob_ref, sga_ref, sgb_ref, x_ref,
        mod_ref, g_ref, wa_ref, wb_ref, wo_ref, wr_ref, br_ref, x1_ref, xs_ref, route_ref, len_ref,
        so1_ref, sl1_ref, so2_ref, sl2_ref, h2_ref, tm=tm, sub=sub, group=group))


def _outproj_tile(o0_ref, l0_ref, o1_ref, l1_ref, o2_ref, l2_ref, ob_ref, sga_ref, sgb_ref, x_ref,
                  mod_ref, g_ref, wa_ref, wb_ref, wo_ref, wr_ref, br_ref,
                  x1_ref, xs_ref, route_ref, len_ref,
                  so1_ref, sl1_ref, so2_ref, sl2_ref, h2_ref, *, tm, sub, group):
    dm = x_ref.shape[1]
    for (o_ref, l_ref, so_ref, sl_ref, d) in ((o1_ref, l1_ref, so1_ref, sl1_ref, DILATIONS[1]),
                                              (o2_ref, l2_ref, so2_ref, sl2_ref, DILATIONS[2])):
        for r in range(d):
            for c in range(2):
                cs = slice(c * LANES, (c + 1) * LANES)
                so_ref[c, pl.ds(r, tm // d, stride=d), :] = o_ref[0, r, :, cs].astype(F32)
                sl_ref[c, pl.ds(r, tm // d, stride=d), :] = l_ref[0, r, :, cs]
    mr = lax.broadcasted_iota(I32, (LANES, LANES), 0)
    mc = lax.broadcasted_iota(I32, (LANES, LANES), 1)
    move_hi = jnp.where(((mr < 2) & (mc == 2 * mr)) | ((mr >= 2) & (mr < 4) & (mc == mr + 2)), 1.0, 0.0).astype(BF16)
    move_lo = jnp.where((mr < 2) & (mc == 2 * mr + 1), 1.0, 0.0).astype(BF16)
    for t in range(tm // sub):
        rs = slice(t * sub, (t + 1) * sub)
        both = lambda ref: jnp.concatenate([ref[0, rs, :], ref[1, rs, :]], axis=1)
        o0, l0 = o0_ref[0, 0, rs, :].astype(F32), l0_ref[0, 0, rs, :]
        o1, l1, o2, l2 = both(so1_ref), both(sl1_ref), both(so2_ref), both(sl2_ref)
        mx = jnp.maximum(jnp.maximum(l0, l1), l2)
        w0, w1, w2 = jnp.exp(l0 - mx), jnp.exp(l1 - mx), jnp.exp(l2 - mx)
        o_a = (w0 * o0 + w1 * o1 + w2 * o2) / (w0 + w1 + w2)
        y_a = jnp.dot(o_a.astype(BF16), wa_ref[...], preferred_element_type=F32)
        y_b = jnp.dot(ob_ref[rs, :], wb_ref[...], preferred_element_type=F32)
        merged = sga_ref[rs, :].astype(F32) * y_a + sgb_ref[rs, :].astype(F32) * y_b
        mix = jnp.dot(merged.astype(BF16), wo_ref[...], preferred_element_type=F32)
        x1 = x_ref[rs, :] + mod_ref[0, 2:3, :] * mix
        x1_ref[rs, :] = x1
        h2 = _rms_mod(x1, g_ref[...], mod_ref[0, 3:4, :], mod_ref[0, 4:5, :]).astype(BF16)
        h2_ref[rs, :dm] = h2
        logits = jnp.dot(h2, wr_ref[...], preferred_element_type=F32) + br_ref[...]
        rt = _route_rows(logits)
        route_ref[rs, :] = rt
        hi = rt.astype(BF16)
        lo = (rt - hi.astype(F32)).astype(BF16)
        aux = (jnp.dot(hi, move_hi, preferred_element_type=F32) + jnp.dot(lo, move_lo, preferred_element_type=F32))
        h2_ref[rs, dm:] = aux.astype(BF16)

    part = route_ref[...]
    e1, e2 = part[:, 2:3], part[:, 3:4]
    lane = lax.broadcasted_iota(I32, (tm, LANES), 1).astype(F32)
    onehot = jnp.where((lane == e1) | (lane == e2), 1.0, 0.0)
    rr = lax.broadcasted_iota(I32, (tm, tm), 0)
    cc = lax.broadcasted_iota(I32, (tm, tm), 1)
    tri = jnp.where(rr > cc, 1.0, 0.0).astype(BF16)
    prefix = jnp.dot(tri, onehot.astype(BF16), preferred_element_type=F32)
    cnt = jnp.sum(onehot, axis=0, keepdims=True)
    seg_len = jnp.ceil(cnt * (1.0 / SEG_ROWS)) * SEG_ROWS
    ur = lax.broadcasted_iota(I32, (LANES, LANES), 0)
    uc = lax.broadcasted_iota(I32, (LANES, LANES), 1)
    upper = jnp.where(ur < uc, 1.0, 0.0).astype(BF16)
    seg_off = jnp.dot(jnp.broadcast_to(seg_len, (8, LANES)).astype(BF16), upper, preferred_element_type=F32)[0:1, :]
    slot_map = seg_off + prefix
    s1 = jnp.sum(jnp.where(lane == e1, slot_map, 0.0), axis=-1, keepdims=True)
    s2 = jnp.sum(jnp.where(lane == e2, slot_map, 0.0), axis=-1, keepdims=True)
    len_ref[0] = jnp.broadcast_to(seg_len, (8, LANES))
    route = jnp.where(lane == 2, s1, jnp.where(lane == 3, s2, part))
    route_ref[...] = route

    pr = lax.broadcasted_iota(I32, (8, LANES), 0)
    pc = lax.broadcasted_iota(I32, (8, LANES), 1)
    lane_pick = jnp.where(pc == pr + 2, 1.0, 0.0)
    slots_t = lax.dot_general(lane_pick, route, (((1,), (1,)), ((), ())), preferred_element_type=F32,
                              precision=lax.Precision.HIGHEST)
    for lo in range(0, xs_ref.shape[0], group):
        srow = (lax.broadcasted_iota(I32, (group, tm), 0) + lo).astype(F32)
        pick = jnp.where((srow == slots_t[0:1, :]) | (srow == slots_t[1:2, :]), 1.0, 0.0).astype(BF16)
        xs_ref[lo:lo + group, :] = jnp.dot(pick, h2_ref[...], preferred_element_type=F32).astype(BF16)


def _outproj(dil_outs, ob, sga, sgb, x2, mod3, g_ffn, wa, wb, wo, wr, br, batch, seq, nslots):
    n, dm = x2.shape
    tm = MOE_TILE
    tpb = seq // tm
    ntiles = n // tm
    tile = lambda i: jnp.minimum(i, ntiles - 1)
    row = lambda i: (tile(i), 0)
    const = lambda i: (0, 0)
    in_specs = []
    args = []
    for (o, l), d in zip(dil_outs, DILATIONS):
        spec = pl.BlockSpec((1, d, tm // d, DIL_GROUP_W), lambda i: (tile(i) // tpb, 0, tile(i) % tpb, 0))
        in_specs += [spec, spec]
        args += [o, l]
    in_specs += [pl.BlockSpec((tm, ob.shape[1]), row), pl.BlockSpec((tm, dm), row), pl.BlockSpec((tm, dm), row),
                 pl.BlockSpec((tm, dm), row),
                 pl.BlockSpec((1, 6, dm), lambda i: (tile(i) // tpb, 0, 0)),
                 pl.BlockSpec((1, dm), const),
                 pl.BlockSpec(wa.shape, const), pl.BlockSpec(wb.shape, const), pl.BlockSpec(wo.shape, const),
                 pl.BlockSpec(wr.shape, const), pl.BlockSpec(br.shape, const)]
    args += [ob, sga, sgb, x2, mod3, g_ffn, wa, wb, wo, wr, br]
    width = dm + LANES
    return pl.pallas_call(
        functools.partial(_outproj_kernel, tm=tm, sub=256, group=SLOT_GROUP),
        grid=(ntiles + 1,),
        in_specs=in_specs,
        out_specs=[pl.BlockSpec((tm, dm), row), pl.BlockSpec((nslots, width), lambda i: (i, 0)),
                   pl.BlockSpec((tm, LANES), row), pl.BlockSpec((1, 8, LANES), lambda i: (tile(i), 0, 0))],
        out_shape=[jax.ShapeDtypeStruct((n, dm), F32), jax.ShapeDtypeStruct(((ntiles + 1) * nslots, width), BF16),
                   jax.ShapeDtypeStruct((n, LANES), F32), jax.ShapeDtypeStruct((ntiles, 8, LANES), F32)],
        scratch_shapes=[pltpu.VMEM((2, tm, LANES), F32)] * 4 + [pltpu.VMEM((tm, width), BF16)],
        compiler_params=_cparams(("arbitrary",)),
        name="outproj",
    )(*args)


def _wait_copies(count, copy):
    def wait_one(c, carry):
        copy.wait()
        return carry

    lax.fori_loop(0, count, wait_one, 0)


def _dispatch_kernel(dst_ref, live_ref, tstart_ref, tch_ref, nused_ref, h_ref, route_ref, xr_ref,
                     sorted_ref, zero_ref, sem, *, tm, nslots, nblk, group):
    i = pl.program_id(0)
    last = pl.num_programs(0) - 1
    slot = i & 1
    nchk = nslots // SEG_ROWS
    nspare = (2 * nslots) // ROW_BLOCK
    zsem = sem.at[2]

    def wait_tile(tile, s):
        for lo in range(0, nslots, group):
            copy = pltpu.make_async_copy(sorted_ref.at[s, pl.ds(lo, group), :], xr_ref.at[pl.ds(0, group), :], sem.at[s])
            if lo < 2 * tm:
                copy.wait()
            else:
                pl.when(live_ref[tile] > lo)(copy.wait)

    def zero_blk(b):
        return pltpu.make_async_copy(zero_ref, xr_ref.at[pl.ds(pl.multiple_of(b * ROW_BLOCK, ROW_BLOCK), ROW_BLOCK), :], zsem)

    def zero_seg(dst):
        return pltpu.make_async_copy(zero_ref.at[pl.ds(0, SEG_ROWS), :], xr_ref.at[pl.ds(dst, SEG_ROWS), :], zsem)

    def start_blocks(lo, hi):
        def blk(b, carry):
            zero_blk(b).start()
            return carry

        lax.fori_loop(lo, hi, blk, 0)

    ntail = lax.fori_loop(0, N_EXPERTS, lambda e, acc: acc + tch_ref[e], 0)

    @pl.when(i == 0)
    def _():
        zero_ref[...] = jnp.zeros_like(zero_ref)
        start_blocks(nblk, nblk + nspare)
        _wait_copies(nspare, zero_blk(0))
        def tail(e, carry):
            def chunk(c, carry2):
                zero_seg(pl.multiple_of(tstart_ref[e] + c * SEG_ROWS, SEG_ROWS)).start()
                return carry2

            lax.fori_loop(0, tch_ref[e], chunk, 0)
            return carry

        lax.fori_loop(0, N_EXPERTS, tail, 0)
        start_blocks(nused_ref[0], nblk)

    @pl.when(i >= 2)
    def _():
        wait_tile(i - 2, slot)

    pr = lax.broadcasted_iota(I32, (8, LANES), 0)
    pc = lax.broadcasted_iota(I32, (8, LANES), 1)
    lane_pick = jnp.where(pc == pr + 2, 1.0, 0.0)
    slots_t = lax.dot_general(lane_pick, route_ref[...], (((1,), (1,)), ((), ())), preferred_element_type=F32,
                              precision=lax.Precision.HIGHEST)

    def sort_group(lo):
        srow = (lax.broadcasted_iota(I32, (group, tm), 0) + lo).astype(F32)
        pick = jnp.where((srow == slots_t[0:1, :]) | (srow == slots_t[1:2, :]), 1.0, 0.0).astype(BF16)
        sorted_ref[slot, lo:lo + group, :] = jnp.dot(pick, h_ref[...], preferred_element_type=F32).astype(BF16)
        for c in range(lo // SEG_ROWS, (lo + group) // SEG_ROWS):
            dst = pl.multiple_of(dst_ref[i * nchk + c], SEG_ROWS)
            pltpu.make_async_copy(sorted_ref.at[slot, pl.ds(c * SEG_ROWS, SEG_ROWS), :],
                                  xr_ref.at[pl.ds(dst, SEG_ROWS), :], sem.at[slot]).start()

    for lo in range(0, nslots, group):
        if lo < 2 * tm:
            sort_group(lo)
        else:
            pl.when(live_ref[i] > lo)(functools.partial(sort_group, lo))

    @pl.when(i == last)
    def _():
        wait_tile(i, slot)

        @pl.when(i >= 1)
        def _():
            wait_tile(i - 1, 1 - slot)

        _wait_copies(ntail, zero_seg(0))
        _wait_copies(nblk - nused_ref[0], zero_blk(0))


def _dispatch(tables, h2, route, nblk, tm, nslots):
    n, width = h2.shape
    rows = nblk * ROW_BLOCK + 2 * nslots
    grid_spec = pltpu.PrefetchScalarGridSpec(
        num_scalar_prefetch=5,
        grid=(n // tm,),
        in_specs=[pl.BlockSpec((tm, width), lambda i, *_: (i, 0)),
                  pl.BlockSpec((tm, LANES), lambda i, *_: (i, 0))],
        out_specs=pl.BlockSpec(memory_space=pl.ANY),
        scratch_shapes=[pltpu.VMEM((2, nslots, width), BF16), pltpu.VMEM((ROW_BLOCK, width), BF16),
                        pltpu.SemaphoreType.DMA((3,))],
    )
    return pl.pallas_call(
        functools.partial(_dispatch_kernel, tm=tm, nslots=nslots, nblk=nblk, group=SLOT_GROUP),
        grid_spec=grid_spec,
        out_shape=jax.ShapeDtypeStruct((rows, width), BF16),
        compiler_params=_cparams(("arbitrary",)),
        name="dispatch",
    )(*tables, h2, route)


def _experts_kernel(first_ref, count_ref, wg_ref, wu_ref, wd_ref, x_hbm, y_hbm,
                    xbuf, ybuf, zbuf, wgb_ref, wub_ref, wdb_ref, xsem, ysem, zsem, *, nblk):
    e = pl.program_id(0)
    first = first_ref[e]
    count = count_ref[e]
    dm = ybuf.shape[2]

    def x_copy(g):
        rows = pl.ds(pl.multiple_of(g * ROW_BLOCK, ROW_BLOCK), ROW_BLOCK)
        return pltpu.make_async_copy(x_hbm.at[rows, :], xbuf.at[g & 1], xsem.at[g & 1])

    def y_copy(g):
        rows = pl.ds(pl.multiple_of(g * ROW_BLOCK, ROW_BLOCK), ROW_BLOCK)
        return pltpu.make_async_copy(ybuf.at[g & 1], y_hbm.at[rows, :], ysem.at[g & 1])

    used = first_ref[pl.num_programs(0) - 1] + count_ref[pl.num_programs(0) - 1]

    def zero_blk(b):
        rows = pl.ds(pl.multiple_of(b * ROW_BLOCK, ROW_BLOCK), ROW_BLOCK)
        return pltpu.make_async_copy(zbuf, y_hbm.at[rows, :], zsem)

    @pl.when(e == 0)
    def _():
        zbuf[...] = jnp.zeros_like(zbuf)

        def start(b, carry):
            zero_blk(b).start()
            return carry

        lax.fori_loop(used, nblk, start, 0)

    @pl.when((e == 0) & (used > 0))
    def _():
        x_copy(0).start(priority=1)

    wgb_ref[...] = wg_ref[0].astype(BF16)
    wub_ref[...] = wu_ref[0].astype(BF16)
    wdb_ref[...] = wd_ref[0].astype(BF16)

    def body(b, carry):
        blk = first + b
        slot = blk & 1
        x_copy(blk).wait()

        @pl.when(blk + 1 < used)
        def _():
            x_copy(blk + 1).start(priority=1)

        @pl.when(blk >= 2)
        def _():
            y_copy(blk - 2).wait()

        xb = xbuf[slot, :, :dm]
        aux = xbuf[slot, :, dm:].astype(F32)
        w = jnp.where(aux[:, 4:5] == e.astype(F32), aux[:, 0:1] + aux[:, 1:2], aux[:, 2:3] + aux[:, 3:4])
        g = jnp.dot(xb, wgb_ref[...], preferred_element_type=F32)
        u = jnp.dot(xb, wub_ref[...], preferred_element_type=F32)
        a = (g * jax.nn.sigmoid(g)) * u
        y = jnp.dot(a.astype(BF16), wdb_ref[...], preferred_element_type=F32)
        ybuf[slot] = (y * w).astype(BF16)
        y_copy(blk).start()
        return carry

    lax.fori_loop(0, count, body, 0)

    @pl.when(e == pl.num_programs(0) - 1)
    def _():
        @pl.when(used >= 2)
        def _():
            y_copy(used - 2).wait()

        @pl.when(used >= 1)
        def _():
            y_copy(used - 1).wait()

        _wait_copies(nblk - used, zero_blk(0))


def _experts(first_blk, count_blk, x_rows, w_gate, w_up, w_down):
    rows, width = x_rows.shape
    n_exp, dm, de = w_gate.shape
    nblk = rows // ROW_BLOCK
    wmap = lambda e, *_: (e, 0, 0)
    grid_spec = pltpu.PrefetchScalarGridSpec(
        num_scalar_prefetch=2,
        grid=(n_exp,),
        in_specs=[pl.BlockSpec((1, dm, de), wmap),
                  pl.BlockSpec((1, dm, de), wmap),
                  pl.BlockSpec((1, de, dm), wmap),
                  pl.BlockSpec(memory_space=pl.ANY)],
        out_specs=pl.BlockSpec(memory_space=pl.ANY),
        scratch_shapes=[pltpu.VMEM((2, ROW_BLOCK, width), BF16), pltpu.VMEM((2, ROW_BLOCK, dm), BF16),
                        pltpu.VMEM((ROW_BLOCK, dm), BF16),
                        pltpu.VMEM((dm, de), BF16), pltpu.VMEM((dm, de), BF16), pltpu.VMEM((de, dm), BF16),
                        pltpu.SemaphoreType.DMA((2,)), pltpu.SemaphoreType.DMA((2,)), pltpu.SemaphoreType.DMA(())],
    )
    return pl.pallas_call(
        functools.partial(_experts_kernel, nblk=nblk),
        grid_spec=grid_spec,
        out_shape=jax.ShapeDtypeStruct((rows, dm), BF16),
        compiler_params=_cparams(("arbitrary",)),
        name="experts",
    )(first_blk, count_blk, w_gate, w_up, w_down, x_rows)


def _slot_parts(tm, nslots, group):
    return [(0, 2 * tm)] + [(lo, group) for lo in range(2 * tm, nslots, group)]


def _combine_kernel(src_ref, live_ref, route_ref, x1_ref, mod_ref, g_ref, y_ref, o_ref, ys_ref, sem,
                    *, tm, nslots, group):
    i = pl.program_id(0)
    last = pl.num_programs(0) - 1
    slot = i & 1
    nchk = nslots // SEG_ROWS
    parts = _slot_parts(tm, nslots, group)

    def for_live_parts(tile, fn):
        for p, (lo, rows) in enumerate(parts):
            if p == 0:
                fn(lo, rows)
            else:
                pl.when(live_ref[tile] > lo)(functools.partial(fn, lo, rows))

    def gather(tile, s):
        def start(lo, rows):
            for c in range(lo // SEG_ROWS, (lo + rows) // SEG_ROWS):
                src = pl.multiple_of(src_ref[tile * nchk + c], SEG_ROWS)
                pltpu.make_async_copy(y_ref.at[pl.ds(src, SEG_ROWS), :],
                                      ys_ref.at[s, pl.ds(c * SEG_ROWS, SEG_ROWS), :], sem.at[s]).start()

        for_live_parts(tile, start)

    def wait(tile, s):
        for_live_parts(tile, lambda lo, rows: pltpu.make_async_copy(
            y_ref.at[pl.ds(0, rows), :], ys_ref.at[s, pl.ds(lo, rows), :], sem.at[s]).wait())

    @pl.when(i == 0)
    def _():
        gather(0, 0)

    nxt = jnp.minimum(i + 1, last)
    gather(nxt, 1 - slot)
    wait(i, slot)
    route = route_ref[...]

    def finish(rows):
        scol = lax.broadcasted_iota(I32, (tm, rows), 1).astype(F32)
        pick = jnp.where((scol == route[:, 2:3]) | (scol == route[:, 3:4]), 1.0, 0.0).astype(BF16)
        moe = jnp.dot(pick, ys_ref[slot, :rows, :], preferred_element_type=F32)
        x = x1_ref[...] + mod_ref[0, 5:6, :] * moe
        ms = jnp.mean(x * x, axis=-1, keepdims=True)
        o_ref[...] = (x * lax.rsqrt(ms + RMS_EPS)) * g_ref[...]

    ends = [lo + rows for lo, rows in parts]
    for p, end in enumerate(ends):
        above = live_ref[i] > (ends[p - 1] if p else -1)
        cond = above if p == len(ends) - 1 else above & (live_ref[i] <= end)
        pl.when(cond)(functools.partial(finish, end))

    @pl.when(i == last)
    def _():
        wait(nxt, 1 - slot)


def _combine(src_tbl, live_tbl, route, x1, mod3, g_final, y_rows, seq, tm, nslots):
    n, dm = x1.shape
    tpb = seq // tm
    grid_spec = pltpu.PrefetchScalarGridSpec(
        num_scalar_prefetch=2,
        grid=(n // tm,),
        in_specs=[pl.BlockSpec((tm, LANES), lambda i, *_: (i, 0)),
                  pl.BlockSpec((tm, dm), lambda i, *_: (i, 0)),
                  pl.BlockSpec((1, 6, dm), lambda i, *_: (i // tpb, 0, 0)),
                  pl.BlockSpec((1, dm), lambda i, *_: (0, 0)),
                  pl.BlockSpec(memory_space=pl.ANY)],
        out_specs=pl.BlockSpec((tm, dm), lambda i, *_: (i, 0)),
        scratch_shapes=[pltpu.VMEM((2, nslots, dm), BF16), pltpu.SemaphoreType.DMA((2,))],
    )
    return pl.pallas_call(
        functools.partial(_combine_kernel, tm=tm, nslots=nslots, group=SLOT_GROUP),
        grid_spec=grid_spec,
        out_shape=jax.ShapeDtypeStruct((n, dm), F32),
        compiler_params=_cparams(("arbitrary",)),
        name="combine",
    )(src_tbl, live_tbl, route, x1, mod3, g_final, y_rows)


def _rope_tables(positions):
    half = HEAD_DIM // 2
    inv_freq = ROPE_THETA ** (-jnp.arange(half, dtype=F32) * (2.0 / HEAD_DIM))
    freq = jnp.tile(inv_freq, LANES // half)
    sign = jnp.tile(jnp.concatenate([-jnp.ones((half,), F32), jnp.ones((half,), F32)]), LANES // HEAD_DIM)
    ang = positions.astype(F32).reshape(-1, 1) * freq
    return jnp.cos(ang), jnp.sin(ang) * sign


def kernel(x, c, positions, w_ada, b_ada, g_mix, w_in, sink_logits, w_branch_a, w_branch_b, w_out, g_ffn,
           w_group, b_group, w_route, b_route, w_expert_gate, w_expert_up, w_expert_down, g_final):
    batch, seq, dm = x.shape
    n = batch * seq
    assert w_ada.shape[0] == 1, "one layer"
    x2 = x.reshape(n, dm)

    c8 = jnp.pad(c, ((0, 8 - batch), (0, 0)))
    mod = _ada(c8, w_ada[0], b_ada[0].reshape(1, -1))
    mod3 = mod[:batch].reshape(batch, 6, dm)

    cos_t, sin_t = _rope_tables(positions)
    outs = _inproj(x2, mod3, g_mix[0].reshape(1, dm), cos_t, sin_t, w_in[0].astype(BF16), batch, seq)
    qkv = outs[:9]
    qb, kb, vb, sga, sgb = outs[9:]

    dil_outs = [_dil_attention(qkv[3 * g], qkv[3 * g + 1], qkv[3 * g + 2]) for g in range(len(DILATIONS))]
    ob = _swa_attention(sink_logits[0], qb.reshape(batch, seq, -1), kb.reshape(batch, seq, -1),
                        vb.reshape(batch, seq, -1)).reshape(n, -1)

    pad = LANES - N_GROUPS - N_EXPERTS
    wr = jnp.concatenate([w_group[0], w_route[0], jnp.zeros((dm, pad), F32)], axis=1).astype(BF16)
    br = jnp.concatenate([b_group[0], b_route[0], jnp.zeros((pad,), F32)]).reshape(1, LANES)
    x1, h2, route, seg_lens = _outproj(dil_outs, ob, sga, sgb, x2, mod3, g_ffn[0].reshape(1, dm),
                                       w_branch_a[0].astype(BF16), w_branch_b[0].astype(BF16), w_out[0].astype(BF16),
                                       wr, br, batch, seq)

    ntiles = n // MOE_TILE
    nslots = 2 * MOE_TILE + N_EXPERTS * SEG_ROWS
    nblk = -(-(2 * n + ntiles * N_EXPERTS * (SEG_ROWS - 1)) // ROW_BLOCK) + N_EXPERTS
    lens = seg_lens[:, 0, :N_EXPERTS].astype(I32)
    tot = jnp.sum(lens, axis=0)
    padded = (tot + ROW_BLOCK - 1) // ROW_BLOCK * ROW_BLOCK
    pend = jnp.cumsum(padded)
    pstart = pend - padded
    base = pstart[None, :] + jnp.cumsum(lens, axis=0) - lens
    nused = pend[-1:] // ROW_BLOCK
    nchk = nslots // SEG_ROWS
    run_end = jnp.cumsum(lens // SEG_ROWS, axis=1)
    chunk = jnp.arange(nchk, dtype=I32)
    owner = jnp.sum((run_end[:, None, :] <= chunk[None, :, None]).astype(I32), axis=2)
    live = owner < N_EXPERTS
    is_owner = owner[:, :, None] == jnp.arange(N_EXPERTS, dtype=I32)[None, None, :]
    run_row0 = base - (run_end - lens // SEG_ROWS) * SEG_ROWS
    row = jnp.sum(jnp.where(is_owner, run_row0[:, None, :], 0), axis=2) + chunk[None, :] * SEG_ROWS
    spare = nblk * ROW_BLOCK + (jnp.arange(ntiles, dtype=I32)[:, None] % 2) * nslots + chunk[None, :] * SEG_ROWS
    dst_tbl = jnp.where(live, row, spare).reshape(-1)
    src_tbl = jnp.where(live, row, chunk[None, :] * SEG_ROWS).reshape(-1)
    live_tbl = jnp.sum(lens, axis=1)
    tables = (dst_tbl, live_tbl, pstart + tot, (padded - tot) // SEG_ROWS, nused)

    x_rows = _dispatch(tables, h2, route, nblk, MOE_TILE, nslots)
    y_rows = _experts(pstart // ROW_BLOCK, padded // ROW_BLOCK, x_rows,
                      w_expert_gate[0], w_expert_up[0], w_expert_down[0])
    out = _combine(src_tbl, live_tbl, route, x1, mod3, g_final.reshape(1, dm), y_rows, seq, MOE_TILE, nslots)
    return out.reshape(batch, seq, dm)


def _wait_copies(count, copy):
    def wait_one(c, carry):
        copy.wait()
        return carry

    lax.fori_loop(0, count, wait_one, 0)


def _experts_kernel(first_ref, count_ref, chunk_ref, live_ref, wg_ref, wu_ref, wd_ref, xs_hbm, ys_hbm,
                    xbuf, ybuf, zbuf, wgb_ref, wub_ref, wdb_ref, xsem, ysem, zsem, *, nslots, ntiles):
    e = pl.program_id(0)
    last = pl.num_programs(0) - 1
    first = first_ref[e]
    count = count_ref[e]
    used = first_ref[last] + count_ref[last]
    dm = ybuf.shape[2]
    cpb = ROW_BLOCK // SEG_ROWS
    cpt = nslots // SEG_ROWS

    def chunk_rows(blk, j):
        return pl.ds(pl.multiple_of(chunk_ref[blk * cpb + j], SEG_ROWS), SEG_ROWS)

    def gather(blk):
        for j in range(cpb):
            pltpu.make_async_copy(xs_hbm.at[chunk_rows(blk, j), :], xbuf.at[blk & 1, pl.ds(j * SEG_ROWS, SEG_ROWS), :],
                                  xsem.at[blk & 1]).start(priority=1)

    def scatter(blk):
        for j in range(cpb):
            pltpu.make_async_copy(ybuf.at[blk & 1, pl.ds(j * SEG_ROWS, SEG_ROWS), :], ys_hbm.at[chunk_rows(blk, j), :],
                                  ysem.at[blk & 1]).start()

    def gather_done(blk):
        return pltpu.make_async_copy(xs_hbm.at[pl.ds(0, ROW_BLOCK), :], xbuf.at[blk & 1], xsem.at[blk & 1])

    def scatter_done(blk):
        return pltpu.make_async_copy(ybuf.at[blk & 1], ys_hbm.at[pl.ds(0, ROW_BLOCK), :], ysem.at[blk & 1])

    def zero_chunk(row):
        return pltpu.make_async_copy(zbuf.at[pl.ds(0, SEG_ROWS), :], ys_hbm.at[pl.ds(row, SEG_ROWS), :], zsem)

    def dead_chunks(t):
        return cpt - live_ref[t] // SEG_ROWS

    @pl.when(e == 0)
    def _():
        zbuf[...] = jnp.zeros_like(zbuf)
        for b in range(nslots // ROW_BLOCK):
            pltpu.make_async_copy(zbuf, ys_hbm.at[pl.ds(ntiles * nslots + b * ROW_BLOCK, ROW_BLOCK), :], zsem).start()
        for b in range(nslots // ROW_BLOCK):
            pltpu.make_async_copy(zbuf, ys_hbm.at[pl.ds(0, ROW_BLOCK), :], zsem).wait()

        def tile(t, carry):
            def chunk(c, carry2):
                zero_chunk(pl.multiple_of(t * nslots + c * SEG_ROWS, SEG_ROWS)).start()
                return carry2

            lax.fori_loop(live_ref[t] // SEG_ROWS, cpt, chunk, 0)
            return carry

        lax.fori_loop(0, ntiles, tile, 0)

        @pl.when(used > 0)
        def _():
            gather(0)

    wgb_ref[...] = wg_ref[0].astype(BF16)
    wub_ref[...] = wu_ref[0].astype(BF16)
    wdb_ref[...] = wd_ref[0].astype(BF16)

    def body(b, carry):
        blk = first + b
        slot = blk & 1
        gather_done(blk).wait()

        @pl.when(blk + 1 < used)
        def _():
            gather(blk + 1)

        @pl.when(blk >= 2)
        def _():
            scatter_done(blk - 2).wait()

        xb = xbuf[slot, :, :dm]
        aux = xbuf[slot, :, dm:].astype(F32)
        w = jnp.where(aux[:, 4:5] == e.astype(F32), aux[:, 0:1] + aux[:, 1:2], aux[:, 2:3] + aux[:, 3:4])
        g = jnp.dot(xb, wgb_ref[...], preferred_element_type=F32)
        u = jnp.dot(xb, wub_ref[...], preferred_element_type=F32)
        a = (g * jax.nn.sigmoid(g)) * u
        y = jnp.dot(a.astype(BF16), wdb_ref[...], preferred_element_type=F32)
        ybuf[slot] = (y * w).astype(BF16)
        scatter(blk)
        return carry

    lax.fori_loop(0, count, body, 0)

    @pl.when(e == last)
    def _():
        @pl.when(used >= 2)
        def _():
            scatter_done(used - 2).wait()

        @pl.when(used >= 1)
        def _():
            scatter_done(used - 1).wait()

        _wait_copies(lax.fori_loop(0, ntiles, lambda t, acc: acc + dead_chunks(t), 0), zero_chunk(0))


def _experts(first_blk, count_blk, chunk_tbl, live_tbl, x_sorted, w_gate, w_up, w_down, nslots, ntiles):
    rows, width = x_sorted.shape
    n_exp, dm, de = w_gate.shape
    wmap = lambda e, *_: (e, 0, 0)
    grid_spec = pltpu.PrefetchScalarGridSpec(
        num_scalar_prefetch=4,
        grid=(n_exp,),
        in_specs=[pl.BlockSpec((1, dm, de), wmap),
                  pl.BlockSpec((1, dm, de), wmap),
                  pl.BlockSpec((1, de, dm), wmap),
                  pl.BlockSpec(memory_space=pl.ANY)],
        out_specs=pl.BlockSpec(memory_space=pl.ANY),
        scratch_shapes=[pltpu.VMEM((2, ROW_BLOCK, width), BF16), pltpu.VMEM((2, ROW_BLOCK, dm), BF16),
                        pltpu.VMEM((ROW_BLOCK, dm), BF16),
                        pltpu.VMEM((dm, de), BF16), pltpu.VMEM((dm, de), BF16), pltpu.VMEM((de, dm), BF16),
                        pltpu.SemaphoreType.DMA((2,)), pltpu.SemaphoreType.DMA((2,)), pltpu.SemaphoreType.DMA(())],
    )
    return pl.pallas_call(
        functools.partial(_experts_kernel, nslots=nslots, ntiles=ntiles),
        grid_spec=grid_spec,
        out_shape=jax.ShapeDtypeStruct((rows, dm), BF16),
        compiler_params=_cparams(("arbitrary",)),
        name="experts",
    )(first_blk, count_blk, chunk_tbl, live_tbl, w_gate, w_up, w_down, x_sorted)


def _combine_kernel(live_ref, route_ref, x1_ref, mod_ref, g_ref, ys_ref, o_ref, *, tm, nslots, group):
    i = pl.program_id(0)
    route = route_ref[...]

    def finish(rows):
        scol = lax.broadcasted_iota(I32, (tm, rows), 1).astype(F32)
        pick = jnp.where((scol == route[:, 2:3]) | (scol == route[:, 3:4]), 1.0, 0.0).astype(BF16)
        moe = jnp.dot(pick, ys_ref[:rows, :], preferred_element_type=F32)
        x = x1_ref[...] + mod_ref[0, 5:6, :] * moe
        ms = jnp.mean(x * x, axis=-1, keepdims=True)
        o_ref[...] = (x * lax.rsqrt(ms + RMS_EPS)) * g_ref[...]

    ends = [2 * tm] + list(range(2 * tm + group, nslots + 1, group))
    for p, end in enumerate(ends):
        above = live_ref[i] > (ends[p - 1] if p else -1)
        cond = above if p == len(ends) - 1 else above & (live_ref[i] <= end)
        pl.when(cond)(functools.partial(finish, end))


def _combine(live_tbl, route, x1, mod3, g_final, y_sorted, seq, tm, nslots):
    n, dm = x1.shape
    tpb = seq // tm
    grid_spec = pltpu.PrefetchScalarGridSpec(
        num_scalar_prefetch=1,
        grid=(n // tm,),
        in_specs=[pl.BlockSpec((tm, LANES), lambda i, *_: (i, 0)),
                  pl.BlockSpec((tm, dm), lambda i, *_: (i, 0)),
                  pl.BlockSpec((1, 6, dm), lambda i, *_: (i // tpb, 0, 0)),
                  pl.BlockSpec((1, dm), lambda i, *_: (0, 0)),
                  pl.BlockSpec((nslots, dm), lambda i, *_: (i, 0))],
        out_specs=pl.BlockSpec((tm, dm), lambda i, *_: (i, 0)),
    )
    return pl.pallas_call(
        functools.partial(_combine_kernel, tm=tm, nslots=nslots, group=SLOT_GROUP),
        grid_spec=grid_spec,
        out_shape=jax.ShapeDtypeStruct((n, dm), F32),
        compiler_params=_cparams(("arbitrary",)),
        name="combine",
    )(live_tbl, route, x1, mod3, g_final, y_sorted)


def _expert_chunk_table(lens, pstart, tot, pend, nblk, nslots):
    ntiles = lens.shape[0]
    cpb = ROW_BLOCK // SEG_ROWS
    runs = lens // SEG_ROWS
    q = jnp.arange(nblk * cpb, dtype=I32)
    owner = jnp.minimum(jnp.sum((pend[None, :] // SEG_ROWS <= q[:, None]).astype(I32), axis=1), N_EXPERTS - 1)
    is_e = owner[:, None] == jnp.arange(N_EXPERTS, dtype=I32)[None, :]
    pick_e = lambda v: jnp.sum(jnp.where(is_e, v[None, :], 0), axis=1)
    off = q - pick_e(pstart // SEG_ROWS)
    in_run = (off < pick_e(tot // SEG_ROWS)) & (q < pend[-1] // SEG_ROWS)
    upto = jnp.cumsum(runs, axis=0)
    upto_e = jnp.sum(jnp.where(is_e[:, None, :], upto[None, :, :], 0), axis=2)
    tile = jnp.minimum(jnp.sum((upto_e <= off[:, None]).astype(I32), axis=1), ntiles - 1)
    is_t = tile[:, None] == jnp.arange(ntiles, dtype=I32)[None, :]
    run_slot0 = jnp.cumsum(runs, axis=1) - runs - (upto - runs)
    slot0 = jnp.sum(jnp.where(is_t[:, :, None] & is_e[:, None, :], run_slot0[None, :, :], 0), axis=(1, 2))
    live_row = tile * nslots + (slot0 + off) * SEG_ROWS
    spare_row = ntiles * nslots + (((q // cpb) % 2) * cpb + q % cpb) * SEG_ROWS
    return jnp.where(in_run, live_row, spare_row)


def kernel(x, c, positions, w_ada, b_ada, g_mix, w_in, sink_logits, w_branch_a, w_branch_b, w_out, g_ffn,
           w_group, b_group, w_route, b_route, w_expert_gate, w_expert_up, w_expert_down, g_final):
    batch, seq, dm = x.shape
    n = batch * seq
    assert w_ada.shape[0] == 1, "one layer"
    x2 = x.reshape(n, dm)

    c8 = jnp.pad(c, ((0, 8 - batch), (0, 0)))
    mod = _ada(c8, w_ada[0], b_ada[0].reshape(1, -1))
    mod3 = mod[:batch].reshape(batch, 6, dm)

    cos_t, sin_t = _rope_tables(positions)
    outs = _inproj(x2, mod3, g_mix[0].reshape(1, dm), cos_t, sin_t, w_in[0].astype(BF16), batch, seq)
    qkv = outs[:9]
    qb, kb, vb, sga, sgb = outs[9:]

    dil_outs = [_dil_attention(qkv[3 * g], qkv[3 * g + 1], qkv[3 * g + 2]) for g in range(len(DILATIONS))]
    ob = _swa_attention(sink_logits[0], qb.reshape(batch, seq, -1), kb.reshape(batch, seq, -1),
                        vb.reshape(batch, seq, -1)).reshape(n, -1)

    pad = LANES - N_GROUPS - N_EXPERTS
    wr = jnp.concatenate([w_group[0], w_route[0], jnp.zeros((dm, pad), F32)], axis=1).astype(BF16)
    br = jnp.concatenate([b_group[0], b_route[0], jnp.zeros((pad,), F32)]).reshape(1, LANES)
    ntiles = n // MOE_TILE
    nslots = 2 * MOE_TILE + N_EXPERTS * SEG_ROWS
    x1, x_sorted, route, seg_lens = _outproj(dil_outs, ob, sga, sgb, x2, mod3, g_ffn[0].reshape(1, dm),
                                             w_branch_a[0].astype(BF16), w_branch_b[0].astype(BF16),
                                             w_out[0].astype(BF16), wr, br, batch, seq, nslots)

    nblk = -(-(2 * n + ntiles * N_EXPERTS * (SEG_ROWS - 1)) // ROW_BLOCK) + N_EXPERTS
    lens = seg_lens[:, 0, :N_EXPERTS].astype(I32)
    tot = jnp.sum(lens, axis=0)
    padded = (tot + ROW_BLOCK - 1) // ROW_BLOCK * ROW_BLOCK
    pend = jnp.cumsum(padded)
    pstart = pend - padded
    chunk_tbl = _expert_chunk_table(lens, pstart, tot, pend, nblk, nslots)
    live_tbl = jnp.sum(lens, axis=1)

    y_sorted = _experts(pstart // ROW_BLOCK, padded // ROW_BLOCK, chunk_tbl, live_tbl, x_sorted,
                        w_expert_gate[0], w_expert_up[0], w_expert_down[0], nslots, ntiles)
    out = _combine(live_tbl, route, x1, mod3, g_final.reshape(1, dm), y_sorted, seq, MOE_TILE, nslots)
    return out.reshape(batch, seq, dm)
```

```python
import functools

import jax
import jax.numpy as jnp
from jax import lax
from jax.experimental import pallas as pl
from jax.experimental.pallas import tpu as pltpu

F32 = jnp.float32
BF16 = jnp.bfloat16
I32 = jnp.int32

HEAD_DIM = 64
ROPE_THETA = 10000.0
RMS_EPS = 1e-6
NEG_INF = -1e30
Q_SCALE = HEAD_DIM ** -0.5
DILATIONS = (1, 4, 16)
DIL_HALF_WINDOW = 64
DIL_GROUP_W = 256
SWA_WINDOW = 128
N_GROUPS = 4
EXPERTS_PER_GROUP = 8
N_EXPERTS = 32
LANES = 128
ROW_BLOCK = 512
SEG_ROWS = 16
MOE_TILE = 512
SLOT_GROUP = 256
VMEM_LIMIT = 56 * 1024 * 1024


def _cparams(sem):
    return pltpu.CompilerParams(dimension_semantics=sem, vmem_limit_bytes=VMEM_LIMIT)


def _ada_kernel(c_ref, w_ref, b_ref, o_ref):
    c = c_ref[...]
    cs = c * jax.nn.sigmoid(c)
    o_ref[...] = jnp.dot(cs.astype(BF16), w_ref[...].astype(BF16), preferred_element_type=F32) + b_ref[...]


def _ada(c8, w_ada, b_ada):
    d, n = w_ada.shape
    tn = 1536
    return pl.pallas_call(
        _ada_kernel,
        grid=(n // tn,),
        in_specs=[pl.BlockSpec((8, d), lambda j: (0, 0)),
                  pl.BlockSpec((d, tn), lambda j: (0, j)),
                  pl.BlockSpec((1, tn), lambda j: (0, j))],
        out_specs=pl.BlockSpec((8, tn), lambda j: (0, j)),
        out_shape=jax.ShapeDtypeStruct((8, n), F32),
        compiler_params=_cparams(("arbitrary",)),
        name="ada",
    )(c8, w_ada, b_ada)


def _rms_mod(x, g, shift, scale):
    ms = jnp.mean(x * x, axis=-1, keepdims=True)
    return (x * lax.rsqrt(ms + RMS_EPS)) * (g * (1.0 + scale)) + shift


def _inproj_kernel(x_ref, mod_ref, g_ref, cos_ref, sin_ref, w_ref,
                   q0_ref, k0_ref, v0_ref, q1_ref, k1_ref, v1_ref, q2_ref, k2_ref, v2_ref,
                   qb_ref, kb_ref, vb_ref, sga_ref, sgb_ref, stg_ref, *, tm):
    h = _rms_mod(x_ref[...], g_ref[...], mod_ref[0, 0:1, :], mod_ref[0, 1:2, :])
    hb = h.astype(BF16)
    cos = cos_ref[...]
    sin = sin_ref[...]
    lane = lax.broadcasted_iota(I32, (tm, LANES), 1)
    first_half = (lane & 32) == 0
    low = lane < 64

    def proj(c0, width):
        return jnp.dot(hb, w_ref[:, c0:c0 + width], preferred_element_type=F32)

    def rope(t):
        rot = jnp.where(first_half, pltpu.roll(t, 96, 1), pltpu.roll(t, 32, 1))
        return t * cos + rot * sin

    def rope256(p):
        return jnp.concatenate([rope(p[:, :LANES]), rope(p[:, LANES:])], axis=1)

    def store_group(ref, val, d):
        if d == 1:
            ref[0, 0] = val.astype(BF16)
        else:
            for c in range(2):
                stg_ref[c] = val[:, c * LANES:(c + 1) * LANES]
            for r in range(d):
                for c in range(2):
                    ref[0, r, :, c * LANES:(c + 1) * LANES] = (
                        stg_ref[c, pl.ds(r, tm // d, stride=d), :].astype(BF16))

    q_refs = (q0_ref, q1_ref, q2_ref)
    k_refs = (k0_ref, k1_ref, k2_ref)
    v_refs = (v0_ref, v1_ref, v2_ref)
    for g, d in enumerate(DILATIONS):
        store_group(q_refs[g], rope256(proj(g * 256, 256)) * Q_SCALE, d)
        store_group(k_refs[g], rope256(proj(768 + g * 256, 256)), d)
        store_group(v_refs[g], proj(1536 + g * 256, 256), d)
    for j in range(2):
        qb_ref[:, j * 256:(j + 1) * 256] = (rope256(proj(2304 + j * 256, 256)) * Q_SCALE).astype(BF16)
    kv = proj(2816, 256)
    kb = rope(kv[:, :LANES])
    vb = kv[:, LANES:]
    kb_sw = pltpu.roll(kb, 64, 1)
    vb_sw = pltpu.roll(vb, 64, 1)
    kb_ref[:, :LANES] = jnp.where(low, kb, kb_sw).astype(BF16)
    kb_ref[:, LANES:] = jnp.where(low, kb_sw, kb).astype(BF16)
    vb_ref[:, :LANES] = jnp.where(low, vb, vb_sw).astype(BF16)
    vb_ref[:, LANES:] = jnp.where(low, vb_sw, vb).astype(BF16)
    for j in range(4):
        sga_ref[:, j * 256:(j + 1) * 256] = jax.nn.sigmoid(proj(3072 + j * 256, 256)).astype(BF16)
        sgb_ref[:, j * 256:(j + 1) * 256] = jax.nn.sigmoid(proj(4096 + j * 256, 256)).astype(BF16)


def _inproj(x2, mod3, g_mix, cos_t, sin_t, w_in_bf, batch, seq):
    n, dm = x2.shape
    tm = 512
    tpb = seq // tm
    grid = (n // tm,)
    row = lambda i: (i, 0)
    strided_specs, strided_shapes = [], []
    for d in DILATIONS:
        for _ in range(3):
            strided_specs.append(pl.BlockSpec((1, d, tm // d, DIL_GROUP_W), lambda i: (i // tpb, 0, i % tpb, 0)))
            strided_shapes.append(jax.ShapeDtypeStruct((batch, d, seq // d, DIL_GROUP_W), BF16))
    out_specs = strided_specs + [
        pl.BlockSpec((tm, 512), row), pl.BlockSpec((tm, 256), row), pl.BlockSpec((tm, 256), row),
        pl.BlockSpec((tm, dm), row), pl.BlockSpec((tm, dm), row)]
    out_shapes = strided_shapes + [
        jax.ShapeDtypeStruct((n, 512), BF16), jax.ShapeDtypeStruct((n, 256), BF16),
        jax.ShapeDtypeStruct((n, 256), BF16), jax.ShapeDtypeStruct((n, dm), BF16),
        jax.ShapeDtypeStruct((n, dm), BF16)]
    return pl.pallas_call(
        functools.partial(_inproj_kernel, tm=tm),
        grid=grid,
        in_specs=[pl.BlockSpec((tm, dm), row),
                  pl.BlockSpec((1, 6, dm), lambda i: (i // tpb, 0, 0)),
                  pl.BlockSpec((1, dm), lambda i: (0, 0)),
                  pl.BlockSpec((tm, LANES), row),
                  pl.BlockSpec((tm, LANES), row),
                  pl.BlockSpec(w_in_bf.shape, lambda i: (0, 0))],
        out_specs=out_specs,
        out_shape=out_shapes,
        scratch_shapes=[pltpu.VMEM((2, tm, LANES), F32)],
        compiler_params=_cparams(("arbitrary",)),
        name="inproj",
    )(x2, mod3, g_mix, cos_t, sin_t, w_in_bf)


def _split_heads(q2, low):
    zero = jnp.zeros_like(q2)
    return jnp.concatenate([jnp.where(low, q2, zero), jnp.where(low, zero, q2)], axis=0)


def _band_softmax(qst, k2, v2, mask, sinks):
    s = lax.dot_general(qst, k2, (((1,), (1,)), ((), ())), preferred_element_type=F32)
    s = jnp.where(mask, s, NEG_INF)
    rows, tk = s.shape
    m = jnp.max(s, axis=-1, keepdims=True)
    if sinks is not None:
        seg = rows // len(sinks)
        m = jnp.concatenate([jnp.maximum(m[h * seg:(h + 1) * seg], sk) for h, sk in enumerate(sinks)], axis=0)
    m = jnp.broadcast_to(m, (rows, LANES))
    e = jnp.concatenate([jnp.exp(s[:, c * LANES:(c + 1) * LANES] - m) for c in range(tk // LANES)], axis=1)
    v_ones = jnp.concatenate([v2, jnp.ones((tk, LANES), BF16)], axis=1)
    od = jnp.dot(e.astype(BF16), v_ones, preferred_element_type=F32)
    o, den = od[:, :LANES], od[:, LANES:]
    if sinks is not None:
        den = jnp.concatenate([den[h * seg:(h + 1) * seg] + jnp.exp(sk - m[h * seg:(h + 1) * seg])
                               for h, sk in enumerate(sinks)], axis=0)
    return o / den, m, den


def _band_mask(qs, ks, nstack, tq, tk, window):
    row = lax.broadcasted_iota(I32, (nstack * tq, tk), 0) & (tq - 1)
    col = lax.broadcasted_iota(I32, (nstack * tq, tk), 1)
    return jnp.abs((ks + col) - (qs + row)) <= window


def _dil_kernel(q_ref, k_ref, v_ref, o_ref, l_ref, *, length, tq, tk):
    low = lax.broadcasted_iota(I32, (tq, LANES), 1) < 64

    nq = length // tq

    def body(j, carry):
        r = j // nq
        qs = pl.multiple_of((j % nq) * tq, tq)
        ks = pl.multiple_of(jnp.clip(qs - DIL_HALF_WINDOW, 0, length - tk), DIL_HALF_WINDOW)
        mask = _band_mask(qs, ks, 2, tq, tk, DIL_HALF_WINDOW)
        for c in range(DIL_GROUP_W // LANES):
            cs = slice(c * LANES, (c + 1) * LANES)
            qst = _split_heads(q_ref[r, pl.ds(qs, tq), cs], low)
            o, m, den = _band_softmax(qst, k_ref[r, pl.ds(ks, tk), cs], v_ref[r, pl.ds(ks, tk), cs], mask, None)
            lse = m + jnp.log(den)
            o_ref[r, pl.ds(qs, tq), cs] = jnp.where(low, o[:tq], o[tq:]).astype(BF16)
            l_ref[r, pl.ds(qs, tq), cs] = jnp.where(low, lse[:tq], lse[tq:])
        return carry

    lax.fori_loop(0, q_ref.shape[0] * nq, body, 0, unroll=4)


def _dil_attention(q, k, v):
    batch, d, length, w = q.shape
    tq, tk = 128, 256
    spec = pl.BlockSpec((None, d, length, w), lambda b: (b, 0, 0, 0))
    return pl.pallas_call(
        functools.partial(_dil_kernel, length=length, tq=tq, tk=tk),
        grid=(batch,),
        in_specs=[spec, spec, spec],
        out_specs=[spec, spec],
        out_shape=[jax.ShapeDtypeStruct(q.shape, BF16), jax.ShapeDtypeStruct(q.shape, F32)],
        compiler_params=_cparams(("arbitrary",)),
        name=f"dil{d}",
    )(q, k, v)


def _swa_kernel(sink_ref, q_ref, k_ref, v_ref, o_ref, *, length, tq, tk):
    low = lax.broadcasted_iota(I32, (tq, LANES), 1) < 64
    nblk = q_ref.shape[1] // LANES

    def body(j, carry):
        qs = pl.multiple_of(j * tq, tq)
        ks = pl.multiple_of(jnp.clip(qs - SWA_WINDOW, 0, length - tk), SWA_WINDOW)
        mask = _band_mask(qs, ks, 2, tq, tk, SWA_WINDOW)
        for b in range(nblk):
            cs = slice((b // 2) * LANES, (b // 2 + 1) * LANES)
            bs = slice(b * LANES, (b + 1) * LANES)
            qst = _split_heads(q_ref[pl.ds(qs, tq), bs], low)
            sinks = (sink_ref[2 * b], sink_ref[2 * b + 1])
            o, _, _ = _band_softmax(qst, k_ref[pl.ds(ks, tk), cs], v_ref[pl.ds(ks, tk), cs], mask, sinks)
            o_ref[pl.ds(qs, tq), bs] = jnp.where(low, o[:tq], o[tq:]).astype(BF16)
        return carry

    lax.fori_loop(0, length // tq, body, 0, unroll=4)


def _swa_attention(sink, q, k, v):
    batch, length, qw = q.shape
    tq, tk = 128, 384
    return pl.pallas_call(
        functools.partial(_swa_kernel, length=length, tq=tq, tk=tk),
        grid=(batch,),
        in_specs=[pl.BlockSpec(memory_space=pltpu.SMEM),
                  pl.BlockSpec((None, length, qw), lambda b: (b, 0, 0)),
                  pl.BlockSpec((None, length, k.shape[2]), lambda b: (b, 0, 0)),
                  pl.BlockSpec((None, length, v.shape[2]), lambda b: (b, 0, 0))],
        out_specs=pl.BlockSpec((None, length, qw), lambda b: (b, 0, 0)),
        out_shape=jax.ShapeDtypeStruct(q.shape, BF16),
        compiler_params=_cparams(("arbitrary",)),
        name="swa",
    )(sink, q, k, v)


def _route_rows(logits):
    lane = lax.broadcasted_iota(I32, logits.shape, 1).astype(F32)
    big = 1e9
    is_g = lane < N_GROUPS
    gl = jnp.where(is_g, logits, NEG_INF)
    gmax = jnp.max(gl, axis=-1, keepdims=True)
    gsel = jnp.min(jnp.where(is_g & (gl == gmax), lane, big), axis=-1, keepdims=True)
    gw = 1.0 / jnp.sum(jnp.where(is_g, jnp.exp(gl - gmax), 0.0), axis=-1, keepdims=True)
    e_lo = N_GROUPS + gsel * EXPERTS_PER_GROUP
    in_grp = (lane >= e_lo) & (lane < e_lo + EXPERTS_PER_GROUP)
    el = jnp.where(in_grp, logits, NEG_INF)
    m1 = jnp.max(el, axis=-1, keepdims=True)
    i1 = jnp.min(jnp.where(in_grp & (el == m1), lane, big), axis=-1, keepdims=True)
    el2 = jnp.where(lane == i1, NEG_INF, el)
    m2 = jnp.max(el2, axis=-1, keepdims=True)
    i2 = jnp.min(jnp.where(in_grp & (lane != i1) & (el2 == m2), lane, big), axis=-1, keepdims=True)
    t = jnp.exp(m2 - m1)
    tw1 = gw / (1.0 + t)
    tw2 = gw * t / (1.0 + t)
    out = jnp.where(lane == 0, tw1, 0.0)
    out = jnp.where(lane == 1, tw2, out)
    out = jnp.where(lane == 2, i1 - N_GROUPS, out)
    return jnp.where(lane == 3, i2 - N_GROUPS, out)


def _outproj_kernel(o0_ref, l0_ref, o1_ref, l1_ref, o2_ref, l2_ref, ob_ref, sga_ref, sgb_ref, x_ref,
                    mod_ref, g_ref, wa_ref, wb_ref, wo_ref, wr_ref, br_ref,
                    x1_ref, xs_ref, route_ref, len_ref,
                    so1_ref, sl1_ref, so2_ref, sl2_ref, h2_ref, *, tm, sub, group):
    ntiles = pl.num_programs(0) - 1

    @pl.when(pl.program_id(0) == ntiles)
    def _():
        xs_ref[...] = jnp.zeros_like(xs_ref)

    pl.when(pl.program_id(0) < ntiles)(functools.partial(
        _outproj_tile, o0_ref, l0_ref, o1_ref, l1_ref, o2_ref, l2_ref, ob_ref, sga_ref, sgb_ref, x_ref,
        mod_ref, g_ref, wa_ref, wb_ref, wo_ref, wr_ref, br_ref, x1_ref, xs_ref, route_ref, len_ref,
        so1_ref, sl1_ref, so2_ref, sl2_ref, h2_ref, tm=tm, sub=sub, group=group))


def _outproj_tile(o0_ref, l0_ref, o1_ref, l1_ref, o2_ref, l2_ref, ob_ref, sga_ref, sgb_ref, x_ref,
                  mod_ref, g_ref, wa_ref, wb_ref, wo_ref, wr_ref, br_ref,
                  x1_ref, xs_ref, route_ref, len_ref,
                  so1_ref, sl1_ref, so2_ref, sl2_ref, h2_ref, *, tm, sub, group):
    dm = x_ref.shape[1]
    for (o_ref, l_ref, so_ref, sl_ref, d) in ((o1_ref, l1_ref, so1_ref, sl1_ref, DILATIONS[1]),
                                              (o2_ref, l2_ref, so2_ref, sl2_ref, DILATIONS[2])):
        for r in range(d):
            for c in range(2):
                cs = slice(c * LANES, (c + 1) * LANES)
                so_ref[c, pl.ds(r, tm // d, stride=d), :] = o_ref[0, r, :, cs].astype(F32)
                sl_ref[c, pl.ds(r, tm // d, stride=d), :] = l_ref[0, r, :, cs]
    mr = lax.broadcasted_iota(I32, (LANES, LANES), 0)
    mc = lax.broadcasted_iota(I32, (LANES, LANES), 1)
    move_hi = jnp.where(((mr < 2) & (mc == 2 * mr)) | ((mr >= 2) & (mr < 4) & (mc == mr + 2)), 1.0, 0.0).astype(BF16)
    move_lo = jnp.where((mr < 2) & (mc == 2 * mr + 1), 1.0, 0.0).astype(BF16)
    for t in range(tm // sub):
        rs = slice(t * sub, (t + 1) * sub)
        both = lambda ref: jnp.concatenate([ref[0, rs, :], ref[1, rs, :]], axis=1)
        o0, l0 = o0_ref[0, 0, rs, :].astype(F32), l0_ref[0, 0, rs, :]
        o1, l1, o2, l2 = both(so1_ref), both(sl1_ref), both(so2_ref), both(sl2_ref)
        mx = jnp.maximum(jnp.maximum(l0, l1), l2)
        w0, w1, w2 = jnp.exp(l0 - mx), jnp.exp(l1 - mx), jnp.exp(l2 - mx)
        o_a = (w0 * o0 + w1 * o1 + w2 * o2) / (w0 + w1 + w2)
        y_a = jnp.dot(o_a.astype(BF16), wa_ref[...], preferred_element_type=F32)
        y_b = jnp.dot(ob_ref[rs, :], wb_ref[...], preferred_element_type=F32)
        merged = sga_ref[rs, :].astype(F32) * y_a + sgb_ref[rs, :].astype(F32) * y_b
        mix = jnp.dot(merged.astype(BF16), wo_ref[...], preferred_element_type=F32)
        x1 = x_ref[rs, :] + mod_ref[0, 2:3, :] * mix
        x1_ref[rs, :] = x1
        h2 = _rms_mod(x1, g_ref[...], mod_ref[0, 3:4, :], mod_ref[0, 4:5, :]).astype(BF16)
        h2_ref[rs, :dm] = h2
        logits = jnp.dot(h2, wr_ref[...], preferred_element_type=F32) + br_ref[...]
        rt = _route_rows(logits)
        route_ref[rs, :] = rt
        hi = rt.astype(BF16)
        lo = (rt - hi.astype(F32)).astype(BF16)
        aux = (jnp.dot(hi, move_hi, preferred_element_type=F32) + jnp.dot(lo, move_lo, preferred_element_type=F32))
        h2_ref[rs, dm:] = aux.astype(BF16)

    part = route_ref[...]
    e1, e2 = part[:, 2:3], part[:, 3:4]
    lane = lax.broadcasted_iota(I32, (tm, LANES), 1).astype(F32)
    onehot = jnp.where((lane == e1) | (lane == e2), 1.0, 0.0)
    rr = lax.broadcasted_iota(I32, (tm, tm), 0)
    cc = lax.broadcasted_iota(I32, (tm, tm), 1)
    tri = jnp.where(rr > cc, 1.0, 0.0).astype(BF16)
    prefix = jnp.dot(tri, onehot.astype(BF16), preferred_element_type=F32)
    cnt = jnp.sum(onehot, axis=0, keepdims=True)
    seg_len = jnp.ceil(cnt * (1.0 / SEG_ROWS)) * SEG_ROWS
    ur = lax.broadcasted_iota(I32, (LANES, LANES), 0)
    uc = lax.broadcasted_iota(I32, (LANES, LANES), 1)
    upper = jnp.where(ur < uc, 1.0, 0.0).astype(BF16)
    seg_off = jnp.dot(jnp.broadcast_to(seg_len, (8, LANES)).astype(BF16), upper, preferred_element_type=F32)[0:1, :]
    slot_map = seg_off + prefix
    s1 = jnp.sum(jnp.where(lane == e1, slot_map, 0.0), axis=-1, keepdims=True)
    s2 = jnp.sum(jnp.where(lane == e2, slot_map, 0.0), axis=-1, keepdims=True)
    len_ref[0] = jnp.broadcast_to(seg_len, (8, LANES))
    route = jnp.where(lane == 2, s1, jnp.where(lane == 3, s2, part))
    route_ref[...] = route

    pr = lax.broadcasted_iota(I32, (8, LANES), 0)
    pc = lax.broadcasted_iota(I32, (8, LANES), 1)
    lane_pick = jnp.where(pc == pr + 2, 1.0, 0.0)
    slots_t = lax.dot_general(lane_pick, route, (((1,), (1,)), ((), ())), preferred_element_type=F32,
                              precision=lax.Precision.HIGHEST)
    live = jnp.sum(seg_len).astype(I32)

    def sort_group(lo):
        srow = (lax.broadcasted_iota(I32, (group, tm), 0) + lo).astype(F32)
        pick = jnp.where((srow == slots_t[0:1, :]) | (srow == slots_t[1:2, :]), 1.0, 0.0).astype(BF16)
        xs_ref[lo:lo + group, :] = jnp.dot(pick, h2_ref[...], preferred_element_type=F32).astype(BF16)

    def clear_group(lo):
        xs_ref[lo:lo + group, :] = jnp.zeros((group, xs_ref.shape[1]), BF16)

    for lo in range(0, xs_ref.shape[0], group):
        if lo < 2 * tm:
            sort_group(lo)
        else:
            pl.when(live > lo)(functools.partial(sort_group, lo))
            pl.when(live <= lo)(functools.partial(clear_group, lo))


def _outproj(dil_outs, ob, sga, sgb, x2, mod3, g_ffn, wa, wb, wo, wr, br, batch, seq, nslots):
    n, dm = x2.shape
    tm = MOE_TILE
    tpb = seq // tm
    ntiles = n // tm
    tile = lambda i: jnp.minimum(i, ntiles - 1)
    row = lambda i: (tile(i), 0)
    const = lambda i: (0, 0)
    in_specs = []
    args = []
    for (o, l), d in zip(dil_outs, DILATIONS):
        spec = pl.BlockSpec((1, d, tm // d, DIL_GROUP_W), lambda i: (tile(i) // tpb, 0, tile(i) % tpb, 0))
        in_specs += [spec, spec]
        args += [o, l]
    in_specs += [pl.BlockSpec((tm, ob.shape[1]), row), pl.BlockSpec((tm, dm), row), pl.BlockSpec((tm, dm), row),
                 pl.BlockSpec((tm, dm), row),
                 pl.BlockSpec((1, 6, dm), lambda i: (tile(i) // tpb, 0, 0)),
                 pl.BlockSpec((1, dm), const),
                 pl.BlockSpec(wa.shape, const), pl.BlockSpec(wb.shape, const), pl.BlockSpec(wo.shape, const),
                 pl.BlockSpec(wr.shape, const), pl.BlockSpec(br.shape, const)]
    args += [ob, sga, sgb, x2, mod3, g_ffn, wa, wb, wo, wr, br]
    width = dm + LANES
    return pl.pallas_call(
        functools.partial(_outproj_kernel, tm=tm, sub=256, group=SLOT_GROUP),
        grid=(ntiles + 1,),
        in_specs=in_specs,
        out_specs=[pl.BlockSpec((tm, dm), row), pl.BlockSpec((nslots, width), lambda i: (i, 0)),
                   pl.BlockSpec((tm, LANES), row), pl.BlockSpec((1, 8, LANES), lambda i: (tile(i), 0, 0))],
        out_shape=[jax.ShapeDtypeStruct((n, dm), F32), jax.ShapeDtypeStruct(((ntiles + 1) * nslots, width), BF16),
                   jax.ShapeDtypeStruct((n, LANES), F32), jax.ShapeDtypeStruct((ntiles, 8, LANES), F32)],
        scratch_shapes=[pltpu.VMEM((2, tm, LANES), F32)] * 4 + [pltpu.VMEM((tm, width), BF16)],
        compiler_params=_cparams(("arbitrary",)),
        name="outproj",
    )(*args)


def _wait_copies(count, copy):
    def wait_one(c, carry):
        copy.wait()
        return carry

    lax.fori_loop(0, count, wait_one, 0)


def _experts_kernel(first_ref, count_ref, chunk_ref, live_ref, wg_ref, wu_ref, wd_ref, xs_hbm, ys_hbm,
                    xbuf, ybuf, zbuf, wgb_ref, wub_ref, wdb_ref, xsem, ysem, zsem, *, nslots, ntiles):
    e = pl.program_id(0)
    last = pl.num_programs(0) - 1
    first = first_ref[e]
    count = count_ref[e]
    used = first_ref[last] + count_ref[last]
    dm = ybuf.shape[2]
    cpb = ROW_BLOCK // SEG_ROWS
    cpt = nslots // SEG_ROWS

    def chunk_rows(blk, j):
        return pl.ds(pl.multiple_of(chunk_ref[blk * cpb + j], SEG_ROWS), SEG_ROWS)

    def gather(blk):
        for j in range(cpb):
            pltpu.make_async_copy(xs_hbm.at[chunk_rows(blk, j), :], xbuf.at[blk & 1, pl.ds(j * SEG_ROWS, SEG_ROWS), :],
                                  xsem.at[blk & 1]).start(priority=1)

    def scatter(blk):
        for j in range(cpb):
            pltpu.make_async_copy(ybuf.at[blk & 1, pl.ds(j * SEG_ROWS, SEG_ROWS), :], ys_hbm.at[chunk_rows(blk, j), :],
                                  ysem.at[blk & 1]).start()

    def gather_done(blk):
        return pltpu.make_async_copy(xs_hbm.at[pl.ds(0, ROW_BLOCK), :], xbuf.at[blk & 1], xsem.at[blk & 1])

    def scatter_done(blk):
        return pltpu.make_async_copy(ybuf.at[blk & 1], ys_hbm.at[pl.ds(0, ROW_BLOCK), :], ysem.at[blk & 1])

    def zero_chunk(row):
        return pltpu.make_async_copy(zbuf.at[pl.ds(0, SEG_ROWS), :], ys_hbm.at[pl.ds(row, SEG_ROWS), :], zsem)

    def dead_chunks(t):
        return cpt - live_ref[t] // SEG_ROWS

    @pl.when(e == 0)
    def _():
        zbuf[...] = jnp.zeros_like(zbuf)
        for b in range(nslots // ROW_BLOCK):
            pltpu.make_async_copy(zbuf, ys_hbm.at[pl.ds(ntiles * nslots + b * ROW_BLOCK, ROW_BLOCK), :], zsem).start()
        for b in range(nslots // ROW_BLOCK):
            pltpu.make_async_copy(zbuf, ys_hbm.at[pl.ds(0, ROW_BLOCK), :], zsem).wait()

        def tile(t, carry):
            def chunk(c, carry2):
                zero_chunk(pl.multiple_of(t * nslots + c * SEG_ROWS, SEG_ROWS)).start()
                return carry2

            lax.fori_loop(live_ref[t] // SEG_ROWS, cpt, chunk, 0)
            return carry

        lax.fori_loop(0, ntiles, tile, 0)

        @pl.when(used > 0)
        def _():
            gather(0)

    wgb_ref[...] = wg_ref[0].astype(BF16)
    wub_ref[...] = wu_ref[0].astype(BF16)
    wdb_ref[...] = wd_ref[0].astype(BF16)

    def body(b, carry):
        blk = first + b
        slot = blk & 1
        gather_done(blk).wait()

        @pl.when(blk + 1 < used)
        def _():
            gather(blk + 1)

        @pl.when(blk >= 2)
        def _():
            scatter_done(blk - 2).wait()

        xb = xbuf[slot, :, :dm]
        aux = xbuf[slot, :, dm:].astype(F32)
        w = jnp.where(aux[:, 4:5] == e.astype(F32), aux[:, 0:1] + aux[:, 1:2], aux[:, 2:3] + aux[:, 3:4])
        g = jnp.dot(xb, wgb_ref[...], preferred_element_type=F32)
        u = jnp.dot(xb, wub_ref[...], preferred_element_type=F32)
        a = (g * jax.nn.sigmoid(g)) * u
        y = jnp.dot(a.astype(BF16), wdb_ref[...], preferred_element_type=F32)
        ybuf[slot] = (y * w).astype(BF16)
        scatter(blk)
        return carry

    lax.fori_loop(0, count, body, 0)

    @pl.when(e == last)
    def _():
        @pl.when(used >= 2)
        def _():
            scatter_done(used - 2).wait()

        @pl.when(used >= 1)
        def _():
            scatter_done(used - 1).wait()

        _wait_copies(lax.fori_loop(0, ntiles, lambda t, acc: acc + dead_chunks(t), 0), zero_chunk(0))


def _experts(first_blk, count_blk, chunk_tbl, live_tbl, x_sorted, w_gate, w_up, w_down, nslots, ntiles):
    rows, width = x_sorted.shape
    n_exp, dm, de = w_gate.shape
    wmap = lambda e, *_: (e, 0, 0)
    grid_spec = pltpu.PrefetchScalarGridSpec(
        num_scalar_prefetch=4,
        grid=(n_exp,),
        in_specs=[pl.BlockSpec((1, dm, de), wmap),
                  pl.BlockSpec((1, dm, de), wmap),
                  pl.BlockSpec((1, de, dm), wmap),
                  pl.BlockSpec(memory_space=pl.ANY)],
        out_specs=pl.BlockSpec(memory_space=pl.ANY),
        scratch_shapes=[pltpu.VMEM((2, ROW_BLOCK, width), BF16), pltpu.VMEM((2, ROW_BLOCK, dm), BF16),
                        pltpu.VMEM((ROW_BLOCK, dm), BF16),
                        pltpu.VMEM((dm, de), BF16), pltpu.VMEM((dm, de), BF16), pltpu.VMEM((de, dm), BF16),
                        pltpu.SemaphoreType.DMA((2,)), pltpu.SemaphoreType.DMA((2,)), pltpu.SemaphoreType.DMA(())],
    )
    return pl.pallas_call(
        functools.partial(_experts_kernel, nslots=nslots, ntiles=ntiles),
        grid_spec=grid_spec,
        out_shape=jax.ShapeDtypeStruct((rows, dm), BF16),
        compiler_params=_cparams(("arbitrary",)),
        name="experts",
    )(first_blk, count_blk, chunk_tbl, live_tbl, w_gate, w_up, w_down, x_sorted)


def _combine_kernel(live_ref, route_ref, x1_ref, mod_ref, g_ref, ys_ref, yt_ref, o_ref, *, tm, head):
    i = pl.program_id(0)
    route = route_ref[...]

    def one_hot(lo, rows):
        scol = (lax.broadcasted_iota(I32, (tm, rows), 1) + lo).astype(F32)
        return jnp.where((scol == route[:, 2:3]) | (scol == route[:, 3:4]), 1.0, 0.0).astype(BF16)

    def finish(with_tail):
        moe = jnp.dot(one_hot(0, head), ys_ref[...], preferred_element_type=F32)
        if with_tail:
            moe = moe + jnp.dot(one_hot(head, yt_ref.shape[0]), yt_ref[...], preferred_element_type=F32)
        x = x1_ref[...] + mod_ref[0, 5:6, :] * moe
        ms = jnp.mean(x * x, axis=-1, keepdims=True)
        o_ref[...] = (x * lax.rsqrt(ms + RMS_EPS)) * g_ref[...]

    pl.when(live_ref[i] > head)(functools.partial(finish, True))
    pl.when(live_ref[i] <= head)(functools.partial(finish, False))


def _combine(live_tbl, route, x1, mod3, g_final, y_sorted, seq, tm, nslots):
    n, dm = x1.shape
    tpb = seq // tm
    ntiles = n // tm
    head = nslots - SLOT_GROUP
    grid_spec = pltpu.PrefetchScalarGridSpec(
        num_scalar_prefetch=1,
        grid=(n // tm,),
        in_specs=[pl.BlockSpec((tm, LANES), lambda i, *_: (i, 0)),
                  pl.BlockSpec((tm, dm), lambda i, *_: (i, 0)),
                  pl.BlockSpec((1, 6, dm), lambda i, *_: (i // tpb, 0, 0)),
                  pl.BlockSpec((1, dm), lambda i, *_: (0, 0)),
                  pl.BlockSpec((None, head, dm), lambda i, live: (i, 0, 0)),
                  pl.BlockSpec((None, SLOT_GROUP, dm),
                               lambda i, live: (jnp.where(live[i] > head, i, ntiles), head // SLOT_GROUP, 0))],
        out_specs=pl.BlockSpec((tm, dm), lambda i, *_: (i, 0)),
    )
    y3 = y_sorted.reshape(ntiles + 1, nslots, dm)
    return pl.pallas_call(
        functools.partial(_combine_kernel, tm=tm, head=head),
        grid_spec=grid_spec,
        out_shape=jax.ShapeDtypeStruct((n, dm), F32),
        compiler_params=_cparams(("arbitrary",)),
        name="combine",
    )(live_tbl, route, x1, mod3, g_final, y3, y3)


def _rope_tables(positions):
    half = HEAD_DIM // 2
    inv_freq = ROPE_THETA ** (-jnp.arange(half, dtype=F32) * (2.0 / HEAD_DIM))
    freq = jnp.tile(inv_freq, LANES // half)
    sign = jnp.tile(jnp.concatenate([-jnp.ones((half,), F32), jnp.ones((half,), F32)]), LANES // HEAD_DIM)
    ang = positions.astype(F32).reshape(-1, 1) * freq
    return jnp.cos(ang), jnp.sin(ang) * sign


def _expert_chunk_table(lens, pstart, tot, pend, nblk, nslots):
    ntiles = lens.shape[0]
    cpb = ROW_BLOCK // SEG_ROWS
    runs = lens // SEG_ROWS
    q = jnp.arange(nblk * cpb, dtype=I32)
    owner = jnp.minimum(jnp.sum((pend[None, :] // SEG_ROWS <= q[:, None]).astype(I32), axis=1), N_EXPERTS - 1)
    is_e = owner[:, None] == jnp.arange(N_EXPERTS, dtype=I32)[None, :]
    pick_e = lambda v: jnp.sum(jnp.where(is_e, v[None, :], 0), axis=1)
    off = q - pick_e(pstart // SEG_ROWS)
    in_run = (off < pick_e(tot // SEG_ROWS)) & (q < pend[-1] // SEG_ROWS)
    upto = jnp.cumsum(runs, axis=0)
    upto_e = jnp.sum(jnp.where(is_e[:, None, :], upto[None, :, :], 0), axis=2)
    tile = jnp.minimum(jnp.sum((upto_e <= off[:, None]).astype(I32), axis=1), ntiles - 1)
    is_t = tile[:, None] == jnp.arange(ntiles, dtype=I32)[None, :]
    run_slot0 = jnp.cumsum(runs, axis=1) - runs - (upto - runs)
    slot0 = jnp.sum(jnp.where(is_t[:, :, None] & is_e[:, None, :], run_slot0[None, :, :], 0), axis=(1, 2))
    live_row = tile * nslots + (slot0 + off) * SEG_ROWS
    spare_row = ntiles * nslots + (((q // cpb) % 2) * cpb + q % cpb) * SEG_ROWS
    return jnp.where(in_run, live_row, spare_row)


def kernel(x, c, positions, w_ada, b_ada, g_mix, w_in, sink_logits, w_branch_a, w_branch_b, w_out, g_ffn,
           w_group, b_group, w_route, b_route, w_expert_gate, w_expert_up, w_expert_down, g_final):
    batch, seq, dm = x.shape
    n = batch * seq
    assert w_ada.shape[0] == 1, "one layer"
    x2 = x.reshape(n, dm)

    c8 = jnp.pad(c, ((0, 8 - batch), (0, 0)))
    mod = _ada(c8, w_ada[0], b_ada[0].reshape(1, -1))
    mod3 = mod[:batch].reshape(batch, 6, dm)

    cos_t, sin_t = _rope_tables(positions)
    outs = _inproj(x2, mod3, g_mix[0].reshape(1, dm), cos_t, sin_t, w_in[0].astype(BF16), batch, seq)
    qkv = outs[:9]
    qb, kb, vb, sga, sgb = outs[9:]

    dil_outs = [_dil_attention(qkv[3 * g], qkv[3 * g + 1], qkv[3 * g + 2]) for g in range(len(DILATIONS))]
    ob = _swa_attention(sink_logits[0], qb.reshape(batch, seq, -1), kb.reshape(batch, seq, -1),
                        vb.reshape(batch, seq, -1)).reshape(n, -1)

    pad = LANES - N_GROUPS - N_EXPERTS
    wr = jnp.concatenate([w_group[0], w_route[0], jnp.zeros((dm, pad), F32)], axis=1).astype(BF16)
    br = jnp.concatenate([b_group[0], b_route[0], jnp.zeros((pad,), F32)]).reshape(1, LANES)
    ntiles = n // MOE_TILE
    nslots = 2 * MOE_TILE + N_EXPERTS * SEG_ROWS
    x1, x_sorted, route, seg_lens = _outproj(dil_outs, ob, sga, sgb, x2, mod3, g_ffn[0].reshape(1, dm),
                                             w_branch_a[0].astype(BF16), w_branch_b[0].astype(BF16),
                                             w_out[0].astype(BF16), wr, br, batch, seq, nslots)

    nblk = -(-(2 * n + ntiles * N_EXPERTS * (SEG_ROWS - 1)) // ROW_BLOCK) + N_EXPERTS
    lens = seg_lens[:, 0, :N_EXPERTS].astype(I32)
    tot = jnp.sum(lens, axis=0)
    padded = (tot + ROW_BLOCK - 1) // ROW_BLOCK * ROW_BLOCK
    pend = jnp.cumsum(padded)
    pstart = pend - padded
    chunk_tbl = _expert_chunk_table(lens, pstart, tot, pend, nblk, nslots)
    live_tbl = jnp.sum(lens, axis=1)

    y_sorted = _experts(pstart // ROW_BLOCK, padded // ROW_BLOCK, chunk_tbl, live_tbl, x_sorted,
                        w_expert_gate[0], w_expert_up[0], w_expert_down[0], nslots, ntiles)
    out = _combine(live_tbl, route, x1, mod3, g_final.reshape(1, dm), y_sorted, seq, MOE_TILE, nslots)
    return out.reshape(batch, seq, dm)
```

```python
import functools

import jax
import jax.numpy as jnp
from jax import lax
from jax.experimental import pallas as pl
from jax.experimental.pallas import tpu as pltpu

F32 = jnp.float32
BF16 = jnp.bfloat16
I32 = jnp.int32

HEAD_DIM = 64
ROPE_THETA = 10000.0
RMS_EPS = 1e-6
NEG_INF = -1e30
Q_SCALE = HEAD_DIM ** -0.5
DILATIONS = (1, 4, 16)
DIL_HALF_WINDOW = 64
DIL_GROUP_W = 256
SWA_WINDOW = 128
N_GROUPS = 4
EXPERTS_PER_GROUP = 8
N_EXPERTS = 32
LANES = 128
ROW_BLOCK = 512
SEG_ROWS = 16
MOE_TILE = 512
SLOT_GROUP = 256
VMEM_LIMIT = 56 * 1024 * 1024


def _cparams(sem):
    return pltpu.CompilerParams(dimension_semantics=sem, vmem_limit_bytes=VMEM_LIMIT)


def _ada_kernel(c_ref, w_ref, b_ref, o_ref):
    c = c_ref[...]
    cs = c * jax.nn.sigmoid(c)
    o_ref[...] = jnp.dot(cs.astype(BF16), w_ref[...].astype(BF16), preferred_element_type=F32) + b_ref[...]


def _ada(c8, w_ada, b_ada):
    d, n = w_ada.shape
    tn = 1536
    return pl.pallas_call(
        _ada_kernel,
        grid=(n // tn,),
        in_specs=[pl.BlockSpec((8, d), lambda j: (0, 0)),
                  pl.BlockSpec((d, tn), lambda j: (0, j)),
                  pl.BlockSpec((1, tn), lambda j: (0, j))],
        out_specs=pl.BlockSpec((8, tn), lambda j: (0, j)),
        out_shape=jax.ShapeDtypeStruct((8, n), F32),
        compiler_params=_cparams(("arbitrary",)),
        name="ada",
    )(c8, w_ada, b_ada)


def _rms_mod(x, g, shift, scale):
    ms = jnp.mean(x * x, axis=-1, keepdims=True)
    return (x * lax.rsqrt(ms + RMS_EPS)) * (g * (1.0 + scale)) + shift


def _inproj_kernel(x_ref, mod_ref, g_ref, cos_ref, sin_ref, w_ref,
                   q0_ref, k0_ref, v0_ref, q1_ref, k1_ref, v1_ref, q2_ref, k2_ref, v2_ref,
                   qb_ref, kb_ref, vb_ref, sga_ref, sgb_ref, stg_ref, *, tm):
    h = _rms_mod(x_ref[...], g_ref[...], mod_ref[0, 0:1, :], mod_ref[0, 1:2, :])
    hb = h.astype(BF16)
    cos = cos_ref[...]
    sin = sin_ref[...]
    lane = lax.broadcasted_iota(I32, (tm, LANES), 1)
    first_half = (lane & 32) == 0
    low = lane < 64

    def proj(c0, width):
        return jnp.dot(hb, w_ref[:, c0:c0 + width], preferred_element_type=F32)

    def rope(t):
        rot = jnp.where(first_half, pltpu.roll(t, 96, 1), pltpu.roll(t, 32, 1))
        return t * cos + rot * sin

    def rope256(p):
        return jnp.concatenate([rope(p[:, :LANES]), rope(p[:, LANES:])], axis=1)

    def store_group(ref, val, d):
        if d == 1:
            ref[0, 0] = val.astype(BF16)
        else:
            for c in range(2):
                stg_ref[c] = val[:, c * LANES:(c + 1) * LANES]
            for r in range(d):
                for c in range(2):
                    ref[0, r, :, c * LANES:(c + 1) * LANES] = (
                        stg_ref[c, pl.ds(r, tm // d, stride=d), :].astype(BF16))

    q_refs = (q0_ref, q1_ref, q2_ref)
    k_refs = (k0_ref, k1_ref, k2_ref)
    v_refs = (v0_ref, v1_ref, v2_ref)
    for g, d in enumerate(DILATIONS):
        store_group(q_refs[g], rope256(proj(g * 256, 256)) * Q_SCALE, d)
        store_group(k_refs[g], rope256(proj(768 + g * 256, 256)), d)
        store_group(v_refs[g], proj(1536 + g * 256, 256), d)
    for j in range(2):
        qb_ref[:, j * 256:(j + 1) * 256] = (rope256(proj(2304 + j * 256, 256)) * Q_SCALE).astype(BF16)
    kv = proj(2816, 256)
    kb = rope(kv[:, :LANES])
    vb = kv[:, LANES:]
    kb_sw = pltpu.roll(kb, 64, 1)
    vb_sw = pltpu.roll(vb, 64, 1)
    kb_ref[:, :LANES] = jnp.where(low, kb, kb_sw).astype(BF16)
    kb_ref[:, LANES:] = jnp.where(low, kb_sw, kb).astype(BF16)
    vb_ref[:, :LANES] = jnp.where(low, vb, vb_sw).astype(BF16)
    vb_ref[:, LANES:] = jnp.where(low, vb_sw, vb).astype(BF16)
    for j in range(4):
        sga_ref[:, j * 256:(j + 1) * 256] = jax.nn.sigmoid(proj(3072 + j * 256, 256)).astype(BF16)
        sgb_ref[:, j * 256:(j + 1) * 256] = jax.nn.sigmoid(proj(4096 + j * 256, 256)).astype(BF16)


def _inproj(x2, mod3, g_mix, cos_t, sin_t, w_in_bf, batch, seq):
    n, dm = x2.shape
    tm = 512
    tpb = seq // tm
    grid = (n // tm,)
    row = lambda i: (i, 0)
    strided_specs, strided_shapes = [], []
    for d in DILATIONS:
        for _ in range(3):
            strided_specs.append(pl.BlockSpec((1, d, tm // d, DIL_GROUP_W), lambda i: (i // tpb, 0, i % tpb, 0)))
            strided_shapes.append(jax.ShapeDtypeStruct((batch, d, seq // d, DIL_GROUP_W), BF16))
    out_specs = strided_specs + [
        pl.BlockSpec((tm, 512), row), pl.BlockSpec((tm, 256), row), pl.BlockSpec((tm, 256), row),
        pl.BlockSpec((tm, dm), row), pl.BlockSpec((tm, dm), row)]
    out_shapes = strided_shapes + [
        jax.ShapeDtypeStruct((n, 512), BF16), jax.ShapeDtypeStruct((n, 256), BF16),
        jax.ShapeDtypeStruct((n, 256), BF16), jax.ShapeDtypeStruct((n, dm), BF16),
        jax.ShapeDtypeStruct((n, dm), BF16)]
    return pl.pallas_call(
        functools.partial(_inproj_kernel, tm=tm),
        grid=grid,
        in_specs=[pl.BlockSpec((tm, dm), row),
                  pl.BlockSpec((1, 6, dm), lambda i: (i // tpb, 0, 0)),
                  pl.BlockSpec((1, dm), lambda i: (0, 0)),
                  pl.BlockSpec((tm, LANES), row),
                  pl.BlockSpec((tm, LANES), row),
                  pl.BlockSpec(w_in_bf.shape, lambda i: (0, 0))],
        out_specs=out_specs,
        out_shape=out_shapes,
        scratch_shapes=[pltpu.VMEM((2, tm, LANES), F32)],
        compiler_params=_cparams(("arbitrary",)),
        name="inproj",
    )(x2, mod3, g_mix, cos_t, sin_t, w_in_bf)


def _split_heads(q2, low):
    zero = jnp.zeros_like(q2)
    return jnp.concatenate([jnp.where(low, q2, zero), jnp.where(low, zero, q2)], axis=0)


def _band_softmax(qst, k2, v2, mask, sinks):
    s = lax.dot_general(qst, k2, (((1,), (1,)), ((), ())), preferred_element_type=F32)
    s = jnp.where(mask, s, NEG_INF)
    rows, tk = s.shape
    m = jnp.max(s, axis=-1, keepdims=True)
    if sinks is not None:
        seg = rows // len(sinks)
        m = jnp.concatenate([jnp.maximum(m[h * seg:(h + 1) * seg], sk) for h, sk in enumerate(sinks)], axis=0)
    m = jnp.broadcast_to(m, (rows, LANES))
    e = jnp.concatenate([jnp.exp(s[:, c * LANES:(c + 1) * LANES] - m) for c in range(tk // LANES)], axis=1)
    v_ones = jnp.concatenate([v2, jnp.ones((tk, LANES), BF16)], axis=1)
    od = jnp.dot(e.astype(BF16), v_ones, preferred_element_type=F32)
    o, den = od[:, :LANES], od[:, LANES:]
    if sinks is not None:
        den = jnp.concatenate([den[h * seg:(h + 1) * seg] + jnp.exp(sk - m[h * seg:(h + 1) * seg])
                               for h, sk in enumerate(sinks)], axis=0)
    return o / den, m, den


def _band_mask(qs, ks, nstack, tq, tk, window):
    row = lax.broadcasted_iota(I32, (nstack * tq, tk), 0) & (tq - 1)
    col = lax.broadcasted_iota(I32, (nstack * tq, tk), 1)
    return jnp.abs((ks + col) - (qs + row)) <= window


def _dil_kernel(q_ref, k_ref, v_ref, o_ref, l_ref, *, length, tq, tk):
    low = lax.broadcasted_iota(I32, (tq, LANES), 1) < 64

    nq = length // tq

    def body(j, carry):
        r = j // nq
        qs = pl.multiple_of((j % nq) * tq, tq)
        ks = pl.multiple_of(jnp.clip(qs - DIL_HALF_WINDOW, 0, length - tk), DIL_HALF_WINDOW)
        mask = _band_mask(qs, ks, 2, tq, tk, DIL_HALF_WINDOW)
        for c in range(DIL_GROUP_W // LANES):
            cs = slice(c * LANES, (c + 1) * LANES)
            qst = _split_heads(q_ref[r, pl.ds(qs, tq), cs], low)
            o, m, den = _band_softmax(qst, k_ref[r, pl.ds(ks, tk), cs], v_ref[r, pl.ds(ks, tk), cs], mask, None)
            lse = m + jnp.log(den)
            o_ref[r, pl.ds(qs, tq), cs] = jnp.where(low, o[:tq], o[tq:]).astype(BF16)
            l_ref[r, pl.ds(qs, tq), cs] = jnp.where(low, lse[:tq], lse[tq:])
        return carry

    lax.fori_loop(0, q_ref.shape[0] * nq, body, 0, unroll=8)


def _dil_attention(q, k, v):
    batch, d, length, w = q.shape
    tq, tk = 128, 256
    spec = pl.BlockSpec((None, d, length, w), lambda b: (b, 0, 0, 0))
    return pl.pallas_call(
        functools.partial(_dil_kernel, length=length, tq=tq, tk=tk),
        grid=(batch,),
        in_specs=[spec, spec, spec],
        out_specs=[spec, spec],
        out_shape=[jax.ShapeDtypeStruct(q.shape, BF16), jax.ShapeDtypeStruct(q.shape, F32)],
        compiler_params=_cparams(("arbitrary",)),
        name=f"dil{d}",
    )(q, k, v)


def _swa_kernel(sink_ref, q_ref, k_ref, v_ref, o_ref, *, length, tq, tk):
    low = lax.broadcasted_iota(I32, (tq, LANES), 1) < 64
    nblk = q_ref.shape[1] // LANES

    def body(j, carry):
        qs = pl.multiple_of(j * tq, tq)
        ks = pl.multiple_of(jnp.clip(qs - SWA_WINDOW, 0, length - tk), SWA_WINDOW)
        mask = _band_mask(qs, ks, 2, tq, tk, SWA_WINDOW)
        for b in range(nblk):
            cs = slice((b // 2) * LANES, (b // 2 + 1) * LANES)
            bs = slice(b * LANES, (b + 1) * LANES)
            qst = _split_heads(q_ref[pl.ds(qs, tq), bs], low)
            sinks = (sink_ref[2 * b], sink_ref[2 * b + 1])
            o, _, _ = _band_softmax(qst, k_ref[pl.ds(ks, tk), cs], v_ref[pl.ds(ks, tk), cs], mask, sinks)
            o_ref[pl.ds(qs, tq), bs] = jnp.where(low, o[:tq], o[tq:]).astype(BF16)
        return carry

    lax.fori_loop(0, length // tq, body, 0, unroll=8)


def _swa_attention(sink, q, k, v):
    batch, length, qw = q.shape
    tq, tk = 128, 384
    return pl.pallas_call(
        functools.partial(_swa_kernel, length=length, tq=tq, tk=tk),
        grid=(batch,),
        in_specs=[pl.BlockSpec(memory_space=pltpu.SMEM),
                  pl.BlockSpec((None, length, qw), lambda b: (b, 0, 0)),
                  pl.BlockSpec((None, length, k.shape[2]), lambda b: (b, 0, 0)),
                  pl.BlockSpec((None, length, v.shape[2]), lambda b: (b, 0, 0))],
        out_specs=pl.BlockSpec((None, length, qw), lambda b: (b, 0, 0)),
        out_shape=jax.ShapeDtypeStruct(q.shape, BF16),
        compiler_params=_cparams(("arbitrary",)),
        name="swa",
    )(sink, q, k, v)


def _route_rows(logits):
    lane = lax.broadcasted_iota(I32, logits.shape, 1).astype(F32)
    big = 1e9
    is_g = lane < N_GROUPS
    gl = jnp.where(is_g, logits, NEG_INF)
    gmax = jnp.max(gl, axis=-1, keepdims=True)
    gsel = jnp.min(jnp.where(is_g & (gl == gmax), lane, big), axis=-1, keepdims=True)
    gw = 1.0 / jnp.sum(jnp.where(is_g, jnp.exp(gl - gmax), 0.0), axis=-1, keepdims=True)
    e_lo = N_GROUPS + gsel * EXPERTS_PER_GROUP
    in_grp = (lane >= e_lo) & (lane < e_lo + EXPERTS_PER_GROUP)
    el = jnp.where(in_grp, logits, NEG_INF)
    m1 = jnp.max(el, axis=-1, keepdims=True)
    i1 = jnp.min(jnp.where(in_grp & (el == m1), lane, big), axis=-1, keepdims=True)
    el2 = jnp.where(lane == i1, NEG_INF, el)
    m2 = jnp.max(el2, axis=-1, keepdims=True)
    i2 = jnp.min(jnp.where(in_grp & (lane != i1) & (el2 == m2), lane, big), axis=-1, keepdims=True)
    t = jnp.exp(m2 - m1)
    tw1 = gw / (1.0 + t)
    tw2 = gw * t / (1.0 + t)
    out = jnp.where(lane == 0, tw1, 0.0)
    out = jnp.where(lane == 1, tw2, out)
    out = jnp.where(lane == 2, i1 - N_GROUPS, out)
    return jnp.where(lane == 3, i2 - N_GROUPS, out)


def _outproj_kernel(o0_ref, l0_ref, o1_ref, l1_ref, o2_ref, l2_ref, ob_ref, sga_ref, sgb_ref, x_ref,
                    mod_ref, g_ref, wa_ref, wb_ref, wo_ref, wr_ref, br_ref,
                    x1_ref, xs_ref, route_ref, len_ref,
                    so1_ref, sl1_ref, so2_ref, sl2_ref, h2_ref, *, tm, sub, group):
    ntiles = pl.num_programs(0) - 1

    @pl.when(pl.program_id(0) == ntiles)
    def _():
        xs_ref[...] = jnp.zeros_like(xs_ref)

    pl.when(pl.program_id(0) < ntiles)(functools.partial(
        _outproj_tile, o0_ref, l0_ref, o1_ref, l1_ref, o2_ref, l2_ref, ob_ref, sga_ref, sgb_ref, x_ref,
        mod_ref, g_ref, wa_ref, wb_ref, wo_ref, wr_ref, br_ref, x1_ref, xs_ref, route_ref, len_ref,
        so1_ref, sl1_ref, so2_ref, sl2_ref, h2_ref, tm=tm, sub=sub, group=group))


def _outproj_tile(o0_ref, l0_ref, o1_ref, l1_ref, o2_ref, l2_ref, ob_ref, sga_ref, sgb_ref, x_ref,
                  mod_ref, g_ref, wa_ref, wb_ref, wo_ref, wr_ref, br_ref,
                  x1_ref, xs_ref, route_ref, len_ref,
                  so1_ref, sl1_ref, so2_ref, sl2_ref, h2_ref, *, tm, sub, group):
    dm = x_ref.shape[1]
    for (o_ref, l_ref, so_ref, sl_ref, d) in ((o1_ref, l1_ref, so1_ref, sl1_ref, DILATIONS[1]),
                                              (o2_ref, l2_ref, so2_ref, sl2_ref, DILATIONS[2])):
        for r in range(d):
            for c in range(2):
                cs = slice(c * LANES, (c + 1) * LANES)
                so_ref[c, pl.ds(r, tm // d, stride=d), :] = o_ref[0, r, :, cs].astype(F32)
                sl_ref[c, pl.ds(r, tm // d, stride=d), :] = l_ref[0, r, :, cs]
    mr = lax.broadcasted_iota(I32, (LANES, LANES), 0)
    mc = lax.broadcasted_iota(I32, (LANES, LANES), 1)
    move_hi = jnp.where(((mr < 2) & (mc == 2 * mr)) | ((mr >= 2) & (mr < 4) & (mc == mr + 2)), 1.0, 0.0).astype(BF16)
    move_lo = jnp.where((mr < 2) & (mc == 2 * mr + 1), 1.0, 0.0).astype(BF16)
    for t in range(tm // sub):
        rs = slice(t * sub, (t + 1) * sub)
        both = lambda ref: jnp.concatenate([ref[0, rs, :], ref[1, rs, :]], axis=1)
        o0, l0 = o0_ref[0, 0, rs, :].astype(F32), l0_ref[0, 0, rs, :]
        o1, l1, o2, l2 = both(so1_ref), both(sl1_ref), both(so2_ref), both(sl2_ref)
        mx = jnp.maximum(jnp.maximum(l0, l1), l2)
        w0, w1, w2 = jnp.exp(l0 - mx), jnp.exp(l1 - mx), jnp.exp(l2 - mx)
        o_a = (w0 * o0 + w1 * o1 + w2 * o2) / (w0 + w1 + w2)
        y_a = jnp.dot(o_a.astype(BF16), wa_ref[...], preferred_element_type=F32)
        y_b = jnp.dot(ob_ref[rs, :], wb_ref[...], preferred_element_type=F32)
        merged = sga_ref[rs, :].astype(F32) * y_a + sgb_ref[rs, :].astype(F32) * y_b
        mix = jnp.dot(merged.astype(BF16), wo_ref[...], preferred_element_type=F32)
        x1 = x_ref[rs, :] + mod_ref[0, 2:3, :] * mix
        x1_ref[rs, :] = x1
        h2 = _rms_mod(x1, g_ref[...], mod_ref[0, 3:4, :], mod_ref[0, 4:5, :]).astype(BF16)
        h2_ref[rs, :dm] = h2
        logits = jnp.dot(h2, wr_ref[...], preferred_element_type=F32) + br_ref[...]
        rt = _route_rows(logits)
        route_ref[rs, :] = rt
        hi = rt.astype(BF16)
        lo = (rt - hi.astype(F32)).astype(BF16)
        aux = (jnp.dot(hi, move_hi, preferred_element_type=F32) + jnp.dot(lo, move_lo, preferred_element_type=F32))
        h2_ref[rs, dm:] = aux.astype(BF16)

    part = route_ref[...]
    e1, e2 = part[:, 2:3], part[:, 3:4]
    lane = lax.broadcasted_iota(I32, (tm, LANES), 1).astype(F32)
    onehot = jnp.where((lane == e1) | (lane == e2), 1.0, 0.0)
    rr = lax.broadcasted_iota(I32, (tm, tm), 0)
    cc = lax.broadcasted_iota(I32, (tm, tm), 1)
    tri = jnp.where(rr > cc, 1.0, 0.0).astype(BF16)
    prefix = jnp.dot(tri, onehot.astype(BF16), preferred_element_type=F32)
    cnt = jnp.sum(onehot, axis=0, keepdims=True)
    seg_len = jnp.ceil(cnt * (1.0 / SEG_ROWS)) * SEG_ROWS
    ur = lax.broadcasted_iota(I32, (LANES, LANES), 0)
    uc = lax.broadcasted_iota(I32, (LANES, LANES), 1)
    upper = jnp.where(ur < uc, 1.0, 0.0).astype(BF16)
    seg_off = jnp.dot(jnp.broadcast_to(seg_len, (8, LANES)).astype(BF16), upper, preferred_element_type=F32)[0:1, :]
    slot_map = seg_off + prefix
    s1 = jnp.sum(jnp.where(lane == e1, slot_map, 0.0), axis=-1, keepdims=True)
    s2 = jnp.sum(jnp.where(lane == e2, slot_map, 0.0), axis=-1, keepdims=True)
    len_ref[0] = jnp.broadcast_to(seg_len, (8, LANES))
    route = jnp.where(lane == 2, s1, jnp.where(lane == 3, s2, part))
    route_ref[...] = route

    pr = lax.broadcasted_iota(I32, (8, LANES), 0)
    pc = lax.broadcasted_iota(I32, (8, LANES), 1)
    lane_pick = jnp.where(pc == pr + 2, 1.0, 0.0)
    slots_t = lax.dot_general(lane_pick, route, (((1,), (1,)), ((), ())), preferred_element_type=F32,
                              precision=lax.Precision.HIGHEST)
    live = jnp.sum(seg_len).astype(I32)

    def sort_rows(lo, rows):
        srow = (lax.broadcasted_iota(I32, (rows, tm), 0) + lo).astype(F32)
        pick = jnp.where((srow == slots_t[0:1, :]) | (srow == slots_t[1:2, :]), 1.0, 0.0).astype(BF16)
        xs_ref[lo:lo + rows, :] = jnp.dot(pick, h2_ref[...], preferred_element_type=F32).astype(BF16)

    def clear_rows(lo, rows):
        xs_ref[lo:lo + rows, :] = jnp.zeros((rows, xs_ref.shape[1]), BF16)

    head = xs_ref.shape[0] - group
    sort_rows(0, head)
    pl.when(live > head)(functools.partial(sort_rows, head, group))
    pl.when(live <= head)(functools.partial(clear_rows, head, group))


def _outproj(dil_outs, ob, sga, sgb, x2, mod3, g_ffn, wa, wb, wo, wr, br, batch, seq, nslots):
    n, dm = x2.shape
    tm = MOE_TILE
    tpb = seq // tm
    ntiles = n // tm
    tile = lambda i: jnp.minimum(i, ntiles - 1)
    row = lambda i: (tile(i), 0)
    const = lambda i: (0, 0)
    in_specs = []
    args = []
    for (o, l), d in zip(dil_outs, DILATIONS):
        spec = pl.BlockSpec((1, d, tm // d, DIL_GROUP_W), lambda i: (tile(i) // tpb, 0, tile(i) % tpb, 0))
        in_specs += [spec, spec]
        args += [o, l]
    in_specs += [pl.BlockSpec((tm, ob.shape[1]), row), pl.BlockSpec((tm, dm), row), pl.BlockSpec((tm, dm), row),
                 pl.BlockSpec((tm, dm), row),
                 pl.BlockSpec((1, 6, dm), lambda i: (tile(i) // tpb, 0, 0)),
                 pl.BlockSpec((1, dm), const),
                 pl.BlockSpec(wa.shape, const), pl.BlockSpec(wb.shape, const), pl.BlockSpec(wo.shape, const),
                 pl.BlockSpec(wr.shape, const), pl.BlockSpec(br.shape, const)]
    args += [ob, sga, sgb, x2, mod3, g_ffn, wa, wb, wo, wr, br]
    width = dm + LANES
    return pl.pallas_call(
        functools.partial(_outproj_kernel, tm=tm, sub=512, group=SLOT_GROUP),
        grid=(ntiles + 1,),
        in_specs=in_specs,
        out_specs=[pl.BlockSpec((tm, dm), row), pl.BlockSpec((nslots, width), lambda i: (i, 0)),
                   pl.BlockSpec((tm, LANES), row), pl.BlockSpec((1, 8, LANES), lambda i: (tile(i), 0, 0))],
        out_shape=[jax.ShapeDtypeStruct((n, dm), F32), jax.ShapeDtypeStruct(((ntiles + 1) * nslots, width), BF16),
                   jax.ShapeDtypeStruct((n, LANES), F32), jax.ShapeDtypeStruct((ntiles, 8, LANES), F32)],
        scratch_shapes=[pltpu.VMEM((2, tm, LANES), F32)] * 4 + [pltpu.VMEM((tm, width), BF16)],
        compiler_params=_cparams(("arbitrary",)),
        name="outproj",
    )(*args)


def _wait_copies(count, copy):
    def wait_one(c, carry):
        copy.wait()
        return carry

    lax.fori_loop(0, count, wait_one, 0)


def _experts_kernel(first_ref, count_ref, chunk_ref, live_ref, wg_ref, wu_ref, wd_ref, xs_hbm, ys_hbm,
                    xbuf, ybuf, zbuf, wgb_ref, wub_ref, wdb_ref, xsem, ysem, zsem, *, nslots, ntiles):
    e = pl.program_id(0)
    last = pl.num_programs(0) - 1
    first = first_ref[e]
    count = count_ref[e]
    used = first_ref[last] + count_ref[last]
    dm = ybuf.shape[2]
    cpb = ROW_BLOCK // SEG_ROWS
    cpt = nslots // SEG_ROWS

    def chunk_rows(blk, j):
        return pl.ds(pl.multiple_of(chunk_ref[blk * cpb + j], SEG_ROWS), SEG_ROWS)

    def gather(blk):
        for j in range(cpb):
            pltpu.make_async_copy(xs_hbm.at[chunk_rows(blk, j), :], xbuf.at[blk & 1, pl.ds(j * SEG_ROWS, SEG_ROWS), :],
                                  xsem.at[blk & 1]).start(priority=1)

    def scatter(blk):
        for j in range(cpb):
            pltpu.make_async_copy(ybuf.at[blk & 1, pl.ds(j * SEG_ROWS, SEG_ROWS), :], ys_hbm.at[chunk_rows(blk, j), :],
                                  ysem.at[blk & 1]).start()

    def gather_done(blk):
        return pltpu.make_async_copy(xs_hbm.at[pl.ds(0, ROW_BLOCK), :], xbuf.at[blk & 1], xsem.at[blk & 1])

    def scatter_done(blk):
        return pltpu.make_async_copy(ybuf.at[blk & 1], ys_hbm.at[pl.ds(0, ROW_BLOCK), :], ysem.at[blk & 1])

    def zero_chunk(row):
        return pltpu.make_async_copy(zbuf.at[pl.ds(0, SEG_ROWS), :], ys_hbm.at[pl.ds(row, SEG_ROWS), :], zsem)

    def dead_chunks(t):
        return cpt - live_ref[t] // SEG_ROWS

    @pl.when(e == 0)
    def _():
        zbuf[...] = jnp.zeros_like(zbuf)
        for b in range(nslots // ROW_BLOCK):
            pltpu.make_async_copy(zbuf, ys_hbm.at[pl.ds(ntiles * nslots + b * ROW_BLOCK, ROW_BLOCK), :], zsem).start()
        for b in range(nslots // ROW_BLOCK):
            pltpu.make_async_copy(zbuf, ys_hbm.at[pl.ds(0, ROW_BLOCK), :], zsem).wait()

        def tile(t, carry):
            def chunk(c, carry2):
                zero_chunk(pl.multiple_of(t * nslots + c * SEG_ROWS, SEG_ROWS)).start()
                return carry2

            lax.fori_loop(live_ref[t] // SEG_ROWS, cpt, chunk, 0)
            return carry

        lax.fori_loop(0, ntiles, tile, 0)

        @pl.when(used > 0)
        def _():
            gather(0)

    wgb_ref[...] = wg_ref[0].astype(BF16)
    wub_ref[...] = wu_ref[0].astype(BF16)
    wdb_ref[...] = wd_ref[0].astype(BF16)

    def body(b, carry):
        blk = first + b
        slot = blk & 1
        gather_done(blk).wait()

        @pl.when(blk + 1 < used)
        def _():
            gather(blk + 1)

        @pl.when(blk >= 2)
        def _():
            scatter_done(blk - 2).wait()

        xb = xbuf[slot, :, :dm]
        aux = xbuf[slot, :, dm:].astype(F32)
        w = jnp.where(aux[:, 4:5] == e.astype(F32), aux[:, 0:1] + aux[:, 1:2], aux[:, 2:3] + aux[:, 3:4])
        g = jnp.dot(xb, wgb_ref[...], preferred_element_type=F32)
        u = jnp.dot(xb, wub_ref[...], preferred_element_type=F32)
        a = (g * jax.nn.sigmoid(g)) * u
        y = jnp.dot(a.astype(BF16), wdb_ref[...], preferred_element_type=F32)
        ybuf[slot] = (y * w).astype(BF16)
        scatter(blk)
        return carry

    lax.fori_loop(0, count, body, 0)

    @pl.when(e == last)
    def _():
        @pl.when(used >= 2)
        def _():
            scatter_done(used - 2).wait()

        @pl.when(used >= 1)
        def _():
            scatter_done(used - 1).wait()

        _wait_copies(lax.fori_loop(0, ntiles, lambda t, acc: acc + dead_chunks(t), 0), zero_chunk(0))


def _experts(first_blk, count_blk, chunk_tbl, live_tbl, x_sorted, w_gate, w_up, w_down, nslots, ntiles):
    rows, width = x_sorted.shape
    n_exp, dm, de = w_gate.shape
    wmap = lambda e, *_: (e, 0, 0)
    grid_spec = pltpu.PrefetchScalarGridSpec(
        num_scalar_prefetch=4,
        grid=(n_exp,),
        in_specs=[pl.BlockSpec((1, dm, de), wmap),
                  pl.BlockSpec((1, dm, de), wmap),
                  pl.BlockSpec((1, de, dm), wmap),
                  pl.BlockSpec(memory_space=pl.ANY)],
        out_specs=pl.BlockSpec(memory_space=pl.ANY),
        scratch_shapes=[pltpu.VMEM((2, ROW_BLOCK, width), BF16), pltpu.VMEM((2, ROW_BLOCK, dm), BF16),
                        pltpu.VMEM((ROW_BLOCK, dm), BF16),
                        pltpu.VMEM((dm, de), BF16), pltpu.VMEM((dm, de), BF16), pltpu.VMEM((de, dm), BF16),
                        pltpu.SemaphoreType.DMA((2,)), pltpu.SemaphoreType.DMA((2,)), pltpu.SemaphoreType.DMA(())],
    )
    return pl.pallas_call(
        functools.partial(_experts_kernel, nslots=nslots, ntiles=ntiles),
        grid_spec=grid_spec,
        out_shape=jax.ShapeDtypeStruct((rows, dm), BF16),
        compiler_params=_cparams(("arbitrary",)),
        name="experts",
    )(first_blk, count_blk, chunk_tbl, live_tbl, w_gate, w_up, w_down, x_sorted)


def _combine_kernel(live_ref, route_ref, x1_ref, mod_ref, g_ref, ys_ref, yt_ref, o_ref, *, tm, head):
    i = pl.program_id(0)
    route = route_ref[...]

    def one_hot(lo, rows):
        scol = (lax.broadcasted_iota(I32, (tm, rows), 1) + lo).astype(F32)
        return jnp.where((scol == route[:, 2:3]) | (scol == route[:, 3:4]), 1.0, 0.0).astype(BF16)

    def finish(with_tail):
        moe = jnp.dot(one_hot(0, head), ys_ref[...], preferred_element_type=F32)
        if with_tail:
            moe = moe + jnp.dot(one_hot(head, yt_ref.shape[0]), yt_ref[...], preferred_element_type=F32)
        x = x1_ref[...] + mod_ref[0, 5:6, :] * moe
        ms = jnp.mean(x * x, axis=-1, keepdims=True)
        o_ref[...] = (x * lax.rsqrt(ms + RMS_EPS)) * g_ref[...]

    pl.when(live_ref[i] > head)(functools.partial(finish, True))
    pl.when(live_ref[i] <= head)(functools.partial(finish, False))


def _combine(live_tbl, route, x1, mod3, g_final, y_sorted, seq, tm, nslots):
    n, dm = x1.shape
    tpb = seq // tm
    ntiles = n // tm
    head = nslots - SLOT_GROUP
    grid_spec = pltpu.PrefetchScalarGridSpec(
        num_scalar_prefetch=1,
        grid=(n // tm,),
        in_specs=[pl.BlockSpec((tm, LANES), lambda i, *_: (i, 0)),
                  pl.BlockSpec((tm, dm), lambda i, *_: (i, 0)),
                  pl.BlockSpec((1, 6, dm), lambda i, *_: (i // tpb, 0, 0)),
                  pl.BlockSpec((1, dm), lambda i, *_: (0, 0)),
                  pl.BlockSpec((None, head, dm), lambda i, live: (i, 0, 0)),
                  pl.BlockSpec((None, SLOT_GROUP, dm),
                               lambda i, live: (jnp.where(live[i] > head, i, ntiles), head // SLOT_GROUP, 0))],
        out_specs=pl.BlockSpec((tm, dm), lambda i, *_: (i, 0)),
    )
    y3 = y_sorted.reshape(ntiles + 1, nslots, dm)
    return pl.pallas_call(
        functools.partial(_combine_kernel, tm=tm, head=head),
        grid_spec=grid_spec,
        out_shape=jax.ShapeDtypeStruct((n, dm), F32),
        compiler_params=_cparams(("arbitrary",)),
        name="combine",
    )(live_tbl, route, x1, mod3, g_final, y3, y3)


def _rope_tables(positions):
    half = HEAD_DIM // 2
    inv_freq = ROPE_THETA ** (-jnp.arange(half, dtype=F32) * (2.0 / HEAD_DIM))
    freq = jnp.tile(inv_freq, LANES // half)
    sign = jnp.tile(jnp.concatenate([-jnp.ones((half,), F32), jnp.ones((half,), F32)]), LANES // HEAD_DIM)
    ang = positions.astype(F32).reshape(-1, 1) * freq
    return jnp.cos(ang), jnp.sin(ang) * sign


def _expert_chunk_table(lens, pstart, tot, pend, nblk, nslots):
    ntiles = lens.shape[0]
    cpb = ROW_BLOCK // SEG_ROWS
    runs = lens // SEG_ROWS
    q = jnp.arange(nblk * cpb, dtype=I32)
    owner = jnp.minimum(jnp.sum((pend[None, :] // SEG_ROWS <= q[:, None]).astype(I32), axis=1), N_EXPERTS - 1)
    is_e = owner[:, None] == jnp.arange(N_EXPERTS, dtype=I32)[None, :]
    pick_e = lambda v: jnp.sum(jnp.where(is_e, v[None, :], 0), axis=1)
    off = q - pick_e(pstart // SEG_ROWS)
    in_run = (off < pick_e(tot // SEG_ROWS)) & (q < pend[-1] // SEG_ROWS)
    upto = jnp.cumsum(runs, axis=0)
    upto_e = jnp.sum(jnp.where(is_e[:, None, :], upto[None, :, :], 0), axis=2)
    tile = jnp.minimum(jnp.sum((upto_e <= off[:, None]).astype(I32), axis=1), ntiles - 1)
    is_t = tile[:, None] == jnp.arange(ntiles, dtype=I32)[None, :]
    run_slot0 = jnp.cumsum(runs, axis=1) - runs - (upto - runs)
    slot0 = jnp.sum(jnp.where(is_t[:, :, None] & is_e[:, None, :], run_slot0[None, :, :], 0), axis=(1, 2))
    live_row = tile * nslots + (slot0 + off) * SEG_ROWS
    spare_row = ntiles * nslots + (((q // cpb) % 2) * cpb + q % cpb) * SEG_ROWS
    return jnp.where(in_run, live_row, spare_row)


def kernel(x, c, positions, w_ada, b_ada, g_mix, w_in, sink_logits, w_branch_a, w_branch_b, w_out, g_ffn,
           w_group, b_group, w_route, b_route, w_expert_gate, w_expert_up, w_expert_down, g_final):
    batch, seq, dm = x.shape
    n = batch * seq
    assert w_ada.shape[0] == 1, "one layer"
    x2 = x.reshape(n, dm)

    c8 = jnp.pad(c, ((0, 8 - batch), (0, 0)))
    mod = _ada(c8, w_ada[0], b_ada[0].reshape(1, -1))
    mod3 = mod[:batch].reshape(batch, 6, dm)

    cos_t, sin_t = _rope_tables(positions)
    outs = _inproj(x2, mod3, g_mix[0].reshape(1, dm), cos_t, sin_t, w_in[0].astype(BF16), batch, seq)
    qkv = outs[:9]
    qb, kb, vb, sga, sgb = outs[9:]

    dil_outs = [_dil_attention(qkv[3 * g], qkv[3 * g + 1], qkv[3 * g + 2]) for g in range(len(DILATIONS))]
    ob = _swa_attention(sink_logits[0], qb.reshape(batch, seq, -1), kb.reshape(batch, seq, -1),
                        vb.reshape(batch, seq, -1)).reshape(n, -1)

    pad = LANES - N_GROUPS - N_EXPERTS
    wr = jnp.concatenate([w_group[0], w_route[0], jnp.zeros((dm, pad), F32)], axis=1).astype(BF16)
    br = jnp.concatenate([b_group[0], b_route[0], jnp.zeros((pad,), F32)]).reshape(1, LANES)
    ntiles = n // MOE_TILE
    nslots = 2 * MOE_TILE + N_EXPERTS * SEG_ROWS
    x1, x_sorted, route, seg_lens = _outproj(dil_outs, ob, sga, sgb, x2, mod3, g_ffn[0].reshape(1, dm),
                                             w_branch_a[0].astype(BF16), w_branch_b[0].astype(BF16),
                                             w_out[0].astype(BF16), wr, br, batch, seq, nslots)

    nblk = -(-(2 * n + ntiles * N_EXPERTS * (SEG_ROWS - 1)) // ROW_BLOCK) + N_EXPERTS
    lens = seg_lens[:, 0, :N_EXPERTS].astype(I32)
    tot = jnp.sum(lens, axis=0)
    padded = (tot + ROW_BLOCK - 1) // ROW_BLOCK * ROW_BLOCK
    pend = jnp.cumsum(padded)
    pstart = pend - padded
    chunk_tbl = _expert_chunk_table(lens, pstart, tot, pend, nblk, nslots)
    live_tbl = jnp.sum(lens, axis=1)

    y_sorted = _experts(pstart // ROW_BLOCK, padded // ROW_BLOCK, chunk_tbl, live_tbl, x_sorted,
                        w_expert_gate[0], w_expert_up[0], w_expert_down[0], nslots, ntiles)
    out = _combine(live_tbl, route, x1, mod3, g_final.reshape(1, dm), y_sorted, seq, MOE_TILE, nslots)
    return out.reshape(batch, seq, dm)
```

```python
import functools

import jax
import jax.numpy as jnp
from jax import lax
from jax.experimental import pallas as pl
from jax.experimental.pallas import tpu as pltpu

F32 = jnp.float32
BF16 = jnp.bfloat16
I32 = jnp.int32

HEAD_DIM = 64
ROPE_THETA = 10000.0
RMS_EPS = 1e-6
NEG_INF = -1e30
Q_SCALE = HEAD_DIM ** -0.5
DILATIONS = (1, 4, 16)
DIL_HALF_WINDOW = 64
DIL_GROUP_W = 256
SWA_WINDOW = 128
N_GROUPS = 4
EXPERTS_PER_GROUP = 8
N_EXPERTS = 32
LANES = 128
ROW_BLOCK = 512
SEG_ROWS = 16
MOE_TILE = 512
SLOT_GROUP = 256
VMEM_LIMIT = 56 * 1024 * 1024


def _cparams(sem):
    return pltpu.CompilerParams(dimension_semantics=sem, vmem_limit_bytes=VMEM_LIMIT)


def _ada_kernel(c_ref, w_ref, b_ref, o_ref):
    c = c_ref[...]
    cs = c * jax.nn.sigmoid(c)
    o_ref[...] = jnp.dot(cs.astype(BF16), w_ref[...].astype(BF16), preferred_element_type=F32) + b_ref[...]


def _ada(c8, w_ada, b_ada):
    d, n = w_ada.shape
    tn = 1536
    return pl.pallas_call(
        _ada_kernel,
        grid=(n // tn,),
        in_specs=[pl.BlockSpec((8, d), lambda j: (0, 0)),
                  pl.BlockSpec((d, tn), lambda j: (0, j)),
                  pl.BlockSpec((1, tn), lambda j: (0, j))],
        out_specs=pl.BlockSpec((8, tn), lambda j: (0, j)),
        out_shape=jax.ShapeDtypeStruct((8, n), F32),
        compiler_params=_cparams(("arbitrary",)),
        name="ada",
    )(c8, w_ada, b_ada)


def _rms_mod(x, g, shift, scale):
    ms = jnp.mean(x * x, axis=-1, keepdims=True)
    return (x * lax.rsqrt(ms + RMS_EPS)) * (g * (1.0 + scale)) + shift


def _inproj_kernel(x_ref, mod_ref, g_ref, cos_ref, sin_ref, w_ref,
                   q0_ref, k0_ref, v0_ref, q1_ref, k1_ref, v1_ref, q2_ref, k2_ref, v2_ref,
                   qb_ref, kb_ref, vb_ref, sga_ref, sgb_ref, stg_ref, *, tm):
    h = _rms_mod(x_ref[...], g_ref[...], mod_ref[0, 0:1, :], mod_ref[0, 1:2, :])
    hb = h.astype(BF16)
    cos = cos_ref[...]
    sin = sin_ref[...]
    lane = lax.broadcasted_iota(I32, (tm, LANES), 1)
    first_half = (lane & 32) == 0
    low = lane < 64

    def proj(c0, width):
        return jnp.dot(hb, w_ref[:, c0:c0 + width], preferred_element_type=F32)

    def rope(t):
        rot = jnp.where(first_half, pltpu.roll(t, 96, 1), pltpu.roll(t, 32, 1))
        return t * cos + rot * sin

    def rope256(p):
        return jnp.concatenate([rope(p[:, :LANES]), rope(p[:, LANES:])], axis=1)

    def store_group(ref, val, d):
        if d == 1:
            ref[0, 0] = val.astype(BF16)
        else:
            for c in range(2):
                stg_ref[c] = val[:, c * LANES:(c + 1) * LANES]
            for r in range(d):
                for c in range(2):
                    ref[0, r, :, c * LANES:(c + 1) * LANES] = (
                        stg_ref[c, pl.ds(r, tm // d, stride=d), :].astype(BF16))

    q_refs = (q0_ref, q1_ref, q2_ref)
    k_refs = (k0_ref, k1_ref, k2_ref)
    v_refs = (v0_ref, v1_ref, v2_ref)
    for g, d in enumerate(DILATIONS):
        store_group(q_refs[g], rope256(proj(g * 256, 256)) * Q_SCALE, d)
        store_group(k_refs[g], rope256(proj(768 + g * 256, 256)), d)
        store_group(v_refs[g], proj(1536 + g * 256, 256), d)
    for j in range(2):
        qb_ref[:, j * 256:(j + 1) * 256] = (rope256(proj(2304 + j * 256, 256)) * Q_SCALE).astype(BF16)
    kv = proj(2816, 256)
    kb = rope(kv[:, :LANES])
    vb = kv[:, LANES:]
    kb_sw = pltpu.roll(kb, 64, 1)
    vb_sw = pltpu.roll(vb, 64, 1)
    kb_ref[:, :LANES] = jnp.where(low, kb, kb_sw).astype(BF16)
    kb_ref[:, LANES:] = jnp.where(low, kb_sw, kb).astype(BF16)
    vb_ref[:, :LANES] = jnp.where(low, vb, vb_sw).astype(BF16)
    vb_ref[:, LANES:] = jnp.where(low, vb_sw, vb).astype(BF16)
    for j in range(4):
        sga_ref[:, j * 256:(j + 1) * 256] = jax.nn.sigmoid(proj(3072 + j * 256, 256)).astype(BF16)
        sgb_ref[:, j * 256:(j + 1) * 256] = jax.nn.sigmoid(proj(4096 + j * 256, 256)).astype(BF16)


def _inproj(x2, mod3, g_mix, cos_t, sin_t, w_in_bf, batch, seq):
    n, dm = x2.shape
    tm = 512
    tpb = seq // tm
    grid = (n // tm,)
    row = lambda i: (i, 0)
    strided_specs, strided_shapes = [], []
    for d in DILATIONS:
        for _ in range(3):
            strided_specs.append(pl.BlockSpec((1, d, tm // d, DIL_GROUP_W), lambda i: (i // tpb, 0, i % tpb, 0)))
            strided_shapes.append(jax.ShapeDtypeStruct((batch, d, seq // d, DIL_GROUP_W), BF16))
    out_specs = strided_specs + [
        pl.BlockSpec((tm, 512), row), pl.BlockSpec((tm, 256), row), pl.BlockSpec((tm, 256), row),
        pl.BlockSpec((tm, dm), row), pl.BlockSpec((tm, dm), row)]
    out_shapes = strided_shapes + [
        jax.ShapeDtypeStruct((n, 512), BF16), jax.ShapeDtypeStruct((n, 256), BF16),
        jax.ShapeDtypeStruct((n, 256), BF16), jax.ShapeDtypeStruct((n, dm), BF16),
        jax.ShapeDtypeStruct((n, dm), BF16)]
    return pl.pallas_call(
        functools.partial(_inproj_kernel, tm=tm),
        grid=grid,
        in_specs=[pl.BlockSpec((tm, dm), row),
                  pl.BlockSpec((1, 6, dm), lambda i: (i // tpb, 0, 0)),
                  pl.BlockSpec((1, dm), lambda i: (0, 0)),
                  pl.BlockSpec((tm, LANES), row),
                  pl.BlockSpec((tm, LANES), row),
                  pl.BlockSpec(w_in_bf.shape, lambda i: (0, 0))],
        out_specs=out_specs,
        out_shape=out_shapes,
        scratch_shapes=[pltpu.VMEM((2, tm, LANES), F32)],
        compiler_params=_cparams(("arbitrary",)),
        name="inproj",
    )(x2, mod3, g_mix, cos_t, sin_t, w_in_bf)


def _split_heads(q2, low):
    zero = jnp.zeros_like(q2)
    return jnp.concatenate([jnp.where(low, q2, zero), jnp.where(low, zero, q2)], axis=0)


def _band_softmax(qst, k2, v2, mask, sinks):
    s = lax.dot_general(qst, k2, (((1,), (1,)), ((), ())), preferred_element_type=F32)
    s = jnp.where(mask, s, NEG_INF)
    rows, tk = s.shape
    m = jnp.max(s, axis=-1, keepdims=True)
    if sinks is not None:
        seg = rows // len(sinks)
        m = jnp.concatenate([jnp.maximum(m[h * seg:(h + 1) * seg], sk) for h, sk in enumerate(sinks)], axis=0)
    m = jnp.broadcast_to(m, (rows, LANES))
    e = jnp.concatenate([jnp.exp(s[:, c * LANES:(c + 1) * LANES] - m) for c in range(tk // LANES)], axis=1)
    v_ones = jnp.concatenate([v2, jnp.ones((tk, LANES), BF16)], axis=1)
    od = jnp.dot(e.astype(BF16), v_ones, preferred_element_type=F32)
    o, den = od[:, :LANES], od[:, LANES:]
    if sinks is not None:
        den = jnp.concatenate([den[h * seg:(h + 1) * seg] + jnp.exp(sk - m[h * seg:(h + 1) * seg])
                               for h, sk in enumerate(sinks)], axis=0)
    return o / den, m, den


def _band_mask(qs, ks, nstack, tq, tk, window):
    row = lax.broadcasted_iota(I32, (nstack * tq, tk), 0) & (tq - 1)
    col = lax.broadcasted_iota(I32, (nstack * tq, tk), 1)
    return jnp.abs((ks + col) - (qs + row)) <= window


def _dil_kernel(q_ref, k_ref, v_ref, o_ref, l_ref, *, length, tq, tk):
    low = lax.broadcasted_iota(I32, (tq, LANES), 1) < 64

    nq = length // tq

    def body(j, carry):
        r = j // nq
        qs = pl.multiple_of((j % nq) * tq, tq)
        ks = pl.multiple_of(jnp.clip(qs - DIL_HALF_WINDOW, 0, length - tk), DIL_HALF_WINDOW)
        mask = _band_mask(qs, ks, 2, tq, tk, DIL_HALF_WINDOW)
        for c in range(DIL_GROUP_W // LANES):
            cs = slice(c * LANES, (c + 1) * LANES)
            qst = _split_heads(q_ref[r, pl.ds(qs, tq), cs], low)
            o, m, den = _band_softmax(qst, k_ref[r, pl.ds(ks, tk), cs], v_ref[r, pl.ds(ks, tk), cs], mask, None)
            lse = m + jnp.log(den)
            o_ref[r, pl.ds(qs, tq), cs] = jnp.where(low, o[:tq], o[tq:]).astype(BF16)
            l_ref[r, pl.ds(qs, tq), cs] = jnp.where(low, lse[:tq], lse[tq:])
        return carry

    lax.fori_loop(0, q_ref.shape[0] * nq, body, 0, unroll=8)


def _dil_attention(q, k, v):
    batch, d, length, w = q.shape
    tq, tk = 128, 256
    spec = pl.BlockSpec((None, d, length, w), lambda b: (b, 0, 0, 0))
    return pl.pallas_call(
        functools.partial(_dil_kernel, length=length, tq=tq, tk=tk),
        grid=(batch,),
        in_specs=[spec, spec, spec],
        out_specs=[spec, spec],
        out_shape=[jax.ShapeDtypeStruct(q.shape, BF16), jax.ShapeDtypeStruct(q.shape, F32)],
        compiler_params=_cparams(("arbitrary",)),
        name=f"dil{d}",
    )(q, k, v)


def _swa_kernel(sink_ref, q_ref, k_ref, v_ref, o_ref, *, length, tq, tk):
    low = lax.broadcasted_iota(I32, (tq, LANES), 1) < 64
    nblk = q_ref.shape[1] // LANES

    def body(j, carry):
        qs = pl.multiple_of(j * tq, tq)
        ks = pl.multiple_of(jnp.clip(qs - SWA_WINDOW, 0, length - tk), SWA_WINDOW)
        mask = _band_mask(qs, ks, 2, tq, tk, SWA_WINDOW)
        for b in range(nblk):
            cs = slice((b // 2) * LANES, (b // 2 + 1) * LANES)
            bs = slice(b * LANES, (b + 1) * LANES)
            qst = _split_heads(q_ref[pl.ds(qs, tq), bs], low)
            sinks = (sink_ref[2 * b], sink_ref[2 * b + 1])
            o, _, _ = _band_softmax(qst, k_ref[pl.ds(ks, tk), cs], v_ref[pl.ds(ks, tk), cs], mask, sinks)
            o_ref[pl.ds(qs, tq), bs] = jnp.where(low, o[:tq], o[tq:]).astype(BF16)
        return carry

    lax.fori_loop(0, length // tq, body, 0, unroll=8)


def _swa_attention(sink, q, k, v):
    batch, length, qw = q.shape
    tq, tk = 128, 384
    return pl.pallas_call(
        functools.partial(_swa_kernel, length=length, tq=tq, tk=tk),
        grid=(batch,),
        in_specs=[pl.BlockSpec(memory_space=pltpu.SMEM),
                  pl.BlockSpec((None, length, qw), lambda b: (b, 0, 0)),
                  pl.BlockSpec((None, length, k.shape[2]), lambda b: (b, 0, 0)),
                  pl.BlockSpec((None, length, v.shape[2]), lambda b: (b, 0, 0))],
        out_specs=pl.BlockSpec((None, length, qw), lambda b: (b, 0, 0)),
        out_shape=jax.ShapeDtypeStruct(q.shape, BF16),
        compiler_params=_cparams(("arbitrary",)),
        name="swa",
    )(sink, q, k, v)


def _route_rows(logits):
    lane = lax.broadcasted_iota(I32, logits.shape, 1).astype(F32)
    big = 1e9
    is_g = lane < N_GROUPS
    gl = jnp.where(is_g, logits, NEG_INF)
    gmax = jnp.max(gl, axis=-1, keepdims=True)
    gsel = jnp.min(jnp.where(is_g & (gl == gmax), lane, big), axis=-1, keepdims=True)
    gw = 1.0 / jnp.sum(jnp.where(is_g, jnp.exp(gl - gmax), 0.0), axis=-1, keepdims=True)
    e_lo = N_GROUPS + gsel * EXPERTS_PER_GROUP
    in_grp = (lane >= e_lo) & (lane < e_lo + EXPERTS_PER_GROUP)
    el = jnp.where(in_grp, logits, NEG_INF)
    m1 = jnp.max(el, axis=-1, keepdims=True)
    i1 = jnp.min(jnp.where(in_grp & (el == m1), lane, big), axis=-1, keepdims=True)
    el2 = jnp.where(lane == i1, NEG_INF, el)
    m2 = jnp.max(el2, axis=-1, keepdims=True)
    i2 = jnp.min(jnp.where(in_grp & (lane != i1) & (el2 == m2), lane, big), axis=-1, keepdims=True)
    t = jnp.exp(m2 - m1)
    tw1 = gw / (1.0 + t)
    tw2 = gw * t / (1.0 + t)
    out = jnp.where(lane == 0, tw1, 0.0)
    out = jnp.where(lane == 1, tw2, out)
    out = jnp.where(lane == 2, i1 - N_GROUPS, out)
    return jnp.where(lane == 3, i2 - N_GROUPS, out)


def _outproj_kernel(o0_ref, l0_ref, o1_ref, l1_ref, o2_ref, l2_ref, ob_ref, sga_ref, sgb_ref, x_ref,
                    mod_ref, g_ref, wa_ref, wb_ref, wo_ref, wr_ref, br_ref,
                    x1_ref, xs_ref, route_ref, len_ref,
                    so1_ref, sl1_ref, so2_ref, sl2_ref, h2_ref, *, tm, sub, group):
    ntiles = pl.num_programs(0) - 1

    @pl.when(pl.program_id(0) == ntiles)
    def _():
        xs_ref[...] = jnp.zeros_like(xs_ref)

    pl.when(pl.program_id(0) < ntiles)(functools.partial(
        _outproj_tile, o0_ref, l0_ref, o1_ref, l1_ref, o2_ref, l2_ref, ob_ref, sga_ref, sgb_ref, x_ref,
        mod_ref, g_ref, wa_ref, wb_ref, wo_ref, wr_ref, br_ref, x1_ref, xs_ref, route_ref, len_ref,
        so1_ref, sl1_ref, so2_ref, sl2_ref, h2_ref, tm=tm, sub=sub, group=group))


def _outproj_tile(o0_ref, l0_ref, o1_ref, l1_ref, o2_ref, l2_ref, ob_ref, sga_ref, sgb_ref, x_ref,
                  mod_ref, g_ref, wa_ref, wb_ref, wo_ref, wr_ref, br_ref,
                  x1_ref, xs_ref, route_ref, len_ref,
                  so1_ref, sl1_ref, so2_ref, sl2_ref, h2_ref, *, tm, sub, group):
    dm = x_ref.shape[1]
    for (o_ref, l_ref, so_ref, sl_ref, d) in ((o1_ref, l1_ref, so1_ref, sl1_ref, DILATIONS[1]),
                                              (o2_ref, l2_ref, so2_ref, sl2_ref, DILATIONS[2])):
        for r in range(d):
            for c in range(2):
                cs = slice(c * LANES, (c + 1) * LANES)
                so_ref[c, pl.ds(r, tm // d, stride=d), :] = o_ref[0, r, :, cs].astype(F32)
                sl_ref[c, pl.ds(r, tm // d, stride=d), :] = l_ref[0, r, :, cs]
    mr = lax.broadcasted_iota(I32, (LANES, LANES), 0)
    mc = lax.broadcasted_iota(I32, (LANES, LANES), 1)
    move_hi = jnp.where(((mr < 2) & (mc == 2 * mr)) | ((mr >= 2) & (mr < 4) & (mc == mr + 2)), 1.0, 0.0).astype(BF16)
    move_lo = jnp.where((mr < 2) & (mc == 2 * mr + 1), 1.0, 0.0).astype(BF16)
    for t in range(tm // sub):
        rs = slice(t * sub, (t + 1) * sub)
        both = lambda ref: jnp.concatenate([ref[0, rs, :], ref[1, rs, :]], axis=1)
        o0, l0 = o0_ref[0, 0, rs, :].astype(F32), l0_ref[0, 0, rs, :]
        o1, l1, o2, l2 = both(so1_ref), both(sl1_ref), both(so2_ref), both(sl2_ref)
        mx = jnp.maximum(jnp.maximum(l0, l1), l2)
        w0, w1, w2 = jnp.exp(l0 - mx), jnp.exp(l1 - mx), jnp.exp(l2 - mx)
        o_a = (w0 * o0 + w1 * o1 + w2 * o2) / (w0 + w1 + w2)
        y_a = jnp.dot(o_a.astype(BF16), wa_ref[...], preferred_element_type=F32)
        y_b = jnp.dot(ob_ref[rs, :], wb_ref[...], preferred_element_type=F32)
        merged = sga_ref[rs, :].astype(F32) * y_a + sgb_ref[rs, :].astype(F32) * y_b
        mix = jnp.dot(merged.astype(BF16), wo_ref[...], preferred_element_type=F32)
        x1 = x_ref[rs, :] + mod_ref[0, 2:3, :] * mix
        x1_ref[rs, :] = x1
        h2 = _rms_mod(x1, g_ref[...], mod_ref[0, 3:4, :], mod_ref[0, 4:5, :]).astype(BF16)
        h2_ref[rs, :dm] = h2
        logits = jnp.dot(h2, wr_ref[...], preferred_element_type=F32) + br_ref[...]
        rt = _route_rows(logits)
        route_ref[rs, :] = rt
        hi = rt.astype(BF16)
        lo = (rt - hi.astype(F32)).astype(BF16)
        aux = (jnp.dot(hi, move_hi, preferred_element_type=F32) + jnp.dot(lo, move_lo, preferred_element_type=F32))
        h2_ref[rs, dm:] = aux.astype(BF16)

    part = route_ref[...]
    e1, e2 = part[:, 2:3], part[:, 3:4]
    lane = lax.broadcasted_iota(I32, (tm, LANES), 1).astype(F32)
    onehot = jnp.where((lane == e1) | (lane == e2), 1.0, 0.0)
    rr = lax.broadcasted_iota(I32, (tm, tm), 0)
    cc = lax.broadcasted_iota(I32, (tm, tm), 1)
    tri = jnp.where(rr > cc, 1.0, 0.0).astype(BF16)
    prefix = jnp.dot(tri, onehot.astype(BF16), preferred_element_type=F32)
    cnt = jnp.sum(onehot, axis=0, keepdims=True)
    seg_len = jnp.ceil(cnt * (1.0 / SEG_ROWS)) * SEG_ROWS
    ur = lax.broadcasted_iota(I32, (LANES, LANES), 0)
    uc = lax.broadcasted_iota(I32, (LANES, LANES), 1)
    upper = jnp.where(ur < uc, 1.0, 0.0).astype(BF16)
    seg_off = jnp.dot(jnp.broadcast_to(seg_len, (8, LANES)).astype(BF16), upper, preferred_element_type=F32)[0:1, :]
    slot_map = seg_off + prefix
    s1 = jnp.sum(jnp.where(lane == e1, slot_map, 0.0), axis=-1, keepdims=True)
    s2 = jnp.sum(jnp.where(lane == e2, slot_map, 0.0), axis=-1, keepdims=True)
    len_ref[0] = jnp.broadcast_to(seg_len, (8, LANES))
    route = jnp.where(lane == 2, s1, jnp.where(lane == 3, s2, part))
    route_ref[...] = route

    pr = lax.broadcasted_iota(I32, (8, LANES), 0)
    pc = lax.broadcasted_iota(I32, (8, LANES), 1)
    lane_pick = jnp.where(pc == pr + 2, 1.0, 0.0)
    slots_t = lax.dot_general(lane_pick, route, (((1,), (1,)), ((), ())), preferred_element_type=F32,
                              precision=lax.Precision.HIGHEST)
    live = jnp.sum(seg_len).astype(I32)

    def sort_rows(lo, rows):
        srow = (lax.broadcasted_iota(I32, (rows, tm), 0) + lo).astype(F32)
        pick = jnp.where((srow == slots_t[0:1, :]) | (srow == slots_t[1:2, :]), 1.0, 0.0).astype(BF16)
        xs_ref[lo:lo + rows, :] = jnp.dot(pick, h2_ref[...], preferred_element_type=F32).astype(BF16)

    def clear_rows(lo, rows):
        xs_ref[lo:lo + rows, :] = jnp.zeros((rows, xs_ref.shape[1]), BF16)

    head = xs_ref.shape[0] - group
    sort_rows(0, head)
    pl.when(live > head)(functools.partial(sort_rows, head, group))
    pl.when(live <= head)(functools.partial(clear_rows, head, group))


def _outproj(dil_outs, ob, sga, sgb, x2, mod3, g_ffn, wa, wb, wo, wr, br, batch, seq, nslots):
    n, dm = x2.shape
    tm = MOE_TILE
    tpb = seq // tm
    ntiles = n // tm
    tile = lambda i: jnp.minimum(i, ntiles - 1)
    row = lambda i: (tile(i), 0)
    const = lambda i: (0, 0)
    in_specs = []
    args = []
    for (o, l), d in zip(dil_outs, DILATIONS):
        spec = pl.BlockSpec((1, d, tm // d, DIL_GROUP_W), lambda i: (tile(i) // tpb, 0, tile(i) % tpb, 0))
        in_specs += [spec, spec]
        args += [o, l]
    in_specs += [pl.BlockSpec((tm, ob.shape[1]), row), pl.BlockSpec((tm, dm), row), pl.BlockSpec((tm, dm), row),
                 pl.BlockSpec((tm, dm), row),
                 pl.BlockSpec((1, 6, dm), lambda i: (tile(i) // tpb, 0, 0)),
                 pl.BlockSpec((1, dm), const),
                 pl.BlockSpec(wa.shape, const), pl.BlockSpec(wb.shape, const), pl.BlockSpec(wo.shape, const),
                 pl.BlockSpec(wr.shape, const), pl.BlockSpec(br.shape, const)]
    args += [ob, sga, sgb, x2, mod3, g_ffn, wa, wb, wo, wr, br]
    width = dm + LANES
    return pl.pallas_call(
        functools.partial(_outproj_kernel, tm=tm, sub=512, group=SLOT_GROUP),
        grid=(ntiles + 1,),
        in_specs=in_specs,
        out_specs=[pl.BlockSpec((tm, dm), row), pl.BlockSpec((nslots, width), lambda i: (i, 0)),
                   pl.BlockSpec((tm, LANES), row), pl.BlockSpec((1, 8, LANES), lambda i: (tile(i), 0, 0))],
        out_shape=[jax.ShapeDtypeStruct((n, dm), F32), jax.ShapeDtypeStruct(((ntiles + 1) * nslots, width), BF16),
                   jax.ShapeDtypeStruct((n, LANES), F32), jax.ShapeDtypeStruct((ntiles, 8, LANES), F32)],
        scratch_shapes=[pltpu.VMEM((2, tm, LANES), F32)] * 4 + [pltpu.VMEM((tm, width), BF16)],
        compiler_params=_cparams(("arbitrary",)),
        name="outproj",
    )(*args)


def _wait_copies(count, copy):
    def wait_one(c, carry):
        copy.wait()
        return carry

    lax.fori_loop(0, count, wait_one, 0)


def _experts_kernel(first_ref, count_ref, chunk_ref, live_ref, wg_ref, wu_ref, wd_ref, xs_hbm, ys_hbm,
                    xbuf, ybuf, zbuf, wgb_ref, wub_ref, wdb_ref, xsem, ysem, zsem, *, nslots, ntiles):
    s = pl.program_id(0)
    last = pl.num_programs(0) - 2
    e = jnp.maximum(s - 1, 0)
    first = first_ref[e]
    count = jnp.where(s > 0, count_ref[e], 0)
    used = first_ref[last] + count_ref[last]
    n_cast = 4
    dm = ybuf.shape[2]
    cpb = ROW_BLOCK // SEG_ROWS
    cpt = nslots // SEG_ROWS

    def chunk_rows(blk, j):
        return pl.ds(pl.multiple_of(chunk_ref[blk * cpb + j], SEG_ROWS), SEG_ROWS)

    def gather(blk):
        for j in range(cpb):
            pltpu.make_async_copy(xs_hbm.at[chunk_rows(blk, j), :], xbuf.at[blk & 1, pl.ds(j * SEG_ROWS, SEG_ROWS), :],
                                  xsem.at[blk & 1]).start(priority=1)

    def scatter(blk):
        for j in range(cpb):
            pltpu.make_async_copy(ybuf.at[blk & 1, pl.ds(j * SEG_ROWS, SEG_ROWS), :], ys_hbm.at[chunk_rows(blk, j), :],
                                  ysem.at[blk & 1]).start()

    def gather_done(blk):
        return pltpu.make_async_copy(xs_hbm.at[pl.ds(0, ROW_BLOCK), :], xbuf.at[blk & 1], xsem.at[blk & 1])

    def scatter_done(blk):
        return pltpu.make_async_copy(ybuf.at[blk & 1], ys_hbm.at[pl.ds(0, ROW_BLOCK), :], ysem.at[blk & 1])

    def zero_chunk(row):
        return pltpu.make_async_copy(zbuf.at[pl.ds(0, SEG_ROWS), :], ys_hbm.at[pl.ds(row, SEG_ROWS), :], zsem)

    def dead_chunks(t):
        return cpt - live_ref[t] // SEG_ROWS

    def cast_piece(k):
        for src, dst in ((wg_ref, wgb_ref), (wu_ref, wub_ref), (wd_ref, wdb_ref)):
            rows = src.shape[1] // n_cast
            sl = pl.ds(pl.multiple_of(k * rows, rows), rows)
            dst[s & 1, sl, :] = src[0, sl, :].astype(BF16)

    @pl.when(s == 0)
    def _():
        zbuf[...] = jnp.zeros_like(zbuf)
        for b in range(nslots // ROW_BLOCK):
            pltpu.make_async_copy(zbuf, ys_hbm.at[pl.ds(ntiles * nslots + b * ROW_BLOCK, ROW_BLOCK), :], zsem).start()
        for b in range(nslots // ROW_BLOCK):
            pltpu.make_async_copy(zbuf, ys_hbm.at[pl.ds(0, ROW_BLOCK), :], zsem).wait()

        def tile(t, carry):
            def chunk(c, carry2):
                zero_chunk(pl.multiple_of(t * nslots + c * SEG_ROWS, SEG_ROWS)).start()
                return carry2

            lax.fori_loop(live_ref[t] // SEG_ROWS, cpt, chunk, 0)
            return carry

        lax.fori_loop(0, ntiles, tile, 0)

        @pl.when(used > 0)
        def _():
            gather(0)

    wslot = (s - 1) & 1

    def body(b, carry):
        blk = first + b
        slot = blk & 1
        cast_piece(jnp.minimum(b, n_cast - 1))
        gather_done(blk).wait()

        @pl.when(blk + 1 < used)
        def _():
            gather(blk + 1)

        @pl.when(blk >= 2)
        def _():
            scatter_done(blk - 2).wait()

        xb = xbuf[slot, :, :dm]
        aux = xbuf[slot, :, dm:].astype(F32)
        w = jnp.where(aux[:, 4:5] == e.astype(F32), aux[:, 0:1] + aux[:, 1:2], aux[:, 2:3] + aux[:, 3:4])
        g = jnp.dot(xb, wgb_ref[wslot], preferred_element_type=F32)
        u = jnp.dot(xb, wub_ref[wslot], preferred_element_type=F32)
        a = (g * jax.nn.sigmoid(g)) * u
        y = jnp.dot(a.astype(BF16), wdb_ref[wslot], preferred_element_type=F32)
        ybuf[slot] = (y * w).astype(BF16)
        scatter(blk)
        return carry

    lax.fori_loop(0, count, body, 0)

    for k in range(n_cast):
        pl.when((count <= k) & (s <= last))(functools.partial(cast_piece, k))

    @pl.when(s == last + 1)
    def _():
        @pl.when(used >= 2)
        def _():
            scatter_done(used - 2).wait()

        @pl.when(used >= 1)
        def _():
            scatter_done(used - 1).wait()

        _wait_copies(lax.fori_loop(0, ntiles, lambda t, acc: acc + dead_chunks(t), 0), zero_chunk(0))


def _experts(first_blk, count_blk, chunk_tbl, live_tbl, x_sorted, w_gate, w_up, w_down, nslots, ntiles):
    rows, width = x_sorted.shape
    n_exp, dm, de = w_gate.shape
    wmap = lambda s, *_: (jnp.minimum(s, n_exp - 1), 0, 0)
    grid_spec = pltpu.PrefetchScalarGridSpec(
        num_scalar_prefetch=4,
        grid=(n_exp + 1,),
        in_specs=[pl.BlockSpec((1, dm, de), wmap),
                  pl.BlockSpec((1, dm, de), wmap),
                  pl.BlockSpec((1, de, dm), wmap),
                  pl.BlockSpec(memory_space=pl.ANY)],
        out_specs=pl.BlockSpec(memory_space=pl.ANY),
        scratch_shapes=[pltpu.VMEM((2, ROW_BLOCK, width), BF16), pltpu.VMEM((2, ROW_BLOCK, dm), BF16),
                        pltpu.VMEM((ROW_BLOCK, dm), BF16),
                        pltpu.VMEM((2, dm, de), BF16), pltpu.VMEM((2, dm, de), BF16), pltpu.VMEM((2, de, dm), BF16),
                        pltpu.SemaphoreType.DMA((2,)), pltpu.SemaphoreType.DMA((2,)), pltpu.SemaphoreType.DMA(())],
    )
    return pl.pallas_call(
        functools.partial(_experts_kernel, nslots=nslots, ntiles=ntiles),
        grid_spec=grid_spec,
        out_shape=jax.ShapeDtypeStruct((rows, dm), BF16),
        compiler_params=_cparams(("arbitrary",)),
        name="experts",
    )(first_blk, count_blk, chunk_tbl, live_tbl, w_gate, w_up, w_down, x_sorted)


def _combine_kernel(live_ref, route_ref, x1_ref, mod_ref, g_ref, ys_ref, yt_ref, o_ref, *, tm, head):
    i = pl.program_id(0)
    route = route_ref[...]

    def one_hot(lo, rows):
        scol = (lax.broadcasted_iota(I32, (tm, rows), 1) + lo).astype(F32)
        return jnp.where((scol == route[:, 2:3]) | (scol == route[:, 3:4]), 1.0, 0.0).astype(BF16)

    def finish(with_tail):
        moe = jnp.dot(one_hot(0, head), ys_ref[...], preferred_element_type=F32)
        if with_tail:
            moe = moe + jnp.dot(one_hot(head, yt_ref.shape[0]), yt_ref[...], preferred_element_type=F32)
        x = x1_ref[...] + mod_ref[0, 5:6, :] * moe
        ms = jnp.mean(x * x, axis=-1, keepdims=True)
        o_ref[...] = (x * lax.rsqrt(ms + RMS_EPS)) * g_ref[...]

    pl.when(live_ref[i] > head)(functools.partial(finish, True))
    pl.when(live_ref[i] <= head)(functools.partial(finish, False))


def _combine(live_tbl, route, x1, mod3, g_final, y_sorted, seq, tm, nslots):
    n, dm = x1.shape
    tpb = seq // tm
    ntiles = n // tm
    head = nslots - SLOT_GROUP
    grid_spec = pltpu.PrefetchScalarGridSpec(
        num_scalar_prefetch=1,
        grid=(n // tm,),
        in_specs=[pl.BlockSpec((tm, LANES), lambda i, *_: (i, 0)),
                  pl.BlockSpec((tm, dm), lambda i, *_: (i, 0)),
                  pl.BlockSpec((1, 6, dm), lambda i, *_: (i // tpb, 0, 0)),
                  pl.BlockSpec((1, dm), lambda i, *_: (0, 0)),
                  pl.BlockSpec((None, head, dm), lambda i, live: (i, 0, 0)),
                  pl.BlockSpec((None, SLOT_GROUP, dm),
                               lambda i, live: (jnp.where(live[i] > head, i, ntiles), head // SLOT_GROUP, 0))],
        out_specs=pl.BlockSpec((tm, dm), lambda i, *_: (i, 0)),
    )
    y3 = y_sorted.reshape(ntiles + 1, nslots, dm)
    return pl.pallas_call(
        functools.partial(_combine_kernel, tm=tm, head=head),
        grid_spec=grid_spec,
        out_shape=jax.ShapeDtypeStruct((n, dm), F32),
        compiler_params=_cparams(("arbitrary",)),
        name="combine",
    )(live_tbl, route, x1, mod3, g_final, y3, y3)


def _rope_tables(positions):
    half = HEAD_DIM // 2
    inv_freq = ROPE_THETA ** (-jnp.arange(half, dtype=F32) * (2.0 / HEAD_DIM))
    freq = jnp.tile(inv_freq, LANES // half)
    sign = jnp.tile(jnp.concatenate([-jnp.ones((half,), F32), jnp.ones((half,), F32)]), LANES // HEAD_DIM)
    ang = positions.astype(F32).reshape(-1, 1) * freq
    return jnp.cos(ang), jnp.sin(ang) * sign


def _expert_chunk_table(lens, pstart, tot, pend, nblk, nslots):
    ntiles = lens.shape[0]
    cpb = ROW_BLOCK // SEG_ROWS
    runs = lens // SEG_ROWS
    q = jnp.arange(nblk * cpb, dtype=I32)
    owner = jnp.minimum(jnp.sum((pend[None, :] // SEG_ROWS <= q[:, None]).astype(I32), axis=1), N_EXPERTS - 1)
    is_e = owner[:, None] == jnp.arange(N_EXPERTS, dtype=I32)[None, :]
    pick_e = lambda v: jnp.sum(jnp.where(is_e, v[None, :], 0), axis=1)
    off = q - pick_e(pstart // SEG_ROWS)
    in_run = (off < pick_e(tot // SEG_ROWS)) & (q < pend[-1] // SEG_ROWS)
    upto = jnp.cumsum(runs, axis=0)
    upto_e = jnp.sum(jnp.where(is_e[:, None, :], upto[None, :, :], 0), axis=2)
    tile = jnp.minimum(jnp.sum((upto_e <= off[:, None]).astype(I32), axis=1), ntiles - 1)
    is_t = tile[:, None] == jnp.arange(ntiles, dtype=I32)[None, :]
    run_slot0 = jnp.cumsum(runs, axis=1) - runs - (upto - runs)
    slot0 = jnp.sum(jnp.where(is_t[:, :, None] & is_e[:, None, :], run_slot0[None, :, :], 0), axis=(1, 2))
    live_row = tile * nslots + (slot0 + off) * SEG_ROWS
    spare_row = ntiles * nslots + (((q // cpb) % 2) * cpb + q % cpb) * SEG_ROWS
    return jnp.where(in_run, live_row, spare_row)


def kernel(x, c, positions, w_ada, b_ada, g_mix, w_in, sink_logits, w_branch_a, w_branch_b, w_out, g_ffn,
           w_group, b_group, w_route, b_route, w_expert_gate, w_expert_up, w_expert_down, g_final):
    batch, seq, dm = x.shape
    n = batch * seq
    assert w_ada.shape[0] == 1, "one layer"
    x2 = x.reshape(n, dm)

    c8 = jnp.pad(c, ((0, 8 - batch), (0, 0)))
    mod = _ada(c8, w_ada[0], b_ada[0].reshape(1, -1))
    mod3 = mod[:batch].reshape(batch, 6, dm)

    cos_t, sin_t = _rope_tables(positions)
    outs = _inproj(x2, mod3, g_mix[0].reshape(1, dm), cos_t, sin_t, w_in[0].astype(BF16), batch, seq)
    qkv = outs[:9]
    qb, kb, vb, sga, sgb = outs[9:]

    dil_outs = [_dil_attention(qkv[3 * g], qkv[3 * g + 1], qkv[3 * g + 2]) for g in range(len(DILATIONS))]
    ob = _swa_attention(sink_logits[0], qb.reshape(batch, seq, -1), kb.reshape(batch, seq, -1),
                        vb.reshape(batch, seq, -1)).reshape(n, -1)

    pad = LANES - N_GROUPS - N_EXPERTS
    wr = jnp.concatenate([w_group[0], w_route[0], jnp.zeros((dm, pad), F32)], axis=1).astype(BF16)
    br = jnp.concatenate([b_group[0], b_route[0], jnp.zeros((pad,), F32)]).reshape(1, LANES)
    ntiles = n // MOE_TILE
    nslots = 2 * MOE_TILE + N_EXPERTS * SEG_ROWS
    x1, x_sorted, route, seg_lens = _outproj(dil_outs, ob, sga, sgb, x2, mod3, g_ffn[0].reshape(1, dm),
                                             w_branch_a[0].astype(BF16), w_branch_b[0].astype(BF16),
                                             w_out[0].astype(BF16), wr, br, batch, seq, nslots)

    nblk = -(-(2 * n + ntiles * N_EXPERTS * (SEG_ROWS - 1)) // ROW_BLOCK) + N_EXPERTS
    lens = seg_lens[:, 0, :N_EXPERTS].astype(I32)
    tot = jnp.sum(lens, axis=0)
    padded = (tot + ROW_BLOCK - 1) // ROW_BLOCK * ROW_BLOCK
    pend = jnp.cumsum(padded)
    pstart = pend - padded
    chunk_tbl = _expert_chunk_table(lens, pstart, tot, pend, nblk, nslots)
    live_tbl = jnp.sum(lens, axis=1)

    y_sorted = _experts(pstart // ROW_BLOCK, padded // ROW_BLOCK, chunk_tbl, live_tbl, x_sorted,
                        w_expert_gate[0], w_expert_up[0], w_expert_down[0], nslots, ntiles)
    out = _combine(live_tbl, route, x1, mod3, g_final.reshape(1, dm), y_sorted, seq, MOE_TILE, nslots)
    return out.reshape(batch, seq, dm)
```

```python
import functools

import jax
import jax.numpy as jnp
from jax import lax
from jax.experimental import pallas as pl
from jax.experimental.pallas import tpu as pltpu

F32 = jnp.float32
BF16 = jnp.bfloat16
I32 = jnp.int32

HEAD_DIM = 64
ROPE_THETA = 10000.0
RMS_EPS = 1e-6
NEG_INF = -1e30
Q_SCALE = HEAD_DIM ** -0.5
DILATIONS = (1, 4, 16)
DIL_HALF_WINDOW = 64
DIL_GROUP_W = 256
SWA_WINDOW = 128
N_GROUPS = 4
EXPERTS_PER_GROUP = 8
N_EXPERTS = 32
LANES = 128
ROW_BLOCK = 512
SEG_ROWS = 16
MOE_TILE = 512
SLOT_GROUP = 256
VMEM_LIMIT = 56 * 1024 * 1024


def _cparams(sem):
    return pltpu.CompilerParams(dimension_semantics=sem, vmem_limit_bytes=VMEM_LIMIT)


def _ada_kernel(c_ref, w_ref, b_ref, o_ref):
    c = c_ref[...]
    cs = c * jax.nn.sigmoid(c)
    o_ref[...] = jnp.dot(cs.astype(BF16), w_ref[...].astype(BF16), preferred_element_type=F32) + b_ref[...]


def _ada(c8, w_ada, b_ada):
    d, n = w_ada.shape
    tn = 1536
    return pl.pallas_call(
        _ada_kernel,
        grid=(n // tn,),
        in_specs=[pl.BlockSpec((8, d), lambda j: (0, 0)),
                  pl.BlockSpec((d, tn), lambda j: (0, j)),
                  pl.BlockSpec((1, tn), lambda j: (0, j))],
        out_specs=pl.BlockSpec((8, tn), lambda j: (0, j)),
        out_shape=jax.ShapeDtypeStruct((8, n), F32),
        compiler_params=_cparams(("arbitrary",)),
        name="ada",
    )(c8, w_ada, b_ada)


def _rms_mod(x, g, shift, scale):
    ms = jnp.mean(x * x, axis=-1, keepdims=True)
    return (x * lax.rsqrt(ms + RMS_EPS)) * (g * (1.0 + scale)) + shift


def _inproj_kernel(x_ref, mod_ref, g_ref, cos_ref, sin_ref, w_ref, eg_ref, eu_ref, ed_ref,
                   q0_ref, k0_ref, v0_ref, q1_ref, k1_ref, v1_ref, q2_ref, k2_ref, v2_ref,
                   qb_ref, kb_ref, vb_ref, sga_ref, sgb_ref, egb_ref, eub_ref, edb_ref, stg_ref, *, tm, n_exp):
    @pl.when(pl.program_id(0) < n_exp)
    def _():
        egb_ref[...] = eg_ref[...].astype(BF16)
        eub_ref[...] = eu_ref[...].astype(BF16)
        edb_ref[...] = ed_ref[...].astype(BF16)

    h = _rms_mod(x_ref[...], g_ref[...], mod_ref[0, 0:1, :], mod_ref[0, 1:2, :])
    hb = h.astype(BF16)
    cos = cos_ref[...]
    sin = sin_ref[...]
    lane = lax.broadcasted_iota(I32, (tm, LANES), 1)
    first_half = (lane & 32) == 0
    low = lane < 64

    def proj(c0, width):
        return jnp.dot(hb, w_ref[:, c0:c0 + width], preferred_element_type=F32)

    def rope(t):
        rot = jnp.where(first_half, pltpu.roll(t, 96, 1), pltpu.roll(t, 32, 1))
        return t * cos + rot * sin

    def rope256(p):
        return jnp.concatenate([rope(p[:, :LANES]), rope(p[:, LANES:])], axis=1)

    def store_group(ref, val, d):
        if d == 1:
            ref[0, 0] = val.astype(BF16)
        else:
            for c in range(2):
                stg_ref[c] = val[:, c * LANES:(c + 1) * LANES]
            for r in range(d):
                for c in range(2):
                    ref[0, r, :, c * LANES:(c + 1) * LANES] = (
                        stg_ref[c, pl.ds(r, tm // d, stride=d), :].astype(BF16))

    q_refs = (q0_ref, q1_ref, q2_ref)
    k_refs = (k0_ref, k1_ref, k2_ref)
    v_refs = (v0_ref, v1_ref, v2_ref)
    for g, d in enumerate(DILATIONS):
        store_group(q_refs[g], rope256(proj(g * 256, 256)) * Q_SCALE, d)
        store_group(k_refs[g], rope256(proj(768 + g * 256, 256)), d)
        store_group(v_refs[g], proj(1536 + g * 256, 256), d)
    for j in range(2):
        qb_ref[:, j * 256:(j + 1) * 256] = (rope256(proj(2304 + j * 256, 256)) * Q_SCALE).astype(BF16)
    kv = proj(2816, 256)
    kb = rope(kv[:, :LANES])
    vb = kv[:, LANES:]
    kb_sw = pltpu.roll(kb, 64, 1)
    vb_sw = pltpu.roll(vb, 64, 1)
    kb_ref[:, :LANES] = jnp.where(low, kb, kb_sw).astype(BF16)
    kb_ref[:, LANES:] = jnp.where(low, kb_sw, kb).astype(BF16)
    vb_ref[:, :LANES] = jnp.where(low, vb, vb_sw).astype(BF16)
    vb_ref[:, LANES:] = jnp.where(low, vb_sw, vb).astype(BF16)
    for j in range(4):
        sga_ref[:, j * 256:(j + 1) * 256] = jax.nn.sigmoid(proj(3072 + j * 256, 256)).astype(BF16)
        sgb_ref[:, j * 256:(j + 1) * 256] = jax.nn.sigmoid(proj(4096 + j * 256, 256)).astype(BF16)


def _inproj(x2, mod3, g_mix, cos_t, sin_t, w_in_bf, expert_w, batch, seq):
    n, dm = x2.shape
    tm = 512
    tpb = seq // tm
    grid = (n // tm,)
    row = lambda i: (i, 0)
    strided_specs, strided_shapes = [], []
    for d in DILATIONS:
        for _ in range(3):
            strided_specs.append(pl.BlockSpec((1, d, tm // d, DIL_GROUP_W), lambda i: (i // tpb, 0, i % tpb, 0)))
            strided_shapes.append(jax.ShapeDtypeStruct((batch, d, seq // d, DIL_GROUP_W), BF16))
    out_specs = strided_specs + [
        pl.BlockSpec((tm, 512), row), pl.BlockSpec((tm, 256), row), pl.BlockSpec((tm, 256), row),
        pl.BlockSpec((tm, dm), row), pl.BlockSpec((tm, dm), row)]
    out_shapes = strided_shapes + [
        jax.ShapeDtypeStruct((n, 512), BF16), jax.ShapeDtypeStruct((n, 256), BF16),
        jax.ShapeDtypeStruct((n, 256), BF16), jax.ShapeDtypeStruct((n, dm), BF16),
        jax.ShapeDtypeStruct((n, dm), BF16)]
    n_exp = expert_w[0].shape[0]
    assert n // tm >= n_exp, "one expert's weights are cast per grid step"
    emap = lambda i: (jnp.minimum(i, n_exp - 1), 0, 0)
    expert_specs = [pl.BlockSpec((1,) + w.shape[1:], emap) for w in expert_w]
    return pl.pallas_call(
        functools.partial(_inproj_kernel, tm=tm, n_exp=n_exp),
        grid=grid,
        in_specs=[pl.BlockSpec((tm, dm), row),
                  pl.BlockSpec((1, 6, dm), lambda i: (i // tpb, 0, 0)),
                  pl.BlockSpec((1, dm), lambda i: (0, 0)),
                  pl.BlockSpec((tm, LANES), row),
                  pl.BlockSpec((tm, LANES), row),
                  pl.BlockSpec(w_in_bf.shape, lambda i: (0, 0), pipeline_mode=pl.Buffered(1))] + expert_specs,
        out_specs=out_specs + expert_specs,
        out_shape=out_shapes + [jax.ShapeDtypeStruct(w.shape, BF16) for w in expert_w],
        scratch_shapes=[pltpu.VMEM((2, tm, LANES), F32)],
        compiler_params=_cparams(("arbitrary",)),
        name="inproj",
    )(x2, mod3, g_mix, cos_t, sin_t, w_in_bf, *expert_w)


def _split_heads(q2, low):
    zero = jnp.zeros_like(q2)
    return jnp.concatenate([jnp.where(low, q2, zero), jnp.where(low, zero, q2)], axis=0)


def _band_softmax(qst, k2, v2, mask, sinks):
    s = lax.dot_general(qst, k2, (((1,), (1,)), ((), ())), preferred_element_type=F32)
    s = jnp.where(mask, s, NEG_INF)
    rows, tk = s.shape
    m = jnp.max(s, axis=-1, keepdims=True)
    if sinks is not None:
        seg = rows // len(sinks)
        m = jnp.concatenate([jnp.maximum(m[h * seg:(h + 1) * seg], sk) for h, sk in enumerate(sinks)], axis=0)
    m = jnp.broadcast_to(m, (rows, LANES))
    e = jnp.concatenate([jnp.exp(s[:, c * LANES:(c + 1) * LANES] - m) for c in range(tk // LANES)], axis=1)
    v_ones = jnp.concatenate([v2, jnp.ones((tk, LANES), BF16)], axis=1)
    od = jnp.dot(e.astype(BF16), v_ones, preferred_element_type=F32)
    o, den = od[:, :LANES], od[:, LANES:]
    if sinks is not None:
        den = jnp.concatenate([den[h * seg:(h + 1) * seg] + jnp.exp(sk - m[h * seg:(h + 1) * seg])
                               for h, sk in enumerate(sinks)], axis=0)
    return o / den, m, den


def _band_mask(qs, ks, nstack, tq, tk, window):
    row = lax.broadcasted_iota(I32, (nstack * tq, tk), 0) & (tq - 1)
    col = lax.broadcasted_iota(I32, (nstack * tq, tk), 1)
    return jnp.abs((ks + col) - (qs + row)) <= window


def _dil_kernel(q_ref, k_ref, v_ref, o_ref, l_ref, *, length, tq, tk):
    low = lax.broadcasted_iota(I32, (tq, LANES), 1) < 64

    nq = length // tq

    def body(j, carry):
        r = j // nq
        qs = pl.multiple_of((j % nq) * tq, tq)
        ks = pl.multiple_of(jnp.clip(qs - DIL_HALF_WINDOW, 0, length - tk), DIL_HALF_WINDOW)
        mask = _band_mask(qs, ks, 2, tq, tk, DIL_HALF_WINDOW)
        for c in range(DIL_GROUP_W // LANES):
            cs = slice(c * LANES, (c + 1) * LANES)
            qst = _split_heads(q_ref[r, pl.ds(qs, tq), cs], low)
            o, m, den = _band_softmax(qst, k_ref[r, pl.ds(ks, tk), cs], v_ref[r, pl.ds(ks, tk), cs], mask, None)
            lse = m + jnp.log(den)
            o_ref[r, pl.ds(qs, tq), cs] = jnp.where(low, o[:tq], o[tq:]).astype(BF16)
            l_ref[r, pl.ds(qs, tq), cs] = jnp.where(low, lse[:tq], lse[tq:])
        return carry

    lax.fori_loop(0, q_ref.shape[0] * nq, body, 0, unroll=8)


def _dil_attention(q, k, v):
    batch, d, length, w = q.shape
    tq, tk = 128, 256
    spec = pl.BlockSpec((None, d, length, w), lambda b: (b, 0, 0, 0))
    return pl.pallas_call(
        functools.partial(_dil_kernel, length=length, tq=tq, tk=tk),
        grid=(batch,),
        in_specs=[spec, spec, spec],
        out_specs=[spec, spec],
        out_shape=[jax.ShapeDtypeStruct(q.shape, BF16), jax.ShapeDtypeStruct(q.shape, F32)],
        compiler_params=_cparams(("arbitrary",)),
        name=f"dil{d}",
    )(q, k, v)


def _swa_kernel(sink_ref, q_ref, k_ref, v_ref, o_ref, *, length, tq, tk):
    low = lax.broadcasted_iota(I32, (tq, LANES), 1) < 64
    nblk = q_ref.shape[1] // LANES

    def body(j, carry):
        qs = pl.multiple_of(j * tq, tq)
        ks = pl.multiple_of(jnp.clip(qs - SWA_WINDOW, 0, length - tk), SWA_WINDOW)
        mask = _band_mask(qs, ks, 2, tq, tk, SWA_WINDOW)
        for b in range(nblk):
            cs = slice((b // 2) * LANES, (b // 2 + 1) * LANES)
            bs = slice(b * LANES, (b + 1) * LANES)
            qst = _split_heads(q_ref[pl.ds(qs, tq), bs], low)
            sinks = (sink_ref[2 * b], sink_ref[2 * b + 1])
            o, _, _ = _band_softmax(qst, k_ref[pl.ds(ks, tk), cs], v_ref[pl.ds(ks, tk), cs], mask, sinks)
            o_ref[pl.ds(qs, tq), bs] = jnp.where(low, o[:tq], o[tq:]).astype(BF16)
        return carry

    lax.fori_loop(0, length // tq, body, 0, unroll=8)


def _swa_attention(sink, q, k, v):
    batch, length, qw = q.shape
    tq, tk = 128, 384
    return pl.pallas_call(
        functools.partial(_swa_kernel, length=length, tq=tq, tk=tk),
        grid=(batch,),
        in_specs=[pl.BlockSpec(memory_space=pltpu.SMEM),
                  pl.BlockSpec((None, length, qw), lambda b: (b, 0, 0)),
                  pl.BlockSpec((None, length, k.shape[2]), lambda b: (b, 0, 0)),
                  pl.BlockSpec((None, length, v.shape[2]), lambda b: (b, 0, 0))],
        out_specs=pl.BlockSpec((None, length, qw), lambda b: (b, 0, 0)),
        out_shape=jax.ShapeDtypeStruct(q.shape, BF16),
        compiler_params=_cparams(("arbitrary",)),
        name="swa",
    )(sink, q, k, v)


def _route_rows(logits):
    lane = lax.broadcasted_iota(I32, logits.shape, 1).astype(F32)
    big = 1e9
    is_g = lane < N_GROUPS
    gl = jnp.where(is_g, logits, NEG_INF)
    gmax = jnp.max(gl, axis=-1, keepdims=True)
    gsel = jnp.min(jnp.where(is_g & (gl == gmax), lane, big), axis=-1, keepdims=True)
    gw = 1.0 / jnp.sum(jnp.where(is_g, jnp.exp(gl - gmax), 0.0), axis=-1, keepdims=True)
    e_lo = N_GROUPS + gsel * EXPERTS_PER_GROUP
    in_grp = (lane >= e_lo) & (lane < e_lo + EXPERTS_PER_GROUP)
    el = jnp.where(in_grp, logits, NEG_INF)
    m1 = jnp.max(el, axis=-1, keepdims=True)
    i1 = jnp.min(jnp.where(in_grp & (el == m1), lane, big), axis=-1, keepdims=True)
    el2 = jnp.where(lane == i1, NEG_INF, el)
    m2 = jnp.max(el2, axis=-1, keepdims=True)
    i2 = jnp.min(jnp.where(in_grp & (lane != i1) & (el2 == m2), lane, big), axis=-1, keepdims=True)
    t = jnp.exp(m2 - m1)
    tw1 = gw / (1.0 + t)
    tw2 = gw * t / (1.0 + t)
    out = jnp.where(lane == 0, tw1, 0.0)
    out = jnp.where(lane == 1, tw2, out)
    out = jnp.where(lane == 2, i1 - N_GROUPS, out)
    return jnp.where(lane == 3, i2 - N_GROUPS, out)


def _outproj_kernel(o0_ref, l0_ref, o1_ref, l1_ref, o2_ref, l2_ref, ob_ref, sga_ref, sgb_ref, x_ref,
                    mod_ref, g_ref, wa_ref, wb_ref, wo_ref, wr_ref, br_ref,
                    x1_ref, xs_ref, route_ref, len_ref,
                    so1_ref, sl1_ref, so2_ref, sl2_ref, h2_ref, *, tm, sub, group):
    ntiles = pl.num_programs(0) - 1

    @pl.when(pl.program_id(0) == ntiles)
    def _():
        xs_ref[...] = jnp.zeros_like(xs_ref)

    pl.when(pl.program_id(0) < ntiles)(functools.partial(
        _outproj_tile, o0_ref, l0_ref, o1_ref, l1_ref, o2_ref, l2_ref, ob_ref, sga_ref, sgb_ref, x_ref,
        mod_ref, g_ref, wa_ref, wb_ref, wo_ref, wr_ref, br_ref, x1_ref, xs_ref, route_ref, len_ref,
        so1_ref, sl1_ref, so2_ref, sl2_ref, h2_ref, tm=tm, sub=sub, group=group))


def _outproj_tile(o0_ref, l0_ref, o1_ref, l1_ref, o2_ref, l2_ref, ob_ref, sga_ref, sgb_ref, x_ref,
                  mod_ref, g_ref, wa_ref, wb_ref, wo_ref, wr_ref, br_ref,
                  x1_ref, xs_ref, route_ref, len_ref,
                  so1_ref, sl1_ref, so2_ref, sl2_ref, h2_ref, *, tm, sub, group):
    dm = x_ref.shape[1]
    for (o_ref, l_ref, so_ref, sl_ref, d) in ((o1_ref, l1_ref, so1_ref, sl1_ref, DILATIONS[1]),
                                              (o2_ref, l2_ref, so2_ref, sl2_ref, DILATIONS[2])):
        for r in range(d):
            for c in range(2):
                cs = slice(c * LANES, (c + 1) * LANES)
                so_ref[c, pl.ds(r, tm // d, stride=d), :] = o_ref[0, r, :, cs].astype(F32)
                sl_ref[c, pl.ds(r, tm // d, stride=d), :] = l_ref[0, r, :, cs]
    mr = lax.broadcasted_iota(I32, (LANES, LANES), 0)
    mc = lax.broadcasted_iota(I32, (LANES, LANES), 1)
    move_hi = jnp.where(((mr < 2) & (mc == 2 * mr)) | ((mr >= 2) & (mr < 4) & (mc == mr + 2)), 1.0, 0.0).astype(BF16)
    move_lo = jnp.where((mr < 2) & (mc == 2 * mr + 1), 1.0, 0.0).astype(BF16)
    for t in range(tm // sub):
        rs = slice(t * sub, (t + 1) * sub)
        both = lambda ref: jnp.concatenate([ref[0, rs, :], ref[1, rs, :]], axis=1)
        o0, l0 = o0_ref[0, 0, rs, :].astype(F32), l0_ref[0, 0, rs, :]
        o1, l1, o2, l2 = both(so1_ref), both(sl1_ref), both(so2_ref), both(sl2_ref)
        mx = jnp.maximum(jnp.maximum(l0, l1), l2)
        w0, w1, w2 = jnp.exp(l0 - mx), jnp.exp(l1 - mx), jnp.exp(l2 - mx)
        o_a = (w0 * o0 + w1 * o1 + w2 * o2) / (w0 + w1 + w2)
        y_a = jnp.dot(o_a.astype(BF16), wa_ref[...], preferred_element_type=F32)
        y_b = jnp.dot(ob_ref[rs, :], wb_ref[...], preferred_element_type=F32)
        merged = sga_ref[rs, :].astype(F32) * y_a + sgb_ref[rs, :].astype(F32) * y_b
        mix = jnp.dot(merged.astype(BF16), wo_ref[...], preferred_element_type=F32)
        x1 = x_ref[rs, :] + mod_ref[0, 2:3, :] * mix
        x1_ref[rs, :] = x1
        h2 = _rms_mod(x1, g_ref[...], mod_ref[0, 3:4, :], mod_ref[0, 4:5, :]).astype(BF16)
        h2_ref[rs, :dm] = h2
        logits = jnp.dot(h2, wr_ref[...], preferred_element_type=F32) + br_ref[...]
        rt = _route_rows(logits)
        route_ref[rs, :] = rt
        hi = rt.astype(BF16)
        lo = (rt - hi.astype(F32)).astype(BF16)
        aux = (jnp.dot(hi, move_hi, preferred_element_type=F32) + jnp.dot(lo, move_lo, preferred_element_type=F32))
        h2_ref[rs, dm:] = aux.astype(BF16)

    part = route_ref[...]
    e1, e2 = part[:, 2:3], part[:, 3:4]
    lane = lax.broadcasted_iota(I32, (tm, LANES), 1).astype(F32)
    onehot = jnp.where((lane == e1) | (lane == e2), 1.0, 0.0)
    rr = lax.broadcasted_iota(I32, (tm, tm), 0)
    cc = lax.broadcasted_iota(I32, (tm, tm), 1)
    tri = jnp.where(rr > cc, 1.0, 0.0).astype(BF16)
    prefix = jnp.dot(tri, onehot.astype(BF16), preferred_element_type=F32)
    cnt = jnp.sum(onehot, axis=0, keepdims=True)
    seg_len = jnp.ceil(cnt * (1.0 / SEG_ROWS)) * SEG_ROWS
    ur = lax.broadcasted_iota(I32, (LANES, LANES), 0)
    uc = lax.broadcasted_iota(I32, (LANES, LANES), 1)
    upper = jnp.where(ur < uc, 1.0, 0.0).astype(BF16)
    seg_off = jnp.dot(jnp.broadcast_to(seg_len, (8, LANES)).astype(BF16), upper, preferred_element_type=F32)[0:1, :]
    slot_map = seg_off + prefix
    s1 = jnp.sum(jnp.where(lane == e1, slot_map, 0.0), axis=-1, keepdims=True)
    s2 = jnp.sum(jnp.where(lane == e2, slot_map, 0.0), axis=-1, keepdims=True)
    len_ref[0] = jnp.broadcast_to(seg_len, (8, LANES))
    route = jnp.where(lane == 2, s1, jnp.where(lane == 3, s2, part))
    route_ref[...] = route

    pr = lax.broadcasted_iota(I32, (8, LANES), 0)
    pc = lax.broadcasted_iota(I32, (8, LANES), 1)
    lane_pick = jnp.where(pc == pr + 2, 1.0, 0.0)
    slots_t = lax.dot_general(lane_pick, route, (((1,), (1,)), ((), ())), preferred_element_type=F32,
                              precision=lax.Precision.HIGHEST)
    live = jnp.sum(seg_len).astype(I32)

    def sort_rows(lo, rows):
        srow = (lax.broadcasted_iota(I32, (rows, tm), 0) + lo).astype(F32)
        pick = jnp.where((srow == slots_t[0:1, :]) | (srow == slots_t[1:2, :]), 1.0, 0.0).astype(BF16)
        xs_ref[lo:lo + rows, :] = jnp.dot(pick, h2_ref[...], preferred_element_type=F32).astype(BF16)

    def clear_rows(lo, rows):
        xs_ref[lo:lo + rows, :] = jnp.zeros((rows, xs_ref.shape[1]), BF16)

    head = xs_ref.shape[0] - group
    sort_rows(0, head)
    pl.when(live > head)(functools.partial(sort_rows, head, group))
    pl.when(live <= head)(functools.partial(clear_rows, head, group))


def _outproj(dil_outs, ob, sga, sgb, x2, mod3, g_ffn, wa, wb, wo, wr, br, batch, seq, nslots):
    n, dm = x2.shape
    tm = MOE_TILE
    tpb = seq // tm
    ntiles = n // tm
    tile = lambda i: jnp.minimum(i, ntiles - 1)
    row = lambda i: (tile(i), 0)
    const = lambda i: (0, 0)
    in_specs = []
    args = []
    for (o, l), d in zip(dil_outs, DILATIONS):
        spec = pl.BlockSpec((1, d, tm // d, DIL_GROUP_W), lambda i: (tile(i) // tpb, 0, tile(i) % tpb, 0))
        in_specs += [spec, spec]
        args += [o, l]
    in_specs += [pl.BlockSpec((tm, ob.shape[1]), row), pl.BlockSpec((tm, dm), row), pl.BlockSpec((tm, dm), row),
                 pl.BlockSpec((tm, dm), row),
                 pl.BlockSpec((1, 6, dm), lambda i: (tile(i) // tpb, 0, 0)),
                 pl.BlockSpec((1, dm), const),
                 pl.BlockSpec(wa.shape, const), pl.BlockSpec(wb.shape, const), pl.BlockSpec(wo.shape, const),
                 pl.BlockSpec(wr.shape, const), pl.BlockSpec(br.shape, const)]
    args += [ob, sga, sgb, x2, mod3, g_ffn, wa, wb, wo, wr, br]
    width = dm + LANES
    return pl.pallas_call(
        functools.partial(_outproj_kernel, tm=tm, sub=512, group=SLOT_GROUP),
        grid=(ntiles + 1,),
        in_specs=in_specs,
        out_specs=[pl.BlockSpec((tm, dm), row), pl.BlockSpec((nslots, width), lambda i: (i, 0)),
                   pl.BlockSpec((tm, LANES), row), pl.BlockSpec((1, 8, LANES), lambda i: (tile(i), 0, 0))],
        out_shape=[jax.ShapeDtypeStruct((n, dm), F32), jax.ShapeDtypeStruct(((ntiles + 1) * nslots, width), BF16),
                   jax.ShapeDtypeStruct((n, LANES), F32), jax.ShapeDtypeStruct((ntiles, 8, LANES), F32)],
        scratch_shapes=[pltpu.VMEM((2, tm, LANES), F32)] * 4 + [pltpu.VMEM((tm, width), BF16)],
        compiler_params=_cparams(("arbitrary",)),
        name="outproj",
    )(*args)


def _wait_copies(count, copy):
    def wait_one(c, carry):
        copy.wait()
        return carry

    lax.fori_loop(0, count, wait_one, 0)


def _experts_kernel(first_ref, count_ref, chunk_ref, live_ref, wg_ref, wu_ref, wd_ref, xs_hbm, ys_hbm,
                    xbuf, ybuf, zbuf, xsem, ysem, zsem, *, nslots, ntiles):
    e = pl.program_id(0)
    last = pl.num_programs(0) - 1
    first = first_ref[e]
    count = count_ref[e]
    used = first_ref[last] + count_ref[last]
    dm = ybuf.shape[2]
    cpb = ROW_BLOCK // SEG_ROWS
    cpt = nslots // SEG_ROWS

    def chunk_rows(blk, j):
        return pl.ds(pl.multiple_of(chunk_ref[blk * cpb + j], SEG_ROWS), SEG_ROWS)

    def gather(blk):
        for j in range(cpb):
            pltpu.make_async_copy(xs_hbm.at[chunk_rows(blk, j), :], xbuf.at[blk & 1, pl.ds(j * SEG_ROWS, SEG_ROWS), :],
                                  xsem.at[blk & 1]).start(priority=1)

    def scatter(blk):
        for j in range(cpb):
            pltpu.make_async_copy(ybuf.at[blk & 1, pl.ds(j * SEG_ROWS, SEG_ROWS), :], ys_hbm.at[chunk_rows(blk, j), :],
                                  ysem.at[blk & 1]).start()

    def gather_done(blk):
        return pltpu.make_async_copy(xs_hbm.at[pl.ds(0, ROW_BLOCK), :], xbuf.at[blk & 1], xsem.at[blk & 1])

    def scatter_done(blk):
        return pltpu.make_async_copy(ybuf.at[blk & 1], ys_hbm.at[pl.ds(0, ROW_BLOCK), :], ysem.at[blk & 1])

    def zero_chunk(row):
        return pltpu.make_async_copy(zbuf.at[pl.ds(0, SEG_ROWS), :], ys_hbm.at[pl.ds(row, SEG_ROWS), :], zsem)

    def dead_chunks(t):
        return cpt - live_ref[t] // SEG_ROWS

    @pl.when(e == 0)
    def _():
        zbuf[...] = jnp.zeros_like(zbuf)
        for b in range(nslots // ROW_BLOCK):
            pltpu.make_async_copy(zbuf, ys_hbm.at[pl.ds(ntiles * nslots + b * ROW_BLOCK, ROW_BLOCK), :], zsem).start()
        for b in range(nslots // ROW_BLOCK):
            pltpu.make_async_copy(zbuf, ys_hbm.at[pl.ds(0, ROW_BLOCK), :], zsem).wait()

        def tile(t, carry):
            def chunk(c, carry2):
                zero_chunk(pl.multiple_of(t * nslots + c * SEG_ROWS, SEG_ROWS)).start()
                return carry2

            lax.fori_loop(live_ref[t] // SEG_ROWS, cpt, chunk, 0)
            return carry

        lax.fori_loop(0, ntiles, tile, 0)

        @pl.when(used > 0)
        def _():
            gather(0)

    def body(b, carry):
        blk = first + b
        slot = blk & 1
        gather_done(blk).wait()

        @pl.when(blk + 1 < used)
        def _():
            gather(blk + 1)

        @pl.when(blk >= 2)
        def _():
            scatter_done(blk - 2).wait()

        xb = xbuf[slot, :, :dm]
        aux = xbuf[slot, :, dm:].astype(F32)
        w = jnp.where(aux[:, 4:5] == e.astype(F32), aux[:, 0:1] + aux[:, 1:2], aux[:, 2:3] + aux[:, 3:4])
        g = jnp.dot(xb, wg_ref[0], preferred_element_type=F32)
        u = jnp.dot(xb, wu_ref[0], preferred_element_type=F32)
        a = (g * jax.nn.sigmoid(g)) * u
        y = jnp.dot(a.astype(BF16), wd_ref[0], preferred_element_type=F32)
        ybuf[slot] = (y * w).astype(BF16)
        scatter(blk)
        return carry

    lax.fori_loop(0, count, body, 0)

    @pl.when(e == last)
    def _():
        @pl.when(used >= 2)
        def _():
            scatter_done(used - 2).wait()

        @pl.when(used >= 1)
        def _():
            scatter_done(used - 1).wait()

        _wait_copies(lax.fori_loop(0, ntiles, lambda t, acc: acc + dead_chunks(t), 0), zero_chunk(0))


def _experts(first_blk, count_blk, chunk_tbl, live_tbl, x_sorted, w_gate, w_up, w_down, nslots, ntiles):
    rows, width = x_sorted.shape
    n_exp, dm, de = w_gate.shape
    wmap = lambda e, *_: (e, 0, 0)
    grid_spec = pltpu.PrefetchScalarGridSpec(
        num_scalar_prefetch=4,
        grid=(n_exp,),
        in_specs=[pl.BlockSpec((1, dm, de), wmap),
                  pl.BlockSpec((1, dm, de), wmap),
                  pl.BlockSpec((1, de, dm), wmap),
                  pl.BlockSpec(memory_space=pl.ANY)],
        out_specs=pl.BlockSpec(memory_space=pl.ANY),
        scratch_shapes=[pltpu.VMEM((2, ROW_BLOCK, width), BF16), pltpu.VMEM((2, ROW_BLOCK, dm), BF16),
                        pltpu.VMEM((ROW_BLOCK, dm), BF16),
                        pltpu.SemaphoreType.DMA((2,)), pltpu.SemaphoreType.DMA((2,)), pltpu.SemaphoreType.DMA(())],
    )
    return pl.pallas_call(
        functools.partial(_experts_kernel, nslots=nslots, ntiles=ntiles),
        grid_spec=grid_spec,
        out_shape=jax.ShapeDtypeStruct((rows, dm), BF16),
        compiler_params=_cparams(("arbitrary",)),
        name="experts",
    )(first_blk, count_blk, chunk_tbl, live_tbl, w_gate, w_up, w_down, x_sorted)


def _combine_kernel(live_ref, route_ref, x1_ref, mod_ref, g_ref, ys_ref, yt_ref, o_ref, *, tm, head):
    i = pl.program_id(0)
    route = route_ref[...]

    def one_hot(lo, rows):
        scol = (lax.broadcasted_iota(I32, (tm, rows), 1) + lo).astype(F32)
        return jnp.where((scol == route[:, 2:3]) | (scol == route[:, 3:4]), 1.0, 0.0).astype(BF16)

    def finish(with_tail):
        moe = jnp.dot(one_hot(0, head), ys_ref[...], preferred_element_type=F32)
        if with_tail:
            moe = moe + jnp.dot(one_hot(head, yt_ref.shape[0]), yt_ref[...], preferred_element_type=F32)
        x = x1_ref[...] + mod_ref[0, 5:6, :] * moe
        ms = jnp.mean(x * x, axis=-1, keepdims=True)
        o_ref[...] = (x * lax.rsqrt(ms + RMS_EPS)) * g_ref[...]

    pl.when(live_ref[i] > head)(functools.partial(finish, True))
    pl.when(live_ref[i] <= head)(functools.partial(finish, False))


def _combine(live_tbl, route, x1, mod3, g_final, y_sorted, seq, tm, nslots):
    n, dm = x1.shape
    tpb = seq // tm
    ntiles = n // tm
    head = nslots - SLOT_GROUP
    grid_spec = pltpu.PrefetchScalarGridSpec(
        num_scalar_prefetch=1,
        grid=(n // tm,),
        in_specs=[pl.BlockSpec((tm, LANES), lambda i, *_: (i, 0)),
                  pl.BlockSpec((tm, dm), lambda i, *_: (i, 0)),
                  pl.BlockSpec((1, 6, dm), lambda i, *_: (i // tpb, 0, 0)),
                  pl.BlockSpec((1, dm), lambda i, *_: (0, 0)),
                  pl.BlockSpec((None, head, dm), lambda i, live: (i, 0, 0)),
                  pl.BlockSpec((None, SLOT_GROUP, dm),
                               lambda i, live: (jnp.where(live[i] > head, i, ntiles), head // SLOT_GROUP, 0))],
        out_specs=pl.BlockSpec((tm, dm), lambda i, *_: (i, 0)),
    )
    y3 = y_sorted.reshape(ntiles + 1, nslots, dm)
    return pl.pallas_call(
        functools.partial(_combine_kernel, tm=tm, head=head),
        grid_spec=grid_spec,
        out_shape=jax.ShapeDtypeStruct((n, dm), F32),
        compiler_params=_cparams(("arbitrary",)),
        name="combine",
    )(live_tbl, route, x1, mod3, g_final, y3, y3)


def _rope_tables(positions):
    half = HEAD_DIM // 2
    inv_freq = ROPE_THETA ** (-jnp.arange(half, dtype=F32) * (2.0 / HEAD_DIM))
    freq = jnp.tile(inv_freq, LANES // half)
    sign = jnp.tile(jnp.concatenate([-jnp.ones((half,), F32), jnp.ones((half,), F32)]), LANES // HEAD_DIM)
    ang = positions.astype(F32).reshape(-1, 1) * freq
    return jnp.cos(ang), jnp.sin(ang) * sign


def _expert_chunk_table(lens, pstart, tot, pend, nblk, nslots):
    ntiles = lens.shape[0]
    cpb = ROW_BLOCK // SEG_ROWS
    runs = lens // SEG_ROWS
    q = jnp.arange(nblk * cpb, dtype=I32)
    owner = jnp.minimum(jnp.sum((pend[None, :] // SEG_ROWS <= q[:, None]).astype(I32), axis=1), N_EXPERTS - 1)
    is_e = owner[:, None] == jnp.arange(N_EXPERTS, dtype=I32)[None, :]
    pick_e = lambda v: jnp.sum(jnp.where(is_e, v[None, :], 0), axis=1)
    off = q - pick_e(pstart // SEG_ROWS)
    in_run = (off < pick_e(tot // SEG_ROWS)) & (q < pend[-1] // SEG_ROWS)
    upto = jnp.cumsum(runs, axis=0)
    upto_e = jnp.sum(jnp.where(is_e[:, None, :], upto[None, :, :], 0), axis=2)
    tile = jnp.minimum(jnp.sum((upto_e <= off[:, None]).astype(I32), axis=1), ntiles - 1)
    is_t = tile[:, None] == jnp.arange(ntiles, dtype=I32)[None, :]
    run_slot0 = jnp.cumsum(runs, axis=1) - runs - (upto - runs)
    slot0 = jnp.sum(jnp.where(is_t[:, :, None] & is_e[:, None, :], run_slot0[None, :, :], 0), axis=(1, 2))
    live_row = tile * nslots + (slot0 + off) * SEG_ROWS
    spare_row = ntiles * nslots + (((q // cpb) % 2) * cpb + q % cpb) * SEG_ROWS
    return jnp.where(in_run, live_row, spare_row)


def kernel(x, c, positions, w_ada, b_ada, g_mix, w_in, sink_logits, w_branch_a, w_branch_b, w_out, g_ffn,
           w_group, b_group, w_route, b_route, w_expert_gate, w_expert_up, w_expert_down, g_final):
    batch, seq, dm = x.shape
    n = batch * seq
    assert w_ada.shape[0] == 1, "one layer"
    x2 = x.reshape(n, dm)

    c8 = jnp.pad(c, ((0, 8 - batch), (0, 0)))
    mod = _ada(c8, w_ada[0], b_ada[0].reshape(1, -1))
    mod3 = mod[:batch].reshape(batch, 6, dm)

    cos_t, sin_t = _rope_tables(positions)
    outs = _inproj(x2, mod3, g_mix[0].reshape(1, dm), cos_t, sin_t, w_in[0].astype(BF16),
                   (w_expert_gate[0], w_expert_up[0], w_expert_down[0]), batch, seq)
    qkv = outs[:9]
    qb, kb, vb, sga, sgb = outs[9:14]
    expert_w_bf = outs[14:]

    dil_outs = [_dil_attention(qkv[3 * g], qkv[3 * g + 1], qkv[3 * g + 2]) for g in range(len(DILATIONS))]
    ob = _swa_attention(sink_logits[0], qb.reshape(batch, seq, -1), kb.reshape(batch, seq, -1),
                        vb.reshape(batch, seq, -1)).reshape(n, -1)

    pad = LANES - N_GROUPS - N_EXPERTS
    wr = jnp.concatenate([w_group[0], w_route[0], jnp.zeros((dm, pad), F32)], axis=1).astype(BF16)
    br = jnp.concatenate([b_group[0], b_route[0], jnp.zeros((pad,), F32)]).reshape(1, LANES)
    ntiles = n // MOE_TILE
    nslots = 2 * MOE_TILE + N_EXPERTS * SEG_ROWS
    x1, x_sorted, route, seg_lens = _outproj(dil_outs, ob, sga, sgb, x2, mod3, g_ffn[0].reshape(1, dm),
                                             w_branch_a[0].astype(BF16), w_branch_b[0].astype(BF16),
                                             w_out[0].astype(BF16), wr, br, batch, seq, nslots)

    nblk = -(-(2 * n + ntiles * N_EXPERTS * (SEG_ROWS - 1)) // ROW_BLOCK) + N_EXPERTS
    lens = seg_lens[:, 0, :N_EXPERTS].astype(I32)
    tot = jnp.sum(lens, axis=0)
    padded = (tot + ROW_BLOCK - 1) // ROW_BLOCK * ROW_BLOCK
    pend = jnp.cumsum(padded)
    pstart = pend - padded
    chunk_tbl = _expert_chunk_table(lens, pstart, tot, pend, nblk, nslots)
    live_tbl = jnp.sum(lens, axis=1)

    y_sorted = _experts(pstart // ROW_BLOCK, padded // ROW_BLOCK, chunk_tbl, live_tbl, x_sorted,
                        *expert_w_bf, nslots, ntiles)
    out = _combine(live_tbl, route, x1, mod3, g_final.reshape(1, dm), y_sorted, seq, MOE_TILE, nslots)
    return out.reshape(batch, seq, dm)
```

```python
import functools

import jax
import jax.numpy as jnp
from jax import lax
from jax.experimental import pallas as pl
from jax.experimental.pallas import tpu as pltpu

F32 = jnp.float32
BF16 = jnp.bfloat16
I32 = jnp.int32

HEAD_DIM = 64
ROPE_THETA = 10000.0
RMS_EPS = 1e-6
NEG_INF = -1e30
Q_SCALE = HEAD_DIM ** -0.5
DILATIONS = (1, 4, 16)
DIL_HALF_WINDOW = 64
DIL_GROUP_W = 256
SWA_WINDOW = 128
N_GROUPS = 4
EXPERTS_PER_GROUP = 8
N_EXPERTS = 32
LANES = 128
ROW_BLOCK = 512
SEG_ROWS = 16
MOE_TILE = 512
SLOT_GROUP = 256
VMEM_LIMIT = 56 * 1024 * 1024


def _cparams(sem):
    return pltpu.CompilerParams(dimension_semantics=sem, vmem_limit_bytes=VMEM_LIMIT)


def _ada_kernel(c_ref, w_ref, b_ref, o_ref):
    c = c_ref[...]
    cs = c * jax.nn.sigmoid(c)
    o_ref[...] = jnp.dot(cs.astype(BF16), w_ref[...].astype(BF16), preferred_element_type=F32) + b_ref[...]


def _ada(c8, w_ada, b_ada):
    d, n = w_ada.shape
    tn = 1536
    return pl.pallas_call(
        _ada_kernel,
        grid=(n // tn,),
        in_specs=[pl.BlockSpec((8, d), lambda j: (0, 0)),
                  pl.BlockSpec((d, tn), lambda j: (0, j)),
                  pl.BlockSpec((1, tn), lambda j: (0, j))],
        out_specs=pl.BlockSpec((8, tn), lambda j: (0, j)),
        out_shape=jax.ShapeDtypeStruct((8, n), F32),
        compiler_params=_cparams(("arbitrary",)),
        name="ada",
    )(c8, w_ada, b_ada)


def _rms_mod(x, g, shift, scale):
    ms = jnp.mean(x * x, axis=-1, keepdims=True)
    return (x * lax.rsqrt(ms + RMS_EPS)) * (g * (1.0 + scale)) + shift


def _inproj_kernel(x_ref, mod_ref, g_ref, ang_ref, sign_ref, w_ref, eg_ref, eu_ref, ed_ref,
                   q0_ref, k0_ref, v0_ref, q1_ref, k1_ref, v1_ref, q2_ref, k2_ref, v2_ref,
                   qb_ref, kb_ref, vb_ref, sga_ref, sgb_ref, egb_ref, eub_ref, edb_ref, stg_ref, *, tm, n_exp):
    def cast_expert():
        egb_ref[...] = eg_ref[...].astype(BF16)
        eub_ref[...] = eu_ref[...].astype(BF16)
        edb_ref[...] = ed_ref[...].astype(BF16)

    if n_exp is None:
        cast_expert()
    else:
        pl.when(pl.program_id(0) < n_exp)(cast_expert)

    h = _rms_mod(x_ref[...], g_ref[...], mod_ref[0, 0:1, :], mod_ref[0, 1:2, :])
    hb = h.astype(BF16)
    ang = ang_ref[...]
    cos = jnp.cos(ang)
    sin = jnp.sin(ang) * sign_ref[...]
    lane = lax.broadcasted_iota(I32, (tm, LANES), 1)
    first_half = (lane & 32) == 0
    low = lane < 64

    def proj(c0, width):
        return jnp.dot(hb, w_ref[:, c0:c0 + width], preferred_element_type=F32)

    def rope(t):
        rot = jnp.where(first_half, pltpu.roll(t, 96, 1), pltpu.roll(t, 32, 1))
        return t * cos + rot * sin

    def rope256(p):
        return jnp.concatenate([rope(p[:, :LANES]), rope(p[:, LANES:])], axis=1)

    def store_group(ref, val, d):
        if d == 1:
            ref[0, 0] = val.astype(BF16)
        else:
            for c in range(2):
                stg_ref[c] = val[:, c * LANES:(c + 1) * LANES]
            for r in range(d):
                for c in range(2):
                    ref[0, r, :, c * LANES:(c + 1) * LANES] = (
                        stg_ref[c, pl.ds(r, tm // d, stride=d), :].astype(BF16))

    q_refs = (q0_ref, q1_ref, q2_ref)
    k_refs = (k0_ref, k1_ref, k2_ref)
    v_refs = (v0_ref, v1_ref, v2_ref)
    for g, d in enumerate(DILATIONS):
        store_group(q_refs[g], rope256(proj(g * 256, 256)) * Q_SCALE, d)
        store_group(k_refs[g], rope256(proj(768 + g * 256, 256)), d)
        store_group(v_refs[g], proj(1536 + g * 256, 256), d)
    for j in range(2):
        qb_ref[:, j * 256:(j + 1) * 256] = (rope256(proj(2304 + j * 256, 256)) * Q_SCALE).astype(BF16)
    kv = proj(2816, 256)
    kb = rope(kv[:, :LANES])
    vb = kv[:, LANES:]
    kb_sw = pltpu.roll(kb, 64, 1)
    vb_sw = pltpu.roll(vb, 64, 1)
    kb_ref[:, :LANES] = jnp.where(low, kb, kb_sw).astype(BF16)
    kb_ref[:, LANES:] = jnp.where(low, kb_sw, kb).astype(BF16)
    vb_ref[:, :LANES] = jnp.where(low, vb, vb_sw).astype(BF16)
    vb_ref[:, LANES:] = jnp.where(low, vb_sw, vb).astype(BF16)
    for j in range(4):
        sga_ref[:, j * 256:(j + 1) * 256] = jax.nn.sigmoid(proj(3072 + j * 256, 256)).astype(BF16)
        sgb_ref[:, j * 256:(j + 1) * 256] = jax.nn.sigmoid(proj(4096 + j * 256, 256)).astype(BF16)


def _inproj(x2, mod3, g_mix, angles, sign, w_in_bf, expert_w, batch, seq):
    n, dm = x2.shape
    tm = 512
    tpb = seq // tm
    grid = (n // tm,)
    row = lambda i: (i, 0)
    strided_specs, strided_shapes = [], []
    for d in DILATIONS:
        for _ in range(3):
            strided_specs.append(pl.BlockSpec((1, d, tm // d, DIL_GROUP_W), lambda i: (i // tpb, 0, i % tpb, 0)))
            strided_shapes.append(jax.ShapeDtypeStruct((batch, d, seq // d, DIL_GROUP_W), BF16))
    out_specs = strided_specs + [
        pl.BlockSpec((tm, 512), row), pl.BlockSpec((tm, 256), row), pl.BlockSpec((tm, 256), row),
        pl.BlockSpec((tm, dm), row), pl.BlockSpec((tm, dm), row)]
    out_shapes = strided_shapes + [
        jax.ShapeDtypeStruct((n, 512), BF16), jax.ShapeDtypeStruct((n, 256), BF16),
        jax.ShapeDtypeStruct((n, 256), BF16), jax.ShapeDtypeStruct((n, dm), BF16),
        jax.ShapeDtypeStruct((n, dm), BF16)]
    n_exp = expert_w[0].shape[0]
    assert n // tm >= n_exp, "one expert's weights are cast per grid step"
    emap = lambda i: (jnp.minimum(i, n_exp - 1), 0, 0)
    expert_specs = [pl.BlockSpec((1,) + w.shape[1:], emap) for w in expert_w]
    return pl.pallas_call(
        functools.partial(_inproj_kernel, tm=tm, n_exp=None if n // tm == n_exp else n_exp),
        grid=grid,
        in_specs=[pl.BlockSpec((tm, dm), row),
                  pl.BlockSpec((1, 6, dm), lambda i: (i // tpb, 0, 0)),
                  pl.BlockSpec((1, dm), lambda i: (0, 0)),
                  pl.BlockSpec((tm, LANES), row),
                  pl.BlockSpec((1, LANES), lambda i: (0, 0)),
                  pl.BlockSpec(w_in_bf.shape, lambda i: (0, 0), pipeline_mode=pl.Buffered(1))] + expert_specs,
        out_specs=out_specs + expert_specs,
        out_shape=out_shapes + [jax.ShapeDtypeStruct(w.shape, BF16) for w in expert_w],
        scratch_shapes=[pltpu.VMEM((2, tm, LANES), F32)],
        compiler_params=_cparams(("arbitrary",)),
        name="inproj",
    )(x2, mod3, g_mix, angles, sign, w_in_bf, *expert_w)


def _split_heads(q2, low):
    zero = jnp.zeros_like(q2)
    return jnp.concatenate([jnp.where(low, q2, zero), jnp.where(low, zero, q2)], axis=0)


def _band_softmax(qst, k2, v2, mask, sinks):
    s = lax.dot_general(qst, k2, (((1,), (1,)), ((), ())), preferred_element_type=F32)
    s = jnp.where(mask, s, NEG_INF)
    rows, tk = s.shape
    m = jnp.max(s, axis=-1, keepdims=True)
    if sinks is not None:
        seg = rows // len(sinks)
        m = jnp.concatenate([jnp.maximum(m[h * seg:(h + 1) * seg], sk) for h, sk in enumerate(sinks)], axis=0)
    m = jnp.broadcast_to(m, (rows, LANES))
    e = jnp.concatenate([jnp.exp(s[:, c * LANES:(c + 1) * LANES] - m) for c in range(tk // LANES)], axis=1)
    v_ones = jnp.concatenate([v2, jnp.ones((tk, LANES), BF16)], axis=1)
    od = jnp.dot(e.astype(BF16), v_ones, preferred_element_type=F32)
    o, den = od[:, :LANES], od[:, LANES:]
    if sinks is not None:
        den = jnp.concatenate([den[h * seg:(h + 1) * seg] + jnp.exp(sk - m[h * seg:(h + 1) * seg])
                               for h, sk in enumerate(sinks)], axis=0)
    return o / den, m, den


def _band_mask(qs, ks, nstack, tq, tk, window):
    row = lax.broadcasted_iota(I32, (nstack * tq, tk), 0) & (tq - 1)
    col = lax.broadcasted_iota(I32, (nstack * tq, tk), 1)
    return jnp.abs((ks + col) - (qs + row)) <= window


def _dil_kernel(q_ref, k_ref, v_ref, o_ref, l_ref, *, length, tq, tk):
    low = lax.broadcasted_iota(I32, (tq, LANES), 1) < 64

    nq = length // tq

    def body(j, carry):
        r = j // nq
        qs = pl.multiple_of((j % nq) * tq, tq)
        ks = pl.multiple_of(jnp.clip(qs - DIL_HALF_WINDOW, 0, length - tk), DIL_HALF_WINDOW)
        mask = _band_mask(qs, ks, 2, tq, tk, DIL_HALF_WINDOW)
        for c in range(DIL_GROUP_W // LANES):
            cs = slice(c * LANES, (c + 1) * LANES)
            qst = _split_heads(q_ref[r, pl.ds(qs, tq), cs], low)
            o, m, den = _band_softmax(qst, k_ref[r, pl.ds(ks, tk), cs], v_ref[r, pl.ds(ks, tk), cs], mask, None)
            lse = m + jnp.log(den)
            o_ref[r, pl.ds(qs, tq), cs] = jnp.where(low, o[:tq], o[tq:]).astype(BF16)
            l_ref[r, pl.ds(qs, tq), cs] = jnp.where(low, lse[:tq], lse[tq:])
        return carry

    lax.fori_loop(0, q_ref.shape[0] * nq, body, 0, unroll=8)


def _dil_attention(q, k, v):
    batch, d, length, w = q.shape
    tq, tk = 128, 256
    spec = pl.BlockSpec((None, d, length, w), lambda b: (b, 0, 0, 0))
    return pl.pallas_call(
        functools.partial(_dil_kernel, length=length, tq=tq, tk=tk),
        grid=(batch,),
        in_specs=[spec, spec, spec],
        out_specs=[spec, spec],
        out_shape=[jax.ShapeDtypeStruct(q.shape, BF16), jax.ShapeDtypeStruct(q.shape, F32)],
        compiler_params=_cparams(("arbitrary",)),
        name=f"dil{d}",
    )(q, k, v)


def _swa_kernel(sink_ref, q_ref, k_ref, v_ref, o_ref, *, length, tq, tk):
    low = lax.broadcasted_iota(I32, (tq, LANES), 1) < 64
    nblk = q_ref.shape[1] // LANES

    def body(j, carry):
        qs = pl.multiple_of(j * tq, tq)
        ks = pl.multiple_of(jnp.clip(qs - SWA_WINDOW, 0, length - tk), SWA_WINDOW)
        mask = _band_mask(qs, ks, 2, tq, tk, SWA_WINDOW)
        for b in range(nblk):
            cs = slice((b // 2) * LANES, (b // 2 + 1) * LANES)
            bs = slice(b * LANES, (b + 1) * LANES)
            qst = _split_heads(q_ref[pl.ds(qs, tq), bs], low)
            sinks = (sink_ref[2 * b], sink_ref[2 * b + 1])
            o, _, _ = _band_softmax(qst, k_ref[pl.ds(ks, tk), cs], v_ref[pl.ds(ks, tk), cs], mask, sinks)
            o_ref[pl.ds(qs, tq), bs] = jnp.where(low, o[:tq], o[tq:]).astype(BF16)
        return carry

    lax.fori_loop(0, length // tq, body, 0, unroll=8)


def _swa_attention(sink, q, k, v):
    batch, length, qw = q.shape
    tq, tk = 128, 384
    return pl.pallas_call(
        functools.partial(_swa_kernel, length=length, tq=tq, tk=tk),
        grid=(batch,),
        in_specs=[pl.BlockSpec(memory_space=pltpu.SMEM),
                  pl.BlockSpec((None, length, qw), lambda b: (b, 0, 0)),
                  pl.BlockSpec((None, length, k.shape[2]), lambda b: (b, 0, 0)),
                  pl.BlockSpec((None, length, v.shape[2]), lambda b: (b, 0, 0))],
        out_specs=pl.BlockSpec((None, length, qw), lambda b: (b, 0, 0)),
        out_shape=jax.ShapeDtypeStruct(q.shape, BF16),
        compiler_params=_cparams(("arbitrary",)),
        name="swa",
    )(sink, q, k, v)


def _route_rows(logits):
    lane = lax.broadcasted_iota(I32, logits.shape, 1).astype(F32)
    big = 1e9
    is_g = lane < N_GROUPS
    gl = jnp.where(is_g, logits, NEG_INF)
    gmax = jnp.max(gl, axis=-1, keepdims=True)
    gsel = jnp.min(jnp.where(is_g & (gl == gmax), lane, big), axis=-1, keepdims=True)
    gw = 1.0 / jnp.sum(jnp.where(is_g, jnp.exp(gl - gmax), 0.0), axis=-1, keepdims=True)
    e_lo = N_GROUPS + gsel * EXPERTS_PER_GROUP
    in_grp = (lane >= e_lo) & (lane < e_lo + EXPERTS_PER_GROUP)
    el = jnp.where(in_grp, logits, NEG_INF)
    m1 = jnp.max(el, axis=-1, keepdims=True)
    i1 = jnp.min(jnp.where(in_grp & (el == m1), lane, big), axis=-1, keepdims=True)
    el2 = jnp.where(lane == i1, NEG_INF, el)
    m2 = jnp.max(el2, axis=-1, keepdims=True)
    i2 = jnp.min(jnp.where(in_grp & (lane != i1) & (el2 == m2), lane, big), axis=-1, keepdims=True)
    t = jnp.exp(m2 - m1)
    tw1 = gw / (1.0 + t)
    tw2 = gw * t / (1.0 + t)
    out = jnp.where(lane == 0, tw1, 0.0)
    out = jnp.where(lane == 1, tw2, out)
    out = jnp.where(lane == 2, i1 - N_GROUPS, out)
    return jnp.where(lane == 3, i2 - N_GROUPS, out)


def _outproj_kernel(o0_ref, l0_ref, o1_ref, l1_ref, o2_ref, l2_ref, ob_ref, sga_ref, sgb_ref, x_ref,
                    mod_ref, g_ref, wa_ref, wb_ref, wo_ref, wr_ref, br_ref,
                    x1_ref, xs_ref, route_ref, len_ref,
                    so1_ref, sl1_ref, so2_ref, sl2_ref, h2_ref, *, tm, sub, group):
    ntiles = pl.num_programs(0) - 1

    @pl.when(pl.program_id(0) == ntiles)
    def _():
        xs_ref[...] = jnp.zeros_like(xs_ref)

    pl.when(pl.program_id(0) < ntiles)(functools.partial(
        _outproj_tile, o0_ref, l0_ref, o1_ref, l1_ref, o2_ref, l2_ref, ob_ref, sga_ref, sgb_ref, x_ref,
        mod_ref, g_ref, wa_ref, wb_ref, wo_ref, wr_ref, br_ref, x1_ref, xs_ref, route_ref, len_ref,
        so1_ref, sl1_ref, so2_ref, sl2_ref, h2_ref, tm=tm, sub=sub, group=group))


def _outproj_tile(o0_ref, l0_ref, o1_ref, l1_ref, o2_ref, l2_ref, ob_ref, sga_ref, sgb_ref, x_ref,
                  mod_ref, g_ref, wa_ref, wb_ref, wo_ref, wr_ref, br_ref,
                  x1_ref, xs_ref, route_ref, len_ref,
                  so1_ref, sl1_ref, so2_ref, sl2_ref, h2_ref, *, tm, sub, group):
    dm = x_ref.shape[1]
    for (o_ref, l_ref, so_ref, sl_ref, d) in ((o1_ref, l1_ref, so1_ref, sl1_ref, DILATIONS[1]),
                                              (o2_ref, l2_ref, so2_ref, sl2_ref, DILATIONS[2])):
        for r in range(d):
            for c in range(2):
                cs = slice(c * LANES, (c + 1) * LANES)
                so_ref[c, pl.ds(r, tm // d, stride=d), :] = o_ref[0, r, :, cs].astype(F32)
                sl_ref[c, pl.ds(r, tm // d, stride=d), :] = l_ref[0, r, :, cs]
    mr = lax.broadcasted_iota(I32, (LANES, LANES), 0)
    mc = lax.broadcasted_iota(I32, (LANES, LANES), 1)
    move_hi = jnp.where(((mr < 2) & (mc == 2 * mr)) | ((mr >= 2) & (mr < 4) & (mc == mr + 2)), 1.0, 0.0).astype(BF16)
    move_lo = jnp.where((mr < 2) & (mc == 2 * mr + 1), 1.0, 0.0).astype(BF16)
    for t in range(tm // sub):
        rs = slice(t * sub, (t + 1) * sub)
        both = lambda ref: jnp.concatenate([ref[0, rs, :], ref[1, rs, :]], axis=1)
        o0, l0 = o0_ref[0, 0, rs, :].astype(F32), l0_ref[0, 0, rs, :]
        o1, l1, o2, l2 = both(so1_ref), both(sl1_ref), both(so2_ref), both(sl2_ref)
        mx = jnp.maximum(jnp.maximum(l0, l1), l2)
        w0, w1, w2 = jnp.exp(l0 - mx), jnp.exp(l1 - mx), jnp.exp(l2 - mx)
        o_a = (w0 * o0 + w1 * o1 + w2 * o2) / (w0 + w1 + w2)
        y_a = jnp.dot(o_a.astype(BF16), wa_ref[...], preferred_element_type=F32)
        y_b = jnp.dot(ob_ref[rs, :], wb_ref[...], preferred_element_type=F32)
        merged = sga_ref[rs, :].astype(F32) * y_a + sgb_ref[rs, :].astype(F32) * y_b
        mix = jnp.dot(merged.astype(BF16), wo_ref[...], preferred_element_type=F32)
        x1 = x_ref[rs, :] + mod_ref[0, 2:3, :] * mix
        x1_ref[rs, :] = x1
        h2 = _rms_mod(x1, g_ref[...], mod_ref[0, 3:4, :], mod_ref[0, 4:5, :]).astype(BF16)
        h2_ref[rs, :dm] = h2
        logits = jnp.dot(h2, wr_ref[...], preferred_element_type=F32) + br_ref[...]
        rt = _route_rows(logits)
        route_ref[rs, :] = rt
        hi = rt.astype(BF16)
        lo = (rt - hi.astype(F32)).astype(BF16)
        aux = (jnp.dot(hi, move_hi, preferred_element_type=F32) + jnp.dot(lo, move_lo, preferred_element_type=F32))
        h2_ref[rs, dm:] = aux.astype(BF16)

    part = route_ref[...]
    e1, e2 = part[:, 2:3], part[:, 3:4]
    lane = lax.broadcasted_iota(I32, (tm, LANES), 1).astype(F32)
    onehot = jnp.where((lane == e1) | (lane == e2), 1.0, 0.0)
    rr = lax.broadcasted_iota(I32, (tm, tm), 0)
    cc = lax.broadcasted_iota(I32, (tm, tm), 1)
    tri = jnp.where(rr > cc, 1.0, 0.0).astype(BF16)
    prefix = jnp.dot(tri, onehot.astype(BF16), preferred_element_type=F32)
    cnt = jnp.sum(onehot, axis=0, keepdims=True)
    seg_len = jnp.ceil(cnt * (1.0 / SEG_ROWS)) * SEG_ROWS
    ur = lax.broadcasted_iota(I32, (LANES, LANES), 0)
    uc = lax.broadcasted_iota(I32, (LANES, LANES), 1)
    upper = jnp.where(ur < uc, 1.0, 0.0).astype(BF16)
    seg_off = jnp.dot(jnp.broadcast_to(seg_len, (8, LANES)).astype(BF16), upper, preferred_element_type=F32)[0:1, :]
    slot_map = seg_off + prefix
    s1 = jnp.sum(jnp.where(lane == e1, slot_map, 0.0), axis=-1, keepdims=True)
    s2 = jnp.sum(jnp.where(lane == e2, slot_map, 0.0), axis=-1, keepdims=True)
    len_ref[0] = jnp.broadcast_to(seg_len, (8, LANES))
    route = jnp.where(lane == 2, s1, jnp.where(lane == 3, s2, part))
    route_ref[...] = route

    pr = lax.broadcasted_iota(I32, (8, LANES), 0)
    pc = lax.broadcasted_iota(I32, (8, LANES), 1)
    lane_pick = jnp.where(pc == pr + 2, 1.0, 0.0)
    slots_t = lax.dot_general(lane_pick, route, (((1,), (1,)), ((), ())), preferred_element_type=F32,
                              precision=lax.Precision.HIGHEST)
    live = jnp.sum(seg_len).astype(I32)

    def sort_rows(lo, rows):
        srow = (lax.broadcasted_iota(I32, (rows, tm), 0) + lo).astype(F32)
        pick = jnp.where((srow == slots_t[0:1, :]) | (srow == slots_t[1:2, :]), 1.0, 0.0).astype(BF16)
        xs_ref[lo:lo + rows, :] = jnp.dot(pick, h2_ref[...], preferred_element_type=F32).astype(BF16)

    def clear_rows(lo, rows):
        xs_ref[lo:lo + rows, :] = jnp.zeros((rows, xs_ref.shape[1]), BF16)

    head = xs_ref.shape[0] - group
    sort_rows(0, head)
    pl.when(live > head)(functools.partial(sort_rows, head, group))
    pl.when(live <= head)(functools.partial(clear_rows, head, group))


def _outproj(dil_outs, ob, sga, sgb, x2, mod3, g_ffn, wa, wb, wo, wr, br, batch, seq, nslots):
    n, dm = x2.shape
    tm = MOE_TILE
    tpb = seq // tm
    ntiles = n // tm
    tile = lambda i: jnp.minimum(i, ntiles - 1)
    row = lambda i: (tile(i), 0)
    const = lambda i: (0, 0)
    in_specs = []
    args = []
    for (o, l), d in zip(dil_outs, DILATIONS):
        spec = pl.BlockSpec((1, d, tm // d, DIL_GROUP_W), lambda i: (tile(i) // tpb, 0, tile(i) % tpb, 0))
        in_specs += [spec, spec]
        args += [o, l]
    in_specs += [pl.BlockSpec((tm, ob.shape[1]), row), pl.BlockSpec((tm, dm), row), pl.BlockSpec((tm, dm), row),
                 pl.BlockSpec((tm, dm), row),
                 pl.BlockSpec((1, 6, dm), lambda i: (tile(i) // tpb, 0, 0)),
                 pl.BlockSpec((1, dm), const),
                 pl.BlockSpec(wa.shape, const), pl.BlockSpec(wb.shape, const), pl.BlockSpec(wo.shape, const),
                 pl.BlockSpec(wr.shape, const), pl.BlockSpec(br.shape, const)]
    args += [ob, sga, sgb, x2, mod3, g_ffn, wa, wb, wo, wr, br]
    width = dm + LANES
    return pl.pallas_call(
        functools.partial(_outproj_kernel, tm=tm, sub=512, group=SLOT_GROUP),
        grid=(ntiles + 1,),
        in_specs=in_specs,
        out_specs=[pl.BlockSpec((tm, dm), row), pl.BlockSpec((nslots, width), lambda i: (i, 0)),
                   pl.BlockSpec((tm, LANES), row), pl.BlockSpec((1, 8, LANES), lambda i: (tile(i), 0, 0))],
        out_shape=[jax.ShapeDtypeStruct((n, dm), F32), jax.ShapeDtypeStruct(((ntiles + 1) * nslots, width), BF16),
                   jax.ShapeDtypeStruct((n, LANES), F32), jax.ShapeDtypeStruct((ntiles, 8, LANES), F32)],
        scratch_shapes=[pltpu.VMEM((2, tm, LANES), F32)] * 4 + [pltpu.VMEM((tm, width), BF16)],
        compiler_params=_cparams(("arbitrary",)),
        name="outproj",
    )(*args)


def _wait_copies(count, copy):
    def wait_one(c, carry):
        copy.wait()
        return carry

    lax.fori_loop(0, count, wait_one, 0)


def _experts_kernel(first_ref, count_ref, chunk_ref, live_ref, wg_ref, wu_ref, wd_ref, xs_hbm, ys_hbm,
                    xbuf, ybuf, zbuf, xsem, ysem, zsem, *, nslots, ntiles):
    e = pl.program_id(0)
    last = pl.num_programs(0) - 1
    first = first_ref[e]
    count = count_ref[e]
    used = first_ref[last] + count_ref[last]
    dm = ybuf.shape[2]
    cpb = ROW_BLOCK // SEG_ROWS
    cpt = nslots // SEG_ROWS

    def chunk_rows(blk, j):
        return pl.ds(pl.multiple_of(chunk_ref[blk * cpb + j], SEG_ROWS), SEG_ROWS)

    def gather(blk):
        for j in range(cpb):
            pltpu.make_async_copy(xs_hbm.at[chunk_rows(blk, j), :], xbuf.at[blk & 1, pl.ds(j * SEG_ROWS, SEG_ROWS), :],
                                  xsem.at[blk & 1]).start(priority=1)

    def scatter(blk):
        for j in range(cpb):
            pltpu.make_async_copy(ybuf.at[blk & 1, pl.ds(j * SEG_ROWS, SEG_ROWS), :], ys_hbm.at[chunk_rows(blk, j), :],
                                  ysem.at[blk & 1]).start()

    def gather_done(blk):
        return pltpu.make_async_copy(xs_hbm.at[pl.ds(0, ROW_BLOCK), :], xbuf.at[blk & 1], xsem.at[blk & 1])

    def scatter_done(blk):
        return pltpu.make_async_copy(ybuf.at[blk & 1], ys_hbm.at[pl.ds(0, ROW_BLOCK), :], ysem.at[blk & 1])

    def zero_chunk(row):
        return pltpu.make_async_copy(zbuf.at[pl.ds(0, SEG_ROWS), :], ys_hbm.at[pl.ds(row, SEG_ROWS), :], zsem)

    def dead_chunks(t):
        return cpt - live_ref[t] // SEG_ROWS

    @pl.when(e == 0)
    def _():
        zbuf[...] = jnp.zeros_like(zbuf)
        for b in range(nslots // ROW_BLOCK):
            pltpu.make_async_copy(zbuf, ys_hbm.at[pl.ds(ntiles * nslots + b * ROW_BLOCK, ROW_BLOCK), :], zsem).start()
        for b in range(nslots // ROW_BLOCK):
            pltpu.make_async_copy(zbuf, ys_hbm.at[pl.ds(0, ROW_BLOCK), :], zsem).wait()

        def tile(t, carry):
            def chunk(c, carry2):
                zero_chunk(pl.multiple_of(t * nslots + c * SEG_ROWS, SEG_ROWS)).start()
                return carry2

            lax.fori_loop(live_ref[t] // SEG_ROWS, cpt, chunk, 0)
            return carry

        lax.fori_loop(0, ntiles, tile, 0)

        @pl.when(used > 0)
        def _():
            gather(0)

    def body(b, carry):
        blk = first + b
        slot = blk & 1
        gather_done(blk).wait()

        @pl.when(blk + 1 < used)
        def _():
            gather(blk + 1)

        @pl.when(blk >= 2)
        def _():
            scatter_done(blk - 2).wait()

        xb = xbuf[slot, :, :dm]
        aux = xbuf[slot, :, dm:].astype(F32)
        w = jnp.where(aux[:, 4:5] == e.astype(F32), aux[:, 0:1] + aux[:, 1:2], aux[:, 2:3] + aux[:, 3:4])
        g = jnp.dot(xb, wg_ref[0], preferred_element_type=F32)
        u = jnp.dot(xb, wu_ref[0], preferred_element_type=F32)
        a = (g * jax.nn.sigmoid(g)) * u
        y = jnp.dot(a.astype(BF16), wd_ref[0], preferred_element_type=F32)
        ybuf[slot] = (y * w).astype(BF16)
        scatter(blk)
        return carry

    lax.fori_loop(0, count, body, 0)

    @pl.when(e == last)
    def _():
        @pl.when(used >= 2)
        def _():
            scatter_done(used - 2).wait()

        @pl.when(used >= 1)
        def _():
            scatter_done(used - 1).wait()

        _wait_copies(lax.fori_loop(0, ntiles, lambda t, acc: acc + dead_chunks(t), 0), zero_chunk(0))


def _experts(first_blk, count_blk, chunk_tbl, live_tbl, x_sorted, w_gate, w_up, w_down, nslots, ntiles):
    rows, width = x_sorted.shape
    n_exp, dm, de = w_gate.shape
    wmap = lambda e, *_: (e, 0, 0)
    grid_spec = pltpu.PrefetchScalarGridSpec(
        num_scalar_prefetch=4,
        grid=(n_exp,),
        in_specs=[pl.BlockSpec((1, dm, de), wmap),
                  pl.BlockSpec((1, dm, de), wmap),
                  pl.BlockSpec((1, de, dm), wmap),
                  pl.BlockSpec(memory_space=pl.ANY)],
        out_specs=pl.BlockSpec(memory_space=pl.ANY),
        scratch_shapes=[pltpu.VMEM((2, ROW_BLOCK, width), BF16), pltpu.VMEM((2, ROW_BLOCK, dm), BF16),
                        pltpu.VMEM((ROW_BLOCK, dm), BF16),
                        pltpu.SemaphoreType.DMA((2,)), pltpu.SemaphoreType.DMA((2,)), pltpu.SemaphoreType.DMA(())],
    )
    return pl.pallas_call(
        functools.partial(_experts_kernel, nslots=nslots, ntiles=ntiles),
        grid_spec=grid_spec,
        out_shape=jax.ShapeDtypeStruct((rows, dm), BF16),
        compiler_params=_cparams(("arbitrary",)),
        name="experts",
    )(first_blk, count_blk, chunk_tbl, live_tbl, w_gate, w_up, w_down, x_sorted)


def _combine_kernel(live_ref, route_ref, x1_ref, mod_ref, g_ref, ys_ref, yt_ref, o_ref, *, tm, head):
    i = pl.program_id(0)
    route = route_ref[...]

    def one_hot(lo, rows):
        scol = (lax.broadcasted_iota(I32, (tm, rows), 1) + lo).astype(F32)
        return jnp.where((scol == route[:, 2:3]) | (scol == route[:, 3:4]), 1.0, 0.0).astype(BF16)

    def finish(with_tail):
        moe = jnp.dot(one_hot(0, head), ys_ref[...], preferred_element_type=F32)
        if with_tail:
            moe = moe + jnp.dot(one_hot(head, yt_ref.shape[0]), yt_ref[...], preferred_element_type=F32)
        x = x1_ref[...] + mod_ref[0, 5:6, :] * moe
        ms = jnp.mean(x * x, axis=-1, keepdims=True)
        o_ref[...] = (x * lax.rsqrt(ms + RMS_EPS)) * g_ref[...]

    pl.when(live_ref[i] > head)(functools.partial(finish, True))
    pl.when(live_ref[i] <= head)(functools.partial(finish, False))


def _combine(live_tbl, route, x1, mod3, g_final, y_sorted, seq, tm, nslots):
    n, dm = x1.shape
    tpb = seq // tm
    ntiles = n // tm
    head = nslots - SLOT_GROUP
    grid_spec = pltpu.PrefetchScalarGridSpec(
        num_scalar_prefetch=1,
        grid=(n // tm,),
        in_specs=[pl.BlockSpec((tm, LANES), lambda i, *_: (i, 0)),
                  pl.BlockSpec((tm, dm), lambda i, *_: (i, 0)),
                  pl.BlockSpec((1, 6, dm), lambda i, *_: (i // tpb, 0, 0)),
                  pl.BlockSpec((1, dm), lambda i, *_: (0, 0)),
                  pl.BlockSpec((None, head, dm), lambda i, live: (i, 0, 0)),
                  pl.BlockSpec((None, SLOT_GROUP, dm),
                               lambda i, live: (jnp.where(live[i] > head, i, ntiles), head // SLOT_GROUP, 0))],
        out_specs=pl.BlockSpec((tm, dm), lambda i, *_: (i, 0)),
    )
    y3 = y_sorted.reshape(ntiles + 1, nslots, dm)
    return pl.pallas_call(
        functools.partial(_combine_kernel, tm=tm, head=head),
        grid_spec=grid_spec,
        out_shape=jax.ShapeDtypeStruct((n, dm), F32),
        compiler_params=_cparams(("arbitrary",)),
        name="combine",
    )(live_tbl, route, x1, mod3, g_final, y3, y3)


def _rope_angles(positions):
    half = HEAD_DIM // 2
    inv_freq = ROPE_THETA ** (-jnp.arange(half, dtype=F32) * (2.0 / HEAD_DIM))
    freq = jnp.tile(inv_freq, LANES // half)
    sign = jnp.tile(jnp.concatenate([-jnp.ones((half,), F32), jnp.ones((half,), F32)]), LANES // HEAD_DIM)
    ang = positions.astype(F32).reshape(-1, 1) * freq
    return ang, sign.reshape(1, LANES)


def _expert_chunk_table(lens, pstart, tot, pend, nblk, nslots):
    ntiles = lens.shape[0]
    cpb = ROW_BLOCK // SEG_ROWS
    runs = lens // SEG_ROWS
    q = jnp.arange(nblk * cpb, dtype=I32)
    owner = jnp.minimum(jnp.sum((pend[None, :] // SEG_ROWS <= q[:, None]).astype(I32), axis=1), N_EXPERTS - 1)
    is_e = owner[:, None] == jnp.arange(N_EXPERTS, dtype=I32)[None, :]
    pick_e = lambda v: jnp.sum(jnp.where(is_e, v[None, :], 0), axis=1)
    off = q - pick_e(pstart // SEG_ROWS)
    in_run = (off < pick_e(tot // SEG_ROWS)) & (q < pend[-1] // SEG_ROWS)
    upto = jnp.cumsum(runs, axis=0)
    upto_e = jnp.sum(jnp.where(is_e[:, None, :], upto[None, :, :], 0), axis=2)
    tile = jnp.minimum(jnp.sum((upto_e <= off[:, None]).astype(I32), axis=1), ntiles - 1)
    is_t = tile[:, None] == jnp.arange(ntiles, dtype=I32)[None, :]
    run_slot0 = jnp.cumsum(runs, axis=1) - runs - (upto - runs)
    slot0 = jnp.sum(jnp.where(is_t[:, :, None] & is_e[:, None, :], run_slot0[None, :, :], 0), axis=(1, 2))
    live_row = tile * nslots + (slot0 + off) * SEG_ROWS
    spare_row = ntiles * nslots + (((q // cpb) % 2) * cpb + q % cpb) * SEG_ROWS
    return jnp.where(in_run, live_row, spare_row)


def kernel(x, c, positions, w_ada, b_ada, g_mix, w_in, sink_logits, w_branch_a, w_branch_b, w_out, g_ffn,
           w_group, b_group, w_route, b_route, w_expert_gate, w_expert_up, w_expert_down, g_final):
    batch, seq, dm = x.shape
    n = batch * seq
    assert w_ada.shape[0] == 1, "one layer"
    x2 = x.reshape(n, dm)

    c8 = jnp.pad(c, ((0, 8 - batch), (0, 0)))
    mod = _ada(c8, w_ada[0], b_ada[0].reshape(1, -1))
    mod3 = mod[:batch].reshape(batch, 6, dm)

    angles, sign = _rope_angles(positions)
    outs = _inproj(x2, mod3, g_mix[0].reshape(1, dm), angles, sign, w_in[0].astype(BF16),
                   (w_expert_gate[0], w_expert_up[0], w_expert_down[0]), batch, seq)
    qkv = outs[:9]
    qb, kb, vb, sga, sgb = outs[9:14]
    expert_w_bf = outs[14:]

    dil_outs = [_dil_attention(qkv[3 * g], qkv[3 * g + 1], qkv[3 * g + 2]) for g in range(len(DILATIONS))]
    ob = _swa_attention(sink_logits[0], qb.reshape(batch, seq, -1), kb.reshape(batch, seq, -1),
                        vb.reshape(batch, seq, -1)).reshape(n, -1)

    pad = LANES - N_GROUPS - N_EXPERTS
    wr = jnp.concatenate([w_group[0], w_route[0], jnp.zeros((dm, pad), F32)], axis=1).astype(BF16)
    br = jnp.concatenate([b_group[0], b_route[0], jnp.zeros((pad,), F32)]).reshape(1, LANES)
    ntiles = n // MOE_TILE
    nslots = 2 * MOE_TILE + N_EXPERTS * SEG_ROWS
    x1, x_sorted, route, seg_lens = _outproj(dil_outs, ob, sga, sgb, x2, mod3, g_ffn[0].reshape(1, dm),
                                             w_branch_a[0].astype(BF16), w_branch_b[0].astype(BF16),
                                             w_out[0].astype(BF16), wr, br, batch, seq, nslots)

    nblk = -(-(2 * n + ntiles * N_EXPERTS * (SEG_ROWS - 1)) // ROW_BLOCK) + N_EXPERTS
    lens = seg_lens[:, 0, :N_EXPERTS].astype(I32)
    tot = jnp.sum(lens, axis=0)
    padded = (tot + ROW_BLOCK - 1) // ROW_BLOCK * ROW_BLOCK
    pend = jnp.cumsum(padded)
    pstart = pend - padded
    chunk_tbl = _expert_chunk_table(lens, pstart, tot, pend, nblk, nslots)
    live_tbl = jnp.sum(lens, axis=1)

    y_sorted = _experts(pstart // ROW_BLOCK, padded // ROW_BLOCK, chunk_tbl, live_tbl, x_sorted,
                        *expert_w_bf, nslots, ntiles)
    out = _combine(live_tbl, route, x1, mod3, g_final.reshape(1, dm), y_sorted, seq, MOE_TILE, nslots)
    return out.reshape(batch, seq, dm)
```

```python
import functools

import jax
import jax.numpy as jnp
from jax import lax
from jax.experimental import pallas as pl
from jax.experimental.pallas import tpu as pltpu

F32 = jnp.float32
BF16 = jnp.bfloat16
I32 = jnp.int32

HEAD_DIM = 64
ROPE_THETA = 10000.0
RMS_EPS = 1e-6
NEG_INF = -1e30
Q_SCALE = HEAD_DIM ** -0.5
DILATIONS = (1, 4, 16)
DIL_HALF_WINDOW = 64
DIL_GROUP_W = 256
SWA_WINDOW = 128
N_GROUPS = 4
EXPERTS_PER_GROUP = 8
N_EXPERTS = 32
LANES = 128
ROW_BLOCK = 512
SEG_ROWS = 16
MOE_TILE = 512
SLOT_GROUP = 256
VMEM_LIMIT = 56 * 1024 * 1024


def _cparams(sem):
    return pltpu.CompilerParams(dimension_semantics=sem, vmem_limit_bytes=VMEM_LIMIT)


def _ada_kernel(c_ref, w_ref, b_ref, o_ref):
    c = c_ref[...]
    cs = c * jax.nn.sigmoid(c)
    o_ref[...] = jnp.dot(cs.astype(BF16), w_ref[...].astype(BF16), preferred_element_type=F32) + b_ref[...]


def _ada(c8, w_ada, b_ada):
    d, n = w_ada.shape
    tn = 1536
    return pl.pallas_call(
        _ada_kernel,
        grid=(n // tn,),
        in_specs=[pl.BlockSpec((8, d), lambda j: (0, 0)),
                  pl.BlockSpec((d, tn), lambda j: (0, j)),
                  pl.BlockSpec((1, tn), lambda j: (0, j))],
        out_specs=pl.BlockSpec((8, tn), lambda j: (0, j)),
        out_shape=jax.ShapeDtypeStruct((8, n), F32),
        compiler_params=_cparams(("arbitrary",)),
        name="ada",
    )(c8, w_ada, b_ada)


def _rms_mod(x, g, shift, scale):
    ms = jnp.mean(x * x, axis=-1, keepdims=True)
    return (x * lax.rsqrt(ms + RMS_EPS)) * (g * (1.0 + scale)) + shift


def _inproj_kernel(x_ref, mod_ref, g_ref, ang_ref, sign_ref, w_ref, eg_ref, eu_ref, ed_ref,
                   q0_ref, k0_ref, v0_ref, q1_ref, k1_ref, v1_ref, q2_ref, k2_ref, v2_ref,
                   qb_ref, kb_ref, vb_ref, sga_ref, sgb_ref, egb_ref, eub_ref, edb_ref, stg_ref, *, tm, n_exp):
    def cast_expert():
        egb_ref[...] = eg_ref[...].astype(BF16)
        eub_ref[...] = eu_ref[...].astype(BF16)
        edb_ref[...] = ed_ref[...].astype(BF16)

    if n_exp is None:
        cast_expert()
    else:
        pl.when(pl.program_id(0) < n_exp)(cast_expert)

    h = _rms_mod(x_ref[...], g_ref[...], mod_ref[0, 0:1, :], mod_ref[0, 1:2, :])
    hb = h.astype(BF16)
    ang = ang_ref[...]
    cos = jnp.cos(ang)
    sin = jnp.sin(ang) * sign_ref[...]
    lane = lax.broadcasted_iota(I32, (tm, LANES), 1)
    first_half = (lane & 32) == 0
    low = lane < 64

    def proj(c0, width):
        return jnp.dot(hb, w_ref[:, c0:c0 + width], preferred_element_type=F32)

    def rope(t):
        rot = jnp.where(first_half, pltpu.roll(t, 96, 1), pltpu.roll(t, 32, 1))
        return t * cos + rot * sin

    def rope256(p):
        return jnp.concatenate([rope(p[:, :LANES]), rope(p[:, LANES:])], axis=1)

    def store_group(ref, val, d):
        if d == 1:
            ref[0, 0] = val.astype(BF16)
        else:
            for c in range(2):
                stg_ref[c] = val[:, c * LANES:(c + 1) * LANES]
            for r in range(d):
                for c in range(2):
                    ref[0, r, :, c * LANES:(c + 1) * LANES] = (
                        stg_ref[c, pl.ds(r, tm // d, stride=d), :].astype(BF16))

    q_refs = (q0_ref, q1_ref, q2_ref)
    k_refs = (k0_ref, k1_ref, k2_ref)
    v_refs = (v0_ref, v1_ref, v2_ref)
    for g, d in enumerate(DILATIONS):
        store_group(q_refs[g], rope256(proj(g * 256, 256)) * Q_SCALE, d)
        store_group(k_refs[g], rope256(proj(768 + g * 256, 256)), d)
        store_group(v_refs[g], proj(1536 + g * 256, 256), d)
    for j in range(2):
        qb_ref[:, j * 256:(j + 1) * 256] = (rope256(proj(2304 + j * 256, 256)) * Q_SCALE).astype(BF16)
    kv = proj(2816, 256)
    kb = rope(kv[:, :LANES])
    vb = kv[:, LANES:]
    kb_sw = pltpu.roll(kb, 64, 1)
    vb_sw = pltpu.roll(vb, 64, 1)
    kb_ref[:, :LANES] = jnp.where(low, kb, kb_sw).astype(BF16)
    kb_ref[:, LANES:] = jnp.where(low, kb_sw, kb).astype(BF16)
    vb_ref[:, :LANES] = jnp.where(low, vb, vb_sw).astype(BF16)
    vb_ref[:, LANES:] = jnp.where(low, vb_sw, vb).astype(BF16)
    for j in range(4):
        sga_ref[:, j * 256:(j + 1) * 256] = jax.nn.sigmoid(proj(3072 + j * 256, 256)).astype(BF16)
        sgb_ref[:, j * 256:(j + 1) * 256] = jax.nn.sigmoid(proj(4096 + j * 256, 256)).astype(BF16)


def _inproj(x2, mod3, g_mix, angles, sign, w_in_bf, expert_w, batch, seq):
    n, dm = x2.shape
    tm = 512
    tpb = seq // tm
    grid = (n // tm,)
    row = lambda i: (i, 0)
    strided_specs, strided_shapes = [], []
    for d in DILATIONS:
        for _ in range(3):
            strided_specs.append(pl.BlockSpec((1, d, tm // d, DIL_GROUP_W), lambda i: (i // tpb, 0, i % tpb, 0)))
            strided_shapes.append(jax.ShapeDtypeStruct((batch, d, seq // d, DIL_GROUP_W), BF16))
    out_specs = strided_specs + [
        pl.BlockSpec((tm, 512), row), pl.BlockSpec((tm, 256), row), pl.BlockSpec((tm, 256), row),
        pl.BlockSpec((tm, dm), row), pl.BlockSpec((tm, dm), row)]
    out_shapes = strided_shapes + [
        jax.ShapeDtypeStruct((n, 512), BF16), jax.ShapeDtypeStruct((n, 256), BF16),
        jax.ShapeDtypeStruct((n, 256), BF16), jax.ShapeDtypeStruct((n, dm), BF16),
        jax.ShapeDtypeStruct((n, dm), BF16)]
    n_exp = expert_w[0].shape[0]
    assert n // tm >= n_exp, "one expert's weights are cast per grid step"
    emap = lambda i: (jnp.minimum(i, n_exp - 1), 0, 0)
    expert_specs = [pl.BlockSpec((1,) + w.shape[1:], emap) for w in expert_w]
    return pl.pallas_call(
        functools.partial(_inproj_kernel, tm=tm, n_exp=None if n // tm == n_exp else n_exp),
        grid=grid,
        in_specs=[pl.BlockSpec((tm, dm), row),
                  pl.BlockSpec((1, 6, dm), lambda i: (i // tpb, 0, 0)),
                  pl.BlockSpec((1, dm), lambda i: (0, 0)),
                  pl.BlockSpec((tm, LANES), row),
                  pl.BlockSpec((1, LANES), lambda i: (0, 0)),
                  pl.BlockSpec(w_in_bf.shape, lambda i: (0, 0), pipeline_mode=pl.Buffered(1))] + expert_specs,
        out_specs=out_specs + expert_specs,
        out_shape=out_shapes + [jax.ShapeDtypeStruct(w.shape, BF16) for w in expert_w],
        scratch_shapes=[pltpu.VMEM((2, tm, LANES), F32)],
        compiler_params=_cparams(("arbitrary",)),
        name="inproj",
    )(x2, mod3, g_mix, angles, sign, w_in_bf, *expert_w)


def _split_heads(q2, low):
    zero = jnp.zeros_like(q2)
    return jnp.concatenate([jnp.where(low, q2, zero), jnp.where(low, zero, q2)], axis=0)


def _band_softmax(qst, k2, v2, bias, sinks):
    s = lax.dot_general(qst, k2, (((1,), (1,)), ((), ())), preferred_element_type=F32)
    s = s + bias
    rows, tk = s.shape
    m = jnp.max(s, axis=-1, keepdims=True)
    if sinks is not None:
        seg = rows // len(sinks)
        m = jnp.concatenate([jnp.maximum(m[h * seg:(h + 1) * seg], sk) for h, sk in enumerate(sinks)], axis=0)
    m = jnp.broadcast_to(m, (rows, LANES))
    e = jnp.concatenate([jnp.exp(s[:, c * LANES:(c + 1) * LANES] - m) for c in range(tk // LANES)], axis=1)
    v_ones = jnp.concatenate([v2, jnp.ones((tk, LANES), BF16)], axis=1)
    od = jnp.dot(e.astype(BF16), v_ones, preferred_element_type=F32)
    o, den = od[:, :LANES], od[:, LANES:]
    if sinks is not None:
        den = jnp.concatenate([den[h * seg:(h + 1) * seg] + jnp.exp(sk - m[h * seg:(h + 1) * seg])
                               for h, sk in enumerate(sinks)], axis=0)
    return o / den, m, den


def _fill_band_bias(bias_ref, tq, window):
    rows, tk = bias_ref.shape[1:]
    row = lax.broadcasted_iota(I32, (rows, tk), 0) & (tq - 1)
    col = lax.broadcasted_iota(I32, (rows, tk), 1)
    for i in range(bias_ref.shape[0]):
        bias_ref[i] = jnp.where(jnp.abs(col - row - i * window) <= window, 0.0, NEG_INF)


def _dil_kernel(q_ref, k_ref, v_ref, o_ref, l_ref, bias_ref, *, length, tq, tk):
    low = lax.broadcasted_iota(I32, (tq, LANES), 1) < 64
    _fill_band_bias(bias_ref, tq, DIL_HALF_WINDOW)
    nq = length // tq

    def body(j, carry):
        r = j // nq
        qs = pl.multiple_of((j % nq) * tq, tq)
        ks = pl.multiple_of(jnp.clip(qs - DIL_HALF_WINDOW, 0, length - tk), DIL_HALF_WINDOW)
        which = (qs - ks) // DIL_HALF_WINDOW
        for c in range(DIL_GROUP_W // LANES):
            cs = slice(c * LANES, (c + 1) * LANES)
            qst = _split_heads(q_ref[r, pl.ds(qs, tq), cs], low)
            o, m, den = _band_softmax(qst, k_ref[r, pl.ds(ks, tk), cs], v_ref[r, pl.ds(ks, tk), cs],
                                      bias_ref[which], None)
            lse = m + jnp.log(den)
            o_ref[r, pl.ds(qs, tq), cs] = jnp.where(low, o[:tq], o[tq:]).astype(BF16)
            l_ref[r, pl.ds(qs, tq), cs] = jnp.where(low, lse[:tq], lse[tq:])
        return carry

    lax.fori_loop(0, q_ref.shape[0] * nq, body, 0, unroll=8)


def _dil_attention(q, k, v):
    batch, d, length, w = q.shape
    tq, tk = 128, 256
    spec = pl.BlockSpec((None, d, length, w), lambda b: (b, 0, 0, 0))
    return pl.pallas_call(
        functools.partial(_dil_kernel, length=length, tq=tq, tk=tk),
        grid=(batch,),
        in_specs=[spec, spec, spec],
        out_specs=[spec, spec],
        out_shape=[jax.ShapeDtypeStruct(q.shape, BF16), jax.ShapeDtypeStruct(q.shape, F32)],
        scratch_shapes=[pltpu.VMEM((3, 2 * tq, tk), F32)],
        compiler_params=_cparams(("arbitrary",)),
        name=f"dil{d}",
    )(q, k, v)


def _swa_kernel(sink_ref, q_ref, k_ref, v_ref, o_ref, bias_ref, *, length, tq, tk):
    low = lax.broadcasted_iota(I32, (tq, LANES), 1) < 64
    nblk = q_ref.shape[1] // LANES
    _fill_band_bias(bias_ref, tq, SWA_WINDOW)

    def body(j, carry):
        qs = pl.multiple_of(j * tq, tq)
        ks = pl.multiple_of(jnp.clip(qs - SWA_WINDOW, 0, length - tk), SWA_WINDOW)
        bias = bias_ref[(qs - ks) // SWA_WINDOW]
        for b in range(nblk):
            cs = slice((b // 2) * LANES, (b // 2 + 1) * LANES)
            bs = slice(b * LANES, (b + 1) * LANES)
            qst = _split_heads(q_ref[pl.ds(qs, tq), bs], low)
            sinks = (sink_ref[2 * b], sink_ref[2 * b + 1])
            o, _, _ = _band_softmax(qst, k_ref[pl.ds(ks, tk), cs], v_ref[pl.ds(ks, tk), cs], bias, sinks)
            o_ref[pl.ds(qs, tq), bs] = jnp.where(low, o[:tq], o[tq:]).astype(BF16)
        return carry

    lax.fori_loop(0, length // tq, body, 0, unroll=8)


def _swa_attention(sink, q, k, v):
    batch, length, qw = q.shape
    tq, tk = 128, 384
    return pl.pallas_call(
        functools.partial(_swa_kernel, length=length, tq=tq, tk=tk),
        grid=(batch,),
        in_specs=[pl.BlockSpec(memory_space=pltpu.SMEM),
                  pl.BlockSpec((None, length, qw), lambda b: (b, 0, 0)),
                  pl.BlockSpec((None, length, k.shape[2]), lambda b: (b, 0, 0)),
                  pl.BlockSpec((None, length, v.shape[2]), lambda b: (b, 0, 0))],
        out_specs=pl.BlockSpec((None, length, qw), lambda b: (b, 0, 0)),
        out_shape=jax.ShapeDtypeStruct(q.shape, BF16),
        scratch_shapes=[pltpu.VMEM((3, 2 * tq, tk), F32)],
        compiler_params=_cparams(("arbitrary",)),
        name="swa",
    )(sink, q, k, v)


def _route_rows(logits):
    lane = lax.broadcasted_iota(I32, logits.shape, 1).astype(F32)
    big = 1e9
    is_g = lane < N_GROUPS
    gl = jnp.where(is_g, logits, NEG_INF)
    gmax = jnp.max(gl, axis=-1, keepdims=True)
    gsel = jnp.min(jnp.where(is_g & (gl == gmax), lane, big), axis=-1, keepdims=True)
    gw = 1.0 / jnp.sum(jnp.where(is_g, jnp.exp(gl - gmax), 0.0), axis=-1, keepdims=True)
    e_lo = N_GROUPS + gsel * EXPERTS_PER_GROUP
    in_grp = (lane >= e_lo) & (lane < e_lo + EXPERTS_PER_GROUP)
    el = jnp.where(in_grp, logits, NEG_INF)
    m1 = jnp.max(el, axis=-1, keepdims=True)
    i1 = jnp.min(jnp.where(in_grp & (el == m1), lane, big), axis=-1, keepdims=True)
    el2 = jnp.where(lane == i1, NEG_INF, el)
    m2 = jnp.max(el2, axis=-1, keepdims=True)
    i2 = jnp.min(jnp.where(in_grp & (lane != i1) & (el2 == m2), lane, big), axis=-1, keepdims=True)
    t = jnp.exp(m2 - m1)
    tw1 = gw / (1.0 + t)
    tw2 = gw * t / (1.0 + t)
    out = jnp.where(lane == 0, tw1, 0.0)
    out = jnp.where(lane == 1, tw2, out)
    out = jnp.where(lane == 2, i1 - N_GROUPS, out)
    return jnp.where(lane == 3, i2 - N_GROUPS, out)


def _outproj_kernel(o0_ref, l0_ref, o1_ref, l1_ref, o2_ref, l2_ref, ob_ref, sga_ref, sgb_ref, x_ref,
                    mod_ref, g_ref, wa_ref, wb_ref, wo_ref, wr_ref, br_ref,
                    x1_ref, xs_ref, route_ref, len_ref,
                    so1_ref, sl1_ref, so2_ref, sl2_ref, h2_ref, *, tm, sub, group):
    ntiles = pl.num_programs(0) - 1

    @pl.when(pl.program_id(0) == ntiles)
    def _():
        xs_ref[...] = jnp.zeros_like(xs_ref)

    pl.when(pl.program_id(0) < ntiles)(functools.partial(
        _outproj_tile, o0_ref, l0_ref, o1_ref, l1_ref, o2_ref, l2_ref, ob_ref, sga_ref, sgb_ref, x_ref,
        mod_ref, g_ref, wa_ref, wb_ref, wo_ref, wr_ref, br_ref, x1_ref, xs_ref, route_ref, len_ref,
        so1_ref, sl1_ref, so2_ref, sl2_ref, h2_ref, tm=tm, sub=sub, group=group))


def _outproj_tile(o0_ref, l0_ref, o1_ref, l1_ref, o2_ref, l2_ref, ob_ref, sga_ref, sgb_ref, x_ref,
                  mod_ref, g_ref, wa_ref, wb_ref, wo_ref, wr_ref, br_ref,
                  x1_ref, xs_ref, route_ref, len_ref,
                  so1_ref, sl1_ref, so2_ref, sl2_ref, h2_ref, *, tm, sub, group):
    dm = x_ref.shape[1]
    for (o_ref, l_ref, so_ref, sl_ref, d) in ((o1_ref, l1_ref, so1_ref, sl1_ref, DILATIONS[1]),
                                              (o2_ref, l2_ref, so2_ref, sl2_ref, DILATIONS[2])):
        for r in range(d):
            for c in range(2):
                cs = slice(c * LANES, (c + 1) * LANES)
                so_ref[c, pl.ds(r, tm // d, stride=d), :] = o_ref[0, r, :, cs].astype(F32)
                sl_ref[c, pl.ds(r, tm // d, stride=d), :] = l_ref[0, r, :, cs]
    mr = lax.broadcasted_iota(I32, (LANES, LANES), 0)
    mc = lax.broadcasted_iota(I32, (LANES, LANES), 1)
    move_hi = jnp.where(((mr < 2) & (mc == 2 * mr)) | ((mr >= 2) & (mr < 4) & (mc == mr + 2)), 1.0, 0.0).astype(BF16)
    move_lo = jnp.where((mr < 2) & (mc == 2 * mr + 1), 1.0, 0.0).astype(BF16)
    for t in range(tm // sub):
        rs = slice(t * sub, (t + 1) * sub)
        both = lambda ref: jnp.concatenate([ref[0, rs, :], ref[1, rs, :]], axis=1)
        o0, l0 = o0_ref[0, 0, rs, :].astype(F32), l0_ref[0, 0, rs, :]
        o1, l1, o2, l2 = both(so1_ref), both(sl1_ref), both(so2_ref), both(sl2_ref)
        mx = jnp.maximum(jnp.maximum(l0, l1), l2)
        w0, w1, w2 = jnp.exp(l0 - mx), jnp.exp(l1 - mx), jnp.exp(l2 - mx)
        o_a = (w0 * o0 + w1 * o1 + w2 * o2) / (w0 + w1 + w2)
        y_a = jnp.dot(o_a.astype(BF16), wa_ref[...], preferred_element_type=F32)
        y_b = jnp.dot(ob_ref[rs, :], wb_ref[...], preferred_element_type=F32)
        merged = sga_ref[rs, :].astype(F32) * y_a + sgb_ref[rs, :].astype(F32) * y_b
        mix = jnp.dot(merged.astype(BF16), wo_ref[...], preferred_element_type=F32)
        x1 = x_ref[rs, :] + mod_ref[0, 2:3, :] * mix
        x1_ref[rs, :] = x1
        h2 = _rms_mod(x1, g_ref[...], mod_ref[0, 3:4, :], mod_ref[0, 4:5, :]).astype(BF16)
        h2_ref[rs, :dm] = h2
        logits = jnp.dot(h2, wr_ref[...], preferred_element_type=F32) + br_ref[...]
        rt = _route_rows(logits)
        route_ref[rs, :] = rt
        hi = rt.astype(BF16)
        lo = (rt - hi.astype(F32)).astype(BF16)
        aux = (jnp.dot(hi, move_hi, preferred_element_type=F32) + jnp.dot(lo, move_lo, preferred_element_type=F32))
        h2_ref[rs, dm:] = aux.astype(BF16)

    part = route_ref[...]
    e1, e2 = part[:, 2:3], part[:, 3:4]
    lane = lax.broadcasted_iota(I32, (tm, LANES), 1).astype(F32)
    onehot = jnp.where((lane == e1) | (lane == e2), 1.0, 0.0)
    rr = lax.broadcasted_iota(I32, (tm, tm), 0)
    cc = lax.broadcasted_iota(I32, (tm, tm), 1)
    tri = jnp.where(rr > cc, 1.0, 0.0).astype(BF16)
    prefix = jnp.dot(tri, onehot.astype(BF16), preferred_element_type=F32)
    cnt = jnp.sum(onehot, axis=0, keepdims=True)
    seg_len = jnp.ceil(cnt * (1.0 / SEG_ROWS)) * SEG_ROWS
    ur = lax.broadcasted_iota(I32, (LANES, LANES), 0)
    uc = lax.broadcasted_iota(I32, (LANES, LANES), 1)
    upper = jnp.where(ur < uc, 1.0, 0.0).astype(BF16)
    seg_off = jnp.dot(jnp.broadcast_to(seg_len, (8, LANES)).astype(BF16), upper, preferred_element_type=F32)[0:1, :]
    slot_map = seg_off + prefix
    s1 = jnp.sum(jnp.where(lane == e1, slot_map, 0.0), axis=-1, keepdims=True)
    s2 = jnp.sum(jnp.where(lane == e2, slot_map, 0.0), axis=-1, keepdims=True)
    len_ref[0] = jnp.broadcast_to(seg_len, (8, LANES))
    route = jnp.where(lane == 2, s1, jnp.where(lane == 3, s2, part))
    route_ref[...] = route

    pr = lax.broadcasted_iota(I32, (8, LANES), 0)
    pc = lax.broadcasted_iota(I32, (8, LANES), 1)
    lane_pick = jnp.where(pc == pr + 2, 1.0, 0.0)
    slots_t = lax.dot_general(lane_pick, route, (((1,), (1,)), ((), ())), preferred_element_type=F32,
                              precision=lax.Precision.HIGHEST)
    live = jnp.sum(seg_len).astype(I32)

    def sort_rows(lo, rows):
        srow = (lax.broadcasted_iota(I32, (rows, tm), 0) + lo).astype(F32)
        pick = jnp.where((srow == slots_t[0:1, :]) | (srow == slots_t[1:2, :]), 1.0, 0.0).astype(BF16)
        xs_ref[lo:lo + rows, :] = jnp.dot(pick, h2_ref[...], preferred_element_type=F32).astype(BF16)

    def clear_rows(lo, rows):
        xs_ref[lo:lo + rows, :] = jnp.zeros((rows, xs_ref.shape[1]), BF16)

    head = xs_ref.shape[0] - group
    sort_rows(0, head)
    pl.when(live > head)(functools.partial(sort_rows, head, group))
    pl.when(live <= head)(functools.partial(clear_rows, head, group))


def _outproj(dil_outs, ob, sga, sgb, x2, mod3, g_ffn, wa, wb, wo, wr, br, batch, seq, nslots):
    n, dm = x2.shape
    tm = MOE_TILE
    tpb = seq // tm
    ntiles = n // tm
    tile = lambda i: jnp.minimum(i, ntiles - 1)
    row = lambda i: (tile(i), 0)
    const = lambda i: (0, 0)
    in_specs = []
    args = []
    for (o, l), d in zip(dil_outs, DILATIONS):
        spec = pl.BlockSpec((1, d, tm // d, DIL_GROUP_W), lambda i: (tile(i) // tpb, 0, tile(i) % tpb, 0))
        in_specs += [spec, spec]
        args += [o, l]
    in_specs += [pl.BlockSpec((tm, ob.shape[1]), row), pl.BlockSpec((tm, dm), row), pl.BlockSpec((tm, dm), row),
                 pl.BlockSpec((tm, dm), row),
                 pl.BlockSpec((1, 6, dm), lambda i: (tile(i) // tpb, 0, 0)),
                 pl.BlockSpec((1, dm), const),
                 pl.BlockSpec(wa.shape, const), pl.BlockSpec(wb.shape, const), pl.BlockSpec(wo.shape, const),
                 pl.BlockSpec(wr.shape, const), pl.BlockSpec(br.shape, const)]
    args += [ob, sga, sgb, x2, mod3, g_ffn, wa, wb, wo, wr, br]
    width = dm + LANES
    return pl.pallas_call(
        functools.partial(_outproj_kernel, tm=tm, sub=512, group=SLOT_GROUP),
        grid=(ntiles + 1,),
        in_specs=in_specs,
        out_specs=[pl.BlockSpec((tm, dm), row), pl.BlockSpec((nslots, width), lambda i: (i, 0)),
                   pl.BlockSpec((tm, LANES), row), pl.BlockSpec((1, 8, LANES), lambda i: (tile(i), 0, 0))],
        out_shape=[jax.ShapeDtypeStruct((n, dm), F32), jax.ShapeDtypeStruct(((ntiles + 1) * nslots, width), BF16),
                   jax.ShapeDtypeStruct((n, LANES), F32), jax.ShapeDtypeStruct((ntiles, 8, LANES), F32)],
        scratch_shapes=[pltpu.VMEM((2, tm, LANES), F32)] * 4 + [pltpu.VMEM((tm, width), BF16)],
        compiler_params=_cparams(("arbitrary",)),
        name="outproj",
    )(*args)


def _wait_copies(count, copy):
    def wait_one(c, carry):
        copy.wait()
        return carry

    lax.fori_loop(0, count, wait_one, 0)


def _experts_kernel(first_ref, count_ref, chunk_ref, live_ref, wg_ref, wu_ref, wd_ref, xs_hbm, ys_hbm,
                    xbuf, ybuf, zbuf, xsem, ysem, zsem, *, nslots, ntiles):
    e = pl.program_id(0)
    last = pl.num_programs(0) - 1
    first = first_ref[e]
    count = count_ref[e]
    used = first_ref[last] + count_ref[last]
    dm = ybuf.shape[2]
    cpb = ROW_BLOCK // SEG_ROWS
    cpt = nslots // SEG_ROWS

    def chunk_rows(blk, j):
        return pl.ds(pl.multiple_of(chunk_ref[blk * cpb + j], SEG_ROWS), SEG_ROWS)

    def gather(blk):
        for j in range(cpb):
            pltpu.make_async_copy(xs_hbm.at[chunk_rows(blk, j), :], xbuf.at[blk & 1, pl.ds(j * SEG_ROWS, SEG_ROWS), :],
                                  xsem.at[blk & 1]).start(priority=1)

    def scatter(blk):
        for j in range(cpb):
            pltpu.make_async_copy(ybuf.at[blk & 1, pl.ds(j * SEG_ROWS, SEG_ROWS), :], ys_hbm.at[chunk_rows(blk, j), :],
                                  ysem.at[blk & 1]).start()

    def gather_done(blk):
        return pltpu.make_async_copy(xs_hbm.at[pl.ds(0, ROW_BLOCK), :], xbuf.at[blk & 1], xsem.at[blk & 1])

    def scatter_done(blk):
        return pltpu.make_async_copy(ybuf.at[blk & 1], ys_hbm.at[pl.ds(0, ROW_BLOCK), :], ysem.at[blk & 1])

    def zero_chunk(row):
        return pltpu.make_async_copy(zbuf.at[pl.ds(0, SEG_ROWS), :], ys_hbm.at[pl.ds(row, SEG_ROWS), :], zsem)

    def dead_chunks(t):
        return cpt - live_ref[t] // SEG_ROWS

    @pl.when(e == 0)
    def _():
        zbuf[...] = jnp.zeros_like(zbuf)
        for b in range(nslots // ROW_BLOCK):
            pltpu.make_async_copy(zbuf, ys_hbm.at[pl.ds(ntiles * nslots + b * ROW_BLOCK, ROW_BLOCK), :], zsem).start()
        for b in range(nslots // ROW_BLOCK):
            pltpu.make_async_copy(zbuf, ys_hbm.at[pl.ds(0, ROW_BLOCK), :], zsem).wait()

        def tile(t, carry):
            def chunk(c, carry2):
                zero_chunk(pl.multiple_of(t * nslots + c * SEG_ROWS, SEG_ROWS)).start()
                return carry2

            lax.fori_loop(live_ref[t] // SEG_ROWS, cpt, chunk, 0)
            return carry

        lax.fori_loop(0, ntiles, tile, 0)

        @pl.when(used > 0)
        def _():
            gather(0)

    def body(b, carry):
        blk = first + b
        slot = blk & 1
        gather_done(blk).wait()

        @pl.when(blk + 1 < used)
        def _():
            gather(blk + 1)

        @pl.when(blk >= 2)
        def _():
            scatter_done(blk - 2).wait()

        xb = xbuf[slot, :, :dm]
        aux = xbuf[slot, :, dm:].astype(F32)
        w = jnp.where(aux[:, 4:5] == e.astype(F32), aux[:, 0:1] + aux[:, 1:2], aux[:, 2:3] + aux[:, 3:4])
        g = jnp.dot(xb, wg_ref[0], preferred_element_type=F32)
        u = jnp.dot(xb, wu_ref[0], preferred_element_type=F32)
        a = (g * jax.nn.sigmoid(g)) * u
        y = jnp.dot(a.astype(BF16), wd_ref[0], preferred_element_type=F32)
        ybuf[slot] = (y * w).astype(BF16)
        scatter(blk)
        return carry

    lax.fori_loop(0, count, body, 0)

    @pl.when(e == last)
    def _():
        @pl.when(used >= 2)
        def _():
            scatter_done(used - 2).wait()

        @pl.when(used >= 1)
        def _():
            scatter_done(used - 1).wait()

        _wait_copies(lax.fori_loop(0, ntiles, lambda t, acc: acc + dead_chunks(t), 0), zero_chunk(0))


def _experts(first_blk, count_blk, chunk_tbl, live_tbl, x_sorted, w_gate, w_up, w_down, nslots, ntiles):
    rows, width = x_sorted.shape
    n_exp, dm, de = w_gate.shape
    wmap = lambda e, *_: (e, 0, 0)
    grid_spec = pltpu.PrefetchScalarGridSpec(
        num_scalar_prefetch=4,
        grid=(n_exp,),
        in_specs=[pl.BlockSpec((1, dm, de), wmap),
                  pl.BlockSpec((1, dm, de), wmap),
                  pl.BlockSpec((1, de, dm), wmap),
                  pl.BlockSpec(memory_space=pl.ANY)],
        out_specs=pl.BlockSpec(memory_space=pl.ANY),
        scratch_shapes=[pltpu.VMEM((2, ROW_BLOCK, width), BF16), pltpu.VMEM((2, ROW_BLOCK, dm), BF16),
                        pltpu.VMEM((ROW_BLOCK, dm), BF16),
                        pltpu.SemaphoreType.DMA((2,)), pltpu.SemaphoreType.DMA((2,)), pltpu.SemaphoreType.DMA(())],
    )
    return pl.pallas_call(
        functools.partial(_experts_kernel, nslots=nslots, ntiles=ntiles),
        grid_spec=grid_spec,
        out_shape=jax.ShapeDtypeStruct((rows, dm), BF16),
        compiler_params=_cparams(("arbitrary",)),
        name="experts",
    )(first_blk, count_blk, chunk_tbl, live_tbl, w_gate, w_up, w_down, x_sorted)


def _combine_kernel(live_ref, route_ref, x1_ref, mod_ref, g_ref, ys_ref, yt_ref, o_ref, *, tm, head):
    i = pl.program_id(0)
    route = route_ref[...]

    def one_hot(lo, rows):
        scol = (lax.broadcasted_iota(I32, (tm, rows), 1) + lo).astype(F32)
        return jnp.where((scol == route[:, 2:3]) | (scol == route[:, 3:4]), 1.0, 0.0).astype(BF16)

    def finish(with_tail):
        moe = jnp.dot(one_hot(0, head), ys_ref[...], preferred_element_type=F32)
        if with_tail:
            moe = moe + jnp.dot(one_hot(head, yt_ref.shape[0]), yt_ref[...], preferred_element_type=F32)
        x = x1_ref[...] + mod_ref[0, 5:6, :] * moe
        ms = jnp.mean(x * x, axis=-1, keepdims=True)
        o_ref[...] = (x * lax.rsqrt(ms + RMS_EPS)) * g_ref[...]

    pl.when(live_ref[i] > head)(functools.partial(finish, True))
    pl.when(live_ref[i] <= head)(functools.partial(finish, False))


def _combine(live_tbl, route, x1, mod3, g_final, y_sorted, seq, tm, nslots):
    n, dm = x1.shape
    tpb = seq // tm
    ntiles = n // tm
    head = nslots - SLOT_GROUP
    grid_spec = pltpu.PrefetchScalarGridSpec(
        num_scalar_prefetch=1,
        grid=(n // tm,),
        in_specs=[pl.BlockSpec((tm, LANES), lambda i, *_: (i, 0)),
                  pl.BlockSpec((tm, dm), lambda i, *_: (i, 0)),
                  pl.BlockSpec((1, 6, dm), lambda i, *_: (i // tpb, 0, 0)),
                  pl.BlockSpec((1, dm), lambda i, *_: (0, 0)),
                  pl.BlockSpec((None, head, dm), lambda i, live: (i, 0, 0)),
                  pl.BlockSpec((None, SLOT_GROUP, dm),
                               lambda i, live: (jnp.where(live[i] > head, i, ntiles), head // SLOT_GROUP, 0))],
        out_specs=pl.BlockSpec((tm, dm), lambda i, *_: (i, 0)),
    )
    y3 = y_sorted.reshape(ntiles + 1, nslots, dm)
    return pl.pallas_call(
        functools.partial(_combine_kernel, tm=tm, head=head),
        grid_spec=grid_spec,
        out_shape=jax.ShapeDtypeStruct((n, dm), F32),
        compiler_params=_cparams(("arbitrary",)),
        name="combine",
    )(live_tbl, route, x1, mod3, g_final, y3, y3)


def _rope_angles(positions):
    half = HEAD_DIM // 2
    inv_freq = ROPE_THETA ** (-jnp.arange(half, dtype=F32) * (2.0 / HEAD_DIM))
    freq = jnp.tile(inv_freq, LANES // half)
    sign = jnp.tile(jnp.concatenate([-jnp.ones((half,), F32), jnp.ones((half,), F32)]), LANES // HEAD_DIM)
    ang = positions.astype(F32).reshape(-1, 1) * freq
    return ang, sign.reshape(1, LANES)


def _expert_chunk_table(lens, pstart, tot, pend, nblk, nslots):
    ntiles = lens.shape[0]
    cpb = ROW_BLOCK // SEG_ROWS
    runs = lens // SEG_ROWS
    q = jnp.arange(nblk * cpb, dtype=I32)
    owner = jnp.minimum(jnp.sum((pend[None, :] // SEG_ROWS <= q[:, None]).astype(I32), axis=1), N_EXPERTS - 1)
    is_e = owner[:, None] == jnp.arange(N_EXPERTS, dtype=I32)[None, :]
    pick_e = lambda v: jnp.sum(jnp.where(is_e, v[None, :], 0), axis=1)
    off = q - pick_e(pstart // SEG_ROWS)
    in_run = (off < pick_e(tot // SEG_ROWS)) & (q < pend[-1] // SEG_ROWS)
    upto = jnp.cumsum(runs, axis=0)
    upto_e = jnp.sum(jnp.where(is_e[:, None, :], upto[None, :, :], 0), axis=2)
    tile = jnp.minimum(jnp.sum((upto_e <= off[:, None]).astype(I32), axis=1), ntiles - 1)
    is_t = tile[:, None] == jnp.arange(ntiles, dtype=I32)[None, :]
    run_slot0 = jnp.cumsum(runs, axis=1) - runs - (upto - runs)
    slot0 = jnp.sum(jnp.where(is_t[:, :, None] & is_e[:, None, :], run_slot0[None, :, :], 0), axis=(1, 2))
    live_row = tile * nslots + (slot0 + off) * SEG_ROWS
    spare_row = ntiles * nslots + (((q // cpb) % 2) * cpb + q % cpb) * SEG_ROWS
    return jnp.where(in_run, live_row, spare_row)


def kernel(x, c, positions, w_ada, b_ada, g_mix, w_in, sink_logits, w_branch_a, w_branch_b, w_out, g_ffn,
           w_group, b_group, w_route, b_route, w_expert_gate, w_expert_up, w_expert_down, g_final):
    batch, seq, dm = x.shape
    n = batch * seq
    assert w_ada.shape[0] == 1, "one layer"
    x2 = x.reshape(n, dm)

    c8 = jnp.pad(c, ((0, 8 - batch), (0, 0)))
    mod = _ada(c8, w_ada[0], b_ada[0].reshape(1, -1))
    mod3 = mod[:batch].reshape(batch, 6, dm)

    angles, sign = _rope_angles(positions)
    outs = _inproj(x2, mod3, g_mix[0].reshape(1, dm), angles, sign, w_in[0].astype(BF16),
                   (w_expert_gate[0], w_expert_up[0], w_expert_down[0]), batch, seq)
    qkv = outs[:9]
    qb, kb, vb, sga, sgb = outs[9:14]
    expert_w_bf = outs[14:]

    dil_outs = [_dil_attention(qkv[3 * g], qkv[3 * g + 1], qkv[3 * g + 2]) for g in range(len(DILATIONS))]
    ob = _swa_attention(sink_logits[0], qb.reshape(batch, seq, -1), kb.reshape(batch, seq, -1),
                        vb.reshape(batch, seq, -1)).reshape(n, -1)

    pad = LANES - N_GROUPS - N_EXPERTS
    wr = jnp.concatenate([w_group[0], w_route[0], jnp.zeros((dm, pad), F32)], axis=1).astype(BF16)
    br = jnp.concatenate([b_group[0], b_route[0], jnp.zeros((pad,), F32)]).reshape(1, LANES)
    ntiles = n // MOE_TILE
    nslots = 2 * MOE_TILE + N_EXPERTS * SEG_ROWS
    x1, x_sorted, route, seg_lens = _outproj(dil_outs, ob, sga, sgb, x2, mod3, g_ffn[0].reshape(1, dm),
                                             w_branch_a[0].astype(BF16), w_branch_b[0].astype(BF16),
                                             w_out[0].astype(BF16), wr, br, batch, seq, nslots)

    nblk = -(-(2 * n + ntiles * N_EXPERTS * (SEG_ROWS - 1)) // ROW_BLOCK) + N_EXPERTS
    lens = seg_lens[:, 0, :N_EXPERTS].astype(I32)
    tot = jnp.sum(lens, axis=0)
    padded = (tot + ROW_BLOCK - 1) // ROW_BLOCK * ROW_BLOCK
    pend = jnp.cumsum(padded)
    pstart = pend - padded
    chunk_tbl = _expert_chunk_table(lens, pstart, tot, pend, nblk, nslots)
    live_tbl = jnp.sum(lens, axis=1)

    y_sorted = _experts(pstart // ROW_BLOCK, padded // ROW_BLOCK, chunk_tbl, live_tbl, x_sorted,
                        *expert_w_bf, nslots, ntiles)
    out = _combine(live_tbl, route, x1, mod3, g_final.reshape(1, dm), y_sorted, seq, MOE_TILE, nslots)
    return out.reshape(batch, seq, dm)
```

```python
import functools

import jax
import jax.numpy as jnp
from jax import lax
from jax.experimental import pallas as pl
from jax.experimental.pallas import tpu as pltpu

F32 = jnp.float32
BF16 = jnp.bfloat16
I32 = jnp.int32

HEAD_DIM = 64
ROPE_THETA = 10000.0
RMS_EPS = 1e-6
NEG_INF = -1e30
Q_SCALE = HEAD_DIM ** -0.5
DILATIONS = (1, 4, 16)
DIL_HALF_WINDOW = 64
DIL_GROUP_W = 256
SWA_WINDOW = 128
N_GROUPS = 4
EXPERTS_PER_GROUP = 8
N_EXPERTS = 32
LANES = 128
ROW_BLOCK = 512
SEG_ROWS = 16
MOE_TILE = 512
SLOT_GROUP = 256
VMEM_LIMIT = 56 * 1024 * 1024


def _cparams(sem):
    return pltpu.CompilerParams(dimension_semantics=sem, vmem_limit_bytes=VMEM_LIMIT)


def _ada_kernel(c_ref, w_ref, b_ref, o_ref):
    c = c_ref[...]
    cs = c * jax.nn.sigmoid(c)
    o_ref[...] = jnp.dot(cs.astype(BF16), w_ref[...].astype(BF16), preferred_element_type=F32) + b_ref[...]


def _ada(c8, w_ada, b_ada):
    d, n = w_ada.shape
    tn = 1536
    return pl.pallas_call(
        _ada_kernel,
        grid=(n // tn,),
        in_specs=[pl.BlockSpec((8, d), lambda j: (0, 0)),
                  pl.BlockSpec((d, tn), lambda j: (0, j)),
                  pl.BlockSpec((1, tn), lambda j: (0, j))],
        out_specs=pl.BlockSpec((8, tn), lambda j: (0, j)),
        out_shape=jax.ShapeDtypeStruct((8, n), F32),
        compiler_params=_cparams(("arbitrary",)),
        name="ada",
    )(c8, w_ada, b_ada)


def _rms_mod(x, g, shift, scale):
    ms = jnp.mean(x * x, axis=-1, keepdims=True)
    return (x * lax.rsqrt(ms + RMS_EPS)) * (g * (1.0 + scale)) + shift


def _inproj_kernel(x_ref, mod_ref, g_ref, ang_ref, sign_ref, w_ref, eg_ref, eu_ref, ed_ref,
                   q0_ref, k0_ref, v0_ref, q1_ref, k1_ref, v1_ref, q2_ref, k2_ref, v2_ref,
                   qb_ref, kb_ref, vb_ref, sga_ref, sgb_ref, egb_ref, eub_ref, edb_ref, stg_ref, *, tm, n_exp):
    def cast_expert():
        egb_ref[...] = eg_ref[...].astype(BF16)
        eub_ref[...] = eu_ref[...].astype(BF16)
        edb_ref[...] = ed_ref[...].astype(BF16)

    if n_exp is None:
        cast_expert()
    else:
        pl.when(pl.program_id(0) < n_exp)(cast_expert)

    h = _rms_mod(x_ref[...], g_ref[...], mod_ref[0, 0:1, :], mod_ref[0, 1:2, :])
    hb = h.astype(BF16)
    ang = ang_ref[...]
    cos = jnp.cos(ang)
    sin = jnp.sin(ang) * sign_ref[...]
    lane = lax.broadcasted_iota(I32, (tm, LANES), 1)
    first_half = (lane & 32) == 0
    low = lane < 64

    def proj(c0, width):
        return jnp.dot(hb, w_ref[:, c0:c0 + width], preferred_element_type=F32)

    def rope(t):
        rot = jnp.where(first_half, pltpu.roll(t, 96, 1), pltpu.roll(t, 32, 1))
        return t * cos + rot * sin

    def rope256(p):
        return jnp.concatenate([rope(p[:, :LANES]), rope(p[:, LANES:])], axis=1)

    def store_group(ref, val, d):
        if d == 1:
            ref[0, 0] = val.astype(BF16)
        else:
            for c in range(2):
                stg_ref[c] = val[:, c * LANES:(c + 1) * LANES]
            for r in range(d):
                for c in range(2):
                    ref[0, r, :, c * LANES:(c + 1) * LANES] = (
                        stg_ref[c, pl.ds(r, tm // d, stride=d), :].astype(BF16))

    q_refs = (q0_ref, q1_ref, q2_ref)
    k_refs = (k0_ref, k1_ref, k2_ref)
    v_refs = (v0_ref, v1_ref, v2_ref)
    for g, d in enumerate(DILATIONS):
        store_group(q_refs[g], rope256(proj(g * 256, 256)) * Q_SCALE, d)
        store_group(k_refs[g], rope256(proj(768 + g * 256, 256)), d)
        store_group(v_refs[g], proj(1536 + g * 256, 256), d)
    for j in range(2):
        qb_ref[:, j * 256:(j + 1) * 256] = (rope256(proj(2304 + j * 256, 256)) * Q_SCALE).astype(BF16)
    kv = proj(2816, 256)
    kb = rope(kv[:, :LANES])
    vb = kv[:, LANES:]
    kb_sw = pltpu.roll(kb, 64, 1)
    vb_sw = pltpu.roll(vb, 64, 1)
    kb_ref[:, :LANES] = jnp.where(low, kb, kb_sw).astype(BF16)
    kb_ref[:, LANES:] = jnp.where(low, kb_sw, kb).astype(BF16)
    vb_ref[:, :LANES] = jnp.where(low, vb, vb_sw).astype(BF16)
    vb_ref[:, LANES:] = jnp.where(low, vb_sw, vb).astype(BF16)
    for j in range(4):
        sga_ref[:, j * 256:(j + 1) * 256] = jax.nn.sigmoid(proj(3072 + j * 256, 256)).astype(BF16)
        sgb_ref[:, j * 256:(j + 1) * 256] = jax.nn.sigmoid(proj(4096 + j * 256, 256)).astype(BF16)


def _inproj(x2, mod3, g_mix, angles, sign, w_in_bf, expert_w, batch, seq):
    n, dm = x2.shape
    tm = 512
    tpb = seq // tm
    grid = (n // tm,)
    row = lambda i: (i, 0)
    strided_specs, strided_shapes = [], []
    for d in DILATIONS:
        for _ in range(3):
            strided_specs.append(pl.BlockSpec((1, d, tm // d, DIL_GROUP_W), lambda i: (i // tpb, 0, i % tpb, 0)))
            strided_shapes.append(jax.ShapeDtypeStruct((batch, d, seq // d, DIL_GROUP_W), BF16))
    out_specs = strided_specs + [
        pl.BlockSpec((tm, 512), row), pl.BlockSpec((tm, 256), row), pl.BlockSpec((tm, 256), row),
        pl.BlockSpec((tm, dm), row), pl.BlockSpec((tm, dm), row)]
    out_shapes = strided_shapes + [
        jax.ShapeDtypeStruct((n, 512), BF16), jax.ShapeDtypeStruct((n, 256), BF16),
        jax.ShapeDtypeStruct((n, 256), BF16), jax.ShapeDtypeStruct((n, dm), BF16),
        jax.ShapeDtypeStruct((n, dm), BF16)]
    n_exp = expert_w[0].shape[0]
    assert n // tm >= n_exp, "one expert's weights are cast per grid step"
    emap = lambda i: (jnp.minimum(i, n_exp - 1), 0, 0)
    expert_specs = [pl.BlockSpec((1,) + w.shape[1:], emap) for w in expert_w]
    return pl.pallas_call(
        functools.partial(_inproj_kernel, tm=tm, n_exp=None if n // tm == n_exp else n_exp),
        grid=grid,
        in_specs=[pl.BlockSpec((tm, dm), row),
                  pl.BlockSpec((1, 6, dm), lambda i: (i // tpb, 0, 0)),
                  pl.BlockSpec((1, dm), lambda i: (0, 0)),
                  pl.BlockSpec((tm, LANES), row),
                  pl.BlockSpec((1, LANES), lambda i: (0, 0)),
                  pl.BlockSpec(w_in_bf.shape, lambda i: (0, 0), pipeline_mode=pl.Buffered(1))] + expert_specs,
        out_specs=out_specs + expert_specs,
        out_shape=out_shapes + [jax.ShapeDtypeStruct(w.shape, BF16) for w in expert_w],
        scratch_shapes=[pltpu.VMEM((2, tm, LANES), F32)],
        compiler_params=_cparams(("arbitrary",)),
        name="inproj",
    )(x2, mod3, g_mix, angles, sign, w_in_bf, *expert_w)


def _split_heads(q2, low):
    zero = jnp.zeros_like(q2)
    return jnp.concatenate([jnp.where(low, q2, zero), jnp.where(low, zero, q2)], axis=0)


def _band_softmax(qst, k2, v2, bias, sinks):
    s = lax.dot_general(qst, k2, (((1,), (1,)), ((), ())), preferred_element_type=F32)
    s = s + bias
    rows, tk = s.shape
    m = jnp.max(s, axis=-1, keepdims=True)
    if sinks is not None:
        seg = rows // len(sinks)
        m = jnp.concatenate([jnp.maximum(m[h * seg:(h + 1) * seg], sk) for h, sk in enumerate(sinks)], axis=0)
    m = jnp.broadcast_to(m, (rows, LANES))
    e = jnp.concatenate([jnp.exp(s[:, c * LANES:(c + 1) * LANES] - m) for c in range(tk // LANES)], axis=1)
    v_ones = jnp.concatenate([v2, jnp.ones((tk, LANES), BF16)], axis=1)
    od = jnp.dot(e.astype(BF16), v_ones, preferred_element_type=F32)
    o, den = od[:, :LANES], od[:, LANES:]
    if sinks is not None:
        den = jnp.concatenate([den[h * seg:(h + 1) * seg] + jnp.exp(sk - m[h * seg:(h + 1) * seg])
                               for h, sk in enumerate(sinks)], axis=0)
    return o / den, m, den


def _fill_band_bias(bias_ref, tq, window):
    rows, tk = bias_ref.shape[1:]
    row = lax.broadcasted_iota(I32, (rows, tk), 0) & (tq - 1)
    col = lax.broadcasted_iota(I32, (rows, tk), 1)
    for i in range(bias_ref.shape[0]):
        bias_ref[i] = jnp.where(jnp.abs(col - row - i * window) <= window, 0.0, NEG_INF)


def _dil_kernel(q_ref, k_ref, v_ref, o_ref, l_ref, bias_ref, *, length, tq, tk):
    low = lax.broadcasted_iota(I32, (tq, LANES), 1) < 64
    _fill_band_bias(bias_ref, tq, DIL_HALF_WINDOW)
    nq = length // tq

    def body(j, carry):
        r = j // nq
        qs = pl.multiple_of((j % nq) * tq, tq)
        ks = pl.multiple_of(jnp.clip(qs - DIL_HALF_WINDOW, 0, length - tk), DIL_HALF_WINDOW)
        which = (qs - ks) // DIL_HALF_WINDOW
        for c in range(DIL_GROUP_W // LANES):
            cs = slice(c * LANES, (c + 1) * LANES)
            qst = _split_heads(q_ref[r, pl.ds(qs, tq), cs], low)
            o, m, den = _band_softmax(qst, k_ref[r, pl.ds(ks, tk), cs], v_ref[r, pl.ds(ks, tk), cs],
                                      bias_ref[which], None)
            lse = m + jnp.log(den)
            o_ref[r, pl.ds(qs, tq), cs] = jnp.where(low, o[:tq], o[tq:]).astype(BF16)
            l_ref[r, pl.ds(qs, tq), cs] = jnp.where(low, lse[:tq], lse[tq:])
        return carry

    lax.fori_loop(0, q_ref.shape[0] * nq, body, 0, unroll=8)


def _dil_attention(q, k, v):
    batch, d, length, w = q.shape
    tq, tk = 128, 256
    spec = pl.BlockSpec((None, d, length, w), lambda b: (b, 0, 0, 0))
    return pl.pallas_call(
        functools.partial(_dil_kernel, length=length, tq=tq, tk=tk),
        grid=(batch,),
        in_specs=[spec, spec, spec],
        out_specs=[spec, spec],
        out_shape=[jax.ShapeDtypeStruct(q.shape, BF16), jax.ShapeDtypeStruct(q.shape, F32)],
        scratch_shapes=[pltpu.VMEM((3, 2 * tq, tk), F32)],
        compiler_params=_cparams(("arbitrary",)),
        name=f"dil{d}",
    )(q, k, v)


def _swa_kernel(sink_ref, q_ref, k_ref, v_ref, o_ref, bias_ref, *, length, tq, tk):
    low = lax.broadcasted_iota(I32, (tq, LANES), 1) < 64
    nblk = q_ref.shape[1] // LANES
    _fill_band_bias(bias_ref, tq, SWA_WINDOW)

    def body(j, carry):
        qs = pl.multiple_of(j * tq, tq)
        ks = pl.multiple_of(jnp.clip(qs - SWA_WINDOW, 0, length - tk), SWA_WINDOW)
        bias = bias_ref[(qs - ks) // SWA_WINDOW]
        for b in range(nblk):
            cs = slice((b // 2) * LANES, (b // 2 + 1) * LANES)
            bs = slice(b * LANES, (b + 1) * LANES)
            qst = _split_heads(q_ref[pl.ds(qs, tq), bs], low)
            sinks = (sink_ref[2 * b], sink_ref[2 * b + 1])
            o, _, _ = _band_softmax(qst, k_ref[pl.ds(ks, tk), cs], v_ref[pl.ds(ks, tk), cs], bias, sinks)
            o_ref[pl.ds(qs, tq), bs] = jnp.where(low, o[:tq], o[tq:]).astype(BF16)
        return carry

    lax.fori_loop(0, length // tq, body, 0, unroll=8)


def _swa_attention(sink, q, k, v):
    batch, length, qw = q.shape
    tq, tk = 128, 384
    return pl.pallas_call(
        functools.partial(_swa_kernel, length=length, tq=tq, tk=tk),
        grid=(batch,),
        in_specs=[pl.BlockSpec(memory_space=pltpu.SMEM),
                  pl.BlockSpec((None, length, qw), lambda b: (b, 0, 0)),
                  pl.BlockSpec((None, length, k.shape[2]), lambda b: (b, 0, 0)),
                  pl.BlockSpec((None, length, v.shape[2]), lambda b: (b, 0, 0))],
        out_specs=pl.BlockSpec((None, length, qw), lambda b: (b, 0, 0)),
        out_shape=jax.ShapeDtypeStruct(q.shape, BF16),
        scratch_shapes=[pltpu.VMEM((3, 2 * tq, tk), F32)],
        compiler_params=_cparams(("arbitrary",)),
        name="swa",
    )(sink, q, k, v)


def _route_rows(logits):
    lane = lax.broadcasted_iota(I32, logits.shape, 1).astype(F32)
    big = 1e9
    is_g = lane < N_GROUPS
    gl = jnp.where(is_g, logits, NEG_INF)
    gmax = jnp.max(gl, axis=-1, keepdims=True)
    gsel = jnp.min(jnp.where(is_g & (gl == gmax), lane, big), axis=-1, keepdims=True)
    gw = 1.0 / jnp.sum(jnp.where(is_g, jnp.exp(gl - gmax), 0.0), axis=-1, keepdims=True)
    e_lo = N_GROUPS + gsel * EXPERTS_PER_GROUP
    in_grp = (lane >= e_lo) & (lane < e_lo + EXPERTS_PER_GROUP)
    el = jnp.where(in_grp, logits, NEG_INF)
    m1 = jnp.max(el, axis=-1, keepdims=True)
    i1 = jnp.min(jnp.where(in_grp & (el == m1), lane, big), axis=-1, keepdims=True)
    el2 = jnp.where(lane == i1, NEG_INF, el)
    m2 = jnp.max(el2, axis=-1, keepdims=True)
    i2 = jnp.min(jnp.where(in_grp & (lane != i1) & (el2 == m2), lane, big), axis=-1, keepdims=True)
    t = jnp.exp(m2 - m1)
    tw1 = gw / (1.0 + t)
    tw2 = gw * t / (1.0 + t)
    out = jnp.where(lane == 0, tw1, 0.0)
    out = jnp.where(lane == 1, tw2, out)
    out = jnp.where(lane == 2, i1 - N_GROUPS, out)
    return jnp.where(lane == 3, i2 - N_GROUPS, out)


def _outproj_kernel(o0_ref, l0_ref, o1_ref, l1_ref, o2_ref, l2_ref, ob_ref, sga_ref, sgb_ref, x_ref,
                    mod_ref, g_ref, wa_ref, wb_ref, wo_ref, wr_ref, br_ref,
                    x1_ref, xs_ref, route_ref, len_ref,
                    so1_ref, sl1_ref, so2_ref, sl2_ref, h2_ref, *, tm, sub, group):
    ntiles = pl.num_programs(0) - 1

    @pl.when(pl.program_id(0) == ntiles)
    def _():
        xs_ref[...] = jnp.zeros_like(xs_ref)

    pl.when(pl.program_id(0) < ntiles)(functools.partial(
        _outproj_tile, o0_ref, l0_ref, o1_ref, l1_ref, o2_ref, l2_ref, ob_ref, sga_ref, sgb_ref, x_ref,
        mod_ref, g_ref, wa_ref, wb_ref, wo_ref, wr_ref, br_ref, x1_ref, xs_ref, route_ref, len_ref,
        so1_ref, sl1_ref, so2_ref, sl2_ref, h2_ref, tm=tm, sub=sub, group=group))


def _outproj_tile(o0_ref, l0_ref, o1_ref, l1_ref, o2_ref, l2_ref, ob_ref, sga_ref, sgb_ref, x_ref,
                  mod_ref, g_ref, wa_ref, wb_ref, wo_ref, wr_ref, br_ref,
                  x1_ref, xs_ref, route_ref, len_ref,
                  so1_ref, sl1_ref, so2_ref, sl2_ref, h2_ref, *, tm, sub, group):
    dm = x_ref.shape[1]
    for (o_ref, l_ref, so_ref, sl_ref, d) in ((o1_ref, l1_ref, so1_ref, sl1_ref, DILATIONS[1]),
                                              (o2_ref, l2_ref, so2_ref, sl2_ref, DILATIONS[2])):
        for r in range(d):
            for c in range(2):
                cs = slice(c * LANES, (c + 1) * LANES)
                so_ref[c, pl.ds(r, tm // d, stride=d), :] = o_ref[0, r, :, cs].astype(F32)
                sl_ref[c, pl.ds(r, tm // d, stride=d), :] = l_ref[0, r, :, cs]
    mr = lax.broadcasted_iota(I32, (LANES, LANES), 0)
    mc = lax.broadcasted_iota(I32, (LANES, LANES), 1)
    move_hi = jnp.where(((mr < 2) & (mc == 2 * mr)) | ((mr >= 2) & (mr < 4) & (mc == mr + 2)), 1.0, 0.0).astype(BF16)
    move_lo = jnp.where((mr < 2) & (mc == 2 * mr + 1), 1.0, 0.0).astype(BF16)
    for t in range(tm // sub):
        rs = slice(t * sub, (t + 1) * sub)
        both = lambda ref: jnp.concatenate([ref[0, rs, :], ref[1, rs, :]], axis=1)
        o0, l0 = o0_ref[0, 0, rs, :].astype(F32), l0_ref[0, 0, rs, :]
        o1, l1, o2, l2 = both(so1_ref), both(sl1_ref), both(so2_ref), both(sl2_ref)
        mx = jnp.maximum(jnp.maximum(l0, l1), l2)
        w0, w1, w2 = jnp.exp(l0 - mx), jnp.exp(l1 - mx), jnp.exp(l2 - mx)
        o_a = (w0 * o0 + w1 * o1 + w2 * o2) / (w0 + w1 + w2)
        y_a = jnp.dot(o_a.astype(BF16), wa_ref[...], preferred_element_type=F32)
        y_b = jnp.dot(ob_ref[rs, :], wb_ref[...], preferred_element_type=F32)
        merged = sga_ref[rs, :].astype(F32) * y_a + sgb_ref[rs, :].astype(F32) * y_b
        mix = jnp.dot(merged.astype(BF16), wo_ref[...], preferred_element_type=F32)
        x1 = x_ref[rs, :] + mod_ref[0, 2:3, :] * mix
        x1_ref[rs, :] = x1
        h2 = _rms_mod(x1, g_ref[...], mod_ref[0, 3:4, :], mod_ref[0, 4:5, :]).astype(BF16)
        h2_ref[rs, :dm] = h2
        logits = jnp.dot(h2, wr_ref[...], preferred_element_type=F32) + br_ref[...]
        rt = _route_rows(logits)
        route_ref[rs, :] = rt
        hi = rt.astype(BF16)
        lo = (rt - hi.astype(F32)).astype(BF16)
        aux = (jnp.dot(hi, move_hi, preferred_element_type=F32) + jnp.dot(lo, move_lo, preferred_element_type=F32))
        h2_ref[rs, dm:] = aux.astype(BF16)

    part = route_ref[...]
    e1, e2 = part[:, 2:3], part[:, 3:4]
    lane = lax.broadcasted_iota(I32, (tm, LANES), 1).astype(F32)
    onehot = jnp.where((lane == e1) | (lane == e2), 1.0, 0.0)
    rr = lax.broadcasted_iota(I32, (tm, tm), 0)
    cc = lax.broadcasted_iota(I32, (tm, tm), 1)
    tri = jnp.where(rr > cc, 1.0, 0.0).astype(BF16)
    prefix = jnp.dot(tri, onehot.astype(BF16), preferred_element_type=F32)
    cnt = jnp.sum(onehot, axis=0, keepdims=True)
    seg_len = jnp.ceil(cnt * (1.0 / SEG_ROWS)) * SEG_ROWS
    ur = lax.broadcasted_iota(I32, (LANES, LANES), 0)
    uc = lax.broadcasted_iota(I32, (LANES, LANES), 1)
    upper = jnp.where(ur < uc, 1.0, 0.0).astype(BF16)
    seg_off = jnp.dot(jnp.broadcast_to(seg_len, (8, LANES)).astype(BF16), upper, preferred_element_type=F32)[0:1, :]
    slot_map = seg_off + prefix
    s1 = jnp.sum(jnp.where(lane == e1, slot_map, 0.0), axis=-1, keepdims=True)
    s2 = jnp.sum(jnp.where(lane == e2, slot_map, 0.0), axis=-1, keepdims=True)
    len_ref[0] = jnp.broadcast_to(seg_len, (8, LANES))
    route = jnp.where(lane == 2, s1, jnp.where(lane == 3, s2, part))
    route_ref[...] = route

    pr = lax.broadcasted_iota(I32, (8, LANES), 0)
    pc = lax.broadcasted_iota(I32, (8, LANES), 1)
    lane_pick = jnp.where(pc == pr + 2, 1.0, 0.0)
    slots_t = lax.dot_general(lane_pick, route, (((1,), (1,)), ((), ())), preferred_element_type=F32,
                              precision=lax.Precision.HIGHEST)
    live = jnp.sum(seg_len).astype(I32)

    def sort_rows(lo, rows):
        srow = (lax.broadcasted_iota(I32, (rows, tm), 0) + lo).astype(F32)
        pick = jnp.where((srow == slots_t[0:1, :]) | (srow == slots_t[1:2, :]), 1.0, 0.0).astype(BF16)
        xs_ref[lo:lo + rows, :] = jnp.dot(pick, h2_ref[...], preferred_element_type=F32).astype(BF16)

    def clear_rows(lo, rows):
        xs_ref[lo:lo + rows, :] = jnp.zeros((rows, xs_ref.shape[1]), BF16)

    head = xs_ref.shape[0] - group
    sort_rows(0, head)
    pl.when(live > head)(functools.partial(sort_rows, head, group))
    pl.when(live <= head)(functools.partial(clear_rows, head, group))


def _outproj(dil_outs, ob, sga, sgb, x2, mod3, g_ffn, wa, wb, wo, wr, br, batch, seq, nslots):
    n, dm = x2.shape
    tm = MOE_TILE
    tpb = seq // tm
    ntiles = n // tm
    tile = lambda i: jnp.minimum(i, ntiles - 1)
    row = lambda i: (tile(i), 0)
    const = lambda i: (0, 0)
    in_specs = []
    args = []
    for (o, l), d in zip(dil_outs, DILATIONS):
        spec = pl.BlockSpec((1, d, tm // d, DIL_GROUP_W), lambda i: (tile(i) // tpb, 0, tile(i) % tpb, 0))
        in_specs += [spec, spec]
        args += [o, l]
    in_specs += [pl.BlockSpec((tm, ob.shape[1]), row), pl.BlockSpec((tm, dm), row), pl.BlockSpec((tm, dm), row),
                 pl.BlockSpec((tm, dm), row),
                 pl.BlockSpec((1, 6, dm), lambda i: (tile(i) // tpb, 0, 0)),
                 pl.BlockSpec((1, dm), const),
                 pl.BlockSpec(wa.shape, const), pl.BlockSpec(wb.shape, const), pl.BlockSpec(wo.shape, const),
                 pl.BlockSpec(wr.shape, const), pl.BlockSpec(br.shape, const)]
    args += [ob, sga, sgb, x2, mod3, g_ffn, wa, wb, wo, wr, br]
    width = dm + LANES
    return pl.pallas_call(
        functools.partial(_outproj_kernel, tm=tm, sub=512, group=SLOT_GROUP),
        grid=(ntiles + 1,),
        in_specs=in_specs,
        out_specs=[pl.BlockSpec((tm, dm), row), pl.BlockSpec((nslots, width), lambda i: (i, 0)),
                   pl.BlockSpec((tm, LANES), row), pl.BlockSpec((1, 8, LANES), lambda i: (tile(i), 0, 0))],
        out_shape=[jax.ShapeDtypeStruct((n, dm), F32), jax.ShapeDtypeStruct(((ntiles + 1) * nslots, width), BF16),
                   jax.ShapeDtypeStruct((n, LANES), F32), jax.ShapeDtypeStruct((ntiles, 8, LANES), F32)],
        scratch_shapes=[pltpu.VMEM((2, tm, LANES), F32)] * 4 + [pltpu.VMEM((tm, width), BF16)],
        compiler_params=_cparams(("arbitrary",)),
        name="outproj",
    )(*args)


def _wait_copies(count, copy):
    def wait_one(c, carry):
        copy.wait()
        return carry

    lax.fori_loop(0, count, wait_one, 0)


def _experts_kernel(first_ref, count_ref, chunk_ref, wg_ref, wu_ref, wd_ref, xs_hbm, ys_hbm,
                    xbuf, ybuf, zbuf, xsem, ysem, zsem, *, nblk):
    e = pl.program_id(0)
    last = pl.num_programs(0) - 1
    first = first_ref[e]
    count = count_ref[e]
    used = first_ref[last] + count_ref[last]
    dm = ybuf.shape[2]
    cpb = ROW_BLOCK // SEG_ROWS

    def chunk_rows(blk, j):
        return pl.ds(pl.multiple_of(chunk_ref[blk * cpb + j], SEG_ROWS), SEG_ROWS)

    def gather(blk):
        for j in range(cpb):
            pltpu.make_async_copy(xs_hbm.at[chunk_rows(blk, j), :], xbuf.at[blk & 1, pl.ds(j * SEG_ROWS, SEG_ROWS), :],
                                  xsem.at[blk & 1]).start(priority=1)

    def gather_done(blk):
        return pltpu.make_async_copy(xs_hbm.at[pl.ds(0, ROW_BLOCK), :], xbuf.at[blk & 1], xsem.at[blk & 1])

    def put(blk):
        rows = pl.ds(pl.multiple_of(blk * ROW_BLOCK, ROW_BLOCK), ROW_BLOCK)
        return pltpu.make_async_copy(ybuf.at[blk & 1], ys_hbm.at[rows, :], ysem.at[blk & 1])

    def zero_blk(b):
        rows = pl.ds(pl.multiple_of(b * ROW_BLOCK, ROW_BLOCK), ROW_BLOCK)
        return pltpu.make_async_copy(zbuf, ys_hbm.at[rows, :], zsem)

    @pl.when(e == 0)
    def _():
        zbuf[...] = jnp.zeros_like(zbuf)

        def start(b, carry):
            zero_blk(b).start()
            return carry

        lax.fori_loop(used, nblk, start, 0)

        @pl.when(used > 0)
        def _():
            gather(0)

    def body(b, carry):
        blk = first + b
        slot = blk & 1
        gather_done(blk).wait()

        @pl.when(blk + 1 < used)
        def _():
            gather(blk + 1)

        @pl.when(blk >= 2)
        def _():
            put(blk - 2).wait()

        xb = xbuf[slot, :, :dm]
        aux = xbuf[slot, :, dm:].astype(F32)
        w = jnp.where(aux[:, 4:5] == e.astype(F32), aux[:, 0:1] + aux[:, 1:2], aux[:, 2:3] + aux[:, 3:4])
        g = jnp.dot(xb, wg_ref[0], preferred_element_type=F32)
        u = jnp.dot(xb, wu_ref[0], preferred_element_type=F32)
        a = (g * jax.nn.sigmoid(g)) * u
        y = jnp.dot(a.astype(BF16), wd_ref[0], preferred_element_type=F32)
        ybuf[slot] = (y * w).astype(BF16)
        put(blk).start()
        return carry

    lax.fori_loop(0, count, body, 0)

    @pl.when(e == last)
    def _():
        @pl.when(used >= 2)
        def _():
            put(used - 2).wait()

        @pl.when(used >= 1)
        def _():
            put(used - 1).wait()

        _wait_copies(nblk - used, zero_blk(0))


def _experts(first_blk, count_blk, chunk_tbl, x_sorted, w_gate, w_up, w_down, nblk):
    rows, width = x_sorted.shape
    n_exp, dm, de = w_gate.shape
    wmap = lambda e, *_: (e, 0, 0)
    grid_spec = pltpu.PrefetchScalarGridSpec(
        num_scalar_prefetch=3,
        grid=(n_exp,),
        in_specs=[pl.BlockSpec((1, dm, de), wmap),
                  pl.BlockSpec((1, dm, de), wmap),
                  pl.BlockSpec((1, de, dm), wmap),
                  pl.BlockSpec(memory_space=pl.ANY)],
        out_specs=pl.BlockSpec(memory_space=pl.ANY),
        scratch_shapes=[pltpu.VMEM((2, ROW_BLOCK, width), BF16), pltpu.VMEM((2, ROW_BLOCK, dm), BF16),
                        pltpu.VMEM((ROW_BLOCK, dm), BF16),
                        pltpu.SemaphoreType.DMA((2,)), pltpu.SemaphoreType.DMA((2,)), pltpu.SemaphoreType.DMA(())],
    )
    return pl.pallas_call(
        functools.partial(_experts_kernel, nblk=nblk),
        grid_spec=grid_spec,
        out_shape=jax.ShapeDtypeStruct((nblk * ROW_BLOCK, dm), BF16),
        compiler_params=_cparams(("arbitrary",)),
        name="experts",
    )(first_blk, count_blk, chunk_tbl, w_gate, w_up, w_down, x_sorted)


def _slot_parts(tm, nslots, group):
    return [(0, 2 * tm)] + [(lo, group) for lo in range(2 * tm, nslots, group)]


def _combine_kernel(src_ref, live_ref, route_ref, x1_ref, mod_ref, g_ref, y_ref, o_ref, ys_ref, sem,
                    *, tm, nslots, group):
    i = pl.program_id(0)
    last = pl.num_programs(0) - 1
    slot = i & 1
    nchk = nslots // SEG_ROWS
    parts = _slot_parts(tm, nslots, group)

    def for_live_parts(tile, fn):
        for p, (lo, rows) in enumerate(parts):
            if p == 0:
                fn(lo, rows)
            else:
                pl.when(live_ref[tile] > lo)(functools.partial(fn, lo, rows))

    def gather(tile, s):
        def start(lo, rows):
            for c in range(lo // SEG_ROWS, (lo + rows) // SEG_ROWS):
                src = pl.multiple_of(src_ref[tile * nchk + c], SEG_ROWS)
                pltpu.make_async_copy(y_ref.at[pl.ds(src, SEG_ROWS), :],
                                      ys_ref.at[s, pl.ds(c * SEG_ROWS, SEG_ROWS), :], sem.at[s]).start()

        for_live_parts(tile, start)

    def wait(tile, s):
        for_live_parts(tile, lambda lo, rows: pltpu.make_async_copy(
            y_ref.at[pl.ds(0, rows), :], ys_ref.at[s, pl.ds(lo, rows), :], sem.at[s]).wait())

    @pl.when(i == 0)
    def _():
        gather(0, 0)

    nxt = jnp.minimum(i + 1, last)
    gather(nxt, 1 - slot)
    wait(i, slot)
    route = route_ref[...]

    def finish(rows):
        scol = lax.broadcasted_iota(I32, (tm, rows), 1).astype(F32)
        pick = jnp.where((scol == route[:, 2:3]) | (scol == route[:, 3:4]), 1.0, 0.0).astype(BF16)
        moe = jnp.dot(pick, ys_ref[slot, :rows, :], preferred_element_type=F32)
        x = x1_ref[...] + mod_ref[0, 5:6, :] * moe
        ms = jnp.mean(x * x, axis=-1, keepdims=True)
        o_ref[...] = (x * lax.rsqrt(ms + RMS_EPS)) * g_ref[...]

    ends = [lo + rows for lo, rows in parts]
    for p, end in enumerate(ends):
        above = live_ref[i] > (ends[p - 1] if p else -1)
        cond = above if p == len(ends) - 1 else above & (live_ref[i] <= end)
        pl.when(cond)(functools.partial(finish, end))

    @pl.when(i == last)
    def _():
        wait(nxt, 1 - slot)


def _combine(src_tbl, live_tbl, route, x1, mod3, g_final, y_rows, seq, tm, nslots):
    n, dm = x1.shape
    tpb = seq // tm
    grid_spec = pltpu.PrefetchScalarGridSpec(
        num_scalar_prefetch=2,
        grid=(n // tm,),
        in_specs=[pl.BlockSpec((tm, LANES), lambda i, *_: (i, 0)),
                  pl.BlockSpec((tm, dm), lambda i, *_: (i, 0)),
                  pl.BlockSpec((1, 6, dm), lambda i, *_: (i // tpb, 0, 0)),
                  pl.BlockSpec((1, dm), lambda i, *_: (0, 0)),
                  pl.BlockSpec(memory_space=pl.ANY)],
        out_specs=pl.BlockSpec((tm, dm), lambda i, *_: (i, 0)),
        scratch_shapes=[pltpu.VMEM((2, nslots, dm), BF16), pltpu.SemaphoreType.DMA((2,))],
    )
    return pl.pallas_call(
        functools.partial(_combine_kernel, tm=tm, nslots=nslots, group=SLOT_GROUP),
        grid_spec=grid_spec,
        out_shape=jax.ShapeDtypeStruct((n, dm), F32),
        compiler_params=_cparams(("arbitrary",)),
        name="combine",
    )(src_tbl, live_tbl, route, x1, mod3, g_final, y_rows)


def _rope_angles(positions):
    half = HEAD_DIM // 2
    inv_freq = ROPE_THETA ** (-jnp.arange(half, dtype=F32) * (2.0 / HEAD_DIM))
    freq = jnp.tile(inv_freq, LANES // half)
    sign = jnp.tile(jnp.concatenate([-jnp.ones((half,), F32), jnp.ones((half,), F32)]), LANES // HEAD_DIM)
    ang = positions.astype(F32).reshape(-1, 1) * freq
    return ang, sign.reshape(1, LANES)


def _expert_chunk_table(lens, pstart, tot, pend, nblk, nslots):
    ntiles = lens.shape[0]
    cpb = ROW_BLOCK // SEG_ROWS
    runs = lens // SEG_ROWS
    q = jnp.arange(nblk * cpb, dtype=I32)
    owner = jnp.minimum(jnp.sum((pend[None, :] // SEG_ROWS <= q[:, None]).astype(I32), axis=1), N_EXPERTS - 1)
    is_e = owner[:, None] == jnp.arange(N_EXPERTS, dtype=I32)[None, :]
    pick_e = lambda v: jnp.sum(jnp.where(is_e, v[None, :], 0), axis=1)
    off = q - pick_e(pstart // SEG_ROWS)
    in_run = (off < pick_e(tot // SEG_ROWS)) & (q < pend[-1] // SEG_ROWS)
    upto = jnp.cumsum(runs, axis=0)
    upto_e = jnp.sum(jnp.where(is_e[:, None, :], upto[None, :, :], 0), axis=2)
    tile = jnp.minimum(jnp.sum((upto_e <= off[:, None]).astype(I32), axis=1), ntiles - 1)
    is_t = tile[:, None] == jnp.arange(ntiles, dtype=I32)[None, :]
    run_slot0 = jnp.cumsum(runs, axis=1) - runs - (upto - runs)
    slot0 = jnp.sum(jnp.where(is_t[:, :, None] & is_e[:, None, :], run_slot0[None, :, :], 0), axis=(1, 2))
    live_row = tile * nslots + (slot0 + off) * SEG_ROWS
    spare_row = ntiles * nslots + (((q // cpb) % 2) * cpb + q % cpb) * SEG_ROWS
    return jnp.where(in_run, live_row, spare_row)


def kernel(x, c, positions, w_ada, b_ada, g_mix, w_in, sink_logits, w_branch_a, w_branch_b, w_out, g_ffn,
           w_group, b_group, w_route, b_route, w_expert_gate, w_expert_up, w_expert_down, g_final):
    batch, seq, dm = x.shape
    n = batch * seq
    assert w_ada.shape[0] == 1, "one layer"
    x2 = x.reshape(n, dm)

    c8 = jnp.pad(c, ((0, 8 - batch), (0, 0)))
    mod = _ada(c8, w_ada[0], b_ada[0].reshape(1, -1))
    mod3 = mod[:batch].reshape(batch, 6, dm)

    angles, sign = _rope_angles(positions)
    outs = _inproj(x2, mod3, g_mix[0].reshape(1, dm), angles, sign, w_in[0].astype(BF16),
                   (w_expert_gate[0], w_expert_up[0], w_expert_down[0]), batch, seq)
    qkv = outs[:9]
    qb, kb, vb, sga, sgb = outs[9:14]
    expert_w_bf = outs[14:]

    dil_outs = [_dil_attention(qkv[3 * g], qkv[3 * g + 1], qkv[3 * g + 2]) for g in range(len(DILATIONS))]
    ob = _swa_attention(sink_logits[0], qb.reshape(batch, seq, -1), kb.reshape(batch, seq, -1),
                        vb.reshape(batch, seq, -1)).reshape(n, -1)

    pad = LANES - N_GROUPS - N_EXPERTS
    wr = jnp.concatenate([w_group[0], w_route[0], jnp.zeros((dm, pad), F32)], axis=1).astype(BF16)
    br = jnp.concatenate([b_group[0], b_route[0], jnp.zeros((pad,), F32)]).reshape(1, LANES)
    ntiles = n // MOE_TILE
    nslots = 2 * MOE_TILE + N_EXPERTS * SEG_ROWS
    x1, x_sorted, route, seg_lens = _outproj(dil_outs, ob, sga, sgb, x2, mod3, g_ffn[0].reshape(1, dm),
                                             w_branch_a[0].astype(BF16), w_branch_b[0].astype(BF16),
                                             w_out[0].astype(BF16), wr, br, batch, seq, nslots)

    nblk = -(-(2 * n + ntiles * N_EXPERTS * (SEG_ROWS - 1)) // ROW_BLOCK) + N_EXPERTS
    lens = seg_lens[:, 0, :N_EXPERTS].astype(I32)
    tot = jnp.sum(lens, axis=0)
    padded = (tot + ROW_BLOCK - 1) // ROW_BLOCK * ROW_BLOCK
    pend = jnp.cumsum(padded)
    pstart = pend - padded
    chunk_tbl = _expert_chunk_table(lens, pstart, tot, pend, nblk, nslots)
    live_tbl = jnp.sum(lens, axis=1)

    nchk = nslots // SEG_ROWS
    base = pstart[None, :] + jnp.cumsum(lens, axis=0) - lens
    run_end = jnp.cumsum(lens // SEG_ROWS, axis=1)
    chunk = jnp.arange(nchk, dtype=I32)
    owner = jnp.sum((run_end[:, None, :] <= chunk[None, :, None]).astype(I32), axis=2)
    is_owner = owner[:, :, None] == jnp.arange(N_EXPERTS, dtype=I32)[None, None, :]
    run_row0 = base - (run_end - lens // SEG_ROWS) * SEG_ROWS
    row = jnp.sum(jnp.where(is_owner, run_row0[:, None, :], 0), axis=2) + chunk[None, :] * SEG_ROWS
    src_tbl = jnp.where(owner < N_EXPERTS, row, chunk[None, :] * SEG_ROWS).reshape(-1)

    y_rows = _experts(pstart // ROW_BLOCK, padded // ROW_BLOCK, chunk_tbl, x_sorted, *expert_w_bf, nblk)
    out = _combine(src_tbl, live_tbl, route, x1, mod3, g_final.reshape(1, dm), y_rows, seq, MOE_TILE, nslots)
    return out.reshape(batch, seq, dm)
```

```python
import functools

import jax
import jax.numpy as jnp
from jax import lax
from jax.experimental import pallas as pl
from jax.experimental.pallas import tpu as pltpu

F32 = jnp.float32
BF16 = jnp.bfloat16
I32 = jnp.int32

HEAD_DIM = 64
ROPE_THETA = 10000.0
RMS_EPS = 1e-6
NEG_INF = -1e30
Q_SCALE = HEAD_DIM ** -0.5
DILATIONS = (1, 4, 16)
DIL_HALF_WINDOW = 64
DIL_GROUP_W = 256
SWA_WINDOW = 128
N_GROUPS = 4
EXPERTS_PER_GROUP = 8
N_EXPERTS = 32
LANES = 128
ROW_BLOCK = 512
SEG_ROWS = 16
MOE_TILE = 512
SLOT_GROUP = 256
VMEM_LIMIT = 56 * 1024 * 1024


def _cparams(sem):
    return pltpu.CompilerParams(dimension_semantics=sem, vmem_limit_bytes=VMEM_LIMIT)


def _ada_kernel(c_ref, w_ref, b_ref, o_ref):
    c = c_ref[...]
    cs = c * jax.nn.sigmoid(c)
    o_ref[...] = jnp.dot(cs.astype(BF16), w_ref[...].astype(BF16), preferred_element_type=F32) + b_ref[...]


def _ada(c8, w_ada, b_ada):
    d, n = w_ada.shape
    tn = 1536
    return pl.pallas_call(
        _ada_kernel,
        grid=(n // tn,),
        in_specs=[pl.BlockSpec((8, d), lambda j: (0, 0)),
                  pl.BlockSpec((d, tn), lambda j: (0, j)),
                  pl.BlockSpec((1, tn), lambda j: (0, j))],
        out_specs=pl.BlockSpec((8, tn), lambda j: (0, j)),
        out_shape=jax.ShapeDtypeStruct((8, n), F32),
        compiler_params=_cparams(("arbitrary",)),
        name="ada",
    )(c8, w_ada, b_ada)


def _rms_mod(x, g, shift, scale):
    ms = jnp.mean(x * x, axis=-1, keepdims=True)
    return (x * lax.rsqrt(ms + RMS_EPS)) * (g * (1.0 + scale)) + shift


def _inproj_kernel(x_ref, mod_ref, g_ref, ang_ref, sign_ref, w_ref, eg_ref, eu_ref, ed_ref,
                   q0_ref, k0_ref, v0_ref, q1_ref, k1_ref, v1_ref, q2_ref, k2_ref, v2_ref,
                   qb_ref, kb_ref, vb_ref, sga_ref, sgb_ref, egb_ref, eub_ref, edb_ref, stg_ref, *, tm, n_exp):
    def cast_expert():
        egb_ref[...] = eg_ref[...].astype(BF16)
        eub_ref[...] = eu_ref[...].astype(BF16)
        edb_ref[...] = ed_ref[...].astype(BF16)

    if n_exp is None:
        cast_expert()
    else:
        pl.when(pl.program_id(0) < n_exp)(cast_expert)

    h = _rms_mod(x_ref[...], g_ref[...], mod_ref[0, 0:1, :], mod_ref[0, 1:2, :])
    hb = h.astype(BF16)
    ang = ang_ref[...]
    cos = jnp.cos(ang)
    sin = jnp.sin(ang) * sign_ref[...]
    lane = lax.broadcasted_iota(I32, (tm, LANES), 1)
    first_half = (lane & 32) == 0
    low = lane < 64

    def proj(c0, width):
        return jnp.dot(hb, w_ref[:, c0:c0 + width], preferred_element_type=F32)

    def rope(t):
        rot = jnp.where(first_half, pltpu.roll(t, 96, 1), pltpu.roll(t, 32, 1))
        return t * cos + rot * sin

    def rope256(p):
        return jnp.concatenate([rope(p[:, :LANES]), rope(p[:, LANES:])], axis=1)

    def store_group(ref, val, d):
        if d == 1:
            ref[0, 0] = val.astype(BF16)
        else:
            for c in range(2):
                stg_ref[c] = val[:, c * LANES:(c + 1) * LANES]
            for r in range(d):
                for c in range(2):
                    ref[0, r, :, c * LANES:(c + 1) * LANES] = (
                        stg_ref[c, pl.ds(r, tm // d, stride=d), :].astype(BF16))

    q_refs = (q0_ref, q1_ref, q2_ref)
    k_refs = (k0_ref, k1_ref, k2_ref)
    v_refs = (v0_ref, v1_ref, v2_ref)
    for g, d in enumerate(DILATIONS):
        store_group(q_refs[g], rope256(proj(g * 256, 256)) * Q_SCALE, d)
        store_group(k_refs[g], rope256(proj(768 + g * 256, 256)), d)
        store_group(v_refs[g], proj(1536 + g * 256, 256), d)
    for j in range(2):
        qb_ref[:, j * 256:(j + 1) * 256] = (rope256(proj(2304 + j * 256, 256)) * Q_SCALE).astype(BF16)
    kv = proj(2816, 256)
    kb = rope(kv[:, :LANES])
    vb = kv[:, LANES:]
    kb_sw = pltpu.roll(kb, 64, 1)
    vb_sw = pltpu.roll(vb, 64, 1)
    kb_ref[:, :LANES] = jnp.where(low, kb, kb_sw).astype(BF16)
    kb_ref[:, LANES:] = jnp.where(low, kb_sw, kb).astype(BF16)
    vb_ref[:, :LANES] = jnp.where(low, vb, vb_sw).astype(BF16)
    vb_ref[:, LANES:] = jnp.where(low, vb_sw, vb).astype(BF16)
    for j in range(4):
        sga_ref[:, j * 256:(j + 1) * 256] = jax.nn.sigmoid(proj(3072 + j * 256, 256)).astype(BF16)
        sgb_ref[:, j * 256:(j + 1) * 256] = jax.nn.sigmoid(proj(4096 + j * 256, 256)).astype(BF16)


def _inproj(x2, mod3, g_mix, angles, sign, w_in_bf, expert_w, batch, seq):
    n, dm = x2.shape
    tm = 512
    tpb = seq // tm
    grid = (n // tm,)
    row = lambda i: (i, 0)
    strided_specs, strided_shapes = [], []
    for d in DILATIONS:
        for _ in range(3):
            strided_specs.append(pl.BlockSpec((1, d, tm // d, DIL_GROUP_W), lambda i: (i // tpb, 0, i % tpb, 0)))
            strided_shapes.append(jax.ShapeDtypeStruct((batch, d, seq // d, DIL_GROUP_W), BF16))
    out_specs = strided_specs + [
        pl.BlockSpec((tm, 512), row), pl.BlockSpec((tm, 256), row), pl.BlockSpec((tm, 256), row),
        pl.BlockSpec((tm, dm), row), pl.BlockSpec((tm, dm), row)]
    out_shapes = strided_shapes + [
        jax.ShapeDtypeStruct((n, 512), BF16), jax.ShapeDtypeStruct((n, 256), BF16),
        jax.ShapeDtypeStruct((n, 256), BF16), jax.ShapeDtypeStruct((n, dm), BF16),
        jax.ShapeDtypeStruct((n, dm), BF16)]
    n_exp = expert_w[0].shape[0]
    assert n // tm >= n_exp, "one expert's weights are cast per grid step"
    emap = lambda i: (jnp.minimum(i, n_exp - 1), 0, 0)
    expert_specs = [pl.BlockSpec((1,) + w.shape[1:], emap) for w in expert_w]
    return pl.pallas_call(
        functools.partial(_inproj_kernel, tm=tm, n_exp=None if n // tm == n_exp else n_exp),
        grid=grid,
        in_specs=[pl.BlockSpec((tm, dm), row),
                  pl.BlockSpec((1, 6, dm), lambda i: (i // tpb, 0, 0)),
                  pl.BlockSpec((1, dm), lambda i: (0, 0)),
                  pl.BlockSpec((tm, LANES), row),
                  pl.BlockSpec((1, LANES), lambda i: (0, 0)),
                  pl.BlockSpec(w_in_bf.shape, lambda i: (0, 0), pipeline_mode=pl.Buffered(1))] + expert_specs,
        out_specs=out_specs + expert_specs,
        out_shape=out_shapes + [jax.ShapeDtypeStruct(w.shape, BF16) for w in expert_w],
        scratch_shapes=[pltpu.VMEM((2, tm, LANES), F32)],
        compiler_params=_cparams(("arbitrary",)),
        name="inproj",
    )(x2, mod3, g_mix, angles, sign, w_in_bf, *expert_w)


def _split_heads(q2, low):
    zero = jnp.zeros_like(q2)
    return jnp.concatenate([jnp.where(low, q2, zero), jnp.where(low, zero, q2)], axis=0)


def _band_softmax(qst, k2, v2, bias, sinks):
    s = lax.dot_general(qst, k2, (((1,), (1,)), ((), ())), preferred_element_type=F32)
    s = s + bias
    rows, tk = s.shape
    m = jnp.max(s, axis=-1, keepdims=True)
    if sinks is not None:
        seg = rows // len(sinks)
        m = jnp.concatenate([jnp.maximum(m[h * seg:(h + 1) * seg], sk) for h, sk in enumerate(sinks)], axis=0)
    m = jnp.broadcast_to(m, (rows, LANES))
    e = jnp.concatenate([jnp.exp(s[:, c * LANES:(c + 1) * LANES] - m) for c in range(tk // LANES)], axis=1)
    v_ones = jnp.concatenate([v2, jnp.ones((tk, LANES), BF16)], axis=1)
    od = jnp.dot(e.astype(BF16), v_ones, preferred_element_type=F32)
    o, den = od[:, :LANES], od[:, LANES:]
    if sinks is not None:
        den = jnp.concatenate([den[h * seg:(h + 1) * seg] + jnp.exp(sk - m[h * seg:(h + 1) * seg])
                               for h, sk in enumerate(sinks)], axis=0)
    return o / den, m, den


def _fill_band_bias(bias_ref, tq, window):
    rows, tk = bias_ref.shape[1:]
    row = lax.broadcasted_iota(I32, (rows, tk), 0) & (tq - 1)
    col = lax.broadcasted_iota(I32, (rows, tk), 1)
    for i in range(bias_ref.shape[0]):
        bias_ref[i] = jnp.where(jnp.abs(col - row - i * window) <= window, 0.0, NEG_INF)


def _dil_kernel(q_ref, k_ref, v_ref, o_ref, l_ref, bias_ref, *, length, tq, tk):
    low = lax.broadcasted_iota(I32, (tq, LANES), 1) < 64
    _fill_band_bias(bias_ref, tq, DIL_HALF_WINDOW)
    nq = length // tq

    def body(j, carry):
        r = j // nq
        qs = pl.multiple_of((j % nq) * tq, tq)
        ks = pl.multiple_of(jnp.clip(qs - DIL_HALF_WINDOW, 0, length - tk), DIL_HALF_WINDOW)
        which = (qs - ks) // DIL_HALF_WINDOW
        for c in range(DIL_GROUP_W // LANES):
            cs = slice(c * LANES, (c + 1) * LANES)
            qst = _split_heads(q_ref[r, pl.ds(qs, tq), cs], low)
            o, m, den = _band_softmax(qst, k_ref[r, pl.ds(ks, tk), cs], v_ref[r, pl.ds(ks, tk), cs],
                                      bias_ref[which], None)
            lse = m + jnp.log(den)
            o_ref[r, pl.ds(qs, tq), cs] = jnp.where(low, o[:tq], o[tq:]).astype(BF16)
            l_ref[r, pl.ds(qs, tq), cs] = jnp.where(low, lse[:tq], lse[tq:])
        return carry

    lax.fori_loop(0, q_ref.shape[0] * nq, body, 0, unroll=16)


def _dil_attention(q, k, v):
    batch, d, length, w = q.shape
    tq, tk = 128, 256
    spec = pl.BlockSpec((None, d, length, w), lambda b: (b, 0, 0, 0))
    return pl.pallas_call(
        functools.partial(_dil_kernel, length=length, tq=tq, tk=tk),
        grid=(batch,),
        in_specs=[spec, spec, spec],
        out_specs=[spec, spec],
        out_shape=[jax.ShapeDtypeStruct(q.shape, BF16), jax.ShapeDtypeStruct(q.shape, F32)],
        scratch_shapes=[pltpu.VMEM((3, 2 * tq, tk), F32)],
        compiler_params=_cparams(("arbitrary",)),
        name=f"dil{d}",
    )(q, k, v)


def _swa_kernel(sink_ref, q_ref, k_ref, v_ref, o_ref, bias_ref, *, length, tq, tk):
    low = lax.broadcasted_iota(I32, (tq, LANES), 1) < 64
    nblk = q_ref.shape[1] // LANES
    _fill_band_bias(bias_ref, tq, SWA_WINDOW)

    def body(j, carry):
        qs = pl.multiple_of(j * tq, tq)
        ks = pl.multiple_of(jnp.clip(qs - SWA_WINDOW, 0, length - tk), SWA_WINDOW)
        bias = bias_ref[(qs - ks) // SWA_WINDOW]
        for b in range(nblk):
            cs = slice((b // 2) * LANES, (b // 2 + 1) * LANES)
            bs = slice(b * LANES, (b + 1) * LANES)
            qst = _split_heads(q_ref[pl.ds(qs, tq), bs], low)
            sinks = (sink_ref[2 * b], sink_ref[2 * b + 1])
            o, _, _ = _band_softmax(qst, k_ref[pl.ds(ks, tk), cs], v_ref[pl.ds(ks, tk), cs], bias, sinks)
            o_ref[pl.ds(qs, tq), bs] = jnp.where(low, o[:tq], o[tq:]).astype(BF16)
        return carry

    lax.fori_loop(0, length // tq, body, 0, unroll=16)


def _swa_attention(sink, q, k, v):
    batch, length, qw = q.shape
    tq, tk = 128, 384
    return pl.pallas_call(
        functools.partial(_swa_kernel, length=length, tq=tq, tk=tk),
        grid=(batch,),
        in_specs=[pl.BlockSpec(memory_space=pltpu.SMEM),
                  pl.BlockSpec((None, length, qw), lambda b: (b, 0, 0)),
                  pl.BlockSpec((None, length, k.shape[2]), lambda b: (b, 0, 0)),
                  pl.BlockSpec((None, length, v.shape[2]), lambda b: (b, 0, 0))],
        out_specs=pl.BlockSpec((None, length, qw), lambda b: (b, 0, 0)),
        out_shape=jax.ShapeDtypeStruct(q.shape, BF16),
        scratch_shapes=[pltpu.VMEM((3, 2 * tq, tk), F32)],
        compiler_params=_cparams(("arbitrary",)),
        name="swa",
    )(sink, q, k, v)


def _route_rows(logits):
    lane = lax.broadcasted_iota(I32, logits.shape, 1).astype(F32)
    big = 1e9
    is_g = lane < N_GROUPS
    gl = jnp.where(is_g, logits, NEG_INF)
    gmax = jnp.max(gl, axis=-1, keepdims=True)
    gsel = jnp.min(jnp.where(is_g & (gl == gmax), lane, big), axis=-1, keepdims=True)
    gw = 1.0 / jnp.sum(jnp.where(is_g, jnp.exp(gl - gmax), 0.0), axis=-1, keepdims=True)
    e_lo = N_GROUPS + gsel * EXPERTS_PER_GROUP
    in_grp = (lane >= e_lo) & (lane < e_lo + EXPERTS_PER_GROUP)
    el = jnp.where(in_grp, logits, NEG_INF)
    m1 = jnp.max(el, axis=-1, keepdims=True)
    i1 = jnp.min(jnp.where(in_grp & (el == m1), lane, big), axis=-1, keepdims=True)
    el2 = jnp.where(lane == i1, NEG_INF, el)
    m2 = jnp.max(el2, axis=-1, keepdims=True)
    i2 = jnp.min(jnp.where(in_grp & (lane != i1) & (el2 == m2), lane, big), axis=-1, keepdims=True)
    t = jnp.exp(m2 - m1)
    tw1 = gw / (1.0 + t)
    tw2 = gw * t / (1.0 + t)
    out = jnp.where(lane == 0, tw1, 0.0)
    out = jnp.where(lane == 1, tw2, out)
    out = jnp.where(lane == 2, i1 - N_GROUPS, out)
    return jnp.where(lane == 3, i2 - N_GROUPS, out)


def _outproj_kernel(o0_ref, l0_ref, o1_ref, l1_ref, o2_ref, l2_ref, ob_ref, sga_ref, sgb_ref, x_ref,
                    mod_ref, g_ref, wa_ref, wb_ref, wo_ref, wr_ref, br_ref,
                    x1_ref, xs_ref, route_ref, len_ref,
                    so1_ref, sl1_ref, so2_ref, sl2_ref, h2_ref, *, tm, sub, group):
    ntiles = pl.num_programs(0) - 1

    @pl.when(pl.program_id(0) == ntiles)
    def _():
        xs_ref[...] = jnp.zeros_like(xs_ref)

    pl.when(pl.program_id(0) < ntiles)(functools.partial(
        _outproj_tile, o0_ref, l0_ref, o1_ref, l1_ref, o2_ref, l2_ref, ob_ref, sga_ref, sgb_ref, x_ref,
        mod_ref, g_ref, wa_ref, wb_ref, wo_ref, wr_ref, br_ref, x1_ref, xs_ref, route_ref, len_ref,
        so1_ref, sl1_ref, so2_ref, sl2_ref, h2_ref, tm=tm, sub=sub, group=group))


def _outproj_tile(o0_ref, l0_ref, o1_ref, l1_ref, o2_ref, l2_ref, ob_ref, sga_ref, sgb_ref, x_ref,
                  mod_ref, g_ref, wa_ref, wb_ref, wo_ref, wr_ref, br_ref,
                  x1_ref, xs_ref, route_ref, len_ref,
                  so1_ref, sl1_ref, so2_ref, sl2_ref, h2_ref, *, tm, sub, group):
    dm = x_ref.shape[1]
    for (o_ref, l_ref, so_ref, sl_ref, d) in ((o1_ref, l1_ref, so1_ref, sl1_ref, DILATIONS[1]),
                                              (o2_ref, l2_ref, so2_ref, sl2_ref, DILATIONS[2])):
        for r in range(d):
            for c in range(2):
                cs = slice(c * LANES, (c + 1) * LANES)
                so_ref[c, pl.ds(r, tm // d, stride=d), :] = o_ref[0, r, :, cs].astype(F32)
                sl_ref[c, pl.ds(r, tm // d, stride=d), :] = l_ref[0, r, :, cs]
    mr = lax.broadcasted_iota(I32, (LANES, LANES), 0)
    mc = lax.broadcasted_iota(I32, (LANES, LANES), 1)
    move_hi = jnp.where(((mr < 2) & (mc == 2 * mr)) | ((mr >= 2) & (mr < 4) & (mc == mr + 2)), 1.0, 0.0).astype(BF16)
    move_lo = jnp.where((mr < 2) & (mc == 2 * mr + 1), 1.0, 0.0).astype(BF16)
    for t in range(tm // sub):
        rs = slice(t * sub, (t + 1) * sub)
        both = lambda ref: jnp.concatenate([ref[0, rs, :], ref[1, rs, :]], axis=1)
        o0, l0 = o0_ref[0, 0, rs, :].astype(F32), l0_ref[0, 0, rs, :]
        o1, l1, o2, l2 = both(so1_ref), both(sl1_ref), both(so2_ref), both(sl2_ref)
        mx = jnp.maximum(jnp.maximum(l0, l1), l2)
        w0, w1, w2 = jnp.exp(l0 - mx), jnp.exp(l1 - mx), jnp.exp(l2 - mx)
        o_a = (w0 * o0 + w1 * o1 + w2 * o2) / (w0 + w1 + w2)
        y_a = jnp.dot(o_a.astype(BF16), wa_ref[...], preferred_element_type=F32)
        y_b = jnp.dot(ob_ref[rs, :], wb_ref[...], preferred_element_type=F32)
        merged = sga_ref[rs, :].astype(F32) * y_a + sgb_ref[rs, :].astype(F32) * y_b
        mix = jnp.dot(merged.astype(BF16), wo_ref[...], preferred_element_type=F32)
        x1 = x_ref[rs, :] + mod_ref[0, 2:3, :] * mix
        x1_ref[rs, :] = x1
        h2 = _rms_mod(x1, g_ref[...], mod_ref[0, 3:4, :], mod_ref[0, 4:5, :]).astype(BF16)
        h2_ref[rs, :dm] = h2
        logits = jnp.dot(h2, wr_ref[...], preferred_element_type=F32) + br_ref[...]
        rt = _route_rows(logits)
        route_ref[rs, :] = rt
        hi = rt.astype(BF16)
        lo = (rt - hi.astype(F32)).astype(BF16)
        aux = (jnp.dot(hi, move_hi, preferred_element_type=F32) + jnp.dot(lo, move_lo, preferred_element_type=F32))
        h2_ref[rs, dm:] = aux.astype(BF16)

    part = route_ref[...]
    e1, e2 = part[:, 2:3], part[:, 3:4]
    lane = lax.broadcasted_iota(I32, (tm, LANES), 1).astype(F32)
    onehot = jnp.where((lane == e1) | (lane == e2), 1.0, 0.0)
    rr = lax.broadcasted_iota(I32, (tm, tm), 0)
    cc = lax.broadcasted_iota(I32, (tm, tm), 1)
    tri = jnp.where(rr > cc, 1.0, 0.0).astype(BF16)
    prefix = jnp.dot(tri, onehot.astype(BF16), preferred_element_type=F32)
    cnt = jnp.sum(onehot, axis=0, keepdims=True)
    seg_len = jnp.ceil(cnt * (1.0 / SEG_ROWS)) * SEG_ROWS
    ur = lax.broadcasted_iota(I32, (LANES, LANES), 0)
    uc = lax.broadcasted_iota(I32, (LANES, LANES), 1)
    upper = jnp.where(ur < uc, 1.0, 0.0).astype(BF16)
    seg_off = jnp.dot(jnp.broadcast_to(seg_len, (8, LANES)).astype(BF16), upper, preferred_element_type=F32)[0:1, :]
    slot_map = seg_off + prefix
    s1 = jnp.sum(jnp.where(lane == e1, slot_map, 0.0), axis=-1, keepdims=True)
    s2 = jnp.sum(jnp.where(lane == e2, slot_map, 0.0), axis=-1, keepdims=True)
    len_ref[0] = jnp.broadcast_to(seg_len, (8, LANES))
    route = jnp.where(lane == 2, s1, jnp.where(lane == 3, s2, part))
    route_ref[...] = route

    pr = lax.broadcasted_iota(I32, (8, LANES), 0)
    pc = lax.broadcasted_iota(I32, (8, LANES), 1)
    lane_pick = jnp.where(pc == pr + 2, 1.0, 0.0)
    slots_t = lax.dot_general(lane_pick, route, (((1,), (1,)), ((), ())), preferred_element_type=F32,
                              precision=lax.Precision.HIGHEST)
    live = jnp.sum(seg_len).astype(I32)

    def sort_rows(lo, rows):
        srow = (lax.broadcasted_iota(I32, (rows, tm), 0) + lo).astype(F32)
        pick = jnp.where((srow == slots_t[0:1, :]) | (srow == slots_t[1:2, :]), 1.0, 0.0).astype(BF16)
        xs_ref[lo:lo + rows, :] = jnp.dot(pick, h2_ref[...], preferred_element_type=F32).astype(BF16)

    def clear_rows(lo, rows):
        xs_ref[lo:lo + rows, :] = jnp.zeros((rows, xs_ref.shape[1]), BF16)

    head = xs_ref.shape[0] - group
    sort_rows(0, head)
    pl.when(live > head)(functools.partial(sort_rows, head, group))
    pl.when(live <= head)(functools.partial(clear_rows, head, group))


def _outproj(dil_outs, ob, sga, sgb, x2, mod3, g_ffn, wa, wb, wo, wr, br, batch, seq, nslots):
    n, dm = x2.shape
    tm = MOE_TILE
    tpb = seq // tm
    ntiles = n // tm
    tile = lambda i: jnp.minimum(i, ntiles - 1)
    row = lambda i: (tile(i), 0)
    const = lambda i: (0, 0)
    in_specs = []
    args = []
    for (o, l), d in zip(dil_outs, DILATIONS):
        spec = pl.BlockSpec((1, d, tm // d, DIL_GROUP_W), lambda i: (tile(i) // tpb, 0, tile(i) % tpb, 0))
        in_specs += [spec, spec]
        args += [o, l]
    in_specs += [pl.BlockSpec((tm, ob.shape[1]), row), pl.BlockSpec((tm, dm), row), pl.BlockSpec((tm, dm), row),
                 pl.BlockSpec((tm, dm), row),
                 pl.BlockSpec((1, 6, dm), lambda i: (tile(i) // tpb, 0, 0)),
                 pl.BlockSpec((1, dm), const),
                 pl.BlockSpec(wa.shape, const), pl.BlockSpec(wb.shape, const), pl.BlockSpec(wo.shape, const),
                 pl.BlockSpec(wr.shape, const), pl.BlockSpec(br.shape, const)]
    args += [ob, sga, sgb, x2, mod3, g_ffn, wa, wb, wo, wr, br]
    width = dm + LANES
    return pl.pallas_call(
        functools.partial(_outproj_kernel, tm=tm, sub=512, group=SLOT_GROUP),
        grid=(ntiles + 1,),
        in_specs=in_specs,
        out_specs=[pl.BlockSpec((tm, dm), row), pl.BlockSpec((nslots, width), lambda i: (i, 0)),
                   pl.BlockSpec((tm, LANES), row), pl.BlockSpec((1, 8, LANES), lambda i: (tile(i), 0, 0))],
        out_shape=[jax.ShapeDtypeStruct((n, dm), F32), jax.ShapeDtypeStruct(((ntiles + 1) * nslots, width), BF16),
                   jax.ShapeDtypeStruct((n, LANES), F32), jax.ShapeDtypeStruct((ntiles, 8, LANES), F32)],
        scratch_shapes=[pltpu.VMEM((2, tm, LANES), F32)] * 4 + [pltpu.VMEM((tm, width), BF16)],
        compiler_params=_cparams(("arbitrary",)),
        name="outproj",
    )(*args)


def _wait_copies(count, copy):
    def wait_one(c, carry):
        copy.wait()
        return carry

    lax.fori_loop(0, count, wait_one, 0)


def _experts_kernel(first_ref, count_ref, chunk_ref, wg_ref, wu_ref, wd_ref, xs_hbm, ys_hbm,
                    xbuf, ybuf, zbuf, xsem, ysem, zsem, *, nblk):
    e = pl.program_id(0)
    last = pl.num_programs(0) - 1
    first = first_ref[e]
    count = count_ref[e]
    used = first_ref[last] + count_ref[last]
    dm = ybuf.shape[2]
    cpb = ROW_BLOCK // SEG_ROWS

    def chunk_rows(blk, j):
        return pl.ds(pl.multiple_of(chunk_ref[blk * cpb + j], SEG_ROWS), SEG_ROWS)

    def gather(blk):
        for j in range(cpb):
            pltpu.make_async_copy(xs_hbm.at[chunk_rows(blk, j), :], xbuf.at[blk & 1, pl.ds(j * SEG_ROWS, SEG_ROWS), :],
                                  xsem.at[blk & 1]).start(priority=1)

    def gather_done(blk):
        return pltpu.make_async_copy(xs_hbm.at[pl.ds(0, ROW_BLOCK), :], xbuf.at[blk & 1], xsem.at[blk & 1])

    def put(blk):
        rows = pl.ds(pl.multiple_of(blk * ROW_BLOCK, ROW_BLOCK), ROW_BLOCK)
        return pltpu.make_async_copy(ybuf.at[blk & 1], ys_hbm.at[rows, :], ysem.at[blk & 1])

    def zero_blk(b):
        rows = pl.ds(pl.multiple_of(b * ROW_BLOCK, ROW_BLOCK), ROW_BLOCK)
        return pltpu.make_async_copy(zbuf, ys_hbm.at[rows, :], zsem)

    @pl.when(e == 0)
    def _():
        zbuf[...] = jnp.zeros_like(zbuf)

        def start(b, carry):
            zero_blk(b).start()
            return carry

        lax.fori_loop(used, nblk, start, 0)

        @pl.when(used > 0)
        def _():
            gather(0)

    def body(b, carry):
        blk = first + b
        slot = blk & 1
        gather_done(blk).wait()

        @pl.when(blk + 1 < used)
        def _():
            gather(blk + 1)

        @pl.when(blk >= 2)
        def _():
            put(blk - 2).wait()

        xb = xbuf[slot, :, :dm]
        aux = xbuf[slot, :, dm:].astype(F32)
        w = jnp.where(aux[:, 4:5] == e.astype(F32), aux[:, 0:1] + aux[:, 1:2], aux[:, 2:3] + aux[:, 3:4])
        g = jnp.dot(xb, wg_ref[0], preferred_element_type=F32)
        u = jnp.dot(xb, wu_ref[0], preferred_element_type=F32)
        a = (g * jax.nn.sigmoid(g)) * u
        y = jnp.dot(a.astype(BF16), wd_ref[0], preferred_element_type=F32)
        ybuf[slot] = (y * w).astype(BF16)
        put(blk).start()
        return carry

    lax.fori_loop(0, count, body, 0)

    @pl.when(e == last)
    def _():
        @pl.when(used >= 2)
        def _():
            put(used - 2).wait()

        @pl.when(used >= 1)
        def _():
            put(used - 1).wait()

        _wait_copies(nblk - used, zero_blk(0))


def _experts(first_blk, count_blk, chunk_tbl, x_sorted, w_gate, w_up, w_down, nblk):
    rows, width = x_sorted.shape
    n_exp, dm, de = w_gate.shape
    wmap = lambda e, *_: (e, 0, 0)
    grid_spec = pltpu.PrefetchScalarGridSpec(
        num_scalar_prefetch=3,
        grid=(n_exp,),
        in_specs=[pl.BlockSpec((1, dm, de), wmap),
                  pl.BlockSpec((1, dm, de), wmap),
                  pl.BlockSpec((1, de, dm), wmap),
                  pl.BlockSpec(memory_space=pl.ANY)],
        out_specs=pl.BlockSpec(memory_space=pl.ANY),
        scratch_shapes=[pltpu.VMEM((2, ROW_BLOCK, width), BF16), pltpu.VMEM((2, ROW_BLOCK, dm), BF16),
                        pltpu.VMEM((ROW_BLOCK, dm), BF16),
                        pltpu.SemaphoreType.DMA((2,)), pltpu.SemaphoreType.DMA((2,)), pltpu.SemaphoreType.DMA(())],
    )
    return pl.pallas_call(
        functools.partial(_experts_kernel, nblk=nblk),
        grid_spec=grid_spec,
        out_shape=jax.ShapeDtypeStruct((nblk * ROW_BLOCK, dm), BF16),
        compiler_params=_cparams(("arbitrary",)),
        name="experts",
    )(first_blk, count_blk, chunk_tbl, w_gate, w_up, w_down, x_sorted)


def _slot_parts(tm, nslots, group):
    return [(0, 2 * tm)] + [(lo, group) for lo in range(2 * tm, nslots, group)]


def _combine_kernel(src_ref, live_ref, route_ref, x1_ref, mod_ref, g_ref, y_ref, o_ref, ys_ref, sem,
                    *, tm, nslots, group):
    i = pl.program_id(0)
    last = pl.num_programs(0) - 1
    slot = i & 1
    nchk = nslots // SEG_ROWS
    parts = _slot_parts(tm, nslots, group)

    def for_live_parts(tile, fn):
        for p, (lo, rows) in enumerate(parts):
            if p == 0:
                fn(lo, rows)
            else:
                pl.when(live_ref[tile] > lo)(functools.partial(fn, lo, rows))

    def gather(tile, s):
        def start(lo, rows):
            for c in range(lo // SEG_ROWS, (lo + rows) // SEG_ROWS):
                src = pl.multiple_of(src_ref[tile * nchk + c], SEG_ROWS)
                pltpu.make_async_copy(y_ref.at[pl.ds(src, SEG_ROWS), :],
                                      ys_ref.at[s, pl.ds(c * SEG_ROWS, SEG_ROWS), :], sem.at[s]).start()

        for_live_parts(tile, start)

    def wait(tile, s):
        for_live_parts(tile, lambda lo, rows: pltpu.make_async_copy(
            y_ref.at[pl.ds(0, rows), :], ys_ref.at[s, pl.ds(lo, rows), :], sem.at[s]).wait())

    @pl.when(i == 0)
    def _():
        gather(0, 0)

    nxt = jnp.minimum(i + 1, last)
    gather(nxt, 1 - slot)
    wait(i, slot)
    route = route_ref[...]

    def finish(rows):
        scol = lax.broadcasted_iota(I32, (tm, rows), 1).astype(F32)
        pick = jnp.where((scol == route[:, 2:3]) | (scol == route[:, 3:4]), 1.0, 0.0).astype(BF16)
        moe = jnp.dot(pick, ys_ref[slot, :rows, :], preferred_element_type=F32)
        x = x1_ref[...] + mod_ref[0, 5:6, :] * moe
        ms = jnp.mean(x * x, axis=-1, keepdims=True)
        o_ref[...] = (x * lax.rsqrt(ms + RMS_EPS)) * g_ref[...]

    ends = [lo + rows for lo, rows in parts]
    for p, end in enumerate(ends):
        above = live_ref[i] > (ends[p - 1] if p else -1)
        cond = above if p == len(ends) - 1 else above & (live_ref[i] <= end)
        pl.when(cond)(functools.partial(finish, end))

    @pl.when(i == last)
    def _():
        wait(nxt, 1 - slot)


def _combine(src_tbl, live_tbl, route, x1, mod3, g_final, y_rows, seq, tm, nslots):
    n, dm = x1.shape
    tpb = seq // tm
    grid_spec = pltpu.PrefetchScalarGridSpec(
        num_scalar_prefetch=2,
        grid=(n // tm,),
        in_specs=[pl.BlockSpec((tm, LANES), lambda i, *_: (i, 0)),
                  pl.BlockSpec((tm, dm), lambda i, *_: (i, 0)),
                  pl.BlockSpec((1, 6, dm), lambda i, *_: (i // tpb, 0, 0)),
                  pl.BlockSpec((1, dm), lambda i, *_: (0, 0)),
                  pl.BlockSpec(memory_space=pl.ANY)],
        out_specs=pl.BlockSpec((tm, dm), lambda i, *_: (i, 0)),
        scratch_shapes=[pltpu.VMEM((2, nslots, dm), BF16), pltpu.SemaphoreType.DMA((2,))],
    )
    return pl.pallas_call(
        functools.partial(_combine_kernel, tm=tm, nslots=nslots, group=SLOT_GROUP),
        grid_spec=grid_spec,
        out_shape=jax.ShapeDtypeStruct((n, dm), F32),
        compiler_params=_cparams(("arbitrary",)),
        name="combine",
    )(src_tbl, live_tbl, route, x1, mod3, g_final, y_rows)


def _rope_angles(positions):
    half = HEAD_DIM // 2
    inv_freq = ROPE_THETA ** (-jnp.arange(half, dtype=F32) * (2.0 / HEAD_DIM))
    freq = jnp.tile(inv_freq, LANES // half)
    sign = jnp.tile(jnp.concatenate([-jnp.ones((half,), F32), jnp.ones((half,), F32)]), LANES // HEAD_DIM)
    ang = positions.astype(F32).reshape(-1, 1) * freq
    return ang, sign.reshape(1, LANES)


def _expert_chunk_table(lens, pstart, tot, pend, nblk, nslots):
    ntiles = lens.shape[0]
    cpb = ROW_BLOCK // SEG_ROWS
    runs = lens // SEG_ROWS
    q = jnp.arange(nblk * cpb, dtype=I32)
    owner = jnp.minimum(jnp.sum((pend[None, :] // SEG_ROWS <= q[:, None]).astype(I32), axis=1), N_EXPERTS - 1)
    is_e = owner[:, None] == jnp.arange(N_EXPERTS, dtype=I32)[None, :]
    pick_e = lambda v: jnp.sum(jnp.where(is_e, v[None, :], 0), axis=1)
    off = q - pick_e(pstart // SEG_ROWS)
    in_run = (off < pick_e(tot // SEG_ROWS)) & (q < pend[-1] // SEG_ROWS)
    upto = jnp.cumsum(runs, axis=0)
    upto_e = jnp.sum(jnp.where(is_e[:, None, :], upto[None, :, :], 0), axis=2)
    tile = jnp.minimum(jnp.sum((upto_e <= off[:, None]).astype(I32), axis=1), ntiles - 1)
    is_t = tile[:, None] == jnp.arange(ntiles, dtype=I32)[None, :]
    run_slot0 = jnp.cumsum(runs, axis=1) - runs - (upto - runs)
    slot0 = jnp.sum(jnp.where(is_t[:, :, None] & is_e[:, None, :], run_slot0[None, :, :], 0), axis=(1, 2))
    live_row = tile * nslots + (slot0 + off) * SEG_ROWS
    spare_row = ntiles * nslots + (((q // cpb) % 2) * cpb + q % cpb) * SEG_ROWS
    return jnp.where(in_run, live_row, spare_row)


def kernel(x, c, positions, w_ada, b_ada, g_mix, w_in, sink_logits, w_branch_a, w_branch_b, w_out, g_ffn,
           w_group, b_group, w_route, b_route, w_expert_gate, w_expert_up, w_expert_down, g_final):
    batch, seq, dm = x.shape
    n = batch * seq
    assert w_ada.shape[0] == 1, "one layer"
    x2 = x.reshape(n, dm)

    c8 = jnp.pad(c, ((0, 8 - batch), (0, 0)))
    mod = _ada(c8, w_ada[0], b_ada[0].reshape(1, -1))
    mod3 = mod[:batch].reshape(batch, 6, dm)

    angles, sign = _rope_angles(positions)
    outs = _inproj(x2, mod3, g_mix[0].reshape(1, dm), angles, sign, w_in[0].astype(BF16),
                   (w_expert_gate[0], w_expert_up[0], w_expert_down[0]), batch, seq)
    qkv = outs[:9]
    qb, kb, vb, sga, sgb = outs[9:14]
    expert_w_bf = outs[14:]

    dil_outs = [_dil_attention(qkv[3 * g], qkv[3 * g + 1], qkv[3 * g + 2]) for g in range(len(DILATIONS))]
    ob = _swa_attention(sink_logits[0], qb.reshape(batch, seq, -1), kb.reshape(batch, seq, -1),
                        vb.reshape(batch, seq, -1)).reshape(n, -1)

    pad = LANES - N_GROUPS - N_EXPERTS
    wr = jnp.concatenate([w_group[0], w_route[0], jnp.zeros((dm, pad), F32)], axis=1).astype(BF16)
    br = jnp.concatenate([b_group[0], b_route[0], jnp.zeros((pad,), F32)]).reshape(1, LANES)
    ntiles = n // MOE_TILE
    nslots = 2 * MOE_TILE + N_EXPERTS * SEG_ROWS
    x1, x_sorted, route, seg_lens = _outproj(dil_outs, ob, sga, sgb, x2, mod3, g_ffn[0].reshape(1, dm),
                                             w_branch_a[0].astype(BF16), w_branch_b[0].astype(BF16),
                                             w_out[0].astype(BF16), wr, br, batch, seq, nslots)

    nblk = -(-(2 * n + ntiles * N_EXPERTS * (SEG_ROWS - 1)) // ROW_BLOCK) + N_EXPERTS
    lens = seg_lens[:, 0, :N_EXPERTS].astype(I32)
    tot = jnp.sum(lens, axis=0)
    padded = (tot + ROW_BLOCK - 1) // ROW_BLOCK * ROW_BLOCK
    pend = jnp.cumsum(padded)
    pstart = pend - padded
    chunk_tbl = _expert_chunk_table(lens, pstart, tot, pend, nblk, nslots)
    live_tbl = jnp.sum(lens, axis=1)

    nchk = nslots // SEG_ROWS
    base = pstart[None, :] + jnp.cumsum(lens, axis=0) - lens
    run_end = jnp.cumsum(lens // SEG_ROWS, axis=1)
    chunk = jnp.arange(nchk, dtype=I32)
    owner = jnp.sum((run_end[:, None, :] <= chunk[None, :, None]).astype(I32), axis=2)
    is_owner = owner[:, :, None] == jnp.arange(N_EXPERTS, dtype=I32)[None, None, :]
    run_row0 = base - (run_end - lens // SEG_ROWS) * SEG_ROWS
    row = jnp.sum(jnp.where(is_owner, run_row0[:, None, :], 0), axis=2) + chunk[None, :] * SEG_ROWS
    src_tbl = jnp.where(owner < N_EXPERTS, row, chunk[None, :] * SEG_ROWS).reshape(-1)

    y_rows = _experts(pstart // ROW_BLOCK, padded // ROW_BLOCK, chunk_tbl, x_sorted, *expert_w_bf, nblk)
    out = _combine(src_tbl, live_tbl, route, x1, mod3, g_final.reshape(1, dm), y_rows, seq, MOE_TILE, nslots)
    return out.reshape(batch, seq, dm)
```

```python
import functools

import jax
import jax.numpy as jnp
from jax import lax
from jax.experimental import pallas as pl
from jax.experimental.pallas import tpu as pltpu

F32 = jnp.float32
BF16 = jnp.bfloat16
I32 = jnp.int32

HEAD_DIM = 64
ROPE_THETA = 10000.0
RMS_EPS = 1e-6
NEG_INF = -1e30
Q_SCALE = HEAD_DIM ** -0.5
DILATIONS = (1, 4, 16)
DIL_HALF_WINDOW = 64
DIL_GROUP_W = 256
SWA_WINDOW = 128
N_GROUPS = 4
EXPERTS_PER_GROUP = 8
N_EXPERTS = 32
LANES = 128
ROW_BLOCK = 512
SEG_ROWS = 16
MOE_TILE = 512
SLOT_GROUP = 256
VMEM_LIMIT = 56 * 1024 * 1024


def _cparams(sem):
    return pltpu.CompilerParams(dimension_semantics=sem, vmem_limit_bytes=VMEM_LIMIT)


def _ada_kernel(c_ref, w_ref, b_ref, o_ref):
    c = c_ref[...]
    cs = c * jax.nn.sigmoid(c)
    o_ref[...] = jnp.dot(cs.astype(BF16), w_ref[...].astype(BF16), preferred_element_type=F32) + b_ref[...]


def _ada(c8, w_ada, b_ada):
    d, n = w_ada.shape
    tn = 1536
    return pl.pallas_call(
        _ada_kernel,
        grid=(n // tn,),
        in_specs=[pl.BlockSpec((8, d), lambda j: (0, 0)),
                  pl.BlockSpec((d, tn), lambda j: (0, j)),
                  pl.BlockSpec((1, tn), lambda j: (0, j))],
        out_specs=pl.BlockSpec((8, tn), lambda j: (0, j)),
        out_shape=jax.ShapeDtypeStruct((8, n), F32),
        compiler_params=_cparams(("arbitrary",)),
        name="ada",
    )(c8, w_ada, b_ada)


def _rms_mod(x, g, shift, scale):
    ms = jnp.mean(x * x, axis=-1, keepdims=True)
    return (x * lax.rsqrt(ms + RMS_EPS)) * (g * (1.0 + scale)) + shift


def _inproj_kernel(x_ref, mod_ref, g_ref, ang_ref, sign_ref, w_ref, eg_ref, eu_ref, ed_ref,
                   q0_ref, k0_ref, v0_ref, q1_ref, k1_ref, v1_ref, q2_ref, k2_ref, v2_ref,
                   qb_ref, kb_ref, vb_ref, sga_ref, sgb_ref, egb_ref, eub_ref, edb_ref, stg_ref, *, tm, n_exp):
    def cast_expert():
        egb_ref[...] = eg_ref[...].astype(BF16)
        eub_ref[...] = eu_ref[...].astype(BF16)
        edb_ref[...] = ed_ref[...].astype(BF16)

    if n_exp is None:
        cast_expert()
    else:
        pl.when(pl.program_id(0) < n_exp)(cast_expert)

    h = _rms_mod(x_ref[...], g_ref[...], mod_ref[0, 0:1, :], mod_ref[0, 1:2, :])
    hb = h.astype(BF16)
    ang = ang_ref[...]
    cos = jnp.cos(ang)
    sin = jnp.sin(ang) * sign_ref[...]
    lane = lax.broadcasted_iota(I32, (tm, LANES), 1)
    first_half = (lane & 32) == 0
    low = lane < 64

    def proj(c0, width):
        return jnp.dot(hb, w_ref[:, c0:c0 + width], preferred_element_type=F32)

    def rope(t):
        rot = jnp.where(first_half, pltpu.roll(t, 96, 1), pltpu.roll(t, 32, 1))
        return t * cos + rot * sin

    def rope256(p):
        return jnp.concatenate([rope(p[:, :LANES]), rope(p[:, LANES:])], axis=1)

    def store_group(ref, val, d):
        if d == 1:
            ref[0, 0] = val.astype(BF16)
        else:
            for c in range(2):
                stg_ref[c] = val[:, c * LANES:(c + 1) * LANES]
            for r in range(d):
                for c in range(2):
                    ref[0, r, :, c * LANES:(c + 1) * LANES] = (
                        stg_ref[c, pl.ds(r, tm // d, stride=d), :].astype(BF16))

    q_refs = (q0_ref, q1_ref, q2_ref)
    k_refs = (k0_ref, k1_ref, k2_ref)
    v_refs = (v0_ref, v1_ref, v2_ref)
    for g, d in enumerate(DILATIONS):
        store_group(q_refs[g], rope256(proj(g * 256, 256)) * Q_SCALE, d)
        store_group(k_refs[g], rope256(proj(768 + g * 256, 256)), d)
        store_group(v_refs[g], proj(1536 + g * 256, 256), d)
    for j in range(2):
        qb_ref[:, j * 256:(j + 1) * 256] = (rope256(proj(2304 + j * 256, 256)) * Q_SCALE).astype(BF16)
    kv = proj(2816, 256)
    kb = rope(kv[:, :LANES])
    vb = kv[:, LANES:]
    kb_sw = pltpu.roll(kb, 64, 1)
    vb_sw = pltpu.roll(vb, 64, 1)
    kb_ref[:, :LANES] = jnp.where(low, kb, kb_sw).astype(BF16)
    kb_ref[:, LANES:] = jnp.where(low, kb_sw, kb).astype(BF16)
    vb_ref[:, :LANES] = jnp.where(low, vb, vb_sw).astype(BF16)
    vb_ref[:, LANES:] = jnp.where(low, vb_sw, vb).astype(BF16)
    for j in range(4):
        sga_ref[:, j * 256:(j + 1) * 256] = jax.nn.sigmoid(proj(3072 + j * 256, 256)).astype(BF16)
        sgb_ref[:, j * 256:(j + 1) * 256] = jax.nn.sigmoid(proj(4096 + j * 256, 256)).astype(BF16)


def _inproj(x2, mod3, g_mix, angles, sign, w_in_bf, expert_w, batch, seq):
    n, dm = x2.shape
    tm = 512
    tpb = seq // tm
    grid = (n // tm,)
    row = lambda i: (i, 0)
    strided_specs, strided_shapes = [], []
    for d in DILATIONS:
        for _ in range(3):
            strided_specs.append(pl.BlockSpec((1, d, tm // d, DIL_GROUP_W), lambda i: (i // tpb, 0, i % tpb, 0)))
            strided_shapes.append(jax.ShapeDtypeStruct((batch, d, seq // d, DIL_GROUP_W), BF16))
    out_specs = strided_specs + [
        pl.BlockSpec((tm, 512), row), pl.BlockSpec((tm, 256), row), pl.BlockSpec((tm, 256), row),
        pl.BlockSpec((tm, dm), row), pl.BlockSpec((tm, dm), row)]
    out_shapes = strided_shapes + [
        jax.ShapeDtypeStruct((n, 512), BF16), jax.ShapeDtypeStruct((n, 256), BF16),
        jax.ShapeDtypeStruct((n, 256), BF16), jax.ShapeDtypeStruct((n, dm), BF16),
        jax.ShapeDtypeStruct((n, dm), BF16)]
    n_exp = expert_w[0].shape[0]
    assert n // tm >= n_exp, "one expert's weights are cast per grid step"
    emap = lambda i: (jnp.minimum(i, n_exp - 1), 0, 0)
    expert_specs = [pl.BlockSpec((1,) + w.shape[1:], emap) for w in expert_w]
    return pl.pallas_call(
        functools.partial(_inproj_kernel, tm=tm, n_exp=None if n // tm == n_exp else n_exp),
        grid=grid,
        in_specs=[pl.BlockSpec((tm, dm), row),
                  pl.BlockSpec((1, 6, dm), lambda i: (i // tpb, 0, 0)),
                  pl.BlockSpec((1, dm), lambda i: (0, 0)),
                  pl.BlockSpec((tm, LANES), row),
                  pl.BlockSpec((1, LANES), lambda i: (0, 0)),
                  pl.BlockSpec(w_in_bf.shape, lambda i: (0, 0), pipeline_mode=pl.Buffered(1))] + expert_specs,
        out_specs=out_specs + expert_specs,
        out_shape=out_shapes + [jax.ShapeDtypeStruct(w.shape, BF16) for w in expert_w],
        scratch_shapes=[pltpu.VMEM((2, tm, LANES), F32)],
        compiler_params=_cparams(("arbitrary",)),
        name="inproj",
    )(x2, mod3, g_mix, angles, sign, w_in_bf, *expert_w)


def _split_heads(q2, low):
    zero = jnp.zeros_like(q2)
    return jnp.concatenate([jnp.where(low, q2, zero), jnp.where(low, zero, q2)], axis=0)


def _band_softmax(qst, k2, v2, bias, sinks):
    s = lax.dot_general(qst, k2, (((1,), (1,)), ((), ())), preferred_element_type=F32)
    s = s + bias
    rows, tk = s.shape
    m = jnp.max(s, axis=-1, keepdims=True)
    if sinks is not None:
        seg = rows // len(sinks)
        m = jnp.concatenate([jnp.maximum(m[h * seg:(h + 1) * seg], sk) for h, sk in enumerate(sinks)], axis=0)
    m = jnp.broadcast_to(m, (rows, LANES))
    e = jnp.concatenate([jnp.exp(s[:, c * LANES:(c + 1) * LANES] - m) for c in range(tk // LANES)], axis=1)
    v_ones = jnp.concatenate([v2, jnp.ones((tk, LANES), BF16)], axis=1)
    od = jnp.dot(e.astype(BF16), v_ones, preferred_element_type=F32)
    o, den = od[:, :LANES], od[:, LANES:]
    if sinks is not None:
        den = jnp.concatenate([den[h * seg:(h + 1) * seg] + jnp.exp(sk - m[h * seg:(h + 1) * seg])
                               for h, sk in enumerate(sinks)], axis=0)
    return o / den, m, den


def _fill_band_bias(bias_ref, tq, window):
    rows, tk = bias_ref.shape[1:]
    row = lax.broadcasted_iota(I32, (rows, tk), 0) & (tq - 1)
    col = lax.broadcasted_iota(I32, (rows, tk), 1)
    for i in range(bias_ref.shape[0]):
        bias_ref[i] = jnp.where(jnp.abs(col - row - i * window) <= window, 0.0, NEG_INF)


def _dil_kernel(q_ref, k_ref, v_ref, o_ref, l_ref, bias_ref, *, length, tq, tk):
    low = lax.broadcasted_iota(I32, (tq, LANES), 1) < 64
    _fill_band_bias(bias_ref, tq, DIL_HALF_WINDOW)
    nq = length // tq

    def body(j, carry):
        r = j // nq
        qs = pl.multiple_of((j % nq) * tq, tq)
        ks = pl.multiple_of(jnp.clip(qs - DIL_HALF_WINDOW, 0, length - tk), DIL_HALF_WINDOW)
        which = (qs - ks) // DIL_HALF_WINDOW
        for c in range(DIL_GROUP_W // LANES):
            cs = slice(c * LANES, (c + 1) * LANES)
            qst = _split_heads(q_ref[r, pl.ds(qs, tq), cs], low)
            o, m, den = _band_softmax(qst, k_ref[r, pl.ds(ks, tk), cs], v_ref[r, pl.ds(ks, tk), cs],
                                      bias_ref[which], None)
            lse = m + jnp.log(den)
            o_ref[r, pl.ds(qs, tq), cs] = jnp.where(low, o[:tq], o[tq:]).astype(BF16)
            l_ref[r, pl.ds(qs, tq), cs] = jnp.where(low, lse[:tq], lse[tq:])
        return carry

    lax.fori_loop(0, q_ref.shape[0] * nq, body, 0, unroll=16)


def _dil_attention(q, k, v):
    batch, d, length, w = q.shape
    tq, tk = 128, 256
    spec = pl.BlockSpec((None, d, length, w), lambda b: (b, 0, 0, 0))
    return pl.pallas_call(
        functools.partial(_dil_kernel, length=length, tq=tq, tk=tk),
        grid=(batch,),
        in_specs=[spec, spec, spec],
        out_specs=[spec, spec],
        out_shape=[jax.ShapeDtypeStruct(q.shape, BF16), jax.ShapeDtypeStruct(q.shape, F32)],
        scratch_shapes=[pltpu.VMEM((3, 2 * tq, tk), F32)],
        compiler_params=_cparams(("arbitrary",)),
        name=f"dil{d}",
    )(q, k, v)


def _swa_kernel(sink_ref, q_ref, k_ref, v_ref, o_ref, bias_ref, *, length, tq, tk):
    low = lax.broadcasted_iota(I32, (tq, LANES), 1) < 64
    nblk = q_ref.shape[1] // LANES
    _fill_band_bias(bias_ref, tq, SWA_WINDOW)

    def body(j, carry):
        qs = pl.multiple_of(j * tq, tq)
        ks = pl.multiple_of(jnp.clip(qs - SWA_WINDOW, 0, length - tk), SWA_WINDOW)
        bias = bias_ref[(qs - ks) // SWA_WINDOW]
        for b in range(nblk):
            cs = slice((b // 2) * LANES, (b // 2 + 1) * LANES)
            bs = slice(b * LANES, (b + 1) * LANES)
            qst = _split_heads(q_ref[pl.ds(qs, tq), bs], low)
            sinks = (sink_ref[2 * b], sink_ref[2 * b + 1])
            o, _, _ = _band_softmax(qst, k_ref[pl.ds(ks, tk), cs], v_ref[pl.ds(ks, tk), cs], bias, sinks)
            o_ref[pl.ds(qs, tq), bs] = jnp.where(low, o[:tq], o[tq:]).astype(BF16)
        return carry

    lax.fori_loop(0, length // tq, body, 0, unroll=8)


def _swa_attention(sink, q, k, v):
    batch, length, qw = q.shape
    tq, tk = 128, 384
    return pl.pallas_call(
        functools.partial(_swa_kernel, length=length, tq=tq, tk=tk),
        grid=(batch,),
        in_specs=[pl.BlockSpec(memory_space=pltpu.SMEM),
                  pl.BlockSpec((None, length, qw), lambda b: (b, 0, 0)),
                  pl.BlockSpec((None, length, k.shape[2]), lambda b: (b, 0, 0)),
                  pl.BlockSpec((None, length, v.shape[2]), lambda b: (b, 0, 0))],
        out_specs=pl.BlockSpec((None, length, qw), lambda b: (b, 0, 0)),
        out_shape=jax.ShapeDtypeStruct(q.shape, BF16),
        scratch_shapes=[pltpu.VMEM((3, 2 * tq, tk), F32)],
        compiler_params=_cparams(("arbitrary",)),
        name="swa",
    )(sink, q, k, v)


def _route_rows(logits):
    lane = lax.broadcasted_iota(I32, logits.shape, 1).astype(F32)
    big = 1e9
    is_g = lane < N_GROUPS
    gl = jnp.where(is_g, logits, NEG_INF)
    gmax = jnp.max(gl, axis=-1, keepdims=True)
    gsel = jnp.min(jnp.where(is_g & (gl == gmax), lane, big), axis=-1, keepdims=True)
    gw = 1.0 / jnp.sum(jnp.where(is_g, jnp.exp(gl - gmax), 0.0), axis=-1, keepdims=True)
    e_lo = N_GROUPS + gsel * EXPERTS_PER_GROUP
    in_grp = (lane >= e_lo) & (lane < e_lo + EXPERTS_PER_GROUP)
    el = jnp.where(in_grp, logits, NEG_INF)
    m1 = jnp.max(el, axis=-1, keepdims=True)
    i1 = jnp.min(jnp.where(in_grp & (el == m1), lane, big), axis=-1, keepdims=True)
    el2 = jnp.where(lane == i1, NEG_INF, el)
    m2 = jnp.max(el2, axis=-1, keepdims=True)
    i2 = jnp.min(jnp.where(in_grp & (lane != i1) & (el2 == m2), lane, big), axis=-1, keepdims=True)
    t = jnp.exp(m2 - m1)
    tw1 = gw / (1.0 + t)
    tw2 = gw * t / (1.0 + t)
    out = jnp.where(lane == 0, tw1, 0.0)
    out = jnp.where(lane == 1, tw2, out)
    out = jnp.where(lane == 2, i1 - N_GROUPS, out)
    return jnp.where(lane == 3, i2 - N_GROUPS, out)


def _outproj_kernel(o0_ref, l0_ref, o1_ref, l1_ref, o2_ref, l2_ref, ob_ref, sga_ref, sgb_ref, x_ref,
                    mod_ref, g_ref, wa_ref, wb_ref, wo_ref, wr_ref, br_ref,
                    x1_ref, xs_ref, route_ref, len_ref,
                    so1_ref, sl1_ref, so2_ref, sl2_ref, h2_ref, *, tm, sub, group):
    ntiles = pl.num_programs(0) - 1

    @pl.when(pl.program_id(0) == ntiles)
    def _():
        xs_ref[...] = jnp.zeros_like(xs_ref)

    pl.when(pl.program_id(0) < ntiles)(functools.partial(
        _outproj_tile, o0_ref, l0_ref, o1_ref, l1_ref, o2_ref, l2_ref, ob_ref, sga_ref, sgb_ref, x_ref,
        mod_ref, g_ref, wa_ref, wb_ref, wo_ref, wr_ref, br_ref, x1_ref, xs_ref, route_ref, len_ref,
        so1_ref, sl1_ref, so2_ref, sl2_ref, h2_ref, tm=tm, sub=sub, group=group))


def _outproj_tile(o0_ref, l0_ref, o1_ref, l1_ref, o2_ref, l2_ref, ob_ref, sga_ref, sgb_ref, x_ref,
                  mod_ref, g_ref, wa_ref, wb_ref, wo_ref, wr_ref, br_ref,
                  x1_ref, xs_ref, route_ref, len_ref,
                  so1_ref, sl1_ref, so2_ref, sl2_ref, h2_ref, *, tm, sub, group):
    dm = x_ref.shape[1]
    for (o_ref, l_ref, so_ref, sl_ref, d) in ((o1_ref, l1_ref, so1_ref, sl1_ref, DILATIONS[1]),
                                              (o2_ref, l2_ref, so2_ref, sl2_ref, DILATIONS[2])):
        for r in range(d):
            for c in range(2):
                cs = slice(c * LANES, (c + 1) * LANES)
                so_ref[c, pl.ds(r, tm // d, stride=d), :] = o_ref[0, r, :, cs].astype(F32)
                sl_ref[c, pl.ds(r, tm // d, stride=d), :] = l_ref[0, r, :, cs]
    mr = lax.broadcasted_iota(I32, (LANES, LANES), 0)
    mc = lax.broadcasted_iota(I32, (LANES, LANES), 1)
    move_hi = jnp.where(((mr < 2) & (mc == 2 * mr)) | ((mr >= 2) & (mr < 4) & (mc == mr + 2)), 1.0, 0.0).astype(BF16)
    move_lo = jnp.where((mr < 2) & (mc == 2 * mr + 1), 1.0, 0.0).astype(BF16)
    for t in range(tm // sub):
        rs = slice(t * sub, (t + 1) * sub)
        both = lambda ref: jnp.concatenate([ref[0, rs, :], ref[1, rs, :]], axis=1)
        o0, l0 = o0_ref[0, 0, rs, :].astype(F32), l0_ref[0, 0, rs, :]
        o1, l1, o2, l2 = both(so1_ref), both(sl1_ref), both(so2_ref), both(sl2_ref)
        mx = jnp.maximum(jnp.maximum(l0, l1), l2)
        w0, w1, w2 = jnp.exp(l0 - mx), jnp.exp(l1 - mx), jnp.exp(l2 - mx)
        o_a = (w0 * o0 + w1 * o1 + w2 * o2) / (w0 + w1 + w2)
        y_a = jnp.dot(o_a.astype(BF16), wa_ref[...], preferred_element_type=F32)
        y_b = jnp.dot(ob_ref[rs, :], wb_ref[...], preferred_element_type=F32)
        merged = sga_ref[rs, :].astype(F32) * y_a + sgb_ref[rs, :].astype(F32) * y_b
        mix = jnp.dot(merged.astype(BF16), wo_ref[...], preferred_element_type=F32)
        x1 = x_ref[rs, :] + mod_ref[0, 2:3, :] * mix
        x1_ref[rs, :] = x1
        h2 = _rms_mod(x1, g_ref[...], mod_ref[0, 3:4, :], mod_ref[0, 4:5, :]).astype(BF16)
        h2_ref[rs, :dm] = h2
        logits = jnp.dot(h2, wr_ref[...], preferred_element_type=F32) + br_ref[...]
        rt = _route_rows(logits)
        route_ref[rs, :] = rt
        hi = rt.astype(BF16)
        lo = (rt - hi.astype(F32)).astype(BF16)
        aux = (jnp.dot(hi, move_hi, preferred_element_type=F32) + jnp.dot(lo, move_lo, preferred_element_type=F32))
        h2_ref[rs, dm:] = aux.astype(BF16)

    part = route_ref[...]
    e1, e2 = part[:, 2:3], part[:, 3:4]
    lane = lax.broadcasted_iota(I32, (tm, LANES), 1).astype(F32)
    onehot = jnp.where((lane == e1) | (lane == e2), 1.0, 0.0)
    rr = lax.broadcasted_iota(I32, (tm, tm), 0)
    cc = lax.broadcasted_iota(I32, (tm, tm), 1)
    tri = jnp.where(rr > cc, 1.0, 0.0).astype(BF16)
    prefix = jnp.dot(tri, onehot.astype(BF16), preferred_element_type=F32)
    cnt = jnp.sum(onehot, axis=0, keepdims=True)
    seg_len = jnp.ceil(cnt * (1.0 / SEG_ROWS)) * SEG_ROWS
    ur = lax.broadcasted_iota(I32, (LANES, LANES), 0)
    uc = lax.broadcasted_iota(I32, (LANES, LANES), 1)
    upper = jnp.where(ur < uc, 1.0, 0.0).astype(BF16)
    seg_off = jnp.dot(jnp.broadcast_to(seg_len, (8, LANES)).astype(BF16), upper, preferred_element_type=F32)[0:1, :]
    slot_map = seg_off + prefix
    s1 = jnp.sum(jnp.where(lane == e1, slot_map, 0.0), axis=-1, keepdims=True)
    s2 = jnp.sum(jnp.where(lane == e2, slot_map, 0.0), axis=-1, keepdims=True)
    len_ref[0] = jnp.broadcast_to(seg_len, (8, LANES))
    route = jnp.where(lane == 2, s1, jnp.where(lane == 3, s2, part))
    route_ref[...] = route

    pr = lax.broadcasted_iota(I32, (8, LANES), 0)
    pc = lax.broadcasted_iota(I32, (8, LANES), 1)
    lane_pick = jnp.where(pc == pr + 2, 1.0, 0.0)
    slots_t = lax.dot_general(lane_pick, route, (((1,), (1,)), ((), ())), preferred_element_type=F32,
                              precision=lax.Precision.HIGHEST)
    live = jnp.sum(seg_len).astype(I32)

    def sort_rows(lo, rows):
        srow = (lax.broadcasted_iota(I32, (rows, tm), 0) + lo).astype(F32)
        pick = jnp.where((srow == slots_t[0:1, :]) | (srow == slots_t[1:2, :]), 1.0, 0.0).astype(BF16)
        xs_ref[lo:lo + rows, :] = jnp.dot(pick, h2_ref[...], preferred_element_type=F32).astype(BF16)

    def clear_rows(lo, rows):
        xs_ref[lo:lo + rows, :] = jnp.zeros((rows, xs_ref.shape[1]), BF16)

    head = xs_ref.shape[0] - group
    sort_rows(0, head)
    pl.when(live > head)(functools.partial(sort_rows, head, group))
    pl.when(live <= head)(functools.partial(clear_rows, head, group))


def _outproj(dil_outs, ob, sga, sgb, x2, mod3, g_ffn, wa, wb, wo, wr, br, batch, seq, nslots):
    n, dm = x2.shape
    tm = MOE_TILE
    tpb = seq // tm
    ntiles = n // tm
    tile = lambda i: jnp.minimum(i, ntiles - 1)
    row = lambda i: (tile(i), 0)
    const = lambda i: (0, 0)
    in_specs = []
    args = []
    for (o, l), d in zip(dil_outs, DILATIONS):
        spec = pl.BlockSpec((1, d, tm // d, DIL_GROUP_W), lambda i: (tile(i) // tpb, 0, tile(i) % tpb, 0))
        in_specs += [spec, spec]
        args += [o, l]
    in_specs += [pl.BlockSpec((tm, ob.shape[1]), row), pl.BlockSpec((tm, dm), row), pl.BlockSpec((tm, dm), row),
                 pl.BlockSpec((tm, dm), row),
                 pl.BlockSpec((1, 6, dm), lambda i: (tile(i) // tpb, 0, 0)),
                 pl.BlockSpec((1, dm), const),
                 pl.BlockSpec(wa.shape, const), pl.BlockSpec(wb.shape, const), pl.BlockSpec(wo.shape, const),
                 pl.BlockSpec(wr.shape, const), pl.BlockSpec(br.shape, const)]
    args += [ob, sga, sgb, x2, mod3, g_ffn, wa, wb, wo, wr, br]
    width = dm + LANES
    return pl.pallas_call(
        functools.partial(_outproj_kernel, tm=tm, sub=512, group=SLOT_GROUP),
        grid=(ntiles + 1,),
        in_specs=in_specs,
        out_specs=[pl.BlockSpec((tm, dm), row), pl.BlockSpec((nslots, width), lambda i: (i, 0)),
                   pl.BlockSpec((tm, LANES), row), pl.BlockSpec((1, 8, LANES), lambda i: (tile(i), 0, 0))],
        out_shape=[jax.ShapeDtypeStruct((n, dm), F32), jax.ShapeDtypeStruct(((ntiles + 1) * nslots, width), BF16),
                   jax.ShapeDtypeStruct((n, LANES), F32), jax.ShapeDtypeStruct((ntiles, 8, LANES), F32)],
        scratch_shapes=[pltpu.VMEM((2, tm, LANES), F32)] * 4 + [pltpu.VMEM((tm, width), BF16)],
        compiler_params=_cparams(("arbitrary",)),
        name="outproj",
    )(*args)


def _wait_copies(count, copy):
    def wait_one(c, carry):
        copy.wait()
        return carry

    lax.fori_loop(0, count, wait_one, 0)


def _experts_kernel(first_ref, count_ref, chunk_ref, wg_ref, wu_ref, wd_ref, xs_hbm, ys_hbm,
                    xbuf, ybuf, zbuf, xsem, ysem, zsem, *, nblk):
    e = pl.program_id(0)
    last = pl.num_programs(0) - 1
    first = first_ref[e]
    count = count_ref[e]
    used = first_ref[last] + count_ref[last]
    dm = ybuf.shape[2]
    cpb = ROW_BLOCK // SEG_ROWS

    def chunk_rows(blk, j):
        return pl.ds(pl.multiple_of(chunk_ref[blk * cpb + j], SEG_ROWS), SEG_ROWS)

    def gather(blk):
        for j in range(cpb):
            pltpu.make_async_copy(xs_hbm.at[chunk_rows(blk, j), :], xbuf.at[blk & 1, pl.ds(j * SEG_ROWS, SEG_ROWS), :],
                                  xsem.at[blk & 1]).start(priority=1)

    def gather_done(blk):
        return pltpu.make_async_copy(xs_hbm.at[pl.ds(0, ROW_BLOCK), :], xbuf.at[blk & 1], xsem.at[blk & 1])

    def put(blk):
        rows = pl.ds(pl.multiple_of(blk * ROW_BLOCK, ROW_BLOCK), ROW_BLOCK)
        return pltpu.make_async_copy(ybuf.at[blk & 1], ys_hbm.at[rows, :], ysem.at[blk & 1])

    def zero_blk(b):
        rows = pl.ds(pl.multiple_of(b * ROW_BLOCK, ROW_BLOCK), ROW_BLOCK)
        return pltpu.make_async_copy(zbuf, ys_hbm.at[rows, :], zsem)

    @pl.when(e == 0)
    def _():
        zbuf[...] = jnp.zeros_like(zbuf)

        def start(b, carry):
            zero_blk(b).start()
            return carry

        lax.fori_loop(used, nblk, start, 0)

        @pl.when(used > 0)
        def _():
            gather(0)

    def body(b, carry):
        blk = first + b
        slot = blk & 1
        gather_done(blk).wait()

        @pl.when(blk + 1 < used)
        def _():
            gather(blk + 1)

        @pl.when(blk >= 2)
        def _():
            put(blk - 2).wait()

        xb = xbuf[slot, :, :dm]
        aux = xbuf[slot, :, dm:].astype(F32)
        w = jnp.where(aux[:, 4:5] == e.astype(F32), aux[:, 0:1] + aux[:, 1:2], aux[:, 2:3] + aux[:, 3:4])
        g = jnp.dot(xb, wg_ref[0], preferred_element_type=F32)
        u = jnp.dot(xb, wu_ref[0], preferred_element_type=F32)
        a = (g * jax.nn.sigmoid(g)) * u
        y = jnp.dot(a.astype(BF16), wd_ref[0], preferred_element_type=F32)
        ybuf[slot] = (y * w).astype(BF16)
        put(blk).start()
        return carry

    lax.fori_loop(0, count, body, 0)

    @pl.when(e == last)
    def _():
        @pl.when(used >= 2)
        def _():
            put(used - 2).wait()

        @pl.when(used >= 1)
        def _():
            put(used - 1).wait()

        _wait_copies(nblk - used, zero_blk(0))


def _experts(first_blk, count_blk, chunk_tbl, x_sorted, w_gate, w_up, w_down, nblk):
    rows, width = x_sorted.shape
    n_exp, dm, de = w_gate.shape
    wmap = lambda e, *_: (e, 0, 0)
    grid_spec = pltpu.PrefetchScalarGridSpec(
        num_scalar_prefetch=3,
        grid=(n_exp,),
        in_specs=[pl.BlockSpec((1, dm, de), wmap),
                  pl.BlockSpec((1, dm, de), wmap),
                  pl.BlockSpec((1, de, dm), wmap),
                  pl.BlockSpec(memory_space=pl.ANY)],
        out_specs=pl.BlockSpec(memory_space=pl.ANY),
        scratch_shapes=[pltpu.VMEM((2, ROW_BLOCK, width), BF16), pltpu.VMEM((2, ROW_BLOCK, dm), BF16),
                        pltpu.VMEM((ROW_BLOCK, dm), BF16),
                        pltpu.SemaphoreType.DMA((2,)), pltpu.SemaphoreType.DMA((2,)), pltpu.SemaphoreType.DMA(())],
    )
    return pl.pallas_call(
        functools.partial(_experts_kernel, nblk=nblk),
        grid_spec=grid_spec,
        out_shape=jax.ShapeDtypeStruct((nblk * ROW_BLOCK, dm), BF16),
        compiler_params=_cparams(("arbitrary",)),
        name="experts",
    )(first_blk, count_blk, chunk_tbl, w_gate, w_up, w_down, x_sorted)


def _slot_parts(tm, nslots, group):
    return [(0, 2 * tm)] + [(lo, group) for lo in range(2 * tm, nslots, group)]


def _combine_kernel(src_ref, live_ref, route_ref, x1_ref, mod_ref, g_ref, y_ref, o_ref, ys_ref, sem,
                    *, tm, nslots, group):
    i = pl.program_id(0)
    last = pl.num_programs(0) - 1
    slot = i & 1
    nchk = nslots // SEG_ROWS
    parts = _slot_parts(tm, nslots, group)

    def for_live_parts(tile, fn):
        for p, (lo, rows) in enumerate(parts):
            if p == 0:
                fn(lo, rows)
            else:
                pl.when(live_ref[tile] > lo)(functools.partial(fn, lo, rows))

    def gather(tile, s):
        def start(lo, rows):
            for c in range(lo // SEG_ROWS, (lo + rows) // SEG_ROWS):
                src = pl.multiple_of(src_ref[tile * nchk + c], SEG_ROWS)
                pltpu.make_async_copy(y_ref.at[pl.ds(src, SEG_ROWS), :],
                                      ys_ref.at[s, pl.ds(c * SEG_ROWS, SEG_ROWS), :], sem.at[s]).start()

        for_live_parts(tile, start)

    def wait(tile, s):
        for_live_parts(tile, lambda lo, rows: pltpu.make_async_copy(
            y_ref.at[pl.ds(0, rows), :], ys_ref.at[s, pl.ds(lo, rows), :], sem.at[s]).wait())

    @pl.when(i == 0)
    def _():
        gather(0, 0)

    nxt = jnp.minimum(i + 1, last)
    gather(nxt, 1 - slot)
    wait(i, slot)
    route = route_ref[...]

    def finish(rows):
        scol = lax.broadcasted_iota(I32, (tm, rows), 1).astype(F32)
        pick = jnp.where((scol == route[:, 2:3]) | (scol == route[:, 3:4]), 1.0, 0.0).astype(BF16)
        moe = jnp.dot(pick, ys_ref[slot, :rows, :], preferred_element_type=F32)
        x = x1_ref[...] + mod_ref[0, 5:6, :] * moe
        ms = jnp.mean(x * x, axis=-1, keepdims=True)
        o_ref[...] = (x * lax.rsqrt(ms + RMS_EPS)) * g_ref[...]

    ends = [lo + rows for lo, rows in parts]
    for p, end in enumerate(ends):
        above = live_ref[i] > (ends[p - 1] if p else -1)
        cond = above if p == len(ends) - 1 else above & (live_ref[i] <= end)
        pl.when(cond)(functools.partial(finish, end))

    @pl.when(i == last)
    def _():
        wait(nxt, 1 - slot)


def _combine(src_tbl, live_tbl, route, x1, mod3, g_final, y_rows, seq, tm, nslots):
    n, dm = x1.shape
    tpb = seq // tm
    grid_spec = pltpu.PrefetchScalarGridSpec(
        num_scalar_prefetch=2,
        grid=(n // tm,),
        in_specs=[pl.BlockSpec((tm, LANES), lambda i, *_: (i, 0)),
                  pl.BlockSpec((tm, dm), lambda i, *_: (i, 0)),
                  pl.BlockSpec((1, 6, dm), lambda i, *_: (i // tpb, 0, 0)),
                  pl.BlockSpec((1, dm), lambda i, *_: (0, 0)),
                  pl.BlockSpec(memory_space=pl.ANY)],
        out_specs=pl.BlockSpec((tm, dm), lambda i, *_: (i, 0)),
        scratch_shapes=[pltpu.VMEM((2, nslots, dm), BF16), pltpu.SemaphoreType.DMA((2,))],
    )
    return pl.pallas_call(
        functools.partial(_combine_kernel, tm=tm, nslots=nslots, group=SLOT_GROUP),
        grid_spec=grid_spec,
        out_shape=jax.ShapeDtypeStruct((n, dm), F32),
        compiler_params=_cparams(("arbitrary",)),
        name="combine",
    )(src_tbl, live_tbl, route, x1, mod3, g_final, y_rows)


def _rope_angles(positions):
    half = HEAD_DIM // 2
    inv_freq = ROPE_THETA ** (-jnp.arange(half, dtype=F32) * (2.0 / HEAD_DIM))
    freq = jnp.tile(inv_freq, LANES // half)
    sign = jnp.tile(jnp.concatenate([-jnp.ones((half,), F32), jnp.ones((half,), F32)]), LANES // HEAD_DIM)
    ang = positions.astype(F32).reshape(-1, 1) * freq
    return ang, sign.reshape(1, LANES)


def _expert_chunk_table(lens, pstart, tot, pend, nblk, nslots):
    ntiles = lens.shape[0]
    cpb = ROW_BLOCK // SEG_ROWS
    runs = lens // SEG_ROWS
    q = jnp.arange(nblk * cpb, dtype=I32)
    owner = jnp.minimum(jnp.sum((pend[None, :] // SEG_ROWS <= q[:, None]).astype(I32), axis=1), N_EXPERTS - 1)
    is_e = owner[:, None] == jnp.arange(N_EXPERTS, dtype=I32)[None, :]
    pick_e = lambda v: jnp.sum(jnp.where(is_e, v[None, :], 0), axis=1)
    off = q - pick_e(pstart // SEG_ROWS)
    in_run = (off < pick_e(tot // SEG_ROWS)) & (q < pend[-1] // SEG_ROWS)
    upto = jnp.cumsum(runs, axis=0)
    upto_e = jnp.sum(jnp.where(is_e[:, None, :], upto[None, :, :], 0), axis=2)
    tile = jnp.minimum(jnp.sum((upto_e <= off[:, None]).astype(I32), axis=1), ntiles - 1)
    is_t = tile[:, None] == jnp.arange(ntiles, dtype=I32)[None, :]
    run_slot0 = jnp.cumsum(runs, axis=1) - runs - (upto - runs)
    slot0 = jnp.sum(jnp.where(is_t[:, :, None] & is_e[:, None, :], run_slot0[None, :, :], 0), axis=(1, 2))
    live_row = tile * nslots + (slot0 + off) * SEG_ROWS
    spare_row = ntiles * nslots + (((q // cpb) % 2) * cpb + q % cpb) * SEG_ROWS
    return jnp.where(in_run, live_row, spare_row)


def kernel(x, c, positions, w_ada, b_ada, g_mix, w_in, sink_logits, w_branch_a, w_branch_b, w_out, g_ffn,
           w_group, b_group, w_route, b_route, w_expert_gate, w_expert_up, w_expert_down, g_final):
    batch, seq, dm = x.shape
    n = batch * seq
    assert w_ada.shape[0] == 1, "one layer"
    x2 = x.reshape(n, dm)

    c8 = jnp.pad(c, ((0, 8 - batch), (0, 0)))
    mod = _ada(c8, w_ada[0], b_ada[0].reshape(1, -1))
    mod3 = mod[:batch].reshape(batch, 6, dm)

    angles, sign = _rope_angles(positions)
    outs = _inproj(x2, mod3, g_mix[0].reshape(1, dm), angles, sign, w_in[0].astype(BF16),
                   (w_expert_gate[0], w_expert_up[0], w_expert_down[0]), batch, seq)
    qkv = outs[:9]
    qb, kb, vb, sga, sgb = outs[9:14]
    expert_w_bf = outs[14:]

    dil_outs = [_dil_attention(qkv[3 * g], qkv[3 * g + 1], qkv[3 * g + 2]) for g in range(len(DILATIONS))]
    ob = _swa_attention(sink_logits[0], qb.reshape(batch, seq, -1), kb.reshape(batch, seq, -1),
                        vb.reshape(batch, seq, -1)).reshape(n, -1)

    pad = LANES - N_GROUPS - N_EXPERTS
    wr = jnp.concatenate([w_group[0], w_route[0], jnp.zeros((dm, pad), F32)], axis=1).astype(BF16)
    br = jnp.concatenate([b_group[0], b_route[0], jnp.zeros((pad,), F32)]).reshape(1, LANES)
    ntiles = n // MOE_TILE
    nslots = 2 * MOE_TILE + N_EXPERTS * SEG_ROWS
    x1, x_sorted, route, seg_lens = _outproj(dil_outs, ob, sga, sgb, x2, mod3, g_ffn[0].reshape(1, dm),
                                             w_branch_a[0].astype(BF16), w_branch_b[0].astype(BF16),
                                             w_out[0].astype(BF16), wr, br, batch, seq, nslots)

    nblk = -(-(2 * n + ntiles * N_EXPERTS * (SEG_ROWS - 1)) // ROW_BLOCK) + N_EXPERTS
    lens = seg_lens[:, 0, :N_EXPERTS].astype(I32)
    tot = jnp.sum(lens, axis=0)
    padded = (tot + ROW_BLOCK - 1) // ROW_BLOCK * ROW_BLOCK
    pend = jnp.cumsum(padded)
    pstart = pend - padded
    chunk_tbl = _expert_chunk_table(lens, pstart, tot, pend, nblk, nslots)
    live_tbl = jnp.sum(lens, axis=1)

    nchk = nslots // SEG_ROWS
    base = pstart[None, :] + jnp.cumsum(lens, axis=0) - lens
    run_end = jnp.cumsum(lens // SEG_ROWS, axis=1)
    chunk = jnp.arange(nchk, dtype=I32)
    owner = jnp.sum((run_end[:, None, :] <= chunk[None, :, None]).astype(I32), axis=2)
    is_owner = owner[:, :, None] == jnp.arange(N_EXPERTS, dtype=I32)[None, None, :]
    run_row0 = base - (run_end - lens // SEG_ROWS) * SEG_ROWS
    row = jnp.sum(jnp.where(is_owner, run_row0[:, None, :], 0), axis=2) + chunk[None, :] * SEG_ROWS
    src_tbl = jnp.where(owner < N_EXPERTS, row, chunk[None, :] * SEG_ROWS).reshape(-1)

    y_rows = _experts(pstart // ROW_BLOCK, padded // ROW_BLOCK, chunk_tbl, x_sorted, *expert_w_bf, nblk)
    out = _combine(src_tbl, live_tbl, route, x1, mod3, g_final.reshape(1, dm), y_rows, seq, MOE_TILE, nslots)
    return out.reshape(batch, seq, dm)
```

```python
import functools

import jax
import jax.numpy as jnp
from jax import lax
from jax.experimental import pallas as pl
from jax.experimental.pallas import tpu as pltpu

F32 = jnp.float32
BF16 = jnp.bfloat16
I32 = jnp.int32

HEAD_DIM = 64
ROPE_THETA = 10000.0
RMS_EPS = 1e-6
NEG_INF = -1e30
Q_SCALE = HEAD_DIM ** -0.5
DILATIONS = (1, 4, 16)
DIL_HALF_WINDOW = 64
DIL_GROUP_W = 256
SWA_WINDOW = 128
N_GROUPS = 4
EXPERTS_PER_GROUP = 8
N_EXPERTS = 32
LANES = 128
ROW_BLOCK = 512
SEG_ROWS = 16
MOE_TILE = 512
SLOT_GROUP = 256
VMEM_LIMIT = 56 * 1024 * 1024


def _cparams(sem):
    return pltpu.CompilerParams(dimension_semantics=sem, vmem_limit_bytes=VMEM_LIMIT)


def _ada_kernel(c_ref, w_ref, b_ref, o_ref):
    c = c_ref[...]
    cs = c * jax.nn.sigmoid(c)
    o_ref[...] = jnp.dot(cs.astype(BF16), w_ref[...].astype(BF16), preferred_element_type=F32) + b_ref[...]


def _ada(c8, w_ada, b_ada):
    d, n = w_ada.shape
    tn = 1536
    return pl.pallas_call(
        _ada_kernel,
        grid=(n // tn,),
        in_specs=[pl.BlockSpec((8, d), lambda j: (0, 0)),
                  pl.BlockSpec((d, tn), lambda j: (0, j)),
                  pl.BlockSpec((1, tn), lambda j: (0, j))],
        out_specs=pl.BlockSpec((8, tn), lambda j: (0, j)),
        out_shape=jax.ShapeDtypeStruct((8, n), F32),
        compiler_params=_cparams(("arbitrary",)),
        name="ada",
    )(c8, w_ada, b_ada)


def _rms_mod(x, g, shift, scale):
    ms = jnp.mean(x * x, axis=-1, keepdims=True)
    return (x * lax.rsqrt(ms + RMS_EPS)) * (g * (1.0 + scale)) + shift


def _inproj_kernel(x_ref, mod_ref, g_ref, ang_ref, sign_ref, w_ref, eg_ref, eu_ref, ed_ref,
                   q0_ref, k0_ref, v0_ref, q1_ref, k1_ref, v1_ref, q2_ref, k2_ref, v2_ref,
                   qb_ref, kb_ref, vb_ref, sga_ref, sgb_ref, egb_ref, eub_ref, edb_ref, stg_ref, *, tm, n_exp):
    def cast_expert():
        egb_ref[...] = eg_ref[...].astype(BF16)
        eub_ref[...] = eu_ref[...].astype(BF16)
        edb_ref[...] = ed_ref[...].astype(BF16)

    if n_exp is None:
        cast_expert()
    else:
        pl.when(pl.program_id(0) < n_exp)(cast_expert)

    h = _rms_mod(x_ref[...], g_ref[...], mod_ref[0, 0:1, :], mod_ref[0, 1:2, :])
    hb = h.astype(BF16)
    ang = ang_ref[...]
    cos = jnp.cos(ang)
    sin = jnp.sin(ang) * sign_ref[...]
    lane = lax.broadcasted_iota(I32, (tm, LANES), 1)
    first_half = (lane & 32) == 0
    low = lane < 64

    def proj(c0, width):
        return jnp.dot(hb, w_ref[:, c0:c0 + width], preferred_element_type=F32)

    def rope(t):
        rot = jnp.where(first_half, pltpu.roll(t, 96, 1), pltpu.roll(t, 32, 1))
        return t * cos + rot * sin

    def rope256(p):
        return jnp.concatenate([rope(p[:, :LANES]), rope(p[:, LANES:])], axis=1)

    def store_group(ref, val, d):
        if d == 1:
            ref[0, 0] = val.astype(BF16)
        else:
            for c in range(2):
                stg_ref[c] = val[:, c * LANES:(c + 1) * LANES]
            for r in range(d):
                for c in range(2):
                    ref[0, r, :, c * LANES:(c + 1) * LANES] = (
                        stg_ref[c, pl.ds(r, tm // d, stride=d), :].astype(BF16))

    q_refs = (q0_ref, q1_ref, q2_ref)
    k_refs = (k0_ref, k1_ref, k2_ref)
    v_refs = (v0_ref, v1_ref, v2_ref)
    for g, d in enumerate(DILATIONS):
        store_group(q_refs[g], rope256(proj(g * 256, 256)) * Q_SCALE, d)
        store_group(k_refs[g], rope256(proj(768 + g * 256, 256)), d)
        store_group(v_refs[g], proj(1536 + g * 256, 256), d)
    for j in range(2):
        qb_ref[:, j * 256:(j + 1) * 256] = (rope256(proj(2304 + j * 256, 256)) * Q_SCALE).astype(BF16)
    kv = proj(2816, 256)
    kb = rope(kv[:, :LANES])
    vb = kv[:, LANES:]
    kb_sw = pltpu.roll(kb, 64, 1)
    vb_sw = pltpu.roll(vb, 64, 1)
    kb_ref[:, :LANES] = jnp.where(low, kb, kb_sw).astype(BF16)
    kb_ref[:, LANES:] = jnp.where(low, kb_sw, kb).astype(BF16)
    vb_ref[:, :LANES] = jnp.where(low, vb, vb_sw).astype(BF16)
    vb_ref[:, LANES:] = jnp.where(low, vb_sw, vb).astype(BF16)
    for j in range(4):
        sga_ref[:, j * 256:(j + 1) * 256] = jax.nn.sigmoid(proj(3072 + j * 256, 256)).astype(BF16)
        sgb_ref[:, j * 256:(j + 1) * 256] = jax.nn.sigmoid(proj(4096 + j * 256, 256)).astype(BF16)


def _inproj(x2, mod3, g_mix, angles, sign, w_in_bf, expert_w, batch, seq):
    n, dm = x2.shape
    tm = 512
    tpb = seq // tm
    grid = (n // tm,)
    row = lambda i: (i, 0)
    strided_specs, strided_shapes = [], []
    for d in DILATIONS:
        for _ in range(3):
            strided_specs.append(pl.BlockSpec((1, d, tm // d, DIL_GROUP_W), lambda i: (i // tpb, 0, i % tpb, 0)))
            strided_shapes.append(jax.ShapeDtypeStruct((batch, d, seq // d, DIL_GROUP_W), BF16))
    out_specs = strided_specs + [
        pl.BlockSpec((tm, 512), row), pl.BlockSpec((tm, 256), row), pl.BlockSpec((tm, 256), row),
        pl.BlockSpec((tm, dm), row), pl.BlockSpec((tm, dm), row)]
    out_shapes = strided_shapes + [
        jax.ShapeDtypeStruct((n, 512), BF16), jax.ShapeDtypeStruct((n, 256), BF16),
        jax.ShapeDtypeStruct((n, 256), BF16), jax.ShapeDtypeStruct((n, dm), BF16),
        jax.ShapeDtypeStruct((n, dm), BF16)]
    n_exp = expert_w[0].shape[0]
    assert n // tm >= n_exp, "one expert's weights are cast per grid step"
    emap = lambda i: (jnp.minimum(i, n_exp - 1), 0, 0)
    expert_specs = [pl.BlockSpec((1,) + w.shape[1:], emap) for w in expert_w]
    return pl.pallas_call(
        functools.partial(_inproj_kernel, tm=tm, n_exp=None if n // tm == n_exp else n_exp),
        grid=grid,
        in_specs=[pl.BlockSpec((tm, dm), row),
                  pl.BlockSpec((1, 6, dm), lambda i: (i // tpb, 0, 0)),
                  pl.BlockSpec((1, dm), lambda i: (0, 0)),
                  pl.BlockSpec((tm, LANES), row),
                  pl.BlockSpec((1, LANES), lambda i: (0, 0)),
                  pl.BlockSpec(w_in_bf.shape, lambda i: (0, 0), pipeline_mode=pl.Buffered(1))] + expert_specs,
        out_specs=out_specs + expert_specs,
        out_shape=out_shapes + [jax.ShapeDtypeStruct(w.shape, BF16) for w in expert_w],
        scratch_shapes=[pltpu.VMEM((2, tm, LANES), F32)],
        compiler_params=_cparams(("arbitrary",)),
        name="inproj",
    )(x2, mod3, g_mix, angles, sign, w_in_bf, *expert_w)


def _split_heads(q2, low):
    zero = jnp.zeros_like(q2)
    return jnp.concatenate([jnp.where(low, q2, zero), jnp.where(low, zero, q2)], axis=0)


def _band_softmax(qst, k2, v2, bias, sinks):
    s = lax.dot_general(qst, k2, (((1,), (1,)), ((), ())), preferred_element_type=F32)
    s = s + bias
    rows, tk = s.shape
    m = jnp.max(s, axis=-1, keepdims=True)
    if sinks is not None:
        seg = rows // len(sinks)
        m = jnp.concatenate([jnp.maximum(m[h * seg:(h + 1) * seg], sk) for h, sk in enumerate(sinks)], axis=0)
    m = jnp.broadcast_to(m, (rows, LANES))
    e = jnp.concatenate([jnp.exp(s[:, c * LANES:(c + 1) * LANES] - m) for c in range(tk // LANES)], axis=1)
    v_ones = jnp.concatenate([v2, jnp.ones((tk, LANES), BF16)], axis=1)
    od = jnp.dot(e.astype(BF16), v_ones, preferred_element_type=F32)
    o, den = od[:, :LANES], od[:, LANES:]
    if sinks is not None:
        den = jnp.concatenate([den[h * seg:(h + 1) * seg] + jnp.exp(sk - m[h * seg:(h + 1) * seg])
                               for h, sk in enumerate(sinks)], axis=0)
    return o / den, m, den


def _fill_band_bias(bias_ref, tq, window):
    rows, tk = bias_ref.shape[1:]
    row = lax.broadcasted_iota(I32, (rows, tk), 0) & (tq - 1)
    col = lax.broadcasted_iota(I32, (rows, tk), 1)
    for i in range(bias_ref.shape[0]):
        bias_ref[i] = jnp.where(jnp.abs(col - row - i * window) <= window, 0.0, NEG_INF)


def _dil_kernel(q_ref, k_ref, v_ref, o_ref, l_ref, bias_ref, *, length, tq, tk):
    low = lax.broadcasted_iota(I32, (tq, LANES), 1) < 64
    _fill_band_bias(bias_ref, tq, DIL_HALF_WINDOW)
    nq = length // tq

    def body(j, carry):
        r = j // nq
        qs = pl.multiple_of((j % nq) * tq, tq)
        ks = pl.multiple_of(jnp.clip(qs - DIL_HALF_WINDOW, 0, length - tk), DIL_HALF_WINDOW)
        which = (qs - ks) // DIL_HALF_WINDOW
        for c in range(DIL_GROUP_W // LANES):
            cs = slice(c * LANES, (c + 1) * LANES)
            qst = _split_heads(q_ref[r, pl.ds(qs, tq), cs], low)
            o, m, den = _band_softmax(qst, k_ref[r, pl.ds(ks, tk), cs], v_ref[r, pl.ds(ks, tk), cs],
                                      bias_ref[which], None)
            lse = m + jnp.log(den)
            o_ref[r, pl.ds(qs, tq), cs] = jnp.where(low, o[:tq], o[tq:]).astype(BF16)
            l_ref[r, pl.ds(qs, tq), cs] = jnp.where(low, lse[:tq], lse[tq:])
        return carry

    lax.fori_loop(0, q_ref.shape[0] * nq, body, 0, unroll=32)


def _dil_attention(q, k, v):
    batch, d, length, w = q.shape
    tq, tk = 128, 256
    spec = pl.BlockSpec((None, d, length, w), lambda b: (b, 0, 0, 0))
    return pl.pallas_call(
        functools.partial(_dil_kernel, length=length, tq=tq, tk=tk),
        grid=(batch,),
        in_specs=[spec, spec, spec],
        out_specs=[spec, spec],
        out_shape=[jax.ShapeDtypeStruct(q.shape, BF16), jax.ShapeDtypeStruct(q.shape, F32)],
        scratch_shapes=[pltpu.VMEM((3, 2 * tq, tk), F32)],
        compiler_params=_cparams(("arbitrary",)),
        name=f"dil{d}",
    )(q, k, v)


def _swa_kernel(sink_ref, q_ref, k_ref, v_ref, o_ref, bias_ref, *, length, tq, tk):
    low = lax.broadcasted_iota(I32, (tq, LANES), 1) < 64
    nblk = q_ref.shape[1] // LANES
    _fill_band_bias(bias_ref, tq, SWA_WINDOW)

    def body(j, carry):
        qs = pl.multiple_of(j * tq, tq)
        ks = pl.multiple_of(jnp.clip(qs - SWA_WINDOW, 0, length - tk), SWA_WINDOW)
        bias = bias_ref[(qs - ks) // SWA_WINDOW]
        for b in range(nblk):
            cs = slice((b // 2) * LANES, (b // 2 + 1) * LANES)
            bs = slice(b * LANES, (b + 1) * LANES)
            qst = _split_heads(q_ref[pl.ds(qs, tq), bs], low)
            sinks = (sink_ref[2 * b], sink_ref[2 * b + 1])
            o, _, _ = _band_softmax(qst, k_ref[pl.ds(ks, tk), cs], v_ref[pl.ds(ks, tk), cs], bias, sinks)
            o_ref[pl.ds(qs, tq), bs] = jnp.where(low, o[:tq], o[tq:]).astype(BF16)
        return carry

    lax.fori_loop(0, length // tq, body, 0, unroll=8)


def _swa_attention(sink, q, k, v):
    batch, length, qw = q.shape
    tq, tk = 128, 384
    return pl.pallas_call(
        functools.partial(_swa_kernel, length=length, tq=tq, tk=tk),
        grid=(batch,),
        in_specs=[pl.BlockSpec(memory_space=pltpu.SMEM),
                  pl.BlockSpec((None, length, qw), lambda b: (b, 0, 0)),
                  pl.BlockSpec((None, length, k.shape[2]), lambda b: (b, 0, 0)),
                  pl.BlockSpec((None, length, v.shape[2]), lambda b: (b, 0, 0))],
        out_specs=pl.BlockSpec((None, length, qw), lambda b: (b, 0, 0)),
        out_shape=jax.ShapeDtypeStruct(q.shape, BF16),
        scratch_shapes=[pltpu.VMEM((3, 2 * tq, tk), F32)],
        compiler_params=_cparams(("arbitrary",)),
        name="swa",
    )(sink, q, k, v)


def _route_rows(logits):
    lane = lax.broadcasted_iota(I32, logits.shape, 1).astype(F32)
    big = 1e9
    is_g = lane < N_GROUPS
    gl = jnp.where(is_g, logits, NEG_INF)
    gmax = jnp.max(gl, axis=-1, keepdims=True)
    gsel = jnp.min(jnp.where(is_g & (gl == gmax), lane, big), axis=-1, keepdims=True)
    gw = 1.0 / jnp.sum(jnp.where(is_g, jnp.exp(gl - gmax), 0.0), axis=-1, keepdims=True)
    e_lo = N_GROUPS + gsel * EXPERTS_PER_GROUP
    in_grp = (lane >= e_lo) & (lane < e_lo + EXPERTS_PER_GROUP)
    el = jnp.where(in_grp, logits, NEG_INF)
    m1 = jnp.max(el, axis=-1, keepdims=True)
    i1 = jnp.min(jnp.where(in_grp & (el == m1), lane, big), axis=-1, keepdims=True)
    el2 = jnp.where(lane == i1, NEG_INF, el)
    m2 = jnp.max(el2, axis=-1, keepdims=True)
    i2 = jnp.min(jnp.where(in_grp & (lane != i1) & (el2 == m2), lane, big), axis=-1, keepdims=True)
    t = jnp.exp(m2 - m1)
    tw1 = gw / (1.0 + t)
    tw2 = gw * t / (1.0 + t)
    out = jnp.where(lane == 0, tw1, 0.0)
    out = jnp.where(lane == 1, tw2, out)
    out = jnp.where(lane == 2, i1 - N_GROUPS, out)
    return jnp.where(lane == 3, i2 - N_GROUPS, out)


def _outproj_kernel(o0_ref, l0_ref, o1_ref, l1_ref, o2_ref, l2_ref, ob_ref, sga_ref, sgb_ref, x_ref,
                    mod_ref, g_ref, wa_ref, wb_ref, wo_ref, wr_ref, br_ref,
                    x1_ref, xs_ref, route_ref, len_ref,
                    so1_ref, sl1_ref, so2_ref, sl2_ref, h2_ref, *, tm, sub, group):
    ntiles = pl.num_programs(0) - 1

    @pl.when(pl.program_id(0) == ntiles)
    def _():
        xs_ref[...] = jnp.zeros_like(xs_ref)

    pl.when(pl.program_id(0) < ntiles)(functools.partial(
        _outproj_tile, o0_ref, l0_ref, o1_ref, l1_ref, o2_ref, l2_ref, ob_ref, sga_ref, sgb_ref, x_ref,
        mod_ref, g_ref, wa_ref, wb_ref, wo_ref, wr_ref, br_ref, x1_ref, xs_ref, route_ref, len_ref,
        so1_ref, sl1_ref, so2_ref, sl2_ref, h2_ref, tm=tm, sub=sub, group=group))


def _outproj_tile(o0_ref, l0_ref, o1_ref, l1_ref, o2_ref, l2_ref, ob_ref, sga_ref, sgb_ref, x_ref,
                  mod_ref, g_ref, wa_ref, wb_ref, wo_ref, wr_ref, br_ref,
                  x1_ref, xs_ref, route_ref, len_ref,
                  so1_ref, sl1_ref, so2_ref, sl2_ref, h2_ref, *, tm, sub, group):
    dm = x_ref.shape[1]
    for (o_ref, l_ref, so_ref, sl_ref, d) in ((o1_ref, l1_ref, so1_ref, sl1_ref, DILATIONS[1]),
                                              (o2_ref, l2_ref, so2_ref, sl2_ref, DILATIONS[2])):
        for r in range(d):
            for c in range(2):
                cs = slice(c * LANES, (c + 1) * LANES)
                so_ref[c, pl.ds(r, tm // d, stride=d), :] = o_ref[0, r, :, cs].astype(F32)
                sl_ref[c, pl.ds(r, tm // d, stride=d), :] = l_ref[0, r, :, cs]
    mr = lax.broadcasted_iota(I32, (LANES, LANES), 0)
    mc = lax.broadcasted_iota(I32, (LANES, LANES), 1)
    move_hi = jnp.where(((mr < 2) & (mc == 2 * mr)) | ((mr >= 2) & (mr < 4) & (mc == mr + 2)), 1.0, 0.0).astype(BF16)
    move_lo = jnp.where((mr < 2) & (mc == 2 * mr + 1), 1.0, 0.0).astype(BF16)
    for t in range(tm // sub):
        rs = slice(t * sub, (t + 1) * sub)
        both = lambda ref: jnp.concatenate([ref[0, rs, :], ref[1, rs, :]], axis=1)
        o0, l0 = o0_ref[0, 0, rs, :].astype(F32), l0_ref[0, 0, rs, :]
        o1, l1, o2, l2 = both(so1_ref), both(sl1_ref), both(so2_ref), both(sl2_ref)
        mx = jnp.maximum(jnp.maximum(l0, l1), l2)
        w0, w1, w2 = jnp.exp(l0 - mx), jnp.exp(l1 - mx), jnp.exp(l2 - mx)
        o_a = (w0 * o0 + w1 * o1 + w2 * o2) / (w0 + w1 + w2)
        y_a = jnp.dot(o_a.astype(BF16), wa_ref[...], preferred_element_type=F32)
        y_b = jnp.dot(ob_ref[rs, :], wb_ref[...], preferred_element_type=F32)
        merged = sga_ref[rs, :].astype(F32) * y_a + sgb_ref[rs, :].astype(F32) * y_b
        mix = jnp.dot(merged.astype(BF16), wo_ref[...], preferred_element_type=F32)
        x1 = x_ref[rs, :] + mod_ref[0, 2:3, :] * mix
        x1_ref[rs, :] = x1
        h2 = _rms_mod(x1, g_ref[...], mod_ref[0, 3:4, :], mod_ref[0, 4:5, :]).astype(BF16)
        h2_ref[rs, :dm] = h2
        logits = jnp.dot(h2, wr_ref[...], preferred_element_type=F32) + br_ref[...]
        rt = _route_rows(logits)
        route_ref[rs, :] = rt
        hi = rt.astype(BF16)
        lo = (rt - hi.astype(F32)).astype(BF16)
        aux = (jnp.dot(hi, move_hi, preferred_element_type=F32) + jnp.dot(lo, move_lo, preferred_element_type=F32))
        h2_ref[rs, dm:] = aux.astype(BF16)

    part = route_ref[...]
    e1, e2 = part[:, 2:3], part[:, 3:4]
    lane = lax.broadcasted_iota(I32, (tm, LANES), 1).astype(F32)
    onehot = jnp.where((lane == e1) | (lane == e2), 1.0, 0.0)
    rr = lax.broadcasted_iota(I32, (tm, tm), 0)
    cc = lax.broadcasted_iota(I32, (tm, tm), 1)
    tri = jnp.where(rr > cc, 1.0, 0.0).astype(BF16)
    prefix = jnp.dot(tri, onehot.astype(BF16), preferred_element_type=F32)
    cnt = jnp.sum(onehot, axis=0, keepdims=True)
    seg_len = jnp.ceil(cnt * (1.0 / SEG_ROWS)) * SEG_ROWS
    ur = lax.broadcasted_iota(I32, (LANES, LANES), 0)
    uc = lax.broadcasted_iota(I32, (LANES, LANES), 1)
    upper = jnp.where(ur < uc, 1.0, 0.0).astype(BF16)
    seg_off = jnp.dot(jnp.broadcast_to(seg_len, (8, LANES)).astype(BF16), upper, preferred_element_type=F32)[0:1, :]
    slot_map = seg_off + prefix
    s1 = jnp.sum(jnp.where(lane == e1, slot_map, 0.0), axis=-1, keepdims=True)
    s2 = jnp.sum(jnp.where(lane == e2, slot_map, 0.0), axis=-1, keepdims=True)
    len_ref[0] = jnp.broadcast_to(seg_len, (8, LANES))
    route = jnp.where(lane == 2, s1, jnp.where(lane == 3, s2, part))
    route_ref[...] = route

    pr = lax.broadcasted_iota(I32, (8, LANES), 0)
    pc = lax.broadcasted_iota(I32, (8, LANES), 1)
    lane_pick = jnp.where(pc == pr + 2, 1.0, 0.0)
    slots_t = lax.dot_general(lane_pick, route, (((1,), (1,)), ((), ())), preferred_element_type=F32,
                              precision=lax.Precision.HIGHEST)
    live = jnp.sum(seg_len).astype(I32)

    def sort_rows(lo, rows):
        srow = (lax.broadcasted_iota(I32, (rows, tm), 0) + lo).astype(F32)
        pick = jnp.where((srow == slots_t[0:1, :]) | (srow == slots_t[1:2, :]), 1.0, 0.0).astype(BF16)
        xs_ref[lo:lo + rows, :] = jnp.dot(pick, h2_ref[...], preferred_element_type=F32).astype(BF16)

    def clear_rows(lo, rows):
        xs_ref[lo:lo + rows, :] = jnp.zeros((rows, xs_ref.shape[1]), BF16)

    head = xs_ref.shape[0] - group
    sort_rows(0, head)
    pl.when(live > head)(functools.partial(sort_rows, head, group))
    pl.when(live <= head)(functools.partial(clear_rows, head, group))


def _outproj(dil_outs, ob, sga, sgb, x2, mod3, g_ffn, wa, wb, wo, wr, br, batch, seq, nslots):
    n, dm = x2.shape
    tm = MOE_TILE
    tpb = seq // tm
    ntiles = n // tm
    tile = lambda i: jnp.minimum(i, ntiles - 1)
    row = lambda i: (tile(i), 0)
    const = lambda i: (0, 0)
    in_specs = []
    args = []
    for (o, l), d in zip(dil_outs, DILATIONS):
        spec = pl.BlockSpec((1, d, tm // d, DIL_GROUP_W), lambda i: (tile(i) // tpb, 0, tile(i) % tpb, 0))
        in_specs += [spec, spec]
        args += [o, l]
    in_specs += [pl.BlockSpec((tm, ob.shape[1]), row), pl.BlockSpec((tm, dm), row), pl.BlockSpec((tm, dm), row),
                 pl.BlockSpec((tm, dm), row),
                 pl.BlockSpec((1, 6, dm), lambda i: (tile(i) // tpb, 0, 0)),
                 pl.BlockSpec((1, dm), const),
                 pl.BlockSpec(wa.shape, const), pl.BlockSpec(wb.shape, const), pl.BlockSpec(wo.shape, const),
                 pl.BlockSpec(wr.shape, const), pl.BlockSpec(br.shape, const)]
    args += [ob, sga, sgb, x2, mod3, g_ffn, wa, wb, wo, wr, br]
    width = dm + LANES
    return pl.pallas_call(
        functools.partial(_outproj_kernel, tm=tm, sub=512, group=SLOT_GROUP),
        grid=(ntiles + 1,),
        in_specs=in_specs,
        out_specs=[pl.BlockSpec((tm, dm), row), pl.BlockSpec((nslots, width), lambda i: (i, 0)),
                   pl.BlockSpec((tm, LANES), row), pl.BlockSpec((1, 8, LANES), lambda i: (tile(i), 0, 0))],
        out_shape=[jax.ShapeDtypeStruct((n, dm), F32), jax.ShapeDtypeStruct(((ntiles + 1) * nslots, width), BF16),
                   jax.ShapeDtypeStruct((n, LANES), F32), jax.ShapeDtypeStruct((ntiles, 8, LANES), F32)],
        scratch_shapes=[pltpu.VMEM((2, tm, LANES), F32)] * 4 + [pltpu.VMEM((tm, width), BF16)],
        compiler_params=_cparams(("arbitrary",)),
        name="outproj",
    )(*args)


def _wait_copies(count, copy):
    def wait_one(c, carry):
        copy.wait()
        return carry

    lax.fori_loop(0, count, wait_one, 0)


def _experts_kernel(first_ref, count_ref, chunk_ref, wg_ref, wu_ref, wd_ref, xs_hbm, ys_hbm,
                    xbuf, ybuf, zbuf, xsem, ysem, zsem, *, nblk):
    e = pl.program_id(0)
    last = pl.num_programs(0) - 1
    first = first_ref[e]
    count = count_ref[e]
    used = first_ref[last] + count_ref[last]
    dm = ybuf.shape[2]
    cpb = ROW_BLOCK // SEG_ROWS

    def chunk_rows(blk, j):
        return pl.ds(pl.multiple_of(chunk_ref[blk * cpb + j], SEG_ROWS), SEG_ROWS)

    def gather(blk):
        for j in range(cpb):
            pltpu.make_async_copy(xs_hbm.at[chunk_rows(blk, j), :], xbuf.at[blk & 1, pl.ds(j * SEG_ROWS, SEG_ROWS), :],
                                  xsem.at[blk & 1]).start(priority=1)

    def gather_done(blk):
        return pltpu.make_async_copy(xs_hbm.at[pl.ds(0, ROW_BLOCK), :], xbuf.at[blk & 1], xsem.at[blk & 1])

    def put(blk):
        rows = pl.ds(pl.multiple_of(blk * ROW_BLOCK, ROW_BLOCK), ROW_BLOCK)
        return pltpu.make_async_copy(ybuf.at[blk & 1], ys_hbm.at[rows, :], ysem.at[blk & 1])

    def zero_blk(b):
        rows = pl.ds(pl.multiple_of(b * ROW_BLOCK, ROW_BLOCK), ROW_BLOCK)
        return pltpu.make_async_copy(zbuf, ys_hbm.at[rows, :], zsem)

    @pl.when(e == 0)
    def _():
        zbuf[...] = jnp.zeros_like(zbuf)

        def start(b, carry):
            zero_blk(b).start()
            return carry

        lax.fori_loop(used, nblk, start, 0)

        @pl.when(used > 0)
        def _():
            gather(0)

    def body(b, carry):
        blk = first + b
        slot = blk & 1
        gather_done(blk).wait()

        @pl.when(blk + 1 < used)
        def _():
            gather(blk + 1)

        @pl.when(blk >= 2)
        def _():
            put(blk - 2).wait()

        xb = xbuf[slot, :, :dm]
        aux = xbuf[slot, :, dm:].astype(F32)
        w = jnp.where(aux[:, 4:5] == e.astype(F32), aux[:, 0:1] + aux[:, 1:2], aux[:, 2:3] + aux[:, 3:4])
        g = jnp.dot(xb, wg_ref[0], preferred_element_type=F32)
        u = jnp.dot(xb, wu_ref[0], preferred_element_type=F32)
        a = (g * jax.nn.sigmoid(g)) * u
        y = jnp.dot(a.astype(BF16), wd_ref[0], preferred_element_type=F32)
        ybuf[slot] = (y * w).astype(BF16)
        put(blk).start()
        return carry

    lax.fori_loop(0, count, body, 0)

    @pl.when(e == last)
    def _():
        @pl.when(used >= 2)
        def _():
            put(used - 2).wait()

        @pl.when(used >= 1)
        def _():
            put(used - 1).wait()

        _wait_copies(nblk - used, zero_blk(0))


def _experts(first_blk, count_blk, chunk_tbl, x_sorted, w_gate, w_up, w_down, nblk):
    rows, width = x_sorted.shape
    n_exp, dm, de = w_gate.shape
    wmap = lambda e, *_: (e, 0, 0)
    grid_spec = pltpu.PrefetchScalarGridSpec(
        num_scalar_prefetch=3,
        grid=(n_exp,),
        in_specs=[pl.BlockSpec((1, dm, de), wmap),
                  pl.BlockSpec((1, dm, de), wmap),
                  pl.BlockSpec((1, de, dm), wmap),
                  pl.BlockSpec(memory_space=pl.ANY)],
        out_specs=pl.BlockSpec(memory_space=pl.ANY),
        scratch_shapes=[pltpu.VMEM((2, ROW_BLOCK, width), BF16), pltpu.VMEM((2, ROW_BLOCK, dm), BF16),
                        pltpu.VMEM((ROW_BLOCK, dm), BF16),
                        pltpu.SemaphoreType.DMA((2,)), pltpu.SemaphoreType.DMA((2,)), pltpu.SemaphoreType.DMA(())],
    )
    return pl.pallas_call(
        functools.partial(_experts_kernel, nblk=nblk),
        grid_spec=grid_spec,
        out_shape=jax.ShapeDtypeStruct((nblk * ROW_BLOCK, dm), BF16),
        compiler_params=_cparams(("arbitrary",)),
        name="experts",
    )(first_blk, count_blk, chunk_tbl, w_gate, w_up, w_down, x_sorted)


def _slot_parts(tm, nslots, group):
    return [(0, 2 * tm)] + [(lo, group) for lo in range(2 * tm, nslots, group)]


def _combine_kernel(src_ref, live_ref, route_ref, x1_ref, mod_ref, g_ref, y_ref, o_ref, ys_ref, sem,
                    *, tm, nslots, group):
    i = pl.program_id(0)
    last = pl.num_programs(0) - 1
    slot = i & 1
    nchk = nslots // SEG_ROWS
    parts = _slot_parts(tm, nslots, group)

    def for_live_parts(tile, fn):
        for p, (lo, rows) in enumerate(parts):
            if p == 0:
                fn(lo, rows)
            else:
                pl.when(live_ref[tile] > lo)(functools.partial(fn, lo, rows))

    def gather(tile, s):
        def start(lo, rows):
            for c in range(lo // SEG_ROWS, (lo + rows) // SEG_ROWS):
                src = pl.multiple_of(src_ref[tile * nchk + c], SEG_ROWS)
                pltpu.make_async_copy(y_ref.at[pl.ds(src, SEG_ROWS), :],
                                      ys_ref.at[s, pl.ds(c * SEG_ROWS, SEG_ROWS), :], sem.at[s]).start()

        for_live_parts(tile, start)

    def wait(tile, s):
        for_live_parts(tile, lambda lo, rows: pltpu.make_async_copy(
            y_ref.at[pl.ds(0, rows), :], ys_ref.at[s, pl.ds(lo, rows), :], sem.at[s]).wait())

    @pl.when(i == 0)
    def _():
        gather(0, 0)

    nxt = jnp.minimum(i + 1, last)
    gather(nxt, 1 - slot)
    wait(i, slot)
    route = route_ref[...]

    def finish(rows):
        scol = lax.broadcasted_iota(I32, (tm, rows), 1).astype(F32)
        pick = jnp.where((scol == route[:, 2:3]) | (scol == route[:, 3:4]), 1.0, 0.0).astype(BF16)
        moe = jnp.dot(pick, ys_ref[slot, :rows, :], preferred_element_type=F32)
        x = x1_ref[...] + mod_ref[0, 5:6, :] * moe
        ms = jnp.mean(x * x, axis=-1, keepdims=True)
        o_ref[...] = (x * lax.rsqrt(ms + RMS_EPS)) * g_ref[...]

    ends = [lo + rows for lo, rows in parts]
    for p, end in enumerate(ends):
        above = live_ref[i] > (ends[p - 1] if p else -1)
        cond = above if p == len(ends) - 1 else above & (live_ref[i] <= end)
        pl.when(cond)(functools.partial(finish, end))

    @pl.when(i == last)
    def _():
        wait(nxt, 1 - slot)


def _combine(src_tbl, live_tbl, route, x1, mod3, g_final, y_rows, seq, tm, nslots):
    n, dm = x1.shape
    tpb = seq // tm
    grid_spec = pltpu.PrefetchScalarGridSpec(
        num_scalar_prefetch=2,
        grid=(n // tm,),
        in_specs=[pl.BlockSpec((tm, LANES), lambda i, *_: (i, 0)),
                  pl.BlockSpec((tm, dm), lambda i, *_: (i, 0)),
                  pl.BlockSpec((1, 6, dm), lambda i, *_: (i // tpb, 0, 0)),
                  pl.BlockSpec((1, dm), lambda i, *_: (0, 0)),
                  pl.BlockSpec(memory_space=pl.ANY)],
        out_specs=pl.BlockSpec((tm, dm), lambda i, *_: (i, 0)),
        scratch_shapes=[pltpu.VMEM((2, nslots, dm), BF16), pltpu.SemaphoreType.DMA((2,))],
    )
    return pl.pallas_call(
        functools.partial(_combine_kernel, tm=tm, nslots=nslots, group=SLOT_GROUP),
        grid_spec=grid_spec,
        out_shape=jax.ShapeDtypeStruct((n, dm), F32),
        compiler_params=_cparams(("arbitrary",)),
        name="combine",
    )(src_tbl, live_tbl, route, x1, mod3, g_final, y_rows)


def _rope_angles(positions):
    half = HEAD_DIM // 2
    inv_freq = ROPE_THETA ** (-jnp.arange(half, dtype=F32) * (2.0 / HEAD_DIM))
    freq = jnp.tile(inv_freq, LANES // half)
    sign = jnp.tile(jnp.concatenate([-jnp.ones((half,), F32), jnp.ones((half,), F32)]), LANES // HEAD_DIM)
    ang = positions.astype(F32).reshape(-1, 1) * freq
    return ang, sign.reshape(1, LANES)


def _expert_chunk_table(lens, pstart, tot, pend, nblk, nslots):
    ntiles = lens.shape[0]
    cpb = ROW_BLOCK // SEG_ROWS
    runs = lens // SEG_ROWS
    q = jnp.arange(nblk * cpb, dtype=I32)
    owner = jnp.minimum(jnp.sum((pend[None, :] // SEG_ROWS <= q[:, None]).astype(I32), axis=1), N_EXPERTS - 1)
    is_e = owner[:, None] == jnp.arange(N_EXPERTS, dtype=I32)[None, :]
    pick_e = lambda v: jnp.sum(jnp.where(is_e, v[None, :], 0), axis=1)
    off = q - pick_e(pstart // SEG_ROWS)
    in_run = (off < pick_e(tot // SEG_ROWS)) & (q < pend[-1] // SEG_ROWS)
    upto = jnp.cumsum(runs, axis=0)
    upto_e = jnp.sum(jnp.where(is_e[:, None, :], upto[None, :, :], 0), axis=2)
    tile = jnp.minimum(jnp.sum((upto_e <= off[:, None]).astype(I32), axis=1), ntiles - 1)
    is_t = tile[:, None] == jnp.arange(ntiles, dtype=I32)[None, :]
    run_slot0 = jnp.cumsum(runs, axis=1) - runs - (upto - runs)
    slot0 = jnp.sum(jnp.where(is_t[:, :, None] & is_e[:, None, :], run_slot0[None, :, :], 0), axis=(1, 2))
    live_row = tile * nslots + (slot0 + off) * SEG_ROWS
    spare_row = ntiles * nslots + (((q // cpb) % 2) * cpb + q % cpb) * SEG_ROWS
    return jnp.where(in_run, live_row, spare_row)


def kernel(x, c, positions, w_ada, b_ada, g_mix, w_in, sink_logits, w_branch_a, w_branch_b, w_out, g_ffn,
           w_group, b_group, w_route, b_route, w_expert_gate, w_expert_up, w_expert_down, g_final):
    batch, seq, dm = x.shape
    n = batch * seq
    assert w_ada.shape[0] == 1, "one layer"
    x2 = x.reshape(n, dm)

    c8 = jnp.pad(c, ((0, 8 - batch), (0, 0)))
    mod = _ada(c8, w_ada[0], b_ada[0].reshape(1, -1))
    mod3 = mod[:batch].reshape(batch, 6, dm)

    angles, sign = _rope_angles(positions)
    outs = _inproj(x2, mod3, g_mix[0].reshape(1, dm), angles, sign, w_in[0].astype(BF16),
                   (w_expert_gate[0], w_expert_up[0], w_expert_down[0]), batch, seq)
    qkv = outs[:9]
    qb, kb, vb, sga, sgb = outs[9:14]
    expert_w_bf = outs[14:]

    dil_outs = [_dil_attention(qkv[3 * g], qkv[3 * g + 1], qkv[3 * g + 2]) for g in range(len(DILATIONS))]
    ob = _swa_attention(sink_logits[0], qb.reshape(batch, seq, -1), kb.reshape(batch, seq, -1),
                        vb.reshape(batch, seq, -1)).reshape(n, -1)

    pad = LANES - N_GROUPS - N_EXPERTS
    wr = jnp.concatenate([w_group[0], w_route[0], jnp.zeros((dm, pad), F32)], axis=1).astype(BF16)
    br = jnp.concatenate([b_group[0], b_route[0], jnp.zeros((pad,), F32)]).reshape(1, LANES)
    ntiles = n // MOE_TILE
    nslots = 2 * MOE_TILE + N_EXPERTS * SEG_ROWS
    x1, x_sorted, route, seg_lens = _outproj(dil_outs, ob, sga, sgb, x2, mod3, g_ffn[0].reshape(1, dm),
                                             w_branch_a[0].astype(BF16), w_branch_b[0].astype(BF16),
                                             w_out[0].astype(BF16), wr, br, batch, seq, nslots)

    nblk = -(-(2 * n + ntiles * N_EXPERTS * (SEG_ROWS - 1)) // ROW_BLOCK) + N_EXPERTS
    lens = seg_lens[:, 0, :N_EXPERTS].astype(I32)
    tot = jnp.sum(lens, axis=0)
    padded = (tot + ROW_BLOCK - 1) // ROW_BLOCK * ROW_BLOCK
    pend = jnp.cumsum(padded)
    pstart = pend - padded
    chunk_tbl = _expert_chunk_table(lens, pstart, tot, pend, nblk, nslots)
    live_tbl = jnp.sum(lens, axis=1)

    nchk = nslots // SEG_ROWS
    base = pstart[None, :] + jnp.cumsum(lens, axis=0) - lens
    run_end = jnp.cumsum(lens // SEG_ROWS, axis=1)
    chunk = jnp.arange(nchk, dtype=I32)
    owner = jnp.sum((run_end[:, None, :] <= chunk[None, :, None]).astype(I32), axis=2)
    is_owner = owner[:, :, None] == jnp.arange(N_EXPERTS, dtype=I32)[None, None, :]
    run_row0 = base - (run_end - lens // SEG_ROWS) * SEG_ROWS
    row = jnp.sum(jnp.where(is_owner, run_row0[:, None, :], 0), axis=2) + chunk[None, :] * SEG_ROWS
    src_tbl = jnp.where(owner < N_EXPERTS, row, chunk[None, :] * SEG_ROWS).reshape(-1)

    y_rows = _experts(pstart // ROW_BLOCK, padded // ROW_BLOCK, chunk_tbl, x_sorted, *expert_w_bf, nblk)
    out = _combine(src_tbl, live_tbl, route, x1, mod3, g_final.reshape(1, dm), y_rows, seq, MOE_TILE, nslots)
    return out.reshape(batch, seq, dm)
```

```python
import functools

import jax
import jax.numpy as jnp
from jax import lax
from jax.experimental import pallas as pl
from jax.experimental.pallas import tpu as pltpu

F32 = jnp.float32
BF16 = jnp.bfloat16
I32 = jnp.int32

HEAD_DIM = 64
ROPE_THETA = 10000.0
RMS_EPS = 1e-6
NEG_INF = -1e30
Q_SCALE = HEAD_DIM ** -0.5
DILATIONS = (1, 4, 16)
DIL_HALF_WINDOW = 64
DIL_GROUP_W = 256
SWA_WINDOW = 128
N_GROUPS = 4
EXPERTS_PER_GROUP = 8
N_EXPERTS = 32
LANES = 128
ROW_BLOCK = 512
SEG_ROWS = 16
MOE_TILE = 512
SLOT_GROUP = 256
VMEM_LIMIT = 56 * 1024 * 1024


def _cparams(sem):
    return pltpu.CompilerParams(dimension_semantics=sem, vmem_limit_bytes=VMEM_LIMIT)


def _ada_kernel(c_ref, w_ref, b_ref, o_ref):
    c = c_ref[...]
    cs = c * jax.nn.sigmoid(c)
    o_ref[...] = jnp.dot(cs.astype(BF16), w_ref[...].astype(BF16), preferred_element_type=F32) + b_ref[...]


def _ada(c8, w_ada, b_ada):
    d, n = w_ada.shape
    tn = 1536
    return pl.pallas_call(
        _ada_kernel,
        grid=(n // tn,),
        in_specs=[pl.BlockSpec((8, d), lambda j: (0, 0)),
                  pl.BlockSpec((d, tn), lambda j: (0, j)),
                  pl.BlockSpec((1, tn), lambda j: (0, j))],
        out_specs=pl.BlockSpec((8, tn), lambda j: (0, j)),
        out_shape=jax.ShapeDtypeStruct((8, n), F32),
        compiler_params=_cparams(("arbitrary",)),
        name="ada",
    )(c8, w_ada, b_ada)


def _rms_mod(x, g, shift, scale):
    ms = jnp.mean(x * x, axis=-1, keepdims=True)
    return (x * lax.rsqrt(ms + RMS_EPS)) * (g * (1.0 + scale)) + shift


def _inproj_kernel(x_ref, mod_ref, g_ref, ang_ref, sign_ref, w_ref, eg_ref, eu_ref, ed_ref,
                   q0_ref, k0_ref, v0_ref, q1_ref, k1_ref, v1_ref, q2_ref, k2_ref, v2_ref,
                   qb_ref, kb_ref, vb_ref, sga_ref, sgb_ref, egb_ref, eub_ref, edb_ref, stg_ref, *, tm, n_exp):
    def cast_expert():
        egb_ref[...] = eg_ref[...].astype(BF16)
        eub_ref[...] = eu_ref[...].astype(BF16)
        edb_ref[...] = ed_ref[...].astype(BF16)

    if n_exp is None:
        cast_expert()
    else:
        pl.when(pl.program_id(0) < n_exp)(cast_expert)

    h = _rms_mod(x_ref[...], g_ref[...], mod_ref[0, 0:1, :], mod_ref[0, 1:2, :])
    hb = h.astype(BF16)
    ang = ang_ref[...]
    cos = jnp.cos(ang)
    sin = jnp.sin(ang) * sign_ref[...]
    lane = lax.broadcasted_iota(I32, (tm, LANES), 1)
    first_half = (lane & 32) == 0
    low = lane < 64

    def proj(c0, width):
        return jnp.dot(hb, w_ref[:, c0:c0 + width], preferred_element_type=F32)

    def rope(t):
        rot = jnp.where(first_half, pltpu.roll(t, 96, 1), pltpu.roll(t, 32, 1))
        return t * cos + rot * sin

    def rope256(p):
        return jnp.concatenate([rope(p[:, :LANES]), rope(p[:, LANES:])], axis=1)

    def store_group(ref, val, d):
        if d == 1:
            ref[0, 0] = val.astype(BF16)
        else:
            for c in range(2):
                stg_ref[c] = val[:, c * LANES:(c + 1) * LANES]
            for r in range(d):
                for c in range(2):
                    ref[0, r, :, c * LANES:(c + 1) * LANES] = (
                        stg_ref[c, pl.ds(r, tm // d, stride=d), :].astype(BF16))

    q_refs = (q0_ref, q1_ref, q2_ref)
    k_refs = (k0_ref, k1_ref, k2_ref)
    v_refs = (v0_ref, v1_ref, v2_ref)
    for g, d in enumerate(DILATIONS):
        store_group(q_refs[g], rope256(proj(g * 256, 256)) * Q_SCALE, d)
        store_group(k_refs[g], rope256(proj(768 + g * 256, 256)), d)
        store_group(v_refs[g], proj(1536 + g * 256, 256), d)
    for j in range(2):
        qb_ref[:, j * 256:(j + 1) * 256] = (rope256(proj(2304 + j * 256, 256)) * Q_SCALE).astype(BF16)
    kv = proj(2816, 256)
    kb = rope(kv[:, :LANES])
    vb = kv[:, LANES:]
    kb_sw = pltpu.roll(kb, 64, 1)
    vb_sw = pltpu.roll(vb, 64, 1)
    kb_ref[:, :LANES] = jnp.where(low, kb, kb_sw).astype(BF16)
    kb_ref[:, LANES:] = jnp.where(low, kb_sw, kb).astype(BF16)
    vb_ref[:, :LANES] = jnp.where(low, vb, vb_sw).astype(BF16)
    vb_ref[:, LANES:] = jnp.where(low, vb_sw, vb).astype(BF16)
    for j in range(4):
        sig = lambda t: 0.5 * jnp.tanh(0.5 * t) + 0.5
        sga_ref[:, j * 256:(j + 1) * 256] = sig(proj(3072 + j * 256, 256)).astype(BF16)
        sgb_ref[:, j * 256:(j + 1) * 256] = sig(proj(4096 + j * 256, 256)).astype(BF16)


def _inproj(x2, mod3, g_mix, angles, sign, w_in_bf, expert_w, batch, seq):
    n, dm = x2.shape
    tm = 512
    tpb = seq // tm
    grid = (n // tm,)
    row = lambda i: (i, 0)
    strided_specs, strided_shapes = [], []
    for d in DILATIONS:
        for _ in range(3):
            strided_specs.append(pl.BlockSpec((1, d, tm // d, DIL_GROUP_W), lambda i: (i // tpb, 0, i % tpb, 0)))
            strided_shapes.append(jax.ShapeDtypeStruct((batch, d, seq // d, DIL_GROUP_W), BF16))
    out_specs = strided_specs + [
        pl.BlockSpec((tm, 512), row), pl.BlockSpec((tm, 256), row), pl.BlockSpec((tm, 256), row),
        pl.BlockSpec((tm, dm), row), pl.BlockSpec((tm, dm), row)]
    out_shapes = strided_shapes + [
        jax.ShapeDtypeStruct((n, 512), BF16), jax.ShapeDtypeStruct((n, 256), BF16),
        jax.ShapeDtypeStruct((n, 256), BF16), jax.ShapeDtypeStruct((n, dm), BF16),
        jax.ShapeDtypeStruct((n, dm), BF16)]
    n_exp = expert_w[0].shape[0]
    assert n // tm >= n_exp, "one expert's weights are cast per grid step"
    emap = lambda i: (jnp.minimum(i, n_exp - 1), 0, 0)
    expert_specs = [pl.BlockSpec((1,) + w.shape[1:], emap) for w in expert_w]
    return pl.pallas_call(
        functools.partial(_inproj_kernel, tm=tm, n_exp=None if n // tm == n_exp else n_exp),
        grid=grid,
        in_specs=[pl.BlockSpec((tm, dm), row),
                  pl.BlockSpec((1, 6, dm), lambda i: (i // tpb, 0, 0)),
                  pl.BlockSpec((1, dm), lambda i: (0, 0)),
                  pl.BlockSpec((tm, LANES), row),
                  pl.BlockSpec((1, LANES), lambda i: (0, 0)),
                  pl.BlockSpec(w_in_bf.shape, lambda i: (0, 0), pipeline_mode=pl.Buffered(1))] + expert_specs,
        out_specs=out_specs + expert_specs,
        out_shape=out_shapes + [jax.ShapeDtypeStruct(w.shape, BF16) for w in expert_w],
        scratch_shapes=[pltpu.VMEM((2, tm, LANES), F32)],
        compiler_params=_cparams(("arbitrary",)),
        name="inproj",
    )(x2, mod3, g_mix, angles, sign, w_in_bf, *expert_w)


def _split_heads(q2, low):
    zero = jnp.zeros_like(q2)
    return jnp.concatenate([jnp.where(low, q2, zero), jnp.where(low, zero, q2)], axis=0)


def _band_softmax(qst, k2, v2, bias, sinks):
    s = lax.dot_general(qst, k2, (((1,), (1,)), ((), ())), preferred_element_type=F32)
    s = s + bias
    rows, tk = s.shape
    m = jnp.max(s, axis=-1, keepdims=True)
    if sinks is not None:
        seg = rows // len(sinks)
        m = jnp.concatenate([jnp.maximum(m[h * seg:(h + 1) * seg], sk) for h, sk in enumerate(sinks)], axis=0)
    m = jnp.broadcast_to(m, (rows, LANES))
    e = jnp.concatenate([jnp.exp(s[:, c * LANES:(c + 1) * LANES] - m) for c in range(tk // LANES)], axis=1)
    v_ones = jnp.concatenate([v2, jnp.ones((tk, LANES), BF16)], axis=1)
    od = jnp.dot(e.astype(BF16), v_ones, preferred_element_type=F32)
    o, den = od[:, :LANES], od[:, LANES:]
    if sinks is not None:
        den = jnp.concatenate([den[h * seg:(h + 1) * seg] + jnp.exp(sk - m[h * seg:(h + 1) * seg])
                               for h, sk in enumerate(sinks)], axis=0)
    return o / den, m, den


def _fill_band_bias(bias_ref, tq, window):
    rows, tk = bias_ref.shape[1:]
    row = lax.broadcasted_iota(I32, (rows, tk), 0) & (tq - 1)
    col = lax.broadcasted_iota(I32, (rows, tk), 1)
    for i in range(bias_ref.shape[0]):
        bias_ref[i] = jnp.where(jnp.abs(col - row - i * window) <= window, 0.0, NEG_INF)


def _dil_kernel(q_ref, k_ref, v_ref, o_ref, l_ref, bias_ref, *, length, tq, tk):
    low = lax.broadcasted_iota(I32, (tq, LANES), 1) < 64
    _fill_band_bias(bias_ref, tq, DIL_HALF_WINDOW)
    nq = length // tq

    def body(j, carry):
        r = j // nq
        qs = pl.multiple_of((j % nq) * tq, tq)
        ks = pl.multiple_of(jnp.clip(qs - DIL_HALF_WINDOW, 0, length - tk), DIL_HALF_WINDOW)
        which = (qs - ks) // DIL_HALF_WINDOW
        for c in range(DIL_GROUP_W // LANES):
            cs = slice(c * LANES, (c + 1) * LANES)
            qst = _split_heads(q_ref[r, pl.ds(qs, tq), cs], low)
            o, m, den = _band_softmax(qst, k_ref[r, pl.ds(ks, tk), cs], v_ref[r, pl.ds(ks, tk), cs],
                                      bias_ref[which], None)
            lse = m + jnp.log(den)
            o_ref[r, pl.ds(qs, tq), cs] = jnp.where(low, o[:tq], o[tq:]).astype(BF16)
            l_ref[r, pl.ds(qs, tq), cs] = jnp.where(low, lse[:tq], lse[tq:])
        return carry

    lax.fori_loop(0, q_ref.shape[0] * nq, body, 0, unroll=16)


def _dil_attention(q, k, v):
    batch, d, length, w = q.shape
    tq, tk = 128, 256
    spec = pl.BlockSpec((None, d, length, w), lambda b: (b, 0, 0, 0))
    return pl.pallas_call(
        functools.partial(_dil_kernel, length=length, tq=tq, tk=tk),
        grid=(batch,),
        in_specs=[spec, spec, spec],
        out_specs=[spec, spec],
        out_shape=[jax.ShapeDtypeStruct(q.shape, BF16), jax.ShapeDtypeStruct(q.shape, F32)],
        scratch_shapes=[pltpu.VMEM((3, 2 * tq, tk), F32)],
        compiler_params=_cparams(("arbitrary",)),
        name=f"dil{d}",
    )(q, k, v)


def _swa_kernel(sink_ref, q_ref, k_ref, v_ref, o_ref, bias_ref, *, length, tq, tk):
    low = lax.broadcasted_iota(I32, (tq, LANES), 1) < 64
    nblk = q_ref.shape[1] // LANES
    _fill_band_bias(bias_ref, tq, SWA_WINDOW)

    def body(j, carry):
        qs = pl.multiple_of(j * tq, tq)
        ks = pl.multiple_of(jnp.clip(qs - SWA_WINDOW, 0, length - tk), SWA_WINDOW)
        bias = bias_ref[(qs - ks) // SWA_WINDOW]
        for b in range(nblk):
            cs = slice((b // 2) * LANES, (b // 2 + 1) * LANES)
            bs = slice(b * LANES, (b + 1) * LANES)
            qst = _split_heads(q_ref[pl.ds(qs, tq), bs], low)
            sinks = (sink_ref[2 * b], sink_ref[2 * b + 1])
            o, _, _ = _band_softmax(qst, k_ref[pl.ds(ks, tk), cs], v_ref[pl.ds(ks, tk), cs], bias, sinks)
            o_ref[pl.ds(qs, tq), bs] = jnp.where(low, o[:tq], o[tq:]).astype(BF16)
        return carry

    lax.fori_loop(0, length // tq, body, 0, unroll=8)


def _swa_attention(sink, q, k, v):
    batch, length, qw = q.shape
    tq, tk = 128, 384
    return pl.pallas_call(
        functools.partial(_swa_kernel, length=length, tq=tq, tk=tk),
        grid=(batch,),
        in_specs=[pl.BlockSpec(memory_space=pltpu.SMEM),
                  pl.BlockSpec((None, length, qw), lambda b: (b, 0, 0)),
                  pl.BlockSpec((None, length, k.shape[2]), lambda b: (b, 0, 0)),
                  pl.BlockSpec((None, length, v.shape[2]), lambda b: (b, 0, 0))],
        out_specs=pl.BlockSpec((None, length, qw), lambda b: (b, 0, 0)),
        out_shape=jax.ShapeDtypeStruct(q.shape, BF16),
        scratch_shapes=[pltpu.VMEM((3, 2 * tq, tk), F32)],
        compiler_params=_cparams(("arbitrary",)),
        name="swa",
    )(sink, q, k, v)


def _route_rows(logits):
    lane = lax.broadcasted_iota(I32, logits.shape, 1).astype(F32)
    big = 1e9
    is_g = lane < N_GROUPS
    gl = jnp.where(is_g, logits, NEG_INF)
    gmax = jnp.max(gl, axis=-1, keepdims=True)
    gsel = jnp.min(jnp.where(is_g & (gl == gmax), lane, big), axis=-1, keepdims=True)
    gw = 1.0 / jnp.sum(jnp.where(is_g, jnp.exp(gl - gmax), 0.0), axis=-1, keepdims=True)
    e_lo = N_GROUPS + gsel * EXPERTS_PER_GROUP
    in_grp = (lane >= e_lo) & (lane < e_lo + EXPERTS_PER_GROUP)
    el = jnp.where(in_grp, logits, NEG_INF)
    m1 = jnp.max(el, axis=-1, keepdims=True)
    i1 = jnp.min(jnp.where(in_grp & (el == m1), lane, big), axis=-1, keepdims=True)
    el2 = jnp.where(lane == i1, NEG_INF, el)
    m2 = jnp.max(el2, axis=-1, keepdims=True)
    i2 = jnp.min(jnp.where(in_grp & (lane != i1) & (el2 == m2), lane, big), axis=-1, keepdims=True)
    t = jnp.exp(m2 - m1)
    tw1 = gw / (1.0 + t)
    tw2 = gw * t / (1.0 + t)
    out = jnp.where(lane == 0, tw1, 0.0)
    out = jnp.where(lane == 1, tw2, out)
    out = jnp.where(lane == 2, i1 - N_GROUPS, out)
    return jnp.where(lane == 3, i2 - N_GROUPS, out)


def _outproj_kernel(o0_ref, l0_ref, o1_ref, l1_ref, o2_ref, l2_ref, ob_ref, sga_ref, sgb_ref, x_ref,
                    mod_ref, g_ref, wa_ref, wb_ref, wo_ref, wr_ref, br_ref,
                    x1_ref, xs_ref, route_ref, len_ref,
                    so1_ref, sl1_ref, so2_ref, sl2_ref, h2_ref, *, tm, sub, group):
    ntiles = pl.num_programs(0) - 1

    @pl.when(pl.program_id(0) == ntiles)
    def _():
        xs_ref[...] = jnp.zeros_like(xs_ref)

    pl.when(pl.program_id(0) < ntiles)(functools.partial(
        _outproj_tile, o0_ref, l0_ref, o1_ref, l1_ref, o2_ref, l2_ref, ob_ref, sga_ref, sgb_ref, x_ref,
        mod_ref, g_ref, wa_ref, wb_ref, wo_ref, wr_ref, br_ref, x1_ref, xs_ref, route_ref, len_ref,
        so1_ref, sl1_ref, so2_ref, sl2_ref, h2_ref, tm=tm, sub=sub, group=group))


def _outproj_tile(o0_ref, l0_ref, o1_ref, l1_ref, o2_ref, l2_ref, ob_ref, sga_ref, sgb_ref, x_ref,
                  mod_ref, g_ref, wa_ref, wb_ref, wo_ref, wr_ref, br_ref,
                  x1_ref, xs_ref, route_ref, len_ref,
                  so1_ref, sl1_ref, so2_ref, sl2_ref, h2_ref, *, tm, sub, group):
    dm = x_ref.shape[1]
    for (o_ref, l_ref, so_ref, sl_ref, d) in ((o1_ref, l1_ref, so1_ref, sl1_ref, DILATIONS[1]),
                                              (o2_ref, l2_ref, so2_ref, sl2_ref, DILATIONS[2])):
        for r in range(d):
            for c in range(2):
                cs = slice(c * LANES, (c + 1) * LANES)
                so_ref[c, pl.ds(r, tm // d, stride=d), :] = o_ref[0, r, :, cs].astype(F32)
                sl_ref[c, pl.ds(r, tm // d, stride=d), :] = l_ref[0, r, :, cs]
    mr = lax.broadcasted_iota(I32, (LANES, LANES), 0)
    mc = lax.broadcasted_iota(I32, (LANES, LANES), 1)
    move_hi = jnp.where(((mr < 2) & (mc == 2 * mr)) | ((mr >= 2) & (mr < 4) & (mc == mr + 2)), 1.0, 0.0).astype(BF16)
    move_lo = jnp.where((mr < 2) & (mc == 2 * mr + 1), 1.0, 0.0).astype(BF16)
    for t in range(tm // sub):
        rs = slice(t * sub, (t + 1) * sub)
        both = lambda ref: jnp.concatenate([ref[0, rs, :], ref[1, rs, :]], axis=1)
        o0, l0 = o0_ref[0, 0, rs, :].astype(F32), l0_ref[0, 0, rs, :]
        o1, l1, o2, l2 = both(so1_ref), both(sl1_ref), both(so2_ref), both(sl2_ref)
        mx = jnp.maximum(jnp.maximum(l0, l1), l2)
        w0, w1, w2 = jnp.exp(l0 - mx), jnp.exp(l1 - mx), jnp.exp(l2 - mx)
        o_a = (w0 * o0 + w1 * o1 + w2 * o2) / (w0 + w1 + w2)
        y_a = jnp.dot(o_a.astype(BF16), wa_ref[...], preferred_element_type=F32)
        y_b = jnp.dot(ob_ref[rs, :], wb_ref[...], preferred_element_type=F32)
        merged = sga_ref[rs, :].astype(F32) * y_a + sgb_ref[rs, :].astype(F32) * y_b
        mix = jnp.dot(merged.astype(BF16), wo_ref[...], preferred_element_type=F32)
        x1 = x_ref[rs, :] + mod_ref[0, 2:3, :] * mix
        x1_ref[rs, :] = x1
        h2 = _rms_mod(x1, g_ref[...], mod_ref[0, 3:4, :], mod_ref[0, 4:5, :]).astype(BF16)
        h2_ref[rs, :dm] = h2
        logits = jnp.dot(h2, wr_ref[...], preferred_element_type=F32) + br_ref[...]
        rt = _route_rows(logits)
        route_ref[rs, :] = rt
        hi = rt.astype(BF16)
        lo = (rt - hi.astype(F32)).astype(BF16)
        aux = (jnp.dot(hi, move_hi, preferred_element_type=F32) + jnp.dot(lo, move_lo, preferred_element_type=F32))
        h2_ref[rs, dm:] = aux.astype(BF16)

    part = route_ref[...]
    e1, e2 = part[:, 2:3], part[:, 3:4]
    lane = lax.broadcasted_iota(I32, (tm, LANES), 1).astype(F32)
    onehot = jnp.where((lane == e1) | (lane == e2), 1.0, 0.0)
    rr = lax.broadcasted_iota(I32, (tm, tm), 0)
    cc = lax.broadcasted_iota(I32, (tm, tm), 1)
    tri = jnp.where(rr > cc, 1.0, 0.0).astype(BF16)
    prefix = jnp.dot(tri, onehot.astype(BF16), preferred_element_type=F32)
    cnt = jnp.sum(onehot, axis=0, keepdims=True)
    seg_len = jnp.ceil(cnt * (1.0 / SEG_ROWS)) * SEG_ROWS
    ur = lax.broadcasted_iota(I32, (LANES, LANES), 0)
    uc = lax.broadcasted_iota(I32, (LANES, LANES), 1)
    upper = jnp.where(ur < uc, 1.0, 0.0).astype(BF16)
    seg_off = jnp.dot(jnp.broadcast_to(seg_len, (8, LANES)).astype(BF16), upper, preferred_element_type=F32)[0:1, :]
    slot_map = seg_off + prefix
    s1 = jnp.sum(jnp.where(lane == e1, slot_map, 0.0), axis=-1, keepdims=True)
    s2 = jnp.sum(jnp.where(lane == e2, slot_map, 0.0), axis=-1, keepdims=True)
    len_ref[0] = jnp.broadcast_to(seg_len, (8, LANES))
    route = jnp.where(lane == 2, s1, jnp.where(lane == 3, s2, part))
    route_ref[...] = route

    pr = lax.broadcasted_iota(I32, (8, LANES), 0)
    pc = lax.broadcasted_iota(I32, (8, LANES), 1)
    lane_pick = jnp.where(pc == pr + 2, 1.0, 0.0)
    slots_t = lax.dot_general(lane_pick, route, (((1,), (1,)), ((), ())), preferred_element_type=F32,
                              precision=lax.Precision.HIGHEST)
    live = jnp.sum(seg_len).astype(I32)

    def sort_rows(lo, rows):
        srow = (lax.broadcasted_iota(I32, (rows, tm), 0) + lo).astype(F32)
        pick = jnp.where((srow == slots_t[0:1, :]) | (srow == slots_t[1:2, :]), 1.0, 0.0).astype(BF16)
        xs_ref[lo:lo + rows, :] = jnp.dot(pick, h2_ref[...], preferred_element_type=F32).astype(BF16)

    def clear_rows(lo, rows):
        xs_ref[lo:lo + rows, :] = jnp.zeros((rows, xs_ref.shape[1]), BF16)

    head = xs_ref.shape[0] - group
    sort_rows(0, head)
    pl.when(live > head)(functools.partial(sort_rows, head, group))
    pl.when(live <= head)(functools.partial(clear_rows, head, group))


def _outproj(dil_outs, ob, sga, sgb, x2, mod3, g_ffn, wa, wb, wo, wr, br, batch, seq, nslots):
    n, dm = x2.shape
    tm = MOE_TILE
    tpb = seq // tm
    ntiles = n // tm
    tile = lambda i: jnp.minimum(i, ntiles - 1)
    row = lambda i: (tile(i), 0)
    const = lambda i: (0, 0)
    in_specs = []
    args = []
    for (o, l), d in zip(dil_outs, DILATIONS):
        spec = pl.BlockSpec((1, d, tm // d, DIL_GROUP_W), lambda i: (tile(i) // tpb, 0, tile(i) % tpb, 0))
        in_specs += [spec, spec]
        args += [o, l]
    in_specs += [pl.BlockSpec((tm, ob.shape[1]), row), pl.BlockSpec((tm, dm), row), pl.BlockSpec((tm, dm), row),
                 pl.BlockSpec((tm, dm), row),
                 pl.BlockSpec((1, 6, dm), lambda i: (tile(i) // tpb, 0, 0)),
                 pl.BlockSpec((1, dm), const),
                 pl.BlockSpec(wa.shape, const), pl.BlockSpec(wb.shape, const), pl.BlockSpec(wo.shape, const),
                 pl.BlockSpec(wr.shape, const), pl.BlockSpec(br.shape, const)]
    args += [ob, sga, sgb, x2, mod3, g_ffn, wa, wb, wo, wr, br]
    width = dm + LANES
    return pl.pallas_call(
        functools.partial(_outproj_kernel, tm=tm, sub=512, group=SLOT_GROUP),
        grid=(ntiles + 1,),
        in_specs=in_specs,
        out_specs=[pl.BlockSpec((tm, dm), row), pl.BlockSpec((nslots, width), lambda i: (i, 0)),
                   pl.BlockSpec((tm, LANES), row), pl.BlockSpec((1, 8, LANES), lambda i: (tile(i), 0, 0))],
        out_shape=[jax.ShapeDtypeStruct((n, dm), F32), jax.ShapeDtypeStruct(((ntiles + 1) * nslots, width), BF16),
                   jax.ShapeDtypeStruct((n, LANES), F32), jax.ShapeDtypeStruct((ntiles, 8, LANES), F32)],
        scratch_shapes=[pltpu.VMEM((2, tm, LANES), F32)] * 4 + [pltpu.VMEM((tm, width), BF16)],
        compiler_params=_cparams(("arbitrary",)),
        name="outproj",
    )(*args)


def _wait_copies(count, copy):
    def wait_one(c, carry):
        copy.wait()
        return carry

    lax.fori_loop(0, count, wait_one, 0)


def _experts_kernel(first_ref, count_ref, chunk_ref, wg_ref, wu_ref, wd_ref, xs_hbm, ys_hbm,
                    xbuf, ybuf, zbuf, xsem, ysem, zsem, *, nblk):
    e = pl.program_id(0)
    last = pl.num_programs(0) - 1
    first = first_ref[e]
    count = count_ref[e]
    used = first_ref[last] + count_ref[last]
    dm = ybuf.shape[2]
    cpb = ROW_BLOCK // SEG_ROWS

    def chunk_rows(blk, j):
        return pl.ds(pl.multiple_of(chunk_ref[blk * cpb + j], SEG_ROWS), SEG_ROWS)

    def gather(blk):
        for j in range(cpb):
            pltpu.make_async_copy(xs_hbm.at[chunk_rows(blk, j), :], xbuf.at[blk & 1, pl.ds(j * SEG_ROWS, SEG_ROWS), :],
                                  xsem.at[blk & 1]).start(priority=1)

    def gather_done(blk):
        return pltpu.make_async_copy(xs_hbm.at[pl.ds(0, ROW_BLOCK), :], xbuf.at[blk & 1], xsem.at[blk & 1])

    def put(blk):
        rows = pl.ds(pl.multiple_of(blk * ROW_BLOCK, ROW_BLOCK), ROW_BLOCK)
        return pltpu.make_async_copy(ybuf.at[blk & 1], ys_hbm.at[rows, :], ysem.at[blk & 1])

    def zero_blk(b):
        rows = pl.ds(pl.multiple_of(b * ROW_BLOCK, ROW_BLOCK), ROW_BLOCK)
        return pltpu.make_async_copy(zbuf, ys_hbm.at[rows, :], zsem)

    @pl.when(e == 0)
    def _():
        zbuf[...] = jnp.zeros_like(zbuf)

        def start(b, carry):
            zero_blk(b).start()
            return carry

        lax.fori_loop(used, nblk, start, 0)

        @pl.when(used > 0)
        def _():
            gather(0)

    def body(b, carry):
        blk = first + b
        slot = blk & 1
        gather_done(blk).wait()

        @pl.when(blk + 1 < used)
        def _():
            gather(blk + 1)

        @pl.when(blk >= 2)
        def _():
            put(blk - 2).wait()

        xb = xbuf[slot, :, :dm]
        aux = xbuf[slot, :, dm:].astype(F32)
        w = jnp.where(aux[:, 4:5] == e.astype(F32), aux[:, 0:1] + aux[:, 1:2], aux[:, 2:3] + aux[:, 3:4])
        g = jnp.dot(xb, wg_ref[0], preferred_element_type=F32)
        u = jnp.dot(xb, wu_ref[0], preferred_element_type=F32)
        a = (g * jax.nn.sigmoid(g)) * u
        y = jnp.dot(a.astype(BF16), wd_ref[0], preferred_element_type=F32)
        ybuf[slot] = (y * w).astype(BF16)
        put(blk).start()
        return carry

    lax.fori_loop(0, count, body, 0)

    @pl.when(e == last)
    def _():
        @pl.when(used >= 2)
        def _():
            put(used - 2).wait()

        @pl.when(used >= 1)
        def _():
            put(used - 1).wait()

        _wait_copies(nblk - used, zero_blk(0))


def _experts(first_blk, count_blk, chunk_tbl, x_sorted, w_gate, w_up, w_down, nblk):
    rows, width = x_sorted.shape
    n_exp, dm, de = w_gate.shape
    wmap = lambda e, *_: (e, 0, 0)
    grid_spec = pltpu.PrefetchScalarGridSpec(
        num_scalar_prefetch=3,
        grid=(n_exp,),
        in_specs=[pl.BlockSpec((1, dm, de), wmap),
                  pl.BlockSpec((1, dm, de), wmap),
                  pl.BlockSpec((1, de, dm), wmap),
                  pl.BlockSpec(memory_space=pl.ANY)],
        out_specs=pl.BlockSpec(memory_space=pl.ANY),
        scratch_shapes=[pltpu.VMEM((2, ROW_BLOCK, width), BF16), pltpu.VMEM((2, ROW_BLOCK, dm), BF16),
                        pltpu.VMEM((ROW_BLOCK, dm), BF16),
                        pltpu.SemaphoreType.DMA((2,)), pltpu.SemaphoreType.DMA((2,)), pltpu.SemaphoreType.DMA(())],
    )
    return pl.pallas_call(
        functools.partial(_experts_kernel, nblk=nblk),
        grid_spec=grid_spec,
        out_shape=jax.ShapeDtypeStruct((nblk * ROW_BLOCK, dm), BF16),
        compiler_params=_cparams(("arbitrary",)),
        name="experts",
    )(first_blk, count_blk, chunk_tbl, w_gate, w_up, w_down, x_sorted)


def _slot_parts(tm, nslots, group):
    return [(0, 2 * tm)] + [(lo, group) for lo in range(2 * tm, nslots, group)]


def _combine_kernel(src_ref, live_ref, route_ref, x1_ref, mod_ref, g_ref, y_ref, o_ref, ys_ref, sem,
                    *, tm, nslots, group):
    i = pl.program_id(0)
    last = pl.num_programs(0) - 1
    slot = i & 1
    nchk = nslots // SEG_ROWS
    parts = _slot_parts(tm, nslots, group)

    def for_live_parts(tile, fn):
        for p, (lo, rows) in enumerate(parts):
            if p == 0:
                fn(lo, rows)
            else:
                pl.when(live_ref[tile] > lo)(functools.partial(fn, lo, rows))

    def gather(tile, s):
        def start(lo, rows):
            for c in range(lo // SEG_ROWS, (lo + rows) // SEG_ROWS):
                src = pl.multiple_of(src_ref[tile * nchk + c], SEG_ROWS)
                pltpu.make_async_copy(y_ref.at[pl.ds(src, SEG_ROWS), :],
                                      ys_ref.at[s, pl.ds(c * SEG_ROWS, SEG_ROWS), :], sem.at[s]).start()

        for_live_parts(tile, start)

    def wait(tile, s):
        for_live_parts(tile, lambda lo, rows: pltpu.make_async_copy(
            y_ref.at[pl.ds(0, rows), :], ys_ref.at[s, pl.ds(lo, rows), :], sem.at[s]).wait())

    @pl.when(i == 0)
    def _():
        gather(0, 0)

    nxt = jnp.minimum(i + 1, last)
    gather(nxt, 1 - slot)
    wait(i, slot)
    route = route_ref[...]

    def finish(rows):
        scol = lax.broadcasted_iota(I32, (tm, rows), 1).astype(F32)
        pick = jnp.where((scol == route[:, 2:3]) | (scol == route[:, 3:4]), 1.0, 0.0).astype(BF16)
        moe = jnp.dot(pick, ys_ref[slot, :rows, :], preferred_element_type=F32)
        x = x1_ref[...] + mod_ref[0, 5:6, :] * moe
        ms = jnp.mean(x * x, axis=-1, keepdims=True)
        o_ref[...] = (x * lax.rsqrt(ms + RMS_EPS)) * g_ref[...]

    ends = [lo + rows for lo, rows in parts]
    for p, end in enumerate(ends):
        above = live_ref[i] > (ends[p - 1] if p else -1)
        cond = above if p == len(ends) - 1 else above & (live_ref[i] <= end)
        pl.when(cond)(functools.partial(finish, end))

    @pl.when(i == last)
    def _():
        wait(nxt, 1 - slot)


def _combine(src_tbl, live_tbl, route, x1, mod3, g_final, y_rows, seq, tm, nslots):
    n, dm = x1.shape
    tpb = seq // tm
    grid_spec = pltpu.PrefetchScalarGridSpec(
        num_scalar_prefetch=2,
        grid=(n // tm,),
        in_specs=[pl.BlockSpec((tm, LANES), lambda i, *_: (i, 0)),
                  pl.BlockSpec((tm, dm), lambda i, *_: (i, 0)),
                  pl.BlockSpec((1, 6, dm), lambda i, *_: (i // tpb, 0, 0)),
                  pl.BlockSpec((1, dm), lambda i, *_: (0, 0)),
                  pl.BlockSpec(memory_space=pl.ANY)],
        out_specs=pl.BlockSpec((tm, dm), lambda i, *_: (i, 0)),
        scratch_shapes=[pltpu.VMEM((2, nslots, dm), BF16), pltpu.SemaphoreType.DMA((2,))],
    )
    return pl.pallas_call(
        functools.partial(_combine_kernel, tm=tm, nslots=nslots, group=SLOT_GROUP),
        grid_spec=grid_spec,
        out_shape=jax.ShapeDtypeStruct((n, dm), F32),
        compiler_params=_cparams(("arbitrary",)),
        name="combine",
    )(src_tbl, live_tbl, route, x1, mod3, g_final, y_rows)


def _rope_angles(positions):
    half = HEAD_DIM // 2
    inv_freq = ROPE_THETA ** (-jnp.arange(half, dtype=F32) * (2.0 / HEAD_DIM))
    freq = jnp.tile(inv_freq, LANES // half)
    sign = jnp.tile(jnp.concatenate([-jnp.ones((half,), F32), jnp.ones((half,), F32)]), LANES // HEAD_DIM)
    ang = positions.astype(F32).reshape(-1, 1) * freq
    return ang, sign.reshape(1, LANES)


def _expert_chunk_table(lens, pstart, tot, pend, nblk, nslots):
    ntiles = lens.shape[0]
    cpb = ROW_BLOCK // SEG_ROWS
    runs = lens // SEG_ROWS
    q = jnp.arange(nblk * cpb, dtype=I32)
    owner = jnp.minimum(jnp.sum((pend[None, :] // SEG_ROWS <= q[:, None]).astype(I32), axis=1), N_EXPERTS - 1)
    is_e = owner[:, None] == jnp.arange(N_EXPERTS, dtype=I32)[None, :]
    pick_e = lambda v: jnp.sum(jnp.where(is_e, v[None, :], 0), axis=1)
    off = q - pick_e(pstart // SEG_ROWS)
    in_run = (off < pick_e(tot // SEG_ROWS)) & (q < pend[-1] // SEG_ROWS)
    upto = jnp.cumsum(runs, axis=0)
    upto_e = jnp.sum(jnp.where(is_e[:, None, :], upto[None, :, :], 0), axis=2)
    tile = jnp.minimum(jnp.sum((upto_e <= off[:, None]).astype(I32), axis=1), ntiles - 1)
    is_t = tile[:, None] == jnp.arange(ntiles, dtype=I32)[None, :]
    run_slot0 = jnp.cumsum(runs, axis=1) - runs - (upto - runs)
    slot0 = jnp.sum(jnp.where(is_t[:, :, None] & is_e[:, None, :], run_slot0[None, :, :], 0), axis=(1, 2))
    live_row = tile * nslots + (slot0 + off) * SEG_ROWS
    spare_row = ntiles * nslots + (((q // cpb) % 2) * cpb + q % cpb) * SEG_ROWS
    return jnp.where(in_run, live_row, spare_row)


def kernel(x, c, positions, w_ada, b_ada, g_mix, w_in, sink_logits, w_branch_a, w_branch_b, w_out, g_ffn,
           w_group, b_group, w_route, b_route, w_expert_gate, w_expert_up, w_expert_down, g_final):
    batch, seq, dm = x.shape
    n = batch * seq
    assert w_ada.shape[0] == 1, "one layer"
    x2 = x.reshape(n, dm)

    c8 = jnp.pad(c, ((0, 8 - batch), (0, 0)))
    mod = _ada(c8, w_ada[0], b_ada[0].reshape(1, -1))
    mod3 = mod[:batch].reshape(batch, 6, dm)

    angles, sign = _rope_angles(positions)
    outs = _inproj(x2, mod3, g_mix[0].reshape(1, dm), angles, sign, w_in[0].astype(BF16),
                   (w_expert_gate[0], w_expert_up[0], w_expert_down[0]), batch, seq)
    qkv = outs[:9]
    qb, kb, vb, sga, sgb = outs[9:14]
    expert_w_bf = outs[14:]

    dil_outs = [_dil_attention(qkv[3 * g], qkv[3 * g + 1], qkv[3 * g + 2]) for g in range(len(DILATIONS))]
    ob = _swa_attention(sink_logits[0], qb.reshape(batch, seq, -1), kb.reshape(batch, seq, -1),
                        vb.reshape(batch, seq, -1)).reshape(n, -1)

    pad = LANES - N_GROUPS - N_EXPERTS
    wr = jnp.concatenate([w_group[0], w_route[0], jnp.zeros((dm, pad), F32)], axis=1).astype(BF16)
    br = jnp.concatenate([b_group[0], b_route[0], jnp.zeros((pad,), F32)]).reshape(1, LANES)
    ntiles = n // MOE_TILE
    nslots = 2 * MOE_TILE + N_EXPERTS * SEG_ROWS
    x1, x_sorted, route, seg_lens = _outproj(dil_outs, ob, sga, sgb, x2, mod3, g_ffn[0].reshape(1, dm),
                                             w_branch_a[0].astype(BF16), w_branch_b[0].astype(BF16),
                                             w_out[0].astype(BF16), wr, br, batch, seq, nslots)

    nblk = -(-(2 * n + ntiles * N_EXPERTS * (SEG_ROWS - 1)) // ROW_BLOCK) + N_EXPERTS
    lens = seg_lens[:, 0, :N_EXPERTS].astype(I32)
    tot = jnp.sum(lens, axis=0)
    padded = (tot + ROW_BLOCK - 1) // ROW_BLOCK * ROW_BLOCK
    pend = jnp.cumsum(padded)
    pstart = pend - padded
    chunk_tbl = _expert_chunk_table(lens, pstart, tot, pend, nblk, nslots)
    live_tbl = jnp.sum(lens, axis=1)

    nchk = nslots // SEG_ROWS
    base = pstart[None, :] + jnp.cumsum(lens, axis=0) - lens
    run_end = jnp.cumsum(lens // SEG_ROWS, axis=1)
    chunk = jnp.arange(nchk, dtype=I32)
    owner = jnp.sum((run_end[:, None, :] <= chunk[None, :, None]).astype(I32), axis=2)
    is_owner = owner[:, :, None] == jnp.arange(N_EXPERTS, dtype=I32)[None, None, :]
    run_row0 = base - (run_end - lens // SEG_ROWS) * SEG_ROWS
    row = jnp.sum(jnp.where(is_owner, run_row0[:, None, :], 0), axis=2) + chunk[None, :] * SEG_ROWS
    src_tbl = jnp.where(owner < N_EXPERTS, row, chunk[None, :] * SEG_ROWS).reshape(-1)

    y_rows = _experts(pstart // ROW_BLOCK, padded // ROW_BLOCK, chunk_tbl, x_sorted, *expert_w_bf, nblk)
    out = _combine(src_tbl, live_tbl, route, x1, mod3, g_final.reshape(1, dm), y_rows, seq, MOE_TILE, nslots)
    return out.reshape(batch, seq, dm)
```

```python
import functools

import jax
import jax.numpy as jnp
from jax import lax
from jax.experimental import pallas as pl
from jax.experimental.pallas import tpu as pltpu

F32 = jnp.float32
BF16 = jnp.bfloat16
I32 = jnp.int32

HEAD_DIM = 64
ROPE_THETA = 10000.0
RMS_EPS = 1e-6
NEG_INF = -1e30
Q_SCALE = HEAD_DIM ** -0.5
DILATIONS = (1, 4, 16)
DIL_HALF_WINDOW = 64
DIL_GROUP_W = 256
SWA_WINDOW = 128
N_GROUPS = 4
EXPERTS_PER_GROUP = 8
N_EXPERTS = 32
LANES = 128
ROW_BLOCK = 512
SEG_ROWS = 16
MOE_TILE = 512
SLOT_GROUP = 256
VMEM_LIMIT = 56 * 1024 * 1024


def _cparams(sem):
    return pltpu.CompilerParams(dimension_semantics=sem, vmem_limit_bytes=VMEM_LIMIT)


def _ada_kernel(c_ref, w_ref, b_ref, o_ref):
    c = c_ref[...]
    cs = c * jax.nn.sigmoid(c)
    o_ref[...] = jnp.dot(cs.astype(BF16), w_ref[...].astype(BF16), preferred_element_type=F32) + b_ref[...]


def _ada(c8, w_ada, b_ada):
    d, n = w_ada.shape
    tn = 1536
    return pl.pallas_call(
        _ada_kernel,
        grid=(n // tn,),
        in_specs=[pl.BlockSpec((8, d), lambda j: (0, 0)),
                  pl.BlockSpec((d, tn), lambda j: (0, j)),
                  pl.BlockSpec((1, tn), lambda j: (0, j))],
        out_specs=pl.BlockSpec((8, tn), lambda j: (0, j)),
        out_shape=jax.ShapeDtypeStruct((8, n), F32),
        compiler_params=_cparams(("arbitrary",)),
        name="ada",
    )(c8, w_ada, b_ada)


def _rms_mod(x, g, shift, scale):
    ms = jnp.mean(x * x, axis=-1, keepdims=True)
    return (x * lax.rsqrt(ms + RMS_EPS)) * (g * (1.0 + scale)) + shift


def _inproj_kernel(x_ref, mod_ref, g_ref, ang_ref, sign_ref, w_ref, eg_ref, eu_ref, ed_ref,
                   q0_ref, k0_ref, v0_ref, q1_ref, k1_ref, v1_ref, q2_ref, k2_ref, v2_ref,
                   qb_ref, kb_ref, vb_ref, sga_ref, sgb_ref, egb_ref, eub_ref, edb_ref, stg_ref, *, tm, n_exp):
    def cast_expert():
        egb_ref[...] = eg_ref[...].astype(BF16)
        eub_ref[...] = eu_ref[...].astype(BF16)
        edb_ref[...] = ed_ref[...].astype(BF16)

    if n_exp is None:
        cast_expert()
    else:
        pl.when(pl.program_id(0) < n_exp)(cast_expert)

    h = _rms_mod(x_ref[...], g_ref[...], mod_ref[0, 0:1, :], mod_ref[0, 1:2, :])
    hb = h.astype(BF16)
    ang = ang_ref[...]
    cos = jnp.cos(ang)
    sin = jnp.sin(ang) * sign_ref[...]
    lane = lax.broadcasted_iota(I32, (tm, LANES), 1)
    first_half = (lane & 32) == 0
    low = lane < 64

    def proj(c0, width):
        return jnp.dot(hb, w_ref[:, c0:c0 + width], preferred_element_type=F32)

    def rope(t):
        rot = jnp.where(first_half, pltpu.roll(t, 96, 1), pltpu.roll(t, 32, 1))
        return t * cos + rot * sin

    def rope256(p):
        return jnp.concatenate([rope(p[:, :LANES]), rope(p[:, LANES:])], axis=1)

    def store_group(ref, val, d):
        if d == 1:
            ref[0, 0] = val.astype(BF16)
        else:
            for c in range(2):
                stg_ref[c] = val[:, c * LANES:(c + 1) * LANES]
            for r in range(d):
                for c in range(2):
                    ref[0, r, :, c * LANES:(c + 1) * LANES] = (
                        stg_ref[c, pl.ds(r, tm // d, stride=d), :].astype(BF16))

    q_refs = (q0_ref, q1_ref, q2_ref)
    k_refs = (k0_ref, k1_ref, k2_ref)
    v_refs = (v0_ref, v1_ref, v2_ref)
    for g, d in enumerate(DILATIONS):
        store_group(q_refs[g], rope256(proj(g * 256, 256)) * Q_SCALE, d)
        store_group(k_refs[g], rope256(proj(768 + g * 256, 256)), d)
        store_group(v_refs[g], proj(1536 + g * 256, 256), d)
    for j in range(2):
        qb_ref[:, j * 256:(j + 1) * 256] = (rope256(proj(2304 + j * 256, 256)) * Q_SCALE).astype(BF16)
    kv = proj(2816, 256)
    kb = rope(kv[:, :LANES])
    vb = kv[:, LANES:]
    kb_sw = pltpu.roll(kb, 64, 1)
    vb_sw = pltpu.roll(vb, 64, 1)
    kb_ref[:, :LANES] = jnp.where(low, kb, kb_sw).astype(BF16)
    kb_ref[:, LANES:] = jnp.where(low, kb_sw, kb).astype(BF16)
    vb_ref[:, :LANES] = jnp.where(low, vb, vb_sw).astype(BF16)
    vb_ref[:, LANES:] = jnp.where(low, vb_sw, vb).astype(BF16)
    for j in range(4):
        sig = lambda t: 0.5 * jnp.tanh(0.5 * t) + 0.5
        sga_ref[:, j * 256:(j + 1) * 256] = sig(proj(3072 + j * 256, 256)).astype(BF16)
        sgb_ref[:, j * 256:(j + 1) * 256] = sig(proj(4096 + j * 256, 256)).astype(BF16)


def _inproj(x2, mod3, g_mix, angles, sign, w_in_bf, expert_w, batch, seq):
    n, dm = x2.shape
    tm = 512
    tpb = seq // tm
    grid = (n // tm,)
    row = lambda i: (i, 0)
    strided_specs, strided_shapes = [], []
    for d in DILATIONS:
        for _ in range(3):
            strided_specs.append(pl.BlockSpec((1, d, tm // d, DIL_GROUP_W), lambda i: (i // tpb, 0, i % tpb, 0)))
            strided_shapes.append(jax.ShapeDtypeStruct((batch, d, seq // d, DIL_GROUP_W), BF16))
    out_specs = strided_specs + [
        pl.BlockSpec((tm, 512), row), pl.BlockSpec((tm, 256), row), pl.BlockSpec((tm, 256), row),
        pl.BlockSpec((tm, dm), row), pl.BlockSpec((tm, dm), row)]
    out_shapes = strided_shapes + [
        jax.ShapeDtypeStruct((n, 512), BF16), jax.ShapeDtypeStruct((n, 256), BF16),
        jax.ShapeDtypeStruct((n, 256), BF16), jax.ShapeDtypeStruct((n, dm), BF16),
        jax.ShapeDtypeStruct((n, dm), BF16)]
    n_exp = expert_w[0].shape[0]
    assert n // tm >= n_exp, "one expert's weights are cast per grid step"
    emap = lambda i: (jnp.minimum(i, n_exp - 1), 0, 0)
    expert_specs = [pl.BlockSpec((1,) + w.shape[1:], emap) for w in expert_w]
    return pl.pallas_call(
        functools.partial(_inproj_kernel, tm=tm, n_exp=None if n // tm == n_exp else n_exp),
        grid=grid,
        in_specs=[pl.BlockSpec((tm, dm), row),
                  pl.BlockSpec((1, 6, dm), lambda i: (i // tpb, 0, 0)),
                  pl.BlockSpec((1, dm), lambda i: (0, 0)),
                  pl.BlockSpec((tm, LANES), row),
                  pl.BlockSpec((1, LANES), lambda i: (0, 0)),
                  pl.BlockSpec(w_in_bf.shape, lambda i: (0, 0), pipeline_mode=pl.Buffered(1))] + expert_specs,
        out_specs=out_specs + expert_specs,
        out_shape=out_shapes + [jax.ShapeDtypeStruct(w.shape, BF16) for w in expert_w],
        scratch_shapes=[pltpu.VMEM((2, tm, LANES), F32)],
        compiler_params=_cparams(("arbitrary",)),
        name="inproj",
    )(x2, mod3, g_mix, angles, sign, w_in_bf, *expert_w)


def _split_heads(q2, low):
    zero = jnp.zeros_like(q2)
    return jnp.concatenate([jnp.where(low, q2, zero), jnp.where(low, zero, q2)], axis=0)


def _band_softmax(qst, k2, v2, bias, sinks):
    s = lax.dot_general(qst, k2, (((1,), (1,)), ((), ())), preferred_element_type=F32)
    s = s + bias
    rows, tk = s.shape
    m = jnp.max(s, axis=-1, keepdims=True)
    if sinks is not None:
        seg = rows // len(sinks)
        m = jnp.concatenate([jnp.maximum(m[h * seg:(h + 1) * seg], sk) for h, sk in enumerate(sinks)], axis=0)
    m = jnp.broadcast_to(m, (rows, LANES))
    e = jnp.concatenate([jnp.exp(s[:, c * LANES:(c + 1) * LANES] - m) for c in range(tk // LANES)], axis=1)
    v_ones = jnp.concatenate([v2, jnp.ones((tk, LANES), BF16)], axis=1)
    od = jnp.dot(e.astype(BF16), v_ones, preferred_element_type=F32)
    o, den = od[:, :LANES], od[:, LANES:]
    if sinks is not None:
        den = jnp.concatenate([den[h * seg:(h + 1) * seg] + jnp.exp(sk - m[h * seg:(h + 1) * seg])
                               for h, sk in enumerate(sinks)], axis=0)
    return o / den, m, den


def _fill_band_bias(bias_ref, tq, window):
    rows, tk = bias_ref.shape[1:]
    row = lax.broadcasted_iota(I32, (rows, tk), 0) & (tq - 1)
    col = lax.broadcasted_iota(I32, (rows, tk), 1)
    for i in range(bias_ref.shape[0]):
        bias_ref[i] = jnp.where(jnp.abs(col - row - i * window) <= window, 0.0, NEG_INF)


def _dil_kernel(q_ref, k_ref, v_ref, o_ref, l_ref, bias_ref, *, length, tq, tk):
    low = lax.broadcasted_iota(I32, (tq, LANES), 1) < 64
    _fill_band_bias(bias_ref, tq, DIL_HALF_WINDOW)
    nq = length // tq

    def body(j, carry):
        r = j // nq
        qs = pl.multiple_of((j % nq) * tq, tq)
        ks = pl.multiple_of(jnp.clip(qs - DIL_HALF_WINDOW, 0, length - tk), DIL_HALF_WINDOW)
        which = (qs - ks) // DIL_HALF_WINDOW
        for c in range(DIL_GROUP_W // LANES):
            cs = slice(c * LANES, (c + 1) * LANES)
            qst = _split_heads(q_ref[r, pl.ds(qs, tq), cs], low)
            o, m, den = _band_softmax(qst, k_ref[r, pl.ds(ks, tk), cs], v_ref[r, pl.ds(ks, tk), cs],
                                      bias_ref[which], None)
            lse = m + jnp.log(den)
            o_ref[r, pl.ds(qs, tq), cs] = jnp.where(low, o[:tq], o[tq:]).astype(BF16)
            l_ref[r, pl.ds(qs, tq), cs] = jnp.where(low, lse[:tq], lse[tq:])
        return carry

    lax.fori_loop(0, q_ref.shape[0] * nq, body, 0, unroll=16)


def _dil_attention(q, k, v):
    batch, d, length, w = q.shape
    tq, tk = 128, 256
    spec = pl.BlockSpec((None, d, length, w), lambda b: (b, 0, 0, 0))
    return pl.pallas_call(
        functools.partial(_dil_kernel, length=length, tq=tq, tk=tk),
        grid=(batch,),
        in_specs=[spec, spec, spec],
        out_specs=[spec, spec],
        out_shape=[jax.ShapeDtypeStruct(q.shape, BF16), jax.ShapeDtypeStruct(q.shape, F32)],
        scratch_shapes=[pltpu.VMEM((3, 2 * tq, tk), F32)],
        compiler_params=_cparams(("arbitrary",)),
        name=f"dil{d}",
    )(q, k, v)


def _swa_kernel(sink_ref, q_ref, k_ref, v_ref, o_ref, bias_ref, *, length, tq, tk):
    low = lax.broadcasted_iota(I32, (tq, LANES), 1) < 64
    nblk = q_ref.shape[1] // LANES
    _fill_band_bias(bias_ref, tq, SWA_WINDOW)

    def body(j, carry):
        qs = pl.multiple_of(j * tq, tq)
        ks = pl.multiple_of(jnp.clip(qs - SWA_WINDOW, 0, length - tk), SWA_WINDOW)
        bias = bias_ref[(qs - ks) // SWA_WINDOW]
        for b in range(nblk):
            cs = slice((b // 2) * LANES, (b // 2 + 1) * LANES)
            bs = slice(b * LANES, (b + 1) * LANES)
            qst = _split_heads(q_ref[pl.ds(qs, tq), bs], low)
            sinks = (sink_ref[2 * b], sink_ref[2 * b + 1])
            o, _, _ = _band_softmax(qst, k_ref[pl.ds(ks, tk), cs], v_ref[pl.ds(ks, tk), cs], bias, sinks)
            o_ref[pl.ds(qs, tq), bs] = jnp.where(low, o[:tq], o[tq:]).astype(BF16)
        return carry

    lax.fori_loop(0, length // tq, body, 0, unroll=8)


def _swa_attention(sink, q, k, v):
    batch, length, qw = q.shape
    tq, tk = 128, 384
    return pl.pallas_call(
        functools.partial(_swa_kernel, length=length, tq=tq, tk=tk),
        grid=(batch,),
        in_specs=[pl.BlockSpec(memory_space=pltpu.SMEM),
                  pl.BlockSpec((None, length, qw), lambda b: (b, 0, 0)),
                  pl.BlockSpec((None, length, k.shape[2]), lambda b: (b, 0, 0)),
                  pl.BlockSpec((None, length, v.shape[2]), lambda b: (b, 0, 0))],
        out_specs=pl.BlockSpec((None, length, qw), lambda b: (b, 0, 0)),
        out_shape=jax.ShapeDtypeStruct(q.shape, BF16),
        scratch_shapes=[pltpu.VMEM((3, 2 * tq, tk), F32)],
        compiler_params=_cparams(("arbitrary",)),
        name="swa",
    )(sink, q, k, v)


def _route_rows(logits):
    lane = lax.broadcasted_iota(I32, logits.shape, 1).astype(F32)
    big = 1e9
    is_g = lane < N_GROUPS
    gl = jnp.where(is_g, logits, NEG_INF)
    gmax = jnp.max(gl, axis=-1, keepdims=True)
    gsel = jnp.min(jnp.where(is_g & (gl == gmax), lane, big), axis=-1, keepdims=True)
    gw = 1.0 / jnp.sum(jnp.where(is_g, jnp.exp(gl - gmax), 0.0), axis=-1, keepdims=True)
    e_lo = N_GROUPS + gsel * EXPERTS_PER_GROUP
    in_grp = (lane >= e_lo) & (lane < e_lo + EXPERTS_PER_GROUP)
    el = jnp.where(in_grp, logits, NEG_INF)
    m1 = jnp.max(el, axis=-1, keepdims=True)
    i1 = jnp.min(jnp.where(in_grp & (el == m1), lane, big), axis=-1, keepdims=True)
    el2 = jnp.where(lane == i1, NEG_INF, el)
    m2 = jnp.max(el2, axis=-1, keepdims=True)
    i2 = jnp.min(jnp.where(in_grp & (lane != i1) & (el2 == m2), lane, big), axis=-1, keepdims=True)
    t = jnp.exp(m2 - m1)
    tw1 = gw / (1.0 + t)
    tw2 = gw * t / (1.0 + t)
    out = jnp.where(lane == 0, tw1, 0.0)
    out = jnp.where(lane == 1, tw2, out)
    out = jnp.where(lane == 2, i1 - N_GROUPS, out)
    return jnp.where(lane == 3, i2 - N_GROUPS, out)


def _outproj_kernel(o0_ref, l0_ref, o1_ref, l1_ref, o2_ref, l2_ref, ob_ref, sga_ref, sgb_ref, x_ref,
                    mod_ref, g_ref, wa_ref, wb_ref, wo_ref, wr_ref, br_ref,
                    x1_ref, xs_ref, route_ref, len_ref,
                    so1_ref, sl1_ref, so2_ref, sl2_ref, h2_ref, *, tm, sub, group):
    ntiles = pl.num_programs(0) - 1

    @pl.when(pl.program_id(0) == ntiles)
    def _():
        xs_ref[...] = jnp.zeros_like(xs_ref)

    pl.when(pl.program_id(0) < ntiles)(functools.partial(
        _outproj_tile, o0_ref, l0_ref, o1_ref, l1_ref, o2_ref, l2_ref, ob_ref, sga_ref, sgb_ref, x_ref,
        mod_ref, g_ref, wa_ref, wb_ref, wo_ref, wr_ref, br_ref, x1_ref, xs_ref, route_ref, len_ref,
        so1_ref, sl1_ref, so2_ref, sl2_ref, h2_ref, tm=tm, sub=sub, group=group))


def _outproj_tile(o0_ref, l0_ref, o1_ref, l1_ref, o2_ref, l2_ref, ob_ref, sga_ref, sgb_ref, x_ref,
                  mod_ref, g_ref, wa_ref, wb_ref, wo_ref, wr_ref, br_ref,
                  x1_ref, xs_ref, route_ref, len_ref,
                  so1_ref, sl1_ref, so2_ref, sl2_ref, h2_ref, *, tm, sub, group):
    dm = x_ref.shape[1]
    for (o_ref, l_ref, so_ref, sl_ref, d) in ((o1_ref, l1_ref, so1_ref, sl1_ref, DILATIONS[1]),
                                              (o2_ref, l2_ref, so2_ref, sl2_ref, DILATIONS[2])):
        for r in range(d):
            for c in range(2):
                cs = slice(c * LANES, (c + 1) * LANES)
                so_ref[c, pl.ds(r, tm // d, stride=d), :] = o_ref[0, r, :, cs].astype(F32)
                sl_ref[c, pl.ds(r, tm // d, stride=d), :] = l_ref[0, r, :, cs]
    mr = lax.broadcasted_iota(I32, (LANES, LANES), 0)
    mc = lax.broadcasted_iota(I32, (LANES, LANES), 1)
    move_hi = jnp.where(((mr < 2) & (mc == 2 * mr)) | ((mr >= 2) & (mr < 4) & (mc == mr + 2)), 1.0, 0.0).astype(BF16)
    move_lo = jnp.where((mr < 2) & (mc == 2 * mr + 1), 1.0, 0.0).astype(BF16)
    for t in range(tm // sub):
        rs = slice(t * sub, (t + 1) * sub)
        both = lambda ref: jnp.concatenate([ref[0, rs, :], ref[1, rs, :]], axis=1)
        o0, l0 = o0_ref[0, 0, rs, :].astype(F32), l0_ref[0, 0, rs, :]
        o1, l1, o2, l2 = both(so1_ref), both(sl1_ref), both(so2_ref), both(sl2_ref)
        mx = jnp.maximum(jnp.maximum(l0, l1), l2)
        w0, w1, w2 = jnp.exp(l0 - mx), jnp.exp(l1 - mx), jnp.exp(l2 - mx)
        o_a = (w0 * o0 + w1 * o1 + w2 * o2) / (w0 + w1 + w2)
        y_a = jnp.dot(o_a.astype(BF16), wa_ref[...], preferred_element_type=F32)
        y_b = jnp.dot(ob_ref[rs, :], wb_ref[...], preferred_element_type=F32)
        merged = sga_ref[rs, :].astype(F32) * y_a + sgb_ref[rs, :].astype(F32) * y_b
        mix = jnp.dot(merged.astype(BF16), wo_ref[...], preferred_element_type=F32)
        x1 = x_ref[rs, :] + mod_ref[0, 2:3, :] * mix
        x1_ref[rs, :] = x1
        h2 = _rms_mod(x1, g_ref[...], mod_ref[0, 3:4, :], mod_ref[0, 4:5, :]).astype(BF16)
        h2_ref[rs, :dm] = h2
        logits = jnp.dot(h2, wr_ref[...], preferred_element_type=F32) + br_ref[...]
        rt = _route_rows(logits)
        route_ref[rs, :] = rt
        hi = rt.astype(BF16)
        lo = (rt - hi.astype(F32)).astype(BF16)
        aux = (jnp.dot(hi, move_hi, preferred_element_type=F32) + jnp.dot(lo, move_lo, preferred_element_type=F32))
        h2_ref[rs, dm:] = aux.astype(BF16)

    part = route_ref[...]
    e1, e2 = part[:, 2:3], part[:, 3:4]
    lane = lax.broadcasted_iota(I32, (tm, LANES), 1).astype(F32)
    onehot = jnp.where((lane == e1) | (lane == e2), 1.0, 0.0)
    rr = lax.broadcasted_iota(I32, (tm, tm), 0)
    cc = lax.broadcasted_iota(I32, (tm, tm), 1)
    tri = jnp.where(rr > cc, 1.0, 0.0).astype(BF16)
    prefix = jnp.dot(tri, onehot.astype(BF16), preferred_element_type=F32)
    cnt = jnp.sum(onehot, axis=0, keepdims=True)
    seg_len = jnp.ceil(cnt * (1.0 / SEG_ROWS)) * SEG_ROWS
    ur = lax.broadcasted_iota(I32, (LANES, LANES), 0)
    uc = lax.broadcasted_iota(I32, (LANES, LANES), 1)
    upper = jnp.where(ur < uc, 1.0, 0.0).astype(BF16)
    seg_off = jnp.dot(jnp.broadcast_to(seg_len, (8, LANES)).astype(BF16), upper, preferred_element_type=F32)[0:1, :]
    slot_map = seg_off + prefix
    s1 = jnp.sum(jnp.where(lane == e1, slot_map, 0.0), axis=-1, keepdims=True)
    s2 = jnp.sum(jnp.where(lane == e2, slot_map, 0.0), axis=-1, keepdims=True)
    len_ref[0] = jnp.broadcast_to(seg_len, (8, LANES))
    route = jnp.where(lane == 2, s1, jnp.where(lane == 3, s2, part))
    route_ref[...] = route

    pr = lax.broadcasted_iota(I32, (8, LANES), 0)
    pc = lax.broadcasted_iota(I32, (8, LANES), 1)
    lane_pick = jnp.where(pc == pr + 2, 1.0, 0.0)
    slots_t = lax.dot_general(lane_pick, route, (((1,), (1,)), ((), ())), preferred_element_type=F32,
                              precision=lax.Precision.HIGHEST)
    live = jnp.sum(seg_len).astype(I32)

    def sort_rows(lo, rows):
        srow = (lax.broadcasted_iota(I32, (rows, tm), 0) + lo).astype(F32)
        pick = jnp.where((srow == slots_t[0:1, :]) | (srow == slots_t[1:2, :]), 1.0, 0.0).astype(BF16)
        xs_ref[lo:lo + rows, :] = jnp.dot(pick, h2_ref[...], preferred_element_type=F32).astype(BF16)

    def clear_rows(lo, rows):
        xs_ref[lo:lo + rows, :] = jnp.zeros((rows, xs_ref.shape[1]), BF16)

    head = xs_ref.shape[0] - group
    sort_rows(0, head)
    pl.when(live > head)(functools.partial(sort_rows, head, group))
    pl.when(live <= head)(functools.partial(clear_rows, head, group))


def _outproj(dil_outs, ob, sga, sgb, x2, mod3, g_ffn, wa, wb, wo, wr, br, batch, seq, nslots):
    n, dm = x2.shape
    tm = MOE_TILE
    tpb = seq // tm
    ntiles = n // tm
    tile = lambda i: jnp.minimum(i, ntiles - 1)
    row = lambda i: (tile(i), 0)
    const = lambda i: (0, 0)
    in_specs = []
    args = []
    for (o, l), d in zip(dil_outs, DILATIONS):
        spec = pl.BlockSpec((1, d, tm // d, DIL_GROUP_W), lambda i: (tile(i) // tpb, 0, tile(i) % tpb, 0))
        in_specs += [spec, spec]
        args += [o, l]
    in_specs += [pl.BlockSpec((tm, ob.shape[1]), row), pl.BlockSpec((tm, dm), row), pl.BlockSpec((tm, dm), row),
                 pl.BlockSpec((tm, dm), row),
                 pl.BlockSpec((1, 6, dm), lambda i: (tile(i) // tpb, 0, 0)),
                 pl.BlockSpec((1, dm), const),
                 pl.BlockSpec(wa.shape, const), pl.BlockSpec(wb.shape, const), pl.BlockSpec(wo.shape, const),
                 pl.BlockSpec(wr.shape, const), pl.BlockSpec(br.shape, const)]
    args += [ob, sga, sgb, x2, mod3, g_ffn, wa, wb, wo, wr, br]
    width = dm + LANES
    return pl.pallas_call(
        functools.partial(_outproj_kernel, tm=tm, sub=512, group=SLOT_GROUP),
        grid=(ntiles + 1,),
        in_specs=in_specs,
        out_specs=[pl.BlockSpec((tm, dm), row), pl.BlockSpec((nslots, width), lambda i: (i, 0)),
                   pl.BlockSpec((tm, LANES), row), pl.BlockSpec((1, 8, LANES), lambda i: (tile(i), 0, 0))],
        out_shape=[jax.ShapeDtypeStruct((n, dm), F32), jax.ShapeDtypeStruct(((ntiles + 1) * nslots, width), BF16),
                   jax.ShapeDtypeStruct((n, LANES), F32), jax.ShapeDtypeStruct((ntiles, 8, LANES), F32)],
        scratch_shapes=[pltpu.VMEM((2, tm, LANES), F32)] * 4 + [pltpu.VMEM((tm, width), BF16)],
        compiler_params=_cparams(("arbitrary",)),
        name="outproj",
    )(*args)


def _wait_copies(count, copy):
    def wait_one(c, carry):
        copy.wait()
        return carry

    lax.fori_loop(0, count, wait_one, 0)


def _experts_kernel(first_ref, count_ref, chunk_ref, wg_ref, wu_ref, wd_ref, xs_hbm, ys_hbm,
                    xbuf, ybuf, zbuf, xsem, ysem, zsem, *, nblk):
    e = pl.program_id(0)
    last = pl.num_programs(0) - 1
    first = first_ref[e]
    count = count_ref[e]
    used = first_ref[last] + count_ref[last]
    dm = ybuf.shape[2]
    cpb = ROW_BLOCK // SEG_ROWS

    def chunk_rows(blk, j):
        return pl.ds(pl.multiple_of(chunk_ref[blk * cpb + j], SEG_ROWS), SEG_ROWS)

    def gather(blk):
        for j in range(cpb):
            pltpu.make_async_copy(xs_hbm.at[chunk_rows(blk, j), :], xbuf.at[blk & 1, pl.ds(j * SEG_ROWS, SEG_ROWS), :],
                                  xsem.at[blk & 1]).start(priority=1)

    def gather_done(blk):
        return pltpu.make_async_copy(xs_hbm.at[pl.ds(0, ROW_BLOCK), :], xbuf.at[blk & 1], xsem.at[blk & 1])

    def put(blk):
        rows = pl.ds(pl.multiple_of(blk * ROW_BLOCK, ROW_BLOCK), ROW_BLOCK)
        return pltpu.make_async_copy(ybuf.at[blk & 1], ys_hbm.at[rows, :], ysem.at[blk & 1])

    def zero_blk(b):
        rows = pl.ds(pl.multiple_of(b * ROW_BLOCK, ROW_BLOCK), ROW_BLOCK)
        return pltpu.make_async_copy(zbuf, ys_hbm.at[rows, :], zsem)

    @pl.when(e == 0)
    def _():
        zbuf[...] = jnp.zeros_like(zbuf)

        def start(b, carry):
            zero_blk(b).start()
            return carry

        lax.fori_loop(used, nblk, start, 0)

        @pl.when(used > 0)
        def _():
            gather(0)

    def body(b, carry):
        blk = first + b
        slot = blk & 1
        gather_done(blk).wait()

        @pl.when(blk + 1 < used)
        def _():
            gather(blk + 1)

        @pl.when(blk >= 2)
        def _():
            put(blk - 2).wait()

        xb = xbuf[slot, :, :dm]
        aux = xbuf[slot, :, dm:].astype(F32)
        w = jnp.where(aux[:, 4:5] == e.astype(F32), aux[:, 0:1] + aux[:, 1:2], aux[:, 2:3] + aux[:, 3:4])
        g = jnp.dot(xb, wg_ref[0], preferred_element_type=F32)
        u = jnp.dot(xb, wu_ref[0], preferred_element_type=F32)
        a = (g * (0.5 * jnp.tanh(0.5 * g) + 0.5)) * u
        y = jnp.dot(a.astype(BF16), wd_ref[0], preferred_element_type=F32)
        ybuf[slot] = (y * w).astype(BF16)
        put(blk).start()
        return carry

    lax.fori_loop(0, count, body, 0)

    @pl.when(e == last)
    def _():
        @pl.when(used >= 2)
        def _():
            put(used - 2).wait()

        @pl.when(used >= 1)
        def _():
            put(used - 1).wait()

        _wait_copies(nblk - used, zero_blk(0))


def _experts(first_blk, count_blk, chunk_tbl, x_sorted, w_gate, w_up, w_down, nblk):
    rows, width = x_sorted.shape
    n_exp, dm, de = w_gate.shape
    wmap = lambda e, *_: (e, 0, 0)
    grid_spec = pltpu.PrefetchScalarGridSpec(
        num_scalar_prefetch=3,
        grid=(n_exp,),
        in_specs=[pl.BlockSpec((1, dm, de), wmap),
                  pl.BlockSpec((1, dm, de), wmap),
                  pl.BlockSpec((1, de, dm), wmap),
                  pl.BlockSpec(memory_space=pl.ANY)],
        out_specs=pl.BlockSpec(memory_space=pl.ANY),
        scratch_shapes=[pltpu.VMEM((2, ROW_BLOCK, width), BF16), pltpu.VMEM((2, ROW_BLOCK, dm), BF16),
                        pltpu.VMEM((ROW_BLOCK, dm), BF16),
                        pltpu.SemaphoreType.DMA((2,)), pltpu.SemaphoreType.DMA((2,)), pltpu.SemaphoreType.DMA(())],
    )
    return pl.pallas_call(
        functools.partial(_experts_kernel, nblk=nblk),
        grid_spec=grid_spec,
        out_shape=jax.ShapeDtypeStruct((nblk * ROW_BLOCK, dm), BF16),
        compiler_params=_cparams(("arbitrary",)),
        name="experts",
    )(first_blk, count_blk, chunk_tbl, w_gate, w_up, w_down, x_sorted)


def _slot_parts(tm, nslots, group):
    return [(0, 2 * tm)] + [(lo, group) for lo in range(2 * tm, nslots, group)]


def _combine_kernel(src_ref, live_ref, route_ref, x1_ref, mod_ref, g_ref, y_ref, o_ref, ys_ref, sem,
                    *, tm, nslots, group):
    i = pl.program_id(0)
    last = pl.num_programs(0) - 1
    slot = i & 1
    nchk = nslots // SEG_ROWS
    parts = _slot_parts(tm, nslots, group)

    def for_live_parts(tile, fn):
        for p, (lo, rows) in enumerate(parts):
            if p == 0:
                fn(lo, rows)
            else:
                pl.when(live_ref[tile] > lo)(functools.partial(fn, lo, rows))

    def gather(tile, s):
        def start(lo, rows):
            for c in range(lo // SEG_ROWS, (lo + rows) // SEG_ROWS):
                src = pl.multiple_of(src_ref[tile * nchk + c], SEG_ROWS)
                pltpu.make_async_copy(y_ref.at[pl.ds(src, SEG_ROWS), :],
                                      ys_ref.at[s, pl.ds(c * SEG_ROWS, SEG_ROWS), :], sem.at[s]).start()

        for_live_parts(tile, start)

    def wait(tile, s):
        for_live_parts(tile, lambda lo, rows: pltpu.make_async_copy(
            y_ref.at[pl.ds(0, rows), :], ys_ref.at[s, pl.ds(lo, rows), :], sem.at[s]).wait())

    @pl.when(i == 0)
    def _():
        gather(0, 0)

    nxt = jnp.minimum(i + 1, last)
    gather(nxt, 1 - slot)
    wait(i, slot)
    route = route_ref[...]

    def finish(rows):
        scol = lax.broadcasted_iota(I32, (tm, rows), 1).astype(F32)
        pick = jnp.where((scol == route[:, 2:3]) | (scol == route[:, 3:4]), 1.0, 0.0).astype(BF16)
        moe = jnp.dot(pick, ys_ref[slot, :rows, :], preferred_element_type=F32)
        x = x1_ref[...] + mod_ref[0, 5:6, :] * moe
        ms = jnp.mean(x * x, axis=-1, keepdims=True)
        o_ref[...] = (x * lax.rsqrt(ms + RMS_EPS)) * g_ref[...]

    ends = [lo + rows for lo, rows in parts]
    for p, end in enumerate(ends):
        above = live_ref[i] > (ends[p - 1] if p else -1)
        cond = above if p == len(ends) - 1 else above & (live_ref[i] <= end)
        pl.when(cond)(functools.partial(finish, end))

    @pl.when(i == last)
    def _():
        wait(nxt, 1 - slot)


def _combine(src_tbl, live_tbl, route, x1, mod3, g_final, y_rows, seq, tm, nslots):
    n, dm = x1.shape
    tpb = seq // tm
    grid_spec = pltpu.PrefetchScalarGridSpec(
        num_scalar_prefetch=2,
        grid=(n // tm,),
        in_specs=[pl.BlockSpec((tm, LANES), lambda i, *_: (i, 0)),
                  pl.BlockSpec((tm, dm), lambda i, *_: (i, 0)),
                  pl.BlockSpec((1, 6, dm), lambda i, *_: (i // tpb, 0, 0)),
                  pl.BlockSpec((1, dm), lambda i, *_: (0, 0)),
                  pl.BlockSpec(memory_space=pl.ANY)],
        out_specs=pl.BlockSpec((tm, dm), lambda i, *_: (i, 0)),
        scratch_shapes=[pltpu.VMEM((2, nslots, dm), BF16), pltpu.SemaphoreType.DMA((2,))],
    )
    return pl.pallas_call(
        functools.partial(_combine_kernel, tm=tm, nslots=nslots, group=SLOT_GROUP),
        grid_spec=grid_spec,
        out_shape=jax.ShapeDtypeStruct((n, dm), F32),
        compiler_params=_cparams(("arbitrary",)),
        name="combine",
    )(src_tbl, live_tbl, route, x1, mod3, g_final, y_rows)


def _rope_angles(positions):
    half = HEAD_DIM // 2
    inv_freq = ROPE_THETA ** (-jnp.arange(half, dtype=F32) * (2.0 / HEAD_DIM))
    freq = jnp.tile(inv_freq, LANES // half)
    sign = jnp.tile(jnp.concatenate([-jnp.ones((half,), F32), jnp.ones((half,), F32)]), LANES // HEAD_DIM)
    ang = positions.astype(F32).reshape(-1, 1) * freq
    return ang, sign.reshape(1, LANES)


def _expert_chunk_table(lens, pstart, tot, pend, nblk, nslots):
    ntiles = lens.shape[0]
    cpb = ROW_BLOCK // SEG_ROWS
    runs = lens // SEG_ROWS
    q = jnp.arange(nblk * cpb, dtype=I32)
    owner = jnp.minimum(jnp.sum((pend[None, :] // SEG_ROWS <= q[:, None]).astype(I32), axis=1), N_EXPERTS - 1)
    is_e = owner[:, None] == jnp.arange(N_EXPERTS, dtype=I32)[None, :]
    pick_e = lambda v: jnp.sum(jnp.where(is_e, v[None, :], 0), axis=1)
    off = q - pick_e(pstart // SEG_ROWS)
    in_run = (off < pick_e(tot // SEG_ROWS)) & (q < pend[-1] // SEG_ROWS)
    upto = jnp.cumsum(runs, axis=0)
    upto_e = jnp.sum(jnp.where(is_e[:, None, :], upto[None, :, :], 0), axis=2)
    tile = jnp.minimum(jnp.sum((upto_e <= off[:, None]).astype(I32), axis=1), ntiles - 1)
    is_t = tile[:, None] == jnp.arange(ntiles, dtype=I32)[None, :]
    run_slot0 = jnp.cumsum(runs, axis=1) - runs - (upto - runs)
    slot0 = jnp.sum(jnp.where(is_t[:, :, None] & is_e[:, None, :], run_slot0[None, :, :], 0), axis=(1, 2))
    live_row = tile * nslots + (slot0 + off) * SEG_ROWS
    spare_row = ntiles * nslots + (((q // cpb) % 2) * cpb + q % cpb) * SEG_ROWS
    return jnp.where(in_run, live_row, spare_row)


def kernel(x, c, positions, w_ada, b_ada, g_mix, w_in, sink_logits, w_branch_a, w_branch_b, w_out, g_ffn,
           w_group, b_group, w_route, b_route, w_expert_gate, w_expert_up, w_expert_down, g_final):
    batch, seq, dm = x.shape
    n = batch * seq
    assert w_ada.shape[0] == 1, "one layer"
    x2 = x.reshape(n, dm)

    c8 = jnp.pad(c, ((0, 8 - batch), (0, 0)))
    mod = _ada(c8, w_ada[0], b_ada[0].reshape(1, -1))
    mod3 = mod[:batch].reshape(batch, 6, dm)

    angles, sign = _rope_angles(positions)
    outs = _inproj(x2, mod3, g_mix[0].reshape(1, dm), angles, sign, w_in[0].astype(BF16),
                   (w_expert_gate[0], w_expert_up[0], w_expert_down[0]), batch, seq)
    qkv = outs[:9]
    qb, kb, vb, sga, sgb = outs[9:14]
    expert_w_bf = outs[14:]

    dil_outs = [_dil_attention(qkv[3 * g], qkv[3 * g + 1], qkv[3 * g + 2]) for g in range(len(DILATIONS))]
    ob = _swa_attention(sink_logits[0], qb.reshape(batch, seq, -1), kb.reshape(batch, seq, -1),
                        vb.reshape(batch, seq, -1)).reshape(n, -1)

    pad = LANES - N_GROUPS - N_EXPERTS
    wr = jnp.concatenate([w_group[0], w_route[0], jnp.zeros((dm, pad), F32)], axis=1).astype(BF16)
    br = jnp.concatenate([b_group[0], b_route[0], jnp.zeros((pad,), F32)]).reshape(1, LANES)
    ntiles = n // MOE_TILE
    nslots = 2 * MOE_TILE + N_EXPERTS * SEG_ROWS
    x1, x_sorted, route, seg_lens = _outproj(dil_outs, ob, sga, sgb, x2, mod3, g_ffn[0].reshape(1, dm),
                                             w_branch_a[0].astype(BF16), w_branch_b[0].astype(BF16),
                                             w_out[0].astype(BF16), wr, br, batch, seq, nslots)

    nblk = -(-(2 * n + ntiles * N_EXPERTS * (SEG_ROWS - 1)) // ROW_BLOCK) + N_EXPERTS
    lens = seg_lens[:, 0, :N_EXPERTS].astype(I32)
    tot = jnp.sum(lens, axis=0)
    padded = (tot + ROW_BLOCK - 1) // ROW_BLOCK * ROW_BLOCK
    pend = jnp.cumsum(padded)
    pstart = pend - padded
    chunk_tbl = _expert_chunk_table(lens, pstart, tot, pend, nblk, nslots)
    live_tbl = jnp.sum(lens, axis=1)

    nchk = nslots // SEG_ROWS
    base = pstart[None, :] + jnp.cumsum(lens, axis=0) - lens
    run_end = jnp.cumsum(lens // SEG_ROWS, axis=1)
    chunk = jnp.arange(nchk, dtype=I32)
    owner = jnp.sum((run_end[:, None, :] <= chunk[None, :, None]).astype(I32), axis=2)
    is_owner = owner[:, :, None] == jnp.arange(N_EXPERTS, dtype=I32)[None, None, :]
    run_row0 = base - (run_end - lens // SEG_ROWS) * SEG_ROWS
    row = jnp.sum(jnp.where(is_owner, run_row0[:, None, :], 0), axis=2) + chunk[None, :] * SEG_ROWS
    src_tbl = jnp.where(owner < N_EXPERTS, row, chunk[None, :] * SEG_ROWS).reshape(-1)

    y_rows = _experts(pstart // ROW_BLOCK, padded // ROW_BLOCK, chunk_tbl, x_sorted, *expert_w_bf, nblk)
    out = _combine(src_tbl, live_tbl, route, x1, mod3, g_final.reshape(1, dm), y_rows, seq, MOE_TILE, nslots)
    return out.reshape(batch, seq, dm)
```

```python
import functools

import jax
import jax.numpy as jnp
from jax import lax
from jax.experimental import pallas as pl
from jax.experimental.pallas import tpu as pltpu

F32 = jnp.float32
BF16 = jnp.bfloat16
I32 = jnp.int32

HEAD_DIM = 64
ROPE_THETA = 10000.0
RMS_EPS = 1e-6
NEG_INF = -1e30
Q_SCALE = HEAD_DIM ** -0.5
DILATIONS = (1, 4, 16)
DIL_HALF_WINDOW = 64
DIL_GROUP_W = 256
SWA_WINDOW = 128
N_GROUPS = 4
EXPERTS_PER_GROUP = 8
N_EXPERTS = 32
LANES = 128
ROW_BLOCK = 512
SEG_ROWS = 16
MOE_TILE = 512
SLOT_GROUP = 256
VMEM_LIMIT = 56 * 1024 * 1024


def _cparams(sem):
    return pltpu.CompilerParams(dimension_semantics=sem, vmem_limit_bytes=VMEM_LIMIT)


def _ada_kernel(c_ref, w_ref, b_ref, o_ref):
    c = c_ref[...]
    cs = c * jax.nn.sigmoid(c)
    o_ref[...] = jnp.dot(cs.astype(BF16), w_ref[...].astype(BF16), preferred_element_type=F32) + b_ref[...]


def _ada(c8, w_ada, b_ada):
    d, n = w_ada.shape
    tn = 1536
    return pl.pallas_call(
        _ada_kernel,
        grid=(n // tn,),
        in_specs=[pl.BlockSpec((8, d), lambda j: (0, 0)),
                  pl.BlockSpec((d, tn), lambda j: (0, j)),
                  pl.BlockSpec((1, tn), lambda j: (0, j))],
        out_specs=pl.BlockSpec((8, tn), lambda j: (0, j)),
        out_shape=jax.ShapeDtypeStruct((8, n), F32),
        compiler_params=_cparams(("arbitrary",)),
        name="ada",
    )(c8, w_ada, b_ada)


def _rms_mod(x, g, shift, scale):
    ms = jnp.mean(x * x, axis=-1, keepdims=True)
    return (x * lax.rsqrt(ms + RMS_EPS)) * (g * (1.0 + scale)) + shift


def _inproj_kernel(x_ref, mod_ref, g_ref, ang_ref, sign_ref, w_ref, eg_ref, eu_ref, ed_ref,
                   q0_ref, k0_ref, v0_ref, q1_ref, k1_ref, v1_ref, q2_ref, k2_ref, v2_ref,
                   qb_ref, kb_ref, vb_ref, sga_ref, sgb_ref, egb_ref, eub_ref, edb_ref, stg_ref, *, tm, n_exp):
    def cast_expert():
        egb_ref[...] = eg_ref[...].astype(BF16)
        eub_ref[...] = eu_ref[...].astype(BF16)
        edb_ref[...] = ed_ref[...].astype(BF16)

    if n_exp is None:
        cast_expert()
    else:
        pl.when(pl.program_id(0) < n_exp)(cast_expert)

    h = _rms_mod(x_ref[...], g_ref[...], mod_ref[0, 0:1, :], mod_ref[0, 1:2, :])
    hb = h.astype(BF16)
    ang = ang_ref[...]
    cos = jnp.cos(ang)
    sin = jnp.sin(ang) * sign_ref[...]
    lane = lax.broadcasted_iota(I32, (tm, LANES), 1)
    first_half = (lane & 32) == 0
    low = lane < 64

    def proj(c0, width):
        return jnp.dot(hb, w_ref[:, c0:c0 + width], preferred_element_type=F32)

    def rope(t):
        rot = jnp.where(first_half, pltpu.roll(t, 96, 1), pltpu.roll(t, 32, 1))
        return t * cos + rot * sin

    def rope256(p):
        return jnp.concatenate([rope(p[:, :LANES]), rope(p[:, LANES:])], axis=1)

    def store_group(ref, val, d):
        if d == 1:
            ref[0, 0] = val.astype(BF16)
        else:
            for c in range(2):
                stg_ref[c] = val[:, c * LANES:(c + 1) * LANES]
            for r in range(d):
                for c in range(2):
                    ref[0, r, :, c * LANES:(c + 1) * LANES] = (
                        stg_ref[c, pl.ds(r, tm // d, stride=d), :].astype(BF16))

    q_refs = (q0_ref, q1_ref, q2_ref)
    k_refs = (k0_ref, k1_ref, k2_ref)
    v_refs = (v0_ref, v1_ref, v2_ref)
    for g, d in enumerate(DILATIONS):
        store_group(q_refs[g], rope256(proj(g * 256, 256)) * Q_SCALE, d)
        store_group(k_refs[g], rope256(proj(768 + g * 256, 256)), d)
        store_group(v_refs[g], proj(1536 + g * 256, 256), d)
    for j in range(2):
        qb_ref[:, j * 256:(j + 1) * 256] = (rope256(proj(2304 + j * 256, 256)) * Q_SCALE).astype(BF16)
    kv = proj(2816, 256)
    kb = rope(kv[:, :LANES])
    vb = kv[:, LANES:]
    kb_sw = pltpu.roll(kb, 64, 1)
    vb_sw = pltpu.roll(vb, 64, 1)
    kb_ref[:, :LANES] = jnp.where(low, kb, kb_sw).astype(BF16)
    kb_ref[:, LANES:] = jnp.where(low, kb_sw, kb).astype(BF16)
    vb_ref[:, :LANES] = jnp.where(low, vb, vb_sw).astype(BF16)
    vb_ref[:, LANES:] = jnp.where(low, vb_sw, vb).astype(BF16)
    for j in range(4):
        sig = lambda t: 0.5 * jnp.tanh(0.5 * t) + 0.5
        sga_ref[:, j * 256:(j + 1) * 256] = sig(proj(3072 + j * 256, 256)).astype(BF16)
        sgb_ref[:, j * 256:(j + 1) * 256] = sig(proj(4096 + j * 256, 256)).astype(BF16)


def _inproj(x2, mod3, g_mix, angles, sign, w_in_bf, expert_w, batch, seq):
    n, dm = x2.shape
    tm = 512
    tpb = seq // tm
    grid = (n // tm,)
    row = lambda i: (i, 0)
    strided_specs, strided_shapes = [], []
    for d in DILATIONS:
        for _ in range(3):
            strided_specs.append(pl.BlockSpec((1, d, tm // d, DIL_GROUP_W), lambda i: (i // tpb, 0, i % tpb, 0)))
            strided_shapes.append(jax.ShapeDtypeStruct((batch, d, seq // d, DIL_GROUP_W), BF16))
    out_specs = strided_specs + [
        pl.BlockSpec((tm, 512), row), pl.BlockSpec((tm, 256), row), pl.BlockSpec((tm, 256), row),
        pl.BlockSpec((tm, dm), row), pl.BlockSpec((tm, dm), row)]
    out_shapes = strided_shapes + [
        jax.ShapeDtypeStruct((n, 512), BF16), jax.ShapeDtypeStruct((n, 256), BF16),
        jax.ShapeDtypeStruct((n, 256), BF16), jax.ShapeDtypeStruct((n, dm), BF16),
        jax.ShapeDtypeStruct((n, dm), BF16)]
    n_exp = expert_w[0].shape[0]
    assert n // tm >= n_exp, "one expert's weights are cast per grid step"
    emap = lambda i: (jnp.minimum(i, n_exp - 1), 0, 0)
    expert_specs = [pl.BlockSpec((1,) + w.shape[1:], emap) for w in expert_w]
    return pl.pallas_call(
        functools.partial(_inproj_kernel, tm=tm, n_exp=None if n // tm == n_exp else n_exp),
        grid=grid,
        in_specs=[pl.BlockSpec((tm, dm), row),
                  pl.BlockSpec((1, 6, dm), lambda i: (i // tpb, 0, 0)),
                  pl.BlockSpec((1, dm), lambda i: (0, 0)),
                  pl.BlockSpec((tm, LANES), row),
                  pl.BlockSpec((1, LANES), lambda i: (0, 0)),
                  pl.BlockSpec(w_in_bf.shape, lambda i: (0, 0), pipeline_mode=pl.Buffered(1))] + expert_specs,
        out_specs=out_specs + expert_specs,
        out_shape=out_shapes + [jax.ShapeDtypeStruct(w.shape, BF16) for w in expert_w],
        scratch_shapes=[pltpu.VMEM((2, tm, LANES), F32)],
        compiler_params=_cparams(("arbitrary",)),
        name="inproj",
    )(x2, mod3, g_mix, angles, sign, w_in_bf, *expert_w)


def _split_heads(q2, low):
    zero = jnp.zeros_like(q2)
    return jnp.concatenate([jnp.where(low, q2, zero), jnp.where(low, zero, q2)], axis=0)


def _band_softmax(qst, k2, v2, bias, sinks):
    s = lax.dot_general(qst, k2, (((1,), (1,)), ((), ())), preferred_element_type=F32)
    s = s + bias
    rows, tk = s.shape
    m = jnp.max(s, axis=-1, keepdims=True)
    if sinks is not None:
        seg = rows // len(sinks)
        m = jnp.concatenate([jnp.maximum(m[h * seg:(h + 1) * seg], sk) for h, sk in enumerate(sinks)], axis=0)
    m = jnp.broadcast_to(m, (rows, LANES))
    e = jnp.concatenate([jnp.exp(s[:, c * LANES:(c + 1) * LANES] - m) for c in range(tk // LANES)], axis=1)
    v_ones = jnp.concatenate([v2, jnp.ones((tk, LANES), BF16)], axis=1)
    od = jnp.dot(e.astype(BF16), v_ones, preferred_element_type=F32)
    o, den = od[:, :LANES], od[:, LANES:]
    if sinks is not None:
        den = jnp.concatenate([den[h * seg:(h + 1) * seg] + jnp.exp(sk - m[h * seg:(h + 1) * seg])
                               for h, sk in enumerate(sinks)], axis=0)
    return o / den, m, den


def _fill_band_bias(bias_ref, tq, window):
    rows, tk = bias_ref.shape[1:]
    row = lax.broadcasted_iota(I32, (rows, tk), 0) & (tq - 1)
    col = lax.broadcasted_iota(I32, (rows, tk), 1)
    for i in range(bias_ref.shape[0]):
        bias_ref[i] = jnp.where(jnp.abs(col - row - i * window) <= window, 0.0, NEG_INF)


def _dil_kernel(q_ref, k_ref, v_ref, o_ref, l_ref, bias_ref, *, length, tq, tk):
    low = lax.broadcasted_iota(I32, (tq, LANES), 1) < 64
    _fill_band_bias(bias_ref, tq, DIL_HALF_WINDOW)
    nq = length // tq

    def body(j, carry):
        r = j // nq
        qs = pl.multiple_of((j % nq) * tq, tq)
        ks = pl.multiple_of(jnp.clip(qs - DIL_HALF_WINDOW, 0, length - tk), DIL_HALF_WINDOW)
        which = (qs - ks) // DIL_HALF_WINDOW
        for c in range(DIL_GROUP_W // LANES):
            cs = slice(c * LANES, (c + 1) * LANES)
            qst = _split_heads(q_ref[r, pl.ds(qs, tq), cs], low)
            o, m, den = _band_softmax(qst, k_ref[r, pl.ds(ks, tk), cs], v_ref[r, pl.ds(ks, tk), cs],
                                      bias_ref[which], None)
            lse = m + jnp.log(den)
            o_ref[r, pl.ds(qs, tq), cs] = jnp.where(low, o[:tq], o[tq:]).astype(BF16)
            l_ref[r, pl.ds(qs, tq), cs] = jnp.where(low, lse[:tq], lse[tq:])
        return carry

    lax.fori_loop(0, q_ref.shape[0] * nq, body, 0, unroll=32)


def _dil_attention(q, k, v):
    batch, d, length, w = q.shape
    tq, tk = 128, 256
    spec = pl.BlockSpec((None, d, length, w), lambda b: (b, 0, 0, 0))
    return pl.pallas_call(
        functools.partial(_dil_kernel, length=length, tq=tq, tk=tk),
        grid=(batch,),
        in_specs=[spec, spec, spec],
        out_specs=[spec, spec],
        out_shape=[jax.ShapeDtypeStruct(q.shape, BF16), jax.ShapeDtypeStruct(q.shape, F32)],
        scratch_shapes=[pltpu.VMEM((3, 2 * tq, tk), F32)],
        compiler_params=_cparams(("arbitrary",)),
        name=f"dil{d}",
    )(q, k, v)


def _swa_kernel(sink_ref, q_ref, k_ref, v_ref, o_ref, bias_ref, *, length, tq, tk):
    low = lax.broadcasted_iota(I32, (tq, LANES), 1) < 64
    nblk = q_ref.shape[1] // LANES
    _fill_band_bias(bias_ref, tq, SWA_WINDOW)

    def body(j, carry):
        qs = pl.multiple_of(j * tq, tq)
        ks = pl.multiple_of(jnp.clip(qs - SWA_WINDOW, 0, length - tk), SWA_WINDOW)
        bias = bias_ref[(qs - ks) // SWA_WINDOW]
        for b in range(nblk):
            cs = slice((b // 2) * LANES, (b // 2 + 1) * LANES)
            bs = slice(b * LANES, (b + 1) * LANES)
            qst = _split_heads(q_ref[pl.ds(qs, tq), bs], low)
            sinks = (sink_ref[2 * b], sink_ref[2 * b + 1])
            o, _, _ = _band_softmax(qst, k_ref[pl.ds(ks, tk), cs], v_ref[pl.ds(ks, tk), cs], bias, sinks)
            o_ref[pl.ds(qs, tq), bs] = jnp.where(low, o[:tq], o[tq:]).astype(BF16)
        return carry

    lax.fori_loop(0, length // tq, body, 0, unroll=8)


def _swa_attention(sink, q, k, v):
    batch, length, qw = q.shape
    tq, tk = 128, 384
    return pl.pallas_call(
        functools.partial(_swa_kernel, length=length, tq=tq, tk=tk),
        grid=(batch,),
        in_specs=[pl.BlockSpec(memory_space=pltpu.SMEM),
                  pl.BlockSpec((None, length, qw), lambda b: (b, 0, 0)),
                  pl.BlockSpec((None, length, k.shape[2]), lambda b: (b, 0, 0)),
                  pl.BlockSpec((None, length, v.shape[2]), lambda b: (b, 0, 0))],
        out_specs=pl.BlockSpec((None, length, qw), lambda b: (b, 0, 0)),
        out_shape=jax.ShapeDtypeStruct(q.shape, BF16),
        scratch_shapes=[pltpu.VMEM((3, 2 * tq, tk), F32)],
        compiler_params=_cparams(("arbitrary",)),
        name="swa",
    )(sink, q, k, v)


def _route_rows(logits):
    lane = lax.broadcasted_iota(I32, logits.shape, 1).astype(F32)
    big = 1e9
    is_g = lane < N_GROUPS
    gl = jnp.where(is_g, logits, NEG_INF)
    gmax = jnp.max(gl, axis=-1, keepdims=True)
    gsel = jnp.min(jnp.where(is_g & (gl == gmax), lane, big), axis=-1, keepdims=True)
    gw = 1.0 / jnp.sum(jnp.where(is_g, jnp.exp(gl - gmax), 0.0), axis=-1, keepdims=True)
    e_lo = N_GROUPS + gsel * EXPERTS_PER_GROUP
    in_grp = (lane >= e_lo) & (lane < e_lo + EXPERTS_PER_GROUP)
    el = jnp.where(in_grp, logits, NEG_INF)
    m1 = jnp.max(el, axis=-1, keepdims=True)
    i1 = jnp.min(jnp.where(in_grp & (el == m1), lane, big), axis=-1, keepdims=True)
    el2 = jnp.where(lane == i1, NEG_INF, el)
    m2 = jnp.max(el2, axis=-1, keepdims=True)
    i2 = jnp.min(jnp.where(in_grp & (lane != i1) & (el2 == m2), lane, big), axis=-1, keepdims=True)
    t = jnp.exp(m2 - m1)
    tw1 = gw / (1.0 + t)
    tw2 = gw * t / (1.0 + t)
    out = jnp.where(lane == 0, tw1, 0.0)
    out = jnp.where(lane == 1, tw2, out)
    out = jnp.where(lane == 2, i1 - N_GROUPS, out)
    return jnp.where(lane == 3, i2 - N_GROUPS, out)


def _outproj_kernel(o0_ref, l0_ref, o1_ref, l1_ref, o2_ref, l2_ref, ob_ref, sga_ref, sgb_ref, x_ref,
                    mod_ref, g_ref, wa_ref, wb_ref, wo_ref, wr_ref, br_ref,
                    x1_ref, xs_ref, route_ref, len_ref,
                    so1_ref, sl1_ref, so2_ref, sl2_ref, h2_ref, *, tm, sub, group):
    ntiles = pl.num_programs(0) - 1

    @pl.when(pl.program_id(0) == ntiles)
    def _():
        xs_ref[...] = jnp.zeros_like(xs_ref)

    pl.when(pl.program_id(0) < ntiles)(functools.partial(
        _outproj_tile, o0_ref, l0_ref, o1_ref, l1_ref, o2_ref, l2_ref, ob_ref, sga_ref, sgb_ref, x_ref,
        mod_ref, g_ref, wa_ref, wb_ref, wo_ref, wr_ref, br_ref, x1_ref, xs_ref, route_ref, len_ref,
        so1_ref, sl1_ref, so2_ref, sl2_ref, h2_ref, tm=tm, sub=sub, group=group))


def _outproj_tile(o0_ref, l0_ref, o1_ref, l1_ref, o2_ref, l2_ref, ob_ref, sga_ref, sgb_ref, x_ref,
                  mod_ref, g_ref, wa_ref, wb_ref, wo_ref, wr_ref, br_ref,
                  x1_ref, xs_ref, route_ref, len_ref,
                  so1_ref, sl1_ref, so2_ref, sl2_ref, h2_ref, *, tm, sub, group):
    dm = x_ref.shape[1]
    for (o_ref, l_ref, so_ref, sl_ref, d) in ((o1_ref, l1_ref, so1_ref, sl1_ref, DILATIONS[1]),
                                              (o2_ref, l2_ref, so2_ref, sl2_ref, DILATIONS[2])):
        for r in range(d):
            for c in range(2):
                cs = slice(c * LANES, (c + 1) * LANES)
                so_ref[c, pl.ds(r, tm // d, stride=d), :] = o_ref[0, r, :, cs].astype(F32)
                sl_ref[c, pl.ds(r, tm // d, stride=d), :] = l_ref[0, r, :, cs]
    mr = lax.broadcasted_iota(I32, (LANES, LANES), 0)
    mc = lax.broadcasted_iota(I32, (LANES, LANES), 1)
    move_hi = jnp.where(((mr < 2) & (mc == 2 * mr)) | ((mr >= 2) & (mr < 4) & (mc == mr + 2)), 1.0, 0.0).astype(BF16)
    move_lo = jnp.where((mr < 2) & (mc == 2 * mr + 1), 1.0, 0.0).astype(BF16)
    for t in range(tm // sub):
        rs = slice(t * sub, (t + 1) * sub)
        both = lambda ref: jnp.concatenate([ref[0, rs, :], ref[1, rs, :]], axis=1)
        o0, l0 = o0_ref[0, 0, rs, :].astype(F32), l0_ref[0, 0, rs, :]
        o1, l1, o2, l2 = both(so1_ref), both(sl1_ref), both(so2_ref), both(sl2_ref)
        mx = jnp.maximum(jnp.maximum(l0, l1), l2)
        w0, w1, w2 = jnp.exp(l0 - mx), jnp.exp(l1 - mx), jnp.exp(l2 - mx)
        o_a = (w0 * o0 + w1 * o1 + w2 * o2) / (w0 + w1 + w2)
        y_a = jnp.dot(o_a.astype(BF16), wa_ref[...], preferred_element_type=F32)
        y_b = jnp.dot(ob_ref[rs, :], wb_ref[...], preferred_element_type=F32)
        merged = sga_ref[rs, :].astype(F32) * y_a + sgb_ref[rs, :].astype(F32) * y_b
        mix = jnp.dot(merged.astype(BF16), wo_ref[...], preferred_element_type=F32)
        x1 = x_ref[rs, :] + mod_ref[0, 2:3, :] * mix
        x1_ref[rs, :] = x1
        h2 = _rms_mod(x1, g_ref[...], mod_ref[0, 3:4, :], mod_ref[0, 4:5, :]).astype(BF16)
        h2_ref[rs, :dm] = h2
        logits = jnp.dot(h2, wr_ref[...], preferred_element_type=F32) + br_ref[...]
        rt = _route_rows(logits)
        route_ref[rs, :] = rt
        hi = rt.astype(BF16)
        lo = (rt - hi.astype(F32)).astype(BF16)
        aux = (jnp.dot(hi, move_hi, preferred_element_type=F32) + jnp.dot(lo, move_lo, preferred_element_type=F32))
        h2_ref[rs, dm:] = aux.astype(BF16)

    part = route_ref[...]
    e1, e2 = part[:, 2:3], part[:, 3:4]
    lane = lax.broadcasted_iota(I32, (tm, LANES), 1).astype(F32)
    onehot = jnp.where((lane == e1) | (lane == e2), 1.0, 0.0)
    rr = lax.broadcasted_iota(I32, (tm, tm), 0)
    cc = lax.broadcasted_iota(I32, (tm, tm), 1)
    tri = jnp.where(rr > cc, 1.0, 0.0).astype(BF16)
    prefix = jnp.dot(tri, onehot.astype(BF16), preferred_element_type=F32)
    cnt = jnp.sum(onehot, axis=0, keepdims=True)
    seg_len = jnp.ceil(cnt * (1.0 / SEG_ROWS)) * SEG_ROWS
    ur = lax.broadcasted_iota(I32, (LANES, LANES), 0)
    uc = lax.broadcasted_iota(I32, (LANES, LANES), 1)
    upper = jnp.where(ur < uc, 1.0, 0.0).astype(BF16)
    seg_off = jnp.dot(jnp.broadcast_to(seg_len, (8, LANES)).astype(BF16), upper, preferred_element_type=F32)[0:1, :]
    slot_map = seg_off + prefix
    s1 = jnp.sum(jnp.where(lane == e1, slot_map, 0.0), axis=-1, keepdims=True)
    s2 = jnp.sum(jnp.where(lane == e2, slot_map, 0.0), axis=-1, keepdims=True)
    len_ref[0] = jnp.broadcast_to(seg_len, (8, LANES))
    route = jnp.where(lane == 2, s1, jnp.where(lane == 3, s2, part))
    route_ref[...] = route

    pr = lax.broadcasted_iota(I32, (8, LANES), 0)
    pc = lax.broadcasted_iota(I32, (8, LANES), 1)
    lane_pick = jnp.where(pc == pr + 2, 1.0, 0.0)
    slots_t = lax.dot_general(lane_pick, route, (((1,), (1,)), ((), ())), preferred_element_type=F32,
                              precision=lax.Precision.HIGHEST)
    live = jnp.sum(seg_len).astype(I32)

    def sort_rows(lo, rows):
        srow = (lax.broadcasted_iota(I32, (rows, tm), 0) + lo).astype(F32)
        pick = jnp.where((srow == slots_t[0:1, :]) | (srow == slots_t[1:2, :]), 1.0, 0.0).astype(BF16)
        xs_ref[lo:lo + rows, :] = jnp.dot(pick, h2_ref[...], preferred_element_type=F32).astype(BF16)

    def clear_rows(lo, rows):
        xs_ref[lo:lo + rows, :] = jnp.zeros((rows, xs_ref.shape[1]), BF16)

    head = xs_ref.shape[0] - group
    sort_rows(0, head)
    pl.when(live > head)(functools.partial(sort_rows, head, group))
    pl.when(live <= head)(functools.partial(clear_rows, head, group))


def _outproj(dil_outs, ob, sga, sgb, x2, mod3, g_ffn, wa, wb, wo, wr, br, batch, seq, nslots):
    n, dm = x2.shape
    tm = MOE_TILE
    tpb = seq // tm
    ntiles = n // tm
    tile = lambda i: jnp.minimum(i, ntiles - 1)
    row = lambda i: (tile(i), 0)
    const = lambda i: (0, 0)
    in_specs = []
    args = []
    for (o, l), d in zip(dil_outs, DILATIONS):
        spec = pl.BlockSpec((1, d, tm // d, DIL_GROUP_W), lambda i: (tile(i) // tpb, 0, tile(i) % tpb, 0))
        in_specs += [spec, spec]
        args += [o, l]
    in_specs += [pl.BlockSpec((tm, ob.shape[1]), row), pl.BlockSpec((tm, dm), row), pl.BlockSpec((tm, dm), row),
                 pl.BlockSpec((tm, dm), row),
                 pl.BlockSpec((1, 6, dm), lambda i: (tile(i) // tpb, 0, 0)),
                 pl.BlockSpec((1, dm), const),
                 pl.BlockSpec(wa.shape, const), pl.BlockSpec(wb.shape, const), pl.BlockSpec(wo.shape, const),
                 pl.BlockSpec(wr.shape, const), pl.BlockSpec(br.shape, const)]
    args += [ob, sga, sgb, x2, mod3, g_ffn, wa, wb, wo, wr, br]
    width = dm + LANES
    return pl.pallas_call(
        functools.partial(_outproj_kernel, tm=tm, sub=512, group=SLOT_GROUP),
        grid=(ntiles + 1,),
        in_specs=in_specs,
        out_specs=[pl.BlockSpec((tm, dm), row), pl.BlockSpec((nslots, width), lambda i: (i, 0)),
                   pl.BlockSpec((tm, LANES), row), pl.BlockSpec((1, 8, LANES), lambda i: (tile(i), 0, 0))],
        out_shape=[jax.ShapeDtypeStruct((n, dm), F32), jax.ShapeDtypeStruct(((ntiles + 1) * nslots, width), BF16),
                   jax.ShapeDtypeStruct((n, LANES), F32), jax.ShapeDtypeStruct((ntiles, 8, LANES), F32)],
        scratch_shapes=[pltpu.VMEM((2, tm, LANES), F32)] * 4 + [pltpu.VMEM((tm, width), BF16)],
        compiler_params=_cparams(("arbitrary",)),
        name="outproj",
    )(*args)


def _wait_copies(count, copy):
    def wait_one(c, carry):
        copy.wait()
        return carry

    lax.fori_loop(0, count, wait_one, 0)


def _experts_kernel(first_ref, count_ref, chunk_ref, wg_ref, wu_ref, wd_ref, xs_hbm, ys_hbm,
                    xbuf, ybuf, zbuf, xsem, ysem, zsem, *, nblk):
    e = pl.program_id(0)
    last = pl.num_programs(0) - 1
    first = first_ref[e]
    count = count_ref[e]
    used = first_ref[last] + count_ref[last]
    dm = ybuf.shape[2]
    cpb = ROW_BLOCK // SEG_ROWS

    def chunk_rows(blk, j):
        return pl.ds(pl.multiple_of(chunk_ref[blk * cpb + j], SEG_ROWS), SEG_ROWS)

    def gather(blk):
        for j in range(cpb):
            pltpu.make_async_copy(xs_hbm.at[chunk_rows(blk, j), :], xbuf.at[blk & 1, pl.ds(j * SEG_ROWS, SEG_ROWS), :],
                                  xsem.at[blk & 1]).start(priority=1)

    def gather_done(blk):
        return pltpu.make_async_copy(xs_hbm.at[pl.ds(0, ROW_BLOCK), :], xbuf.at[blk & 1], xsem.at[blk & 1])

    def put(blk):
        rows = pl.ds(pl.multiple_of(blk * ROW_BLOCK, ROW_BLOCK), ROW_BLOCK)
        return pltpu.make_async_copy(ybuf.at[blk & 1], ys_hbm.at[rows, :], ysem.at[blk & 1])

    def zero_blk(b):
        rows = pl.ds(pl.multiple_of(b * ROW_BLOCK, ROW_BLOCK), ROW_BLOCK)
        return pltpu.make_async_copy(zbuf, ys_hbm.at[rows, :], zsem)

    @pl.when(e == 0)
    def _():
        zbuf[...] = jnp.zeros_like(zbuf)

        def start(b, carry):
            zero_blk(b).start()
            return carry

        lax.fori_loop(used, nblk, start, 0)

        @pl.when(used > 0)
        def _():
            gather(0)

    def body(b, carry):
        blk = first + b
        slot = blk & 1
        gather_done(blk).wait()

        @pl.when(blk + 1 < used)
        def _():
            gather(blk + 1)

        @pl.when(blk >= 2)
        def _():
            put(blk - 2).wait()

        xb = xbuf[slot, :, :dm]
        aux = xbuf[slot, :, dm:].astype(F32)
        w = jnp.where(aux[:, 4:5] == e.astype(F32), aux[:, 0:1] + aux[:, 1:2], aux[:, 2:3] + aux[:, 3:4])
        g = jnp.dot(xb, wg_ref[0], preferred_element_type=F32)
        u = jnp.dot(xb, wu_ref[0], preferred_element_type=F32)
        a = (g * (0.5 * jnp.tanh(0.5 * g) + 0.5)) * u
        y = jnp.dot(a.astype(BF16), wd_ref[0], preferred_element_type=F32)
        ybuf[slot] = (y * w).astype(BF16)
        put(blk).start()
        return carry

    lax.fori_loop(0, count, body, 0)

    @pl.when(e == last)
    def _():
        @pl.when(used >= 2)
        def _():
            put(used - 2).wait()

        @pl.when(used >= 1)
        def _():
            put(used - 1).wait()

        _wait_copies(nblk - used, zero_blk(0))


def _experts(first_blk, count_blk, chunk_tbl, x_sorted, w_gate, w_up, w_down, nblk):
    rows, width = x_sorted.shape
    n_exp, dm, de = w_gate.shape
    wmap = lambda e, *_: (e, 0, 0)
    grid_spec = pltpu.PrefetchScalarGridSpec(
        num_scalar_prefetch=3,
        grid=(n_exp,),
        in_specs=[pl.BlockSpec((1, dm, de), wmap),
                  pl.BlockSpec((1, dm, de), wmap),
                  pl.BlockSpec((1, de, dm), wmap),
                  pl.BlockSpec(memory_space=pl.ANY)],
        out_specs=pl.BlockSpec(memory_space=pl.ANY),
        scratch_shapes=[pltpu.VMEM((2, ROW_BLOCK, width), BF16), pltpu.VMEM((2, ROW_BLOCK, dm), BF16),
                        pltpu.VMEM((ROW_BLOCK, dm), BF16),
                        pltpu.SemaphoreType.DMA((2,)), pltpu.SemaphoreType.DMA((2,)), pltpu.SemaphoreType.DMA(())],
    )
    return pl.pallas_call(
        functools.partial(_experts_kernel, nblk=nblk),
        grid_spec=grid_spec,
        out_shape=jax.ShapeDtypeStruct((nblk * ROW_BLOCK, dm), BF16),
        compiler_params=_cparams(("arbitrary",)),
        name="experts",
    )(first_blk, count_blk, chunk_tbl, w_gate, w_up, w_down, x_sorted)


def _slot_parts(tm, nslots, group):
    return [(0, 2 * tm)] + [(lo, group) for lo in range(2 * tm, nslots, group)]


def _combine_kernel(src_ref, live_ref, route_ref, x1_ref, mod_ref, g_ref, y_ref, o_ref, ys_ref, sem,
                    *, tm, nslots, group):
    i = pl.program_id(0)
    last = pl.num_programs(0) - 1
    slot = i & 1
    nchk = nslots // SEG_ROWS
    parts = _slot_parts(tm, nslots, group)

    def for_live_parts(tile, fn):
        for p, (lo, rows) in enumerate(parts):
            if p == 0:
                fn(lo, rows)
            else:
                pl.when(live_ref[tile] > lo)(functools.partial(fn, lo, rows))

    def gather(tile, s):
        def start(lo, rows):
            for c in range(lo // SEG_ROWS, (lo + rows) // SEG_ROWS):
                src = pl.multiple_of(src_ref[tile * nchk + c], SEG_ROWS)
                pltpu.make_async_copy(y_ref.at[pl.ds(src, SEG_ROWS), :],
                                      ys_ref.at[s, pl.ds(c * SEG_ROWS, SEG_ROWS), :], sem.at[s]).start()

        for_live_parts(tile, start)

    def wait(tile, s):
        for_live_parts(tile, lambda lo, rows: pltpu.make_async_copy(
            y_ref.at[pl.ds(0, rows), :], ys_ref.at[s, pl.ds(lo, rows), :], sem.at[s]).wait())

    @pl.when(i == 0)
    def _():
        gather(0, 0)

    nxt = jnp.minimum(i + 1, last)
    gather(nxt, 1 - slot)
    wait(i, slot)
    route = route_ref[...]

    def finish(rows):
        scol = lax.broadcasted_iota(I32, (tm, rows), 1).astype(F32)
        pick = jnp.where((scol == route[:, 2:3]) | (scol == route[:, 3:4]), 1.0, 0.0).astype(BF16)
        moe = jnp.dot(pick, ys_ref[slot, :rows, :], preferred_element_type=F32)
        x = x1_ref[...] + mod_ref[0, 5:6, :] * moe
        ms = jnp.mean(x * x, axis=-1, keepdims=True)
        o_ref[...] = (x * lax.rsqrt(ms + RMS_EPS)) * g_ref[...]

    ends = [lo + rows for lo, rows in parts]
    for p, end in enumerate(ends):
        above = live_ref[i] > (ends[p - 1] if p else -1)
        cond = above if p == len(ends) - 1 else above & (live_ref[i] <= end)
        pl.when(cond)(functools.partial(finish, end))

    @pl.when(i == last)
    def _():
        wait(nxt, 1 - slot)


def _combine(src_tbl, live_tbl, route, x1, mod3, g_final, y_rows, seq, tm, nslots):
    n, dm = x1.shape
    tpb = seq // tm
    grid_spec = pltpu.PrefetchScalarGridSpec(
        num_scalar_prefetch=2,
        grid=(n // tm,),
        in_specs=[pl.BlockSpec((tm, LANES), lambda i, *_: (i, 0)),
                  pl.BlockSpec((tm, dm), lambda i, *_: (i, 0)),
                  pl.BlockSpec((1, 6, dm), lambda i, *_: (i // tpb, 0, 0)),
                  pl.BlockSpec((1, dm), lambda i, *_: (0, 0)),
                  pl.BlockSpec(memory_space=pl.ANY)],
        out_specs=pl.BlockSpec((tm, dm), lambda i, *_: (i, 0)),
        scratch_shapes=[pltpu.VMEM((2, nslots, dm), BF16), pltpu.SemaphoreType.DMA((2,))],
    )
    return pl.pallas_call(
        functools.partial(_combine_kernel, tm=tm, nslots=nslots, group=SLOT_GROUP),
        grid_spec=grid_spec,
        out_shape=jax.ShapeDtypeStruct((n, dm), F32),
        compiler_params=_cparams(("arbitrary",)),
        name="combine",
    )(src_tbl, live_tbl, route, x1, mod3, g_final, y_rows)


def _rope_angles(positions):
    half = HEAD_DIM // 2
    inv_freq = ROPE_THETA ** (-jnp.arange(half, dtype=F32) * (2.0 / HEAD_DIM))
    freq = jnp.tile(inv_freq, LANES // half)
    sign = jnp.tile(jnp.concatenate([-jnp.ones((half,), F32), jnp.ones((half,), F32)]), LANES // HEAD_DIM)
    ang = positions.astype(F32).reshape(-1, 1) * freq
    return ang, sign.reshape(1, LANES)


def _expert_chunk_table(lens, pstart, tot, pend, nblk, nslots):
    ntiles = lens.shape[0]
    cpb = ROW_BLOCK // SEG_ROWS
    runs = lens // SEG_ROWS
    q = jnp.arange(nblk * cpb, dtype=I32)
    owner = jnp.minimum(jnp.sum((pend[None, :] // SEG_ROWS <= q[:, None]).astype(I32), axis=1), N_EXPERTS - 1)
    is_e = owner[:, None] == jnp.arange(N_EXPERTS, dtype=I32)[None, :]
    pick_e = lambda v: jnp.sum(jnp.where(is_e, v[None, :], 0), axis=1)
    off = q - pick_e(pstart // SEG_ROWS)
    in_run = (off < pick_e(tot // SEG_ROWS)) & (q < pend[-1] // SEG_ROWS)
    upto = jnp.cumsum(runs, axis=0)
    upto_e = jnp.sum(jnp.where(is_e[:, None, :], upto[None, :, :], 0), axis=2)
    tile = jnp.minimum(jnp.sum((upto_e <= off[:, None]).astype(I32), axis=1), ntiles - 1)
    is_t = tile[:, None] == jnp.arange(ntiles, dtype=I32)[None, :]
    run_slot0 = jnp.cumsum(runs, axis=1) - runs - (upto - runs)
    slot0 = jnp.sum(jnp.where(is_t[:, :, None] & is_e[:, None, :], run_slot0[None, :, :], 0), axis=(1, 2))
    live_row = tile * nslots + (slot0 + off) * SEG_ROWS
    spare_row = ntiles * nslots + (((q // cpb) % 2) * cpb + q % cpb) * SEG_ROWS
    return jnp.where(in_run, live_row, spare_row)


def kernel(x, c, positions, w_ada, b_ada, g_mix, w_in, sink_logits, w_branch_a, w_branch_b, w_out, g_ffn,
           w_group, b_group, w_route, b_route, w_expert_gate, w_expert_up, w_expert_down, g_final):
    batch, seq, dm = x.shape
    n = batch * seq
    assert w_ada.shape[0] == 1, "one layer"
    x2 = x.reshape(n, dm)

    c8 = jnp.pad(c, ((0, 8 - batch), (0, 0)))
    mod = _ada(c8, w_ada[0], b_ada[0].reshape(1, -1))
    mod3 = mod[:batch].reshape(batch, 6, dm)

    angles, sign = _rope_angles(positions)
    outs = _inproj(x2, mod3, g_mix[0].reshape(1, dm), angles, sign, w_in[0].astype(BF16),
                   (w_expert_gate[0], w_expert_up[0], w_expert_down[0]), batch, seq)
    qkv = outs[:9]
    qb, kb, vb, sga, sgb = outs[9:14]
    expert_w_bf = outs[14:]

    dil_outs = [_dil_attention(qkv[3 * g], qkv[3 * g + 1], qkv[3 * g + 2]) for g in range(len(DILATIONS))]
    ob = _swa_attention(sink_logits[0], qb.reshape(batch, seq, -1), kb.reshape(batch, seq, -1),
                        vb.reshape(batch, seq, -1)).reshape(n, -1)

    pad = LANES - N_GROUPS - N_EXPERTS
    wr = jnp.concatenate([w_group[0], w_route[0], jnp.zeros((dm, pad), F32)], axis=1).astype(BF16)
    br = jnp.concatenate([b_group[0], b_route[0], jnp.zeros((pad,), F32)]).reshape(1, LANES)
    ntiles = n // MOE_TILE
    nslots = 2 * MOE_TILE + N_EXPERTS * SEG_ROWS
    x1, x_sorted, route, seg_lens = _outproj(dil_outs, ob, sga, sgb, x2, mod3, g_ffn[0].reshape(1, dm),
                                             w_branch_a[0].astype(BF16), w_branch_b[0].astype(BF16),
                                             w_out[0].astype(BF16), wr, br, batch, seq, nslots)

    nblk = -(-(2 * n + ntiles * N_EXPERTS * (SEG_ROWS - 1)) // ROW_BLOCK) + N_EXPERTS
    lens = seg_lens[:, 0, :N_EXPERTS].astype(I32)
    tot = jnp.sum(lens, axis=0)
    padded = (tot + ROW_BLOCK - 1) // ROW_BLOCK * ROW_BLOCK
    pend = jnp.cumsum(padded)
    pstart = pend - padded
    chunk_tbl = _expert_chunk_table(lens, pstart, tot, pend, nblk, nslots)
    live_tbl = jnp.sum(lens, axis=1)

    nchk = nslots // SEG_ROWS
    base = pstart[None, :] + jnp.cumsum(lens, axis=0) - lens
    run_end = jnp.cumsum(lens // SEG_ROWS, axis=1)
    chunk = jnp.arange(nchk, dtype=I32)
    owner = jnp.sum((run_end[:, None, :] <= chunk[None, :, None]).astype(I32), axis=2)
    is_owner = owner[:, :, None] == jnp.arange(N_EXPERTS, dtype=I32)[None, None, :]
    run_row0 = base - (run_end - lens // SEG_ROWS) * SEG_ROWS
    row = jnp.sum(jnp.where(is_owner, run_row0[:, None, :], 0), axis=2) + chunk[None, :] * SEG_ROWS
    src_tbl = jnp.where(owner < N_EXPERTS, row, chunk[None, :] * SEG_ROWS).reshape(-1)

    y_rows = _experts(pstart // ROW_BLOCK, padded // ROW_BLOCK, chunk_tbl, x_sorted, *expert_w_bf, nblk)
    out = _combine(src_tbl, live_tbl, route, x1, mod3, g_final.reshape(1, dm), y_rows, seq, MOE_TILE, nslots)
    return out.reshape(batch, seq, dm)
```

```python
import functools

import jax
import jax.numpy as jnp
from jax import lax
from jax.experimental import pallas as pl
from jax.experimental.pallas import tpu as pltpu

F32 = jnp.float32
BF16 = jnp.bfloat16
I32 = jnp.int32

HEAD_DIM = 64
ROPE_THETA = 10000.0
RMS_EPS = 1e-6
NEG_INF = -1e30
Q_SCALE = HEAD_DIM ** -0.5
DILATIONS = (1, 4, 16)
DIL_HALF_WINDOW = 64
DIL_GROUP_W = 256
SWA_WINDOW = 128
N_GROUPS = 4
EXPERTS_PER_GROUP = 8
N_EXPERTS = 32
LANES = 128
ROW_BLOCK = 512
SEG_ROWS = 16
MOE_TILE = 512
SLOT_GROUP = 256
VMEM_LIMIT = 56 * 1024 * 1024


def _cparams(sem):
    return pltpu.CompilerParams(dimension_semantics=sem, vmem_limit_bytes=VMEM_LIMIT)


def _ada_kernel(c_ref, w_ref, b_ref, win_ref, o_ref, winb_ref):
    c = c_ref[...]
    cs = c * jax.nn.sigmoid(c)
    o_ref[...] = jnp.dot(cs.astype(BF16), w_ref[...].astype(BF16), preferred_element_type=F32) + b_ref[...]
    winb_ref[...] = win_ref[...].astype(BF16)


def _ada(c8, w_ada, b_ada, w_in):
    d, n = w_ada.shape
    steps = 4
    tn, tw = n // steps, w_in.shape[1] // steps
    return pl.pallas_call(
        _ada_kernel,
        grid=(steps,),
        in_specs=[pl.BlockSpec((8, d), lambda j: (0, 0)),
                  pl.BlockSpec((d, tn), lambda j: (0, j)),
                  pl.BlockSpec((1, tn), lambda j: (0, j)),
                  pl.BlockSpec((w_in.shape[0], tw), lambda j: (0, j))],
        out_specs=[pl.BlockSpec((8, tn), lambda j: (0, j)),
                   pl.BlockSpec((w_in.shape[0], tw), lambda j: (0, j))],
        out_shape=[jax.ShapeDtypeStruct((8, n), F32), jax.ShapeDtypeStruct(w_in.shape, BF16)],
        compiler_params=_cparams(("arbitrary",)),
        name="ada",
    )(c8, w_ada, b_ada, w_in)


def _rms_mod(x, g, shift, scale):
    ms = jnp.mean(x * x, axis=-1, keepdims=True)
    return (x * lax.rsqrt(ms + RMS_EPS)) * (g * (1.0 + scale)) + shift


def _inproj_kernel(x_ref, mod_ref, g_ref, ang_ref, sign_ref, w_ref, eg_ref, eu_ref, ed_ref,
                   q0_ref, k0_ref, v0_ref, q1_ref, k1_ref, v1_ref, q2_ref, k2_ref, v2_ref,
                   qb_ref, kb_ref, vb_ref, sga_ref, sgb_ref, egb_ref, eub_ref, edb_ref, stg_ref, *, tm, n_exp):
    def cast_expert():
        egb_ref[...] = eg_ref[...].astype(BF16)
        eub_ref[...] = eu_ref[...].astype(BF16)
        edb_ref[...] = ed_ref[...].astype(BF16)

    if n_exp is None:
        cast_expert()
    else:
        pl.when(pl.program_id(0) < n_exp)(cast_expert)

    h = _rms_mod(x_ref[...], g_ref[...], mod_ref[0, 0:1, :], mod_ref[0, 1:2, :])
    hb = h.astype(BF16)
    ang = ang_ref[...]
    cos = jnp.cos(ang)
    sin = jnp.sin(ang) * sign_ref[...]
    lane = lax.broadcasted_iota(I32, (tm, LANES), 1)
    first_half = (lane & 32) == 0
    low = lane < 64

    def proj(c0, width):
        return jnp.dot(hb, w_ref[:, c0:c0 + width], preferred_element_type=F32)

    def rope(t):
        rot = jnp.where(first_half, pltpu.roll(t, 96, 1), pltpu.roll(t, 32, 1))
        return t * cos + rot * sin

    def rope256(p):
        return jnp.concatenate([rope(p[:, :LANES]), rope(p[:, LANES:])], axis=1)

    def store_group(ref, val, d):
        if d == 1:
            ref[0, 0] = val.astype(BF16)
        else:
            for c in range(2):
                stg_ref[c] = val[:, c * LANES:(c + 1) * LANES]
            for r in range(d):
                for c in range(2):
                    ref[0, r, :, c * LANES:(c + 1) * LANES] = (
                        stg_ref[c, pl.ds(r, tm // d, stride=d), :].astype(BF16))

    q_refs = (q0_ref, q1_ref, q2_ref)
    k_refs = (k0_ref, k1_ref, k2_ref)
    v_refs = (v0_ref, v1_ref, v2_ref)
    for g, d in enumerate(DILATIONS):
        store_group(q_refs[g], rope256(proj(g * 256, 256)) * Q_SCALE, d)
        store_group(k_refs[g], rope256(proj(768 + g * 256, 256)), d)
        store_group(v_refs[g], proj(1536 + g * 256, 256), d)
    for j in range(2):
        qb_ref[:, j * 256:(j + 1) * 256] = (rope256(proj(2304 + j * 256, 256)) * Q_SCALE).astype(BF16)
    kv = proj(2816, 256)
    kb = rope(kv[:, :LANES])
    vb = kv[:, LANES:]
    kb_sw = pltpu.roll(kb, 64, 1)
    vb_sw = pltpu.roll(vb, 64, 1)
    kb_ref[:, :LANES] = jnp.where(low, kb, kb_sw).astype(BF16)
    kb_ref[:, LANES:] = jnp.where(low, kb_sw, kb).astype(BF16)
    vb_ref[:, :LANES] = jnp.where(low, vb, vb_sw).astype(BF16)
    vb_ref[:, LANES:] = jnp.where(low, vb_sw, vb).astype(BF16)
    for j in range(4):
        sig = lambda t: 0.5 * jnp.tanh(0.5 * t) + 0.5
        sga_ref[:, j * 256:(j + 1) * 256] = sig(proj(3072 + j * 256, 256)).astype(BF16)
        sgb_ref[:, j * 256:(j + 1) * 256] = sig(proj(4096 + j * 256, 256)).astype(BF16)


def _inproj(x2, mod3, g_mix, angles, sign, w_in_bf, expert_w, batch, seq):
    n, dm = x2.shape
    tm = 512
    tpb = seq // tm
    grid = (n // tm,)
    row = lambda i: (i, 0)
    strided_specs, strided_shapes = [], []
    for d in DILATIONS:
        for _ in range(3):
            strided_specs.append(pl.BlockSpec((1, d, tm // d, DIL_GROUP_W), lambda i: (i // tpb, 0, i % tpb, 0)))
            strided_shapes.append(jax.ShapeDtypeStruct((batch, d, seq // d, DIL_GROUP_W), BF16))
    out_specs = strided_specs + [
        pl.BlockSpec((tm, 512), row), pl.BlockSpec((tm, 256), row), pl.BlockSpec((tm, 256), row),
        pl.BlockSpec((tm, dm), row), pl.BlockSpec((tm, dm), row)]
    out_shapes = strided_shapes + [
        jax.ShapeDtypeStruct((n, 512), BF16), jax.ShapeDtypeStruct((n, 256), BF16),
        jax.ShapeDtypeStruct((n, 256), BF16), jax.ShapeDtypeStruct((n, dm), BF16),
        jax.ShapeDtypeStruct((n, dm), BF16)]
    n_exp = expert_w[0].shape[0]
    assert n // tm >= n_exp, "one expert's weights are cast per grid step"
    emap = lambda i: (jnp.minimum(i, n_exp - 1), 0, 0)
    expert_specs = [pl.BlockSpec((1,) + w.shape[1:], emap) for w in expert_w]
    return pl.pallas_call(
        functools.partial(_inproj_kernel, tm=tm, n_exp=None if n // tm == n_exp else n_exp),
        grid=grid,
        in_specs=[pl.BlockSpec((tm, dm), row),
                  pl.BlockSpec((1, 6, dm), lambda i: (i // tpb, 0, 0)),
                  pl.BlockSpec((1, dm), lambda i: (0, 0)),
                  pl.BlockSpec((tm, LANES), row),
                  pl.BlockSpec((1, LANES), lambda i: (0, 0)),
                  pl.BlockSpec(w_in_bf.shape, lambda i: (0, 0), pipeline_mode=pl.Buffered(1))] + expert_specs,
        out_specs=out_specs + expert_specs,
        out_shape=out_shapes + [jax.ShapeDtypeStruct(w.shape, BF16) for w in expert_w],
        scratch_shapes=[pltpu.VMEM((2, tm, LANES), F32)],
        compiler_params=_cparams(("arbitrary",)),
        name="inproj",
    )(x2, mod3, g_mix, angles, sign, w_in_bf, *expert_w)


def _split_heads(q2, low):
    zero = jnp.zeros_like(q2)
    return jnp.concatenate([jnp.where(low, q2, zero), jnp.where(low, zero, q2)], axis=0)


def _band_softmax(qst, k2, v2, bias, sinks):
    s = lax.dot_general(qst, k2, (((1,), (1,)), ((), ())), preferred_element_type=F32)
    s = s + bias
    rows, tk = s.shape
    m = jnp.max(s, axis=-1, keepdims=True)
    if sinks is not None:
        seg = rows // len(sinks)
        m = jnp.concatenate([jnp.maximum(m[h * seg:(h + 1) * seg], sk) for h, sk in enumerate(sinks)], axis=0)
    m = jnp.broadcast_to(m, (rows, LANES))
    e = jnp.concatenate([jnp.exp(s[:, c * LANES:(c + 1) * LANES] - m) for c in range(tk // LANES)], axis=1)
    v_ones = jnp.concatenate([v2, jnp.ones((tk, LANES), BF16)], axis=1)
    od = jnp.dot(e.astype(BF16), v_ones, preferred_element_type=F32)
    o, den = od[:, :LANES], od[:, LANES:]
    if sinks is not None:
        den = jnp.concatenate([den[h * seg:(h + 1) * seg] + jnp.exp(sk - m[h * seg:(h + 1) * seg])
                               for h, sk in enumerate(sinks)], axis=0)
    return o / den, m, den


def _fill_band_bias(bias_ref, tq, window):
    rows, tk = bias_ref.shape[1:]
    row = lax.broadcasted_iota(I32, (rows, tk), 0) & (tq - 1)
    col = lax.broadcasted_iota(I32, (rows, tk), 1)
    for i in range(bias_ref.shape[0]):
        bias_ref[i] = jnp.where(jnp.abs(col - row - i * window) <= window, 0.0, NEG_INF)


def _dil_kernel(q_ref, k_ref, v_ref, o_ref, l_ref, bias_ref, *, length, tq, tk):
    low = lax.broadcasted_iota(I32, (tq, LANES), 1) < 64
    _fill_band_bias(bias_ref, tq, DIL_HALF_WINDOW)
    nq = length // tq

    def body(j, carry):
        r = j // nq
        qs = pl.multiple_of((j % nq) * tq, tq)
        ks = pl.multiple_of(jnp.clip(qs - DIL_HALF_WINDOW, 0, length - tk), DIL_HALF_WINDOW)
        which = (qs - ks) // DIL_HALF_WINDOW
        for c in range(DIL_GROUP_W // LANES):
            cs = slice(c * LANES, (c + 1) * LANES)
            qst = _split_heads(q_ref[r, pl.ds(qs, tq), cs], low)
            o, m, den = _band_softmax(qst, k_ref[r, pl.ds(ks, tk), cs], v_ref[r, pl.ds(ks, tk), cs],
                                      bias_ref[which], None)
            lse = m + jnp.log(den)
            o_ref[r, pl.ds(qs, tq), cs] = jnp.where(low, o[:tq], o[tq:]).astype(BF16)
            l_ref[r, pl.ds(qs, tq), cs] = jnp.where(low, lse[:tq], lse[tq:])
        return carry

    lax.fori_loop(0, q_ref.shape[0] * nq, body, 0, unroll=32)


def _dil_attention(q, k, v):
    batch, d, length, w = q.shape
    tq, tk = 128, 256
    spec = pl.BlockSpec((None, d, length, w), lambda b: (b, 0, 0, 0))
    return pl.pallas_call(
        functools.partial(_dil_kernel, length=length, tq=tq, tk=tk),
        grid=(batch,),
        in_specs=[spec, spec, spec],
        out_specs=[spec, spec],
        out_shape=[jax.ShapeDtypeStruct(q.shape, BF16), jax.ShapeDtypeStruct(q.shape, F32)],
        scratch_shapes=[pltpu.VMEM((3, 2 * tq, tk), F32)],
        compiler_params=_cparams(("arbitrary",)),
        name=f"dil{d}",
    )(q, k, v)


def _swa_kernel(sink_ref, q_ref, k_ref, v_ref, o_ref, bias_ref, *, length, tq, tk):
    low = lax.broadcasted_iota(I32, (tq, LANES), 1) < 64
    nblk = q_ref.shape[1] // LANES
    _fill_band_bias(bias_ref, tq, SWA_WINDOW)

    def body(j, carry):
        qs = pl.multiple_of(j * tq, tq)
        ks = pl.multiple_of(jnp.clip(qs - SWA_WINDOW, 0, length - tk), SWA_WINDOW)
        bias = bias_ref[(qs - ks) // SWA_WINDOW]
        for b in range(nblk):
            cs = slice((b // 2) * LANES, (b // 2 + 1) * LANES)
            bs = slice(b * LANES, (b + 1) * LANES)
            qst = _split_heads(q_ref[pl.ds(qs, tq), bs], low)
            sinks = (sink_ref[2 * b], sink_ref[2 * b + 1])
            o, _, _ = _band_softmax(qst, k_ref[pl.ds(ks, tk), cs], v_ref[pl.ds(ks, tk), cs], bias, sinks)
            o_ref[pl.ds(qs, tq), bs] = jnp.where(low, o[:tq], o[tq:]).astype(BF16)
        return carry

    lax.fori_loop(0, length // tq, body, 0, unroll=8)


def _swa_attention(sink, q, k, v):
    batch, length, qw = q.shape
    tq, tk = 128, 384
    return pl.pallas_call(
        functools.partial(_swa_kernel, length=length, tq=tq, tk=tk),
        grid=(batch,),
        in_specs=[pl.BlockSpec(memory_space=pltpu.SMEM),
                  pl.BlockSpec((None, length, qw), lambda b: (b, 0, 0)),
                  pl.BlockSpec((None, length, k.shape[2]), lambda b: (b, 0, 0)),
                  pl.BlockSpec((None, length, v.shape[2]), lambda b: (b, 0, 0))],
        out_specs=pl.BlockSpec((None, length, qw), lambda b: (b, 0, 0)),
        out_shape=jax.ShapeDtypeStruct(q.shape, BF16),
        scratch_shapes=[pltpu.VMEM((3, 2 * tq, tk), F32)],
        compiler_params=_cparams(("arbitrary",)),
        name="swa",
    )(sink, q, k, v)


def _route_rows(logits):
    lane = lax.broadcasted_iota(I32, logits.shape, 1).astype(F32)
    big = 1e9
    is_g = lane < N_GROUPS
    gl = jnp.where(is_g, logits, NEG_INF)
    gmax = jnp.max(gl, axis=-1, keepdims=True)
    gsel = jnp.min(jnp.where(is_g & (gl == gmax), lane, big), axis=-1, keepdims=True)
    gw = 1.0 / jnp.sum(jnp.where(is_g, jnp.exp(gl - gmax), 0.0), axis=-1, keepdims=True)
    e_lo = N_GROUPS + gsel * EXPERTS_PER_GROUP
    in_grp = (lane >= e_lo) & (lane < e_lo + EXPERTS_PER_GROUP)
    el = jnp.where(in_grp, logits, NEG_INF)
    m1 = jnp.max(el, axis=-1, keepdims=True)
    i1 = jnp.min(jnp.where(in_grp & (el == m1), lane, big), axis=-1, keepdims=True)
    el2 = jnp.where(lane == i1, NEG_INF, el)
    m2 = jnp.max(el2, axis=-1, keepdims=True)
    i2 = jnp.min(jnp.where(in_grp & (lane != i1) & (el2 == m2), lane, big), axis=-1, keepdims=True)
    t = jnp.exp(m2 - m1)
    tw1 = gw / (1.0 + t)
    tw2 = gw * t / (1.0 + t)
    out = jnp.where(lane == 0, tw1, 0.0)
    out = jnp.where(lane == 1, tw2, out)
    out = jnp.where(lane == 2, i1 - N_GROUPS, out)
    return jnp.where(lane == 3, i2 - N_GROUPS, out)


def _outproj_kernel(o0_ref, l0_ref, o1_ref, l1_ref, o2_ref, l2_ref, ob_ref, sga_ref, sgb_ref, x_ref,
                    mod_ref, g_ref, wa_ref, wb_ref, wo_ref, wr_ref, br_ref,
                    x1_ref, xs_ref, route_ref, len_ref,
                    so1_ref, sl1_ref, so2_ref, sl2_ref, h2_ref, *, tm, sub, group):
    ntiles = pl.num_programs(0) - 1

    @pl.when(pl.program_id(0) == ntiles)
    def _():
        xs_ref[...] = jnp.zeros_like(xs_ref)

    pl.when(pl.program_id(0) < ntiles)(functools.partial(
        _outproj_tile, o0_ref, l0_ref, o1_ref, l1_ref, o2_ref, l2_ref, ob_ref, sga_ref, sgb_ref, x_ref,
        mod_ref, g_ref, wa_ref, wb_ref, wo_ref, wr_ref, br_ref, x1_ref, xs_ref, route_ref, len_ref,
        so1_ref, sl1_ref, so2_ref, sl2_ref, h2_ref, tm=tm, sub=sub, group=group))


def _outproj_tile(o0_ref, l0_ref, o1_ref, l1_ref, o2_ref, l2_ref, ob_ref, sga_ref, sgb_ref, x_ref,
                  mod_ref, g_ref, wa_ref, wb_ref, wo_ref, wr_ref, br_ref,
                  x1_ref, xs_ref, route_ref, len_ref,
                  so1_ref, sl1_ref, so2_ref, sl2_ref, h2_ref, *, tm, sub, group):
    dm = x_ref.shape[1]
    for (o_ref, l_ref, so_ref, sl_ref, d) in ((o1_ref, l1_ref, so1_ref, sl1_ref, DILATIONS[1]),
                                              (o2_ref, l2_ref, so2_ref, sl2_ref, DILATIONS[2])):
        for r in range(d):
            for c in range(2):
                cs = slice(c * LANES, (c + 1) * LANES)
                so_ref[c, pl.ds(r, tm // d, stride=d), :] = o_ref[0, r, :, cs].astype(F32)
                sl_ref[c, pl.ds(r, tm // d, stride=d), :] = l_ref[0, r, :, cs]
    mr = lax.broadcasted_iota(I32, (LANES, LANES), 0)
    mc = lax.broadcasted_iota(I32, (LANES, LANES), 1)
    move_hi = jnp.where(((mr < 2) & (mc == 2 * mr)) | ((mr >= 2) & (mr < 4) & (mc == mr + 2)), 1.0, 0.0).astype(BF16)
    move_lo = jnp.where((mr < 2) & (mc == 2 * mr + 1), 1.0, 0.0).astype(BF16)
    for t in range(tm // sub):
        rs = slice(t * sub, (t + 1) * sub)
        both = lambda ref: jnp.concatenate([ref[0, rs, :], ref[1, rs, :]], axis=1)
        o0, l0 = o0_ref[0, 0, rs, :].astype(F32), l0_ref[0, 0, rs, :]
        o1, l1, o2, l2 = both(so1_ref), both(sl1_ref), both(so2_ref), both(sl2_ref)
        mx = jnp.maximum(jnp.maximum(l0, l1), l2)
        w0, w1, w2 = jnp.exp(l0 - mx), jnp.exp(l1 - mx), jnp.exp(l2 - mx)
        o_a = (w0 * o0 + w1 * o1 + w2 * o2) / (w0 + w1 + w2)
        y_a = jnp.dot(o_a.astype(BF16), wa_ref[...], preferred_element_type=F32)
        y_b = jnp.dot(ob_ref[rs, :], wb_ref[...], preferred_element_type=F32)
        merged = sga_ref[rs, :].astype(F32) * y_a + sgb_ref[rs, :].astype(F32) * y_b
        mix = jnp.dot(merged.astype(BF16), wo_ref[...], preferred_element_type=F32)
        x1 = x_ref[rs, :] + mod_ref[0, 2:3, :] * mix
        x1_ref[rs, :] = x1
        h2 = _rms_mod(x1, g_ref[...], mod_ref[0, 3:4, :], mod_ref[0, 4:5, :]).astype(BF16)
        h2_ref[rs, :dm] = h2
        logits = jnp.dot(h2, wr_ref[...], preferred_element_type=F32) + br_ref[...]
        rt = _route_rows(logits)
        route_ref[rs, :] = rt
        hi = rt.astype(BF16)
        lo = (rt - hi.astype(F32)).astype(BF16)
        aux = (jnp.dot(hi, move_hi, preferred_element_type=F32) + jnp.dot(lo, move_lo, preferred_element_type=F32))
        h2_ref[rs, dm:] = aux.astype(BF16)

    part = route_ref[...]
    e1, e2 = part[:, 2:3], part[:, 3:4]
    lane = lax.broadcasted_iota(I32, (tm, LANES), 1).astype(F32)
    onehot = jnp.where((lane == e1) | (lane == e2), 1.0, 0.0)
    rr = lax.broadcasted_iota(I32, (tm, tm), 0)
    cc = lax.broadcasted_iota(I32, (tm, tm), 1)
    tri = jnp.where(rr > cc, 1.0, 0.0).astype(BF16)
    prefix = jnp.dot(tri, onehot.astype(BF16), preferred_element_type=F32)
    cnt = jnp.sum(onehot, axis=0, keepdims=True)
    seg_len = jnp.ceil(cnt * (1.0 / SEG_ROWS)) * SEG_ROWS
    ur = lax.broadcasted_iota(I32, (LANES, LANES), 0)
    uc = lax.broadcasted_iota(I32, (LANES, LANES), 1)
    upper = jnp.where(ur < uc, 1.0, 0.0).astype(BF16)
    seg_off = jnp.dot(jnp.broadcast_to(seg_len, (8, LANES)).astype(BF16), upper, preferred_element_type=F32)[0:1, :]
    slot_map = seg_off + prefix
    s1 = jnp.sum(jnp.where(lane == e1, slot_map, 0.0), axis=-1, keepdims=True)
    s2 = jnp.sum(jnp.where(lane == e2, slot_map, 0.0), axis=-1, keepdims=True)
    len_ref[0] = jnp.broadcast_to(seg_len, (8, LANES))
    route = jnp.where(lane == 2, s1, jnp.where(lane == 3, s2, part))
    route_ref[...] = route

    pr = lax.broadcasted_iota(I32, (8, LANES), 0)
    pc = lax.broadcasted_iota(I32, (8, LANES), 1)
    lane_pick = jnp.where(pc == pr + 2, 1.0, 0.0)
    slots_t = lax.dot_general(lane_pick, route, (((1,), (1,)), ((), ())), preferred_element_type=F32,
                              precision=lax.Precision.HIGHEST)
    live = jnp.sum(seg_len).astype(I32)

    def sort_rows(lo, rows):
        srow = (lax.broadcasted_iota(I32, (rows, tm), 0) + lo).astype(F32)
        pick = jnp.where((srow == slots_t[0:1, :]) | (srow == slots_t[1:2, :]), 1.0, 0.0).astype(BF16)
        xs_ref[lo:lo + rows, :] = jnp.dot(pick, h2_ref[...], preferred_element_type=F32).astype(BF16)

    def clear_rows(lo, rows):
        xs_ref[lo:lo + rows, :] = jnp.zeros((rows, xs_ref.shape[1]), BF16)

    head = xs_ref.shape[0] - group
    sort_rows(0, head)
    pl.when(live > head)(functools.partial(sort_rows, head, group))
    pl.when(live <= head)(functools.partial(clear_rows, head, group))


def _outproj(dil_outs, ob, sga, sgb, x2, mod3, g_ffn, wa, wb, wo, wr, br, batch, seq, nslots):
    n, dm = x2.shape
    tm = MOE_TILE
    tpb = seq // tm
    ntiles = n // tm
    tile = lambda i: jnp.minimum(i, ntiles - 1)
    row = lambda i: (tile(i), 0)
    const = lambda i: (0, 0)
    in_specs = []
    args = []
    for (o, l), d in zip(dil_outs, DILATIONS):
        spec = pl.BlockSpec((1, d, tm // d, DIL_GROUP_W), lambda i: (tile(i) // tpb, 0, tile(i) % tpb, 0))
        in_specs += [spec, spec]
        args += [o, l]
    in_specs += [pl.BlockSpec((tm, ob.shape[1]), row), pl.BlockSpec((tm, dm), row), pl.BlockSpec((tm, dm), row),
                 pl.BlockSpec((tm, dm), row),
                 pl.BlockSpec((1, 6, dm), lambda i: (tile(i) // tpb, 0, 0)),
                 pl.BlockSpec((1, dm), const),
                 pl.BlockSpec(wa.shape, const), pl.BlockSpec(wb.shape, const), pl.BlockSpec(wo.shape, const),
                 pl.BlockSpec(wr.shape, const), pl.BlockSpec(br.shape, const)]
    args += [ob, sga, sgb, x2, mod3, g_ffn, wa, wb, wo, wr, br]
    width = dm + LANES
    return pl.pallas_call(
        functools.partial(_outproj_kernel, tm=tm, sub=512, group=SLOT_GROUP),
        grid=(ntiles + 1,),
        in_specs=in_specs,
        out_specs=[pl.BlockSpec((tm, dm), row), pl.BlockSpec((nslots, width), lambda i: (i, 0)),
                   pl.BlockSpec((tm, LANES), row), pl.BlockSpec((1, 8, LANES), lambda i: (tile(i), 0, 0))],
        out_shape=[jax.ShapeDtypeStruct((n, dm), F32), jax.ShapeDtypeStruct(((ntiles + 1) * nslots, width), BF16),
                   jax.ShapeDtypeStruct((n, LANES), F32), jax.ShapeDtypeStruct((ntiles, 8, LANES), F32)],
        scratch_shapes=[pltpu.VMEM((2, tm, LANES), F32)] * 4 + [pltpu.VMEM((tm, width), BF16)],
        compiler_params=_cparams(("arbitrary",)),
        name="outproj",
    )(*args)


def _wait_copies(count, copy):
    def wait_one(c, carry):
        copy.wait()
        return carry

    lax.fori_loop(0, count, wait_one, 0)


def _experts_kernel(first_ref, count_ref, chunk_ref, wg_ref, wu_ref, wd_ref, xs_hbm, ys_hbm,
                    xbuf, ybuf, zbuf, xsem, ysem, zsem, *, nblk):
    e = pl.program_id(0)
    last = pl.num_programs(0) - 1
    first = first_ref[e]
    count = count_ref[e]
    used = first_ref[last] + count_ref[last]
    dm = ybuf.shape[2]
    cpb = ROW_BLOCK // SEG_ROWS

    def chunk_rows(blk, j):
        return pl.ds(pl.multiple_of(chunk_ref[blk * cpb + j], SEG_ROWS), SEG_ROWS)

    def gather(blk):
        for j in range(cpb):
            pltpu.make_async_copy(xs_hbm.at[chunk_rows(blk, j), :], xbuf.at[blk & 1, pl.ds(j * SEG_ROWS, SEG_ROWS), :],
                                  xsem.at[blk & 1]).start(priority=1)

    def gather_done(blk):
        return pltpu.make_async_copy(xs_hbm.at[pl.ds(0, ROW_BLOCK), :], xbuf.at[blk & 1], xsem.at[blk & 1])

    def put(blk):
        rows = pl.ds(pl.multiple_of(blk * ROW_BLOCK, ROW_BLOCK), ROW_BLOCK)
        return pltpu.make_async_copy(ybuf.at[blk & 1], ys_hbm.at[rows, :], ysem.at[blk & 1])

    def zero_blk(b):
        rows = pl.ds(pl.multiple_of(b * ROW_BLOCK, ROW_BLOCK), ROW_BLOCK)
        return pltpu.make_async_copy(zbuf, ys_hbm.at[rows, :], zsem)

    @pl.when(e == 0)
    def _():
        zbuf[...] = jnp.zeros_like(zbuf)

        def start(b, carry):
            zero_blk(b).start()
            return carry

        lax.fori_loop(used, nblk, start, 0)

        @pl.when(used > 0)
        def _():
            gather(0)

    def body(b, carry):
        blk = first + b
        slot = blk & 1
        gather_done(blk).wait()

        @pl.when(blk + 1 < used)
        def _():
            gather(blk + 1)

        @pl.when(blk >= 2)
        def _():
            put(blk - 2).wait()

        xb = xbuf[slot, :, :dm]
        aux = xbuf[slot, :, dm:].astype(F32)
        w = jnp.where(aux[:, 4:5] == e.astype(F32), aux[:, 0:1] + aux[:, 1:2], aux[:, 2:3] + aux[:, 3:4])
        g = jnp.dot(xb, wg_ref[0], preferred_element_type=F32)
        u = jnp.dot(xb, wu_ref[0], preferred_element_type=F32)
        a = (g * (0.5 * jnp.tanh(0.5 * g) + 0.5)) * u
        y = jnp.dot(a.astype(BF16), wd_ref[0], preferred_element_type=F32)
        ybuf[slot] = (y * w).astype(BF16)
        put(blk).start()
        return carry

    lax.fori_loop(0, count, body, 0)

    @pl.when(e == last)
    def _():
        @pl.when(used >= 2)
        def _():
            put(used - 2).wait()

        @pl.when(used >= 1)
        def _():
            put(used - 1).wait()

        _wait_copies(nblk - used, zero_blk(0))


def _experts(first_blk, count_blk, chunk_tbl, x_sorted, w_gate, w_up, w_down, nblk):
    rows, width = x_sorted.shape
    n_exp, dm, de = w_gate.shape
    wmap = lambda e, *_: (e, 0, 0)
    grid_spec = pltpu.PrefetchScalarGridSpec(
        num_scalar_prefetch=3,
        grid=(n_exp,),
        in_specs=[pl.BlockSpec((1, dm, de), wmap),
                  pl.BlockSpec((1, dm, de), wmap),
                  pl.BlockSpec((1, de, dm), wmap),
                  pl.BlockSpec(memory_space=pl.ANY)],
        out_specs=pl.BlockSpec(memory_space=pl.ANY),
        scratch_shapes=[pltpu.VMEM((2, ROW_BLOCK, width), BF16), pltpu.VMEM((2, ROW_BLOCK, dm), BF16),
                        pltpu.VMEM((ROW_BLOCK, dm), BF16),
                        pltpu.SemaphoreType.DMA((2,)), pltpu.SemaphoreType.DMA((2,)), pltpu.SemaphoreType.DMA(())],
    )
    return pl.pallas_call(
        functools.partial(_experts_kernel, nblk=nblk),
        grid_spec=grid_spec,
        out_shape=jax.ShapeDtypeStruct((nblk * ROW_BLOCK, dm), BF16),
        compiler_params=_cparams(("arbitrary",)),
        name="experts",
    )(first_blk, count_blk, chunk_tbl, w_gate, w_up, w_down, x_sorted)


def _slot_parts(tm, nslots, group):
    return [(0, 2 * tm)] + [(lo, group) for lo in range(2 * tm, nslots, group)]


def _combine_kernel(src_ref, live_ref, route_ref, x1_ref, mod_ref, g_ref, y_ref, o_ref, ys_ref, sem,
                    *, tm, nslots, group):
    i = pl.program_id(0)
    last = pl.num_programs(0) - 1
    slot = i & 1
    nchk = nslots // SEG_ROWS
    parts = _slot_parts(tm, nslots, group)

    def for_live_parts(tile, fn):
        for p, (lo, rows) in enumerate(parts):
            if p == 0:
                fn(lo, rows)
            else:
                pl.when(live_ref[tile] > lo)(functools.partial(fn, lo, rows))

    def gather(tile, s):
        def start(lo, rows):
            for c in range(lo // SEG_ROWS, (lo + rows) // SEG_ROWS):
                src = pl.multiple_of(src_ref[tile * nchk + c], SEG_ROWS)
                pltpu.make_async_copy(y_ref.at[pl.ds(src, SEG_ROWS), :],
                                      ys_ref.at[s, pl.ds(c * SEG_ROWS, SEG_ROWS), :], sem.at[s]).start()

        for_live_parts(tile, start)

    def wait(tile, s):
        for_live_parts(tile, lambda lo, rows: pltpu.make_async_copy(
            y_ref.at[pl.ds(0, rows), :], ys_ref.at[s, pl.ds(lo, rows), :], sem.at[s]).wait())

    @pl.when(i == 0)
    def _():
        gather(0, 0)

    nxt = jnp.minimum(i + 1, last)
    gather(nxt, 1 - slot)
    wait(i, slot)
    route = route_ref[...]

    def finish(rows):
        scol = lax.broadcasted_iota(I32, (tm, rows), 1).astype(F32)
        pick = jnp.where((scol == route[:, 2:3]) | (scol == route[:, 3:4]), 1.0, 0.0).astype(BF16)
        moe = jnp.dot(pick, ys_ref[slot, :rows, :], preferred_element_type=F32)
        x = x1_ref[...] + mod_ref[0, 5:6, :] * moe
        ms = jnp.mean(x * x, axis=-1, keepdims=True)
        o_ref[...] = (x * lax.rsqrt(ms + RMS_EPS)) * g_ref[...]

    ends = [lo + rows for lo, rows in parts]
    for p, end in enumerate(ends):
        above = live_ref[i] > (ends[p - 1] if p else -1)
        cond = above if p == len(ends) - 1 else above & (live_ref[i] <= end)
        pl.when(cond)(functools.partial(finish, end))

    @pl.when(i == last)
    def _():
        wait(nxt, 1 - slot)


def _combine(src_tbl, live_tbl, route, x1, mod3, g_final, y_rows, seq, tm, nslots):
    n, dm = x1.shape
    tpb = seq // tm
    grid_spec = pltpu.PrefetchScalarGridSpec(
        num_scalar_prefetch=2,
        grid=(n // tm,),
        in_specs=[pl.BlockSpec((tm, LANES), lambda i, *_: (i, 0)),
                  pl.BlockSpec((tm, dm), lambda i, *_: (i, 0)),
                  pl.BlockSpec((1, 6, dm), lambda i, *_: (i // tpb, 0, 0)),
                  pl.BlockSpec((1, dm), lambda i, *_: (0, 0)),
                  pl.BlockSpec(memory_space=pl.ANY)],
        out_specs=pl.BlockSpec((tm, dm), lambda i, *_: (i, 0)),
        scratch_shapes=[pltpu.VMEM((2, nslots, dm), BF16), pltpu.SemaphoreType.DMA((2,))],
    )
    return pl.pallas_call(
        functools.partial(_combine_kernel, tm=tm, nslots=nslots, group=SLOT_GROUP),
        grid_spec=grid_spec,
        out_shape=jax.ShapeDtypeStruct((n, dm), F32),
        compiler_params=_cparams(("arbitrary",)),
        name="combine",
    )(src_tbl, live_tbl, route, x1, mod3, g_final, y_rows)


def _rope_angles(positions):
    half = HEAD_DIM // 2
    inv_freq = ROPE_THETA ** (-jnp.arange(half, dtype=F32) * (2.0 / HEAD_DIM))
    freq = jnp.tile(inv_freq, LANES // half)
    sign = jnp.tile(jnp.concatenate([-jnp.ones((half,), F32), jnp.ones((half,), F32)]), LANES // HEAD_DIM)
    ang = positions.astype(F32).reshape(-1, 1) * freq
    return ang, sign.reshape(1, LANES)


def _expert_chunk_table(lens, pstart, tot, pend, nblk, nslots):
    ntiles = lens.shape[0]
    cpb = ROW_BLOCK // SEG_ROWS
    runs = lens // SEG_ROWS
    q = jnp.arange(nblk * cpb, dtype=I32)
    owner = jnp.minimum(jnp.sum((pend[None, :] // SEG_ROWS <= q[:, None]).astype(I32), axis=1), N_EXPERTS - 1)
    is_e = owner[:, None] == jnp.arange(N_EXPERTS, dtype=I32)[None, :]
    pick_e = lambda v: jnp.sum(jnp.where(is_e, v[None, :], 0), axis=1)
    off = q - pick_e(pstart // SEG_ROWS)
    in_run = (off < pick_e(tot // SEG_ROWS)) & (q < pend[-1] // SEG_ROWS)
    upto = jnp.cumsum(runs, axis=0)
    upto_e = jnp.sum(jnp.where(is_e[:, None, :], upto[None, :, :], 0), axis=2)
    tile = jnp.minimum(jnp.sum((upto_e <= off[:, None]).astype(I32), axis=1), ntiles - 1)
    is_t = tile[:, None] == jnp.arange(ntiles, dtype=I32)[None, :]
    run_slot0 = jnp.cumsum(runs, axis=1) - runs - (upto - runs)
    slot0 = jnp.sum(jnp.where(is_t[:, :, None] & is_e[:, None, :], run_slot0[None, :, :], 0), axis=(1, 2))
    live_row = tile * nslots + (slot0 + off) * SEG_ROWS
    spare_row = ntiles * nslots + (((q // cpb) % 2) * cpb + q % cpb) * SEG_ROWS
    return jnp.where(in_run, live_row, spare_row)


def kernel(x, c, positions, w_ada, b_ada, g_mix, w_in, sink_logits, w_branch_a, w_branch_b, w_out, g_ffn,
           w_group, b_group, w_route, b_route, w_expert_gate, w_expert_up, w_expert_down, g_final):
    batch, seq, dm = x.shape
    n = batch * seq
    assert w_ada.shape[0] == 1, "one layer"
    x2 = x.reshape(n, dm)

    c8 = jnp.pad(c, ((0, 8 - batch), (0, 0)))
    mod, w_in_bf = _ada(c8, w_ada[0], b_ada[0].reshape(1, -1), w_in[0])
    mod3 = mod[:batch].reshape(batch, 6, dm)

    angles, sign = _rope_angles(positions)
    outs = _inproj(x2, mod3, g_mix[0].reshape(1, dm), angles, sign, w_in_bf,
                   (w_expert_gate[0], w_expert_up[0], w_expert_down[0]), batch, seq)
    qkv = outs[:9]
    qb, kb, vb, sga, sgb = outs[9:14]
    expert_w_bf = outs[14:]

    dil_outs = [_dil_attention(qkv[3 * g], qkv[3 * g + 1], qkv[3 * g + 2]) for g in range(len(DILATIONS))]
    ob = _swa_attention(sink_logits[0], qb.reshape(batch, seq, -1), kb.reshape(batch, seq, -1),
                        vb.reshape(batch, seq, -1)).reshape(n, -1)

    pad = LANES - N_GROUPS - N_EXPERTS
    wr = jnp.concatenate([w_group[0], w_route[0], jnp.zeros((dm, pad), F32)], axis=1).astype(BF16)
    br = jnp.concatenate([b_group[0], b_route[0], jnp.zeros((pad,), F32)]).reshape(1, LANES)
    ntiles = n // MOE_TILE
    nslots = 2 * MOE_TILE + N_EXPERTS * SEG_ROWS
    x1, x_sorted, route, seg_lens = _outproj(dil_outs, ob, sga, sgb, x2, mod3, g_ffn[0].reshape(1, dm),
                                             w_branch_a[0].astype(BF16), w_branch_b[0].astype(BF16),
                                             w_out[0].astype(BF16), wr, br, batch, seq, nslots)

    nblk = -(-(2 * n + ntiles * N_EXPERTS * (SEG_ROWS - 1)) // ROW_BLOCK) + N_EXPERTS
    lens = seg_lens[:, 0, :N_EXPERTS].astype(I32)
    tot = jnp.sum(lens, axis=0)
    padded = (tot + ROW_BLOCK - 1) // ROW_BLOCK * ROW_BLOCK
    pend = jnp.cumsum(padded)
    pstart = pend - padded
    chunk_tbl = _expert_chunk_table(lens, pstart, tot, pend, nblk, nslots)
    live_tbl = jnp.sum(lens, axis=1)

    nchk = nslots // SEG_ROWS
    base = pstart[None, :] + jnp.cumsum(lens, axis=0) - lens
    run_end = jnp.cumsum(lens // SEG_ROWS, axis=1)
    chunk = jnp.arange(nchk, dtype=I32)
    owner = jnp.sum((run_end[:, None, :] <= chunk[None, :, None]).astype(I32), axis=2)
    is_owner = owner[:, :, None] == jnp.arange(N_EXPERTS, dtype=I32)[None, None, :]
    run_row0 = base - (run_end - lens // SEG_ROWS) * SEG_ROWS
    row = jnp.sum(jnp.where(is_owner, run_row0[:, None, :], 0), axis=2) + chunk[None, :] * SEG_ROWS
    src_tbl = jnp.where(owner < N_EXPERTS, row, chunk[None, :] * SEG_ROWS).reshape(-1)

    y_rows = _experts(pstart // ROW_BLOCK, padded // ROW_BLOCK, chunk_tbl, x_sorted, *expert_w_bf, nblk)
    out = _combine(src_tbl, live_tbl, route, x1, mod3, g_final.reshape(1, dm), y_rows, seq, MOE_TILE, nslots)
    return out.reshape(batch, seq, dm)
```

```python
import functools

import jax
import jax.numpy as jnp
from jax import lax
from jax.experimental import pallas as pl
from jax.experimental.pallas import tpu as pltpu

F32 = jnp.float32
BF16 = jnp.bfloat16
I32 = jnp.int32

HEAD_DIM = 64
ROPE_THETA = 10000.0
RMS_EPS = 1e-6
NEG_INF = -1e30
Q_SCALE = HEAD_DIM ** -0.5
DILATIONS = (1, 4, 16)
DIL_HALF_WINDOW = 64
DIL_GROUP_W = 256
SWA_WINDOW = 128
N_GROUPS = 4
EXPERTS_PER_GROUP = 8
N_EXPERTS = 32
LANES = 128
ROW_BLOCK = 512
SEG_ROWS = 16
MOE_TILE = 512
SLOT_GROUP = 256
VMEM_LIMIT = 56 * 1024 * 1024


def _cparams(sem):
    return pltpu.CompilerParams(dimension_semantics=sem, vmem_limit_bytes=VMEM_LIMIT)


def _ada_kernel(c_ref, w_ref, b_ref, win_ref, o_ref, winb_ref):
    c = c_ref[...]
    cs = c * jax.nn.sigmoid(c)
    o_ref[...] = jnp.dot(cs.astype(BF16), w_ref[...].astype(BF16), preferred_element_type=F32) + b_ref[...]
    winb_ref[...] = win_ref[...].astype(BF16)


def _ada(c8, w_ada, b_ada, w_in):
    d, n = w_ada.shape
    steps = 4
    tn, tw = n // steps, w_in.shape[1] // steps
    return pl.pallas_call(
        _ada_kernel,
        grid=(steps,),
        in_specs=[pl.BlockSpec((8, d), lambda j: (0, 0)),
                  pl.BlockSpec((d, tn), lambda j: (0, j)),
                  pl.BlockSpec((1, tn), lambda j: (0, j)),
                  pl.BlockSpec((w_in.shape[0], tw), lambda j: (0, j))],
        out_specs=[pl.BlockSpec((8, tn), lambda j: (0, j)),
                   pl.BlockSpec((w_in.shape[0], tw), lambda j: (0, j))],
        out_shape=[jax.ShapeDtypeStruct((8, n), F32), jax.ShapeDtypeStruct(w_in.shape, BF16)],
        compiler_params=_cparams(("arbitrary",)),
        name="ada",
    )(c8, w_ada, b_ada, w_in)


def _rms_mod(x, g, shift, scale):
    ms = jnp.mean(x * x, axis=-1, keepdims=True)
    return (x * lax.rsqrt(ms + RMS_EPS)) * (g * (1.0 + scale)) + shift


def _inproj_kernel(x_ref, mod_ref, g_ref, ang_ref, sign_ref, w_ref, eg_ref, eu_ref, ed_ref,
                   q0_ref, k0_ref, v0_ref, q1_ref, k1_ref, v1_ref, q2_ref, k2_ref, v2_ref,
                   qb_ref, kb_ref, vb_ref, sga_ref, sgb_ref, egb_ref, eub_ref, edb_ref, stg_ref, *, tm, n_exp):
    def cast_expert():
        egb_ref[...] = eg_ref[...].astype(BF16)
        eub_ref[...] = eu_ref[...].astype(BF16)
        edb_ref[...] = ed_ref[...].astype(BF16)

    if n_exp is None:
        cast_expert()
    else:
        pl.when(pl.program_id(0) < n_exp)(cast_expert)

    h = _rms_mod(x_ref[...], g_ref[...], mod_ref[0, 0:1, :], mod_ref[0, 1:2, :])
    hb = h.astype(BF16)
    ang = ang_ref[...]
    cos = jnp.cos(ang)
    sin = jnp.sin(ang) * sign_ref[...]
    lane = lax.broadcasted_iota(I32, (tm, LANES), 1)
    first_half = (lane & 32) == 0
    low = lane < 64

    def proj(c0, width):
        return jnp.dot(hb, w_ref[:, c0:c0 + width], preferred_element_type=F32)

    def rope(t):
        rot = jnp.where(first_half, pltpu.roll(t, 96, 1), pltpu.roll(t, 32, 1))
        return t * cos + rot * sin

    def rope256(p):
        return jnp.concatenate([rope(p[:, :LANES]), rope(p[:, LANES:])], axis=1)

    def store_group(ref, val, d):
        if d == 1:
            ref[0, 0] = val.astype(BF16)
        else:
            for c in range(2):
                stg_ref[c] = val[:, c * LANES:(c + 1) * LANES]
            for r in range(d):
                for c in range(2):
                    ref[0, r, :, c * LANES:(c + 1) * LANES] = (
                        stg_ref[c, pl.ds(r, tm // d, stride=d), :].astype(BF16))

    q_refs = (q0_ref, q1_ref, q2_ref)
    k_refs = (k0_ref, k1_ref, k2_ref)
    v_refs = (v0_ref, v1_ref, v2_ref)
    for g, d in enumerate(DILATIONS):
        store_group(q_refs[g], rope256(proj(g * 256, 256)) * Q_SCALE, d)
        store_group(k_refs[g], rope256(proj(768 + g * 256, 256)), d)
        store_group(v_refs[g], proj(1536 + g * 256, 256), d)
    for j in range(2):
        qb_ref[:, j * 256:(j + 1) * 256] = (rope256(proj(2304 + j * 256, 256)) * Q_SCALE).astype(BF16)
    kv = proj(2816, 256)
    kb = rope(kv[:, :LANES])
    vb = kv[:, LANES:]
    kb_sw = pltpu.roll(kb, 64, 1)
    vb_sw = pltpu.roll(vb, 64, 1)
    kb_ref[:, :LANES] = jnp.where(low, kb, kb_sw).astype(BF16)
    kb_ref[:, LANES:] = jnp.where(low, kb_sw, kb).astype(BF16)
    vb_ref[:, :LANES] = jnp.where(low, vb, vb_sw).astype(BF16)
    vb_ref[:, LANES:] = jnp.where(low, vb_sw, vb).astype(BF16)
    for j in range(4):
        sig = lambda t: 0.5 * jnp.tanh(0.5 * t) + 0.5
        sga_ref[:, j * 256:(j + 1) * 256] = sig(proj(3072 + j * 256, 256)).astype(BF16)
        sgb_ref[:, j * 256:(j + 1) * 256] = sig(proj(4096 + j * 256, 256)).astype(BF16)


def _inproj(x2, mod3, g_mix, angles, sign, w_in_bf, expert_w, batch, seq):
    n, dm = x2.shape
    tm = 512
    tpb = seq // tm
    grid = (n // tm,)
    row = lambda i: (i, 0)
    strided_specs, strided_shapes = [], []
    for d in DILATIONS:
        for _ in range(3):
            strided_specs.append(pl.BlockSpec((1, d, tm // d, DIL_GROUP_W), lambda i: (i // tpb, 0, i % tpb, 0)))
            strided_shapes.append(jax.ShapeDtypeStruct((batch, d, seq // d, DIL_GROUP_W), BF16))
    out_specs = strided_specs + [
        pl.BlockSpec((tm, 512), row), pl.BlockSpec((tm, 256), row), pl.BlockSpec((tm, 256), row),
        pl.BlockSpec((tm, dm), row), pl.BlockSpec((tm, dm), row)]
    out_shapes = strided_shapes + [
        jax.ShapeDtypeStruct((n, 512), BF16), jax.ShapeDtypeStruct((n, 256), BF16),
        jax.ShapeDtypeStruct((n, 256), BF16), jax.ShapeDtypeStruct((n, dm), BF16),
        jax.ShapeDtypeStruct((n, dm), BF16)]
    n_exp = expert_w[0].shape[0]
    assert n // tm >= n_exp, "one expert's weights are cast per grid step"
    emap = lambda i: (jnp.minimum(i, n_exp - 1), 0, 0)
    expert_specs = [pl.BlockSpec((1,) + w.shape[1:], emap) for w in expert_w]
    return pl.pallas_call(
        functools.partial(_inproj_kernel, tm=tm, n_exp=None if n // tm == n_exp else n_exp),
        grid=grid,
        in_specs=[pl.BlockSpec((tm, dm), row),
                  pl.BlockSpec((1, 6, dm), lambda i: (i // tpb, 0, 0)),
                  pl.BlockSpec((1, dm), lambda i: (0, 0)),
                  pl.BlockSpec((tm, LANES), row),
                  pl.BlockSpec((1, LANES), lambda i: (0, 0)),
                  pl.BlockSpec(w_in_bf.shape, lambda i: (0, 0), pipeline_mode=pl.Buffered(1))] + expert_specs,
        out_specs=out_specs + expert_specs,
        out_shape=out_shapes + [jax.ShapeDtypeStruct(w.shape, BF16) for w in expert_w],
        scratch_shapes=[pltpu.VMEM((2, tm, LANES), F32)],
        compiler_params=_cparams(("arbitrary",)),
        name="inproj",
    )(x2, mod3, g_mix, angles, sign, w_in_bf, *expert_w)


def _split_heads(q2, low):
    zero = jnp.zeros_like(q2)
    return jnp.concatenate([jnp.where(low, q2, zero), jnp.where(low, zero, q2)], axis=0)


def _band_softmax(qst, k2, v2, bias, sinks):
    s = lax.dot_general(qst, k2, (((1,), (1,)), ((), ())), preferred_element_type=F32)
    s = s + bias
    rows, tk = s.shape
    m = jnp.max(s, axis=-1, keepdims=True)
    if sinks is not None:
        seg = rows // len(sinks)
        m = jnp.concatenate([jnp.maximum(m[h * seg:(h + 1) * seg], sk) for h, sk in enumerate(sinks)], axis=0)
    m = jnp.broadcast_to(m, (rows, LANES))
    e = jnp.concatenate([jnp.exp(s[:, c * LANES:(c + 1) * LANES] - m) for c in range(tk // LANES)], axis=1)
    v_ones = jnp.concatenate([v2, jnp.ones((tk, LANES), BF16)], axis=1)
    od = jnp.dot(e.astype(BF16), v_ones, preferred_element_type=F32)
    o, den = od[:, :LANES], od[:, LANES:]
    if sinks is not None:
        den = jnp.concatenate([den[h * seg:(h + 1) * seg] + jnp.exp(sk - m[h * seg:(h + 1) * seg])
                               for h, sk in enumerate(sinks)], axis=0)
    return o / den, m, den


def _fill_band_bias(bias_ref, tq, window):
    rows, tk = bias_ref.shape[1:]
    row = lax.broadcasted_iota(I32, (rows, tk), 0) & (tq - 1)
    col = lax.broadcasted_iota(I32, (rows, tk), 1)
    for i in range(bias_ref.shape[0]):
        bias_ref[i] = jnp.where(jnp.abs(col - row - i * window) <= window, 0.0, NEG_INF)


def _dil_kernel(q_ref, k_ref, v_ref, o_ref, l_ref, bias_ref, *, length, tq, tk):
    low = lax.broadcasted_iota(I32, (tq, LANES), 1) < 64
    _fill_band_bias(bias_ref, tq, DIL_HALF_WINDOW)
    nq = length // tq

    def body(j, carry):
        r = j // nq
        qs = pl.multiple_of((j % nq) * tq, tq)
        ks = pl.multiple_of(jnp.clip(qs - DIL_HALF_WINDOW, 0, length - tk), DIL_HALF_WINDOW)
        which = (qs - ks) // DIL_HALF_WINDOW
        for c in range(DIL_GROUP_W // LANES):
            cs = slice(c * LANES, (c + 1) * LANES)
            qst = _split_heads(q_ref[r, pl.ds(qs, tq), cs], low)
            o, m, den = _band_softmax(qst, k_ref[r, pl.ds(ks, tk), cs], v_ref[r, pl.ds(ks, tk), cs],
                                      bias_ref[which], None)
            lse = m + jnp.log(den)
            o_ref[r, pl.ds(qs, tq), cs] = jnp.where(low, o[:tq], o[tq:]).astype(BF16)
            l_ref[r, pl.ds(qs, tq), cs] = jnp.where(low, lse[:tq], lse[tq:])
        return carry

    lax.fori_loop(0, q_ref.shape[0] * nq, body, 0, unroll=32)


def _dil_attention(q, k, v):
    batch, d, length, w = q.shape
    tq, tk = 128, 256
    spec = pl.BlockSpec((None, d, length, w), lambda b: (b, 0, 0, 0))
    return pl.pallas_call(
        functools.partial(_dil_kernel, length=length, tq=tq, tk=tk),
        grid=(batch,),
        in_specs=[spec, spec, spec],
        out_specs=[spec, spec],
        out_shape=[jax.ShapeDtypeStruct(q.shape, BF16), jax.ShapeDtypeStruct(q.shape, F32)],
        scratch_shapes=[pltpu.VMEM((3, 2 * tq, tk), F32)],
        compiler_params=_cparams(("arbitrary",)),
        name=f"dil{d}",
    )(q, k, v)


def _swa_kernel(sink_ref, q_ref, k_ref, v_ref, o_ref, bias_ref, *, length, tq, tk):
    low = lax.broadcasted_iota(I32, (tq, LANES), 1) < 64
    nblk = q_ref.shape[1] // LANES
    _fill_band_bias(bias_ref, tq, SWA_WINDOW)

    def body(j, carry):
        qs = pl.multiple_of(j * tq, tq)
        ks = pl.multiple_of(jnp.clip(qs - SWA_WINDOW, 0, length - tk), SWA_WINDOW)
        bias = bias_ref[(qs - ks) // SWA_WINDOW]
        for b in range(nblk):
            cs = slice((b // 2) * LANES, (b // 2 + 1) * LANES)
            bs = slice(b * LANES, (b + 1) * LANES)
            qst = _split_heads(q_ref[pl.ds(qs, tq), bs], low)
            sinks = (sink_ref[2 * b], sink_ref[2 * b + 1])
            o, _, _ = _band_softmax(qst, k_ref[pl.ds(ks, tk), cs], v_ref[pl.ds(ks, tk), cs], bias, sinks)
            o_ref[pl.ds(qs, tq), bs] = jnp.where(low, o[:tq], o[tq:]).astype(BF16)
        return carry

    lax.fori_loop(0, length // tq, body, 0, unroll=8)


def _swa_attention(sink, q, k, v):
    batch, length, qw = q.shape
    tq, tk = 128, 384
    return pl.pallas_call(
        functools.partial(_swa_kernel, length=length, tq=tq, tk=tk),
        grid=(batch,),
        in_specs=[pl.BlockSpec(memory_space=pltpu.SMEM),
                  pl.BlockSpec((None, length, qw), lambda b: (b, 0, 0)),
                  pl.BlockSpec((None, length, k.shape[2]), lambda b: (b, 0, 0)),
                  pl.BlockSpec((None, length, v.shape[2]), lambda b: (b, 0, 0))],
        out_specs=pl.BlockSpec((None, length, qw), lambda b: (b, 0, 0)),
        out_shape=jax.ShapeDtypeStruct(q.shape, BF16),
        scratch_shapes=[pltpu.VMEM((3, 2 * tq, tk), F32)],
        compiler_params=_cparams(("arbitrary",)),
        name="swa",
    )(sink, q, k, v)


def _route_rows(logits):
    lane = lax.broadcasted_iota(I32, logits.shape, 1).astype(F32)
    big = 1e9
    is_g = lane < N_GROUPS
    gl = jnp.where(is_g, logits, NEG_INF)
    gmax = jnp.max(gl, axis=-1, keepdims=True)
    gsel = jnp.min(jnp.where(is_g & (gl == gmax), lane, big), axis=-1, keepdims=True)
    gw = 1.0 / jnp.sum(jnp.where(is_g, jnp.exp(gl - gmax), 0.0), axis=-1, keepdims=True)
    e_lo = N_GROUPS + gsel * EXPERTS_PER_GROUP
    in_grp = (lane >= e_lo) & (lane < e_lo + EXPERTS_PER_GROUP)
    el = jnp.where(in_grp, logits, NEG_INF)
    m1 = jnp.max(el, axis=-1, keepdims=True)
    i1 = jnp.min(jnp.where(in_grp & (el == m1), lane, big), axis=-1, keepdims=True)
    el2 = jnp.where(lane == i1, NEG_INF, el)
    m2 = jnp.max(el2, axis=-1, keepdims=True)
    i2 = jnp.min(jnp.where(in_grp & (lane != i1) & (el2 == m2), lane, big), axis=-1, keepdims=True)
    t = jnp.exp(m2 - m1)
    tw1 = gw / (1.0 + t)
    tw2 = gw * t / (1.0 + t)
    out = jnp.where(lane == 0, tw1, 0.0)
    out = jnp.where(lane == 1, tw2, out)
    out = jnp.where(lane == 2, i1 - N_GROUPS, out)
    return jnp.where(lane == 3, i2 - N_GROUPS, out)


def _outproj_kernel(o0_ref, l0_ref, o1_ref, l1_ref, o2_ref, l2_ref, ob_ref, sga_ref, sgb_ref, x_ref,
                    mod_ref, g_ref, wa_ref, wb_ref, wo_ref, wr_ref, br_ref,
                    x1_ref, xs_ref, route_ref, len_ref,
                    so1_ref, sl1_ref, so2_ref, sl2_ref, h2_ref, *, tm, sub, group):
    ntiles = pl.num_programs(0) - 1

    @pl.when(pl.program_id(0) == ntiles)
    def _():
        xs_ref[...] = jnp.zeros_like(xs_ref)

    pl.when(pl.program_id(0) < ntiles)(functools.partial(
        _outproj_tile, o0_ref, l0_ref, o1_ref, l1_ref, o2_ref, l2_ref, ob_ref, sga_ref, sgb_ref, x_ref,
        mod_ref, g_ref, wa_ref, wb_ref, wo_ref, wr_ref, br_ref, x1_ref, xs_ref, route_ref, len_ref,
        so1_ref, sl1_ref, so2_ref, sl2_ref, h2_ref, tm=tm, sub=sub, group=group))


def _outproj_tile(o0_ref, l0_ref, o1_ref, l1_ref, o2_ref, l2_ref, ob_ref, sga_ref, sgb_ref, x_ref,
                  mod_ref, g_ref, wa_ref, wb_ref, wo_ref, wr_ref, br_ref,
                  x1_ref, xs_ref, route_ref, len_ref,
                  so1_ref, sl1_ref, so2_ref, sl2_ref, h2_ref, *, tm, sub, group):
    dm = x_ref.shape[1]
    for (o_ref, l_ref, so_ref, sl_ref, d) in ((o1_ref, l1_ref, so1_ref, sl1_ref, DILATIONS[1]),
                                              (o2_ref, l2_ref, so2_ref, sl2_ref, DILATIONS[2])):
        for r in range(d):
            for c in range(2):
                cs = slice(c * LANES, (c + 1) * LANES)
                so_ref[c, pl.ds(r, tm // d, stride=d), :] = o_ref[0, r, :, cs].astype(F32)
                sl_ref[c, pl.ds(r, tm // d, stride=d), :] = l_ref[0, r, :, cs]
    mr = lax.broadcasted_iota(I32, (LANES, LANES), 0)
    mc = lax.broadcasted_iota(I32, (LANES, LANES), 1)
    move_hi = jnp.where(((mr < 2) & (mc == 2 * mr)) | ((mr >= 2) & (mr < 4) & (mc == mr + 2)), 1.0, 0.0).astype(BF16)
    move_lo = jnp.where((mr < 2) & (mc == 2 * mr + 1), 1.0, 0.0).astype(BF16)
    for t in range(tm // sub):
        rs = slice(t * sub, (t + 1) * sub)
        both = lambda ref: jnp.concatenate([ref[0, rs, :], ref[1, rs, :]], axis=1)
        o0, l0 = o0_ref[0, 0, rs, :].astype(F32), l0_ref[0, 0, rs, :]
        o1, l1, o2, l2 = both(so1_ref), both(sl1_ref), both(so2_ref), both(sl2_ref)
        mx = jnp.maximum(jnp.maximum(l0, l1), l2)
        w0, w1, w2 = jnp.exp(l0 - mx), jnp.exp(l1 - mx), jnp.exp(l2 - mx)
        o_a = (w0 * o0 + w1 * o1 + w2 * o2) / (w0 + w1 + w2)
        y_a = jnp.dot(o_a.astype(BF16), wa_ref[...], preferred_element_type=F32)
        y_b = jnp.dot(ob_ref[rs, :], wb_ref[...], preferred_element_type=F32)
        merged = sga_ref[rs, :].astype(F32) * y_a + sgb_ref[rs, :].astype(F32) * y_b
        mix = jnp.dot(merged.astype(BF16), wo_ref[...], preferred_element_type=F32)
        x1 = x_ref[rs, :] + mod_ref[0, 2:3, :] * mix
        x1_ref[rs, :] = x1
        h2 = _rms_mod(x1, g_ref[...], mod_ref[0, 3:4, :], mod_ref[0, 4:5, :]).astype(BF16)
        h2_ref[rs, :dm] = h2
        logits = jnp.dot(h2, wr_ref[...], preferred_element_type=F32) + br_ref[...]
        rt = _route_rows(logits)
        route_ref[rs, :] = rt
        hi = rt.astype(BF16)
        lo = (rt - hi.astype(F32)).astype(BF16)
        aux = (jnp.dot(hi, move_hi, preferred_element_type=F32) + jnp.dot(lo, move_lo, preferred_element_type=F32))
        h2_ref[rs, dm:] = aux.astype(BF16)

    part = route_ref[...]
    e1, e2 = part[:, 2:3], part[:, 3:4]
    lane = lax.broadcasted_iota(I32, (tm, LANES), 1).astype(F32)
    onehot = jnp.where((lane == e1) | (lane == e2), 1.0, 0.0)
    rr = lax.broadcasted_iota(I32, (tm, tm), 0)
    cc = lax.broadcasted_iota(I32, (tm, tm), 1)
    tri = jnp.where(rr > cc, 1.0, 0.0).astype(BF16)
    prefix = jnp.dot(tri, onehot.astype(BF16), preferred_element_type=F32)
    cnt = jnp.sum(onehot, axis=0, keepdims=True)
    seg_len = jnp.ceil(cnt * (1.0 / SEG_ROWS)) * SEG_ROWS
    ur = lax.broadcasted_iota(I32, (LANES, LANES), 0)
    uc = lax.broadcasted_iota(I32, (LANES, LANES), 1)
    upper = jnp.where(ur < uc, 1.0, 0.0).astype(BF16)
    seg_off = jnp.dot(jnp.broadcast_to(seg_len, (8, LANES)).astype(BF16), upper, preferred_element_type=F32)[0:1, :]
    slot_map = seg_off + prefix
    s1 = jnp.sum(jnp.where(lane == e1, slot_map, 0.0), axis=-1, keepdims=True)
    s2 = jnp.sum(jnp.where(lane == e2, slot_map, 0.0), axis=-1, keepdims=True)
    len_ref[0] = jnp.broadcast_to(seg_len, (8, LANES))
    route = jnp.where(lane == 2, s1, jnp.where(lane == 3, s2, part))
    route_ref[...] = route

    pr = lax.broadcasted_iota(I32, (8, LANES), 0)
    pc = lax.broadcasted_iota(I32, (8, LANES), 1)
    lane_pick = jnp.where(pc == pr + 2, 1.0, 0.0)
    slots_t = lax.dot_general(lane_pick, route, (((1,), (1,)), ((), ())), preferred_element_type=F32,
                              precision=lax.Precision.HIGHEST)
    live = jnp.sum(seg_len).astype(I32)

    def sort_rows(lo, rows):
        srow = (lax.broadcasted_iota(I32, (rows, tm), 0) + lo).astype(F32)
        pick = jnp.where((srow == slots_t[0:1, :]) | (srow == slots_t[1:2, :]), 1.0, 0.0).astype(BF16)
        xs_ref[lo:lo + rows, :] = jnp.dot(pick, h2_ref[...], preferred_element_type=F32).astype(BF16)

    def clear_rows(lo, rows):
        xs_ref[lo:lo + rows, :] = jnp.zeros((rows, xs_ref.shape[1]), BF16)

    head = xs_ref.shape[0] - group
    sort_rows(0, head)
    pl.when(live > head)(functools.partial(sort_rows, head, group))
    pl.when(live <= head)(functools.partial(clear_rows, head, group))


def _outproj(dil_outs, ob, sga, sgb, x2, mod3, g_ffn, wa, wb, wo, wr, br, batch, seq, nslots):
    n, dm = x2.shape
    tm = MOE_TILE
    tpb = seq // tm
    ntiles = n // tm
    tile = lambda i: jnp.minimum(i, ntiles - 1)
    row = lambda i: (tile(i), 0)
    const = lambda i: (0, 0)
    in_specs = []
    args = []
    for (o, l), d in zip(dil_outs, DILATIONS):
        spec = pl.BlockSpec((1, d, tm // d, DIL_GROUP_W), lambda i: (tile(i) // tpb, 0, tile(i) % tpb, 0))
        in_specs += [spec, spec]
        args += [o, l]
    in_specs += [pl.BlockSpec((tm, ob.shape[1]), row), pl.BlockSpec((tm, dm), row), pl.BlockSpec((tm, dm), row),
                 pl.BlockSpec((tm, dm), row),
                 pl.BlockSpec((1, 6, dm), lambda i: (tile(i) // tpb, 0, 0)),
                 pl.BlockSpec((1, dm), const),
                 pl.BlockSpec(wa.shape, const), pl.BlockSpec(wb.shape, const), pl.BlockSpec(wo.shape, const),
                 pl.BlockSpec(wr.shape, const), pl.BlockSpec(br.shape, const)]
    args += [ob, sga, sgb, x2, mod3, g_ffn, wa, wb, wo, wr, br]
    width = dm + LANES
    return pl.pallas_call(
        functools.partial(_outproj_kernel, tm=tm, sub=512, group=SLOT_GROUP),
        grid=(ntiles + 1,),
        in_specs=in_specs,
        out_specs=[pl.BlockSpec((tm, dm), row), pl.BlockSpec((nslots, width), lambda i: (i, 0)),
                   pl.BlockSpec((tm, LANES), row), pl.BlockSpec((1, 8, LANES), lambda i: (tile(i), 0, 0))],
        out_shape=[jax.ShapeDtypeStruct((n, dm), F32), jax.ShapeDtypeStruct(((ntiles + 1) * nslots, width), BF16),
                   jax.ShapeDtypeStruct((n, LANES), F32), jax.ShapeDtypeStruct((ntiles, 8, LANES), F32)],
        scratch_shapes=[pltpu.VMEM((2, tm, LANES), F32)] * 4 + [pltpu.VMEM((tm, width), BF16)],
        compiler_params=_cparams(("arbitrary",)),
        name="outproj",
    )(*args)


def _wait_copies(count, copy):
    def wait_one(c, carry):
        copy.wait()
        return carry

    lax.fori_loop(0, count, wait_one, 0)


def _experts_kernel(first_ref, count_ref, chunk_ref, wg_ref, wu_ref, wd_ref, xs_hbm, ys_hbm,
                    xbuf, ybuf, zbuf, xsem, ysem, zsem, *, nblk):
    e = pl.program_id(0)
    last = pl.num_programs(0) - 1
    first = first_ref[e]
    count = count_ref[e]
    used = first_ref[last] + count_ref[last]
    dm = ybuf.shape[2]
    cpb = ROW_BLOCK // SEG_ROWS

    def chunk_rows(blk, j):
        return pl.ds(pl.multiple_of(chunk_ref[blk * cpb + j], SEG_ROWS), SEG_ROWS)

    def gather(blk):
        for j in range(cpb):
            pltpu.make_async_copy(xs_hbm.at[chunk_rows(blk, j), :], xbuf.at[blk & 1, pl.ds(j * SEG_ROWS, SEG_ROWS), :],
                                  xsem.at[blk & 1]).start(priority=1)

    def gather_done(blk):
        return pltpu.make_async_copy(xs_hbm.at[pl.ds(0, ROW_BLOCK), :], xbuf.at[blk & 1], xsem.at[blk & 1])

    def put(blk):
        rows = pl.ds(pl.multiple_of(blk * ROW_BLOCK, ROW_BLOCK), ROW_BLOCK)
        return pltpu.make_async_copy(ybuf.at[blk & 1], ys_hbm.at[rows, :], ysem.at[blk & 1])

    def zero_blk(b):
        rows = pl.ds(pl.multiple_of(b * ROW_BLOCK, ROW_BLOCK), ROW_BLOCK)
        return pltpu.make_async_copy(zbuf, ys_hbm.at[rows, :], zsem)

    @pl.when(e == 0)
    def _():
        zbuf[...] = jnp.zeros_like(zbuf)

        def start(b, carry):
            zero_blk(b).start()
            return carry

        lax.fori_loop(used, nblk, start, 0)

        @pl.when(used > 0)
        def _():
            gather(0)

    def body(b, carry):
        blk = first + b
        slot = blk & 1
        gather_done(blk).wait()

        @pl.when(blk + 1 < used)
        def _():
            gather(blk + 1)

        @pl.when(blk >= 2)
        def _():
            put(blk - 2).wait()

        xb = xbuf[slot, :, :dm]
        aux = xbuf[slot, :, dm:].astype(F32)
        w = jnp.where(aux[:, 4:5] == e.astype(F32), aux[:, 0:1] + aux[:, 1:2], aux[:, 2:3] + aux[:, 3:4])
        g = jnp.dot(xb, wg_ref[0], preferred_element_type=F32)
        u = jnp.dot(xb, wu_ref[0], preferred_element_type=F32)
        a = (g * (0.5 * jnp.tanh(0.5 * g) + 0.5)) * u
        y = jnp.dot(a.astype(BF16), wd_ref[0], preferred_element_type=F32)
        ybuf[slot] = (y * w).astype(BF16)
        put(blk).start()
        return carry

    lax.fori_loop(0, count, body, 0)

    @pl.when(e == last)
    def _():
        @pl.when(used >= 2)
        def _():
            put(used - 2).wait()

        @pl.when(used >= 1)
        def _():
            put(used - 1).wait()

        _wait_copies(nblk - used, zero_blk(0))


def _experts(first_blk, count_blk, chunk_tbl, x_sorted, w_gate, w_up, w_down, nblk):
    rows, width = x_sorted.shape
    n_exp, dm, de = w_gate.shape
    wmap = lambda e, *_: (e, 0, 0)
    grid_spec = pltpu.PrefetchScalarGridSpec(
        num_scalar_prefetch=3,
        grid=(n_exp,),
        in_specs=[pl.BlockSpec((1, dm, de), wmap),
                  pl.BlockSpec((1, dm, de), wmap),
                  pl.BlockSpec((1, de, dm), wmap),
                  pl.BlockSpec(memory_space=pl.ANY)],
        out_specs=pl.BlockSpec(memory_space=pl.ANY),
        scratch_shapes=[pltpu.VMEM((2, ROW_BLOCK, width), BF16), pltpu.VMEM((2, ROW_BLOCK, dm), BF16),
                        pltpu.VMEM((ROW_BLOCK, dm), BF16),
                        pltpu.SemaphoreType.DMA((2,)), pltpu.SemaphoreType.DMA((2,)), pltpu.SemaphoreType.DMA(())],
    )
    return pl.pallas_call(
        functools.partial(_experts_kernel, nblk=nblk),
        grid_spec=grid_spec,
        out_shape=jax.ShapeDtypeStruct((nblk * ROW_BLOCK, dm), BF16),
        compiler_params=_cparams(("arbitrary",)),
        name="experts",
    )(first_blk, count_blk, chunk_tbl, w_gate, w_up, w_down, x_sorted)


def _slot_parts(tm, nslots, group):
    return [(0, 2 * tm)] + [(lo, group) for lo in range(2 * tm, nslots, group)]


def _combine_kernel(src_ref, live_ref, route_ref, x1_ref, mod_ref, g_ref, y_ref, o_ref, ys_ref, sem,
                    *, tm, nslots, group):
    i = pl.program_id(0)
    last = pl.num_programs(0) - 1
    slot = i & 1
    nchk = nslots // SEG_ROWS
    parts = _slot_parts(tm, nslots, group)

    def for_live_parts(tile, fn):
        for p, (lo, rows) in enumerate(parts):
            if p == 0:
                fn(lo, rows)
            else:
                pl.when(live_ref[tile] > lo)(functools.partial(fn, lo, rows))

    def gather(tile, s):
        def start(lo, rows):
            for c in range(lo // SEG_ROWS, (lo + rows) // SEG_ROWS):
                src = pl.multiple_of(src_ref[tile * nchk + c], SEG_ROWS)
                pltpu.make_async_copy(y_ref.at[pl.ds(src, SEG_ROWS), :],
                                      ys_ref.at[s, pl.ds(c * SEG_ROWS, SEG_ROWS), :], sem.at[s]).start(priority=c % 2)

        for_live_parts(tile, start)

    def wait(tile, s):
        for_live_parts(tile, lambda lo, rows: pltpu.make_async_copy(
            y_ref.at[pl.ds(0, rows), :], ys_ref.at[s, pl.ds(lo, rows), :], sem.at[s]).wait())

    @pl.when(i == 0)
    def _():
        gather(0, 0)

    nxt = jnp.minimum(i + 1, last)
    gather(nxt, 1 - slot)
    wait(i, slot)
    route = route_ref[...]

    def finish(rows):
        scol = lax.broadcasted_iota(I32, (tm, rows), 1).astype(F32)
        pick = jnp.where((scol == route[:, 2:3]) | (scol == route[:, 3:4]), 1.0, 0.0).astype(BF16)
        moe = jnp.dot(pick, ys_ref[slot, :rows, :], preferred_element_type=F32)
        x = x1_ref[...] + mod_ref[0, 5:6, :] * moe
        ms = jnp.mean(x * x, axis=-1, keepdims=True)
        o_ref[...] = (x * lax.rsqrt(ms + RMS_EPS)) * g_ref[...]

    ends = [lo + rows for lo, rows in parts]
    for p, end in enumerate(ends):
        above = live_ref[i] > (ends[p - 1] if p else -1)
        cond = above if p == len(ends) - 1 else above & (live_ref[i] <= end)
        pl.when(cond)(functools.partial(finish, end))

    @pl.when(i == last)
    def _():
        wait(nxt, 1 - slot)


def _combine(src_tbl, live_tbl, route, x1, mod3, g_final, y_rows, seq, tm, nslots):
    n, dm = x1.shape
    tpb = seq // tm
    grid_spec = pltpu.PrefetchScalarGridSpec(
        num_scalar_prefetch=2,
        grid=(n // tm,),
        in_specs=[pl.BlockSpec((tm, LANES), lambda i, *_: (i, 0)),
                  pl.BlockSpec((tm, dm), lambda i, *_: (i, 0)),
                  pl.BlockSpec((1, 6, dm), lambda i, *_: (i // tpb, 0, 0)),
                  pl.BlockSpec((1, dm), lambda i, *_: (0, 0)),
                  pl.BlockSpec(memory_space=pl.ANY)],
        out_specs=pl.BlockSpec((tm, dm), lambda i, *_: (i, 0)),
        scratch_shapes=[pltpu.VMEM((2, nslots, dm), BF16), pltpu.SemaphoreType.DMA((2,))],
    )
    return pl.pallas_call(
        functools.partial(_combine_kernel, tm=tm, nslots=nslots, group=SLOT_GROUP),
        grid_spec=grid_spec,
        out_shape=jax.ShapeDtypeStruct((n, dm), F32),
        compiler_params=_cparams(("arbitrary",)),
        name="combine",
    )(src_tbl, live_tbl, route, x1, mod3, g_final, y_rows)


def _rope_angles(positions):
    half = HEAD_DIM // 2
    inv_freq = ROPE_THETA ** (-jnp.arange(half, dtype=F32) * (2.0 / HEAD_DIM))
    freq = jnp.tile(inv_freq, LANES // half)
    sign = jnp.tile(jnp.concatenate([-jnp.ones((half,), F32), jnp.ones((half,), F32)]), LANES // HEAD_DIM)
    ang = positions.astype(F32).reshape(-1, 1) * freq
    return ang, sign.reshape(1, LANES)


def _expert_chunk_table(lens, pstart, tot, pend, nblk, nslots):
    ntiles = lens.shape[0]
    cpb = ROW_BLOCK // SEG_ROWS
    runs = lens // SEG_ROWS
    q = jnp.arange(nblk * cpb, dtype=I32)
    owner = jnp.minimum(jnp.sum((pend[None, :] // SEG_ROWS <= q[:, None]).astype(I32), axis=1), N_EXPERTS - 1)
    is_e = owner[:, None] == jnp.arange(N_EXPERTS, dtype=I32)[None, :]
    pick_e = lambda v: jnp.sum(jnp.where(is_e, v[None, :], 0), axis=1)
    off = q - pick_e(pstart // SEG_ROWS)
    in_run = (off < pick_e(tot // SEG_ROWS)) & (q < pend[-1] // SEG_ROWS)
    upto = jnp.cumsum(runs, axis=0)
    upto_e = jnp.sum(jnp.where(is_e[:, None, :], upto[None, :, :], 0), axis=2)
    tile = jnp.minimum(jnp.sum((upto_e <= off[:, None]).astype(I32), axis=1), ntiles - 1)
    is_t = tile[:, None] == jnp.arange(ntiles, dtype=I32)[None, :]
    run_slot0 = jnp.cumsum(runs, axis=1) - runs - (upto - runs)
    slot0 = jnp.sum(jnp.where(is_t[:, :, None] & is_e[:, None, :], run_slot0[None, :, :], 0), axis=(1, 2))
    live_row = tile * nslots + (slot0 + off) * SEG_ROWS
    spare_row = ntiles * nslots + (((q // cpb) % 2) * cpb + q % cpb) * SEG_ROWS
    return jnp.where(in_run, live_row, spare_row)


def kernel(x, c, positions, w_ada, b_ada, g_mix, w_in, sink_logits, w_branch_a, w_branch_b, w_out, g_ffn,
           w_group, b_group, w_route, b_route, w_expert_gate, w_expert_up, w_expert_down, g_final):
    batch, seq, dm = x.shape
    n = batch * seq
    assert w_ada.shape[0] == 1, "one layer"
    x2 = x.reshape(n, dm)

    c8 = jnp.pad(c, ((0, 8 - batch), (0, 0)))
    mod, w_in_bf = _ada(c8, w_ada[0], b_ada[0].reshape(1, -1), w_in[0])
    mod3 = mod[:batch].reshape(batch, 6, dm)

    angles, sign = _rope_angles(positions)
    outs = _inproj(x2, mod3, g_mix[0].reshape(1, dm), angles, sign, w_in_bf,
                   (w_expert_gate[0], w_expert_up[0], w_expert_down[0]), batch, seq)
    qkv = outs[:9]
    qb, kb, vb, sga, sgb = outs[9:14]
    expert_w_bf = outs[14:]

    dil_outs = [_dil_attention(qkv[3 * g], qkv[3 * g + 1], qkv[3 * g + 2]) for g in range(len(DILATIONS))]
    ob = _swa_attention(sink_logits[0], qb.reshape(batch, seq, -1), kb.reshape(batch, seq, -1),
                        vb.reshape(batch, seq, -1)).reshape(n, -1)

    pad = LANES - N_GROUPS - N_EXPERTS
    wr = jnp.concatenate([w_group[0], w_route[0], jnp.zeros((dm, pad), F32)], axis=1).astype(BF16)
    br = jnp.concatenate([b_group[0], b_route[0], jnp.zeros((pad,), F32)]).reshape(1, LANES)
    ntiles = n // MOE_TILE
    nslots = 2 * MOE_TILE + N_EXPERTS * SEG_ROWS
    x1, x_sorted, route, seg_lens = _outproj(dil_outs, ob, sga, sgb, x2, mod3, g_ffn[0].reshape(1, dm),
                                             w_branch_a[0].astype(BF16), w_branch_b[0].astype(BF16),
                                             w_out[0].astype(BF16), wr, br, batch, seq, nslots)

    nblk = -(-(2 * n + ntiles * N_EXPERTS * (SEG_ROWS - 1)) // ROW_BLOCK) + N_EXPERTS
    lens = seg_lens[:, 0, :N_EXPERTS].astype(I32)
    tot = jnp.sum(lens, axis=0)
    padded = (tot + ROW_BLOCK - 1) // ROW_BLOCK * ROW_BLOCK
    pend = jnp.cumsum(padded)
    pstart = pend - padded
    chunk_tbl = _expert_chunk_table(lens, pstart, tot, pend, nblk, nslots)
    live_tbl = jnp.sum(lens, axis=1)

    nchk = nslots // SEG_ROWS
    base = pstart[None, :] + jnp.cumsum(lens, axis=0) - lens
    run_end = jnp.cumsum(lens // SEG_ROWS, axis=1)
    chunk = jnp.arange(nchk, dtype=I32)
    owner = jnp.sum((run_end[:, None, :] <= chunk[None, :, None]).astype(I32), axis=2)
    is_owner = owner[:, :, None] == jnp.arange(N_EXPERTS, dtype=I32)[None, None, :]
    run_row0 = base - (run_end - lens // SEG_ROWS) * SEG_ROWS
    row = jnp.sum(jnp.where(is_owner, run_row0[:, None, :], 0), axis=2) + chunk[None, :] * SEG_ROWS
    src_tbl = jnp.where(owner < N_EXPERTS, row, chunk[None, :] * SEG_ROWS).reshape(-1)

    y_rows = _experts(pstart // ROW_BLOCK, padded // ROW_BLOCK, chunk_tbl, x_sorted, *expert_w_bf, nblk)
    out = _combine(src_tbl, live_tbl, route, x1, mod3, g_final.reshape(1, dm), y_rows, seq, MOE_TILE, nslots)
    return out.reshape(batch, seq, dm)
```
